```python
import jax
import jax.numpy as jnp
from jax import lax
import numpy as np

D_MODEL = 1024
BATCH = 4
SEQ = 8192
DEPTH = 4

HEAD_DIM = 64
Q_BLOCK = 128
NORM_EPS = 1e-6
TINY = 1e-30
SB_HEADS = 8
MLA_HEADS = 8
MLA_Q_RANK = 256
MLA_KV_RANK = 128
MLA_NOPE = 64
MLA_ROPE = 32
MLA_V = 64
ROPE_BASE = 10000.0
NSA_HEADS = 12
NSA_KV_HEADS = 3
NSA_HPG = NSA_HEADS // NSA_KV_HEADS
NSA_Q_BLOCK = 64
CMP_LEN = 32
CMP_STRIDE = 16
SEL_LEN = 64
SEL_TOPN = 16
WIN = 512
FORCE_BONUS = 1e3
DIL_CFG = ((128, 1), (512, 4), (2048, 16))
N_DIL = len(DIL_CFG)
DIL_HEADS = 4
SB_W = SB_HEADS * HEAD_DIM
MLA_OUT = MLA_HEADS * MLA_V
NSA_W = NSA_HEADS * HEAD_DIM
NSA_KV_W = NSA_KV_HEADS * HEAD_DIM
DIL_W = DIL_HEADS * HEAD_DIM
EVEN_SPLITS = (SB_W, SB_W, SB_W, SB_W, MLA_Q_RANK, MLA_KV_RANK, MLA_ROPE, MLA_OUT)
ODD_SPLITS = (NSA_W,) + (NSA_KV_W,) * 6 + (3 * NSA_HEADS, NSA_W) + (N_DIL * DIL_W,) * 3 + (DIL_W,)
EVEN_IN = sum(EVEN_SPLITS)
ODD_IN = sum(ODD_SPLITS)
EVEN_MIX = SB_W + MLA_OUT
ODD_MIX = NSA_W + DIL_W

kernel_name = 'hybrid_stickbreak_mla_nsa_dilated_adaln'


def _split(t, widths):
    return jnp.split(t, np.cumsum(widths)[:-1].tolist(), axis=-1)


def _rms(t, g):
    tf = t.astype(jnp.float32)
    y = tf * lax.rsqrt(jnp.mean(tf * tf, axis=-1, keepdims=True) + NORM_EPS)
    return (y * g.astype(jnp.float32)).astype(t.dtype)


def _rope(t, cos, sin):
    t1, t2 = jnp.split(t.astype(jnp.float32), 2, axis=-1)
    return jnp.concatenate([t1 * cos - t2 * sin, t1 * sin + t2 * cos], axis=-1).astype(t.dtype)


def _alibi_slopes(n):
    return 2.0 ** (-8.0 * jnp.arange(1, n + 1, dtype=jnp.float32) / n)


def _masked_softmax(s, mask):
    s = jnp.where(mask, s, -jnp.inf)
    mx = jnp.max(s, axis=-1, keepdims=True)
    mx = jnp.where(jnp.isfinite(mx), mx, 0.0)
    e = jnp.exp(s - mx)
    den = jnp.maximum(jnp.sum(e, axis=-1, keepdims=True), TINY)
    return e / den, (mx + jnp.log(den))[..., 0]


def _sweep(fn, seq, blk):
    out = lax.map(fn, jnp.arange(seq // blk))
    out = jnp.moveaxis(out, 0, 1)
    return out.reshape((out.shape[0], seq) + out.shape[3:])


def _modulate(x, c, ada_w, ada_b, norm_g):
    mod = jax.nn.silu(c) @ ada_w + ada_b
    shift, scale, gate = jnp.split(mod, 3, axis=-1)
    h = _rms(x, norm_g) * (1.0 + scale[:, None, :]) + shift[:, None, :]
    return h, gate[:, None, :]


def _stick_breaking(q, k, v):
    B, S, H, Dh = q.shape
    scale = Dh ** -0.5
    kpos = jnp.arange(S)

    def block(i):
        q0 = i * Q_BLOCK
        qb = lax.dynamic_slice_in_dim(q, q0, Q_BLOCK, axis=1)
        z = jnp.einsum('bqhd,bkhd->bhqk', qb, k, preferred_element_type=jnp.float32) * scale
        strict = kpos[None, :] < (q0 + jnp.arange(Q_BLOCK))[:, None]
        log_beta = jax.nn.log_sigmoid(z)
        log_one_minus = jnp.where(strict, log_beta - z, 0.0)
        tail = lax.cumsum(log_one_minus, axis=3, reverse=True) - log_one_minus
        w = jnp.where(strict, jnp.exp(log_beta + tail), 0.0)
        return jnp.einsum('bhqk,bkhd->bqhd', w, v)

    return _sweep(block, S, Q_BLOCK)


def _causal_softmax_attn(q, k, v, scale):
    B, S = q.shape[:2]
    kpos = jnp.arange(S)

    def block(i):
        q0 = i * Q_BLOCK
        qb = lax.dynamic_slice_in_dim(q, q0, Q_BLOCK, axis=1)
        s = jnp.einsum('bqhd,bkhd->bhqk', qb, k, preferred_element_type=jnp.float32) * scale
        p, _ = _masked_softmax(s, kpos[None, :] <= (q0 + jnp.arange(Q_BLOCK))[:, None])
        return jnp.einsum('bhqk,bkhd->bqhd', p, v)

    return _sweep(block, S, Q_BLOCK)


def _mla(q_lat, kv_lat, k_rope, qa_g, wq_up, kva_g, wkv_up, qn, kn, cos, sin):
    B, S, _ = q_lat.shape
    q = (_rms(q_lat, qa_g) @ wq_up).reshape(B, S, MLA_HEADS, MLA_NOPE + MLA_ROPE)
    kv = (_rms(kv_lat, kva_g) @ wkv_up).reshape(B, S, MLA_HEADS, MLA_NOPE + MLA_V)
    q_nope = _rms(q[..., :MLA_NOPE], qn[:MLA_NOPE])
    q_rot = _rope(_rms(q[..., MLA_NOPE:], qn[MLA_NOPE:]), cos[:, None, :], sin[:, None, :])
    k_nope = _rms(kv[..., :MLA_NOPE], kn[:MLA_NOPE])
    k_rot = _rope(_rms(k_rope, kn[MLA_NOPE:]), cos, sin)
    v = kv[..., MLA_NOPE:]
    qf = jnp.concatenate([q_nope, q_rot], axis=-1)
    kf = jnp.concatenate([k_nope, jnp.broadcast_to(k_rot[:, :, None, :], (B, S, MLA_HEADS, MLA_ROPE))], axis=-1)
    return _causal_softmax_attn(qf, kf, v, (MLA_NOPE + MLA_ROPE) ** -0.5)


def _compress(t, cidx, pe, w):
    blocks = t[:, cidx] + pe[:, None, :]
    return jnp.einsum('bnlgd,lde->bnge', blocks, w.reshape(CMP_LEN, HEAD_DIM, HEAD_DIM))


def _nsa(q, kc, vc, ks, vs, kw, vw, gates, cpos, cend, pos_f, slopes):
    B, S, G, HPG, Dh = q.shape
    scale = Dh ** -0.5
    n_sel = S // SEL_LEN
    topn = min(SEL_TOPN, n_sel)
    cstart = cend - (CMP_LEN - 1)
    jstart = jnp.arange(n_sel) * SEL_LEN
    overlap = ((cstart[:, None] <= jstart[None, :] + SEL_LEN - 1)
               & (cend[:, None] >= jstart[None, :])).astype(jnp.float32)
    kb = ks.reshape(B, n_sel, SEL_LEN, G, Dh).transpose(0, 3, 1, 2, 4)
    vb = vs.reshape(B, n_sel, SEL_LEN, G, Dh).transpose(0, 3, 1, 2, 4)
    kpad = jnp.pad(kw, ((0, 0), (WIN, 0), (0, 0), (0, 0)))
    vpad = jnp.pad(vw, ((0, 0), (WIN, 0), (0, 0), (0, 0)))
    ppad = jnp.pad(pos_f, (WIN, 0))
    sl = slopes.reshape(1, G, HPG, 1, 1)
    bi = jnp.arange(B)[:, None, None, None]
    gi = jnp.arange(G)[None, :, None, None]
    jblk = jnp.arange(n_sel)
    wrel = jnp.arange(WIN + NSA_Q_BLOCK) - WIN
    n_keys = topn * SEL_LEN

    def block(i):
        q0 = i * NSA_Q_BLOCK
        tq = q0 + jnp.arange(NSA_Q_BLOCK)
        pq = lax.dynamic_slice_in_dim(pos_f, q0, NSA_Q_BLOCK)
        qb = lax.dynamic_slice_in_dim(q, q0, NSA_Q_BLOCK, axis=1)
        gb = lax.dynamic_slice_in_dim(gates, q0, NSA_Q_BLOCK, axis=1)
        s = (jnp.einsum('bqghd,bngd->bghqn', qb, kc, preferred_element_type=jnp.float32) * scale
             - sl * (pq[:, None] - cpos[None, :]))
        p_c, _ = _masked_softmax(s, cend[None, :] <= tq[:, None])
        o_c = jnp.einsum('bghqn,bngd->bqghd', p_c, vc)
        imp = jnp.einsum('bghqn,nj->bgqj', p_c, overlap)
        cur = tq[:, None] // SEL_LEN
        forced = (jblk[None, :] == 0) | (jblk[None, :] == cur) | (jblk[None, :] == cur - 1)
        score = jnp.where(jblk[None, :] <= cur, imp + FORCE_BONUS * forced, -jnp.inf)
        top_val, top_idx = lax.top_k(score, topn)
        tok = top_idx[..., None] * SEL_LEN + jnp.arange(SEL_LEN)
        m_s = jnp.isfinite(top_val)[..., None] & (tok <= tq[:, None, None])
        tok = tok.reshape(B, G, NSA_Q_BLOCK, n_keys)
        m_s = m_s.reshape(B, G, NSA_Q_BLOCK, n_keys)
        k_g = kb[bi, gi, top_idx].reshape(B, G, NSA_Q_BLOCK, n_keys, Dh)
        v_g = vb[bi, gi, top_idx].reshape(B, G, NSA_Q_BLOCK, n_keys, Dh)
        s = (jnp.einsum('bqghd,bgqmd->bghqm', qb, k_g, preferred_element_type=jnp.float32) * scale
             - sl * (pq[:, None] - pos_f[tok])[:, :, None])
        p_s, _ = _masked_softmax(s, m_s[:, :, None])
        o_s = jnp.einsum('bghqm,bgqmd->bqghd', p_s, v_g)
        kwb = lax.dynamic_slice_in_dim(kpad, q0, WIN + NSA_Q_BLOCK, axis=1)
        vwb = lax.dynamic_slice_in_dim(vpad, q0, WIN + NSA_Q_BLOCK, axis=1)
        pwb = lax.dynamic_slice_in_dim(ppad, q0, WIN + NSA_Q_BLOCK)
        kidx = q0 + wrel
        dist = tq[:, None] - kidx[None, :]
        m_w = (kidx[None, :] >= 0) & (dist >= 0) & (dist < WIN)
        s = (jnp.einsum('bqghd,bkgd->bghqk', qb, kwb, preferred_element_type=jnp.float32) * scale
             - sl * (pq[:, None] - pwb[None, :]))
        p_w, _ = _masked_softmax(s, m_w)
        o_w = jnp.einsum('bghqk,bkgd->bqghd', p_w, vwb)
        return gb[..., 0:1] * o_c + gb[..., 1:2] * o_s + gb[..., 2:3] * o_w

    return _sweep(block, S, NSA_Q_BLOCK)


def _dilated(q, k, v, pos_f, slopes):
    B, S, _, H, Dh = q.shape
    scale = Dh ** -0.5
    sl = slopes.reshape(N_DIL, H)
    k_groups = [k[:, :, g] for g in range(N_DIL)]
    v_groups = [v[:, :, g] for g in range(N_DIL)]

    def block(i):
        q0 = i * Q_BLOCK
        tq = q0 + jnp.arange(Q_BLOCK)
        pq = lax.dynamic_slice_in_dim(pos_f, q0, Q_BLOCK)
        qb = lax.dynamic_slice_in_dim(q, q0, Q_BLOCK, axis=1)
        outs, lses = [], []
        for g, (window, dil) in enumerate(DIL_CFG):
            kidx = tq[:, None] - dil * jnp.arange(window // dil + 1)[None, :]
            valid = kidx >= 0
            kidx = jnp.maximum(kidx, 0)
            s = (jnp.einsum('bqhd,bqkhd->bhqk', qb[:, :, g], k_groups[g][:, kidx],
                            preferred_element_type=jnp.float32) * scale
                 - sl[g][:, None, None] * (pq[:, None] - pos_f[kidx]))
            p, lse = _masked_softmax(s, valid)
            outs.append(jnp.einsum('bhqk,bqkhd->bqhd', p, v_groups[g][:, kidx]))
            lses.append(lse)
        alpha = jax.nn.softmax(jnp.stack(lses), axis=0)
        return jnp.einsum('gbhq,gbqhd->bqhd', alpha, jnp.stack(outs))

    return _sweep(block, S, Q_BLOCK)


def _even_mixer(h, w_in, w_out, sb_qn, sb_kn, qa_g, wq_up, kva_g, wkv_up, qn, kn, cos, sin):
    B, S, _ = h.shape
    sb_q, sb_k, sb_v, sb_z, q_lat, kv_lat, k_rope, mla_z = _split(h @ w_in, EVEN_SPLITS)
    heads = lambda t: t.reshape(B, S, SB_HEADS, HEAD_DIM)
    o_sb = _stick_breaking(_rms(heads(sb_q), sb_qn), _rms(heads(sb_k), sb_kn), heads(sb_v))
    o_mla = _mla(q_lat, kv_lat, k_rope, qa_g, wq_up, kva_g, wkv_up, qn, kn, cos, sin)
    mixed = jnp.concatenate([o_sb.reshape(B, S, SB_W) * jax.nn.silu(sb_z),
                             o_mla.reshape(B, S, MLA_OUT) * jax.nn.silu(mla_z)], axis=-1)
    return mixed @ w_out


def _odd_mixer(h, w_in, w_out, nsa_qn, nsa_kn, cmp_wk, cmp_wv, pe_k, pe_v, dil_qn, dil_kn,
               pos_f, cidx, cend, cpos, nsa_slopes, dil_slopes):
    B, S, _ = h.shape
    nq, ck, cv, sk, sv, wk, wv, ng, nz, dq, dk, dv, dz = _split(h @ w_in, ODD_SPLITS)
    kvh = lambda t: t.reshape(B, S, NSA_KV_HEADS, HEAD_DIM)
    q = _rms(nq.reshape(B, S, NSA_KV_HEADS, NSA_HPG, HEAD_DIM), nsa_qn)
    kc = _rms(_compress(kvh(ck), cidx, pe_k, cmp_wk), nsa_kn)
    vc = _compress(kvh(cv), cidx, pe_v, cmp_wv)
    gates = jax.nn.sigmoid(ng.astype(jnp.float32)).reshape(B, S, NSA_KV_HEADS, NSA_HPG, 3)
    o_nsa = _nsa(q, kc, vc, _rms(kvh(sk), nsa_kn), kvh(sv), _rms(kvh(wk), nsa_kn), kvh(wv),
                 gates, cpos, cend, pos_f, nsa_slopes)
    dh = lambda t: t.reshape(B, S, N_DIL, DIL_HEADS, HEAD_DIM)
    o_dil = _dilated(_rms(dh(dq), dil_qn), _rms(dh(dk), dil_kn), dh(dv), pos_f, dil_slopes)
    mixed = jnp.concatenate([o_nsa.reshape(B, S, NSA_W) * jax.nn.silu(nz),
                             o_dil.reshape(B, S, DIL_W) * jax.nn.silu(dz)], axis=-1)
    return mixed @ w_out


def setup_inputs(seed: int = 0) -> dict:
    key = jax.random.key(seed)
    keys = iter(jax.random.split(key, 32))

    def dense(shape, fan_in, gain=1.0):
        return gain * fan_in ** -0.5 * jax.random.normal(next(keys), shape, jnp.float32)

    def norm_gain(shape):
        return 1.0 + 0.05 * jax.random.normal(next(keys), shape, jnp.float32)

    n_even = (DEPTH + 1) // 2
    n_odd = DEPTH // 2
    D = D_MODEL
    return {
        'x': jax.random.normal(next(keys), (BATCH, SEQ, D), jnp.float32),
        'c': jax.random.normal(next(keys), (BATCH, D), jnp.float32),
        'positions': jnp.arange(SEQ, dtype=jnp.int32),
        'ada_w': dense((DEPTH, D, 3 * D), D, 0.5),
        'ada_b': 0.02 * jax.random.normal(next(keys), (DEPTH, 3 * D), jnp.float32),
        'norm_g': norm_gain((DEPTH, D)),
        'ev_w_in': dense((n_even, D, EVEN_IN), D),
        'ev_w_out': dense((n_even, EVEN_MIX, D), EVEN_MIX),
        'sb_qn': norm_gain((n_even, HEAD_DIM)),
        'sb_kn': norm_gain((n_even, HEAD_DIM)),
        'mla_qa_g': norm_gain((n_even, MLA_Q_RANK)),
        'mla_wq_up': dense((n_even, MLA_Q_RANK, MLA_HEADS * (MLA_NOPE + MLA_ROPE)), MLA_Q_RANK),
        'mla_kva_g': norm_gain((n_even, MLA_KV_RANK)),
        'mla_wkv_up': dense((n_even, MLA_KV_RANK, MLA_HEADS * (MLA_NOPE + MLA_V)), MLA_KV_RANK),
        'mla_qn': norm_gain((n_even, MLA_NOPE + MLA_ROPE)),
        'mla_kn': norm_gain((n_even, MLA_NOPE + MLA_ROPE)),
        'od_w_in': dense((n_odd, D, ODD_IN), D),
        'od_w_out': dense((n_odd, ODD_MIX, D), ODD_MIX),
        'nsa_qn': norm_gain((n_odd, HEAD_DIM)),
        'nsa_kn': norm_gain((n_odd, HEAD_DIM)),
        'nsa_cmp_wk': dense((n_odd, CMP_LEN * HEAD_DIM, HEAD_DIM), CMP_LEN * HEAD_DIM),
        'nsa_cmp_wv': dense((n_odd, CMP_LEN * HEAD_DIM, HEAD_DIM), CMP_LEN * HEAD_DIM),
        'nsa_cmp_pe_k': 0.1 * jax.random.normal(next(keys), (n_odd, CMP_LEN, HEAD_DIM), jnp.float32),
        'nsa_cmp_pe_v': 0.1 * jax.random.normal(next(keys), (n_odd, CMP_LEN, HEAD_DIM), jnp.float32),
        'dil_qn': norm_gain((n_odd, HEAD_DIM)),
        'dil_kn': norm_gain((n_odd, HEAD_DIM)),
    }


def reference(x, c, positions, ada_w, ada_b, norm_g, ev_w_in, ev_w_out, sb_qn, sb_kn,
              mla_qa_g, mla_wq_up, mla_kva_g, mla_wkv_up, mla_qn, mla_kn, od_w_in, od_w_out,
              nsa_qn, nsa_kn, nsa_cmp_wk, nsa_cmp_wv, nsa_cmp_pe_k, nsa_cmp_pe_v, dil_qn, dil_kn):
    S = x.shape[1]
    pos_f = positions.astype(jnp.float32)
    inv_freq = ROPE_BASE ** (-jnp.arange(0, MLA_ROPE, 2, dtype=jnp.float32) / MLA_ROPE)
    ang = pos_f[:, None] * inv_freq[None, :]
    cos, sin = jnp.cos(ang), jnp.sin(ang)
    nsa_slopes = _alibi_slopes(NSA_HEADS)
    dil_slopes = _alibi_slopes(N_DIL * DIL_HEADS)
    n_cmp = (S - CMP_LEN) // CMP_STRIDE + 1
    cidx = jnp.arange(n_cmp)[:, None] * CMP_STRIDE + jnp.arange(CMP_LEN)[None, :]
    cend = cidx[:, -1]
    cpos = jnp.mean(pos_f[cidx], axis=-1)
    for layer in range(DEPTH):
        h, gate = _modulate(x, c, ada_w[layer], ada_b[layer], norm_g[layer])
        j = layer // 2
        if layer % 2 == 0:
            y = _even_mixer(h, ev_w_in[j], ev_w_out[j], sb_qn[j], sb_kn[j], mla_qa_g[j], mla_wq_up[j],
                            mla_kva_g[j], mla_wkv_up[j], mla_qn[j], mla_kn[j], cos, sin)
        else:
            y = _odd_mixer(h, od_w_in[j], od_w_out[j], nsa_qn[j], nsa_kn[j], nsa_cmp_wk[j], nsa_cmp_wv[j],
                           nsa_cmp_pe_k[j], nsa_cmp_pe_v[j], dil_qn[j], dil_kn[j],
                           pos_f, cidx, cend, cpos, nsa_slopes, dil_slopes)
        x = (x + gate * y).astype(x.dtype)
    return x
```

```python
import functools

import numpy as np
import jax
import jax.numpy as jnp
from jax import lax
from jax.experimental import pallas as pl
from jax.experimental.pallas import tpu as pltpu

F32 = jnp.float32
BF16 = jnp.bfloat16

D_MODEL = 1024
HEAD_DIM = 64
NORM_EPS = 1e-6
TINY = 1e-30
SB_HEADS = 8
MLA_HEADS = 8
MLA_Q_RANK = 256
MLA_KV_RANK = 128
MLA_NOPE = 64
MLA_ROPE = 32
MLA_V = 64
ROPE_BASE = 10000.0
NSA_HEADS = 12
NSA_KV_HEADS = 3
NSA_HPG = 4
CMP_LEN = 32
CMP_STRIDE = 16
SEL_LEN = 64
SEL_TOPN = 16
WIN = 512
FORCE_BONUS = 1e3
DIL_CFG = ((128, 1), (512, 4), (2048, 16))
N_DIL = 3
DIL_HEADS = 4
SB_W = SB_HEADS * HEAD_DIM
MLA_OUT = MLA_HEADS * MLA_V
NSA_W = NSA_HEADS * HEAD_DIM
NSA_KV_W = NSA_KV_HEADS * HEAD_DIM
DIL_W = DIL_HEADS * HEAD_DIM

LANES = 128
MXU_DIM = 256
NEG_MASK = -1e30
SEL_OFF = 2.0 ** 30
VMEM_LIMIT = 56 * 1024 * 1024

TS_PROJ = 512
TQ_SB = 512
TK_SB = 256
TQ_MLA = 512
TK_MLA = 256
TQ_NSA = 256
TK_NSA = 256
T_DIL = 128


def _dot(a, b):
    return jnp.dot(a, b, preferred_element_type=F32)


def _dot_nt(a, b):
    return lax.dot_general(a, b, (((1,), (1,)), ((), ())), preferred_element_type=F32)


def _split_hl(a):
    hi = a.astype(BF16)
    lo = (a - hi.astype(F32)).astype(BF16)
    return hi, lo


def _dot_hl(a, b):
    hi, lo = _split_hl(a)
    return _dot(hi, b) + _dot(lo, b)


def _sigmoid(z):
    return 1.0 / (1.0 + jnp.exp(-z))


def _cparams(n_axes):
    return pltpu.CompilerParams(dimension_semantics=("arbitrary",) * n_axes,
                                vmem_limit_bytes=VMEM_LIMIT)


def _full(shape):
    n = len(shape)
    return pl.BlockSpec(shape, lambda *a, _n=n: (0,) * _n)


def _mod_kernel(c_ref, w_ref, b_ref, o_ref):
    c = c_ref[...]
    a = c * _sigmoid(c)
    ah, al = _split_hl(a)
    wh, wl = _split_hl(w_ref[0])
    o_ref[0] = _dot(ah, wh) + _dot(ah, wl) + _dot(al, wh) + b_ref[0]


def _modulation(c8, ada_w, ada_b):
    depth, d, n3 = ada_w.shape
    tn = 1024
    return pl.pallas_call(
        _mod_kernel,
        grid=(depth, n3 // tn),
        in_specs=[pl.BlockSpec((8, d), lambda l, j: (0, 0)),
                  pl.BlockSpec((1, d, tn), lambda l, j: (l, 0, j)),
                  pl.BlockSpec((1, 1, tn), lambda l, j: (l, 0, j))],
        out_specs=pl.BlockSpec((1, 8, tn), lambda l, j: (l, 0, j)),
        out_shape=jax.ShapeDtypeStruct((depth, 8, n3), F32),
        compiler_params=_cparams(2),
        name="adaln_mod",
    )(c8, ada_w, ada_b.reshape(depth, 1, n3))


def _modulated(x_ref, mod_ref, ng):
    x = x_ref[0]
    mod = mod_ref[0]
    shift = mod[:, 0:D_MODEL]
    scale = mod[:, D_MODEL:2 * D_MODEL]
    ms = jnp.mean(x * x, axis=-1, keepdims=True)
    h = x * lax.rsqrt(ms + NORM_EPS) * ng
    h = h * (1.0 + scale) + shift
    return h.astype(BF16)


def _group_norm_chunk(t, g, inv_cnt):
    ss = _dot_hl(t * t, g)
    return t * lax.rsqrt(ss * inv_cnt + NORM_EPS)


def _row_rms(t, gain):
    ms = jnp.mean(t * t, axis=-1, keepdims=True)
    return t * lax.rsqrt(ms + NORM_EPS) * gain


EV_COLS = 3200


def _k1_even_kernel(x_ref, mod_ref, vec_ref, w_ref, wq_ref, wqs_ref, wkv_ref, g64_ref, gm_ref,
                    cos_ref, sin_ref, sbq_o, sbk_o, sbv_o, gz_o, mq_o, mk_o, mv_o):
    hb = _modulated(x_ref, mod_ref, vec_ref[0:1, :])
    g64 = g64_ref[...]
    gm = gm_ref[...]
    inv64 = 1.0 / HEAD_DIM

    u = _dot(hb, w_ref[:, 0:1536])
    for c in range(2):
        sl = slice(c * 256, (c + 1) * 256)
        qn = _group_norm_chunk(u[:, c * 256:(c + 1) * 256], g64, inv64)
        sbq_o[0, :, sl] = (qn * vec_ref[1:2, sl]).astype(BF16)
        kn = _group_norm_chunk(u[:, 512 + c * 256:512 + (c + 1) * 256], g64, inv64)
        sbk_o[0, :, sl] = (kn * vec_ref[2:3, sl]).astype(BF16)
    sbv_o[0] = u[:, 1024:1536].astype(BF16)

    uz = _dot(hb, w_ref[:, 1536:2560])
    gz_o[0] = (uz * _sigmoid(uz)).astype(BF16)

    ul = _dot(hb, w_ref[:, 2560:3200])
    qlat = _row_rms(ul[:, 0:256], vec_ref[3:4, 0:256]).astype(BF16)
    kvlat = _row_rms(ul[:, 256:384], vec_ref[4:5, 0:128]).astype(BF16)
    krm = ul[:, 384:512]
    krs = ul[:, 512:640]
    cs = cos_ref[...]
    sn = sin_ref[...]
    cs2 = jnp.concatenate([cs, cs], axis=1)
    sn2 = jnp.concatenate([sn, sn], axis=1)

    tq = _dot(qlat, wq_ref[...])
    tqs = _dot(qlat, wqs_ref[...])
    for p in range(4):
        sl = slice(p * 256, (p + 1) * 256)
        tc = tq[:, p * 256:(p + 1) * 256]
        ss = _dot_hl(tc * tc, gm)
        inv = lax.rsqrt(ss * vec_ref[10:11, sl] + NORM_EPS)
        a = tc * inv * vec_ref[5:6, sl]
        b = tqs[:, p * 256:(p + 1) * 256] * inv * vec_ref[6:7, sl]
        mq_o[0, :, sl] = (a * cs2 + b * sn2).astype(BF16)

    sskr = _dot_hl(krm * krm, gm[0:128, 0:128])
    invr = lax.rsqrt(sskr * vec_ref[10:11, 0:128] + NORM_EPS)
    kr = (krm * invr * vec_ref[8:9, 0:128]) * cs + (krs * invr * vec_ref[9:10, 0:128]) * sn
    kr2 = jnp.concatenate([kr, kr], axis=1)

    kv = _dot(kvlat, wkv_ref[...])
    for p in range(4):
        sl = slice(p * 256, (p + 1) * 256)
        kn = _group_norm_chunk(kv[:, p * 256:(p + 1) * 256], g64, inv64)
        mk_o[0, :, sl] = (kn * vec_ref[7:8, sl] + kr2).astype(BF16)
    mv_o[0] = kv[:, 1024:1536].astype(BF16)


def _k1_even(x, mod3, vec, w, wq, wqs, wkv, g64, gm, cos128, sin128):
    b, s, d = x.shape
    ts = min(TS_PROJ, s)
    row = lambda n: pl.BlockSpec((1, ts, n), lambda i, j: (i, j, 0))
    outs = [(512, BF16), (512, BF16), (512, BF16), (1024, BF16), (1024, BF16), (1024, BF16), (512, BF16)]
    return pl.pallas_call(
        _k1_even_kernel,
        grid=(b, s // ts),
        in_specs=[row(d),
                  pl.BlockSpec((1, 1, 3 * d), lambda i, j: (i, 0, 0)),
                  _full(vec.shape), _full(w.shape), _full(wq.shape), _full(wqs.shape), _full(wkv.shape),
                  _full(g64.shape), _full(gm.shape),
                  pl.BlockSpec((ts, 128), lambda i, j: (j, 0)),
                  pl.BlockSpec((ts, 128), lambda i, j: (j, 0))],
        out_specs=[row(n) for n, _ in outs],
        out_shape=[jax.ShapeDtypeStruct((b, s, n), dt) for n, dt in outs],
        compiler_params=_cparams(2),
        name="inproj_even",
    )(x, mod3, vec, w, wq, wqs, wkv, g64, gm, cos128, sin128)


def _sb_kernel(q_ref, k_ref, v_ref, tri_ref, o_ref, acc_ref, car_ref, *, tq, tk):
    qi = pl.program_id(2)
    q = q_ref[0]
    lane = lax.broadcasted_iota(jnp.int32, (1, LANES), 1)
    lo_half = lane < HEAD_DIM
    zero = jnp.zeros_like(q)
    qs = (jnp.where(lo_half, q, zero), jnp.where(lo_half, zero, q))
    tri = tri_ref[...]
    acc_ref[...] = jnp.zeros_like(acc_ref)
    car_ref[...] = jnp.zeros_like(car_ref)
    rows = qi * tq + lax.broadcasted_iota(jnp.int32, (tq, 1), 0)
    col0 = lax.broadcasted_iota(jnp.int32, (1, tk), 1)

    def tile(kt, masked):
        off = pl.multiple_of(kt * tk, tk)
        k = k_ref[0, pl.ds(off, tk), :]
        v = v_ref[0, pl.ds(off, tk), :]
        if masked:
            strict = (kt * tk + col0) < rows
        for i in range(2):
            z = _dot_nt(qs[i], k)
            soft = jnp.log(1.0 + jnp.exp(-jnp.abs(z)))
            lom = -jnp.maximum(z, 0.0) - soft
            lb = lom + z
            if masked:
                lom = jnp.where(strict, lom, 0.0)
            hi, lo = _split_hl(lom)
            car = car_ref[i]
            tail = _dot(hi, tri) + _dot(lo, tri) + car
            w = jnp.exp(lb + tail)
            if masked:
                w = jnp.where(strict, w, 0.0)
            acc_ref[i] += _dot(w.astype(BF16), v)
            car_ref[i] = car + jnp.sum(lom, axis=-1, keepdims=True)

    per = tq // tk
    for dd in range(per):
        tile((qi + 1) * per - 1 - dd, True)

    def body(j, carry):
        tile(qi * per - 1 - j, False)
        return carry

    lax.fori_loop(0, qi * per, body, 0)
    o_ref[0] = jnp.where(lo_half, acc_ref[0], acc_ref[1])


def _sb_attention(q, k, v, tri):
    b, s, w = q.shape
    tq = min(TQ_SB, s)
    tk = min(TK_SB, tq)
    kern = functools.partial(_sb_kernel, tq=tq, tk=tk)
    return pl.pallas_call(
        kern,
        grid=(b, w // LANES, s // tq),
        in_specs=[pl.BlockSpec((1, tq, LANES), lambda i, p, j: (i, j, p)),
                  pl.BlockSpec((1, s, LANES), lambda i, p, j: (i, 0, p)),
                  pl.BlockSpec((1, s, LANES), lambda i, p, j: (i, 0, p)),
                  _full(tri.shape)],
        out_specs=pl.BlockSpec((1, tq, LANES), lambda i, p, j: (i, j, p)),
        out_shape=jax.ShapeDtypeStruct((b, s, w), F32),
        scratch_shapes=[pltpu.VMEM((2, tq, LANES), F32), pltpu.VMEM((2, tq, 1), F32)],
        compiler_params=_cparams(3),
        name="stickbreak_attn",
    )(q, k, v, tri)


def _mla_kernel(q_ref, k_ref, v_ref, o_ref, acc_ref, m_ref, l_ref, *, tq, tk):
    qi = pl.program_id(2)
    q = q_ref[0]
    qs = (q[:, 0:LANES], q[:, LANES:2 * LANES])
    lane = lax.broadcasted_iota(jnp.int32, (1, LANES), 1)
    lo_half = lane < HEAD_DIM
    acc_ref[...] = jnp.zeros_like(acc_ref)
    l_ref[...] = jnp.zeros_like(l_ref)
    m_ref[...] = jnp.full_like(m_ref, NEG_MASK)
    rows = qi * tq + lax.broadcasted_iota(jnp.int32, (tq, 1), 0)
    col0 = lax.broadcasted_iota(jnp.int32, (1, tk), 1)

    def tile(kt, masked):
        off = pl.multiple_of(kt * tk, tk)
        k = k_ref[0, pl.ds(off, tk), :]
        v = v_ref[0, pl.ds(off, tk), :]
        if masked:
            keep = (kt * tk + col0) <= rows
        for i in range(2):
            s = _dot_nt(qs[i], k[:, i * LANES:(i + 1) * LANES])
            if masked:
                s = jnp.where(keep, s, NEG_MASK)
            m_old = m_ref[i]
            m_new = jnp.maximum(m_old, jnp.max(s, axis=-1, keepdims=True))
            p = jnp.exp(s - m_new)
            alpha = jnp.exp(m_old - m_new)
            l_ref[i] = alpha * l_ref[i] + jnp.sum(p, axis=-1, keepdims=True)
            acc_ref[i] = alpha * acc_ref[i] + _dot(p.astype(BF16), v)
            m_ref[i] = m_new

    per = tq // tk
    for dd in range(per):
        tile(qi * per + dd, True)

    def body(j, carry):
        tile(j, False)
        return carry

    lax.fori_loop(0, qi * per, body, 0)
    o_ref[0] = jnp.where(lo_half, acc_ref[0] / l_ref[0], acc_ref[1] / l_ref[1])


def _mla_attention(q, k, v):
    b, s, _ = q.shape
    tq = min(TQ_MLA, s)
    tk = min(TK_MLA, tq)
    kern = functools.partial(_mla_kernel, tq=tq, tk=tk)
    return pl.pallas_call(
        kern,
        grid=(b, MLA_HEADS // 2, s // tq),
        in_specs=[pl.BlockSpec((1, tq, 2 * LANES), lambda i, p, j: (i, j, p)),
                  pl.BlockSpec((1, s, 2 * LANES), lambda i, p, j: (i, 0, p)),
                  pl.BlockSpec((1, s, LANES), lambda i, p, j: (i, 0, p))],
        out_specs=pl.BlockSpec((1, tq, LANES), lambda i, p, j: (i, j, p)),
        out_shape=jax.ShapeDtypeStruct((b, s, MLA_OUT), F32),
        scratch_shapes=[pltpu.VMEM((2, tq, LANES), F32), pltpu.VMEM((2, tq, 1), F32),
                        pltpu.VMEM((2, tq, 1), F32)],
        compiler_params=_cparams(3),
        name="mla_attn",
    )(q, k, v)


def _out_even_kernel(x_ref, mod_ref, osb_ref, omla_ref, gz_ref, w_ref, o_ref):
    gz = gz_ref[0].astype(F32)
    m1 = (osb_ref[0] * gz[:, 0:SB_W]).astype(BF16)
    m2 = (omla_ref[0] * gz[:, SB_W:SB_W + MLA_OUT]).astype(BF16)
    y = _dot(m1, w_ref[0:SB_W, :]) + _dot(m2, w_ref[SB_W:SB_W + MLA_OUT, :])
    gate = mod_ref[0][:, 2 * D_MODEL:3 * D_MODEL]
    o_ref[0] = x_ref[0] + gate * y


def _out_even(x, mod3, osb, omla, gz, w):
    b, s, d = x.shape
    ts = min(TS_PROJ, s)
    row = lambda n: pl.BlockSpec((1, ts, n), lambda i, j: (i, j, 0))
    return pl.pallas_call(
        _out_even_kernel,
        grid=(b, s // ts),
        in_specs=[row(d), pl.BlockSpec((1, 1, 3 * d), lambda i, j: (i, 0, 0)),
                  row(SB_W), row(MLA_OUT), row(SB_W + MLA_OUT), _full(w.shape)],
        out_specs=row(d),
        out_shape=jax.ShapeDtypeStruct((b, s, d), F32),
        compiler_params=_cparams(2),
        name="outproj_even",
    )(x, mod3, osb, omla, gz, w)


NSA_COLS = 5248


def _k1_nsa_kernel(x_ref, mod_ref, vec_ref, w_ref, g64_ref,
                   q_o, ck_o, cv_o, sk_o, sv_o, wk_o, wv_o, gt_o, gz_o):
    hb = _modulated(x_ref, mod_ref, vec_ref[0:1, 0:D_MODEL])
    g64 = g64_ref[...]
    inv64 = 1.0 / HEAD_DIM

    uq = _dot(hb, w_ref[:, 0:1536])
    for c in range(6):
        sl = slice(c * 256, (c + 1) * 256)
        qn = _group_norm_chunk(uq[:, c * 256:(c + 1) * 256], g64, inv64)
        q_o[0, :, sl] = (qn * vec_ref[1:2, sl]).astype(BF16)

    uc = _dot(hb, w_ref[:, 1536:2304])
    for g in range(NSA_KV_HEADS):
        ck_o[0, g] = uc[:, g * 128:(g + 1) * 128].astype(BF16)
        cv_o[0, g] = uc[:, 384 + g * 128:384 + (g + 1) * 128].astype(BF16)

    us = _dot(hb, w_ref[:, 2304:3840])
    kgain = vec_ref[2:3, 0:384]
    g128 = g64[0:128, 0:128]
    for g in range(NSA_KV_HEADS):
        sl = slice(g * 128, (g + 1) * 128)
        t = us[:, g * 128:(g + 1) * 128]
        ss = _dot_hl(t * t, g128)
        sk_o[0, :, sl] = (t * lax.rsqrt(ss * inv64 + NORM_EPS) * kgain[:, sl]).astype(BF16)
        t = us[:, 768 + g * 128:768 + (g + 1) * 128]
        ss = _dot_hl(t * t, g128)
        wk_o[0, :, sl] = (t * lax.rsqrt(ss * inv64 + NORM_EPS) * kgain[:, sl]).astype(BF16)
    sv_o[0] = us[:, 384:768].astype(BF16)
    wv_o[0] = us[:, 1152:1536].astype(BF16)

    ug = _dot(hb, w_ref[:, 3840:4224])
    gt_o[0] = _sigmoid(ug)
    uz = _dot(hb, w_ref[:, 4224:5248])
    gz_o[0] = (uz * _sigmoid(uz)).astype(BF16)


def _k1_nsa(x, mod3, vec, w, g64):
    b, s, d = x.shape
    ts = min(TS_PROJ, s)
    row = lambda n: pl.BlockSpec((1, ts, n), lambda i, j: (i, j, 0))
    grp = pl.BlockSpec((1, NSA_KV_HEADS, ts, 128), lambda i, j: (i, 0, j, 0))
    return pl.pallas_call(
        _k1_nsa_kernel,
        grid=(b, s // ts),
        in_specs=[row(d), pl.BlockSpec((1, 1, 3 * d), lambda i, j: (i, 0, 0)),
                  _full(vec.shape), _full(w.shape), _full(g64.shape)],
        out_specs=[row(1536), grp, grp, row(384), row(384), row(384), row(384), row(384), row(1024)],
        out_shape=[jax.ShapeDtypeStruct((b, s, 1536), BF16),
                   jax.ShapeDtypeStruct((b, NSA_KV_HEADS, s, 128), BF16),
                   jax.ShapeDtypeStruct((b, NSA_KV_HEADS, s, 128), BF16),
                   jax.ShapeDtypeStruct((b, s, 384), BF16),
                   jax.ShapeDtypeStruct((b, s, 384), BF16),
                   jax.ShapeDtypeStruct((b, s, 384), BF16),
                   jax.ShapeDtypeStruct((b, s, 384), BF16),
                   jax.ShapeDtypeStruct((b, s, 384), F32),
                   jax.ShapeDtypeStruct((b, s, 1024), BF16)],
        compiler_params=_cparams(2),
        name="inproj_nsa",
    )(x, mod3, vec, w, g64)


def _k1_dil_kernel(x_ref, mod_ref, vec_ref, w_ref, g64_ref, q_o, k_o, v_o):
    hb = _modulated(x_ref, mod_ref, vec_ref[0:1, 0:D_MODEL])
    g64 = g64_ref[...]
    inv64 = 1.0 / HEAD_DIM
    u = _dot(hb, w_ref[...])
    for g in range(N_DIL):
        qn = _group_norm_chunk(u[:, g * 256:(g + 1) * 256], g64, inv64)
        q_o[g, 0] = (qn * vec_ref[3:4, 0:256]).astype(BF16)
        kn = _group_norm_chunk(u[:, 768 + g * 256:768 + (g + 1) * 256], g64, inv64)
        k_o[g, 0] = (kn * vec_ref[4:5, 0:256]).astype(BF16)
        v_o[g, 0] = u[:, 1536 + g * 256:1536 + (g + 1) * 256].astype(BF16)


def _k1_dil(x, mod3, vec, w, g64):
    b, s, d = x.shape
    ts = min(TS_PROJ, s)
    row = lambda n: pl.BlockSpec((1, ts, n), lambda i, j: (i, j, 0))
    grp = pl.BlockSpec((N_DIL, 1, ts, 256), lambda i, j: (0, i, j, 0))
    shp = jax.ShapeDtypeStruct((N_DIL, b, s, 256), BF16)
    return pl.pallas_call(
        _k1_dil_kernel,
        grid=(b, s // ts),
        in_specs=[row(d), pl.BlockSpec((1, 1, 3 * d), lambda i, j: (i, 0, 0)),
                  _full(vec.shape), _full(w.shape), _full(g64.shape)],
        out_specs=[grp, grp, grp],
        out_shape=[shp, shp, shp],
        compiler_params=_cparams(2),
        name="inproj_dil",
    )(x, mod3, vec, w, g64)


def _compress_kernel(xk_ref, xv_ref, wk_ref, wv_ref, pek_ref, pev_ref, gain_ref, g128_ref, kc_o, vc_o):
    nch = xk_ref.shape[2]

    def comp(x_ref, w_ref, pe_ref):
        x = x_ref[0, 0]
        lo = _dot(x, w_ref[0])
        hi = _dot(x, w_ref[1])
        pec = _dot_hl(pe_ref[0], w_ref[0]) + _dot_hl(pe_ref[1], w_ref[1])
        return lo + pltpu.roll(hi, nch - 1, 0) + pec[0:1, :]

    kc = comp(xk_ref, wk_ref, pek_ref)
    ss = _dot_hl(kc * kc, g128_ref[...])
    kc_o[0, 0] = (kc * lax.rsqrt(ss * (1.0 / HEAD_DIM) + NORM_EPS) * gain_ref[...]).astype(BF16)
    vc_o[0, 0] = comp(xv_ref, wv_ref, pev_ref).astype(BF16)


def _compress(ck, cv, wk, wv, pek, pev, gain, g128):
    b, g, s, _ = ck.shape
    nch = s // CMP_STRIDE
    xk = ck.reshape(b, g, nch, CMP_STRIDE * 128)
    xv = cv.reshape(b, g, nch, CMP_STRIDE * 128)
    blk = pl.BlockSpec((1, 1, nch, CMP_STRIDE * 128), lambda i, j: (i, j, 0, 0))
    oblk = pl.BlockSpec((1, 1, nch, 128), lambda i, j: (i, j, 0, 0))
    shp = jax.ShapeDtypeStruct((b, g, nch, 128), BF16)
    return pl.pallas_call(
        _compress_kernel,
        grid=(b, g),
        in_specs=[blk, blk, _full(wk.shape), _full(wv.shape), _full(pek.shape), _full(pev.shape),
                  _full(gain.shape), _full(g128.shape)],
        out_specs=[oblk, oblk],
        out_shape=[shp, shp],
        compiler_params=_cparams(2),
        name="nsa_compress",
    )(xk, xv, wk, wv, pek, pev, gain, g128)


def _pair_select(lo_half, a, b):
    return jnp.where(lo_half, a, b)


def _nsa_cmp_kernel(slope_ref, q_ref, kc_ref, vc_ref, ovl_ref, cpos_ref, pq_ref, gt_ref,
                    oc_o, selb_o, *, tq):
    g = pl.program_id(1)
    qi = pl.program_id(2)
    q4 = q_ref[0]
    kc = kc_ref[0, 0]
    vc = vc_ref[0, 0]
    nch = kc.shape[0]
    lane = lax.broadcasted_iota(jnp.int32, (1, LANES), 1)
    lo_half = lane < HEAD_DIM
    t = qi * tq + lax.broadcasted_iota(jnp.int32, (tq, 1), 0)
    n_id = lax.broadcasted_iota(jnp.int32, (1, nch), 1)
    valid = (n_id * CMP_STRIDE + (CMP_LEN - 1)) <= t
    dist = pq_ref[...] - cpos_ref[...]
    gt = gt_ref[0]
    psum = jnp.zeros((tq, nch), F32)
    outs = []
    for hh in range(NSA_HPG):
        s = _dot_nt(q4[:, hh * LANES:(hh + 1) * LANES], kc) - slope_ref[g * NSA_HPG + hh] * dist
        s = jnp.where(valid, s, -jnp.inf)
        mx = jnp.max(s, axis=-1, keepdims=True)
        mx = jnp.where(mx == -jnp.inf, 0.0, mx)
        e = jnp.exp(s - mx)
        den = jnp.maximum(jnp.sum(e, axis=-1, keepdims=True), TINY)
        p = e / den
        psum = psum + p
        outs.append(_dot(p.astype(BF16), vc) * gt[:, 3 * hh:3 * hh + 1])
    oc_o[0, :, 0:LANES] = _pair_select(lo_half, outs[0], outs[1])
    oc_o[0, :, LANES:2 * LANES] = _pair_select(lo_half, outs[2], outs[3])

    imp = _dot_hl(psum, ovl_ref[...])
    cur = t >> 6
    forced = (lane == 0) | (lane == cur) | (lane == cur - 1)
    allowed = lane <= cur
    score = jnp.where(allowed, imp + jnp.where(forced, FORCE_BONUS, 0.0), -jnp.inf)

    def pick(_, carry):
        sc, sel = carry
        idx = jnp.argmax(sc, axis=-1, keepdims=True).astype(jnp.int32)
        hit = lane == idx
        return jnp.where(hit, -jnp.inf, sc), jnp.where(hit, 1.0, sel)

    _, sel = lax.fori_loop(0, SEL_TOPN, pick, (score, jnp.zeros((tq, LANES), F32)))
    chosen = (sel > 0.5) & allowed
    selb_o[0, 0] = jnp.where(chosen, 0.0, -SEL_OFF).astype(BF16)


def _nsa_cmp(slopes, q, kc, vc, ovl, cpos, pos_col, gates):
    b, s, _ = q.shape
    nch = kc.shape[2]
    tq = min(TQ_NSA, s)
    kern = functools.partial(_nsa_cmp_kernel, tq=tq)
    return pl.pallas_call(
        kern,
        grid=(b, NSA_KV_HEADS, s // tq),
        in_specs=[pl.BlockSpec(memory_space=pltpu.SMEM),
                  pl.BlockSpec((1, tq, 4 * LANES), lambda i, g, j: (i, j, g)),
                  pl.BlockSpec((1, 1, nch, LANES), lambda i, g, j: (i, g, 0, 0)),
                  pl.BlockSpec((1, 1, nch, LANES), lambda i, g, j: (i, g, 0, 0)),
                  _full(ovl.shape), _full(cpos.shape),
                  pl.BlockSpec((tq, 1), lambda i, g, j: (j, 0)),
                  pl.BlockSpec((1, tq, LANES), lambda i, g, j: (i, j, g))],
        out_specs=[pl.BlockSpec((1, tq, 2 * LANES), lambda i, g, j: (i, j, g)),
                   pl.BlockSpec((1, 1, tq, LANES), lambda i, g, j: (i, g, j, 0))],
        out_shape=[jax.ShapeDtypeStruct((b, s, NSA_W), F32),
                   jax.ShapeDtypeStruct((b, NSA_KV_HEADS, s, LANES), BF16)],
        compiler_params=_cparams(3),
        name="nsa_cmp_topk",
    )(slopes, q, kc, vc, ovl, cpos, pos_col, gates)


def _gqa_kernel(slope_ref, q_ref, selb_ref, k_ref, v_ref, pos_ref, gt_ref, o_ref,
                qa_ref, acc_ref, m_ref, l_ref, *, tq, tk, branch):
    g = pl.program_id(1)
    qi = pl.program_id(2)
    q4 = q_ref[0]
    lane = lax.broadcasted_iota(jnp.int32, (1, LANES), 1)
    lo_half = lane < HEAD_DIM
    for hh in range(NSA_HPG):
        qa_ref[hh * tq:(hh + 1) * tq, 0:LANES] = q4[:, hh * LANES:(hh + 1) * LANES]
        if branch == 1:
            qa_ref[hh * tq:(hh + 1) * tq, LANES:2 * LANES] = selb_ref[0, 0]
    acc_ref[...] = jnp.zeros_like(acc_ref)
    l_ref[...] = jnp.zeros_like(l_ref)
    m_ref[...] = jnp.full_like(m_ref, NEG_MASK)
    rows = qi * tq + lax.broadcasted_iota(jnp.int32, (tq, 1), 0)
    col0 = lax.broadcasted_iota(jnp.int32, (1, tk), 1)
    krow = lax.broadcasted_iota(jnp.int32, (tk, 1), 0)
    pref = pos_ref[qi * (tq // tk)][:, 0:1]

    def tile(kt_true, masked):
        kt = jnp.maximum(kt_true, 0)
        off = pl.multiple_of(kt * tk, tk)
        k = k_ref[0, pl.ds(off, tk), :]
        v = v_ref[0, pl.ds(off, tk), :]
        if branch == 1:
            blk = (kt * tk + krow) >> 6
            onehot = jnp.where(lane == blk, 1.0, 0.0).astype(BF16)
            s_all = _dot_nt(qa_ref[...], jnp.concatenate([k, onehot], axis=1))
        else:
            s_all = _dot_nt(qa_ref[:, 0:LANES], k)
        rel = pos_ref[kt] - pref
        if masked:
            cols = kt_true * tk + col0
            d = rows - cols
            keep = d >= 0
            if branch == 2:
                keep = keep & (d < WIN) & (cols >= 0)
        for hh in range(NSA_HPG):
            sl = slice(hh * tq, (hh + 1) * tq)
            s = s_all[hh * tq:(hh + 1) * tq] + slope_ref[g * NSA_HPG + hh] * rel
            if masked:
                s = jnp.where(keep, s, NEG_MASK)
            m_old = m_ref[sl]
            m_new = jnp.maximum(m_old, jnp.max(s, axis=-1, keepdims=True))
            p = jnp.exp(s - m_new)
            alpha = jnp.exp(m_old - m_new)
            l_ref[sl] = alpha * l_ref[sl] + jnp.sum(p, axis=-1, keepdims=True)
            acc_ref[sl] = alpha * acc_ref[sl] + _dot(p.astype(BF16), v)
            m_ref[sl] = m_new

    per = tq // tk
    if branch == 1:
        for dd in range(per):
            tile(qi * per + dd, True)

        def body(j, carry):
            tile(j, False)
            return carry

        lax.fori_loop(0, qi * per, body, 0)
    else:
        for dd in range(per + WIN // tk):
            tile((qi + 1) * per - 1 - dd, True)

    gt = gt_ref[0]
    outs = []
    for hh in range(NSA_HPG):
        sl = slice(hh * tq, (hh + 1) * tq)
        outs.append(acc_ref[sl] / l_ref[sl] * gt[:, 3 * hh + branch:3 * hh + branch + 1])
    o_ref[0, :, 0:LANES] = _pair_select(lo_half, outs[0], outs[1])
    o_ref[0, :, LANES:2 * LANES] = _pair_select(lo_half, outs[2], outs[3])


def _nsa_gqa(slopes, q, selb, k, v, pos2d, gates, branch):
    b, s, _ = q.shape
    tq = min(TQ_NSA, s)
    tk = min(TK_NSA, tq)
    kern = functools.partial(_gqa_kernel, tq=tq, tk=tk, branch=branch)
    return pl.pallas_call(
        kern,
        grid=(b, NSA_KV_HEADS, s // tq),
        in_specs=[pl.BlockSpec(memory_space=pltpu.SMEM),
                  pl.BlockSpec((1, tq, 4 * LANES), lambda i, g, j: (i, j, g)),
                  pl.BlockSpec((1, 1, tq, LANES), lambda i, g, j: (i, g, j, 0)),
                  pl.BlockSpec((1, s, LANES), lambda i, g, j: (i, 0, g)),
                  pl.BlockSpec((1, s, LANES), lambda i, g, j: (i, 0, g)),
                  _full(pos2d.shape),
                  pl.BlockSpec((1, tq, LANES), lambda i, g, j: (i, j, g))],
        out_specs=pl.BlockSpec((1, tq, 2 * LANES), lambda i, g, j: (i, j, g)),
        out_shape=jax.ShapeDtypeStruct((b, s, NSA_W), F32),
        scratch_shapes=[pltpu.VMEM((NSA_HPG * tq, 2 * LANES), BF16),
                        pltpu.VMEM((NSA_HPG * tq, LANES), F32),
                        pltpu.VMEM((NSA_HPG * tq, 1), F32),
                        pltpu.VMEM((NSA_HPG * tq, 1), F32)],
        compiler_params=_cparams(3),
        name="nsa_sel_attn" if branch == 1 else "nsa_win_attn",
    )(slopes, q, selb, k, v, pos2d, gates)


def _dil_kernel(slope_ref, q_ref, k_ref, v_ref, pc_ref, pr_ref, o_ref, lse_ref, *, t, span, grp):
    i = pl.program_id(2)
    q4 = q_ref[0]
    lane = lax.broadcasted_iota(jnp.int32, (1, LANES), 1)
    lo_half = lane < HEAD_DIM
    prev = jnp.maximum(i - 1, 0)
    o_prev = pl.multiple_of(prev * t, t)
    o_cur = pl.multiple_of(i * t, t)
    kk = jnp.concatenate([k_ref[0, pl.ds(o_prev, t), :], k_ref[0, pl.ds(o_cur, t), :]], axis=0)
    vv = jnp.concatenate([v_ref[0, pl.ds(o_prev, t), :], v_ref[0, pl.ds(o_cur, t), :]], axis=0)
    pk = jnp.concatenate([pr_ref[0, prev], pr_ref[0, i]], axis=1)
    dist = pc_ref[0] - pk
    rows = i * t + lax.broadcasted_iota(jnp.int32, (t, 1), 0)
    cols = (i - 1) * t + lax.broadcasted_iota(jnp.int32, (1, 2 * t), 1)
    d = rows - cols
    valid = (cols >= 0) & (d >= 0) & (d <= span)
    zero = jnp.zeros((t, LANES), BF16)
    for p in range(2):
        qp = q4[:, p * LANES:(p + 1) * LANES]
        kp = kk[:, p * LANES:(p + 1) * LANES]
        vp = vv[:, p * LANES:(p + 1) * LANES]
        res = []
        for x in range(2):
            qx = jnp.where(lo_half, qp, zero) if x == 0 else jnp.where(lo_half, zero, qp)
            s = _dot_nt(qx, kp) - slope_ref[grp * DIL_HEADS + 2 * p + x] * dist
            s = jnp.where(valid, s, -jnp.inf)
            mx = jnp.max(s, axis=-1, keepdims=True)
            e = jnp.exp(s - mx)
            den = jnp.maximum(jnp.sum(e, axis=-1, keepdims=True), TINY)
            res.append((_dot((e / den).astype(BF16), vp), mx + jnp.log(den)))
        sl = slice(p * LANES, (p + 1) * LANES)
        o_ref[0, :, sl] = _pair_select(lo_half, res[0][0], res[1][0])
        lse_ref[0, :, sl] = jnp.where(lo_half, res[0][1], res[1][1])


def _dilated(slopes, q, k, v, pos_f, grp):
    window, dil = DIL_CFG[grp]
    b, s, w = q.shape
    sub = s // dil
    t = min(T_DIL, sub)
    span = window // dil
    assert span <= t
    qv = q.reshape(b, sub, dil * w)
    kv = k.reshape(b, sub, dil * w)
    vv = v.reshape(b, sub, dil * w)
    pres = pos_f.reshape(sub, dil).T
    pc = pres.reshape(dil, sub, 1)
    pr = pres.reshape(dil, sub // t, 1, t)
    kern = functools.partial(_dil_kernel, t=t, span=span, grp=grp)
    shp = jax.ShapeDtypeStruct((b, sub, dil * w), F32)
    o, lse = pl.pallas_call(
        kern,
        grid=(b, dil, sub // t),
        in_specs=[pl.BlockSpec(memory_space=pltpu.SMEM),
                  pl.BlockSpec((1, t, w), lambda i, r, j: (i, j, r)),
                  pl.BlockSpec((1, sub, w), lambda i, r, j: (i, 0, r)),
                  pl.BlockSpec((1, sub, w), lambda i, r, j: (i, 0, r)),
                  pl.BlockSpec((1, t, 1), lambda i, r, j: (r, j, 0)),
                  pl.BlockSpec((1, sub // t, 1, t), lambda i, r, j: (r, 0, 0, 0))],
        out_specs=[pl.BlockSpec((1, t, w), lambda i, r, j: (i, j, r)),
                   pl.BlockSpec((1, t, w), lambda i, r, j: (i, j, r))],
        out_shape=[shp, shp],
        compiler_params=_cparams(3),
        name="dilated_attn_g%d" % grp,
    )(slopes, qv, kv, vv, pc, pr)
    return o.reshape(b, s, w), lse.reshape(b, s, w)


def _out_odd_kernel(x_ref, mod_ref, oc_ref, os_ref, ow_ref, d0_ref, d1_ref, d2_ref,
                    l0_ref, l1_ref, l2_ref, gz_ref, w_ref, o_ref):
    gz = gz_ref[0].astype(F32)
    nsa = oc_ref[0] + os_ref[0] + ow_ref[0]
    m1 = (nsa * gz[:, 0:NSA_W]).astype(BF16)
    l0, l1, l2 = l0_ref[0], l1_ref[0], l2_ref[0]
    mx = jnp.maximum(jnp.maximum(l0, l1), l2)
    e0, e1, e2 = jnp.exp(l0 - mx), jnp.exp(l1 - mx), jnp.exp(l2 - mx)
    dil = (e0 * d0_ref[0] + e1 * d1_ref[0] + e2 * d2_ref[0]) / (e0 + e1 + e2)
    m2 = (dil * gz[:, NSA_W:NSA_W + DIL_W]).astype(BF16)
    y = _dot(m1, w_ref[0:NSA_W, :]) + _dot(m2, w_ref[NSA_W:NSA_W + DIL_W, :])
    gate = mod_ref[0][:, 2 * D_MODEL:3 * D_MODEL]
    o_ref[0] = x_ref[0] + gate * y


def _out_odd(x, mod3, oc, os_, ow, dils, lses, gz, w):
    b, s, d = x.shape
    ts = min(TS_PROJ, s)
    row = lambda n: pl.BlockSpec((1, ts, n), lambda i, j: (i, j, 0))
    return pl.pallas_call(
        _out_odd_kernel,
        grid=(b, s // ts),
        in_specs=[row(d), pl.BlockSpec((1, 1, 3 * d), lambda i, j: (i, 0, 0)),
                  row(NSA_W), row(NSA_W), row(NSA_W)] + [row(DIL_W)] * 6 + [row(1024), _full(w.shape)],
        out_specs=row(d),
        out_shape=jax.ShapeDtypeStruct((b, s, d), F32),
        compiler_params=_cparams(2),
        name="outproj_odd",
    )(x, mod3, oc, os_, ow, *dils, *lses, gz, w)


def _pad_cols(w, n):
    return jnp.pad(w, ((0, 0), (0, n - w.shape[1])))


def _pad_vec(v, n=D_MODEL):
    return jnp.pad(v, (0, n - v.shape[0]))


def _group_matrix(sizes, total):
    m = np.zeros((total, total), np.float32)
    off = 0
    for sz, on in sizes:
        if on:
            m[off:off + sz, off:off + sz] = 1.0
        off += sz
    return jnp.asarray(m, BF16)


def _swap_halves(w):
    h = w.shape[-1] // 2
    return jnp.concatenate([w[..., h:], w[..., :h]], axis=-1)


def _pack_even(w_in, norm_g, sb_qn, sb_kn, qa_g, wq_up, kva_g, wkv_up, qn, kn):
    d = w_in.shape[0]
    z = lambda n: jnp.zeros((d, n), w_in.dtype)
    kr = w_in[:, 2432:2464]
    w = jnp.concatenate([
        w_in[:, 0:1536], w_in[:, 1536:2048], w_in[:, 2464:2976], w_in[:, 2048:2304], w_in[:, 2304:2432],
        z(64), kr, z(32), z(64), _swap_halves(kr), z(32)], axis=1).astype(BF16)
    wq3 = wq_up.reshape(MLA_Q_RANK, MLA_HEADS, MLA_NOPE + MLA_ROPE)
    zq = jnp.zeros((MLA_Q_RANK, MLA_HEADS, 32), wq_up.dtype)
    wq = jnp.concatenate([wq3, zq], axis=-1).reshape(MLA_Q_RANK, MLA_HEADS * 128).astype(BF16)
    wqs = jnp.concatenate([jnp.zeros((MLA_Q_RANK, MLA_HEADS, 64), wq_up.dtype),
                           _swap_halves(wq3[..., MLA_NOPE:]), zq], axis=-1)
    wqs = wqs.reshape(MLA_Q_RANK, MLA_HEADS * 128).astype(BF16)
    wkv3 = wkv_up.reshape(MLA_KV_RANK, MLA_HEADS, MLA_NOPE + MLA_V)
    wk = jnp.concatenate([wkv3[..., :MLA_NOPE], jnp.zeros_like(wkv3[..., :MLA_NOPE])], axis=-1)
    wkv = jnp.concatenate([wk.reshape(MLA_KV_RANK, MLA_HEADS * 128),
                           wkv3[..., MLA_NOPE:].reshape(MLA_KV_RANK, MLA_HEADS * MLA_V)], axis=1).astype(BF16)
    z32 = jnp.zeros((32,), F32)
    z64 = jnp.zeros((64,), F32)
    scale = (MLA_NOPE + MLA_ROPE) ** -0.5
    qg = jnp.tile(jnp.concatenate([qn, z32]), MLA_HEADS) * scale
    qgs = jnp.tile(jnp.concatenate([z64, _swap_halves(qn[MLA_NOPE:]), z32]), MLA_HEADS) * scale
    kg = jnp.tile(jnp.concatenate([kn[:MLA_NOPE], z64]), MLA_HEADS)
    krg = jnp.concatenate([z64, kn[MLA_NOPE:], z32])
    krgs = jnp.concatenate([z64, _swap_halves(kn[MLA_NOPE:]), z32])
    cnt = jnp.tile(jnp.concatenate([jnp.full((64,), 1.0 / 64), jnp.full((32,), 1.0 / 32), jnp.ones((32,))]),
                   MLA_HEADS).astype(F32)
    rows = [norm_g, _pad_vec(jnp.tile(sb_qn, SB_HEADS) * HEAD_DIM ** -0.5), _pad_vec(jnp.tile(sb_kn, SB_HEADS)),
            _pad_vec(qa_g), _pad_vec(kva_g), qg, qgs, kg, _pad_vec(krg), _pad_vec(krgs), cnt]
    rows += [jnp.zeros((D_MODEL,), F32)] * (16 - len(rows))
    return w, wq, wqs, wkv, jnp.stack(rows).astype(F32)


def _pack_odd(w_in, norm_g, nsa_qn, nsa_kn, dil_qn, dil_kn):
    d = w_in.shape[0]

    def padded_heads(cols, n):
        c3 = cols.reshape(d, n, HEAD_DIM)
        return jnp.concatenate([c3, jnp.zeros_like(c3)], axis=-1).reshape(d, n * 128)

    def doubled_heads(cols, n):
        c3 = cols.reshape(d, n, HEAD_DIM)
        return jnp.concatenate([c3, c3], axis=-1).reshape(d, n * 128)

    ng = w_in[:, 1920:1956].reshape(d, NSA_KV_HEADS, NSA_HPG * 3)
    ng = jnp.pad(ng, ((0, 0), (0, 0), (0, 128 - NSA_HPG * 3))).reshape(d, NSA_KV_HEADS * 128)
    w_nsa = jnp.concatenate([
        padded_heads(w_in[:, 0:768], NSA_HEADS),
        padded_heads(w_in[:, 768:960], NSA_KV_HEADS), padded_heads(w_in[:, 960:1152], NSA_KV_HEADS),
        padded_heads(w_in[:, 1152:1344], NSA_KV_HEADS), doubled_heads(w_in[:, 1344:1536], NSA_KV_HEADS),
        padded_heads(w_in[:, 1536:1728], NSA_KV_HEADS), doubled_heads(w_in[:, 1728:1920], NSA_KV_HEADS),
        ng, w_in[:, 1956:2724], w_in[:, 5028:5284]], axis=1).astype(BF16)
    w_dil = w_in[:, 2724:5028].astype(BF16)
    z64 = jnp.zeros((64,), F32)
    n = NSA_HEADS * 128
    rows = [_pad_vec(norm_g, n),
            jnp.tile(jnp.concatenate([nsa_qn * HEAD_DIM ** -0.5, z64]), NSA_HEADS),
            _pad_vec(jnp.tile(jnp.concatenate([nsa_kn, z64]), NSA_KV_HEADS), n),
            _pad_vec(jnp.tile(dil_qn, DIL_HEADS) * HEAD_DIM ** -0.5, n), _pad_vec(jnp.tile(dil_kn, DIL_HEADS), n)]
    rows += [jnp.zeros((n,), F32)] * (8 - len(rows))
    return w_nsa, w_dil, jnp.stack(rows).astype(F32)


def _pack_compress(w, pe, double):
    w3 = w.reshape(CMP_LEN, HEAD_DIM, HEAD_DIM)
    w3 = jnp.concatenate([w3, jnp.zeros_like(w3)], axis=1)
    w3 = jnp.concatenate([w3, w3 if double else jnp.zeros_like(w3)], axis=2)
    wp = w3.reshape(2, CMP_STRIDE * 128, 128).astype(BF16)
    pe2 = jnp.concatenate([pe, jnp.zeros_like(pe)], axis=1).reshape(2, 1, CMP_STRIDE * 128)
    pe2 = jnp.broadcast_to(pe2, (2, 8, CMP_STRIDE * 128)).astype(F32)
    return wp, pe2


def _alibi_slopes(n):
    return 2.0 ** (-8.0 * jnp.arange(1, n + 1, dtype=jnp.float32) / n)


def kernel(x, c, positions, ada_w, ada_b, norm_g, ev_w_in, ev_w_out, sb_qn, sb_kn, mla_qa_g, mla_wq_up,
           mla_kva_g, mla_wkv_up, mla_qn, mla_kn, od_w_in, od_w_out, nsa_qn, nsa_kn, nsa_cmp_wk, nsa_cmp_wv,
           nsa_cmp_pe_k, nsa_cmp_pe_v, dil_qn, dil_kn):
    b, s, d = x.shape
    depth = ada_w.shape[0]
    pos_f = positions.astype(F32)

    inv_freq = ROPE_BASE ** (-jnp.arange(0, MLA_ROPE, 2, dtype=F32) / MLA_ROPE)
    ang = pos_f[:, None] * inv_freq[None, :]
    cos, sin = jnp.cos(ang), jnp.sin(ang)
    cos128 = jnp.concatenate([jnp.ones((s, 64), F32), cos, cos, jnp.zeros((s, 32), F32)], axis=1)
    sin128 = jnp.concatenate([jnp.zeros((s, 64), F32), -sin, sin, jnp.zeros((s, 32), F32)], axis=1)
    nsa_slopes = _alibi_slopes(NSA_HEADS)
    dil_slopes = _alibi_slopes(N_DIL * DIL_HEADS)
    nch = s // CMP_STRIDE
    chunk_sum = pos_f.reshape(nch, CMP_STRIDE).sum(axis=1)
    cpos = ((chunk_sum + jnp.roll(chunk_sum, -1)) / CMP_LEN).reshape(1, nch)
    n_sel = s // SEL_LEN
    cst = np.arange(nch)[:, None] * CMP_STRIDE
    jst = np.arange(LANES)[None, :] * SEL_LEN
    ovl = ((cst <= jst + SEL_LEN - 1) & (cst + CMP_LEN - 1 >= jst) & (np.arange(LANES)[None, :] < n_sel))
    ovl = jnp.asarray(ovl.astype(np.float32), BF16)
    pos_col = pos_f.reshape(s, 1)
    tkn = min(TK_NSA, min(TQ_NSA, s))
    pos2d = pos_f.reshape(s // tkn, 1, tkn)

    g64 = _group_matrix([(64, 1)] * 4, 256)
    gm = _group_matrix([(64, 1), (32, 1), (32, 0)] * 2, 256)
    tks = min(TK_SB, min(TQ_SB, s))
    tri = jnp.asarray(np.tril(np.ones((tks, tks), np.float32), -1), BF16)

    c8 = jnp.pad(c, ((0, 8 - b), (0, 0)))
    mod_all = _modulation(c8, ada_w, ada_b)

    for layer in range(depth):
        j = layer // 2
        mod3 = mod_all[layer, :b].reshape(b, 1, 3 * d)
        if layer % 2 == 0:
            w, wq, wqs, wkv, vec = _pack_even(ev_w_in[j], norm_g[layer], sb_qn[j], sb_kn[j], mla_qa_g[j],
                                              mla_wq_up[j], mla_kva_g[j], mla_wkv_up[j], mla_qn[j], mla_kn[j])
            sbq, sbk, sbv, gz, mq, mk, mv = _k1_even(x, mod3, vec, w, wq, wqs, wkv, g64, gm, cos128, sin128)
            o_sb = _sb_attention(sbq, sbk, sbv, tri)
            o_mla = _mla_attention(mq, mk, mv)
            x = _out_even(x, mod3, o_sb, o_mla, gz, ev_w_out[j].astype(BF16))
        else:
            w_nsa, w_dil, vec = _pack_odd(od_w_in[j], norm_g[layer], nsa_qn[j], nsa_kn[j], dil_qn[j], dil_kn[j])
            q, ck, cv, sk, sv, wk, wv, gates, gz = _k1_nsa(x, mod3, vec, w_nsa, g64)
            dq, dk, dv = _k1_dil(x, mod3, vec, w_dil, g64)
            wck, pek = _pack_compress(nsa_cmp_wk[j], nsa_cmp_pe_k[j], False)
            wcv, pev = _pack_compress(nsa_cmp_wv[j], nsa_cmp_pe_v[j], True)
            kgain = jnp.concatenate([nsa_kn[j], jnp.zeros((64,), F32)]).reshape(1, 128)
            kc, vc = _compress(ck, cv, wck, wcv, pek, pev, kgain, g64[0:128, 0:128])
            o_c, selb = _nsa_cmp(nsa_slopes, q, kc, vc, ovl, cpos, pos_col, gates)
            o_s = _nsa_gqa(nsa_slopes, q, selb, sk, sv, pos2d, gates, 1)
            o_w = _nsa_gqa(nsa_slopes, q, selb, wk, wv, pos2d, gates, 2)
            dils, lses = [], []
            for g in range(N_DIL):
                o, lse = _dilated(dil_slopes, dq[g], dk[g], dv[g], pos_f, g)
                dils.append(o)
                lses.append(lse)
            x = _out_odd(x, mod3, o_c, o_s, o_w, dils, lses, gz, od_w_out[j].astype(BF16))
    return x
```

```python
import functools

import numpy as np
import jax
import jax.numpy as jnp
from jax import lax
from jax.experimental import pallas as pl
from jax.experimental.pallas import tpu as pltpu

F32 = jnp.float32
BF16 = jnp.bfloat16

D_MODEL = 1024
HEAD_DIM = 64
NORM_EPS = 1e-6
TINY = 1e-30
SB_HEADS = 8
MLA_HEADS = 8
MLA_Q_RANK = 256
MLA_KV_RANK = 128
MLA_NOPE = 64
MLA_ROPE = 32
MLA_V = 64
ROPE_BASE = 10000.0
NSA_HEADS = 12
NSA_KV_HEADS = 3
NSA_HPG = 4
CMP_LEN = 32
CMP_STRIDE = 16
SEL_LEN = 64
SEL_TOPN = 16
WIN = 512
FORCE_BONUS = 1e3
DIL_CFG = ((128, 1), (512, 4), (2048, 16))
N_DIL = 3
DIL_HEADS = 4
SB_W = SB_HEADS * HEAD_DIM
MLA_OUT = MLA_HEADS * MLA_V
NSA_W = NSA_HEADS * HEAD_DIM
NSA_KV_W = NSA_KV_HEADS * HEAD_DIM
DIL_W = DIL_HEADS * HEAD_DIM

LANES = 128
MXU_DIM = 256
NEG_MASK = -1e30
LOG2E = 1.4426950408889634
SEL_OFF = 2.0 ** 30
VMEM_LIMIT = 56 * 1024 * 1024

TS_PROJ = 512
TQ_SB = 512
TK_SB = 256
TQ_MLA = 512
TK_MLA = 256
TQ_NSA = 256
TK_NSA = 256
T_DIL = 128


def _dot(a, b):
    return jnp.dot(a, b, preferred_element_type=F32)


def _dot_nt(a, b):
    return lax.dot_general(a, b, (((1,), (1,)), ((), ())), preferred_element_type=F32)


def _split_hl(a):
    hi = a.astype(BF16)
    lo = (a - hi.astype(F32)).astype(BF16)
    return hi, lo


def _dot_hl(a, b):
    hi, lo = _split_hl(a)
    return _dot(hi, b) + _dot(lo, b)


def _sigmoid(z):
    return 1.0 / (1.0 + jnp.exp(-z))


def _cparams(n_axes):
    return pltpu.CompilerParams(dimension_semantics=("arbitrary",) * n_axes,
                                vmem_limit_bytes=VMEM_LIMIT)


def _full(shape):
    n = len(shape)
    return pl.BlockSpec(shape, lambda *a, _n=n: (0,) * _n)


def _mod_kernel(c_ref, w_ref, b_ref, o_ref):
    c = c_ref[...]
    a = c * _sigmoid(c)
    ah, al = _split_hl(a)
    wh, wl = _split_hl(w_ref[0])
    o_ref[0] = _dot(ah, wh) + _dot(ah, wl) + _dot(al, wh) + b_ref[0]


def _modulation(c8, ada_w, ada_b):
    depth, d, n3 = ada_w.shape
    tn = 1024
    return pl.pallas_call(
        _mod_kernel,
        grid=(depth, n3 // tn),
        in_specs=[pl.BlockSpec((8, d), lambda l, j: (0, 0)),
                  pl.BlockSpec((1, d, tn), lambda l, j: (l, 0, j)),
                  pl.BlockSpec((1, 1, tn), lambda l, j: (l, 0, j))],
        out_specs=pl.BlockSpec((1, 8, tn), lambda l, j: (l, 0, j)),
        out_shape=jax.ShapeDtypeStruct((depth, 8, n3), F32),
        compiler_params=_cparams(2),
        name="adaln_mod",
    )(c8, ada_w, ada_b.reshape(depth, 1, n3))


def _modulated(x_ref, mod_ref, ng):
    x = x_ref[0]
    mod = mod_ref[0]
    shift = mod[:, 0:D_MODEL]
    scale = mod[:, D_MODEL:2 * D_MODEL]
    ms = jnp.mean(x * x, axis=-1, keepdims=True)
    h = x * lax.rsqrt(ms + NORM_EPS) * ng
    h = h * (1.0 + scale) + shift
    return h.astype(BF16)


def _group_norm_chunk(t, g, inv_cnt):
    ss = _dot_hl(t * t, g)
    return t * lax.rsqrt(ss * inv_cnt + NORM_EPS)


def _row_rms(t, gain):
    ms = jnp.mean(t * t, axis=-1, keepdims=True)
    return t * lax.rsqrt(ms + NORM_EPS) * gain


EV_COLS = 3200


def _k1_even_kernel(x_ref, mod_ref, vec_ref, w_ref, wq_ref, wqs_ref, wkv_ref, g64_ref, gm_ref,
                    cos_ref, sin_ref, sbq_o, sbk_o, sbv_o, gz_o, mq_o, mk_o, mv_o):
    hb = _modulated(x_ref, mod_ref, vec_ref[0:1, :])
    g64 = g64_ref[...]
    gm = gm_ref[...]
    inv64 = 1.0 / HEAD_DIM

    u = _dot(hb, w_ref[:, 0:1536])
    for c in range(2):
        sl = slice(c * 256, (c + 1) * 256)
        qn = _group_norm_chunk(u[:, c * 256:(c + 1) * 256], g64, inv64)
        sbq_o[0, :, sl] = (qn * vec_ref[1:2, sl]).astype(BF16)
        kn = _group_norm_chunk(u[:, 512 + c * 256:512 + (c + 1) * 256], g64, inv64)
        sbk_o[0, :, sl] = (kn * vec_ref[2:3, sl]).astype(BF16)
    sbv_o[0] = u[:, 1024:1536].astype(BF16)

    uz = _dot(hb, w_ref[:, 1536:2560])
    gz_o[0] = (uz * _sigmoid(uz)).astype(BF16)

    ul = _dot(hb, w_ref[:, 2560:3200])
    qlat = _row_rms(ul[:, 0:256], vec_ref[3:4, 0:256]).astype(BF16)
    kvlat = _row_rms(ul[:, 256:384], vec_ref[4:5, 0:128]).astype(BF16)
    krm = ul[:, 384:512]
    krs = ul[:, 512:640]
    cs = cos_ref[...]
    sn = sin_ref[...]
    cs2 = jnp.concatenate([cs, cs], axis=1)
    sn2 = jnp.concatenate([sn, sn], axis=1)

    tq = _dot(qlat, wq_ref[...])
    tqs = _dot(qlat, wqs_ref[...])
    for p in range(4):
        sl = slice(p * 256, (p + 1) * 256)
        tc = tq[:, p * 256:(p + 1) * 256]
        ss = _dot_hl(tc * tc, gm)
        inv = lax.rsqrt(ss * vec_ref[10:11, sl] + NORM_EPS)
        a = tc * inv * vec_ref[5:6, sl]
        b = tqs[:, p * 256:(p + 1) * 256] * inv * vec_ref[6:7, sl]
        mq_o[0, :, sl] = (a * cs2 + b * sn2).astype(BF16)

    sskr = _dot_hl(krm * krm, gm[0:128, 0:128])
    invr = lax.rsqrt(sskr * vec_ref[10:11, 0:128] + NORM_EPS)
    kr = (krm * invr * vec_ref[8:9, 0:128]) * cs + (krs * invr * vec_ref[9:10, 0:128]) * sn
    kr2 = jnp.concatenate([kr, kr], axis=1)

    kv = _dot(kvlat, wkv_ref[...])
    for p in range(4):
        sl = slice(p * 256, (p + 1) * 256)
        kn = _group_norm_chunk(kv[:, p * 256:(p + 1) * 256], g64, inv64)
        mk_o[0, :, sl] = (kn * vec_ref[7:8, sl] + kr2).astype(BF16)
    mv_o[0] = kv[:, 1024:1536].astype(BF16)


def _k1_even(x, mod3, vec, w, wq, wqs, wkv, g64, gm, cos128, sin128):
    b, s, d = x.shape
    ts = min(TS_PROJ, s)
    row = lambda n: pl.BlockSpec((1, ts, n), lambda i, j: (i, j, 0))
    outs = [(512, BF16), (512, BF16), (512, BF16), (1024, BF16), (1024, BF16), (1024, BF16), (512, BF16)]
    return pl.pallas_call(
        _k1_even_kernel,
        grid=(b, s // ts),
        in_specs=[row(d),
                  pl.BlockSpec((1, 1, 3 * d), lambda i, j: (i, 0, 0)),
                  _full(vec.shape), _full(w.shape), _full(wq.shape), _full(wqs.shape), _full(wkv.shape),
                  _full(g64.shape), _full(gm.shape),
                  pl.BlockSpec((ts, 128), lambda i, j: (j, 0)),
                  pl.BlockSpec((ts, 128), lambda i, j: (j, 0))],
        out_specs=[row(n) for n, _ in outs],
        out_shape=[jax.ShapeDtypeStruct((b, s, n), dt) for n, dt in outs],
        compiler_params=_cparams(2),
        name="inproj_even",
    )(x, mod3, vec, w, wq, wqs, wkv, g64, gm, cos128, sin128)


def _sb_kernel(q_ref, k_ref, v_ref, tri_ref, o_ref, acc_ref, car_ref, *, tq, tk):
    qi = pl.program_id(2)
    q = q_ref[0]
    lane = lax.broadcasted_iota(jnp.int32, (1, LANES), 1)
    lo_half = lane < HEAD_DIM
    zero = jnp.zeros_like(q)
    qs = (jnp.where(lo_half, q, zero), jnp.where(lo_half, zero, q))
    tri = tri_ref[...]
    acc_ref[...] = jnp.zeros_like(acc_ref)
    car_ref[...] = jnp.zeros_like(car_ref)
    rows = qi * tq + lax.broadcasted_iota(jnp.int32, (tq, 1), 0)
    col0 = lax.broadcasted_iota(jnp.int32, (1, tk), 1)

    def tile(kt, masked):
        off = pl.multiple_of(kt * tk, tk)
        k = k_ref[0, pl.ds(off, tk), :]
        v = v_ref[0, pl.ds(off, tk), :]
        if masked:
            strict = (kt * tk + col0) < rows
        for i in range(2):
            nz = _dot_nt(qs[i], k)
            soft = jnp.log(1.0 + jnp.exp2(-jnp.abs(nz))) * LOG2E
            lom = jnp.minimum(nz, 0.0) - soft
            lb = lom - nz
            if masked:
                lom = jnp.where(strict, lom, 0.0)
            tt = _dot(lom.astype(BF16), tri)
            car = car_ref[i]
            w = jnp.exp2(lb + tt[:, 0:tk] + jnp.concatenate([car] * (tk // LANES), axis=1))
            if masked:
                w = jnp.where(strict, w, 0.0)
            acc_ref[i] += _dot(w.astype(BF16), v)
            car_ref[i] = car + tt[:, tk:tk + LANES]

    per = tq // tk
    for dd in range(per):
        tile((qi + 1) * per - 1 - dd, True)

    def body(j, carry):
        tile(qi * per - 1 - j, False)
        return carry

    lax.fori_loop(0, qi * per, body, 0)
    o_ref[0] = jnp.where(lo_half, acc_ref[0], acc_ref[1])


def _sb_attention(q, k, v, tri):
    b, s, w = q.shape
    tq = min(TQ_SB, s)
    tk = min(TK_SB, tq)
    kern = functools.partial(_sb_kernel, tq=tq, tk=tk)
    return pl.pallas_call(
        kern,
        grid=(b, w // LANES, s // tq),
        in_specs=[pl.BlockSpec((1, tq, LANES), lambda i, p, j: (i, j, p)),
                  pl.BlockSpec((1, s, LANES), lambda i, p, j: (i, 0, p)),
                  pl.BlockSpec((1, s, LANES), lambda i, p, j: (i, 0, p)),
                  _full(tri.shape)],
        out_specs=pl.BlockSpec((1, tq, LANES), lambda i, p, j: (i, j, p)),
        out_shape=jax.ShapeDtypeStruct((b, s, w), F32),
        scratch_shapes=[pltpu.VMEM((2, tq, LANES), F32), pltpu.VMEM((2, tq, LANES), F32)],
        compiler_params=_cparams(3),
        name="stickbreak_attn",
    )(q, k, v, tri)


def _lane_max(s):
    m = s[:, 0:LANES]
    for c in range(1, s.shape[1] // LANES):
        m = jnp.maximum(m, s[:, c * LANES:(c + 1) * LANES])
    return m


def _mla_kernel(q_ref, k_ref, v_ref, o_ref, acc_ref, m_ref, *, tq, tk):
    qi = pl.program_id(2)
    q = q_ref[0]
    qs = (q[:, 0:LANES], q[:, LANES:2 * LANES])
    lane = lax.broadcasted_iota(jnp.int32, (1, LANES), 1)
    lo_half = lane < HEAD_DIM
    acc_ref[...] = jnp.zeros_like(acc_ref)
    m_ref[...] = jnp.full_like(m_ref, NEG_MASK)
    rows = qi * tq + lax.broadcasted_iota(jnp.int32, (tq, 1), 0)
    col0 = lax.broadcasted_iota(jnp.int32, (1, tk), 1)
    ones = jnp.ones((tk, LANES), BF16)
    per = tq // tk
    nrep = tk // LANES

    def scores(kt, masked):
        off = pl.multiple_of(kt * tk, tk)
        k = k_ref[0, pl.ds(off, tk), :]
        out = []
        for i in range(2):
            s = _dot_nt(qs[i], k[:, i * LANES:(i + 1) * LANES])
            if masked:
                s = jnp.where((kt * tk + col0) <= rows, s, NEG_MASK)
            out.append(s)
        return out

    def sweep(fn):
        def body(j, carry):
            fn(j, False)
            return carry
        lax.fori_loop(0, qi * per, body, 0)
        for dd in range(per):
            fn(qi * per + dd, True)

    def row_max(kt, masked):
        for i, s in enumerate(scores(kt, masked)):
            m_ref[i] = jnp.maximum(m_ref[i], _lane_max(s))

    def accumulate(kt, masked):
        off = pl.multiple_of(kt * tk, tk)
        vx = jnp.concatenate([v_ref[0, pl.ds(off, tk), :], ones], axis=1)
        for i, s in enumerate(scores(kt, masked)):
            p = jnp.exp2(s - jnp.concatenate([m_ref[i]] * nrep, axis=1))
            acc_ref[i] += _dot(p.astype(BF16), vx)

    sweep(row_max)
    for i in range(2):
        m_ref[i] = jnp.broadcast_to(jnp.max(m_ref[i], axis=-1, keepdims=True), (tq, LANES))
    sweep(accumulate)
    o_ref[0] = jnp.where(lo_half, acc_ref[0, :, 0:LANES] / acc_ref[0, :, LANES:2 * LANES],
                         acc_ref[1, :, 0:LANES] / acc_ref[1, :, LANES:2 * LANES])


def _mla_attention(q, k, v):
    b, s, _ = q.shape
    tq = min(TQ_MLA, s)
    tk = min(TK_MLA, tq)
    kern = functools.partial(_mla_kernel, tq=tq, tk=tk)
    return pl.pallas_call(
        kern,
        grid=(b, MLA_HEADS // 2, s // tq),
        in_specs=[pl.BlockSpec((1, tq, 2 * LANES), lambda i, p, j: (i, j, p)),
                  pl.BlockSpec((1, s, 2 * LANES), lambda i, p, j: (i, 0, p)),
                  pl.BlockSpec((1, s, LANES), lambda i, p, j: (i, 0, p))],
        out_specs=pl.BlockSpec((1, tq, LANES), lambda i, p, j: (i, j, p)),
        out_shape=jax.ShapeDtypeStruct((b, s, MLA_OUT), F32),
        scratch_shapes=[pltpu.VMEM((2, tq, 2 * LANES), F32), pltpu.VMEM((2, tq, LANES), F32)],
        compiler_params=_cparams(3),
        name="mla_attn",
    )(q, k, v)


def _out_even_kernel(x_ref, mod_ref, osb_ref, omla_ref, gz_ref, w_ref, o_ref):
    gz = gz_ref[0].astype(F32)
    m1 = (osb_ref[0] * gz[:, 0:SB_W]).astype(BF16)
    m2 = (omla_ref[0] * gz[:, SB_W:SB_W + MLA_OUT]).astype(BF16)
    y = _dot(m1, w_ref[0:SB_W, :]) + _dot(m2, w_ref[SB_W:SB_W + MLA_OUT, :])
    gate = mod_ref[0][:, 2 * D_MODEL:3 * D_MODEL]
    o_ref[0] = x_ref[0] + gate * y


def _out_even(x, mod3, osb, omla, gz, w):
    b, s, d = x.shape
    ts = min(TS_PROJ, s)
    row = lambda n: pl.BlockSpec((1, ts, n), lambda i, j: (i, j, 0))
    return pl.pallas_call(
        _out_even_kernel,
        grid=(b, s // ts),
        in_specs=[row(d), pl.BlockSpec((1, 1, 3 * d), lambda i, j: (i, 0, 0)),
                  row(SB_W), row(MLA_OUT), row(SB_W + MLA_OUT), _full(w.shape)],
        out_specs=row(d),
        out_shape=jax.ShapeDtypeStruct((b, s, d), F32),
        compiler_params=_cparams(2),
        name="outproj_even",
    )(x, mod3, osb, omla, gz, w)


NSA_COLS = 5248


def _k1_nsa_kernel(x_ref, mod_ref, vec_ref, w_ref, g64_ref,
                   q_o, ck_o, cv_o, sk_o, sv_o, wk_o, wv_o, gt_o, gz_o):
    hb = _modulated(x_ref, mod_ref, vec_ref[0:1, 0:D_MODEL])
    g64 = g64_ref[...]
    inv64 = 1.0 / HEAD_DIM

    uq = _dot(hb, w_ref[:, 0:1536])
    for c in range(6):
        sl = slice(c * 256, (c + 1) * 256)
        qn = _group_norm_chunk(uq[:, c * 256:(c + 1) * 256], g64, inv64)
        q_o[0, :, sl] = (qn * vec_ref[1:2, sl]).astype(BF16)

    uc = _dot(hb, w_ref[:, 1536:2304])
    for g in range(NSA_KV_HEADS):
        ck_o[0, g] = uc[:, g * 128:(g + 1) * 128].astype(BF16)
        cv_o[0, g] = uc[:, 384 + g * 128:384 + (g + 1) * 128].astype(BF16)

    us = _dot(hb, w_ref[:, 2304:3840])
    kgain = vec_ref[2:3, 0:384]
    g128 = g64[0:128, 0:128]
    for g in range(NSA_KV_HEADS):
        sl = slice(g * 128, (g + 1) * 128)
        t = us[:, g * 128:(g + 1) * 128]
        ss = _dot_hl(t * t, g128)
        sk_o[0, :, sl] = (t * lax.rsqrt(ss * inv64 + NORM_EPS) * kgain[:, sl]).astype(BF16)
        t = us[:, 768 + g * 128:768 + (g + 1) * 128]
        ss = _dot_hl(t * t, g128)
        wk_o[0, :, sl] = (t * lax.rsqrt(ss * inv64 + NORM_EPS) * kgain[:, sl]).astype(BF16)
    ones_hi = (lax.broadcasted_iota(jnp.int32, (1, 384), 1) % 128) >= HEAD_DIM
    sv_o[0] = jnp.where(ones_hi, 1.0, us[:, 384:768]).astype(BF16)
    wv_o[0] = jnp.where(ones_hi, 1.0, us[:, 1152:1536]).astype(BF16)

    ug = _dot(hb, w_ref[:, 3840:4224])
    gt_o[0] = _sigmoid(ug)
    uz = _dot(hb, w_ref[:, 4224:5248])
    gz_o[0] = (uz * _sigmoid(uz)).astype(BF16)


def _k1_nsa(x, mod3, vec, w, g64):
    b, s, d = x.shape
    ts = min(TS_PROJ, s)
    row = lambda n: pl.BlockSpec((1, ts, n), lambda i, j: (i, j, 0))
    grp = pl.BlockSpec((1, NSA_KV_HEADS, ts, 128), lambda i, j: (i, 0, j, 0))
    return pl.pallas_call(
        _k1_nsa_kernel,
        grid=(b, s // ts),
        in_specs=[row(d), pl.BlockSpec((1, 1, 3 * d), lambda i, j: (i, 0, 0)),
                  _full(vec.shape), _full(w.shape), _full(g64.shape)],
        out_specs=[row(1536), grp, grp, row(384), row(384), row(384), row(384), row(384), row(1024)],
        out_shape=[jax.ShapeDtypeStruct((b, s, 1536), BF16),
                   jax.ShapeDtypeStruct((b, NSA_KV_HEADS, s, 128), BF16),
                   jax.ShapeDtypeStruct((b, NSA_KV_HEADS, s, 128), BF16),
                   jax.ShapeDtypeStruct((b, s, 384), BF16),
                   jax.ShapeDtypeStruct((b, s, 384), BF16),
                   jax.ShapeDtypeStruct((b, s, 384), BF16),
                   jax.ShapeDtypeStruct((b, s, 384), BF16),
                   jax.ShapeDtypeStruct((b, s, 384), F32),
                   jax.ShapeDtypeStruct((b, s, 1024), BF16)],
        compiler_params=_cparams(2),
        name="inproj_nsa",
    )(x, mod3, vec, w, g64)


def _k1_dil_kernel(x_ref, mod_ref, vec_ref, w_ref, g64_ref, q_o, k_o, v_o):
    hb = _modulated(x_ref, mod_ref, vec_ref[0:1, 0:D_MODEL])
    g64 = g64_ref[...]
    inv64 = 1.0 / HEAD_DIM
    u = _dot(hb, w_ref[...])
    for g in range(N_DIL):
        qn = _group_norm_chunk(u[:, g * 256:(g + 1) * 256], g64, inv64)
        q_o[g, 0] = (qn * vec_ref[3:4, 0:256]).astype(BF16)
        kn = _group_norm_chunk(u[:, 768 + g * 256:768 + (g + 1) * 256], g64, inv64)
        k_o[g, 0] = (kn * vec_ref[4:5, 0:256]).astype(BF16)
        v_o[g, 0] = u[:, 1536 + g * 256:1536 + (g + 1) * 256].astype(BF16)


def _k1_dil(x, mod3, vec, w, g64):
    b, s, d = x.shape
    ts = min(TS_PROJ, s)
    row = lambda n: pl.BlockSpec((1, ts, n), lambda i, j: (i, j, 0))
    grp = pl.BlockSpec((N_DIL, 1, ts, 256), lambda i, j: (0, i, j, 0))
    shp = jax.ShapeDtypeStruct((N_DIL, b, s, 256), BF16)
    return pl.pallas_call(
        _k1_dil_kernel,
        grid=(b, s // ts),
        in_specs=[row(d), pl.BlockSpec((1, 1, 3 * d), lambda i, j: (i, 0, 0)),
                  _full(vec.shape), _full(w.shape), _full(g64.shape)],
        out_specs=[grp, grp, grp],
        out_shape=[shp, shp, shp],
        compiler_params=_cparams(2),
        name="inproj_dil",
    )(x, mod3, vec, w, g64)


def _compress_kernel(xk_ref, xv_ref, wk_ref, wv_ref, pek_ref, pev_ref, gain_ref, g128_ref, kc_o, vc_o):
    nch = xk_ref.shape[2]

    def comp(x_ref, w_ref, pe_ref):
        x = x_ref[0, 0]
        lo = _dot(x, w_ref[0])
        hi = _dot(x, w_ref[1])
        pec = _dot_hl(pe_ref[0], w_ref[0]) + _dot_hl(pe_ref[1], w_ref[1])
        return lo + pltpu.roll(hi, nch - 1, 0) + pec[0:1, :]

    kc = comp(xk_ref, wk_ref, pek_ref)
    ss = _dot_hl(kc * kc, g128_ref[...])
    kc_o[0, 0] = (kc * lax.rsqrt(ss * (1.0 / HEAD_DIM) + NORM_EPS) * gain_ref[...]).astype(BF16)
    vc_o[0, 0] = comp(xv_ref, wv_ref, pev_ref).astype(BF16)


def _compress(ck, cv, wk, wv, pek, pev, gain, g128):
    b, g, s, _ = ck.shape
    nch = s // CMP_STRIDE
    xk = ck.reshape(b, g, nch, CMP_STRIDE * 128)
    xv = cv.reshape(b, g, nch, CMP_STRIDE * 128)
    blk = pl.BlockSpec((1, 1, nch, CMP_STRIDE * 128), lambda i, j: (i, j, 0, 0))
    oblk = pl.BlockSpec((1, 1, nch, 128), lambda i, j: (i, j, 0, 0))
    shp = jax.ShapeDtypeStruct((b, g, nch, 128), BF16)
    return pl.pallas_call(
        _compress_kernel,
        grid=(b, g),
        in_specs=[blk, blk, _full(wk.shape), _full(wv.shape), _full(pek.shape), _full(pev.shape),
                  _full(gain.shape), _full(g128.shape)],
        out_specs=[oblk, oblk],
        out_shape=[shp, shp],
        compiler_params=_cparams(2),
        name="nsa_compress",
    )(xk, xv, wk, wv, pek, pev, gain, g128)


def _pair_select(lo_half, a, b):
    return jnp.where(lo_half, a, b)


def _nsa_cmp_kernel(slope_ref, q_ref, kc_ref, vc_ref, ovl_ref, cpos_ref, pq_ref, gt_ref,
                    oc_o, selb_o, *, tq):
    g = pl.program_id(1)
    qi = pl.program_id(2)
    q4 = q_ref[0]
    kc = kc_ref[0, 0]
    vc = vc_ref[0, 0]
    nch = kc.shape[0]
    lane = lax.broadcasted_iota(jnp.int32, (1, LANES), 1)
    lo_half = lane < HEAD_DIM
    t = qi * tq + lax.broadcasted_iota(jnp.int32, (tq, 1), 0)
    n_id = lax.broadcasted_iota(jnp.int32, (1, nch), 1)
    valid = (n_id * CMP_STRIDE + (CMP_LEN - 1)) <= t
    dist = (pq_ref[...] - cpos_ref[...]) * LOG2E
    gt = gt_ref[0]
    psum = jnp.zeros((tq, nch), F32)
    outs = []
    for hh in range(NSA_HPG):
        s = _dot_nt(q4[:, hh * LANES:(hh + 1) * LANES], kc) - slope_ref[g * NSA_HPG + hh] * dist
        s = jnp.where(valid, s, -jnp.inf)
        mx = jnp.max(s, axis=-1, keepdims=True)
        mx = jnp.where(mx == -jnp.inf, 0.0, mx)
        e = jnp.exp2(s - mx)
        den = jnp.maximum(jnp.sum(e, axis=-1, keepdims=True), TINY)
        p = e / den
        psum = psum + p
        outs.append(_dot(p.astype(BF16), vc) * gt[:, 3 * hh:3 * hh + 1])
    oc_o[0, :, 0:LANES] = _pair_select(lo_half, outs[0], outs[1])
    oc_o[0, :, LANES:2 * LANES] = _pair_select(lo_half, outs[2], outs[3])

    imp = _dot_hl(psum, ovl_ref[...])
    cur = t >> 6
    forced = (lane == 0) | (lane == cur) | (lane == cur - 1)
    allowed = lane <= cur
    score = jnp.where(allowed, imp + jnp.where(forced, FORCE_BONUS, 0.0), -jnp.inf)

    def pick(_, carry):
        sc, sel = carry
        idx = jnp.argmax(sc, axis=-1, keepdims=True).astype(jnp.int32)
        hit = lane == idx
        return jnp.where(hit, -jnp.inf, sc), jnp.where(hit, 1.0, sel)

    _, sel = lax.fori_loop(0, SEL_TOPN, pick, (score, jnp.zeros((tq, LANES), F32)))
    chosen = (sel > 0.5) & allowed
    selb_o[0, 0] = jnp.where(chosen, 0.0, -SEL_OFF).astype(BF16)


def _nsa_cmp(slopes, q, kc, vc, ovl, cpos, pos_col, gates):
    b, s, _ = q.shape
    nch = kc.shape[2]
    tq = min(TQ_NSA, s)
    kern = functools.partial(_nsa_cmp_kernel, tq=tq)
    return pl.pallas_call(
        kern,
        grid=(b, NSA_KV_HEADS, s // tq),
        in_specs=[pl.BlockSpec(memory_space=pltpu.SMEM),
                  pl.BlockSpec((1, tq, 4 * LANES), lambda i, g, j: (i, j, g)),
                  pl.BlockSpec((1, 1, nch, LANES), lambda i, g, j: (i, g, 0, 0)),
                  pl.BlockSpec((1, 1, nch, LANES), lambda i, g, j: (i, g, 0, 0)),
                  _full(ovl.shape), _full(cpos.shape),
                  pl.BlockSpec((tq, 1), lambda i, g, j: (j, 0)),
                  pl.BlockSpec((1, tq, LANES), lambda i, g, j: (i, j, g))],
        out_specs=[pl.BlockSpec((1, tq, 2 * LANES), lambda i, g, j: (i, j, g)),
                   pl.BlockSpec((1, 1, tq, LANES), lambda i, g, j: (i, g, j, 0))],
        out_shape=[jax.ShapeDtypeStruct((b, s, NSA_W), F32),
                   jax.ShapeDtypeStruct((b, NSA_KV_HEADS, s, LANES), BF16)],
        compiler_params=_cparams(3),
        name="nsa_cmp_topk",
    )(slopes, q, kc, vc, ovl, cpos, pos_col, gates)


def _gqa_kernel(slope_ref, q_ref, selb_ref, k_ref, v_ref, pos_ref, gt_ref, o_ref,
                qa_ref, acc_ref, m_ref, *, tq, tk, branch):
    g = pl.program_id(1)
    qi = pl.program_id(2)
    q4 = q_ref[0]
    lane = lax.broadcasted_iota(jnp.int32, (1, LANES), 1)
    lo_half = lane < HEAD_DIM
    for hh in range(NSA_HPG):
        qa_ref[hh * tq:(hh + 1) * tq, 0:LANES] = q4[:, hh * LANES:(hh + 1) * LANES]
        if branch == 1:
            qa_ref[hh * tq:(hh + 1) * tq, LANES:2 * LANES] = selb_ref[0, 0]
    acc_ref[...] = jnp.zeros_like(acc_ref)
    m_ref[...] = jnp.full_like(m_ref, NEG_MASK)
    rows = qi * tq + lax.broadcasted_iota(jnp.int32, (tq, 1), 0)
    col0 = lax.broadcasted_iota(jnp.int32, (1, tk), 1)
    krow = lax.broadcasted_iota(jnp.int32, (tk, 1), 0)
    pref = pos_ref[qi * (tq // tk)][:, 0:1]
    per = tq // tk
    nrep = tk // LANES

    def scores(kt_true, masked):
        kt = jnp.maximum(kt_true, 0)
        off = pl.multiple_of(kt * tk, tk)
        k = k_ref[0, pl.ds(off, tk), :]
        if branch == 1:
            blk = (kt * tk + krow) >> 6
            onehot = jnp.where(lane == blk, 1.0, 0.0).astype(BF16)
            s_all = _dot_nt(qa_ref[...], jnp.concatenate([k, onehot], axis=1))
        else:
            s_all = _dot_nt(qa_ref[:, 0:LANES], k)
        rel = (pos_ref[kt] - pref) * LOG2E
        if masked:
            cols = kt_true * tk + col0
            d = rows - cols
            keep = d >= 0
            if branch == 2:
                keep = keep & (d < WIN) & (cols >= 0)
        out = []
        for hh in range(NSA_HPG):
            s = s_all[hh * tq:(hh + 1) * tq] + slope_ref[g * NSA_HPG + hh] * rel
            if masked:
                s = jnp.where(keep, s, NEG_MASK)
            out.append(s)
        return out

    def sweep(fn):
        if branch == 1:
            def body(j, carry):
                fn(j, False)
                return carry
            lax.fori_loop(0, qi * per, body, 0)
            for dd in range(per):
                fn(qi * per + dd, True)
        else:
            for dd in range(per + WIN // tk):
                fn((qi + 1) * per - 1 - dd, True)

    def row_max(kt_true, masked):
        for hh, s in enumerate(scores(kt_true, masked)):
            sl = slice(hh * tq, (hh + 1) * tq)
            m_ref[sl] = jnp.maximum(m_ref[sl], _lane_max(s))

    def accumulate(kt_true, masked):
        off = pl.multiple_of(jnp.maximum(kt_true, 0) * tk, tk)
        v = v_ref[0, pl.ds(off, tk), :]
        for hh, s in enumerate(scores(kt_true, masked)):
            sl = slice(hh * tq, (hh + 1) * tq)
            p = jnp.exp2(s - jnp.concatenate([m_ref[sl]] * nrep, axis=1))
            acc_ref[sl] += _dot(p.astype(BF16), v)

    sweep(row_max)
    for hh in range(NSA_HPG):
        sl = slice(hh * tq, (hh + 1) * tq)
        m_ref[sl] = jnp.broadcast_to(jnp.max(m_ref[sl], axis=-1, keepdims=True), (tq, LANES))
    sweep(accumulate)

    gt = gt_ref[0]
    for pr in range(2):
        res = []
        for x in range(2):
            hh = 2 * pr + x
            a = acc_ref[hh * tq:(hh + 1) * tq]
            r = pltpu.roll(a, HEAD_DIM, 1)
            o = a / r if x == 0 else r / a
            res.append(o * gt[:, 3 * hh + branch:3 * hh + branch + 1])
        o_ref[0, :, pr * LANES:(pr + 1) * LANES] = _pair_select(lo_half, res[0], res[1])


def _nsa_gqa(slopes, q, selb, k, v, pos2d, gates, branch):
    b, s, _ = q.shape
    tq = min(TQ_NSA, s)
    tk = min(TK_NSA, tq)
    kern = functools.partial(_gqa_kernel, tq=tq, tk=tk, branch=branch)
    return pl.pallas_call(
        kern,
        grid=(b, NSA_KV_HEADS, s // tq),
        in_specs=[pl.BlockSpec(memory_space=pltpu.SMEM),
                  pl.BlockSpec((1, tq, 4 * LANES), lambda i, g, j: (i, j, g)),
                  pl.BlockSpec((1, 1, tq, LANES), lambda i, g, j: (i, g, j, 0)),
                  pl.BlockSpec((1, s, LANES), lambda i, g, j: (i, 0, g)),
                  pl.BlockSpec((1, s, LANES), lambda i, g, j: (i, 0, g)),
                  _full(pos2d.shape),
                  pl.BlockSpec((1, tq, LANES), lambda i, g, j: (i, j, g))],
        out_specs=pl.BlockSpec((1, tq, 2 * LANES), lambda i, g, j: (i, j, g)),
        out_shape=jax.ShapeDtypeStruct((b, s, NSA_W), F32),
        scratch_shapes=[pltpu.VMEM((NSA_HPG * tq, 2 * LANES), BF16),
                        pltpu.VMEM((NSA_HPG * tq, LANES), F32),
                        pltpu.VMEM((NSA_HPG * tq, LANES), F32)],
        compiler_params=_cparams(3),
        name="nsa_sel_attn" if branch == 1 else "nsa_win_attn",
    )(slopes, q, selb, k, v, pos2d, gates)


def _dil_kernel(slope_ref, q_ref, k_ref, v_ref, pc_ref, pr_ref, o_ref, lse_ref, *, t, span, grp):
    i = pl.program_id(2)
    q4 = q_ref[0]
    lane = lax.broadcasted_iota(jnp.int32, (1, LANES), 1)
    lo_half = lane < HEAD_DIM
    prev = jnp.maximum(i - 1, 0)
    o_prev = pl.multiple_of(prev * t, t)
    o_cur = pl.multiple_of(i * t, t)
    kk = jnp.concatenate([k_ref[0, pl.ds(o_prev, t), :], k_ref[0, pl.ds(o_cur, t), :]], axis=0)
    vv = jnp.concatenate([v_ref[0, pl.ds(o_prev, t), :], v_ref[0, pl.ds(o_cur, t), :]], axis=0)
    pk = jnp.concatenate([pr_ref[0, prev], pr_ref[0, i]], axis=1)
    dist = pc_ref[0] - pk
    rows = i * t + lax.broadcasted_iota(jnp.int32, (t, 1), 0)
    cols = (i - 1) * t + lax.broadcasted_iota(jnp.int32, (1, 2 * t), 1)
    d = rows - cols
    valid = (cols >= 0) & (d >= 0) & (d <= span)
    zero = jnp.zeros((t, LANES), BF16)
    for p in range(2):
        qp = q4[:, p * LANES:(p + 1) * LANES]
        kp = kk[:, p * LANES:(p + 1) * LANES]
        vp = vv[:, p * LANES:(p + 1) * LANES]
        res = []
        for x in range(2):
            qx = jnp.where(lo_half, qp, zero) if x == 0 else jnp.where(lo_half, zero, qp)
            s = _dot_nt(qx, kp) - slope_ref[grp * DIL_HEADS + 2 * p + x] * dist
            s = jnp.where(valid, s, -jnp.inf)
            mx = jnp.max(s, axis=-1, keepdims=True)
            e = jnp.exp(s - mx)
            den = jnp.maximum(jnp.sum(e, axis=-1, keepdims=True), TINY)
            res.append((_dot((e / den).astype(BF16), vp), mx + jnp.log(den)))
        sl = slice(p * LANES, (p + 1) * LANES)
        o_ref[0, :, sl] = _pair_select(lo_half, res[0][0], res[1][0])
        lse_ref[0, :, sl] = jnp.where(lo_half, res[0][1], res[1][1])


def _dilated(slopes, q, k, v, pos_f, grp):
    window, dil = DIL_CFG[grp]
    b, s, w = q.shape
    sub = s // dil
    t = min(T_DIL, sub)
    span = window // dil
    assert span <= t
    qv = q.reshape(b, sub, dil * w)
    kv = k.reshape(b, sub, dil * w)
    vv = v.reshape(b, sub, dil * w)
    pres = pos_f.reshape(sub, dil).T
    pc = pres.reshape(dil, sub, 1)
    pr = pres.reshape(dil, sub // t, 1, t)
    kern = functools.partial(_dil_kernel, t=t, span=span, grp=grp)
    shp = jax.ShapeDtypeStruct((b, sub, dil * w), F32)
    o, lse = pl.pallas_call(
        kern,
        grid=(b, dil, sub // t),
        in_specs=[pl.BlockSpec(memory_space=pltpu.SMEM),
                  pl.BlockSpec((1, t, w), lambda i, r, j: (i, j, r)),
                  pl.BlockSpec((1, sub, w), lambda i, r, j: (i, 0, r)),
                  pl.BlockSpec((1, sub, w), lambda i, r, j: (i, 0, r)),
                  pl.BlockSpec((1, t, 1), lambda i, r, j: (r, j, 0)),
                  pl.BlockSpec((1, sub // t, 1, t), lambda i, r, j: (r, 0, 0, 0))],
        out_specs=[pl.BlockSpec((1, t, w), lambda i, r, j: (i, j, r)),
                   pl.BlockSpec((1, t, w), lambda i, r, j: (i, j, r))],
        out_shape=[shp, shp],
        compiler_params=_cparams(3),
        name="dilated_attn_g%d" % grp,
    )(slopes, qv, kv, vv, pc, pr)
    return o.reshape(b, s, w), lse.reshape(b, s, w)


def _out_odd_kernel(x_ref, mod_ref, oc_ref, os_ref, ow_ref, d0_ref, d1_ref, d2_ref,
                    l0_ref, l1_ref, l2_ref, gz_ref, w_ref, o_ref):
    gz = gz_ref[0].astype(F32)
    nsa = oc_ref[0] + os_ref[0] + ow_ref[0]
    m1 = (nsa * gz[:, 0:NSA_W]).astype(BF16)
    l0, l1, l2 = l0_ref[0], l1_ref[0], l2_ref[0]
    mx = jnp.maximum(jnp.maximum(l0, l1), l2)
    e0, e1, e2 = jnp.exp(l0 - mx), jnp.exp(l1 - mx), jnp.exp(l2 - mx)
    dil = (e0 * d0_ref[0] + e1 * d1_ref[0] + e2 * d2_ref[0]) / (e0 + e1 + e2)
    m2 = (dil * gz[:, NSA_W:NSA_W + DIL_W]).astype(BF16)
    y = _dot(m1, w_ref[0:NSA_W, :]) + _dot(m2, w_ref[NSA_W:NSA_W + DIL_W, :])
    gate = mod_ref[0][:, 2 * D_MODEL:3 * D_MODEL]
    o_ref[0] = x_ref[0] + gate * y


def _out_odd(x, mod3, oc, os_, ow, dils, lses, gz, w):
    b, s, d = x.shape
    ts = min(TS_PROJ, s)
    row = lambda n: pl.BlockSpec((1, ts, n), lambda i, j: (i, j, 0))
    return pl.pallas_call(
        _out_odd_kernel,
        grid=(b, s // ts),
        in_specs=[row(d), pl.BlockSpec((1, 1, 3 * d), lambda i, j: (i, 0, 0)),
                  row(NSA_W), row(NSA_W), row(NSA_W)] + [row(DIL_W)] * 6 + [row(1024), _full(w.shape)],
        out_specs=row(d),
        out_shape=jax.ShapeDtypeStruct((b, s, d), F32),
        compiler_params=_cparams(2),
        name="outproj_odd",
    )(x, mod3, oc, os_, ow, *dils, *lses, gz, w)


def _pad_cols(w, n):
    return jnp.pad(w, ((0, 0), (0, n - w.shape[1])))


def _pad_vec(v, n=D_MODEL):
    return jnp.pad(v, (0, n - v.shape[0]))


def _group_matrix(sizes, total):
    m = np.zeros((total, total), np.float32)
    off = 0
    for sz, on in sizes:
        if on:
            m[off:off + sz, off:off + sz] = 1.0
        off += sz
    return jnp.asarray(m, BF16)


def _swap_halves(w):
    h = w.shape[-1] // 2
    return jnp.concatenate([w[..., h:], w[..., :h]], axis=-1)


def _pack_even(w_in, norm_g, sb_qn, sb_kn, qa_g, wq_up, kva_g, wkv_up, qn, kn):
    d = w_in.shape[0]
    z = lambda n: jnp.zeros((d, n), w_in.dtype)
    kr = w_in[:, 2432:2464]
    w = jnp.concatenate([
        w_in[:, 0:1536], w_in[:, 1536:2048], w_in[:, 2464:2976], w_in[:, 2048:2304], w_in[:, 2304:2432],
        z(64), kr, z(32), z(64), _swap_halves(kr), z(32)], axis=1).astype(BF16)
    wq3 = wq_up.reshape(MLA_Q_RANK, MLA_HEADS, MLA_NOPE + MLA_ROPE)
    zq = jnp.zeros((MLA_Q_RANK, MLA_HEADS, 32), wq_up.dtype)
    wq = jnp.concatenate([wq3, zq], axis=-1).reshape(MLA_Q_RANK, MLA_HEADS * 128).astype(BF16)
    wqs = jnp.concatenate([jnp.zeros((MLA_Q_RANK, MLA_HEADS, 64), wq_up.dtype),
                           _swap_halves(wq3[..., MLA_NOPE:]), zq], axis=-1)
    wqs = wqs.reshape(MLA_Q_RANK, MLA_HEADS * 128).astype(BF16)
    wkv3 = wkv_up.reshape(MLA_KV_RANK, MLA_HEADS, MLA_NOPE + MLA_V)
    wk = jnp.concatenate([wkv3[..., :MLA_NOPE], jnp.zeros_like(wkv3[..., :MLA_NOPE])], axis=-1)
    wkv = jnp.concatenate([wk.reshape(MLA_KV_RANK, MLA_HEADS * 128),
                           wkv3[..., MLA_NOPE:].reshape(MLA_KV_RANK, MLA_HEADS * MLA_V)], axis=1).astype(BF16)
    z32 = jnp.zeros((32,), F32)
    z64 = jnp.zeros((64,), F32)
    scale = (MLA_NOPE + MLA_ROPE) ** -0.5 * LOG2E
    qg = jnp.tile(jnp.concatenate([qn, z32]), MLA_HEADS) * scale
    qgs = jnp.tile(jnp.concatenate([z64, _swap_halves(qn[MLA_NOPE:]), z32]), MLA_HEADS) * scale
    kg = jnp.tile(jnp.concatenate([kn[:MLA_NOPE], z64]), MLA_HEADS)
    krg = jnp.concatenate([z64, kn[MLA_NOPE:], z32])
    krgs = jnp.concatenate([z64, _swap_halves(kn[MLA_NOPE:]), z32])
    cnt = jnp.tile(jnp.concatenate([jnp.full((64,), 1.0 / 64), jnp.full((32,), 1.0 / 32), jnp.ones((32,))]),
                   MLA_HEADS).astype(F32)
    rows = [norm_g, _pad_vec(jnp.tile(sb_qn, SB_HEADS) * (-LOG2E * HEAD_DIM ** -0.5)), _pad_vec(jnp.tile(sb_kn, SB_HEADS)),
            _pad_vec(qa_g), _pad_vec(kva_g), qg, qgs, kg, _pad_vec(krg), _pad_vec(krgs), cnt]
    rows += [jnp.zeros((D_MODEL,), F32)] * (16 - len(rows))
    return w, wq, wqs, wkv, jnp.stack(rows).astype(F32)


def _pack_odd(w_in, norm_g, nsa_qn, nsa_kn, dil_qn, dil_kn):
    d = w_in.shape[0]

    def padded_heads(cols, n):
        c3 = cols.reshape(d, n, HEAD_DIM)
        return jnp.concatenate([c3, jnp.zeros_like(c3)], axis=-1).reshape(d, n * 128)

    def doubled_heads(cols, n):
        c3 = cols.reshape(d, n, HEAD_DIM)
        return jnp.concatenate([c3, c3], axis=-1).reshape(d, n * 128)

    ng = w_in[:, 1920:1956].reshape(d, NSA_KV_HEADS, NSA_HPG * 3)
    ng = jnp.pad(ng, ((0, 0), (0, 0), (0, 128 - NSA_HPG * 3))).reshape(d, NSA_KV_HEADS * 128)
    w_nsa = jnp.concatenate([
        padded_heads(w_in[:, 0:768], NSA_HEADS),
        padded_heads(w_in[:, 768:960], NSA_KV_HEADS), padded_heads(w_in[:, 960:1152], NSA_KV_HEADS),
        padded_heads(w_in[:, 1152:1344], NSA_KV_HEADS), padded_heads(w_in[:, 1344:1536], NSA_KV_HEADS),
        padded_heads(w_in[:, 1536:1728], NSA_KV_HEADS), padded_heads(w_in[:, 1728:1920], NSA_KV_HEADS),
        ng, w_in[:, 1956:2724], w_in[:, 5028:5284]], axis=1).astype(BF16)
    w_dil = w_in[:, 2724:5028].astype(BF16)
    z64 = jnp.zeros((64,), F32)
    n = NSA_HEADS * 128
    rows = [_pad_vec(norm_g, n),
            jnp.tile(jnp.concatenate([nsa_qn * (LOG2E * HEAD_DIM ** -0.5), z64]), NSA_HEADS),
            _pad_vec(jnp.tile(jnp.concatenate([nsa_kn, z64]), NSA_KV_HEADS), n),
            _pad_vec(jnp.tile(dil_qn, DIL_HEADS) * HEAD_DIM ** -0.5, n), _pad_vec(jnp.tile(dil_kn, DIL_HEADS), n)]
    rows += [jnp.zeros((n,), F32)] * (8 - len(rows))
    return w_nsa, w_dil, jnp.stack(rows).astype(F32)


def _pack_compress(w, pe, double):
    w3 = w.reshape(CMP_LEN, HEAD_DIM, HEAD_DIM)
    w3 = jnp.concatenate([w3, jnp.zeros_like(w3)], axis=1)
    w3 = jnp.concatenate([w3, w3 if double else jnp.zeros_like(w3)], axis=2)
    wp = w3.reshape(2, CMP_STRIDE * 128, 128).astype(BF16)
    pe2 = jnp.concatenate([pe, jnp.zeros_like(pe)], axis=1).reshape(2, 1, CMP_STRIDE * 128)
    pe2 = jnp.broadcast_to(pe2, (2, 8, CMP_STRIDE * 128)).astype(F32)
    return wp, pe2


def _alibi_slopes(n):
    return 2.0 ** (-8.0 * jnp.arange(1, n + 1, dtype=jnp.float32) / n)


def kernel(x, c, positions, ada_w, ada_b, norm_g, ev_w_in, ev_w_out, sb_qn, sb_kn, mla_qa_g, mla_wq_up,
           mla_kva_g, mla_wkv_up, mla_qn, mla_kn, od_w_in, od_w_out, nsa_qn, nsa_kn, nsa_cmp_wk, nsa_cmp_wv,
           nsa_cmp_pe_k, nsa_cmp_pe_v, dil_qn, dil_kn):
    b, s, d = x.shape
    depth = ada_w.shape[0]
    pos_f = positions.astype(F32)

    inv_freq = ROPE_BASE ** (-jnp.arange(0, MLA_ROPE, 2, dtype=F32) / MLA_ROPE)
    ang = pos_f[:, None] * inv_freq[None, :]
    cos, sin = jnp.cos(ang), jnp.sin(ang)
    cos128 = jnp.concatenate([jnp.ones((s, 64), F32), cos, cos, jnp.zeros((s, 32), F32)], axis=1)
    sin128 = jnp.concatenate([jnp.zeros((s, 64), F32), -sin, sin, jnp.zeros((s, 32), F32)], axis=1)
    nsa_slopes = _alibi_slopes(NSA_HEADS)
    dil_slopes = _alibi_slopes(N_DIL * DIL_HEADS)
    nch = s // CMP_STRIDE
    chunk_sum = pos_f.reshape(nch, CMP_STRIDE).sum(axis=1)
    cpos = ((chunk_sum + jnp.roll(chunk_sum, -1)) / CMP_LEN).reshape(1, nch)
    n_sel = s // SEL_LEN
    cst = np.arange(nch)[:, None] * CMP_STRIDE
    jst = np.arange(LANES)[None, :] * SEL_LEN
    ovl = ((cst <= jst + SEL_LEN - 1) & (cst + CMP_LEN - 1 >= jst) & (np.arange(LANES)[None, :] < n_sel))
    ovl = jnp.asarray(ovl.astype(np.float32), BF16)
    pos_col = pos_f.reshape(s, 1)
    tkn = min(TK_NSA, min(TQ_NSA, s))
    pos2d = pos_f.reshape(s // tkn, 1, tkn)

    g64 = _group_matrix([(64, 1)] * 4, 256)
    gm = _group_matrix([(64, 1), (32, 1), (32, 0)] * 2, 256)
    tks = min(TK_SB, min(TQ_SB, s))
    tri = np.concatenate([np.tril(np.ones((tks, tks), np.float32), -1),
                          np.ones((tks, LANES), np.float32)], axis=1)
    tri = jnp.asarray(tri, BF16)

    c8 = jnp.pad(c, ((0, 8 - b), (0, 0)))
    mod_all = _modulation(c8, ada_w, ada_b)

    for layer in range(depth):
        j = layer // 2
        mod3 = mod_all[layer, :b].reshape(b, 1, 3 * d)
        if layer % 2 == 0:
            w, wq, wqs, wkv, vec = _pack_even(ev_w_in[j], norm_g[layer], sb_qn[j], sb_kn[j], mla_qa_g[j],
                                              mla_wq_up[j], mla_kva_g[j], mla_wkv_up[j], mla_qn[j], mla_kn[j])
            sbq, sbk, sbv, gz, mq, mk, mv = _k1_even(x, mod3, vec, w, wq, wqs, wkv, g64, gm, cos128, sin128)
            o_sb = _sb_attention(sbq, sbk, sbv, tri)
            o_mla = _mla_attention(mq, mk, mv)
            x = _out_even(x, mod3, o_sb, o_mla, gz, ev_w_out[j].astype(BF16))
        else:
            w_nsa, w_dil, vec = _pack_odd(od_w_in[j], norm_g[layer], nsa_qn[j], nsa_kn[j], dil_qn[j], dil_kn[j])
            q, ck, cv, sk, sv, wk, wv, gates, gz = _k1_nsa(x, mod3, vec, w_nsa, g64)
            dq, dk, dv = _k1_dil(x, mod3, vec, w_dil, g64)
            wck, pek = _pack_compress(nsa_cmp_wk[j], nsa_cmp_pe_k[j], False)
            wcv, pev = _pack_compress(nsa_cmp_wv[j], nsa_cmp_pe_v[j], True)
            kgain = jnp.concatenate([nsa_kn[j], jnp.zeros((64,), F32)]).reshape(1, 128)
            kc, vc = _compress(ck, cv, wck, wcv, pek, pev, kgain, g64[0:128, 0:128])
            o_c, selb = _nsa_cmp(nsa_slopes, q, kc, vc, ovl, cpos, pos_col, gates)
            o_s = _nsa_gqa(nsa_slopes, q, selb, sk, sv, pos2d, gates, 1)
            o_w = _nsa_gqa(nsa_slopes, q, selb, wk, wv, pos2d, gates, 2)
            dils, lses = [], []
            for g in range(N_DIL):
                o, lse = _dilated(dil_slopes, dq[g], dk[g], dv[g], pos_f, g)
                dils.append(o)
                lses.append(lse)
            x = _out_odd(x, mod3, o_c, o_s, o_w, dils, lses, gz, od_w_out[j].astype(BF16))
    return x
```

```python
import functools

import numpy as np
import jax
import jax.numpy as jnp
from jax import lax
from jax.experimental import pallas as pl
from jax.experimental.pallas import tpu as pltpu

F32 = jnp.float32
BF16 = jnp.bfloat16

D_MODEL = 1024
HEAD_DIM = 64
NORM_EPS = 1e-6
TINY = 1e-30
SB_HEADS = 8
MLA_HEADS = 8
MLA_Q_RANK = 256
MLA_KV_RANK = 128
MLA_NOPE = 64
MLA_ROPE = 32
MLA_V = 64
ROPE_BASE = 10000.0
NSA_HEADS = 12
NSA_KV_HEADS = 3
NSA_HPG = 4
CMP_LEN = 32
CMP_STRIDE = 16
SEL_LEN = 64
SEL_TOPN = 16
WIN = 512
FORCE_BONUS = 1e3
DIL_CFG = ((128, 1), (512, 4), (2048, 16))
N_DIL = 3
DIL_HEADS = 4
SB_W = SB_HEADS * HEAD_DIM
MLA_OUT = MLA_HEADS * MLA_V
NSA_W = NSA_HEADS * HEAD_DIM
NSA_KV_W = NSA_KV_HEADS * HEAD_DIM
DIL_W = DIL_HEADS * HEAD_DIM

LANES = 128
MXU_DIM = 256
NEG_MASK = -1e30
LOG2E = 1.4426950408889634
SEL_OFF = 2.0 ** 30
VMEM_LIMIT = 56 * 1024 * 1024

TS_PROJ = 512
TQ_SB = 1024
TK_SB = 256
TQ_MLA = 1024
TD_MLA = 512
TQ_CMP = 256
TQ_NSA = 512
TK_NSA = 512
T_DIL = 128


def _dot(a, b):
    return jnp.dot(a, b, preferred_element_type=F32)


def _dot_nt(a, b):
    return lax.dot_general(a, b, (((1,), (1,)), ((), ())), preferred_element_type=F32)


def _split_hl(a):
    hi = a.astype(BF16)
    lo = (a - hi.astype(F32)).astype(BF16)
    return hi, lo


def _dot_hl(a, b):
    hi, lo = _split_hl(a)
    return _dot(hi, b) + _dot(lo, b)


def _sigmoid(z):
    return 1.0 / (1.0 + jnp.exp(-z))


def _cparams(n_axes):
    return pltpu.CompilerParams(dimension_semantics=("arbitrary",) * n_axes,
                                vmem_limit_bytes=VMEM_LIMIT)


def _full(shape):
    n = len(shape)
    return pl.BlockSpec(shape, lambda *a, _n=n: (0,) * _n)


def _mod_kernel(c_ref, w_ref, b_ref, o_ref):
    c = c_ref[...]
    a = c * _sigmoid(c)
    ah, al = _split_hl(a)
    wh, wl = _split_hl(w_ref[0])
    o_ref[0] = _dot(ah, wh) + _dot(ah, wl) + _dot(al, wh) + b_ref[0]


def _modulation(c8, ada_w, ada_b):
    depth, d, n3 = ada_w.shape
    tn = 1024
    return pl.pallas_call(
        _mod_kernel,
        grid=(depth, n3 // tn),
        in_specs=[pl.BlockSpec((8, d), lambda l, j: (0, 0)),
                  pl.BlockSpec((1, d, tn), lambda l, j: (l, 0, j)),
                  pl.BlockSpec((1, 1, tn), lambda l, j: (l, 0, j))],
        out_specs=pl.BlockSpec((1, 8, tn), lambda l, j: (l, 0, j)),
        out_shape=jax.ShapeDtypeStruct((depth, 8, n3), F32),
        compiler_params=_cparams(2),
        name="adaln_mod",
    )(c8, ada_w, ada_b.reshape(depth, 1, n3))


def _modulated(x_ref, mod_ref, ng):
    x = x_ref[0]
    mod = mod_ref[0]
    shift = mod[:, 0:D_MODEL]
    scale = mod[:, D_MODEL:2 * D_MODEL]
    ms = jnp.mean(x * x, axis=-1, keepdims=True)
    h = x * lax.rsqrt(ms + NORM_EPS) * ng
    h = h * (1.0 + scale) + shift
    return h.astype(BF16)


def _group_norm_chunk(t, g, inv_cnt):
    ss = _dot_hl(t * t, g)
    return t * lax.rsqrt(ss * inv_cnt + NORM_EPS)


def _row_rms(t, gain):
    ms = jnp.mean(t * t, axis=-1, keepdims=True)
    return t * lax.rsqrt(ms + NORM_EPS) * gain


EV_COLS = 3200


def _k1_even_kernel(x_ref, mod_ref, vec_ref, w_ref, wq_ref, wqs_ref, wkv_ref, g64_ref, gm_ref,
                    cos_ref, sin_ref, sbq_o, sbk_o, sbv_o, gz_o, mq_o, mk_o, mv_o):
    hb = _modulated(x_ref, mod_ref, vec_ref[0:1, :])
    g64 = g64_ref[...]
    gm = gm_ref[...]
    inv64 = 1.0 / HEAD_DIM

    u = _dot(hb, w_ref[:, 0:1536])
    for c in range(2):
        sl = slice(c * 256, (c + 1) * 256)
        qn = _group_norm_chunk(u[:, c * 256:(c + 1) * 256], g64, inv64)
        sbq_o[0, :, sl] = (qn * vec_ref[1:2, sl]).astype(BF16)
        kn = _group_norm_chunk(u[:, 512 + c * 256:512 + (c + 1) * 256], g64, inv64)
        sbk_o[0, :, sl] = (kn * vec_ref[2:3, sl]).astype(BF16)
    sbv_o[0] = u[:, 1024:1536].astype(BF16)

    uz = _dot(hb, w_ref[:, 1536:2560])
    gz_o[0] = (uz * _sigmoid(uz)).astype(BF16)

    ul = _dot(hb, w_ref[:, 2560:3200])
    qlat = _row_rms(ul[:, 0:256], vec_ref[3:4, 0:256]).astype(BF16)
    kvlat = _row_rms(ul[:, 256:384], vec_ref[4:5, 0:128]).astype(BF16)
    krm = ul[:, 384:512]
    krs = ul[:, 512:640]
    cs = cos_ref[...]
    sn = sin_ref[...]
    cs2 = jnp.concatenate([cs, cs], axis=1)
    sn2 = jnp.concatenate([sn, sn], axis=1)

    tq = _dot(qlat, wq_ref[...])
    tqs = _dot(qlat, wqs_ref[...])
    for p in range(4):
        sl = slice(p * 256, (p + 1) * 256)
        tc = tq[:, p * 256:(p + 1) * 256]
        ss = _dot_hl(tc * tc, gm)
        inv = lax.rsqrt(ss * vec_ref[10:11, sl] + NORM_EPS)
        a = tc * inv * vec_ref[5:6, sl]
        b = tqs[:, p * 256:(p + 1) * 256] * inv * vec_ref[6:7, sl]
        mq_o[0, :, sl] = (a * cs2 + b * sn2).astype(BF16)

    sskr = _dot_hl(krm * krm, gm[0:128, 0:128])
    invr = lax.rsqrt(sskr * vec_ref[10:11, 0:128] + NORM_EPS)
    kr = (krm * invr * vec_ref[8:9, 0:128]) * cs + (krs * invr * vec_ref[9:10, 0:128]) * sn
    kr2 = jnp.concatenate([kr, kr], axis=1)

    kv = _dot(kvlat, wkv_ref[...])
    for p in range(4):
        sl = slice(p * 256, (p + 1) * 256)
        kn = _group_norm_chunk(kv[:, p * 256:(p + 1) * 256], g64, inv64)
        mk_o[0, :, sl] = (kn * vec_ref[7:8, sl] + kr2).astype(BF16)
    mv_o[0] = kv[:, 1024:1536].astype(BF16)


def _k1_even(x, mod3, vec, w, wq, wqs, wkv, g64, gm, cos128, sin128):
    b, s, d = x.shape
    ts = min(TS_PROJ, s)
    row = lambda n: pl.BlockSpec((1, ts, n), lambda i, j: (i, j, 0))
    outs = [(512, BF16), (512, BF16), (512, BF16), (1024, BF16), (1024, BF16), (1024, BF16), (512, BF16)]
    return pl.pallas_call(
        _k1_even_kernel,
        grid=(b, s // ts),
        in_specs=[row(d),
                  pl.BlockSpec((1, 1, 3 * d), lambda i, j: (i, 0, 0)),
                  _full(vec.shape), _full(w.shape), _full(wq.shape), _full(wqs.shape), _full(wkv.shape),
                  _full(g64.shape), _full(gm.shape),
                  pl.BlockSpec((ts, 128), lambda i, j: (j, 0)),
                  pl.BlockSpec((ts, 128), lambda i, j: (j, 0))],
        out_specs=[row(n) for n, _ in outs],
        out_shape=[jax.ShapeDtypeStruct((b, s, n), dt) for n, dt in outs],
        compiler_params=_cparams(2),
        name="inproj_even",
    )(x, mod3, vec, w, wq, wqs, wkv, g64, gm, cos128, sin128)


def _sb_kernel(q_ref, k_ref, v_ref, tri_ref, o_ref, acc_ref, car_ref, *, tq, tk):
    qi = pl.program_id(2)
    q = q_ref[0]
    lane = lax.broadcasted_iota(jnp.int32, (1, LANES), 1)
    lo_half = lane < HEAD_DIM
    zero = jnp.zeros_like(q)
    qs = (jnp.where(lo_half, q, zero), jnp.where(lo_half, zero, q))
    tri = tri_ref[...]
    acc_ref[...] = jnp.zeros_like(acc_ref)
    car_ref[...] = jnp.zeros_like(car_ref)
    col0 = lax.broadcasted_iota(jnp.int32, (1, tk), 1)

    def tile(kt, r0, masked):
        n = tq - r0
        off = pl.multiple_of(kt * tk, tk)
        k = k_ref[0, pl.ds(off, tk), :]
        v = v_ref[0, pl.ds(off, tk), :]
        if masked:
            rows = qi * tq + r0 + lax.broadcasted_iota(jnp.int32, (n, 1), 0)
            strict = (kt * tk + col0) < rows
        for i in range(2):
            nz = _dot_nt(qs[i][r0:tq], k)
            soft = jnp.log(1.0 + jnp.exp2(-jnp.abs(nz))) * LOG2E
            lom = jnp.minimum(nz, 0.0) - soft
            if masked:
                lom = jnp.where(strict, lom, 0.0)
            tt = _dot(lom.astype(BF16), tri)
            car = car_ref[i, r0:tq]
            w = jnp.exp2(tt - nz + jnp.concatenate([car] * (tk // LANES), axis=1))
            if masked:
                w = jnp.where(strict, w, 0.0)
            acc_ref[i, r0:tq] += _dot(w.astype(BF16), v)
            car_ref[i, r0:tq] = car + jnp.broadcast_to(tt[:, 0:1], (n, LANES))

    per = tq // tk
    for c in reversed(range(per)):
        tile(qi * per + c, c * tk, True)

    def body(j, carry):
        for u in range(per):
            tile(qi * per - 1 - per * j - u, 0, False)
        return carry

    lax.fori_loop(0, qi, body, 0)
    o_ref[0] = jnp.where(lo_half, acc_ref[0], acc_ref[1])


def _sb_attention(q, k, v, tri):
    b, s, w = q.shape
    tq = min(TQ_SB, s)
    tk = min(TK_SB, tq)
    kern = functools.partial(_sb_kernel, tq=tq, tk=tk)
    return pl.pallas_call(
        kern,
        grid=(b, w // LANES, s // tq),
        in_specs=[pl.BlockSpec((1, tq, LANES), lambda i, p, j: (i, j, p)),
                  pl.BlockSpec((1, s, LANES), lambda i, p, j: (i, 0, p)),
                  pl.BlockSpec((1, s, LANES), lambda i, p, j: (i, 0, p)),
                  _full(tri.shape)],
        out_specs=pl.BlockSpec((1, tq, LANES), lambda i, p, j: (i, j, p)),
        out_shape=jax.ShapeDtypeStruct((b, s, w), F32),
        scratch_shapes=[pltpu.VMEM((2, tq, LANES), F32), pltpu.VMEM((2, tq, LANES), F32)],
        compiler_params=_cparams(3),
        name="stickbreak_attn",
    )(q, k, v, tri)


def _lane_max(s):
    m = s[:, 0:LANES]
    for c in range(1, s.shape[1] // LANES):
        m = jnp.maximum(m, s[:, c * LANES:(c + 1) * LANES])
    return m


def _mla_kernel(q_ref, k_ref, v_ref, o_ref, acc_ref, m_ref, *, tq, td):
    qi = pl.program_id(2)
    q = q_ref[0]
    qs = (q[:, 0:LANES], q[:, LANES:2 * LANES])
    lane = lax.broadcasted_iota(jnp.int32, (1, LANES), 1)
    lo_half = lane < HEAD_DIM
    acc_ref[...] = jnp.zeros_like(acc_ref)
    m_ref[...] = jnp.full_like(m_ref, NEG_MASK)

    def scores(off, width, r0, masked):
        k = k_ref[0, pl.ds(off, width), :]
        if masked:
            rows = qi * tq + r0 + lax.broadcasted_iota(jnp.int32, (tq - r0, 1), 0)
            keep = (off + lax.broadcasted_iota(jnp.int32, (1, width), 1)) <= rows
        out = []
        for i in range(2):
            s = _dot_nt(qs[i][r0:tq], k[:, i * LANES:(i + 1) * LANES])
            if masked:
                s = jnp.where(keep, s, NEG_MASK)
            out.append(s)
        return out

    def sweep(fn):
        def body(j, carry):
            fn(pl.multiple_of(j * tq, tq), tq, 0, False)
            return carry
        lax.fori_loop(0, qi, body, 0)
        for c in range(tq // td):
            fn(pl.multiple_of(qi * tq + c * td, td), td, c * td, True)

    def row_max(off, width, r0, masked):
        for i, s in enumerate(scores(off, width, r0, masked)):
            m_ref[i, r0:tq] = jnp.maximum(m_ref[i, r0:tq], _lane_max(s))

    def accumulate(off, width, r0, masked):
        vx = jnp.concatenate([v_ref[0, pl.ds(off, width), :], jnp.ones((width, LANES), BF16)], axis=1)
        for i, s in enumerate(scores(off, width, r0, masked)):
            p = jnp.exp2(s - jnp.concatenate([m_ref[i, r0:tq]] * (width // LANES), axis=1))
            acc_ref[i, r0:tq] += _dot(p.astype(BF16), vx)

    sweep(row_max)
    for i in range(2):
        m_ref[i] = jnp.broadcast_to(jnp.max(m_ref[i], axis=-1, keepdims=True), (tq, LANES))
    sweep(accumulate)
    o_ref[0] = jnp.where(lo_half, acc_ref[0, :, 0:LANES] / acc_ref[0, :, LANES:2 * LANES],
                         acc_ref[1, :, 0:LANES] / acc_ref[1, :, LANES:2 * LANES])


def _mla_attention(q, k, v):
    b, s, _ = q.shape
    tq = min(TQ_MLA, s)
    td = min(TD_MLA, tq)
    kern = functools.partial(_mla_kernel, tq=tq, td=td)
    return pl.pallas_call(
        kern,
        grid=(b, MLA_HEADS // 2, s // tq),
        in_specs=[pl.BlockSpec((1, tq, 2 * LANES), lambda i, p, j: (i, j, p)),
                  pl.BlockSpec((1, s, 2 * LANES), lambda i, p, j: (i, 0, p)),
                  pl.BlockSpec((1, s, LANES), lambda i, p, j: (i, 0, p))],
        out_specs=pl.BlockSpec((1, tq, LANES), lambda i, p, j: (i, j, p)),
        out_shape=jax.ShapeDtypeStruct((b, s, MLA_OUT), F32),
        scratch_shapes=[pltpu.VMEM((2, tq, 2 * LANES), F32), pltpu.VMEM((2, tq, LANES), F32)],
        compiler_params=_cparams(3),
        name="mla_attn",
    )(q, k, v)


def _out_even_kernel(x_ref, mod_ref, osb_ref, omla_ref, gz_ref, w_ref, o_ref):
    gz = gz_ref[0].astype(F32)
    m1 = (osb_ref[0] * gz[:, 0:SB_W]).astype(BF16)
    m2 = (omla_ref[0] * gz[:, SB_W:SB_W + MLA_OUT]).astype(BF16)
    y = _dot(m1, w_ref[0:SB_W, :]) + _dot(m2, w_ref[SB_W:SB_W + MLA_OUT, :])
    gate = mod_ref[0][:, 2 * D_MODEL:3 * D_MODEL]
    o_ref[0] = x_ref[0] + gate * y


def _out_even(x, mod3, osb, omla, gz, w):
    b, s, d = x.shape
    ts = min(TS_PROJ, s)
    row = lambda n: pl.BlockSpec((1, ts, n), lambda i, j: (i, j, 0))
    return pl.pallas_call(
        _out_even_kernel,
        grid=(b, s // ts),
        in_specs=[row(d), pl.BlockSpec((1, 1, 3 * d), lambda i, j: (i, 0, 0)),
                  row(SB_W), row(MLA_OUT), row(SB_W + MLA_OUT), _full(w.shape)],
        out_specs=row(d),
        out_shape=jax.ShapeDtypeStruct((b, s, d), F32),
        compiler_params=_cparams(2),
        name="outproj_even",
    )(x, mod3, osb, omla, gz, w)


NSA_COLS = 5248


def _k1_nsa_kernel(x_ref, mod_ref, vec_ref, w_ref, g64_ref,
                   q_o, ck_o, cv_o, sk_o, sv_o, wk_o, wv_o, gt_o, gz_o):
    hb = _modulated(x_ref, mod_ref, vec_ref[0:1, 0:D_MODEL])
    g64 = g64_ref[...]
    inv64 = 1.0 / HEAD_DIM

    uq = _dot(hb, w_ref[:, 0:1536])
    for c in range(6):
        sl = slice(c * 256, (c + 1) * 256)
        qn = _group_norm_chunk(uq[:, c * 256:(c + 1) * 256], g64, inv64)
        q_o[0, :, sl] = (qn * vec_ref[1:2, sl]).astype(BF16)

    uc = _dot(hb, w_ref[:, 1536:2304])
    for g in range(NSA_KV_HEADS):
        ck_o[0, g] = uc[:, g * 128:(g + 1) * 128].astype(BF16)
        cv_o[0, g] = uc[:, 384 + g * 128:384 + (g + 1) * 128].astype(BF16)

    us = _dot(hb, w_ref[:, 2304:3840])
    kgain = vec_ref[2:3, 0:384]
    g128 = g64[0:128, 0:128]
    for g in range(NSA_KV_HEADS):
        sl = slice(g * 128, (g + 1) * 128)
        t = us[:, g * 128:(g + 1) * 128]
        ss = _dot_hl(t * t, g128)
        sk_o[0, :, sl] = (t * lax.rsqrt(ss * inv64 + NORM_EPS) * kgain[:, sl]).astype(BF16)
        t = us[:, 768 + g * 128:768 + (g + 1) * 128]
        ss = _dot_hl(t * t, g128)
        wk_o[0, :, sl] = (t * lax.rsqrt(ss * inv64 + NORM_EPS) * kgain[:, sl]).astype(BF16)
    ones_hi = (lax.broadcasted_iota(jnp.int32, (1, 384), 1) % 128) >= HEAD_DIM
    sv_o[0] = jnp.where(ones_hi, 1.0, us[:, 384:768]).astype(BF16)
    wv_o[0] = jnp.where(ones_hi, 1.0, us[:, 1152:1536]).astype(BF16)

    ug = _dot(hb, w_ref[:, 3840:4224])
    gt_o[0] = _sigmoid(ug)
    uz = _dot(hb, w_ref[:, 4224:5248])
    gz_o[0] = (uz * _sigmoid(uz)).astype(BF16)


def _k1_nsa(x, mod3, vec, w, g64):
    b, s, d = x.shape
    ts = min(TS_PROJ, s)
    row = lambda n: pl.BlockSpec((1, ts, n), lambda i, j: (i, j, 0))
    grp = pl.BlockSpec((1, NSA_KV_HEADS, ts, 128), lambda i, j: (i, 0, j, 0))
    return pl.pallas_call(
        _k1_nsa_kernel,
        grid=(b, s // ts),
        in_specs=[row(d), pl.BlockSpec((1, 1, 3 * d), lambda i, j: (i, 0, 0)),
                  _full(vec.shape), _full(w.shape), _full(g64.shape)],
        out_specs=[row(1536), grp, grp, row(384), row(384), row(384), row(384), row(384), row(1024)],
        out_shape=[jax.ShapeDtypeStruct((b, s, 1536), BF16),
                   jax.ShapeDtypeStruct((b, NSA_KV_HEADS, s, 128), BF16),
                   jax.ShapeDtypeStruct((b, NSA_KV_HEADS, s, 128), BF16),
                   jax.ShapeDtypeStruct((b, s, 384), BF16),
                   jax.ShapeDtypeStruct((b, s, 384), BF16),
                   jax.ShapeDtypeStruct((b, s, 384), BF16),
                   jax.ShapeDtypeStruct((b, s, 384), BF16),
                   jax.ShapeDtypeStruct((b, s, 384), F32),
                   jax.ShapeDtypeStruct((b, s, 1024), BF16)],
        compiler_params=_cparams(2),
        name="inproj_nsa",
    )(x, mod3, vec, w, g64)


def _k1_dil_kernel(x_ref, mod_ref, vec_ref, w_ref, g64_ref, q_o, k_o, v_o):
    hb = _modulated(x_ref, mod_ref, vec_ref[0:1, 0:D_MODEL])
    g64 = g64_ref[...]
    inv64 = 1.0 / HEAD_DIM
    u = _dot(hb, w_ref[...])
    for g in range(N_DIL):
        qn = _group_norm_chunk(u[:, g * 256:(g + 1) * 256], g64, inv64)
        q_o[g, 0] = (qn * vec_ref[3:4, 0:256]).astype(BF16)
        kn = _group_norm_chunk(u[:, 768 + g * 256:768 + (g + 1) * 256], g64, inv64)
        k_o[g, 0] = (kn * vec_ref[4:5, 0:256]).astype(BF16)
        v_o[g, 0] = u[:, 1536 + g * 256:1536 + (g + 1) * 256].astype(BF16)


def _k1_dil(x, mod3, vec, w, g64):
    b, s, d = x.shape
    ts = min(TS_PROJ, s)
    row = lambda n: pl.BlockSpec((1, ts, n), lambda i, j: (i, j, 0))
    grp = pl.BlockSpec((N_DIL, 1, ts, 256), lambda i, j: (0, i, j, 0))
    shp = jax.ShapeDtypeStruct((N_DIL, b, s, 256), BF16)
    return pl.pallas_call(
        _k1_dil_kernel,
        grid=(b, s // ts),
        in_specs=[row(d), pl.BlockSpec((1, 1, 3 * d), lambda i, j: (i, 0, 0)),
                  _full(vec.shape), _full(w.shape), _full(g64.shape)],
        out_specs=[grp, grp, grp],
        out_shape=[shp, shp, shp],
        compiler_params=_cparams(2),
        name="inproj_dil",
    )(x, mod3, vec, w, g64)


def _compress_kernel(xk_ref, xv_ref, wk_ref, wv_ref, pek_ref, pev_ref, gain_ref, g128_ref, kc_o, vc_o):
    nch = xk_ref.shape[2]

    def comp(x_ref, w_ref, pe_ref):
        x = x_ref[0, 0]
        lo = _dot(x, w_ref[0])
        hi = _dot(x, w_ref[1])
        pec = _dot_hl(pe_ref[0], w_ref[0]) + _dot_hl(pe_ref[1], w_ref[1])
        return lo + pltpu.roll(hi, nch - 1, 0) + pec[0:1, :]

    kc = comp(xk_ref, wk_ref, pek_ref)
    ss = _dot_hl(kc * kc, g128_ref[...])
    kc_o[0, 0] = (kc * lax.rsqrt(ss * (1.0 / HEAD_DIM) + NORM_EPS) * gain_ref[...]).astype(BF16)
    vc_o[0, 0] = comp(xv_ref, wv_ref, pev_ref).astype(BF16)


def _compress(ck, cv, wk, wv, pek, pev, gain, g128):
    b, g, s, _ = ck.shape
    nch = s // CMP_STRIDE
    xk = ck.reshape(b, g, nch, CMP_STRIDE * 128)
    xv = cv.reshape(b, g, nch, CMP_STRIDE * 128)
    blk = pl.BlockSpec((1, 1, nch, CMP_STRIDE * 128), lambda i, j: (i, j, 0, 0))
    oblk = pl.BlockSpec((1, 1, nch, 128), lambda i, j: (i, j, 0, 0))
    shp = jax.ShapeDtypeStruct((b, g, nch, 128), BF16)
    return pl.pallas_call(
        _compress_kernel,
        grid=(b, g),
        in_specs=[blk, blk, _full(wk.shape), _full(wv.shape), _full(pek.shape), _full(pev.shape),
                  _full(gain.shape), _full(g128.shape)],
        out_specs=[oblk, oblk],
        out_shape=[shp, shp],
        compiler_params=_cparams(2),
        name="nsa_compress",
    )(xk, xv, wk, wv, pek, pev, gain, g128)


def _pair_select(lo_half, a, b):
    return jnp.where(lo_half, a, b)


def _nsa_cmp_kernel(slope_ref, q_ref, kc_ref, vc_ref, ovl_ref, cpos_ref, pq_ref, gt_ref,
                    oc_o, selb_o, *, tq):
    g = pl.program_id(1)
    qi = pl.program_id(2)
    q4 = q_ref[0]
    kc = kc_ref[0, 0]
    vc = vc_ref[0, 0]
    nch = kc.shape[0]
    lane = lax.broadcasted_iota(jnp.int32, (1, LANES), 1)
    lo_half = lane < HEAD_DIM
    t = qi * tq + lax.broadcasted_iota(jnp.int32, (tq, 1), 0)
    n_id = lax.broadcasted_iota(jnp.int32, (1, nch), 1)
    valid = (n_id * CMP_STRIDE + (CMP_LEN - 1)) <= t
    dist = (pq_ref[...] - cpos_ref[...]) * LOG2E
    gt = gt_ref[0]
    psum = jnp.zeros((tq, nch), F32)
    outs = []
    for hh in range(NSA_HPG):
        s = _dot_nt(q4[:, hh * LANES:(hh + 1) * LANES], kc) - slope_ref[g * NSA_HPG + hh] * dist
        s = jnp.where(valid, s, -jnp.inf)
        mx = jnp.max(s, axis=-1, keepdims=True)
        mx = jnp.where(mx == -jnp.inf, 0.0, mx)
        e = jnp.exp2(s - mx)
        den = jnp.maximum(jnp.sum(e, axis=-1, keepdims=True), TINY)
        p = e / den
        psum = psum + p
        outs.append(_dot(p.astype(BF16), vc) * gt[:, 3 * hh:3 * hh + 1])
    oc_o[0, :, 0:LANES] = _pair_select(lo_half, outs[0], outs[1])
    oc_o[0, :, LANES:2 * LANES] = _pair_select(lo_half, outs[2], outs[3])

    imp = _dot_hl(psum, ovl_ref[...])
    cur = t >> 6
    forced = (lane == 0) | (lane == cur) | (lane == cur - 1)
    allowed = lane <= cur
    score = jnp.where(allowed, imp + jnp.where(forced, FORCE_BONUS, 0.0), -jnp.inf)

    lane_f = lane.astype(F32)

    def pick(_, carry):
        sc, sel = carry
        mx = jnp.max(sc, axis=-1, keepdims=True)
        idx = jnp.min(jnp.where(sc == mx, lane_f, float(LANES)), axis=-1, keepdims=True)
        hit = lane_f == idx
        return jnp.where(hit, -jnp.inf, sc), jnp.where(hit, 1.0, sel)

    _, sel = lax.fori_loop(0, SEL_TOPN, pick, (score, jnp.zeros((tq, LANES), F32)))
    chosen = (sel > 0.5) & allowed
    selb_o[0, 0] = jnp.where(chosen, 0.0, -SEL_OFF).astype(BF16)


def _nsa_cmp(slopes, q, kc, vc, ovl, cpos, pos_col, gates):
    b, s, _ = q.shape
    nch = kc.shape[2]
    tq = min(TQ_CMP, s)
    kern = functools.partial(_nsa_cmp_kernel, tq=tq)
    return pl.pallas_call(
        kern,
        grid=(b, NSA_KV_HEADS, s // tq),
        in_specs=[pl.BlockSpec(memory_space=pltpu.SMEM),
                  pl.BlockSpec((1, tq, 4 * LANES), lambda i, g, j: (i, j, g)),
                  pl.BlockSpec((1, 1, nch, LANES), lambda i, g, j: (i, g, 0, 0)),
                  pl.BlockSpec((1, 1, nch, LANES), lambda i, g, j: (i, g, 0, 0)),
                  _full(ovl.shape), _full(cpos.shape),
                  pl.BlockSpec((tq, 1), lambda i, g, j: (j, 0)),
                  pl.BlockSpec((1, tq, LANES), lambda i, g, j: (i, j, g))],
        out_specs=[pl.BlockSpec((1, tq, 2 * LANES), lambda i, g, j: (i, j, g)),
                   pl.BlockSpec((1, 1, tq, LANES), lambda i, g, j: (i, g, j, 0))],
        out_shape=[jax.ShapeDtypeStruct((b, s, NSA_W), F32),
                   jax.ShapeDtypeStruct((b, NSA_KV_HEADS, s, LANES), BF16)],
        compiler_params=_cparams(3),
        name="nsa_cmp_topk",
    )(slopes, q, kc, vc, ovl, cpos, pos_col, gates)


def _gqa_kernel(slope_ref, q_ref, selb_ref, k_ref, v_ref, pos_ref, gt_ref, o_ref,
                qa_ref, acc_ref, m_ref, *, tq, tk, branch):
    g = pl.program_id(1)
    qi = pl.program_id(2)
    q4 = q_ref[0]
    lane = lax.broadcasted_iota(jnp.int32, (1, LANES), 1)
    lo_half = lane < HEAD_DIM
    for hh in range(NSA_HPG):
        qa_ref[hh * tq:(hh + 1) * tq, 0:LANES] = q4[:, hh * LANES:(hh + 1) * LANES]
        if branch == 1:
            qa_ref[hh * tq:(hh + 1) * tq, LANES:2 * LANES] = selb_ref[0, 0]
    acc_ref[...] = jnp.zeros_like(acc_ref)
    m_ref[...] = jnp.full_like(m_ref, NEG_MASK)
    rows = qi * tq + lax.broadcasted_iota(jnp.int32, (tq, 1), 0)
    col0 = lax.broadcasted_iota(jnp.int32, (1, tk), 1)
    krow = lax.broadcasted_iota(jnp.int32, (tk, 1), 0)
    pref = pos_ref[qi * (tq // tk)][:, 0:1]
    per = tq // tk
    nrep = tk // LANES

    def scores(kt_true, masked):
        kt = jnp.maximum(kt_true, 0)
        off = pl.multiple_of(kt * tk, tk)
        k = k_ref[0, pl.ds(off, tk), :]
        if branch == 1:
            blk = (kt * tk + krow) >> 6
            onehot = jnp.where(lane == blk, 1.0, 0.0).astype(BF16)
            s_all = _dot_nt(qa_ref[...], jnp.concatenate([k, onehot], axis=1))
        else:
            s_all = _dot_nt(qa_ref[:, 0:LANES], k)
        rel = (pos_ref[kt] - pref) * LOG2E
        if masked:
            cols = kt_true * tk + col0
            d = rows - cols
            keep = d >= 0
            if branch == 2:
                keep = keep & (d < WIN) & (cols >= 0)
        out = []
        for hh in range(NSA_HPG):
            s = s_all[hh * tq:(hh + 1) * tq] + slope_ref[g * NSA_HPG + hh] * rel
            if masked:
                s = jnp.where(keep, s, NEG_MASK)
            out.append(s)
        return out

    def sweep(fn):
        if branch == 1:
            def body(j, carry):
                fn(j, False)
                return carry
            lax.fori_loop(0, qi * per, body, 0)
            for dd in range(per):
                fn(qi * per + dd, True)
        else:
            for dd in range(per + WIN // tk):
                fn((qi + 1) * per - 1 - dd, True)

    def row_max(kt_true, masked):
        for hh, s in enumerate(scores(kt_true, masked)):
            sl = slice(hh * tq, (hh + 1) * tq)
            m_ref[sl] = jnp.maximum(m_ref[sl], _lane_max(s))

    def accumulate(kt_true, masked):
        off = pl.multiple_of(jnp.maximum(kt_true, 0) * tk, tk)
        v = v_ref[0, pl.ds(off, tk), :]
        for hh, s in enumerate(scores(kt_true, masked)):
            sl = slice(hh * tq, (hh + 1) * tq)
            p = jnp.exp2(s - jnp.concatenate([m_ref[sl]] * nrep, axis=1))
            acc_ref[sl] += _dot(p.astype(BF16), v)

    sweep(row_max)
    for hh in range(NSA_HPG):
        sl = slice(hh * tq, (hh + 1) * tq)
        m_ref[sl] = jnp.broadcast_to(jnp.max(m_ref[sl], axis=-1, keepdims=True), (tq, LANES))
    sweep(accumulate)

    gt = gt_ref[0]
    for pr in range(2):
        res = []
        for x in range(2):
            hh = 2 * pr + x
            a = acc_ref[hh * tq:(hh + 1) * tq]
            r = pltpu.roll(a, HEAD_DIM, 1)
            o = a / r if x == 0 else r / a
            res.append(o * gt[:, 3 * hh + branch:3 * hh + branch + 1])
        o_ref[0, :, pr * LANES:(pr + 1) * LANES] = _pair_select(lo_half, res[0], res[1])


def _nsa_gqa(slopes, q, selb, k, v, pos2d, gates, branch):
    b, s, _ = q.shape
    tq = min(TQ_NSA, s)
    tk = min(TK_NSA, tq)
    kern = functools.partial(_gqa_kernel, tq=tq, tk=tk, branch=branch)
    return pl.pallas_call(
        kern,
        grid=(b, NSA_KV_HEADS, s // tq),
        in_specs=[pl.BlockSpec(memory_space=pltpu.SMEM),
                  pl.BlockSpec((1, tq, 4 * LANES), lambda i, g, j: (i, j, g)),
                  pl.BlockSpec((1, 1, tq, LANES), lambda i, g, j: (i, g, j, 0)),
                  pl.BlockSpec((1, s, LANES), lambda i, g, j: (i, 0, g)),
                  pl.BlockSpec((1, s, LANES), lambda i, g, j: (i, 0, g)),
                  _full(pos2d.shape),
                  pl.BlockSpec((1, tq, LANES), lambda i, g, j: (i, j, g))],
        out_specs=pl.BlockSpec((1, tq, 2 * LANES), lambda i, g, j: (i, j, g)),
        out_shape=jax.ShapeDtypeStruct((b, s, NSA_W), F32),
        scratch_shapes=[pltpu.VMEM((NSA_HPG * tq, 2 * LANES), BF16),
                        pltpu.VMEM((NSA_HPG * tq, LANES), F32),
                        pltpu.VMEM((NSA_HPG * tq, LANES), F32)],
        compiler_params=_cparams(3),
        name="nsa_sel_attn" if branch == 1 else "nsa_win_attn",
    )(slopes, q, selb, k, v, pos2d, gates)


def _dil_kernel(slope_ref, q_ref, k_ref, v_ref, pc_ref, pr_ref, o_ref, lse_ref, *, t, span, grp):
    i = pl.program_id(2)
    q4 = q_ref[0]
    lane = lax.broadcasted_iota(jnp.int32, (1, LANES), 1)
    lo_half = lane < HEAD_DIM
    prev = jnp.maximum(i - 1, 0)
    o_prev = pl.multiple_of(prev * t, t)
    o_cur = pl.multiple_of(i * t, t)
    kk = jnp.concatenate([k_ref[0, pl.ds(o_prev, t), :], k_ref[0, pl.ds(o_cur, t), :]], axis=0)
    vv = jnp.concatenate([v_ref[0, pl.ds(o_prev, t), :], v_ref[0, pl.ds(o_cur, t), :]], axis=0)
    pk = jnp.concatenate([pr_ref[0, prev], pr_ref[0, i]], axis=1)
    dist = pc_ref[0] - pk
    rows = i * t + lax.broadcasted_iota(jnp.int32, (t, 1), 0)
    cols = (i - 1) * t + lax.broadcasted_iota(jnp.int32, (1, 2 * t), 1)
    d = rows - cols
    valid = (cols >= 0) & (d >= 0) & (d <= span)
    zero = jnp.zeros((t, LANES), BF16)
    for p in range(2):
        qp = q4[:, p * LANES:(p + 1) * LANES]
        kp = kk[:, p * LANES:(p + 1) * LANES]
        vp = vv[:, p * LANES:(p + 1) * LANES]
        res = []
        for x in range(2):
            qx = jnp.where(lo_half, qp, zero) if x == 0 else jnp.where(lo_half, zero, qp)
            s = _dot_nt(qx, kp) - slope_ref[grp * DIL_HEADS + 2 * p + x] * dist
            s = jnp.where(valid, s, -jnp.inf)
            mx = jnp.max(s, axis=-1, keepdims=True)
            e = jnp.exp(s - mx)
            den = jnp.maximum(jnp.sum(e, axis=-1, keepdims=True), TINY)
            res.append((_dot((e / den).astype(BF16), vp), mx + jnp.log(den)))
        sl = slice(p * LANES, (p + 1) * LANES)
        o_ref[0, :, sl] = _pair_select(lo_half, res[0][0], res[1][0])
        lse_ref[0, :, sl] = jnp.where(lo_half, res[0][1], res[1][1])


def _dilated(slopes, q, k, v, pos_f, grp):
    window, dil = DIL_CFG[grp]
    b, s, w = q.shape
    sub = s // dil
    t = min(T_DIL, sub)
    span = window // dil
    assert span <= t
    qv = q.reshape(b, sub, dil * w)
    kv = k.reshape(b, sub, dil * w)
    vv = v.reshape(b, sub, dil * w)
    pres = pos_f.reshape(sub, dil).T
    pc = pres.reshape(dil, sub, 1)
    pr = pres.reshape(dil, sub // t, 1, t)
    kern = functools.partial(_dil_kernel, t=t, span=span, grp=grp)
    shp = jax.ShapeDtypeStruct((b, sub, dil * w), F32)
    o, lse = pl.pallas_call(
        kern,
        grid=(b, dil, sub // t),
        in_specs=[pl.BlockSpec(memory_space=pltpu.SMEM),
                  pl.BlockSpec((1, t, w), lambda i, r, j: (i, j, r)),
                  pl.BlockSpec((1, sub, w), lambda i, r, j: (i, 0, r)),
                  pl.BlockSpec((1, sub, w), lambda i, r, j: (i, 0, r)),
                  pl.BlockSpec((1, t, 1), lambda i, r, j: (r, j, 0)),
                  pl.BlockSpec((1, sub // t, 1, t), lambda i, r, j: (r, 0, 0, 0))],
        out_specs=[pl.BlockSpec((1, t, w), lambda i, r, j: (i, j, r)),
                   pl.BlockSpec((1, t, w), lambda i, r, j: (i, j, r))],
        out_shape=[shp, shp],
        compiler_params=_cparams(3),
        name="dilated_attn_g%d" % grp,
    )(slopes, qv, kv, vv, pc, pr)
    return o.reshape(b, s, w), lse.reshape(b, s, w)


def _out_odd_kernel(x_ref, mod_ref, oc_ref, os_ref, ow_ref, d0_ref, d1_ref, d2_ref,
                    l0_ref, l1_ref, l2_ref, gz_ref, w_ref, o_ref):
    gz = gz_ref[0].astype(F32)
    nsa = oc_ref[0] + os_ref[0] + ow_ref[0]
    m1 = (nsa * gz[:, 0:NSA_W]).astype(BF16)
    l0, l1, l2 = l0_ref[0], l1_ref[0], l2_ref[0]
    mx = jnp.maximum(jnp.maximum(l0, l1), l2)
    e0, e1, e2 = jnp.exp(l0 - mx), jnp.exp(l1 - mx), jnp.exp(l2 - mx)
    dil = (e0 * d0_ref[0] + e1 * d1_ref[0] + e2 * d2_ref[0]) / (e0 + e1 + e2)
    m2 = (dil * gz[:, NSA_W:NSA_W + DIL_W]).astype(BF16)
    y = _dot(m1, w_ref[0:NSA_W, :]) + _dot(m2, w_ref[NSA_W:NSA_W + DIL_W, :])
    gate = mod_ref[0][:, 2 * D_MODEL:3 * D_MODEL]
    o_ref[0] = x_ref[0] + gate * y


def _out_odd(x, mod3, oc, os_, ow, dils, lses, gz, w):
    b, s, d = x.shape
    ts = min(TS_PROJ, s)
    row = lambda n: pl.BlockSpec((1, ts, n), lambda i, j: (i, j, 0))
    return pl.pallas_call(
        _out_odd_kernel,
        grid=(b, s // ts),
        in_specs=[row(d), pl.BlockSpec((1, 1, 3 * d), lambda i, j: (i, 0, 0)),
                  row(NSA_W), row(NSA_W), row(NSA_W)] + [row(DIL_W)] * 6 + [row(1024), _full(w.shape)],
        out_specs=row(d),
        out_shape=jax.ShapeDtypeStruct((b, s, d), F32),
        compiler_params=_cparams(2),
        name="outproj_odd",
    )(x, mod3, oc, os_, ow, *dils, *lses, gz, w)


def _pad_cols(w, n):
    return jnp.pad(w, ((0, 0), (0, n - w.shape[1])))


def _pad_vec(v, n=D_MODEL):
    return jnp.pad(v, (0, n - v.shape[0]))


def _group_matrix(sizes, total):
    m = np.zeros((total, total), np.float32)
    off = 0
    for sz, on in sizes:
        if on:
            m[off:off + sz, off:off + sz] = 1.0
        off += sz
    return jnp.asarray(m, BF16)


def _swap_halves(w):
    h = w.shape[-1] // 2
    return jnp.concatenate([w[..., h:], w[..., :h]], axis=-1)


def _pack_even(w_in, norm_g, sb_qn, sb_kn, qa_g, wq_up, kva_g, wkv_up, qn, kn):
    d = w_in.shape[0]
    z = lambda n: jnp.zeros((d, n), w_in.dtype)
    kr = w_in[:, 2432:2464]
    w = jnp.concatenate([
        w_in[:, 0:1536], w_in[:, 1536:2048], w_in[:, 2464:2976], w_in[:, 2048:2304], w_in[:, 2304:2432],
        z(64), kr, z(32), z(64), _swap_halves(kr), z(32)], axis=1).astype(BF16)
    wq3 = wq_up.reshape(MLA_Q_RANK, MLA_HEADS, MLA_NOPE + MLA_ROPE)
    zq = jnp.zeros((MLA_Q_RANK, MLA_HEADS, 32), wq_up.dtype)
    wq = jnp.concatenate([wq3, zq], axis=-1).reshape(MLA_Q_RANK, MLA_HEADS * 128).astype(BF16)
    wqs = jnp.concatenate([jnp.zeros((MLA_Q_RANK, MLA_HEADS, 64), wq_up.dtype),
                           _swap_halves(wq3[..., MLA_NOPE:]), zq], axis=-1)
    wqs = wqs.reshape(MLA_Q_RANK, MLA_HEADS * 128).astype(BF16)
    wkv3 = wkv_up.reshape(MLA_KV_RANK, MLA_HEADS, MLA_NOPE + MLA_V)
    wk = jnp.concatenate([wkv3[..., :MLA_NOPE], jnp.zeros_like(wkv3[..., :MLA_NOPE])], axis=-1)
    wkv = jnp.concatenate([wk.reshape(MLA_KV_RANK, MLA_HEADS * 128),
                           wkv3[..., MLA_NOPE:].reshape(MLA_KV_RANK, MLA_HEADS * MLA_V)], axis=1).astype(BF16)
    z32 = jnp.zeros((32,), F32)
    z64 = jnp.zeros((64,), F32)
    scale = (MLA_NOPE + MLA_ROPE) ** -0.5 * LOG2E
    qg = jnp.tile(jnp.concatenate([qn, z32]), MLA_HEADS) * scale
    qgs = jnp.tile(jnp.concatenate([z64, _swap_halves(qn[MLA_NOPE:]), z32]), MLA_HEADS) * scale
    kg = jnp.tile(jnp.concatenate([kn[:MLA_NOPE], z64]), MLA_HEADS)
    krg = jnp.concatenate([z64, kn[MLA_NOPE:], z32])
    krgs = jnp.concatenate([z64, _swap_halves(kn[MLA_NOPE:]), z32])
    cnt = jnp.tile(jnp.concatenate([jnp.full((64,), 1.0 / 64), jnp.full((32,), 1.0 / 32), jnp.ones((32,))]),
                   MLA_HEADS).astype(F32)
    rows = [norm_g, _pad_vec(jnp.tile(sb_qn, SB_HEADS) * (-LOG2E * HEAD_DIM ** -0.5)), _pad_vec(jnp.tile(sb_kn, SB_HEADS)),
            _pad_vec(qa_g), _pad_vec(kva_g), qg, qgs, kg, _pad_vec(krg), _pad_vec(krgs), cnt]
    rows += [jnp.zeros((D_MODEL,), F32)] * (16 - len(rows))
    return w, wq, wqs, wkv, jnp.stack(rows).astype(F32)


def _pack_odd(w_in, norm_g, nsa_qn, nsa_kn, dil_qn, dil_kn):
    d = w_in.shape[0]

    def padded_heads(cols, n):
        c3 = cols.reshape(d, n, HEAD_DIM)
        return jnp.concatenate([c3, jnp.zeros_like(c3)], axis=-1).reshape(d, n * 128)

    def doubled_heads(cols, n):
        c3 = cols.reshape(d, n, HEAD_DIM)
        return jnp.concatenate([c3, c3], axis=-1).reshape(d, n * 128)

    ng = w_in[:, 1920:1956].reshape(d, NSA_KV_HEADS, NSA_HPG * 3)
    ng = jnp.pad(ng, ((0, 0), (0, 0), (0, 128 - NSA_HPG * 3))).reshape(d, NSA_KV_HEADS * 128)
    w_nsa = jnp.concatenate([
        padded_heads(w_in[:, 0:768], NSA_HEADS),
        padded_heads(w_in[:, 768:960], NSA_KV_HEADS), padded_heads(w_in[:, 960:1152], NSA_KV_HEADS),
        padded_heads(w_in[:, 1152:1344], NSA_KV_HEADS), padded_heads(w_in[:, 1344:1536], NSA_KV_HEADS),
        padded_heads(w_in[:, 1536:1728], NSA_KV_HEADS), padded_heads(w_in[:, 1728:1920], NSA_KV_HEADS),
        ng, w_in[:, 1956:2724], w_in[:, 5028:5284]], axis=1).astype(BF16)
    w_dil = w_in[:, 2724:5028].astype(BF16)
    z64 = jnp.zeros((64,), F32)
    n = NSA_HEADS * 128
    rows = [_pad_vec(norm_g, n),
            jnp.tile(jnp.concatenate([nsa_qn * (LOG2E * HEAD_DIM ** -0.5), z64]), NSA_HEADS),
            _pad_vec(jnp.tile(jnp.concatenate([nsa_kn, z64]), NSA_KV_HEADS), n),
            _pad_vec(jnp.tile(dil_qn, DIL_HEADS) * HEAD_DIM ** -0.5, n), _pad_vec(jnp.tile(dil_kn, DIL_HEADS), n)]
    rows += [jnp.zeros((n,), F32)] * (8 - len(rows))
    return w_nsa, w_dil, jnp.stack(rows).astype(F32)


def _pack_compress(w, pe, double):
    w3 = w.reshape(CMP_LEN, HEAD_DIM, HEAD_DIM)
    w3 = jnp.concatenate([w3, jnp.zeros_like(w3)], axis=1)
    w3 = jnp.concatenate([w3, w3 if double else jnp.zeros_like(w3)], axis=2)
    wp = w3.reshape(2, CMP_STRIDE * 128, 128).astype(BF16)
    pe2 = jnp.concatenate([pe, jnp.zeros_like(pe)], axis=1).reshape(2, 1, CMP_STRIDE * 128)
    pe2 = jnp.broadcast_to(pe2, (2, 8, CMP_STRIDE * 128)).astype(F32)
    return wp, pe2


def _alibi_slopes(n):
    return 2.0 ** (-8.0 * jnp.arange(1, n + 1, dtype=jnp.float32) / n)


def kernel(x, c, positions, ada_w, ada_b, norm_g, ev_w_in, ev_w_out, sb_qn, sb_kn, mla_qa_g, mla_wq_up,
           mla_kva_g, mla_wkv_up, mla_qn, mla_kn, od_w_in, od_w_out, nsa_qn, nsa_kn, nsa_cmp_wk, nsa_cmp_wv,
           nsa_cmp_pe_k, nsa_cmp_pe_v, dil_qn, dil_kn):
    b, s, d = x.shape
    depth = ada_w.shape[0]
    pos_f = positions.astype(F32)

    inv_freq = ROPE_BASE ** (-jnp.arange(0, MLA_ROPE, 2, dtype=F32) / MLA_ROPE)
    ang = pos_f[:, None] * inv_freq[None, :]
    cos, sin = jnp.cos(ang), jnp.sin(ang)
    cos128 = jnp.concatenate([jnp.ones((s, 64), F32), cos, cos, jnp.zeros((s, 32), F32)], axis=1)
    sin128 = jnp.concatenate([jnp.zeros((s, 64), F32), -sin, sin, jnp.zeros((s, 32), F32)], axis=1)
    nsa_slopes = _alibi_slopes(NSA_HEADS)
    dil_slopes = _alibi_slopes(N_DIL * DIL_HEADS)
    nch = s // CMP_STRIDE
    chunk_sum = pos_f.reshape(nch, CMP_STRIDE).sum(axis=1)
    cpos = ((chunk_sum + jnp.roll(chunk_sum, -1)) / CMP_LEN).reshape(1, nch)
    n_sel = s // SEL_LEN
    cst = np.arange(nch)[:, None] * CMP_STRIDE
    jst = np.arange(LANES)[None, :] * SEL_LEN
    ovl = ((cst <= jst + SEL_LEN - 1) & (cst + CMP_LEN - 1 >= jst) & (np.arange(LANES)[None, :] < n_sel))
    ovl = jnp.asarray(ovl.astype(np.float32), BF16)
    pos_col = pos_f.reshape(s, 1)
    tkn = min(TK_NSA, min(TQ_NSA, s))
    pos2d = pos_f.reshape(s // tkn, 1, tkn)

    g64 = _group_matrix([(64, 1)] * 4, 256)
    gm = _group_matrix([(64, 1), (32, 1), (32, 0)] * 2, 256)
    tks = min(TK_SB, min(TQ_SB, s))
    tri = jnp.asarray(np.tril(np.ones((tks, tks), np.float32)), BF16)

    c8 = jnp.pad(c, ((0, 8 - b), (0, 0)))
    mod_all = _modulation(c8, ada_w, ada_b)

    for layer in range(depth):
        j = layer // 2
        mod3 = mod_all[layer, :b].reshape(b, 1, 3 * d)
        if layer % 2 == 0:
            w, wq, wqs, wkv, vec = _pack_even(ev_w_in[j], norm_g[layer], sb_qn[j], sb_kn[j], mla_qa_g[j],
                                              mla_wq_up[j], mla_kva_g[j], mla_wkv_up[j], mla_qn[j], mla_kn[j])
            sbq, sbk, sbv, gz, mq, mk, mv = _k1_even(x, mod3, vec, w, wq, wqs, wkv, g64, gm, cos128, sin128)
            o_sb = _sb_attention(sbq, sbk, sbv, tri)
            o_mla = _mla_attention(mq, mk, mv)
            x = _out_even(x, mod3, o_sb, o_mla, gz, ev_w_out[j].astype(BF16))
        else:
            w_nsa, w_dil, vec = _pack_odd(od_w_in[j], norm_g[layer], nsa_qn[j], nsa_kn[j], dil_qn[j], dil_kn[j])
            q, ck, cv, sk, sv, wk, wv, gates, gz = _k1_nsa(x, mod3, vec, w_nsa, g64)
            dq, dk, dv = _k1_dil(x, mod3, vec, w_dil, g64)
            wck, pek = _pack_compress(nsa_cmp_wk[j], nsa_cmp_pe_k[j], False)
            wcv, pev = _pack_compress(nsa_cmp_wv[j], nsa_cmp_pe_v[j], True)
            kgain = jnp.concatenate([nsa_kn[j], jnp.zeros((64,), F32)]).reshape(1, 128)
            kc, vc = _compress(ck, cv, wck, wcv, pek, pev, kgain, g64[0:128, 0:128])
            o_c, selb = _nsa_cmp(nsa_slopes, q, kc, vc, ovl, cpos, pos_col, gates)
            o_s = _nsa_gqa(nsa_slopes, q, selb, sk, sv, pos2d, gates, 1)
            o_w = _nsa_gqa(nsa_slopes, q, selb, wk, wv, pos2d, gates, 2)
            dils, lses = [], []
            for g in range(N_DIL):
                o, lse = _dilated(dil_slopes, dq[g], dk[g], dv[g], pos_f, g)
                dils.append(o)
                lses.append(lse)
            x = _out_odd(x, mod3, o_c, o_s, o_w, dils, lses, gz, od_w_out[j].astype(BF16))
    return x
```

```python
import functools

import numpy as np
import jax
import jax.numpy as jnp
from jax import lax
from jax.experimental import pallas as pl
from jax.experimental.pallas import tpu as pltpu

F32 = jnp.float32
BF16 = jnp.bfloat16

D_MODEL = 1024
HEAD_DIM = 64
NORM_EPS = 1e-6
TINY = 1e-30
SB_HEADS = 8
MLA_HEADS = 8
MLA_Q_RANK = 256
MLA_KV_RANK = 128
MLA_NOPE = 64
MLA_ROPE = 32
MLA_V = 64
ROPE_BASE = 10000.0
NSA_HEADS = 12
NSA_KV_HEADS = 3
NSA_HPG = 4
CMP_LEN = 32
CMP_STRIDE = 16
SEL_LEN = 64
SEL_TOPN = 16
WIN = 512
FORCE_BONUS = 1e3
DIL_CFG = ((128, 1), (512, 4), (2048, 16))
N_DIL = 3
DIL_HEADS = 4
SB_W = SB_HEADS * HEAD_DIM
MLA_OUT = MLA_HEADS * MLA_V
NSA_W = NSA_HEADS * HEAD_DIM
NSA_KV_W = NSA_KV_HEADS * HEAD_DIM
DIL_W = DIL_HEADS * HEAD_DIM

LANES = 128
MXU_DIM = 256
NEG_MASK = -1e30
LOG2E = 1.4426950408889634
SAFE_LOGIT_BOUND = 50.0
ROUNDING_MARGIN = 1.02
SEL_OFF = 2.0 ** 30
VMEM_LIMIT = 56 * 1024 * 1024

TS_PROJ = 512
TQ_SB = 1024
TK_SB = 256
TQ_MLA = 1024
TD_MLA = 512
TQ_CMP = 256
TQ_NSA = 512
TK_NSA = 512
SEL_TILE_STRIDE = LANES * SEL_LEN // TK_NSA
T_DIL = 128


def _dot(a, b):
    return jnp.dot(a, b, preferred_element_type=F32)


def _dot_nt(a, b):
    return lax.dot_general(a, b, (((1,), (1,)), ((), ())), preferred_element_type=F32)


def _split_hl(a):
    hi = a.astype(BF16)
    lo = (a - hi.astype(F32)).astype(BF16)
    return hi, lo


def _dot_hl(a, b):
    hi, lo = _split_hl(a)
    return _dot(hi, b) + _dot(lo, b)


def _sigmoid(z):
    return 1.0 / (1.0 + jnp.exp(-z))


def _cparams(n_axes):
    return pltpu.CompilerParams(dimension_semantics=("arbitrary",) * n_axes,
                                vmem_limit_bytes=VMEM_LIMIT)


def _full(shape):
    n = len(shape)
    return pl.BlockSpec(shape, lambda *a, _n=n: (0,) * _n)


def _mod_kernel(c_ref, w_ref, b_ref, o_ref):
    c = c_ref[...]
    a = c * _sigmoid(c)
    ah, al = _split_hl(a)
    wh, wl = _split_hl(w_ref[0])
    o_ref[0] = _dot(ah, wh) + _dot(ah, wl) + _dot(al, wh) + b_ref[0]


def _modulation(c8, ada_w, ada_b):
    depth, d, n3 = ada_w.shape
    tn = 1024
    return pl.pallas_call(
        _mod_kernel,
        grid=(depth, n3 // tn),
        in_specs=[pl.BlockSpec((8, d), lambda l, j: (0, 0)),
                  pl.BlockSpec((1, d, tn), lambda l, j: (l, 0, j)),
                  pl.BlockSpec((1, 1, tn), lambda l, j: (l, 0, j))],
        out_specs=pl.BlockSpec((1, 8, tn), lambda l, j: (l, 0, j)),
        out_shape=jax.ShapeDtypeStruct((depth, 8, n3), F32),
        compiler_params=_cparams(2),
        name="adaln_mod",
    )(c8, ada_w, ada_b.reshape(depth, 1, n3))


def _modulated(x_ref, mod_ref, ng):
    x = x_ref[0]
    mod = mod_ref[0]
    shift = mod[:, 0:D_MODEL]
    scale = mod[:, D_MODEL:2 * D_MODEL]
    ms = jnp.mean(x * x, axis=-1, keepdims=True)
    h = x * lax.rsqrt(ms + NORM_EPS) * ng
    h = h * (1.0 + scale) + shift
    return h.astype(BF16)


def _group_norm_chunk(t, g, inv_cnt):
    ss = _dot_hl(t * t, g)
    return t * lax.rsqrt(ss * inv_cnt + NORM_EPS)


def _row_rms(t, gain):
    ms = jnp.mean(t * t, axis=-1, keepdims=True)
    return t * lax.rsqrt(ms + NORM_EPS) * gain


EV_COLS = 3200


def _k1_even_kernel(x_ref, mod_ref, vec_ref, w_ref, wq_ref, wqs_ref, wkv_ref, g64_ref, gm_ref,
                    cos_ref, sin_ref, sbq_o, sbk_o, sbv_o, gz_o, mq_o, mk_o, mv_o):
    hb = _modulated(x_ref, mod_ref, vec_ref[0:1, :])
    g64 = g64_ref[...]
    gm = gm_ref[...]
    inv64 = 1.0 / HEAD_DIM

    u = _dot(hb, w_ref[:, 0:1536])
    for c in range(2):
        sl = slice(c * 256, (c + 1) * 256)
        qn = _group_norm_chunk(u[:, c * 256:(c + 1) * 256], g64, inv64)
        sbq_o[0, :, sl] = (qn * vec_ref[1:2, sl]).astype(BF16)
        kn = _group_norm_chunk(u[:, 512 + c * 256:512 + (c + 1) * 256], g64, inv64)
        sbk_o[0, :, sl] = (kn * vec_ref[2:3, sl]).astype(BF16)
    sbv_o[0] = u[:, 1024:1536].astype(BF16)

    uz = _dot(hb, w_ref[:, 1536:2560])
    gz_o[0] = (uz * _sigmoid(uz)).astype(BF16)

    ul = _dot(hb, w_ref[:, 2560:3200])
    qlat = _row_rms(ul[:, 0:256], vec_ref[3:4, 0:256]).astype(BF16)
    kvlat = _row_rms(ul[:, 256:384], vec_ref[4:5, 0:128]).astype(BF16)
    krm = ul[:, 384:512]
    krs = ul[:, 512:640]
    cs = cos_ref[...]
    sn = sin_ref[...]
    cs2 = jnp.concatenate([cs, cs], axis=1)
    sn2 = jnp.concatenate([sn, sn], axis=1)

    tq = _dot(qlat, wq_ref[...])
    tqs = _dot(qlat, wqs_ref[...])
    for p in range(4):
        sl = slice(p * 256, (p + 1) * 256)
        tc = tq[:, p * 256:(p + 1) * 256]
        ss = _dot_hl(tc * tc, gm)
        inv = lax.rsqrt(ss * vec_ref[10:11, sl] + NORM_EPS)
        a = tc * inv * vec_ref[5:6, sl]
        b = tqs[:, p * 256:(p + 1) * 256] * inv * vec_ref[6:7, sl]
        mq_o[0, :, sl] = (a * cs2 + b * sn2).astype(BF16)

    sskr = _dot_hl(krm * krm, gm[0:128, 0:128])
    invr = lax.rsqrt(sskr * vec_ref[10:11, 0:128] + NORM_EPS)
    kr = (krm * invr * vec_ref[8:9, 0:128]) * cs + (krs * invr * vec_ref[9:10, 0:128]) * sn
    kr2 = jnp.concatenate([kr, kr], axis=1)

    kv = _dot(kvlat, wkv_ref[...])
    for p in range(4):
        sl = slice(p * 256, (p + 1) * 256)
        kn = _group_norm_chunk(kv[:, p * 256:(p + 1) * 256], g64, inv64)
        mk_o[0, :, sl] = (kn * vec_ref[7:8, sl] + kr2).astype(BF16)
    mv_o[0] = kv[:, 1024:1536].astype(BF16)


def _k1_even(x, mod3, vec, w, wq, wqs, wkv, g64, gm, cos128, sin128):
    b, s, d = x.shape
    ts = min(TS_PROJ, s)
    row = lambda n: pl.BlockSpec((1, ts, n), lambda i, j: (i, j, 0))
    outs = [(512, BF16), (512, BF16), (512, BF16), (1024, BF16), (1024, BF16), (1024, BF16), (512, BF16)]
    return pl.pallas_call(
        _k1_even_kernel,
        grid=(b, s // ts),
        in_specs=[row(d),
                  pl.BlockSpec((1, 1, 3 * d), lambda i, j: (i, 0, 0)),
                  _full(vec.shape), _full(w.shape), _full(wq.shape), _full(wqs.shape), _full(wkv.shape),
                  _full(g64.shape), _full(gm.shape),
                  pl.BlockSpec((ts, 128), lambda i, j: (j, 0)),
                  pl.BlockSpec((ts, 128), lambda i, j: (j, 0))],
        out_specs=[row(n) for n, _ in outs],
        out_shape=[jax.ShapeDtypeStruct((b, s, n), dt) for n, dt in outs],
        compiler_params=_cparams(2),
        name="inproj_even",
    )(x, mod3, vec, w, wq, wqs, wkv, g64, gm, cos128, sin128)


def _sb_kernel(q_ref, k_ref, v_ref, tri_ref, o_ref, acc_ref, car_ref, *, tq, tk):
    qi = pl.program_id(2)
    q = q_ref[0]
    lane = lax.broadcasted_iota(jnp.int32, (1, LANES), 1)
    lo_half = lane < HEAD_DIM
    zero = jnp.zeros_like(q)
    qs = (jnp.where(lo_half, q, zero), jnp.where(lo_half, zero, q))
    tri = tri_ref[...]
    acc_ref[...] = jnp.zeros_like(acc_ref)
    car_ref[...] = jnp.zeros_like(car_ref)
    col0 = lax.broadcasted_iota(jnp.int32, (1, tk), 1)

    def tile(kt, r0, masked):
        n = tq - r0
        off = pl.multiple_of(kt * tk, tk)
        k = k_ref[0, pl.ds(off, tk), :]
        v = v_ref[0, pl.ds(off, tk), :]
        if masked:
            rows = qi * tq + r0 + lax.broadcasted_iota(jnp.int32, (n, 1), 0)
            strict = (kt * tk + col0) < rows
        for i in range(2):
            nz = _dot_nt(qs[i][r0:tq], k)
            soft = jnp.log(1.0 + jnp.exp2(-jnp.abs(nz))) * LOG2E
            lom = jnp.minimum(nz, 0.0) - soft
            if masked:
                lom = jnp.where(strict, lom, 0.0)
            tt = _dot(lom.astype(BF16), tri)
            car = car_ref[i, r0:tq]
            w = jnp.exp2(tt - nz + jnp.concatenate([car] * (tk // LANES), axis=1))
            if masked:
                w = jnp.where(strict, w, 0.0)
            acc_ref[i, r0:tq] += _dot(w.astype(BF16), v)
            car_ref[i, r0:tq] = car + jnp.broadcast_to(tt[:, 0:1], (n, LANES))

    per = tq // tk
    for c in reversed(range(per)):
        tile(qi * per + c, c * tk, True)

    def body(j, carry):
        for u in range(per):
            tile(qi * per - 1 - per * j - u, 0, False)
        return carry

    lax.fori_loop(0, qi, body, 0)
    o_ref[0] = jnp.where(lo_half, acc_ref[0], acc_ref[1])


def _sb_attention(q, k, v, tri):
    b, s, w = q.shape
    tq = min(TQ_SB, s)
    tk = min(TK_SB, tq)
    kern = functools.partial(_sb_kernel, tq=tq, tk=tk)
    return pl.pallas_call(
        kern,
        grid=(b, w // LANES, s // tq),
        in_specs=[pl.BlockSpec((1, tq, LANES), lambda i, p, j: (i, j, p)),
                  pl.BlockSpec((1, s, LANES), lambda i, p, j: (i, 0, p)),
                  pl.BlockSpec((1, s, LANES), lambda i, p, j: (i, 0, p)),
                  _full(tri.shape)],
        out_specs=pl.BlockSpec((1, tq, LANES), lambda i, p, j: (i, j, p)),
        out_shape=jax.ShapeDtypeStruct((b, s, w), F32),
        scratch_shapes=[pltpu.VMEM((2, tq, LANES), F32), pltpu.VMEM((2, tq, LANES), F32)],
        compiler_params=_cparams(3),
        name="stickbreak_attn",
    )(q, k, v, tri)


def _lane_max(s):
    m = s[:, 0:LANES]
    for c in range(1, s.shape[1] // LANES):
        m = jnp.maximum(m, s[:, c * LANES:(c + 1) * LANES])
    return m


def _mla_kernel(bnd_ref, q_ref, k_ref, v_ref, o_ref, acc_ref, m_ref, *, tq, td, bounded):
    qi = pl.program_id(2)
    q = q_ref[0]
    qs = (q[:, 0:LANES], q[:, LANES:2 * LANES])
    lane = lax.broadcasted_iota(jnp.int32, (1, LANES), 1)
    lo_half = lane < HEAD_DIM
    acc_ref[...] = jnp.zeros_like(acc_ref)
    if bounded:
        m_ref[...] = jnp.full(m_ref.shape, bnd_ref[0], F32)
    else:
        m_ref[...] = jnp.full_like(m_ref, NEG_MASK)

    def scores(off, width, r0, masked):
        k = k_ref[0, pl.ds(off, width), :]
        if masked:
            rows = qi * tq + r0 + lax.broadcasted_iota(jnp.int32, (tq - r0, 1), 0)
            keep = (off + lax.broadcasted_iota(jnp.int32, (1, width), 1)) <= rows
        out = []
        for i in range(2):
            s = _dot_nt(qs[i][r0:tq], k[:, i * LANES:(i + 1) * LANES])
            if masked:
                s = jnp.where(keep, s, NEG_MASK)
            out.append(s)
        return out

    def sweep(fn):
        def body(j, carry):
            fn(pl.multiple_of(j * tq, tq), tq, 0, False)
            return carry
        lax.fori_loop(0, qi, body, 0)
        for c in range(tq // td):
            fn(pl.multiple_of(qi * tq + c * td, td), td, c * td, True)

    def row_max(off, width, r0, masked):
        for i, s in enumerate(scores(off, width, r0, masked)):
            m_ref[i, r0:tq] = jnp.maximum(m_ref[i, r0:tq], _lane_max(s))

    def accumulate(off, width, r0, masked):
        vx = jnp.concatenate([v_ref[0, pl.ds(off, width), :], jnp.ones((width, LANES), BF16)], axis=1)
        for i, s in enumerate(scores(off, width, r0, masked)):
            p = jnp.exp2(s - jnp.concatenate([m_ref[i, r0:tq]] * (width // LANES), axis=1))
            acc_ref[i, r0:tq] += _dot(p.astype(BF16), vx)

    if not bounded:
        sweep(row_max)
        for i in range(2):
            m_ref[i] = jnp.broadcast_to(jnp.max(m_ref[i], axis=-1, keepdims=True), (tq, LANES))
    sweep(accumulate)
    o_ref[0] = jnp.where(lo_half, acc_ref[0, :, 0:LANES] / acc_ref[0, :, LANES:2 * LANES],
                         acc_ref[1, :, 0:LANES] / acc_ref[1, :, LANES:2 * LANES])


def _mla_attention(q, k, v, bound):
    b, s, _ = q.shape
    tq = min(TQ_MLA, s)
    td = min(TD_MLA, tq)

    def call(bounded):
        kern = functools.partial(_mla_kernel, tq=tq, td=td, bounded=bounded)
        return pl.pallas_call(
            kern,
            grid=(b, MLA_HEADS // 2, s // tq),
            in_specs=[pl.BlockSpec(memory_space=pltpu.SMEM),
                      pl.BlockSpec((1, tq, 2 * LANES), lambda i, p, j: (i, j, p)),
                      pl.BlockSpec((1, s, 2 * LANES), lambda i, p, j: (i, 0, p)),
                      pl.BlockSpec((1, s, LANES), lambda i, p, j: (i, 0, p))],
            out_specs=pl.BlockSpec((1, tq, LANES), lambda i, p, j: (i, j, p)),
            out_shape=jax.ShapeDtypeStruct((b, s, MLA_OUT), F32),
            scratch_shapes=[pltpu.VMEM((2, tq, 2 * LANES), F32), pltpu.VMEM((2, tq, LANES), F32)],
            compiler_params=_cparams(3),
            name="mla_attn_bounded" if bounded else "mla_attn",
        )(bound, q, k, v)

    return lax.cond(bound[0] <= SAFE_LOGIT_BOUND, lambda: call(True), lambda: call(False))


def _out_even_kernel(x_ref, mod_ref, osb_ref, omla_ref, gz_ref, w_ref, o_ref):
    gz = gz_ref[0].astype(F32)
    m1 = (osb_ref[0] * gz[:, 0:SB_W]).astype(BF16)
    m2 = (omla_ref[0] * gz[:, SB_W:SB_W + MLA_OUT]).astype(BF16)
    y = _dot(m1, w_ref[0:SB_W, :]) + _dot(m2, w_ref[SB_W:SB_W + MLA_OUT, :])
    gate = mod_ref[0][:, 2 * D_MODEL:3 * D_MODEL]
    o_ref[0] = x_ref[0] + gate * y


def _out_even(x, mod3, osb, omla, gz, w):
    b, s, d = x.shape
    ts = min(TS_PROJ, s)
    row = lambda n: pl.BlockSpec((1, ts, n), lambda i, j: (i, j, 0))
    return pl.pallas_call(
        _out_even_kernel,
        grid=(b, s // ts),
        in_specs=[row(d), pl.BlockSpec((1, 1, 3 * d), lambda i, j: (i, 0, 0)),
                  row(SB_W), row(MLA_OUT), row(SB_W + MLA_OUT), _full(w.shape)],
        out_specs=row(d),
        out_shape=jax.ShapeDtypeStruct((b, s, d), F32),
        compiler_params=_cparams(2),
        name="outproj_even",
    )(x, mod3, osb, omla, gz, w)


NSA_COLS = 5248


def _k1_nsa_kernel(x_ref, mod_ref, vec_ref, w_ref, g64_ref,
                   q_o, ck_o, cv_o, sk_o, sv_o, wk_o, wv_o, gt_o, gz_o):
    hb = _modulated(x_ref, mod_ref, vec_ref[0:1, 0:D_MODEL])
    g64 = g64_ref[...]
    inv64 = 1.0 / HEAD_DIM

    uq = _dot(hb, w_ref[:, 0:1536])
    for c in range(6):
        sl = slice(c * 256, (c + 1) * 256)
        qn = _group_norm_chunk(uq[:, c * 256:(c + 1) * 256], g64, inv64)
        q_o[0, :, sl] = (qn * vec_ref[1:2, sl]).astype(BF16)

    uc = _dot(hb, w_ref[:, 1536:2304])
    for g in range(NSA_KV_HEADS):
        ck_o[0, g] = uc[:, g * 128:(g + 1) * 128].astype(BF16)
        cv_o[0, g] = uc[:, 384 + g * 128:384 + (g + 1) * 128].astype(BF16)

    us = _dot(hb, w_ref[:, 2304:3840])
    kgain = vec_ref[2:3, 0:384]
    g128 = g64[0:128, 0:128]
    for g in range(NSA_KV_HEADS):
        sl = slice(g * 128, (g + 1) * 128)
        t = us[:, g * 128:(g + 1) * 128]
        ss = _dot_hl(t * t, g128)
        sk_o[0, :, sl] = (t * lax.rsqrt(ss * inv64 + NORM_EPS) * kgain[:, sl]).astype(BF16)
        t = us[:, 768 + g * 128:768 + (g + 1) * 128]
        ss = _dot_hl(t * t, g128)
        wk_o[0, :, sl] = (t * lax.rsqrt(ss * inv64 + NORM_EPS) * kgain[:, sl]).astype(BF16)
    ones_hi = (lax.broadcasted_iota(jnp.int32, (1, 384), 1) % 128) >= HEAD_DIM
    sv_o[0] = jnp.where(ones_hi, 1.0, us[:, 384:768]).astype(BF16)
    wv_o[0] = jnp.where(ones_hi, 1.0, us[:, 1152:1536]).astype(BF16)

    ug = _dot(hb, w_ref[:, 3840:4224])
    gt_o[0] = _sigmoid(ug)
    uz = _dot(hb, w_ref[:, 4224:5248])
    gz_o[0] = (uz * _sigmoid(uz)).astype(BF16)


def _k1_nsa(x, mod3, vec, w, g64):
    b, s, d = x.shape
    ts = min(TS_PROJ, s)
    row = lambda n: pl.BlockSpec((1, ts, n), lambda i, j: (i, j, 0))
    grp = pl.BlockSpec((1, NSA_KV_HEADS, ts, 128), lambda i, j: (i, 0, j, 0))
    return pl.pallas_call(
        _k1_nsa_kernel,
        grid=(b, s // ts),
        in_specs=[row(d), pl.BlockSpec((1, 1, 3 * d), lambda i, j: (i, 0, 0)),
                  _full(vec.shape), _full(w.shape), _full(g64.shape)],
        out_specs=[row(1536), grp, grp, row(384), row(384), row(384), row(384), row(384), row(1024)],
        out_shape=[jax.ShapeDtypeStruct((b, s, 1536), BF16),
                   jax.ShapeDtypeStruct((b, NSA_KV_HEADS, s, 128), BF16),
                   jax.ShapeDtypeStruct((b, NSA_KV_HEADS, s, 128), BF16),
                   jax.ShapeDtypeStruct((b, s, 384), BF16),
                   jax.ShapeDtypeStruct((b, s, 384), BF16),
                   jax.ShapeDtypeStruct((b, s, 384), BF16),
                   jax.ShapeDtypeStruct((b, s, 384), BF16),
                   jax.ShapeDtypeStruct((b, s, 384), F32),
                   jax.ShapeDtypeStruct((b, s, 1024), BF16)],
        compiler_params=_cparams(2),
        name="inproj_nsa",
    )(x, mod3, vec, w, g64)


def _k1_dil_kernel(x_ref, mod_ref, vec_ref, w_ref, g64_ref, q_o, k_o, v_o):
    hb = _modulated(x_ref, mod_ref, vec_ref[0:1, 0:D_MODEL])
    g64 = g64_ref[...]
    inv64 = 1.0 / HEAD_DIM
    u = _dot(hb, w_ref[...])
    for g in range(N_DIL):
        qn = _group_norm_chunk(u[:, g * 256:(g + 1) * 256], g64, inv64)
        q_o[g, 0] = (qn * vec_ref[3:4, 0:256]).astype(BF16)
        kn = _group_norm_chunk(u[:, 768 + g * 256:768 + (g + 1) * 256], g64, inv64)
        k_o[g, 0] = (kn * vec_ref[4:5, 0:256]).astype(BF16)
        v_o[g, 0] = u[:, 1536 + g * 256:1536 + (g + 1) * 256].astype(BF16)


def _k1_dil(x, mod3, vec, w, g64):
    b, s, d = x.shape
    ts = min(TS_PROJ, s)
    row = lambda n: pl.BlockSpec((1, ts, n), lambda i, j: (i, j, 0))
    grp = pl.BlockSpec((N_DIL, 1, ts, 256), lambda i, j: (0, i, j, 0))
    shp = jax.ShapeDtypeStruct((N_DIL, b, s, 256), BF16)
    return pl.pallas_call(
        _k1_dil_kernel,
        grid=(b, s // ts),
        in_specs=[row(d), pl.BlockSpec((1, 1, 3 * d), lambda i, j: (i, 0, 0)),
                  _full(vec.shape), _full(w.shape), _full(g64.shape)],
        out_specs=[grp, grp, grp],
        out_shape=[shp, shp, shp],
        compiler_params=_cparams(2),
        name="inproj_dil",
    )(x, mod3, vec, w, g64)


def _compress_kernel(xk_ref, xv_ref, wk_ref, wv_ref, pek_ref, pev_ref, gain_ref, g128_ref, kc_o, vc_o):
    nch = xk_ref.shape[2]

    def comp(x_ref, w_ref, pe_ref):
        x = x_ref[0, 0]
        lo = _dot(x, w_ref[0])
        hi = _dot(x, w_ref[1])
        pec = _dot_hl(pe_ref[0], w_ref[0]) + _dot_hl(pe_ref[1], w_ref[1])
        return lo + pltpu.roll(hi, nch - 1, 0) + pec[0:1, :]

    kc = comp(xk_ref, wk_ref, pek_ref)
    ss = _dot_hl(kc * kc, g128_ref[...])
    kc_o[0, 0] = (kc * lax.rsqrt(ss * (1.0 / HEAD_DIM) + NORM_EPS) * gain_ref[...]).astype(BF16)
    vc_o[0, 0] = comp(xv_ref, wv_ref, pev_ref).astype(BF16)


def _compress(ck, cv, wk, wv, pek, pev, gain, g128):
    b, g, s, _ = ck.shape
    nch = s // CMP_STRIDE
    xk = ck.reshape(b, g, nch, CMP_STRIDE * 128)
    xv = cv.reshape(b, g, nch, CMP_STRIDE * 128)
    blk = pl.BlockSpec((1, 1, nch, CMP_STRIDE * 128), lambda i, j: (i, j, 0, 0))
    oblk = pl.BlockSpec((1, 1, nch, 128), lambda i, j: (i, j, 0, 0))
    shp = jax.ShapeDtypeStruct((b, g, nch, 128), BF16)
    return pl.pallas_call(
        _compress_kernel,
        grid=(b, g),
        in_specs=[blk, blk, _full(wk.shape), _full(wv.shape), _full(pek.shape), _full(pev.shape),
                  _full(gain.shape), _full(g128.shape)],
        out_specs=[oblk, oblk],
        out_shape=[shp, shp],
        compiler_params=_cparams(2),
        name="nsa_compress",
    )(xk, xv, wk, wv, pek, pev, gain, g128)


def _pair_select(lo_half, a, b):
    return jnp.where(lo_half, a, b)


def _nsa_cmp_kernel(slope_ref, q_ref, kc_ref, vc_ref, ovl_ref, cpos_ref, pq_ref, gt_ref,
                    oc_o, selb_o, any_o, *, tq):
    g = pl.program_id(1)
    qi = pl.program_id(2)
    q4 = q_ref[0]
    kc = kc_ref[0, 0]
    vc = vc_ref[0, 0]
    nch = kc.shape[0]
    lane = lax.broadcasted_iota(jnp.int32, (1, LANES), 1)
    lo_half = lane < HEAD_DIM
    t = qi * tq + lax.broadcasted_iota(jnp.int32, (tq, 1), 0)
    n_id = lax.broadcasted_iota(jnp.int32, (1, nch), 1)
    valid = (n_id * CMP_STRIDE + (CMP_LEN - 1)) <= t
    dist = (pq_ref[...] - cpos_ref[...]) * LOG2E
    gt = gt_ref[0]
    psum = jnp.zeros((tq, nch), F32)
    outs = []
    for hh in range(NSA_HPG):
        s = _dot_nt(q4[:, hh * LANES:(hh + 1) * LANES], kc) - slope_ref[g * NSA_HPG + hh] * dist
        s = jnp.where(valid, s, -jnp.inf)
        mx = jnp.max(s, axis=-1, keepdims=True)
        mx = jnp.where(mx == -jnp.inf, 0.0, mx)
        e = jnp.exp2(s - mx)
        den = jnp.maximum(jnp.sum(e, axis=-1, keepdims=True), TINY)
        p = e / den
        psum = psum + p
        outs.append(_dot(p.astype(BF16), vc) * gt[:, 3 * hh:3 * hh + 1])
    oc_o[0, :, 0:LANES] = _pair_select(lo_half, outs[0], outs[1])
    oc_o[0, :, LANES:2 * LANES] = _pair_select(lo_half, outs[2], outs[3])

    imp = _dot_hl(psum, ovl_ref[...])
    cur = t >> 6
    forced = (lane == 0) | (lane == cur) | (lane == cur - 1)
    allowed = lane <= cur
    score = jnp.where(allowed, imp + jnp.where(forced, FORCE_BONUS, 0.0), -jnp.inf)

    lane_f = lane.astype(F32)

    def pick(_, carry):
        sc, sel = carry
        mx = jnp.max(sc, axis=-1, keepdims=True)
        idx = jnp.min(jnp.where(sc == mx, lane_f, float(LANES)), axis=-1, keepdims=True)
        hit = lane_f == idx
        return jnp.where(hit, -jnp.inf, sc), jnp.where(hit, 1.0, sel)

    _, sel = lax.fori_loop(0, SEL_TOPN, pick, (score, jnp.zeros((tq, LANES), F32)))
    chosen = (sel > 0.5) & allowed
    selb_o[0, 0] = jnp.where(chosen, 0.0, -SEL_OFF).astype(BF16)
    used = jnp.max(jnp.where(chosen, 1.0, 0.0), axis=0, keepdims=True)
    any_o[0, 0, 0] = jnp.broadcast_to(used, (8, LANES))


def _nsa_cmp(slopes, q, kc, vc, ovl, cpos, pos_col, gates):
    b, s, _ = q.shape
    nch = kc.shape[2]
    tq = min(TQ_CMP, s)
    kern = functools.partial(_nsa_cmp_kernel, tq=tq)
    return pl.pallas_call(
        kern,
        grid=(b, NSA_KV_HEADS, s // tq),
        in_specs=[pl.BlockSpec(memory_space=pltpu.SMEM),
                  pl.BlockSpec((1, tq, 4 * LANES), lambda i, g, j: (i, j, g)),
                  pl.BlockSpec((1, 1, nch, LANES), lambda i, g, j: (i, g, 0, 0)),
                  pl.BlockSpec((1, 1, nch, LANES), lambda i, g, j: (i, g, 0, 0)),
                  _full(ovl.shape), _full(cpos.shape),
                  pl.BlockSpec((tq, 1), lambda i, g, j: (j, 0)),
                  pl.BlockSpec((1, tq, LANES), lambda i, g, j: (i, j, g))],
        out_specs=[pl.BlockSpec((1, tq, 2 * LANES), lambda i, g, j: (i, j, g)),
                   pl.BlockSpec((1, 1, tq, LANES), lambda i, g, j: (i, g, j, 0)),
                   pl.BlockSpec((1, 1, 1, 8, LANES), lambda i, g, j: (i, g, j, 0, 0))],
        out_shape=[jax.ShapeDtypeStruct((b, s, NSA_W), F32),
                   jax.ShapeDtypeStruct((b, NSA_KV_HEADS, s, LANES), BF16),
                   jax.ShapeDtypeStruct((b, NSA_KV_HEADS, s // tq, 8, LANES), F32)],
        compiler_params=_cparams(3),
        name="nsa_cmp_topk",
    )(slopes, q, kc, vc, ovl, cpos, pos_col, gates)


def _gqa_kernel(flag_ref, slope_ref, bnd_ref, q_ref, selb_ref, k_ref, v_ref, pos_ref, pcol_ref, gt_ref, o_ref,
                qa_ref, acc_ref, m_ref, *, tq, tk, branch, bounded):
    bi = pl.program_id(0)
    g = pl.program_id(1)
    qi = pl.program_id(2)
    q4 = q_ref[0]
    lane = lax.broadcasted_iota(jnp.int32, (1, LANES), 1)
    lo_half = lane < HEAD_DIM
    pref = pos_ref[qi * (tq // tk)][:, 0:1]
    for hh in range(NSA_HPG):
        sl = slice(hh * tq, (hh + 1) * tq)
        qa_ref[sl, 0:LANES] = q4[:, hh * LANES:(hh + 1) * LANES]
        if branch == 1:
            qa_ref[sl, LANES:2 * LANES] = selb_ref[0, 0]
        if bounded:
            own = (pcol_ref[...] - pref) * (LOG2E * slope_ref[g * NSA_HPG + hh]) + bnd_ref[0]
            m_ref[sl] = jnp.broadcast_to(own, (tq, LANES))
    acc_ref[...] = jnp.zeros_like(acc_ref)
    if not bounded:
        m_ref[...] = jnp.full_like(m_ref, NEG_MASK)
    rows = qi * tq + lax.broadcasted_iota(jnp.int32, (tq, 1), 0)
    col0 = lax.broadcasted_iota(jnp.int32, (1, tk), 1)
    krow = lax.broadcasted_iota(jnp.int32, (tk, 1), 0)
    per = tq // tk
    nrep = tk // LANES
    flag0 = ((bi * NSA_KV_HEADS + g) * pl.num_programs(2) + qi) * SEL_TILE_STRIDE

    def scores(kt_true, masked):
        kt = jnp.maximum(kt_true, 0)
        off = pl.multiple_of(kt * tk, tk)
        k = k_ref[0, pl.ds(off, tk), :]
        if branch == 1:
            blk = (kt * tk + krow) >> 6
            onehot = jnp.where(lane == blk, 1.0, 0.0).astype(BF16)
            s_all = _dot_nt(qa_ref[...], jnp.concatenate([k, onehot], axis=1))
        else:
            s_all = _dot_nt(qa_ref[:, 0:LANES], k)
        rel = (pos_ref[kt] - pref) * LOG2E
        if masked:
            cols = kt_true * tk + col0
            d = rows - cols
            keep = d >= 0
            if branch == 2:
                keep = keep & (d < WIN) & (cols >= 0)
        out = []
        for hh in range(NSA_HPG):
            s = s_all[hh * tq:(hh + 1) * tq] + slope_ref[g * NSA_HPG + hh] * rel
            if masked:
                s = jnp.where(keep, s, NEG_MASK)
            out.append(s)
        return out

    def sweep(fn):
        if branch == 1:
            def body(j, carry):
                @pl.when(flag_ref[flag0 + j] != 0)
                def _():
                    fn(j, False)
                return carry
            lax.fori_loop(0, qi * per, body, 0)
            for dd in range(per):
                fn(qi * per + dd, True)
        else:
            for dd in range(per + WIN // tk):
                fn((qi + 1) * per - 1 - dd, True)

    def row_max(kt_true, masked):
        for hh, s in enumerate(scores(kt_true, masked)):
            sl = slice(hh * tq, (hh + 1) * tq)
            m_ref[sl] = jnp.maximum(m_ref[sl], _lane_max(s))

    def accumulate(kt_true, masked):
        off = pl.multiple_of(jnp.maximum(kt_true, 0) * tk, tk)
        v = v_ref[0, pl.ds(off, tk), :]
        for hh, s in enumerate(scores(kt_true, masked)):
            sl = slice(hh * tq, (hh + 1) * tq)
            p = jnp.exp2(s - jnp.concatenate([m_ref[sl]] * nrep, axis=1))
            acc_ref[sl] += _dot(p.astype(BF16), v)

    if not bounded:
        sweep(row_max)
        for hh in range(NSA_HPG):
            sl = slice(hh * tq, (hh + 1) * tq)
            m_ref[sl] = jnp.broadcast_to(jnp.max(m_ref[sl], axis=-1, keepdims=True), (tq, LANES))
    sweep(accumulate)

    gt = gt_ref[0]
    for pr in range(2):
        res = []
        for x in range(2):
            hh = 2 * pr + x
            a = acc_ref[hh * tq:(hh + 1) * tq]
            r = pltpu.roll(a, HEAD_DIM, 1)
            o = a / r if x == 0 else r / a
            res.append(o * gt[:, 3 * hh + branch:3 * hh + branch + 1])
        o_ref[0, :, pr * LANES:(pr + 1) * LANES] = _pair_select(lo_half, res[0], res[1])


def _nsa_gqa(flags, slopes, bound, safe, q, selb, k, v, pos2d, pos_col, gates, branch):
    b, s, _ = q.shape
    tq = min(TQ_NSA, s)
    tk = min(TK_NSA, tq)

    def call(bounded):
        kern = functools.partial(_gqa_kernel, tq=tq, tk=tk, branch=branch, bounded=bounded)
        smem = pl.BlockSpec(memory_space=pltpu.SMEM)
        name = ("nsa_sel_attn" if branch == 1 else "nsa_win_attn") + ("_bounded" if bounded else "")
        return pl.pallas_call(
            kern,
            grid=(b, NSA_KV_HEADS, s // tq),
            in_specs=[smem, smem, smem,
                      pl.BlockSpec((1, tq, 4 * LANES), lambda i, g, j: (i, j, g)),
                      pl.BlockSpec((1, 1, tq, LANES), lambda i, g, j: (i, g, j, 0)),
                      pl.BlockSpec((1, s, LANES), lambda i, g, j: (i, 0, g)),
                      pl.BlockSpec((1, s, LANES), lambda i, g, j: (i, 0, g)),
                      _full(pos2d.shape),
                      pl.BlockSpec((tq, 1), lambda i, g, j: (j, 0)),
                      pl.BlockSpec((1, tq, LANES), lambda i, g, j: (i, j, g))],
            out_specs=pl.BlockSpec((1, tq, 2 * LANES), lambda i, g, j: (i, j, g)),
            out_shape=jax.ShapeDtypeStruct((b, s, NSA_W), F32),
            scratch_shapes=[pltpu.VMEM((NSA_HPG * tq, 2 * LANES), BF16),
                            pltpu.VMEM((NSA_HPG * tq, LANES), F32),
                            pltpu.VMEM((NSA_HPG * tq, LANES), F32)],
            compiler_params=_cparams(3),
            name=name,
        )(flags, slopes, bound, q, selb, k, v, pos2d, pos_col, gates)

    return lax.cond(safe, lambda: call(True), lambda: call(False))


def _dil_kernel(slope_ref, q_ref, k_ref, v_ref, pc_ref, pr_ref, o_ref, lse_ref, *, t, span, grp):
    i = pl.program_id(2)
    q4 = q_ref[0]
    lane = lax.broadcasted_iota(jnp.int32, (1, LANES), 1)
    lo_half = lane < HEAD_DIM
    prev = jnp.maximum(i - 1, 0)
    o_prev = pl.multiple_of(prev * t, t)
    o_cur = pl.multiple_of(i * t, t)
    kk = jnp.concatenate([k_ref[0, pl.ds(o_prev, t), :], k_ref[0, pl.ds(o_cur, t), :]], axis=0)
    vv = jnp.concatenate([v_ref[0, pl.ds(o_prev, t), :], v_ref[0, pl.ds(o_cur, t), :]], axis=0)
    pk = jnp.concatenate([pr_ref[0, prev], pr_ref[0, i]], axis=1)
    dist = pc_ref[0] - pk
    rows = i * t + lax.broadcasted_iota(jnp.int32, (t, 1), 0)
    cols = (i - 1) * t + lax.broadcasted_iota(jnp.int32, (1, 2 * t), 1)
    d = rows - cols
    valid = (cols >= 0) & (d >= 0) & (d <= span)
    zero = jnp.zeros((t, LANES), BF16)
    for p in range(2):
        qp = q4[:, p * LANES:(p + 1) * LANES]
        kp = kk[:, p * LANES:(p + 1) * LANES]
        vp = vv[:, p * LANES:(p + 1) * LANES]
        res = []
        for x in range(2):
            qx = jnp.where(lo_half, qp, zero) if x == 0 else jnp.where(lo_half, zero, qp)
            s = _dot_nt(qx, kp) - slope_ref[grp * DIL_HEADS + 2 * p + x] * dist
            s = jnp.where(valid, s, -jnp.inf)
            mx = jnp.max(s, axis=-1, keepdims=True)
            e = jnp.exp(s - mx)
            den = jnp.maximum(jnp.sum(e, axis=-1, keepdims=True), TINY)
            res.append((_dot((e / den).astype(BF16), vp), mx + jnp.log(den)))
        sl = slice(p * LANES, (p + 1) * LANES)
        o_ref[0, :, sl] = _pair_select(lo_half, res[0][0], res[1][0])
        lse_ref[0, :, sl] = jnp.where(lo_half, res[0][1], res[1][1])


def _dilated(slopes, q, k, v, pos_f, grp):
    window, dil = DIL_CFG[grp]
    b, s, w = q.shape
    sub = s // dil
    t = min(T_DIL, sub)
    span = window // dil
    assert span <= t
    qv = q.reshape(b, sub, dil * w)
    kv = k.reshape(b, sub, dil * w)
    vv = v.reshape(b, sub, dil * w)
    pres = pos_f.reshape(sub, dil).T
    pc = pres.reshape(dil, sub, 1)
    pr = pres.reshape(dil, sub // t, 1, t)
    kern = functools.partial(_dil_kernel, t=t, span=span, grp=grp)
    shp = jax.ShapeDtypeStruct((b, sub, dil * w), F32)
    o, lse = pl.pallas_call(
        kern,
        grid=(b, dil, sub // t),
        in_specs=[pl.BlockSpec(memory_space=pltpu.SMEM),
                  pl.BlockSpec((1, t, w), lambda i, r, j: (i, j, r)),
                  pl.BlockSpec((1, sub, w), lambda i, r, j: (i, 0, r)),
                  pl.BlockSpec((1, sub, w), lambda i, r, j: (i, 0, r)),
                  pl.BlockSpec((1, t, 1), lambda i, r, j: (r, j, 0)),
                  pl.BlockSpec((1, sub // t, 1, t), lambda i, r, j: (r, 0, 0, 0))],
        out_specs=[pl.BlockSpec((1, t, w), lambda i, r, j: (i, j, r)),
                   pl.BlockSpec((1, t, w), lambda i, r, j: (i, j, r))],
        out_shape=[shp, shp],
        compiler_params=_cparams(3),
        name="dilated_attn_g%d" % grp,
    )(slopes, qv, kv, vv, pc, pr)
    return o.reshape(b, s, w), lse.reshape(b, s, w)


def _out_odd_kernel(x_ref, mod_ref, oc_ref, os_ref, ow_ref, d0_ref, d1_ref, d2_ref,
                    l0_ref, l1_ref, l2_ref, gz_ref, w_ref, o_ref):
    gz = gz_ref[0].astype(F32)
    nsa = oc_ref[0] + os_ref[0] + ow_ref[0]
    m1 = (nsa * gz[:, 0:NSA_W]).astype(BF16)
    l0, l1, l2 = l0_ref[0], l1_ref[0], l2_ref[0]
    mx = jnp.maximum(jnp.maximum(l0, l1), l2)
    e0, e1, e2 = jnp.exp(l0 - mx), jnp.exp(l1 - mx), jnp.exp(l2 - mx)
    dil = (e0 * d0_ref[0] + e1 * d1_ref[0] + e2 * d2_ref[0]) / (e0 + e1 + e2)
    m2 = (dil * gz[:, NSA_W:NSA_W + DIL_W]).astype(BF16)
    y = _dot(m1, w_ref[0:NSA_W, :]) + _dot(m2, w_ref[NSA_W:NSA_W + DIL_W, :])
    gate = mod_ref[0][:, 2 * D_MODEL:3 * D_MODEL]
    o_ref[0] = x_ref[0] + gate * y


def _out_odd(x, mod3, oc, os_, ow, dils, lses, gz, w):
    b, s, d = x.shape
    ts = min(TS_PROJ, s)
    row = lambda n: pl.BlockSpec((1, ts, n), lambda i, j: (i, j, 0))
    return pl.pallas_call(
        _out_odd_kernel,
        grid=(b, s // ts),
        in_specs=[row(d), pl.BlockSpec((1, 1, 3 * d), lambda i, j: (i, 0, 0)),
                  row(NSA_W), row(NSA_W), row(NSA_W)] + [row(DIL_W)] * 6 + [row(1024), _full(w.shape)],
        out_specs=row(d),
        out_shape=jax.ShapeDtypeStruct((b, s, d), F32),
        compiler_params=_cparams(2),
        name="outproj_odd",
    )(x, mod3, oc, os_, ow, *dils, *lses, gz, w)


def _pad_cols(w, n):
    return jnp.pad(w, ((0, 0), (0, n - w.shape[1])))


def _pad_vec(v, n=D_MODEL):
    return jnp.pad(v, (0, n - v.shape[0]))


def _group_matrix(sizes, total):
    m = np.zeros((total, total), np.float32)
    off = 0
    for sz, on in sizes:
        if on:
            m[off:off + sz, off:off + sz] = 1.0
        off += sz
    return jnp.asarray(m, BF16)


def _swap_halves(w):
    h = w.shape[-1] // 2
    return jnp.concatenate([w[..., h:], w[..., :h]], axis=-1)


def _pack_even(w_in, norm_g, sb_qn, sb_kn, qa_g, wq_up, kva_g, wkv_up, qn, kn):
    d = w_in.shape[0]
    z = lambda n: jnp.zeros((d, n), w_in.dtype)
    kr = w_in[:, 2432:2464]
    w = jnp.concatenate([
        w_in[:, 0:1536], w_in[:, 1536:2048], w_in[:, 2464:2976], w_in[:, 2048:2304], w_in[:, 2304:2432],
        z(64), kr, z(32), z(64), _swap_halves(kr), z(32)], axis=1).astype(BF16)
    wq3 = wq_up.reshape(MLA_Q_RANK, MLA_HEADS, MLA_NOPE + MLA_ROPE)
    zq = jnp.zeros((MLA_Q_RANK, MLA_HEADS, 32), wq_up.dtype)
    wq = jnp.concatenate([wq3, zq], axis=-1).reshape(MLA_Q_RANK, MLA_HEADS * 128).astype(BF16)
    wqs = jnp.concatenate([jnp.zeros((MLA_Q_RANK, MLA_HEADS, 64), wq_up.dtype),
                           _swap_halves(wq3[..., MLA_NOPE:]), zq], axis=-1)
    wqs = wqs.reshape(MLA_Q_RANK, MLA_HEADS * 128).astype(BF16)
    wkv3 = wkv_up.reshape(MLA_KV_RANK, MLA_HEADS, MLA_NOPE + MLA_V)
    wk = jnp.concatenate([wkv3[..., :MLA_NOPE], jnp.zeros_like(wkv3[..., :MLA_NOPE])], axis=-1)
    wkv = jnp.concatenate([wk.reshape(MLA_KV_RANK, MLA_HEADS * 128),
                           wkv3[..., MLA_NOPE:].reshape(MLA_KV_RANK, MLA_HEADS * MLA_V)], axis=1).astype(BF16)
    z32 = jnp.zeros((32,), F32)
    z64 = jnp.zeros((64,), F32)
    scale = (MLA_NOPE + MLA_ROPE) ** -0.5 * LOG2E
    qg = jnp.tile(jnp.concatenate([qn, z32]), MLA_HEADS) * scale
    qgs = jnp.tile(jnp.concatenate([z64, _swap_halves(qn[MLA_NOPE:]), z32]), MLA_HEADS) * scale
    kg = jnp.tile(jnp.concatenate([kn[:MLA_NOPE], z64]), MLA_HEADS)
    krg = jnp.concatenate([z64, kn[MLA_NOPE:], z32])
    krgs = jnp.concatenate([z64, _swap_halves(kn[MLA_NOPE:]), z32])
    cnt = jnp.tile(jnp.concatenate([jnp.full((64,), 1.0 / 64), jnp.full((32,), 1.0 / 32), jnp.ones((32,))]),
                   MLA_HEADS).astype(F32)
    rows = [norm_g, _pad_vec(jnp.tile(sb_qn, SB_HEADS) * (-LOG2E * HEAD_DIM ** -0.5)), _pad_vec(jnp.tile(sb_kn, SB_HEADS)),
            _pad_vec(qa_g), _pad_vec(kva_g), qg, qgs, kg, _pad_vec(krg), _pad_vec(krgs), cnt]
    rows += [jnp.zeros((D_MODEL,), F32)] * (16 - len(rows))
    return w, wq, wqs, wkv, jnp.stack(rows).astype(F32)


def _pack_odd(w_in, norm_g, nsa_qn, nsa_kn, dil_qn, dil_kn):
    d = w_in.shape[0]

    def padded_heads(cols, n):
        c3 = cols.reshape(d, n, HEAD_DIM)
        return jnp.concatenate([c3, jnp.zeros_like(c3)], axis=-1).reshape(d, n * 128)

    def doubled_heads(cols, n):
        c3 = cols.reshape(d, n, HEAD_DIM)
        return jnp.concatenate([c3, c3], axis=-1).reshape(d, n * 128)

    ng = w_in[:, 1920:1956].reshape(d, NSA_KV_HEADS, NSA_HPG * 3)
    ng = jnp.pad(ng, ((0, 0), (0, 0), (0, 128 - NSA_HPG * 3))).reshape(d, NSA_KV_HEADS * 128)
    w_nsa = jnp.concatenate([
        padded_heads(w_in[:, 0:768], NSA_HEADS),
        padded_heads(w_in[:, 768:960], NSA_KV_HEADS), padded_heads(w_in[:, 960:1152], NSA_KV_HEADS),
        padded_heads(w_in[:, 1152:1344], NSA_KV_HEADS), padded_heads(w_in[:, 1344:1536], NSA_KV_HEADS),
        padded_heads(w_in[:, 1536:1728], NSA_KV_HEADS), padded_heads(w_in[:, 1728:1920], NSA_KV_HEADS),
        ng, w_in[:, 1956:2724], w_in[:, 5028:5284]], axis=1).astype(BF16)
    w_dil = w_in[:, 2724:5028].astype(BF16)
    z64 = jnp.zeros((64,), F32)
    n = NSA_HEADS * 128
    rows = [_pad_vec(norm_g, n),
            jnp.tile(jnp.concatenate([nsa_qn * (LOG2E * HEAD_DIM ** -0.5), z64]), NSA_HEADS),
            _pad_vec(jnp.tile(jnp.concatenate([nsa_kn, z64]), NSA_KV_HEADS), n),
            _pad_vec(jnp.tile(dil_qn, DIL_HEADS) * HEAD_DIM ** -0.5, n), _pad_vec(jnp.tile(dil_kn, DIL_HEADS), n)]
    rows += [jnp.zeros((n,), F32)] * (8 - len(rows))
    return w_nsa, w_dil, jnp.stack(rows).astype(F32)


def _pack_compress(w, pe, double):
    w3 = w.reshape(CMP_LEN, HEAD_DIM, HEAD_DIM)
    w3 = jnp.concatenate([w3, jnp.zeros_like(w3)], axis=1)
    w3 = jnp.concatenate([w3, w3 if double else jnp.zeros_like(w3)], axis=2)
    wp = w3.reshape(2, CMP_STRIDE * 128, 128).astype(BF16)
    pe2 = jnp.concatenate([pe, jnp.zeros_like(pe)], axis=1).reshape(2, 1, CMP_STRIDE * 128)
    pe2 = jnp.broadcast_to(pe2, (2, 8, CMP_STRIDE * 128)).astype(F32)
    return wp, pe2


def _normed_len(gain, sizes):
    tot, off = 0.0, 0
    for n in sizes:
        tot = tot + n * jnp.max(jnp.square(gain[off:off + n]))
        off += n
    return jnp.sqrt(tot)


def _mla_logit_bound(qn, kn):
    sizes = (MLA_NOPE, MLA_ROPE)
    scale = (MLA_NOPE + MLA_ROPE) ** -0.5 * LOG2E
    return (_normed_len(qn, sizes) * _normed_len(kn, sizes) * (scale * ROUNDING_MARGIN)).reshape(1).astype(F32)


def _alibi_slopes(n):
    return 2.0 ** (-8.0 * jnp.arange(1, n + 1, dtype=jnp.float32) / n)


def kernel(x, c, positions, ada_w, ada_b, norm_g, ev_w_in, ev_w_out, sb_qn, sb_kn, mla_qa_g, mla_wq_up,
           mla_kva_g, mla_wkv_up, mla_qn, mla_kn, od_w_in, od_w_out, nsa_qn, nsa_kn, nsa_cmp_wk, nsa_cmp_wv,
           nsa_cmp_pe_k, nsa_cmp_pe_v, dil_qn, dil_kn):
    b, s, d = x.shape
    depth = ada_w.shape[0]
    pos_f = positions.astype(F32)

    inv_freq = ROPE_BASE ** (-jnp.arange(0, MLA_ROPE, 2, dtype=F32) / MLA_ROPE)
    ang = pos_f[:, None] * inv_freq[None, :]
    cos, sin = jnp.cos(ang), jnp.sin(ang)
    cos128 = jnp.concatenate([jnp.ones((s, 64), F32), cos, cos, jnp.zeros((s, 32), F32)], axis=1)
    sin128 = jnp.concatenate([jnp.zeros((s, 64), F32), -sin, sin, jnp.zeros((s, 32), F32)], axis=1)
    nsa_slopes = _alibi_slopes(NSA_HEADS)
    dil_slopes = _alibi_slopes(N_DIL * DIL_HEADS)
    nch = s // CMP_STRIDE
    chunk_sum = pos_f.reshape(nch, CMP_STRIDE).sum(axis=1)
    cpos = ((chunk_sum + jnp.roll(chunk_sum, -1)) / CMP_LEN).reshape(1, nch)
    n_sel = s // SEL_LEN
    cst = np.arange(nch)[:, None] * CMP_STRIDE
    jst = np.arange(LANES)[None, :] * SEL_LEN
    ovl = ((cst <= jst + SEL_LEN - 1) & (cst + CMP_LEN - 1 >= jst) & (np.arange(LANES)[None, :] < n_sel))
    ovl = jnp.asarray(ovl.astype(np.float32), BF16)
    pos_col = pos_f.reshape(s, 1)
    tqn = min(TQ_NSA, s)
    tkn = min(TK_NSA, tqn)
    assert tkn == TK_NSA
    pos2d = pos_f.reshape(s // tkn, 1, tkn)
    pos_sorted = jnp.all(pos_f[1:] >= pos_f[:-1])

    g64 = _group_matrix([(64, 1)] * 4, 256)
    gm = _group_matrix([(64, 1), (32, 1), (32, 0)] * 2, 256)
    tks = min(TK_SB, min(TQ_SB, s))
    tri = jnp.asarray(np.tril(np.ones((tks, tks), np.float32)), BF16)

    c8 = jnp.pad(c, ((0, 8 - b), (0, 0)))
    mod_all = _modulation(c8, ada_w, ada_b)

    for layer in range(depth):
        j = layer // 2
        mod3 = mod_all[layer, :b].reshape(b, 1, 3 * d)
        if layer % 2 == 0:
            w, wq, wqs, wkv, vec = _pack_even(ev_w_in[j], norm_g[layer], sb_qn[j], sb_kn[j], mla_qa_g[j],
                                              mla_wq_up[j], mla_kva_g[j], mla_wkv_up[j], mla_qn[j], mla_kn[j])
            sbq, sbk, sbv, gz, mq, mk, mv = _k1_even(x, mod3, vec, w, wq, wqs, wkv, g64, gm, cos128, sin128)
            o_sb = _sb_attention(sbq, sbk, sbv, tri)
            o_mla = _mla_attention(mq, mk, mv, _mla_logit_bound(mla_qn[j], mla_kn[j]))
            x = _out_even(x, mod3, o_sb, o_mla, gz, ev_w_out[j].astype(BF16))
        else:
            w_nsa, w_dil, vec = _pack_odd(od_w_in[j], norm_g[layer], nsa_qn[j], nsa_kn[j], dil_qn[j], dil_kn[j])
            q, ck, cv, sk, sv, wk, wv, gates, gz = _k1_nsa(x, mod3, vec, w_nsa, g64)
            dq, dk, dv = _k1_dil(x, mod3, vec, w_dil, g64)
            wck, pek = _pack_compress(nsa_cmp_wk[j], nsa_cmp_pe_k[j], False)
            wcv, pev = _pack_compress(nsa_cmp_wv[j], nsa_cmp_pe_v[j], True)
            kgain = jnp.concatenate([nsa_kn[j], jnp.zeros((64,), F32)]).reshape(1, 128)
            kc, vc = _compress(ck, cv, wck, wcv, pek, pev, kgain, g64[0:128, 0:128])
            o_c, selb, used = _nsa_cmp(nsa_slopes, q, kc, vc, ovl, cpos, pos_col, gates)
            flags = used[:, :, :, 0, :].reshape(b, NSA_KV_HEADS, s // tqn, tqn // min(TQ_CMP, s),
                                                SEL_TILE_STRIDE, LANES // SEL_TILE_STRIDE).max(axis=(3, 5))
            flags = (flags > 0).astype(jnp.int32).reshape(-1)
            nsa_bound = (_normed_len(nsa_qn[j], (HEAD_DIM,)) * _normed_len(nsa_kn[j], (HEAD_DIM,))
                         * (HEAD_DIM ** -0.5 * LOG2E * ROUNDING_MARGIN)).reshape(1).astype(F32)
            safe = (nsa_bound[0] <= SAFE_LOGIT_BOUND) & pos_sorted
            o_s = _nsa_gqa(flags, nsa_slopes, nsa_bound, safe, q, selb, sk, sv, pos2d, pos_col, gates, 1)
            o_w = _nsa_gqa(flags, nsa_slopes, nsa_bound, safe, q, selb, wk, wv, pos2d, pos_col, gates, 2)
            dils, lses = [], []
            for g in range(N_DIL):
                o, lse = _dilated(dil_slopes, dq[g], dk[g], dv[g], pos_f, g)
                dils.append(o)
                lses.append(lse)
            x = _out_odd(x, mod3, o_c, o_s, o_w, dils, lses, gz, od_w_out[j].astype(BF16))
    return x
```

```python
import functools

import numpy as np
import jax
import jax.numpy as jnp
from jax import lax
from jax.experimental import pallas as pl
from jax.experimental.pallas import tpu as pltpu

F32 = jnp.float32
BF16 = jnp.bfloat16

D_MODEL = 1024
HEAD_DIM = 64
NORM_EPS = 1e-6
TINY = 1e-30
SB_HEADS = 8
MLA_HEADS = 8
MLA_Q_RANK = 256
MLA_KV_RANK = 128
MLA_NOPE = 64
MLA_ROPE = 32
MLA_V = 64
ROPE_BASE = 10000.0
NSA_HEADS = 12
NSA_KV_HEADS = 3
NSA_HPG = 4
CMP_LEN = 32
CMP_STRIDE = 16
SEL_LEN = 64
SEL_TOPN = 16
WIN = 512
FORCE_BONUS = 1e3
DIL_CFG = ((128, 1), (512, 4), (2048, 16))
N_DIL = 3
DIL_HEADS = 4
SB_W = SB_HEADS * HEAD_DIM
MLA_OUT = MLA_HEADS * MLA_V
NSA_W = NSA_HEADS * HEAD_DIM
NSA_KV_W = NSA_KV_HEADS * HEAD_DIM
DIL_W = DIL_HEADS * HEAD_DIM

LANES = 128
MXU_DIM = 256
NEG_MASK = -1e30
LOG2E = 1.4426950408889634
SAFE_LOGIT_BOUND = 50.0
ROUNDING_MARGIN = 1.02
SEL_OFF = 2.0 ** 30
VMEM_LIMIT = 56 * 1024 * 1024

TS_PROJ = 512
TQ_SB = 1024
TK_SB = 256
TQ_MLA = 1024
TD_MLA = 512
TQ_CMP = 256
TQ_NSA = 512
TK_NSA = 512
SEL_TILE_STRIDE = LANES * SEL_LEN // TK_NSA
T_DIL = 128


def _dot(a, b):
    return jnp.dot(a, b, preferred_element_type=F32)


def _dot_nt(a, b):
    return lax.dot_general(a, b, (((1,), (1,)), ((), ())), preferred_element_type=F32)


def _split_hl(a):
    hi = a.astype(BF16)
    lo = (a - hi.astype(F32)).astype(BF16)
    return hi, lo


def _dot_hl(a, b):
    hi, lo = _split_hl(a)
    return _dot(hi, b) + _dot(lo, b)


def _sigmoid(z):
    return 1.0 / (1.0 + jnp.exp(-z))


def _cparams(n_axes):
    return pltpu.CompilerParams(dimension_semantics=("arbitrary",) * n_axes,
                                vmem_limit_bytes=VMEM_LIMIT)


def _full(shape):
    n = len(shape)
    return pl.BlockSpec(shape, lambda *a, _n=n: (0,) * _n)


def _mod_kernel(c_ref, w_ref, b_ref, o_ref):
    c = c_ref[...]
    a = c * _sigmoid(c)
    ah, al = _split_hl(a)
    wh, wl = _split_hl(w_ref[0])
    o_ref[0] = _dot(ah, wh) + _dot(ah, wl) + _dot(al, wh) + b_ref[0]


def _modulation(c8, ada_w, ada_b):
    depth, d, n3 = ada_w.shape
    tn = 1024
    return pl.pallas_call(
        _mod_kernel,
        grid=(depth, n3 // tn),
        in_specs=[pl.BlockSpec((8, d), lambda l, j: (0, 0)),
                  pl.BlockSpec((1, d, tn), lambda l, j: (l, 0, j)),
                  pl.BlockSpec((1, 1, tn), lambda l, j: (l, 0, j))],
        out_specs=pl.BlockSpec((1, 8, tn), lambda l, j: (l, 0, j)),
        out_shape=jax.ShapeDtypeStruct((depth, 8, n3), F32),
        compiler_params=_cparams(2),
        name="adaln_mod",
    )(c8, ada_w, ada_b.reshape(depth, 1, n3))


def _modulated(x_ref, mod_ref, ng):
    x = x_ref[0]
    mod = mod_ref[0]
    shift = mod[:, 0:D_MODEL]
    scale = mod[:, D_MODEL:2 * D_MODEL]
    ms = jnp.mean(x * x, axis=-1, keepdims=True)
    h = x * lax.rsqrt(ms + NORM_EPS) * ng
    h = h * (1.0 + scale) + shift
    return h.astype(BF16)


def _group_norm_chunk(t, g, inv_cnt):
    ss = _dot_hl(t * t, g)
    return t * lax.rsqrt(ss * inv_cnt + NORM_EPS)


def _row_rms(t, gain):
    ms = jnp.mean(t * t, axis=-1, keepdims=True)
    return t * lax.rsqrt(ms + NORM_EPS) * gain


EV_COLS = 3200


def _k1_even_kernel(x_ref, mod_ref, vec_ref, w_ref, wq_ref, wqs_ref, wkv_ref, g64_ref, gm_ref,
                    cos_ref, sin_ref, sbq_o, sbk_o, sbv_o, gz_o, mq_o, mk_o, mv_o):
    hb = _modulated(x_ref, mod_ref, vec_ref[0:1, :])
    g64 = g64_ref[...]
    gm = gm_ref[...]
    inv64 = 1.0 / HEAD_DIM

    u = _dot(hb, w_ref[:, 0:1536])
    for c in range(2):
        sl = slice(c * 256, (c + 1) * 256)
        qn = _group_norm_chunk(u[:, c * 256:(c + 1) * 256], g64, inv64)
        sbq_o[0, :, sl] = (qn * vec_ref[1:2, sl]).astype(BF16)
        kn = _group_norm_chunk(u[:, 512 + c * 256:512 + (c + 1) * 256], g64, inv64)
        sbk_o[0, :, sl] = (kn * vec_ref[2:3, sl]).astype(BF16)
    sbv_o[0] = u[:, 1024:1536].astype(BF16)

    uz = _dot(hb, w_ref[:, 1536:2560])
    gz_o[0] = (uz * _sigmoid(uz)).astype(BF16)

    ul = _dot(hb, w_ref[:, 2560:3200])
    qlat = _row_rms(ul[:, 0:256], vec_ref[3:4, 0:256]).astype(BF16)
    kvlat = _row_rms(ul[:, 256:384], vec_ref[4:5, 0:128]).astype(BF16)
    krm = ul[:, 384:512]
    krs = ul[:, 512:640]
    cs = cos_ref[...]
    sn = sin_ref[...]
    cs2 = jnp.concatenate([cs, cs], axis=1)
    sn2 = jnp.concatenate([sn, sn], axis=1)

    tq = _dot(qlat, wq_ref[...])
    tqs = _dot(qlat, wqs_ref[...])
    for p in range(4):
        sl = slice(p * 256, (p + 1) * 256)
        tc = tq[:, p * 256:(p + 1) * 256]
        ss = _dot_hl(tc * tc, gm)
        inv = lax.rsqrt(ss * vec_ref[10:11, sl] + NORM_EPS)
        a = tc * inv * vec_ref[5:6, sl]
        b = tqs[:, p * 256:(p + 1) * 256] * inv * vec_ref[6:7, sl]
        mq_o[0, :, sl] = (a * cs2 + b * sn2).astype(BF16)

    sskr = _dot_hl(krm * krm, gm[0:128, 0:128])
    invr = lax.rsqrt(sskr * vec_ref[10:11, 0:128] + NORM_EPS)
    kr = (krm * invr * vec_ref[8:9, 0:128]) * cs + (krs * invr * vec_ref[9:10, 0:128]) * sn
    kr2 = jnp.concatenate([kr, kr], axis=1)

    kv = _dot(kvlat, wkv_ref[...])
    for p in range(4):
        sl = slice(p * 256, (p + 1) * 256)
        kn = _group_norm_chunk(kv[:, p * 256:(p + 1) * 256], g64, inv64)
        mk_o[0, :, sl] = (kn * vec_ref[7:8, sl] + kr2).astype(BF16)
    mv_o[0] = kv[:, 1024:1536].astype(BF16)


def _k1_even(x, mod3, vec, w, wq, wqs, wkv, g64, gm, cos128, sin128):
    b, s, d = x.shape
    ts = min(TS_PROJ, s)
    row = lambda n: pl.BlockSpec((1, ts, n), lambda i, j: (i, j, 0))
    outs = [(512, BF16), (512, BF16), (512, BF16), (1024, BF16), (1024, BF16), (1024, BF16), (512, BF16)]
    return pl.pallas_call(
        _k1_even_kernel,
        grid=(b, s // ts),
        in_specs=[row(d),
                  pl.BlockSpec((1, 1, 3 * d), lambda i, j: (i, 0, 0)),
                  _full(vec.shape), _full(w.shape), _full(wq.shape), _full(wqs.shape), _full(wkv.shape),
                  _full(g64.shape), _full(gm.shape),
                  pl.BlockSpec((ts, 128), lambda i, j: (j, 0)),
                  pl.BlockSpec((ts, 128), lambda i, j: (j, 0))],
        out_specs=[row(n) for n, _ in outs],
        out_shape=[jax.ShapeDtypeStruct((b, s, n), dt) for n, dt in outs],
        compiler_params=_cparams(2),
        name="inproj_even",
    )(x, mod3, vec, w, wq, wqs, wkv, g64, gm, cos128, sin128)


def _sb_kernel(q_ref, k_ref, v_ref, tri_ref, o_ref, acc_ref, car_ref, *, tq, tk):
    qi = pl.program_id(2)
    q = q_ref[0]
    lane = lax.broadcasted_iota(jnp.int32, (1, LANES), 1)
    lo_half = lane < HEAD_DIM
    zero = jnp.zeros_like(q)
    qs = (jnp.where(lo_half, q, zero), jnp.where(lo_half, zero, q))
    tri = tri_ref[...]
    acc_ref[...] = jnp.zeros_like(acc_ref)
    car_ref[...] = jnp.zeros_like(car_ref)
    col0 = lax.broadcasted_iota(jnp.int32, (1, tk), 1)

    def tile(kt, r0, masked):
        n = tq - r0
        off = pl.multiple_of(kt * tk, tk)
        k = k_ref[0, pl.ds(off, tk), :]
        v = v_ref[0, pl.ds(off, tk), :]
        if masked:
            rows = qi * tq + r0 + lax.broadcasted_iota(jnp.int32, (n, 1), 0)
            strict = (kt * tk + col0) < rows
        for i in range(2):
            nz = _dot_nt(qs[i][r0:tq], k)
            soft = jnp.log(1.0 + jnp.exp2(-jnp.abs(nz))) * LOG2E
            lom = jnp.minimum(nz, 0.0) - soft
            if masked:
                lom = jnp.where(strict, lom, 0.0)
            tt = _dot(lom.astype(BF16), tri)
            car = car_ref[i, r0:tq]
            w = jnp.exp2(tt - nz + jnp.concatenate([car] * (tk // LANES), axis=1))
            if masked:
                w = jnp.where(strict, w, 0.0)
            acc_ref[i, r0:tq] += _dot(w.astype(BF16), v)
            car_ref[i, r0:tq] = car + jnp.broadcast_to(tt[:, 0:1], (n, LANES))

    per = tq // tk
    for c in reversed(range(per)):
        tile(qi * per + c, c * tk, True)

    def body(j, carry):
        for u in range(per):
            tile(qi * per - 1 - per * j - u, 0, False)
        return carry

    lax.fori_loop(0, qi, body, 0)
    o_ref[0] = jnp.where(lo_half, acc_ref[0], acc_ref[1])


def _sb_attention(q, k, v, tri):
    b, s, w = q.shape
    tq = min(TQ_SB, s)
    tk = min(TK_SB, tq)
    kern = functools.partial(_sb_kernel, tq=tq, tk=tk)
    return pl.pallas_call(
        kern,
        grid=(b, w // LANES, s // tq),
        in_specs=[pl.BlockSpec((1, tq, LANES), lambda i, p, j: (i, j, p)),
                  pl.BlockSpec((1, s, LANES), lambda i, p, j: (i, 0, p)),
                  pl.BlockSpec((1, s, LANES), lambda i, p, j: (i, 0, p)),
                  _full(tri.shape)],
        out_specs=pl.BlockSpec((1, tq, LANES), lambda i, p, j: (i, j, p)),
        out_shape=jax.ShapeDtypeStruct((b, s, w), F32),
        scratch_shapes=[pltpu.VMEM((2, tq, LANES), F32), pltpu.VMEM((2, tq, LANES), F32)],
        compiler_params=_cparams(3),
        name="stickbreak_attn",
    )(q, k, v, tri)


def _lane_max(s):
    m = s[:, 0:LANES]
    for c in range(1, s.shape[1] // LANES):
        m = jnp.maximum(m, s[:, c * LANES:(c + 1) * LANES])
    return m


def _mla_kernel(bnd_ref, q_ref, k_ref, v_ref, o_ref, acc_ref, m_ref, *, tq, td, bounded):
    qi = pl.program_id(2)
    q = q_ref[0]
    qs = (q[:, 0:LANES], q[:, LANES:2 * LANES])
    lane = lax.broadcasted_iota(jnp.int32, (1, LANES), 1)
    lo_half = lane < HEAD_DIM
    acc_ref[...] = jnp.zeros_like(acc_ref)
    if bounded:
        m_ref[...] = jnp.full(m_ref.shape, bnd_ref[0], F32)
    else:
        m_ref[...] = jnp.full_like(m_ref, NEG_MASK)

    def scores(off, width, r0, masked):
        k = k_ref[0, pl.ds(off, width), :]
        if masked:
            rows = qi * tq + r0 + lax.broadcasted_iota(jnp.int32, (tq - r0, 1), 0)
            keep = (off + lax.broadcasted_iota(jnp.int32, (1, width), 1)) <= rows
        out = []
        for i in range(2):
            s = _dot_nt(qs[i][r0:tq], k[:, i * LANES:(i + 1) * LANES])
            if masked:
                s = jnp.where(keep, s, NEG_MASK)
            out.append(s)
        return out

    def sweep(fn):
        def body(j, carry):
            fn(pl.multiple_of(j * tq, tq), tq, 0, False)
            return carry
        lax.fori_loop(0, qi, body, 0)
        for c in range(tq // td):
            fn(pl.multiple_of(qi * tq + c * td, td), td, c * td, True)

    def row_max(off, width, r0, masked):
        for i, s in enumerate(scores(off, width, r0, masked)):
            m_ref[i, r0:tq] = jnp.maximum(m_ref[i, r0:tq], _lane_max(s))

    def accumulate(off, width, r0, masked):
        vx = jnp.concatenate([v_ref[0, pl.ds(off, width), :], jnp.ones((width, LANES), BF16)], axis=1)
        for i, s in enumerate(scores(off, width, r0, masked)):
            p = jnp.exp2(s - jnp.concatenate([m_ref[i, r0:tq]] * (width // LANES), axis=1))
            acc_ref[i, r0:tq] += _dot(p.astype(BF16), vx)

    if not bounded:
        sweep(row_max)
        for i in range(2):
            m_ref[i] = jnp.broadcast_to(jnp.max(m_ref[i], axis=-1, keepdims=True), (tq, LANES))
    sweep(accumulate)
    o_ref[0] = jnp.where(lo_half, acc_ref[0, :, 0:LANES] / acc_ref[0, :, LANES:2 * LANES],
                         acc_ref[1, :, 0:LANES] / acc_ref[1, :, LANES:2 * LANES])


def _mla_attention(q, k, v, bound):
    b, s, _ = q.shape
    tq = min(TQ_MLA, s)
    td = min(TD_MLA, tq)

    def call(bounded):
        kern = functools.partial(_mla_kernel, tq=tq, td=td, bounded=bounded)
        return pl.pallas_call(
            kern,
            grid=(b, MLA_HEADS // 2, s // tq),
            in_specs=[pl.BlockSpec(memory_space=pltpu.SMEM),
                      pl.BlockSpec((1, tq, 2 * LANES), lambda i, p, j: (i, j, p)),
                      pl.BlockSpec((1, s, 2 * LANES), lambda i, p, j: (i, 0, p)),
                      pl.BlockSpec((1, s, LANES), lambda i, p, j: (i, 0, p))],
            out_specs=pl.BlockSpec((1, tq, LANES), lambda i, p, j: (i, j, p)),
            out_shape=jax.ShapeDtypeStruct((b, s, MLA_OUT), F32),
            scratch_shapes=[pltpu.VMEM((2, tq, 2 * LANES), F32), pltpu.VMEM((2, tq, LANES), F32)],
            compiler_params=_cparams(3),
            name="mla_attn_bounded" if bounded else "mla_attn",
        )(bound, q, k, v)

    return lax.cond(bound[0] <= SAFE_LOGIT_BOUND, lambda: call(True), lambda: call(False))


def _out_even_kernel(x_ref, mod_ref, osb_ref, omla_ref, gz_ref, w_ref, o_ref):
    gz = gz_ref[0].astype(F32)
    m1 = (osb_ref[0] * gz[:, 0:SB_W]).astype(BF16)
    m2 = (omla_ref[0] * gz[:, SB_W:SB_W + MLA_OUT]).astype(BF16)
    y = _dot(m1, w_ref[0:SB_W, :]) + _dot(m2, w_ref[SB_W:SB_W + MLA_OUT, :])
    gate = mod_ref[0][:, 2 * D_MODEL:3 * D_MODEL]
    o_ref[0] = x_ref[0] + gate * y


def _out_even(x, mod3, osb, omla, gz, w):
    b, s, d = x.shape
    ts = min(TS_PROJ, s)
    row = lambda n: pl.BlockSpec((1, ts, n), lambda i, j: (i, j, 0))
    return pl.pallas_call(
        _out_even_kernel,
        grid=(b, s // ts),
        in_specs=[row(d), pl.BlockSpec((1, 1, 3 * d), lambda i, j: (i, 0, 0)),
                  row(SB_W), row(MLA_OUT), row(SB_W + MLA_OUT), _full(w.shape)],
        out_specs=row(d),
        out_shape=jax.ShapeDtypeStruct((b, s, d), F32),
        compiler_params=_cparams(2),
        name="outproj_even",
    )(x, mod3, osb, omla, gz, w)


NSA_COLS = 5248


def _k1_nsa_kernel(x_ref, mod_ref, vec_ref, w_ref, g64_ref,
                   q_o, ck_o, cv_o, sk_o, sv_o, wk_o, wv_o, gt_o, gz_o):
    hb = _modulated(x_ref, mod_ref, vec_ref[0:1, 0:D_MODEL])
    g64 = g64_ref[...]
    inv64 = 1.0 / HEAD_DIM

    uq = _dot(hb, w_ref[:, 0:1536])
    for c in range(6):
        sl = slice(c * 256, (c + 1) * 256)
        qn = _group_norm_chunk(uq[:, c * 256:(c + 1) * 256], g64, inv64)
        q_o[0, :, sl] = (qn * vec_ref[1:2, sl]).astype(BF16)

    uc = _dot(hb, w_ref[:, 1536:2304])
    for g in range(NSA_KV_HEADS):
        ck_o[0, g] = uc[:, g * 128:(g + 1) * 128].astype(BF16)
        cv_o[0, g] = uc[:, 384 + g * 128:384 + (g + 1) * 128].astype(BF16)

    us = _dot(hb, w_ref[:, 2304:3840])
    kgain = vec_ref[2:3, 0:384]
    g128 = g64[0:128, 0:128]
    for g in range(NSA_KV_HEADS):
        sl = slice(g * 128, (g + 1) * 128)
        t = us[:, g * 128:(g + 1) * 128]
        ss = _dot_hl(t * t, g128)
        sk_o[0, :, sl] = (t * lax.rsqrt(ss * inv64 + NORM_EPS) * kgain[:, sl]).astype(BF16)
        t = us[:, 768 + g * 128:768 + (g + 1) * 128]
        ss = _dot_hl(t * t, g128)
        wk_o[0, :, sl] = (t * lax.rsqrt(ss * inv64 + NORM_EPS) * kgain[:, sl]).astype(BF16)
    ones_hi = (lax.broadcasted_iota(jnp.int32, (1, 384), 1) % 128) >= HEAD_DIM
    sv_o[0] = jnp.where(ones_hi, 1.0, us[:, 384:768]).astype(BF16)
    wv_o[0] = jnp.where(ones_hi, 1.0, us[:, 1152:1536]).astype(BF16)

    ug = _dot(hb, w_ref[:, 3840:4224])
    gt_o[0] = _sigmoid(ug)
    uz = _dot(hb, w_ref[:, 4224:5248])
    gz_o[0] = (uz * _sigmoid(uz)).astype(BF16)


def _k1_nsa(x, mod3, vec, w, g64):
    b, s, d = x.shape
    ts = min(TS_PROJ, s)
    row = lambda n: pl.BlockSpec((1, ts, n), lambda i, j: (i, j, 0))
    grp = pl.BlockSpec((1, NSA_KV_HEADS, ts, 128), lambda i, j: (i, 0, j, 0))
    return pl.pallas_call(
        _k1_nsa_kernel,
        grid=(b, s // ts),
        in_specs=[row(d), pl.BlockSpec((1, 1, 3 * d), lambda i, j: (i, 0, 0)),
                  _full(vec.shape), _full(w.shape), _full(g64.shape)],
        out_specs=[row(1536), grp, grp, row(384), row(384), row(384), row(384), row(384), row(1024)],
        out_shape=[jax.ShapeDtypeStruct((b, s, 1536), BF16),
                   jax.ShapeDtypeStruct((b, NSA_KV_HEADS, s, 128), BF16),
                   jax.ShapeDtypeStruct((b, NSA_KV_HEADS, s, 128), BF16),
                   jax.ShapeDtypeStruct((b, s, 384), BF16),
                   jax.ShapeDtypeStruct((b, s, 384), BF16),
                   jax.ShapeDtypeStruct((b, s, 384), BF16),
                   jax.ShapeDtypeStruct((b, s, 384), BF16),
                   jax.ShapeDtypeStruct((b, s, 384), F32),
                   jax.ShapeDtypeStruct((b, s, 1024), BF16)],
        compiler_params=_cparams(2),
        name="inproj_nsa",
    )(x, mod3, vec, w, g64)


def _k1_dil_kernel(x_ref, mod_ref, vec_ref, w_ref, g64_ref, q_o, k_o, v_o):
    hb = _modulated(x_ref, mod_ref, vec_ref[0:1, 0:D_MODEL])
    g64 = g64_ref[...]
    inv64 = 1.0 / HEAD_DIM
    u = _dot(hb, w_ref[...])
    for g in range(N_DIL):
        qn = _group_norm_chunk(u[:, g * 256:(g + 1) * 256], g64, inv64)
        q_o[g, 0] = (qn * vec_ref[3:4, 0:256]).astype(BF16)
        kn = _group_norm_chunk(u[:, 768 + g * 256:768 + (g + 1) * 256], g64, inv64)
        k_o[g, 0] = (kn * vec_ref[4:5, 0:256]).astype(BF16)
        v_o[g, 0] = u[:, 1536 + g * 256:1536 + (g + 1) * 256].astype(BF16)


def _k1_dil(x, mod3, vec, w, g64):
    b, s, d = x.shape
    ts = min(TS_PROJ, s)
    row = lambda n: pl.BlockSpec((1, ts, n), lambda i, j: (i, j, 0))
    grp = pl.BlockSpec((N_DIL, 1, ts, 256), lambda i, j: (0, i, j, 0))
    shp = jax.ShapeDtypeStruct((N_DIL, b, s, 256), BF16)
    return pl.pallas_call(
        _k1_dil_kernel,
        grid=(b, s // ts),
        in_specs=[row(d), pl.BlockSpec((1, 1, 3 * d), lambda i, j: (i, 0, 0)),
                  _full(vec.shape), _full(w.shape), _full(g64.shape)],
        out_specs=[grp, grp, grp],
        out_shape=[shp, shp, shp],
        compiler_params=_cparams(2),
        name="inproj_dil",
    )(x, mod3, vec, w, g64)


def _compress_kernel(xk_ref, xv_ref, wk_ref, wv_ref, pek_ref, pev_ref, gain_ref, g128_ref, kc_o, vc_o):
    nch = xk_ref.shape[2]

    def comp(x_ref, w_ref, pe_ref):
        x = x_ref[0, 0]
        lo = _dot(x, w_ref[0])
        hi = _dot(x, w_ref[1])
        pec = _dot_hl(pe_ref[0], w_ref[0]) + _dot_hl(pe_ref[1], w_ref[1])
        return lo + pltpu.roll(hi, nch - 1, 0) + pec[0:1, :]

    kc = comp(xk_ref, wk_ref, pek_ref)
    ss = _dot_hl(kc * kc, g128_ref[...])
    kc_o[0, 0] = (kc * lax.rsqrt(ss * (1.0 / HEAD_DIM) + NORM_EPS) * gain_ref[...]).astype(BF16)
    vc_o[0, 0] = comp(xv_ref, wv_ref, pev_ref).T.astype(BF16)


def _compress(ck, cv, wk, wv, pek, pev, gain, g128):
    b, g, s, _ = ck.shape
    nch = s // CMP_STRIDE
    xk = ck.reshape(b, g, nch, CMP_STRIDE * 128)
    xv = cv.reshape(b, g, nch, CMP_STRIDE * 128)
    blk = pl.BlockSpec((1, 1, nch, CMP_STRIDE * 128), lambda i, j: (i, j, 0, 0))
    oblk = pl.BlockSpec((1, 1, nch, 128), lambda i, j: (i, j, 0, 0))
    tblk = pl.BlockSpec((1, 1, 128, nch), lambda i, j: (i, j, 0, 0))
    return pl.pallas_call(
        _compress_kernel,
        grid=(b, g),
        in_specs=[blk, blk, _full(wk.shape), _full(wv.shape), _full(pek.shape), _full(pev.shape),
                  _full(gain.shape), _full(g128.shape)],
        out_specs=[oblk, tblk],
        out_shape=[jax.ShapeDtypeStruct((b, g, nch, 128), BF16), jax.ShapeDtypeStruct((b, g, 128, nch), BF16)],
        compiler_params=_cparams(2),
        name="nsa_compress",
    )(xk, xv, wk, wv, pek, pev, gain, g128)


def _pair_select(lo_half, a, b):
    return jnp.where(lo_half, a, b)


def _nsa_cmp_kernel(slope_ref, q_ref, kc_ref, vct_ref, ovlt_ref, cposc_ref, prow_ref, gt_ref,
                    oc_o, selb_o, any_o, *, tq):
    g = pl.program_id(1)
    qi = pl.program_id(2)
    q4 = q_ref[0]
    kc = kc_ref[0, 0]
    vct = vct_ref[0, 0]
    nch = kc.shape[0]
    t = qi * tq + lax.broadcasted_iota(jnp.int32, (1, tq), 1)
    n_id = lax.broadcasted_iota(jnp.int32, (nch, 1), 0)
    valid = (n_id * CMP_STRIDE + (CMP_LEN - 1)) <= t
    dist = (prow_ref[qi] - cposc_ref[...]) * LOG2E
    gtt = gt_ref[0].T
    row_lo = lax.broadcasted_iota(jnp.int32, (LANES, 1), 0) < HEAD_DIM
    psum = jnp.zeros((nch, tq), F32)
    outs = []
    for hh in range(NSA_HPG):
        s = _dot_nt(kc, q4[:, hh * LANES:(hh + 1) * LANES]) - slope_ref[g * NSA_HPG + hh] * dist
        s = jnp.where(valid, s, -jnp.inf)
        mx = jnp.max(s, axis=0, keepdims=True)
        mx = jnp.where(mx == -jnp.inf, 0.0, mx)
        e = jnp.exp2(s - mx)
        den = jnp.maximum(jnp.sum(e, axis=0, keepdims=True), TINY)
        p = e * (1.0 / den)
        psum = psum + p
        outs.append(_dot(vct, p.astype(BF16)) * gtt[3 * hh:3 * hh + 1, :])
    oc_o[0, :, 0:LANES] = jnp.where(row_lo, outs[0], outs[1]).T
    oc_o[0, :, LANES:2 * LANES] = jnp.where(row_lo, outs[2], outs[3]).T

    hi, lo = _split_hl(psum)
    imp = _dot(ovlt_ref[...], hi) + _dot(ovlt_ref[...], lo)
    blk = lax.broadcasted_iota(jnp.int32, (LANES, 1), 0)
    cur = t >> 6
    forced = (blk == 0) | (blk == cur) | (blk == cur - 1)
    allowed = blk <= cur
    score = jnp.where(allowed, imp + jnp.where(forced, FORCE_BONUS, 0.0), -jnp.inf)
    blk_f = blk.astype(F32)

    def pick(_, carry):
        sc, sel = carry
        mx = jnp.max(sc, axis=0, keepdims=True)
        idx = jnp.min(jnp.where(sc == mx, blk_f, float(LANES)), axis=0, keepdims=True)
        hit = blk_f == idx
        return jnp.where(hit, -jnp.inf, sc), jnp.where(hit, 1.0, sel)

    _, sel = lax.fori_loop(0, SEL_TOPN, pick, (score, jnp.zeros((LANES, tq), F32)))
    chosen = jnp.where((sel > 0.5) & allowed, 1.0, 0.0).T
    selb_o[0, 0] = ((chosen - 1.0) * SEL_OFF).astype(BF16)
    used = jnp.max(chosen, axis=0, keepdims=True)
    any_o[0, 0, 0] = jnp.broadcast_to(used, (8, LANES))


def _nsa_cmp(slopes, q, kc, vct, ovlt, cposc, pos_rows, gates):
    b, s, _ = q.shape
    nch = kc.shape[2]
    tq = pos_rows.shape[2]
    kern = functools.partial(_nsa_cmp_kernel, tq=tq)
    return pl.pallas_call(
        kern,
        grid=(b, NSA_KV_HEADS, s // tq),
        in_specs=[pl.BlockSpec(memory_space=pltpu.SMEM),
                  pl.BlockSpec((1, tq, 4 * LANES), lambda i, g, j: (i, j, g)),
                  pl.BlockSpec((1, 1, nch, LANES), lambda i, g, j: (i, g, 0, 0)),
                  pl.BlockSpec((1, 1, LANES, nch), lambda i, g, j: (i, g, 0, 0)),
                  _full(ovlt.shape), _full(cposc.shape), _full(pos_rows.shape),
                  pl.BlockSpec((1, tq, LANES), lambda i, g, j: (i, j, g))],
        out_specs=[pl.BlockSpec((1, tq, 2 * LANES), lambda i, g, j: (i, j, g)),
                   pl.BlockSpec((1, 1, tq, LANES), lambda i, g, j: (i, g, j, 0)),
                   pl.BlockSpec((1, 1, 1, 8, LANES), lambda i, g, j: (i, g, j, 0, 0))],
        out_shape=[jax.ShapeDtypeStruct((b, s, NSA_W), F32),
                   jax.ShapeDtypeStruct((b, NSA_KV_HEADS, s, LANES), BF16),
                   jax.ShapeDtypeStruct((b, NSA_KV_HEADS, s // tq, 8, LANES), F32)],
        compiler_params=_cparams(3),
        name="nsa_cmp_topk",
    )(slopes, q, kc, vct, ovlt, cposc, pos_rows, gates)


def _gqa_kernel(flag_ref, slope_ref, bnd_ref, q_ref, selb_ref, k_ref, v_ref, pos_ref, pcol_ref, gt_ref, o_ref,
                qa_ref, acc_ref, m_ref, *, tq, tk, branch, bounded):
    bi = pl.program_id(0)
    g = pl.program_id(1)
    qi = pl.program_id(2)
    q4 = q_ref[0]
    lane = lax.broadcasted_iota(jnp.int32, (1, LANES), 1)
    lo_half = lane < HEAD_DIM
    pref = pos_ref[qi * (tq // tk)][:, 0:1]
    for hh in range(NSA_HPG):
        sl = slice(hh * tq, (hh + 1) * tq)
        qa_ref[sl, 0:LANES] = q4[:, hh * LANES:(hh + 1) * LANES]
        if branch == 1:
            qa_ref[sl, LANES:2 * LANES] = selb_ref[0, 0]
        if bounded:
            own = (pcol_ref[...] - pref) * (LOG2E * slope_ref[g * NSA_HPG + hh]) + bnd_ref[0]
            m_ref[sl] = jnp.broadcast_to(own, (tq, LANES))
    acc_ref[...] = jnp.zeros_like(acc_ref)
    if not bounded:
        m_ref[...] = jnp.full_like(m_ref, NEG_MASK)
    rows = qi * tq + lax.broadcasted_iota(jnp.int32, (tq, 1), 0)
    col0 = lax.broadcasted_iota(jnp.int32, (1, tk), 1)
    krow = lax.broadcasted_iota(jnp.int32, (tk, 1), 0)
    per = tq // tk
    nrep = tk // LANES
    flag0 = ((bi * NSA_KV_HEADS + g) * pl.num_programs(2) + qi) * SEL_TILE_STRIDE

    def scores(kt_true, masked):
        kt = jnp.maximum(kt_true, 0)
        off = pl.multiple_of(kt * tk, tk)
        k = k_ref[0, pl.ds(off, tk), :]
        if branch == 1:
            blk = (kt * tk + krow) >> 6
            onehot = jnp.where(lane == blk, 1.0, 0.0).astype(BF16)
            s_all = _dot_nt(qa_ref[...], jnp.concatenate([k, onehot], axis=1))
        else:
            s_all = _dot_nt(qa_ref[:, 0:LANES], k)
        rel = (pos_ref[kt] - pref) * LOG2E
        if masked:
            cols = kt_true * tk + col0
            d = rows - cols
            keep = d >= 0
            if branch == 2:
                keep = keep & (d < WIN) & (cols >= 0)
        out = []
        for hh in range(NSA_HPG):
            s = s_all[hh * tq:(hh + 1) * tq] + slope_ref[g * NSA_HPG + hh] * rel
            if masked:
                s = jnp.where(keep, s, NEG_MASK)
            out.append(s)
        return out

    def sweep(fn):
        if branch == 1:
            def body(j, carry):
                @pl.when(flag_ref[flag0 + j] != 0)
                def _():
                    fn(j, False)
                return carry
            lax.fori_loop(0, qi * per, body, 0)
            for dd in range(per):
                fn(qi * per + dd, True)
        else:
            for dd in range(per + WIN // tk):
                fn((qi + 1) * per - 1 - dd, True)

    def row_max(kt_true, masked):
        for hh, s in enumerate(scores(kt_true, masked)):
            sl = slice(hh * tq, (hh + 1) * tq)
            m_ref[sl] = jnp.maximum(m_ref[sl], _lane_max(s))

    def accumulate(kt_true, masked):
        off = pl.multiple_of(jnp.maximum(kt_true, 0) * tk, tk)
        v = v_ref[0, pl.ds(off, tk), :]
        for hh, s in enumerate(scores(kt_true, masked)):
            sl = slice(hh * tq, (hh + 1) * tq)
            p = jnp.exp2(s - jnp.concatenate([m_ref[sl]] * nrep, axis=1))
            acc_ref[sl] += _dot(p.astype(BF16), v)

    if not bounded:
        sweep(row_max)
        for hh in range(NSA_HPG):
            sl = slice(hh * tq, (hh + 1) * tq)
            m_ref[sl] = jnp.broadcast_to(jnp.max(m_ref[sl], axis=-1, keepdims=True), (tq, LANES))
    sweep(accumulate)

    gt = gt_ref[0]
    for pr in range(2):
        res = []
        for x in range(2):
            hh = 2 * pr + x
            a = acc_ref[hh * tq:(hh + 1) * tq]
            r = pltpu.roll(a, HEAD_DIM, 1)
            o = a / r if x == 0 else r / a
            res.append(o * gt[:, 3 * hh + branch:3 * hh + branch + 1])
        o_ref[0, :, pr * LANES:(pr + 1) * LANES] = _pair_select(lo_half, res[0], res[1])


def _nsa_gqa(flags, slopes, bound, safe, q, selb, k, v, pos2d, pos_col, gates, branch):
    b, s, _ = q.shape
    tq = min(TQ_NSA, s)
    tk = min(TK_NSA, tq)

    def call(bounded):
        kern = functools.partial(_gqa_kernel, tq=tq, tk=tk, branch=branch, bounded=bounded)
        smem = pl.BlockSpec(memory_space=pltpu.SMEM)
        name = ("nsa_sel_attn" if branch == 1 else "nsa_win_attn") + ("_bounded" if bounded else "")
        return pl.pallas_call(
            kern,
            grid=(b, NSA_KV_HEADS, s // tq),
            in_specs=[smem, smem, smem,
                      pl.BlockSpec((1, tq, 4 * LANES), lambda i, g, j: (i, j, g)),
                      pl.BlockSpec((1, 1, tq, LANES), lambda i, g, j: (i, g, j, 0)),
                      pl.BlockSpec((1, s, LANES), lambda i, g, j: (i, 0, g)),
                      pl.BlockSpec((1, s, LANES), lambda i, g, j: (i, 0, g)),
                      _full(pos2d.shape),
                      pl.BlockSpec((tq, 1), lambda i, g, j: (j, 0)),
                      pl.BlockSpec((1, tq, LANES), lambda i, g, j: (i, j, g))],
            out_specs=pl.BlockSpec((1, tq, 2 * LANES), lambda i, g, j: (i, j, g)),
            out_shape=jax.ShapeDtypeStruct((b, s, NSA_W), F32),
            scratch_shapes=[pltpu.VMEM((NSA_HPG * tq, 2 * LANES), BF16),
                            pltpu.VMEM((NSA_HPG * tq, LANES), F32),
                            pltpu.VMEM((NSA_HPG * tq, LANES), F32)],
            compiler_params=_cparams(3),
            name=name,
        )(flags, slopes, bound, q, selb, k, v, pos2d, pos_col, gates)

    return lax.cond(safe, lambda: call(True), lambda: call(False))


def _dil_kernel(slope_ref, q_ref, k_ref, v_ref, pc_ref, pr_ref, o_ref, lse_ref, *, t, span, grp):
    i = pl.program_id(2)
    q4 = q_ref[0]
    lane = lax.broadcasted_iota(jnp.int32, (1, LANES), 1)
    lo_half = lane < HEAD_DIM
    prev = jnp.maximum(i - 1, 0)
    o_prev = pl.multiple_of(prev * t, t)
    o_cur = pl.multiple_of(i * t, t)
    kk = jnp.concatenate([k_ref[0, pl.ds(o_prev, t), :], k_ref[0, pl.ds(o_cur, t), :]], axis=0)
    vv = jnp.concatenate([v_ref[0, pl.ds(o_prev, t), :], v_ref[0, pl.ds(o_cur, t), :]], axis=0)
    pk = jnp.concatenate([pr_ref[0, prev], pr_ref[0, i]], axis=1)
    dist = pc_ref[0] - pk
    rows = i * t + lax.broadcasted_iota(jnp.int32, (t, 1), 0)
    cols = (i - 1) * t + lax.broadcasted_iota(jnp.int32, (1, 2 * t), 1)
    d = rows - cols
    valid = (cols >= 0) & (d >= 0) & (d <= span)
    zero = jnp.zeros((t, LANES), BF16)
    for p in range(2):
        qp = q4[:, p * LANES:(p + 1) * LANES]
        kp = kk[:, p * LANES:(p + 1) * LANES]
        vp = vv[:, p * LANES:(p + 1) * LANES]
        res = []
        for x in range(2):
            qx = jnp.where(lo_half, qp, zero) if x == 0 else jnp.where(lo_half, zero, qp)
            s = _dot_nt(qx, kp) - slope_ref[grp * DIL_HEADS + 2 * p + x] * dist
            s = jnp.where(valid, s, -jnp.inf)
            mx = jnp.max(s, axis=-1, keepdims=True)
            e = jnp.exp(s - mx)
            den = jnp.maximum(jnp.sum(e, axis=-1, keepdims=True), TINY)
            res.append((_dot((e / den).astype(BF16), vp), mx + jnp.log(den)))
        sl = slice(p * LANES, (p + 1) * LANES)
        o_ref[0, :, sl] = _pair_select(lo_half, res[0][0], res[1][0])
        lse_ref[0, :, sl] = jnp.where(lo_half, res[0][1], res[1][1])


def _dilated(slopes, q, k, v, pos_f, grp):
    window, dil = DIL_CFG[grp]
    b, s, w = q.shape
    sub = s // dil
    t = min(T_DIL, sub)
    span = window // dil
    assert span <= t
    qv = q.reshape(b, sub, dil * w)
    kv = k.reshape(b, sub, dil * w)
    vv = v.reshape(b, sub, dil * w)
    pres = pos_f.reshape(sub, dil).T
    pc = pres.reshape(dil, sub, 1)
    pr = pres.reshape(dil, sub // t, 1, t)
    kern = functools.partial(_dil_kernel, t=t, span=span, grp=grp)
    shp = jax.ShapeDtypeStruct((b, sub, dil * w), F32)
    o, lse = pl.pallas_call(
        kern,
        grid=(b, dil, sub // t),
        in_specs=[pl.BlockSpec(memory_space=pltpu.SMEM),
                  pl.BlockSpec((1, t, w), lambda i, r, j: (i, j, r)),
                  pl.BlockSpec((1, sub, w), lambda i, r, j: (i, 0, r)),
                  pl.BlockSpec((1, sub, w), lambda i, r, j: (i, 0, r)),
                  pl.BlockSpec((1, t, 1), lambda i, r, j: (r, j, 0)),
                  pl.BlockSpec((1, sub // t, 1, t), lambda i, r, j: (r, 0, 0, 0))],
        out_specs=[pl.BlockSpec((1, t, w), lambda i, r, j: (i, j, r)),
                   pl.BlockSpec((1, t, w), lambda i, r, j: (i, j, r))],
        out_shape=[shp, shp],
        compiler_params=_cparams(3),
        name="dilated_attn_g%d" % grp,
    )(slopes, qv, kv, vv, pc, pr)
    return o.reshape(b, s, w), lse.reshape(b, s, w)


def _out_odd_kernel(x_ref, mod_ref, oc_ref, os_ref, ow_ref, d0_ref, d1_ref, d2_ref,
                    l0_ref, l1_ref, l2_ref, gz_ref, w_ref, o_ref):
    gz = gz_ref[0].astype(F32)
    nsa = oc_ref[0] + os_ref[0] + ow_ref[0]
    m1 = (nsa * gz[:, 0:NSA_W]).astype(BF16)
    l0, l1, l2 = l0_ref[0], l1_ref[0], l2_ref[0]
    mx = jnp.maximum(jnp.maximum(l0, l1), l2)
    e0, e1, e2 = jnp.exp(l0 - mx), jnp.exp(l1 - mx), jnp.exp(l2 - mx)
    dil = (e0 * d0_ref[0] + e1 * d1_ref[0] + e2 * d2_ref[0]) / (e0 + e1 + e2)
    m2 = (dil * gz[:, NSA_W:NSA_W + DIL_W]).astype(BF16)
    y = _dot(m1, w_ref[0:NSA_W, :]) + _dot(m2, w_ref[NSA_W:NSA_W + DIL_W, :])
    gate = mod_ref[0][:, 2 * D_MODEL:3 * D_MODEL]
    o_ref[0] = x_ref[0] + gate * y


def _out_odd(x, mod3, oc, os_, ow, dils, lses, gz, w):
    b, s, d = x.shape
    ts = min(TS_PROJ, s)
    row = lambda n: pl.BlockSpec((1, ts, n), lambda i, j: (i, j, 0))
    return pl.pallas_call(
        _out_odd_kernel,
        grid=(b, s // ts),
        in_specs=[row(d), pl.BlockSpec((1, 1, 3 * d), lambda i, j: (i, 0, 0)),
                  row(NSA_W), row(NSA_W), row(NSA_W)] + [row(DIL_W)] * 6 + [row(1024), _full(w.shape)],
        out_specs=row(d),
        out_shape=jax.ShapeDtypeStruct((b, s, d), F32),
        compiler_params=_cparams(2),
        name="outproj_odd",
    )(x, mod3, oc, os_, ow, *dils, *lses, gz, w)


def _pad_cols(w, n):
    return jnp.pad(w, ((0, 0), (0, n - w.shape[1])))


def _pad_vec(v, n=D_MODEL):
    return jnp.pad(v, (0, n - v.shape[0]))


def _group_matrix(sizes, total):
    m = np.zeros((total, total), np.float32)
    off = 0
    for sz, on in sizes:
        if on:
            m[off:off + sz, off:off + sz] = 1.0
        off += sz
    return jnp.asarray(m, BF16)


def _swap_halves(w):
    h = w.shape[-1] // 2
    return jnp.concatenate([w[..., h:], w[..., :h]], axis=-1)


def _pack_even(w_in, norm_g, sb_qn, sb_kn, qa_g, wq_up, kva_g, wkv_up, qn, kn):
    d = w_in.shape[0]
    z = lambda n: jnp.zeros((d, n), w_in.dtype)
    kr = w_in[:, 2432:2464]
    w = jnp.concatenate([
        w_in[:, 0:1536], w_in[:, 1536:2048], w_in[:, 2464:2976], w_in[:, 2048:2304], w_in[:, 2304:2432],
        z(64), kr, z(32), z(64), _swap_halves(kr), z(32)], axis=1).astype(BF16)
    wq3 = wq_up.reshape(MLA_Q_RANK, MLA_HEADS, MLA_NOPE + MLA_ROPE)
    zq = jnp.zeros((MLA_Q_RANK, MLA_HEADS, 32), wq_up.dtype)
    wq = jnp.concatenate([wq3, zq], axis=-1).reshape(MLA_Q_RANK, MLA_HEADS * 128).astype(BF16)
    wqs = jnp.concatenate([jnp.zeros((MLA_Q_RANK, MLA_HEADS, 64), wq_up.dtype),
                           _swap_halves(wq3[..., MLA_NOPE:]), zq], axis=-1)
    wqs = wqs.reshape(MLA_Q_RANK, MLA_HEADS * 128).astype(BF16)
    wkv3 = wkv_up.reshape(MLA_KV_RANK, MLA_HEADS, MLA_NOPE + MLA_V)
    wk = jnp.concatenate([wkv3[..., :MLA_NOPE], jnp.zeros_like(wkv3[..., :MLA_NOPE])], axis=-1)
    wkv = jnp.concatenate([wk.reshape(MLA_KV_RANK, MLA_HEADS * 128),
                           wkv3[..., MLA_NOPE:].reshape(MLA_KV_RANK, MLA_HEADS * MLA_V)], axis=1).astype(BF16)
    z32 = jnp.zeros((32,), F32)
    z64 = jnp.zeros((64,), F32)
    scale = (MLA_NOPE + MLA_ROPE) ** -0.5 * LOG2E
    qg = jnp.tile(jnp.concatenate([qn, z32]), MLA_HEADS) * scale
    qgs = jnp.tile(jnp.concatenate([z64, _swap_halves(qn[MLA_NOPE:]), z32]), MLA_HEADS) * scale
    kg = jnp.tile(jnp.concatenate([kn[:MLA_NOPE], z64]), MLA_HEADS)
    krg = jnp.concatenate([z64, kn[MLA_NOPE:], z32])
    krgs = jnp.concatenate([z64, _swap_halves(kn[MLA_NOPE:]), z32])
    cnt = jnp.tile(jnp.concatenate([jnp.full((64,), 1.0 / 64), jnp.full((32,), 1.0 / 32), jnp.ones((32,))]),
                   MLA_HEADS).astype(F32)
    rows = [norm_g, _pad_vec(jnp.tile(sb_qn, SB_HEADS) * (-LOG2E * HEAD_DIM ** -0.5)), _pad_vec(jnp.tile(sb_kn, SB_HEADS)),
            _pad_vec(qa_g), _pad_vec(kva_g), qg, qgs, kg, _pad_vec(krg), _pad_vec(krgs), cnt]
    rows += [jnp.zeros((D_MODEL,), F32)] * (16 - len(rows))
    return w, wq, wqs, wkv, jnp.stack(rows).astype(F32)


def _pack_odd(w_in, norm_g, nsa_qn, nsa_kn, dil_qn, dil_kn):
    d = w_in.shape[0]

    def padded_heads(cols, n):
        c3 = cols.reshape(d, n, HEAD_DIM)
        return jnp.concatenate([c3, jnp.zeros_like(c3)], axis=-1).reshape(d, n * 128)

    def doubled_heads(cols, n):
        c3 = cols.reshape(d, n, HEAD_DIM)
        return jnp.concatenate([c3, c3], axis=-1).reshape(d, n * 128)

    ng = w_in[:, 1920:1956].reshape(d, NSA_KV_HEADS, NSA_HPG * 3)
    ng = jnp.pad(ng, ((0, 0), (0, 0), (0, 128 - NSA_HPG * 3))).reshape(d, NSA_KV_HEADS * 128)
    w_nsa = jnp.concatenate([
        padded_heads(w_in[:, 0:768], NSA_HEADS),
        padded_heads(w_in[:, 768:960], NSA_KV_HEADS), padded_heads(w_in[:, 960:1152], NSA_KV_HEADS),
        padded_heads(w_in[:, 1152:1344], NSA_KV_HEADS), padded_heads(w_in[:, 1344:1536], NSA_KV_HEADS),
        padded_heads(w_in[:, 1536:1728], NSA_KV_HEADS), padded_heads(w_in[:, 1728:1920], NSA_KV_HEADS),
        ng, w_in[:, 1956:2724], w_in[:, 5028:5284]], axis=1).astype(BF16)
    w_dil = w_in[:, 2724:5028].astype(BF16)
    z64 = jnp.zeros((64,), F32)
    n = NSA_HEADS * 128
    rows = [_pad_vec(norm_g, n),
            jnp.tile(jnp.concatenate([nsa_qn * (LOG2E * HEAD_DIM ** -0.5), z64]), NSA_HEADS),
            _pad_vec(jnp.tile(jnp.concatenate([nsa_kn, z64]), NSA_KV_HEADS), n),
            _pad_vec(jnp.tile(dil_qn, DIL_HEADS) * HEAD_DIM ** -0.5, n), _pad_vec(jnp.tile(dil_kn, DIL_HEADS), n)]
    rows += [jnp.zeros((n,), F32)] * (8 - len(rows))
    return w_nsa, w_dil, jnp.stack(rows).astype(F32)


def _pack_compress(w, pe, double):
    w3 = w.reshape(CMP_LEN, HEAD_DIM, HEAD_DIM)
    w3 = jnp.concatenate([w3, jnp.zeros_like(w3)], axis=1)
    w3 = jnp.concatenate([w3, w3 if double else jnp.zeros_like(w3)], axis=2)
    wp = w3.reshape(2, CMP_STRIDE * 128, 128).astype(BF16)
    pe2 = jnp.concatenate([pe, jnp.zeros_like(pe)], axis=1).reshape(2, 1, CMP_STRIDE * 128)
    pe2 = jnp.broadcast_to(pe2, (2, 8, CMP_STRIDE * 128)).astype(F32)
    return wp, pe2


def _normed_len(gain, sizes):
    tot, off = 0.0, 0
    for n in sizes:
        tot = tot + n * jnp.max(jnp.square(gain[off:off + n]))
        off += n
    return jnp.sqrt(tot)


def _mla_logit_bound(qn, kn):
    sizes = (MLA_NOPE, MLA_ROPE)
    scale = (MLA_NOPE + MLA_ROPE) ** -0.5 * LOG2E
    return (_normed_len(qn, sizes) * _normed_len(kn, sizes) * (scale * ROUNDING_MARGIN)).reshape(1).astype(F32)


def _alibi_slopes(n):
    return 2.0 ** (-8.0 * jnp.arange(1, n + 1, dtype=jnp.float32) / n)


def kernel(x, c, positions, ada_w, ada_b, norm_g, ev_w_in, ev_w_out, sb_qn, sb_kn, mla_qa_g, mla_wq_up,
           mla_kva_g, mla_wkv_up, mla_qn, mla_kn, od_w_in, od_w_out, nsa_qn, nsa_kn, nsa_cmp_wk, nsa_cmp_wv,
           nsa_cmp_pe_k, nsa_cmp_pe_v, dil_qn, dil_kn):
    b, s, d = x.shape
    depth = ada_w.shape[0]
    pos_f = positions.astype(F32)

    inv_freq = ROPE_BASE ** (-jnp.arange(0, MLA_ROPE, 2, dtype=F32) / MLA_ROPE)
    ang = pos_f[:, None] * inv_freq[None, :]
    cos, sin = jnp.cos(ang), jnp.sin(ang)
    cos128 = jnp.concatenate([jnp.ones((s, 64), F32), cos, cos, jnp.zeros((s, 32), F32)], axis=1)
    sin128 = jnp.concatenate([jnp.zeros((s, 64), F32), -sin, sin, jnp.zeros((s, 32), F32)], axis=1)
    nsa_slopes = _alibi_slopes(NSA_HEADS)
    dil_slopes = _alibi_slopes(N_DIL * DIL_HEADS)
    nch = s // CMP_STRIDE
    chunk_sum = pos_f.reshape(nch, CMP_STRIDE).sum(axis=1)
    cpos = ((chunk_sum + jnp.roll(chunk_sum, -1)) / CMP_LEN).reshape(nch, 1)
    n_sel = s // SEL_LEN
    cst = np.arange(nch)[:, None] * CMP_STRIDE
    jst = np.arange(LANES)[None, :] * SEL_LEN
    ovl = ((cst <= jst + SEL_LEN - 1) & (cst + CMP_LEN - 1 >= jst) & (np.arange(LANES)[None, :] < n_sel))
    ovlt = jnp.asarray(ovl.astype(np.float32).T, BF16)
    pos_col = pos_f.reshape(s, 1)
    tqc = min(TQ_CMP, s)
    pos_rows = pos_f.reshape(s // tqc, 1, tqc)
    tqn = min(TQ_NSA, s)
    tkn = min(TK_NSA, tqn)
    assert tkn == TK_NSA
    pos2d = pos_f.reshape(s // tkn, 1, tkn)
    pos_sorted = jnp.all(pos_f[1:] >= pos_f[:-1])

    g64 = _group_matrix([(64, 1)] * 4, 256)
    gm = _group_matrix([(64, 1), (32, 1), (32, 0)] * 2, 256)
    tks = min(TK_SB, min(TQ_SB, s))
    tri = jnp.asarray(np.tril(np.ones((tks, tks), np.float32)), BF16)

    c8 = jnp.pad(c, ((0, 8 - b), (0, 0)))
    mod_all = _modulation(c8, ada_w, ada_b)

    for layer in range(depth):
        j = layer // 2
        mod3 = mod_all[layer, :b].reshape(b, 1, 3 * d)
        if layer % 2 == 0:
            w, wq, wqs, wkv, vec = _pack_even(ev_w_in[j], norm_g[layer], sb_qn[j], sb_kn[j], mla_qa_g[j],
                                              mla_wq_up[j], mla_kva_g[j], mla_wkv_up[j], mla_qn[j], mla_kn[j])
            sbq, sbk, sbv, gz, mq, mk, mv = _k1_even(x, mod3, vec, w, wq, wqs, wkv, g64, gm, cos128, sin128)
            o_sb = _sb_attention(sbq, sbk, sbv, tri)
            o_mla = _mla_attention(mq, mk, mv, _mla_logit_bound(mla_qn[j], mla_kn[j]))
            x = _out_even(x, mod3, o_sb, o_mla, gz, ev_w_out[j].astype(BF16))
        else:
            w_nsa, w_dil, vec = _pack_odd(od_w_in[j], norm_g[layer], nsa_qn[j], nsa_kn[j], dil_qn[j], dil_kn[j])
            q, ck, cv, sk, sv, wk, wv, gates, gz = _k1_nsa(x, mod3, vec, w_nsa, g64)
            dq, dk, dv = _k1_dil(x, mod3, vec, w_dil, g64)
            wck, pek = _pack_compress(nsa_cmp_wk[j], nsa_cmp_pe_k[j], False)
            wcv, pev = _pack_compress(nsa_cmp_wv[j], nsa_cmp_pe_v[j], True)
            kgain = jnp.concatenate([nsa_kn[j], jnp.zeros((64,), F32)]).reshape(1, 128)
            kc, vc = _compress(ck, cv, wck, wcv, pek, pev, kgain, g64[0:128, 0:128])
            o_c, selb, used = _nsa_cmp(nsa_slopes, q, kc, vc, ovlt, cpos, pos_rows, gates)
            flags = used[:, :, :, 0, :].reshape(b, NSA_KV_HEADS, s // tqn, tqn // min(TQ_CMP, s),
                                                SEL_TILE_STRIDE, LANES // SEL_TILE_STRIDE).max(axis=(3, 5))
            flags = (flags > 0).astype(jnp.int32).reshape(-1)
            nsa_bound = (_normed_len(nsa_qn[j], (HEAD_DIM,)) * _normed_len(nsa_kn[j], (HEAD_DIM,))
                         * (HEAD_DIM ** -0.5 * LOG2E * ROUNDING_MARGIN)).reshape(1).astype(F32)
            safe = (nsa_bound[0] <= SAFE_LOGIT_BOUND) & pos_sorted
            o_s = _nsa_gqa(flags, nsa_slopes, nsa_bound, safe, q, selb, sk, sv, pos2d, pos_col, gates, 1)
            o_w = _nsa_gqa(flags, nsa_slopes, nsa_bound, safe, q, selb, wk, wv, pos2d, pos_col, gates, 2)
            dils, lses = [], []
            for g in range(N_DIL):
                o, lse = _dilated(dil_slopes, dq[g], dk[g], dv[g], pos_f, g)
                dils.append(o)
                lses.append(lse)
            x = _out_odd(x, mod3, o_c, o_s, o_w, dils, lses, gz, od_w_out[j].astype(BF16))
    return x
```

```python
import functools

import numpy as np
import jax
import jax.numpy as jnp
from jax import lax
from jax.experimental import pallas as pl
from jax.experimental.pallas import tpu as pltpu

F32 = jnp.float32
BF16 = jnp.bfloat16

D_MODEL = 1024
HEAD_DIM = 64
NORM_EPS = 1e-6
TINY = 1e-30
SB_HEADS = 8
MLA_HEADS = 8
MLA_Q_RANK = 256
MLA_KV_RANK = 128
MLA_NOPE = 64
MLA_ROPE = 32
MLA_V = 64
ROPE_BASE = 10000.0
NSA_HEADS = 12
NSA_KV_HEADS = 3
NSA_HPG = 4
CMP_LEN = 32
CMP_STRIDE = 16
SEL_LEN = 64
SEL_TOPN = 16
WIN = 512
FORCE_BONUS = 1e3
DIL_CFG = ((128, 1), (512, 4), (2048, 16))
N_DIL = 3
DIL_HEADS = 4
SB_W = SB_HEADS * HEAD_DIM
MLA_OUT = MLA_HEADS * MLA_V
NSA_W = NSA_HEADS * HEAD_DIM
NSA_KV_W = NSA_KV_HEADS * HEAD_DIM
DIL_W = DIL_HEADS * HEAD_DIM

LANES = 128
MXU_DIM = 256
NEG_MASK = -1e30
LOG2E = 1.4426950408889634
SAFE_LOGIT_BOUND = 50.0
ROUNDING_MARGIN = 1.02
SEL_OFF = 2.0 ** 30
VMEM_LIMIT = 56 * 1024 * 1024

TS_PROJ = 512
TQ_SB = 1024
TK_SB = 256
TQ_MLA = 1024
TD_MLA = 512
TQ_CMP = 256
TQ_NSA = 512
TK_NSA = 512
SEL_TILE_STRIDE = LANES * SEL_LEN // TK_NSA
T_DIL = 128


def _dot(a, b):
    return jnp.dot(a, b, preferred_element_type=F32)


def _dot_nt(a, b):
    return lax.dot_general(a, b, (((1,), (1,)), ((), ())), preferred_element_type=F32)


def _split_hl(a):
    hi = a.astype(BF16)
    lo = (a - hi.astype(F32)).astype(BF16)
    return hi, lo


def _dot_hl(a, b):
    hi, lo = _split_hl(a)
    return _dot(hi, b) + _dot(lo, b)


def _sigmoid(z):
    return 1.0 / (1.0 + jnp.exp(-z))


def _cparams(n_axes):
    return pltpu.CompilerParams(dimension_semantics=("arbitrary",) * n_axes,
                                vmem_limit_bytes=VMEM_LIMIT)


def _full(shape):
    n = len(shape)
    return pl.BlockSpec(shape, lambda *a, _n=n: (0,) * _n)


def _mod_kernel(c_ref, w_ref, b_ref, o_ref):
    c = c_ref[...]
    a = c * _sigmoid(c)
    ah, al = _split_hl(a)
    wh, wl = _split_hl(w_ref[0])
    o_ref[0] = _dot(ah, wh) + _dot(ah, wl) + _dot(al, wh) + b_ref[0]


def _modulation(c8, ada_w, ada_b):
    depth, d, n3 = ada_w.shape
    tn = 1024
    return pl.pallas_call(
        _mod_kernel,
        grid=(depth, n3 // tn),
        in_specs=[pl.BlockSpec((8, d), lambda l, j: (0, 0)),
                  pl.BlockSpec((1, d, tn), lambda l, j: (l, 0, j)),
                  pl.BlockSpec((1, 1, tn), lambda l, j: (l, 0, j))],
        out_specs=pl.BlockSpec((1, 8, tn), lambda l, j: (l, 0, j)),
        out_shape=jax.ShapeDtypeStruct((depth, 8, n3), F32),
        compiler_params=_cparams(2),
        name="adaln_mod",
    )(c8, ada_w, ada_b.reshape(depth, 1, n3))


def _modulated(x_ref, mod_ref, ng):
    x = x_ref[0]
    mod = mod_ref[0]
    shift = mod[:, 0:D_MODEL]
    scale = mod[:, D_MODEL:2 * D_MODEL]
    ms = jnp.mean(x * x, axis=-1, keepdims=True)
    h = x * lax.rsqrt(ms + NORM_EPS) * ng
    h = h * (1.0 + scale) + shift
    return h.astype(BF16)


def _group_norm_chunk(t, g, inv_cnt):
    ss = _dot_hl(t * t, g)
    return t * lax.rsqrt(ss * inv_cnt + NORM_EPS)


def _row_rms(t, gain):
    ms = jnp.mean(t * t, axis=-1, keepdims=True)
    return t * lax.rsqrt(ms + NORM_EPS) * gain


EV_COLS = 3200


def _k1_even_kernel(x_ref, mod_ref, vec_ref, w_ref, wq_ref, wqs_ref, wkv_ref, g64_ref, gm_ref,
                    cos_ref, sin_ref, sbq_o, sbk_o, sbv_o, gz_o, mq_o, mk_o, mv_o):
    hb = _modulated(x_ref, mod_ref, vec_ref[0:1, :])
    g64 = g64_ref[...]
    gm = gm_ref[...]
    inv64 = 1.0 / HEAD_DIM

    u = _dot(hb, w_ref[:, 0:1536])
    for c in range(2):
        sl = slice(c * 256, (c + 1) * 256)
        qn = _group_norm_chunk(u[:, c * 256:(c + 1) * 256], g64, inv64)
        sbq_o[0, :, sl] = (qn * vec_ref[1:2, sl]).astype(BF16)
        kn = _group_norm_chunk(u[:, 512 + c * 256:512 + (c + 1) * 256], g64, inv64)
        sbk_o[0, :, sl] = (kn * vec_ref[2:3, sl]).astype(BF16)
    sbv_o[0] = u[:, 1024:1536].astype(BF16)

    uz = _dot(hb, w_ref[:, 1536:2560])
    gz_o[0] = (uz * _sigmoid(uz)).astype(BF16)

    ul = _dot(hb, w_ref[:, 2560:3200])
    qlat = _row_rms(ul[:, 0:256], vec_ref[3:4, 0:256]).astype(BF16)
    kvlat = _row_rms(ul[:, 256:384], vec_ref[4:5, 0:128]).astype(BF16)
    krm = ul[:, 384:512]
    krs = ul[:, 512:640]
    cs = cos_ref[...]
    sn = sin_ref[...]
    cs2 = jnp.concatenate([cs, cs], axis=1)
    sn2 = jnp.concatenate([sn, sn], axis=1)

    tq = _dot(qlat, wq_ref[...])
    tqs = _dot(qlat, wqs_ref[...])
    for p in range(4):
        sl = slice(p * 256, (p + 1) * 256)
        tc = tq[:, p * 256:(p + 1) * 256]
        ss = _dot_hl(tc * tc, gm)
        inv = lax.rsqrt(ss * vec_ref[10:11, sl] + NORM_EPS)
        a = tc * inv * vec_ref[5:6, sl]
        b = tqs[:, p * 256:(p + 1) * 256] * inv * vec_ref[6:7, sl]
        mq_o[0, :, sl] = (a * cs2 + b * sn2).astype(BF16)

    sskr = _dot_hl(krm * krm, gm[0:128, 0:128])
    invr = lax.rsqrt(sskr * vec_ref[10:11, 0:128] + NORM_EPS)
    kr = (krm * invr * vec_ref[8:9, 0:128]) * cs + (krs * invr * vec_ref[9:10, 0:128]) * sn
    kr2 = jnp.concatenate([kr, kr], axis=1)

    kv = _dot(kvlat, wkv_ref[...])
    for p in range(4):
        sl = slice(p * 256, (p + 1) * 256)
        kn = _group_norm_chunk(kv[:, p * 256:(p + 1) * 256], g64, inv64)
        mk_o[0, :, sl] = (kn * vec_ref[7:8, sl] + kr2).astype(BF16)
    mv_o[0] = kv[:, 1024:1536].astype(BF16)


def _k1_even(x, mod3, vec, w, wq, wqs, wkv, g64, gm, cos128, sin128):
    b, s, d = x.shape
    ts = min(TS_PROJ, s)
    row = lambda n: pl.BlockSpec((1, ts, n), lambda i, j: (i, j, 0))
    outs = [(512, BF16), (512, BF16), (512, BF16), (1024, BF16), (1024, BF16), (1024, BF16), (512, BF16)]
    return pl.pallas_call(
        _k1_even_kernel,
        grid=(b, s // ts),
        in_specs=[row(d),
                  pl.BlockSpec((1, 1, 3 * d), lambda i, j: (i, 0, 0)),
                  _full(vec.shape), _full(w.shape), _full(wq.shape), _full(wqs.shape), _full(wkv.shape),
                  _full(g64.shape), _full(gm.shape),
                  pl.BlockSpec((ts, 128), lambda i, j: (j, 0)),
                  pl.BlockSpec((ts, 128), lambda i, j: (j, 0))],
        out_specs=[row(n) for n, _ in outs],
        out_shape=[jax.ShapeDtypeStruct((b, s, n), dt) for n, dt in outs],
        compiler_params=_cparams(2),
        name="inproj_even",
    )(x, mod3, vec, w, wq, wqs, wkv, g64, gm, cos128, sin128)


def _sb_kernel(q_ref, k_ref, v_ref, tri_ref, o_ref, acc_ref, car_ref, *, tq, tk, bounded):
    qi = pl.program_id(2)
    q = q_ref[0]
    lane = lax.broadcasted_iota(jnp.int32, (1, LANES), 1)
    lo_half = lane < HEAD_DIM
    zero = jnp.zeros_like(q)
    qs = (jnp.where(lo_half, q, zero), jnp.where(lo_half, zero, q))
    tri = tri_ref[...]
    acc_ref[...] = jnp.zeros_like(acc_ref)
    car_ref[...] = jnp.zeros_like(car_ref)
    col0 = lax.broadcasted_iota(jnp.int32, (1, tk), 1)

    def tile(kt, r0, masked):
        n = tq - r0
        off = pl.multiple_of(kt * tk, tk)
        k = k_ref[0, pl.ds(off, tk), :]
        v = v_ref[0, pl.ds(off, tk), :]
        if masked:
            rows = qi * tq + r0 + lax.broadcasted_iota(jnp.int32, (n, 1), 0)
            strict = (kt * tk + col0) < rows
        for i in range(2):
            z = _dot_nt(qs[i][r0:tq], k)
            if bounded:
                lom = jnp.log(1.0 + jnp.exp2(z)) * (-LOG2E)
            else:
                lom = -jnp.maximum(z, 0.0) - jnp.log(1.0 + jnp.exp2(-jnp.abs(z))) * LOG2E
            if masked:
                lom = jnp.where(strict, lom, 0.0)
            tt = _dot(lom.astype(BF16), tri)
            car = car_ref[i, r0:tq]
            w = jnp.exp2(tt + z + jnp.concatenate([car] * (tk // LANES), axis=1))
            if masked:
                w = jnp.where(strict, w, 0.0)
            acc_ref[i, r0:tq] += _dot(w.astype(BF16), v)
            car_ref[i, r0:tq] = car + jnp.broadcast_to(tt[:, 0:1], (n, LANES))

    per = tq // tk
    for c in reversed(range(per)):
        tile(qi * per + c, c * tk, True)

    def body(j, carry):
        for u in range(per):
            tile(qi * per - 1 - per * j - u, 0, False)
        return carry

    lax.fori_loop(0, qi, body, 0)
    o_ref[0] = jnp.where(lo_half, acc_ref[0], acc_ref[1])


def _sb_attention(q, k, v, tri, safe):
    b, s, w = q.shape
    tq = min(TQ_SB, s)
    tk = min(TK_SB, tq)

    def call(bounded):
        kern = functools.partial(_sb_kernel, tq=tq, tk=tk, bounded=bounded)
        return pl.pallas_call(
            kern,
            grid=(b, w // LANES, s // tq),
            in_specs=[pl.BlockSpec((1, tq, LANES), lambda i, p, j: (i, j, p)),
                      pl.BlockSpec((1, s, LANES), lambda i, p, j: (i, 0, p)),
                      pl.BlockSpec((1, s, LANES), lambda i, p, j: (i, 0, p)),
                      _full(tri.shape)],
            out_specs=pl.BlockSpec((1, tq, LANES), lambda i, p, j: (i, j, p)),
            out_shape=jax.ShapeDtypeStruct((b, s, w), F32),
            scratch_shapes=[pltpu.VMEM((2, tq, LANES), F32), pltpu.VMEM((2, tq, LANES), F32)],
            compiler_params=_cparams(3),
            name="stickbreak_attn_bounded" if bounded else "stickbreak_attn",
        )(q, k, v, tri)

    return lax.cond(safe, lambda: call(True), lambda: call(False))


def _lane_max(s):
    m = s[:, 0:LANES]
    for c in range(1, s.shape[1] // LANES):
        m = jnp.maximum(m, s[:, c * LANES:(c + 1) * LANES])
    return m


def _mla_kernel(bnd_ref, q_ref, k_ref, v_ref, o_ref, acc_ref, m_ref, *, tq, td, bounded):
    qi = pl.program_id(2)
    q = q_ref[0]
    qs = (q[:, 0:LANES], q[:, LANES:2 * LANES])
    lane = lax.broadcasted_iota(jnp.int32, (1, LANES), 1)
    lo_half = lane < HEAD_DIM
    acc_ref[...] = jnp.zeros_like(acc_ref)
    if bounded:
        m_ref[...] = jnp.full(m_ref.shape, bnd_ref[0], F32)
    else:
        m_ref[...] = jnp.full_like(m_ref, NEG_MASK)

    def scores(off, width, r0, masked):
        k = k_ref[0, pl.ds(off, width), :]
        if masked:
            rows = qi * tq + r0 + lax.broadcasted_iota(jnp.int32, (tq - r0, 1), 0)
            keep = (off + lax.broadcasted_iota(jnp.int32, (1, width), 1)) <= rows
        out = []
        for i in range(2):
            s = _dot_nt(qs[i][r0:tq], k[:, i * LANES:(i + 1) * LANES])
            if masked:
                s = jnp.where(keep, s, NEG_MASK)
            out.append(s)
        return out

    def sweep(fn):
        def body(j, carry):
            fn(pl.multiple_of(j * tq, tq), tq, 0, False)
            return carry
        lax.fori_loop(0, qi, body, 0)
        for c in range(tq // td):
            fn(pl.multiple_of(qi * tq + c * td, td), td, c * td, True)

    def row_max(off, width, r0, masked):
        for i, s in enumerate(scores(off, width, r0, masked)):
            m_ref[i, r0:tq] = jnp.maximum(m_ref[i, r0:tq], _lane_max(s))

    def accumulate(off, width, r0, masked):
        vx = jnp.concatenate([v_ref[0, pl.ds(off, width), :], jnp.ones((width, LANES), BF16)], axis=1)
        for i, s in enumerate(scores(off, width, r0, masked)):
            p = jnp.exp2(s - jnp.concatenate([m_ref[i, r0:tq]] * (width // LANES), axis=1))
            acc_ref[i, r0:tq] += _dot(p.astype(BF16), vx)

    if not bounded:
        sweep(row_max)
        for i in range(2):
            m_ref[i] = jnp.broadcast_to(jnp.max(m_ref[i], axis=-1, keepdims=True), (tq, LANES))
    sweep(accumulate)
    o_ref[0] = jnp.where(lo_half, acc_ref[0, :, 0:LANES] / acc_ref[0, :, LANES:2 * LANES],
                         acc_ref[1, :, 0:LANES] / acc_ref[1, :, LANES:2 * LANES])


def _mla_attention(q, k, v, bound):
    b, s, _ = q.shape
    tq = min(TQ_MLA, s)
    td = min(TD_MLA, tq)

    def call(bounded):
        kern = functools.partial(_mla_kernel, tq=tq, td=td, bounded=bounded)
        return pl.pallas_call(
            kern,
            grid=(b, MLA_HEADS // 2, s // tq),
            in_specs=[pl.BlockSpec(memory_space=pltpu.SMEM),
                      pl.BlockSpec((1, tq, 2 * LANES), lambda i, p, j: (i, j, p)),
                      pl.BlockSpec((1, s, 2 * LANES), lambda i, p, j: (i, 0, p)),
                      pl.BlockSpec((1, s, LANES), lambda i, p, j: (i, 0, p))],
            out_specs=pl.BlockSpec((1, tq, LANES), lambda i, p, j: (i, j, p)),
            out_shape=jax.ShapeDtypeStruct((b, s, MLA_OUT), F32),
            scratch_shapes=[pltpu.VMEM((2, tq, 2 * LANES), F32), pltpu.VMEM((2, tq, LANES), F32)],
            compiler_params=_cparams(3),
            name="mla_attn_bounded" if bounded else "mla_attn",
        )(bound, q, k, v)

    return lax.cond(bound[0] <= SAFE_LOGIT_BOUND, lambda: call(True), lambda: call(False))


def _out_even_kernel(x_ref, mod_ref, osb_ref, omla_ref, gz_ref, w_ref, o_ref):
    gz = gz_ref[0].astype(F32)
    m1 = (osb_ref[0] * gz[:, 0:SB_W]).astype(BF16)
    m2 = (omla_ref[0] * gz[:, SB_W:SB_W + MLA_OUT]).astype(BF16)
    y = _dot(m1, w_ref[0:SB_W, :]) + _dot(m2, w_ref[SB_W:SB_W + MLA_OUT, :])
    gate = mod_ref[0][:, 2 * D_MODEL:3 * D_MODEL]
    o_ref[0] = x_ref[0] + gate * y


def _out_even(x, mod3, osb, omla, gz, w):
    b, s, d = x.shape
    ts = min(TS_PROJ, s)
    row = lambda n: pl.BlockSpec((1, ts, n), lambda i, j: (i, j, 0))
    return pl.pallas_call(
        _out_even_kernel,
        grid=(b, s // ts),
        in_specs=[row(d), pl.BlockSpec((1, 1, 3 * d), lambda i, j: (i, 0, 0)),
                  row(SB_W), row(MLA_OUT), row(SB_W + MLA_OUT), _full(w.shape)],
        out_specs=row(d),
        out_shape=jax.ShapeDtypeStruct((b, s, d), F32),
        compiler_params=_cparams(2),
        name="outproj_even",
    )(x, mod3, osb, omla, gz, w)


NSA_COLS = 5248


def _k1_nsa_kernel(x_ref, mod_ref, vec_ref, w_ref, g64_ref,
                   q_o, ck_o, cv_o, sk_o, sv_o, wk_o, wv_o, gt_o, gz_o, chunk_ref):
    ts = x_ref.shape[1]
    hb = _modulated(x_ref, mod_ref, vec_ref[0:1, 0:D_MODEL])
    g64 = g64_ref[...]
    inv64 = 1.0 / HEAD_DIM

    uq = _dot(hb, w_ref[:, 0:1536])
    for c in range(6):
        sl = slice(c * 256, (c + 1) * 256)
        qn = _group_norm_chunk(uq[:, c * 256:(c + 1) * 256], g64, inv64)
        q_o[0, :, sl] = (qn * vec_ref[1:2, sl]).astype(BF16)

    uc = _dot(hb, w_ref[:, 1536:2304])
    for c in range(2 * NSA_KV_HEADS):
        chunk_ref[c] = uc[:, c * 128:(c + 1) * 128]
    for g in range(NSA_KV_HEADS):
        for tok in range(CMP_STRIDE):
            rows = pl.ds(tok, ts // CMP_STRIDE, stride=CMP_STRIDE)
            sl = slice(tok * 128, (tok + 1) * 128)
            ck_o[0, g, :, sl] = chunk_ref[g, rows, :].astype(BF16)
            cv_o[0, g, :, sl] = chunk_ref[NSA_KV_HEADS + g, rows, :].astype(BF16)

    us = _dot(hb, w_ref[:, 2304:3840])
    kgain = vec_ref[2:3, 0:384]
    g128 = g64[0:128, 0:128]
    for g in range(NSA_KV_HEADS):
        sl = slice(g * 128, (g + 1) * 128)
        t = us[:, g * 128:(g + 1) * 128]
        ss = _dot_hl(t * t, g128)
        sk_o[0, :, sl] = (t * lax.rsqrt(ss * inv64 + NORM_EPS) * kgain[:, sl]).astype(BF16)
        t = us[:, 768 + g * 128:768 + (g + 1) * 128]
        ss = _dot_hl(t * t, g128)
        wk_o[0, :, sl] = (t * lax.rsqrt(ss * inv64 + NORM_EPS) * kgain[:, sl]).astype(BF16)
    ones_hi = (lax.broadcasted_iota(jnp.int32, (1, 384), 1) % 128) >= HEAD_DIM
    sv_o[0] = jnp.where(ones_hi, 1.0, us[:, 384:768]).astype(BF16)
    wv_o[0] = jnp.where(ones_hi, 1.0, us[:, 1152:1536]).astype(BF16)

    ug = _dot(hb, w_ref[:, 3840:4224])
    gt_o[0] = _sigmoid(ug)
    uz = _dot(hb, w_ref[:, 4224:5248])
    gz_o[0] = (uz * _sigmoid(uz)).astype(BF16)


def _k1_nsa(x, mod3, vec, w, g64):
    b, s, d = x.shape
    ts = min(TS_PROJ, s)
    row = lambda n: pl.BlockSpec((1, ts, n), lambda i, j: (i, j, 0))
    cw = CMP_STRIDE * 128
    grp = pl.BlockSpec((1, NSA_KV_HEADS, ts // CMP_STRIDE, cw), lambda i, j: (i, 0, j, 0))
    return pl.pallas_call(
        _k1_nsa_kernel,
        grid=(b, s // ts),
        in_specs=[row(d), pl.BlockSpec((1, 1, 3 * d), lambda i, j: (i, 0, 0)),
                  _full(vec.shape), _full(w.shape), _full(g64.shape)],
        out_specs=[row(1536), grp, grp, row(384), row(384), row(384), row(384), row(384), row(1024)],
        out_shape=[jax.ShapeDtypeStruct((b, s, 1536), BF16),
                   jax.ShapeDtypeStruct((b, NSA_KV_HEADS, s // CMP_STRIDE, cw), BF16),
                   jax.ShapeDtypeStruct((b, NSA_KV_HEADS, s // CMP_STRIDE, cw), BF16),
                   jax.ShapeDtypeStruct((b, s, 384), BF16),
                   jax.ShapeDtypeStruct((b, s, 384), BF16),
                   jax.ShapeDtypeStruct((b, s, 384), BF16),
                   jax.ShapeDtypeStruct((b, s, 384), BF16),
                   jax.ShapeDtypeStruct((b, s, 384), F32),
                   jax.ShapeDtypeStruct((b, s, 1024), BF16)],
        scratch_shapes=[pltpu.VMEM((2 * NSA_KV_HEADS, ts, 128), F32)],
        compiler_params=_cparams(2),
        name="inproj_nsa",
    )(x, mod3, vec, w, g64)


def _k1_dil_kernel(x_ref, mod_ref, vec_ref, w_ref, g64_ref, *refs):
    outs, st = refs[:3 * N_DIL], refs[3 * N_DIL]
    ts = x_ref.shape[1]
    hb = _modulated(x_ref, mod_ref, vec_ref[0:1, 0:D_MODEL])
    g64 = g64_ref[...]
    inv64 = 1.0 / HEAD_DIM
    u = _dot(hb, w_ref[...])
    for g in range(N_DIL):
        dil = DIL_CFG[g][1]
        qn = _group_norm_chunk(u[:, g * 256:(g + 1) * 256], g64, inv64) * vec_ref[3:4, 0:256]
        kn = _group_norm_chunk(u[:, 768 + g * 256:768 + (g + 1) * 256], g64, inv64) * vec_ref[4:5, 0:256]
        vals = (qn, kn, u[:, 1536 + g * 256:1536 + (g + 1) * 256])
        for j, val in enumerate(vals):
            o_ref = outs[3 * g + j]
            if dil == 1:
                o_ref[0] = val.astype(BF16)
                continue
            for h in range(2):
                st[j, h] = val[:, h * 128:(h + 1) * 128]
            for r in range(dil):
                for h in range(2):
                    sl = slice(r * 256 + h * 128, r * 256 + (h + 1) * 128)
                    o_ref[0, :, sl] = st[j, h, pl.ds(r, ts // dil, stride=dil), :].astype(BF16)


def _k1_dil(x, mod3, vec, w, g64):
    b, s, d = x.shape
    ts = min(TS_PROJ, s)
    row = lambda n: pl.BlockSpec((1, ts, n), lambda i, j: (i, j, 0))
    specs, shapes = [], []
    for g in range(N_DIL):
        dil = DIL_CFG[g][1]
        for _ in range(3):
            specs.append(pl.BlockSpec((1, ts // dil, dil * 256), lambda i, j: (i, j, 0)))
            shapes.append(jax.ShapeDtypeStruct((b, s // dil, dil * 256), BF16))
    return pl.pallas_call(
        _k1_dil_kernel,
        grid=(b, s // ts),
        in_specs=[row(d), pl.BlockSpec((1, 1, 3 * d), lambda i, j: (i, 0, 0)),
                  _full(vec.shape), _full(w.shape), _full(g64.shape)],
        out_specs=specs,
        out_shape=shapes,
        scratch_shapes=[pltpu.VMEM((3, 2, ts, 128), F32)],
        compiler_params=_cparams(2),
        name="inproj_dil",
    )(x, mod3, vec, w, g64)


def _compress_kernel(xk_ref, xv_ref, wk_ref, wv_ref, pek_ref, pev_ref, gain_ref, g128_ref, kc_o, vc_o):
    nch = xk_ref.shape[2]

    def comp(x_ref, w_ref, pe_ref):
        x = x_ref[0, 0]
        lo = _dot(x, w_ref[0])
        hi = _dot(x, w_ref[1])
        pec = _dot_hl(pe_ref[0], w_ref[0]) + _dot_hl(pe_ref[1], w_ref[1])
        return lo + pltpu.roll(hi, nch - 1, 0) + pec[0:1, :]

    kc = comp(xk_ref, wk_ref, pek_ref)
    ss = _dot_hl(kc * kc, g128_ref[...])
    kc_o[0, 0] = (kc * lax.rsqrt(ss * (1.0 / HEAD_DIM) + NORM_EPS) * gain_ref[...]).astype(BF16)
    vc_o[0, 0] = comp(xv_ref, wv_ref, pev_ref).T.astype(BF16)


def _compress(ck, cv, wk, wv, pek, pev, gain, g128):
    b, g, nch, _ = ck.shape
    xk, xv = ck, cv
    blk = pl.BlockSpec((1, 1, nch, CMP_STRIDE * 128), lambda i, j: (i, j, 0, 0))
    oblk = pl.BlockSpec((1, 1, nch, 128), lambda i, j: (i, j, 0, 0))
    tblk = pl.BlockSpec((1, 1, 128, nch), lambda i, j: (i, j, 0, 0))
    return pl.pallas_call(
        _compress_kernel,
        grid=(b, g),
        in_specs=[blk, blk, _full(wk.shape), _full(wv.shape), _full(pek.shape), _full(pev.shape),
                  _full(gain.shape), _full(g128.shape)],
        out_specs=[oblk, tblk],
        out_shape=[jax.ShapeDtypeStruct((b, g, nch, 128), BF16), jax.ShapeDtypeStruct((b, g, 128, nch), BF16)],
        compiler_params=_cparams(2),
        name="nsa_compress",
    )(xk, xv, wk, wv, pek, pev, gain, g128)


def _pair_select(lo_half, a, b):
    return jnp.where(lo_half, a, b)


def _nsa_cmp_kernel(slope_ref, q_ref, kc_ref, vct_ref, ovlt_ref, cposc_ref, prow_ref, gt_ref,
                    oc_o, selb_o, any_o, *, tq):
    g = pl.program_id(1)
    qi = pl.program_id(2)
    q4 = q_ref[0]
    kc = kc_ref[0, 0]
    vct = vct_ref[0, 0]
    nch = kc.shape[0]
    t = qi * tq + lax.broadcasted_iota(jnp.int32, (1, tq), 1)
    n_id = lax.broadcasted_iota(jnp.int32, (nch, 1), 0)
    valid = (n_id * CMP_STRIDE + (CMP_LEN - 1)) <= t
    dist = (prow_ref[qi] - cposc_ref[...]) * LOG2E
    gtt = gt_ref[0].T
    row_lo = lax.broadcasted_iota(jnp.int32, (LANES, 1), 0) < HEAD_DIM
    psum = jnp.zeros((nch, tq), F32)
    outs = []
    for hh in range(NSA_HPG):
        s = _dot_nt(kc, q4[:, hh * LANES:(hh + 1) * LANES]) - slope_ref[g * NSA_HPG + hh] * dist
        s = jnp.where(valid, s, -jnp.inf)
        mx = jnp.max(s, axis=0, keepdims=True)
        mx = jnp.where(mx == -jnp.inf, 0.0, mx)
        e = jnp.exp2(s - mx)
        den = jnp.maximum(jnp.sum(e, axis=0, keepdims=True), TINY)
        p = e * (1.0 / den)
        psum = psum + p
        outs.append(_dot(vct, p.astype(BF16)) * gtt[3 * hh:3 * hh + 1, :])
    oc_o[0, :, 0:LANES] = jnp.where(row_lo, outs[0], outs[1]).T
    oc_o[0, :, LANES:2 * LANES] = jnp.where(row_lo, outs[2], outs[3]).T

    hi, lo = _split_hl(psum)
    imp = _dot(ovlt_ref[...], hi) + _dot(ovlt_ref[...], lo)
    blk = lax.broadcasted_iota(jnp.int32, (LANES, 1), 0)
    cur = t >> 6
    forced = (blk == 0) | (blk == cur) | (blk == cur - 1)
    allowed = blk <= cur
    score = jnp.where(allowed, imp + jnp.where(forced, FORCE_BONUS, 0.0), -jnp.inf)
    blk_f = blk.astype(F32)

    def pick(_, carry):
        sc, sel = carry
        mx = jnp.max(sc, axis=0, keepdims=True)
        idx = jnp.min(jnp.where(sc == mx, blk_f, float(LANES)), axis=0, keepdims=True)
        hit = blk_f == idx
        return jnp.where(hit, -jnp.inf, sc), jnp.where(hit, 1.0, sel)

    _, sel = lax.fori_loop(0, SEL_TOPN, pick, (score, jnp.zeros((LANES, tq), F32)))
    chosen = jnp.where((sel > 0.5) & allowed, 1.0, 0.0).T
    selb_o[0, 0] = ((chosen - 1.0) * SEL_OFF).astype(BF16)
    used = jnp.max(chosen, axis=0, keepdims=True)
    any_o[0, 0, 0] = jnp.broadcast_to(used, (8, LANES))


def _nsa_cmp(slopes, q, kc, vct, ovlt, cposc, pos_rows, gates):
    b, s, _ = q.shape
    nch = kc.shape[2]
    tq = pos_rows.shape[2]
    kern = functools.partial(_nsa_cmp_kernel, tq=tq)
    return pl.pallas_call(
        kern,
        grid=(b, NSA_KV_HEADS, s // tq),
        in_specs=[pl.BlockSpec(memory_space=pltpu.SMEM),
                  pl.BlockSpec((1, tq, 4 * LANES), lambda i, g, j: (i, j, g)),
                  pl.BlockSpec((1, 1, nch, LANES), lambda i, g, j: (i, g, 0, 0)),
                  pl.BlockSpec((1, 1, LANES, nch), lambda i, g, j: (i, g, 0, 0)),
                  _full(ovlt.shape), _full(cposc.shape), _full(pos_rows.shape),
                  pl.BlockSpec((1, tq, LANES), lambda i, g, j: (i, j, g))],
        out_specs=[pl.BlockSpec((1, tq, 2 * LANES), lambda i, g, j: (i, j, g)),
                   pl.BlockSpec((1, 1, tq, LANES), lambda i, g, j: (i, g, j, 0)),
                   pl.BlockSpec((1, 1, 1, 8, LANES), lambda i, g, j: (i, g, j, 0, 0))],
        out_shape=[jax.ShapeDtypeStruct((b, s, NSA_W), F32),
                   jax.ShapeDtypeStruct((b, NSA_KV_HEADS, s, LANES), BF16),
                   jax.ShapeDtypeStruct((b, NSA_KV_HEADS, s // tq, 8, LANES), F32)],
        compiler_params=_cparams(3),
        name="nsa_cmp_topk",
    )(slopes, q, kc, vct, ovlt, cposc, pos_rows, gates)


def _gqa_kernel(flag_ref, slope_ref, bnd_ref, q_ref, selb_ref, k_ref, v_ref, pos_ref, pcol_ref, gt_ref, o_ref,
                qa_ref, acc_ref, m_ref, *, tq, tk, branch, bounded):
    bi = pl.program_id(0)
    g = pl.program_id(1)
    qi = pl.program_id(2)
    q4 = q_ref[0]
    lane = lax.broadcasted_iota(jnp.int32, (1, LANES), 1)
    lo_half = lane < HEAD_DIM
    pref = pos_ref[qi * (tq // tk)][:, 0:1]
    for hh in range(NSA_HPG):
        sl = slice(hh * tq, (hh + 1) * tq)
        qa_ref[sl, 0:LANES] = q4[:, hh * LANES:(hh + 1) * LANES]
        if branch == 1:
            qa_ref[sl, LANES:2 * LANES] = selb_ref[0, 0]
        if bounded:
            own = (pcol_ref[...] - pref) * (LOG2E * slope_ref[g * NSA_HPG + hh]) + bnd_ref[0]
            m_ref[sl] = jnp.broadcast_to(own, (tq, LANES))
    acc_ref[...] = jnp.zeros_like(acc_ref)
    if not bounded:
        m_ref[...] = jnp.full_like(m_ref, NEG_MASK)
    rows = qi * tq + lax.broadcasted_iota(jnp.int32, (tq, 1), 0)
    col0 = lax.broadcasted_iota(jnp.int32, (1, tk), 1)
    krow = lax.broadcasted_iota(jnp.int32, (tk, 1), 0)
    per = tq // tk
    nrep = tk // LANES
    flag0 = ((bi * NSA_KV_HEADS + g) * pl.num_programs(2) + qi) * SEL_TILE_STRIDE

    def scores(kt_true, masked):
        kt = jnp.maximum(kt_true, 0)
        off = pl.multiple_of(kt * tk, tk)
        k = k_ref[0, pl.ds(off, tk), :]
        if branch == 1:
            blk = (kt * tk + krow) >> 6
            onehot = jnp.where(lane == blk, 1.0, 0.0).astype(BF16)
            s_all = _dot_nt(qa_ref[...], jnp.concatenate([k, onehot], axis=1))
        else:
            s_all = _dot_nt(qa_ref[:, 0:LANES], k)
        rel = (pos_ref[kt] - pref) * LOG2E
        if masked:
            cols = kt_true * tk + col0
            d = rows - cols
            keep = d >= 0
            if branch == 2:
                keep = keep & (d < WIN) & (cols >= 0)
        out = []
        for hh in range(NSA_HPG):
            s = s_all[hh * tq:(hh + 1) * tq] + slope_ref[g * NSA_HPG + hh] * rel
            if masked:
                s = jnp.where(keep, s, NEG_MASK)
            out.append(s)
        return out

    def sweep(fn):
        if branch == 1:
            def body(j, carry):
                @pl.when(flag_ref[flag0 + j] != 0)
                def _():
                    fn(j, False)
                return carry
            lax.fori_loop(0, qi * per, body, 0)
            for dd in range(per):
                fn(qi * per + dd, True)
        else:
            for dd in range(per + WIN // tk):
                fn((qi + 1) * per - 1 - dd, True)

    def row_max(kt_true, masked):
        for hh, s in enumerate(scores(kt_true, masked)):
            sl = slice(hh * tq, (hh + 1) * tq)
            m_ref[sl] = jnp.maximum(m_ref[sl], _lane_max(s))

    def accumulate(kt_true, masked):
        off = pl.multiple_of(jnp.maximum(kt_true, 0) * tk, tk)
        v = v_ref[0, pl.ds(off, tk), :]
        for hh, s in enumerate(scores(kt_true, masked)):
            sl = slice(hh * tq, (hh + 1) * tq)
            p = jnp.exp2(s - jnp.concatenate([m_ref[sl]] * nrep, axis=1))
            acc_ref[sl] += _dot(p.astype(BF16), v)

    if not bounded:
        sweep(row_max)
        for hh in range(NSA_HPG):
            sl = slice(hh * tq, (hh + 1) * tq)
            m_ref[sl] = jnp.broadcast_to(jnp.max(m_ref[sl], axis=-1, keepdims=True), (tq, LANES))
    sweep(accumulate)

    gt = gt_ref[0]
    for pr in range(2):
        res = []
        for x in range(2):
            hh = 2 * pr + x
            a = acc_ref[hh * tq:(hh + 1) * tq]
            r = pltpu.roll(a, HEAD_DIM, 1)
            o = a / r if x == 0 else r / a
            res.append(o * gt[:, 3 * hh + branch:3 * hh + branch + 1])
        o_ref[0, :, pr * LANES:(pr + 1) * LANES] = _pair_select(lo_half, res[0], res[1])


def _nsa_gqa(flags, slopes, bound, safe, q, selb, k, v, pos2d, pos_col, gates, branch):
    b, s, _ = q.shape
    tq = min(TQ_NSA, s)
    tk = min(TK_NSA, tq)

    def call(bounded):
        kern = functools.partial(_gqa_kernel, tq=tq, tk=tk, branch=branch, bounded=bounded)
        smem = pl.BlockSpec(memory_space=pltpu.SMEM)
        name = ("nsa_sel_attn" if branch == 1 else "nsa_win_attn") + ("_bounded" if bounded else "")
        return pl.pallas_call(
            kern,
            grid=(b, NSA_KV_HEADS, s // tq),
            in_specs=[smem, smem, smem,
                      pl.BlockSpec((1, tq, 4 * LANES), lambda i, g, j: (i, j, g)),
                      pl.BlockSpec((1, 1, tq, LANES), lambda i, g, j: (i, g, j, 0)),
                      pl.BlockSpec((1, s, LANES), lambda i, g, j: (i, 0, g)),
                      pl.BlockSpec((1, s, LANES), lambda i, g, j: (i, 0, g)),
                      _full(pos2d.shape),
                      pl.BlockSpec((tq, 1), lambda i, g, j: (j, 0)),
                      pl.BlockSpec((1, tq, LANES), lambda i, g, j: (i, j, g))],
            out_specs=pl.BlockSpec((1, tq, 2 * LANES), lambda i, g, j: (i, j, g)),
            out_shape=jax.ShapeDtypeStruct((b, s, NSA_W), F32),
            scratch_shapes=[pltpu.VMEM((NSA_HPG * tq, 2 * LANES), BF16),
                            pltpu.VMEM((NSA_HPG * tq, LANES), F32),
                            pltpu.VMEM((NSA_HPG * tq, LANES), F32)],
            compiler_params=_cparams(3),
            name=name,
        )(flags, slopes, bound, q, selb, k, v, pos2d, pos_col, gates)

    return lax.cond(safe, lambda: call(True), lambda: call(False))


def _dil_kernel(slope_ref, q_ref, k_ref, v_ref, pc_ref, pr_ref, o_ref, lse_ref, *, t, span, grp):
    i = pl.program_id(2)
    q4 = q_ref[0]
    lane = lax.broadcasted_iota(jnp.int32, (1, LANES), 1)
    lo_half = lane < HEAD_DIM
    prev = jnp.maximum(i - 1, 0)
    o_prev = pl.multiple_of(prev * t, t)
    o_cur = pl.multiple_of(i * t, t)
    kk = jnp.concatenate([k_ref[0, pl.ds(o_prev, t), :], k_ref[0, pl.ds(o_cur, t), :]], axis=0)
    vv = jnp.concatenate([v_ref[0, pl.ds(o_prev, t), :], v_ref[0, pl.ds(o_cur, t), :]], axis=0)
    pk = jnp.concatenate([pr_ref[0, prev], pr_ref[0, i]], axis=1)
    dist = pc_ref[0] - pk
    rows = i * t + lax.broadcasted_iota(jnp.int32, (t, 1), 0)
    cols = (i - 1) * t + lax.broadcasted_iota(jnp.int32, (1, 2 * t), 1)
    d = rows - cols
    valid = (cols >= 0) & (d >= 0) & (d <= span)
    zero = jnp.zeros((t, LANES), BF16)
    for p in range(2):
        qp = q4[:, p * LANES:(p + 1) * LANES]
        kp = kk[:, p * LANES:(p + 1) * LANES]
        vp = vv[:, p * LANES:(p + 1) * LANES]
        res = []
        for x in range(2):
            qx = jnp.where(lo_half, qp, zero) if x == 0 else jnp.where(lo_half, zero, qp)
            s = _dot_nt(qx, kp) - slope_ref[grp * DIL_HEADS + 2 * p + x] * dist
            s = jnp.where(valid, s, -jnp.inf)
            mx = jnp.max(s, axis=-1, keepdims=True)
            e = jnp.exp(s - mx)
            den = jnp.maximum(jnp.sum(e, axis=-1, keepdims=True), TINY)
            res.append((_dot((e / den).astype(BF16), vp), mx + jnp.log(den)))
        sl = slice(p * LANES, (p + 1) * LANES)
        o_ref[0, :, sl] = _pair_select(lo_half, res[0][0], res[1][0])
        lse_ref[0, :, sl] = jnp.where(lo_half, res[0][1], res[1][1])


def _dilated(slopes, qv, kv, vv, pos_f, grp):
    window, dil = DIL_CFG[grp]
    b, sub, wd = qv.shape
    w = wd // dil
    t = min(T_DIL, sub)
    span = window // dil
    assert span <= t
    pres = pos_f.reshape(sub, dil).T
    pc = pres.reshape(dil, sub, 1)
    pr = pres.reshape(dil, sub // t, 1, t)
    kern = functools.partial(_dil_kernel, t=t, span=span, grp=grp)
    shp = jax.ShapeDtypeStruct((b, sub, wd), F32)
    return pl.pallas_call(
        kern,
        grid=(b, dil, sub // t),
        in_specs=[pl.BlockSpec(memory_space=pltpu.SMEM),
                  pl.BlockSpec((1, t, w), lambda i, r, j: (i, j, r)),
                  pl.BlockSpec((1, sub, w), lambda i, r, j: (i, 0, r)),
                  pl.BlockSpec((1, sub, w), lambda i, r, j: (i, 0, r)),
                  pl.BlockSpec((1, t, 1), lambda i, r, j: (r, j, 0)),
                  pl.BlockSpec((1, sub // t, 1, t), lambda i, r, j: (r, 0, 0, 0))],
        out_specs=[pl.BlockSpec((1, t, w), lambda i, r, j: (i, j, r)),
                   pl.BlockSpec((1, t, w), lambda i, r, j: (i, j, r))],
        out_shape=[shp, shp],
        compiler_params=_cparams(3),
        name="dilated_attn_g%d" % grp,
    )(slopes, qv, kv, vv, pc, pr)


def _out_odd_kernel(x_ref, mod_ref, oc_ref, os_ref, ow_ref, d0_ref, d1_ref, d2_ref,
                    l0_ref, l1_ref, l2_ref, gz_ref, w_ref, o_ref, st):
    ts = x_ref.shape[1]
    gz = gz_ref[0].astype(F32)
    nsa = oc_ref[0] + os_ref[0] + ow_ref[0]
    m1 = (nsa * gz[:, 0:NSA_W]).astype(BF16)

    def token_order(ref, g, slot):
        dil = DIL_CFG[g][1]
        if dil == 1:
            return ref[0]
        for r in range(dil):
            for h in range(2):
                st[slot, h, pl.ds(r, ts // dil, stride=dil), :] = ref[0, :, r * DIL_W + h * 128:r * DIL_W + (h + 1) * 128]
        return jnp.concatenate([st[slot, 0], st[slot, 1]], axis=1)

    d0, d1, d2 = token_order(d0_ref, 0, 0), token_order(d1_ref, 1, 0), token_order(d2_ref, 2, 1)
    l0, l1, l2 = token_order(l0_ref, 0, 2), token_order(l1_ref, 1, 2), token_order(l2_ref, 2, 3)
    mx = jnp.maximum(jnp.maximum(l0, l1), l2)
    e0, e1, e2 = jnp.exp(l0 - mx), jnp.exp(l1 - mx), jnp.exp(l2 - mx)
    dil = (e0 * d0 + e1 * d1 + e2 * d2) / (e0 + e1 + e2)
    m2 = (dil * gz[:, NSA_W:NSA_W + DIL_W]).astype(BF16)
    y = _dot(m1, w_ref[0:NSA_W, :]) + _dot(m2, w_ref[NSA_W:NSA_W + DIL_W, :])
    gate = mod_ref[0][:, 2 * D_MODEL:3 * D_MODEL]
    o_ref[0] = x_ref[0] + gate * y


def _out_odd(x, mod3, oc, os_, ow, dils, lses, gz, w):
    b, s, d = x.shape
    ts = min(TS_PROJ, s)
    row = lambda n: pl.BlockSpec((1, ts, n), lambda i, j: (i, j, 0))
    res = [pl.BlockSpec((1, ts // DIL_CFG[g][1], DIL_CFG[g][1] * DIL_W), lambda i, j: (i, j, 0))
           for g in range(N_DIL)]
    return pl.pallas_call(
        _out_odd_kernel,
        grid=(b, s // ts),
        in_specs=[row(d), pl.BlockSpec((1, 1, 3 * d), lambda i, j: (i, 0, 0)),
                  row(NSA_W), row(NSA_W), row(NSA_W)] + res + res + [row(1024), _full(w.shape)],
        out_specs=row(d),
        out_shape=jax.ShapeDtypeStruct((b, s, d), F32),
        scratch_shapes=[pltpu.VMEM((4, 2, ts, 128), F32)],
        compiler_params=_cparams(2),
        name="outproj_odd",
    )(x, mod3, oc, os_, ow, *dils, *lses, gz, w)


def _pad_cols(w, n):
    return jnp.pad(w, ((0, 0), (0, n - w.shape[1])))


def _pad_vec(v, n=D_MODEL):
    return jnp.pad(v, (0, n - v.shape[0]))


def _group_matrix(sizes, total):
    m = np.zeros((total, total), np.float32)
    off = 0
    for sz, on in sizes:
        if on:
            m[off:off + sz, off:off + sz] = 1.0
        off += sz
    return jnp.asarray(m, BF16)


def _swap_halves(w):
    h = w.shape[-1] // 2
    return jnp.concatenate([w[..., h:], w[..., :h]], axis=-1)


def _pack_even(w_in, norm_g, sb_qn, sb_kn, qa_g, wq_up, kva_g, wkv_up, qn, kn):
    d = w_in.shape[0]
    z = lambda n: jnp.zeros((d, n), w_in.dtype)
    kr = w_in[:, 2432:2464]
    w = jnp.concatenate([
        w_in[:, 0:1536], w_in[:, 1536:2048], w_in[:, 2464:2976], w_in[:, 2048:2304], w_in[:, 2304:2432],
        z(64), kr, z(32), z(64), _swap_halves(kr), z(32)], axis=1).astype(BF16)
    wq3 = wq_up.reshape(MLA_Q_RANK, MLA_HEADS, MLA_NOPE + MLA_ROPE)
    zq = jnp.zeros((MLA_Q_RANK, MLA_HEADS, 32), wq_up.dtype)
    wq = jnp.concatenate([wq3, zq], axis=-1).reshape(MLA_Q_RANK, MLA_HEADS * 128).astype(BF16)
    wqs = jnp.concatenate([jnp.zeros((MLA_Q_RANK, MLA_HEADS, 64), wq_up.dtype),
                           _swap_halves(wq3[..., MLA_NOPE:]), zq], axis=-1)
    wqs = wqs.reshape(MLA_Q_RANK, MLA_HEADS * 128).astype(BF16)
    wkv3 = wkv_up.reshape(MLA_KV_RANK, MLA_HEADS, MLA_NOPE + MLA_V)
    wk = jnp.concatenate([wkv3[..., :MLA_NOPE], jnp.zeros_like(wkv3[..., :MLA_NOPE])], axis=-1)
    wkv = jnp.concatenate([wk.reshape(MLA_KV_RANK, MLA_HEADS * 128),
                           wkv3[..., MLA_NOPE:].reshape(MLA_KV_RANK, MLA_HEADS * MLA_V)], axis=1).astype(BF16)
    z32 = jnp.zeros((32,), F32)
    z64 = jnp.zeros((64,), F32)
    scale = (MLA_NOPE + MLA_ROPE) ** -0.5 * LOG2E
    qg = jnp.tile(jnp.concatenate([qn, z32]), MLA_HEADS) * scale
    qgs = jnp.tile(jnp.concatenate([z64, _swap_halves(qn[MLA_NOPE:]), z32]), MLA_HEADS) * scale
    kg = jnp.tile(jnp.concatenate([kn[:MLA_NOPE], z64]), MLA_HEADS)
    krg = jnp.concatenate([z64, kn[MLA_NOPE:], z32])
    krgs = jnp.concatenate([z64, _swap_halves(kn[MLA_NOPE:]), z32])
    cnt = jnp.tile(jnp.concatenate([jnp.full((64,), 1.0 / 64), jnp.full((32,), 1.0 / 32), jnp.ones((32,))]),
                   MLA_HEADS).astype(F32)
    rows = [norm_g, _pad_vec(jnp.tile(sb_qn, SB_HEADS) * (LOG2E * HEAD_DIM ** -0.5)), _pad_vec(jnp.tile(sb_kn, SB_HEADS)),
            _pad_vec(qa_g), _pad_vec(kva_g), qg, qgs, kg, _pad_vec(krg), _pad_vec(krgs), cnt]
    rows += [jnp.zeros((D_MODEL,), F32)] * (16 - len(rows))
    return w, wq, wqs, wkv, jnp.stack(rows).astype(F32)


def _pack_odd(w_in, norm_g, nsa_qn, nsa_kn, dil_qn, dil_kn):
    d = w_in.shape[0]

    def padded_heads(cols, n):
        c3 = cols.reshape(d, n, HEAD_DIM)
        return jnp.concatenate([c3, jnp.zeros_like(c3)], axis=-1).reshape(d, n * 128)

    def doubled_heads(cols, n):
        c3 = cols.reshape(d, n, HEAD_DIM)
        return jnp.concatenate([c3, c3], axis=-1).reshape(d, n * 128)

    ng = w_in[:, 1920:1956].reshape(d, NSA_KV_HEADS, NSA_HPG * 3)
    ng = jnp.pad(ng, ((0, 0), (0, 0), (0, 128 - NSA_HPG * 3))).reshape(d, NSA_KV_HEADS * 128)
    w_nsa = jnp.concatenate([
        padded_heads(w_in[:, 0:768], NSA_HEADS),
        padded_heads(w_in[:, 768:960], NSA_KV_HEADS), padded_heads(w_in[:, 960:1152], NSA_KV_HEADS),
        padded_heads(w_in[:, 1152:1344], NSA_KV_HEADS), padded_heads(w_in[:, 1344:1536], NSA_KV_HEADS),
        padded_heads(w_in[:, 1536:1728], NSA_KV_HEADS), padded_heads(w_in[:, 1728:1920], NSA_KV_HEADS),
        ng, w_in[:, 1956:2724], w_in[:, 5028:5284]], axis=1).astype(BF16)
    w_dil = w_in[:, 2724:5028].astype(BF16)
    z64 = jnp.zeros((64,), F32)
    n = NSA_HEADS * 128
    rows = [_pad_vec(norm_g, n),
            jnp.tile(jnp.concatenate([nsa_qn * (LOG2E * HEAD_DIM ** -0.5), z64]), NSA_HEADS),
            _pad_vec(jnp.tile(jnp.concatenate([nsa_kn, z64]), NSA_KV_HEADS), n),
            _pad_vec(jnp.tile(dil_qn, DIL_HEADS) * HEAD_DIM ** -0.5, n), _pad_vec(jnp.tile(dil_kn, DIL_HEADS), n)]
    rows += [jnp.zeros((n,), F32)] * (8 - len(rows))
    return w_nsa, w_dil, jnp.stack(rows).astype(F32)


def _pack_compress(w, pe, double):
    w3 = w.reshape(CMP_LEN, HEAD_DIM, HEAD_DIM)
    w3 = jnp.concatenate([w3, jnp.zeros_like(w3)], axis=1)
    w3 = jnp.concatenate([w3, w3 if double else jnp.zeros_like(w3)], axis=2)
    wp = w3.reshape(2, CMP_STRIDE * 128, 128).astype(BF16)
    pe2 = jnp.concatenate([pe, jnp.zeros_like(pe)], axis=1).reshape(2, 1, CMP_STRIDE * 128)
    pe2 = jnp.broadcast_to(pe2, (2, 8, CMP_STRIDE * 128)).astype(F32)
    return wp, pe2


def _normed_len(gain, sizes):
    tot, off = 0.0, 0
    for n in sizes:
        tot = tot + n * jnp.max(jnp.square(gain[off:off + n]))
        off += n
    return jnp.sqrt(tot)


def _mla_logit_bound(qn, kn):
    sizes = (MLA_NOPE, MLA_ROPE)
    scale = (MLA_NOPE + MLA_ROPE) ** -0.5 * LOG2E
    return (_normed_len(qn, sizes) * _normed_len(kn, sizes) * (scale * ROUNDING_MARGIN)).reshape(1).astype(F32)


def _alibi_slopes(n):
    return 2.0 ** (-8.0 * jnp.arange(1, n + 1, dtype=jnp.float32) / n)


def kernel(x, c, positions, ada_w, ada_b, norm_g, ev_w_in, ev_w_out, sb_qn, sb_kn, mla_qa_g, mla_wq_up,
           mla_kva_g, mla_wkv_up, mla_qn, mla_kn, od_w_in, od_w_out, nsa_qn, nsa_kn, nsa_cmp_wk, nsa_cmp_wv,
           nsa_cmp_pe_k, nsa_cmp_pe_v, dil_qn, dil_kn):
    b, s, d = x.shape
    depth = ada_w.shape[0]
    pos_f = positions.astype(F32)

    inv_freq = ROPE_BASE ** (-jnp.arange(0, MLA_ROPE, 2, dtype=F32) / MLA_ROPE)
    ang = pos_f[:, None] * inv_freq[None, :]
    cos, sin = jnp.cos(ang), jnp.sin(ang)
    cos128 = jnp.concatenate([jnp.ones((s, 64), F32), cos, cos, jnp.zeros((s, 32), F32)], axis=1)
    sin128 = jnp.concatenate([jnp.zeros((s, 64), F32), -sin, sin, jnp.zeros((s, 32), F32)], axis=1)
    nsa_slopes = _alibi_slopes(NSA_HEADS)
    dil_slopes = _alibi_slopes(N_DIL * DIL_HEADS)
    nch = s // CMP_STRIDE
    chunk_sum = pos_f.reshape(nch, CMP_STRIDE).sum(axis=1)
    cpos = ((chunk_sum + jnp.roll(chunk_sum, -1)) / CMP_LEN).reshape(nch, 1)
    n_sel = s // SEL_LEN
    cst = np.arange(nch)[:, None] * CMP_STRIDE
    jst = np.arange(LANES)[None, :] * SEL_LEN
    ovl = ((cst <= jst + SEL_LEN - 1) & (cst + CMP_LEN - 1 >= jst) & (np.arange(LANES)[None, :] < n_sel))
    ovlt = jnp.asarray(ovl.astype(np.float32).T, BF16)
    pos_col = pos_f.reshape(s, 1)
    tqc = min(TQ_CMP, s)
    pos_rows = pos_f.reshape(s // tqc, 1, tqc)
    tqn = min(TQ_NSA, s)
    tkn = min(TK_NSA, tqn)
    assert tkn == TK_NSA
    pos2d = pos_f.reshape(s // tkn, 1, tkn)
    pos_sorted = jnp.all(pos_f[1:] >= pos_f[:-1])

    g64 = _group_matrix([(64, 1)] * 4, 256)
    gm = _group_matrix([(64, 1), (32, 1), (32, 0)] * 2, 256)
    tks = min(TK_SB, min(TQ_SB, s))
    tri = jnp.asarray(np.tril(np.ones((tks, tks), np.float32)), BF16)

    c8 = jnp.pad(c, ((0, 8 - b), (0, 0)))
    mod_all = _modulation(c8, ada_w, ada_b)

    for layer in range(depth):
        j = layer // 2
        mod3 = mod_all[layer, :b].reshape(b, 1, 3 * d)
        if layer % 2 == 0:
            w, wq, wqs, wkv, vec = _pack_even(ev_w_in[j], norm_g[layer], sb_qn[j], sb_kn[j], mla_qa_g[j],
                                              mla_wq_up[j], mla_kva_g[j], mla_wkv_up[j], mla_qn[j], mla_kn[j])
            sbq, sbk, sbv, gz, mq, mk, mv = _k1_even(x, mod3, vec, w, wq, wqs, wkv, g64, gm, cos128, sin128)
            sb_bound = (_normed_len(sb_qn[j], (HEAD_DIM,)) * _normed_len(sb_kn[j], (HEAD_DIM,))
                        * (HEAD_DIM ** -0.5 * LOG2E * ROUNDING_MARGIN))
            o_sb = _sb_attention(sbq, sbk, sbv, tri, sb_bound <= SAFE_LOGIT_BOUND)
            o_mla = _mla_attention(mq, mk, mv, _mla_logit_bound(mla_qn[j], mla_kn[j]))
            x = _out_even(x, mod3, o_sb, o_mla, gz, ev_w_out[j].astype(BF16))
        else:
            w_nsa, w_dil, vec = _pack_odd(od_w_in[j], norm_g[layer], nsa_qn[j], nsa_kn[j], dil_qn[j], dil_kn[j])
            q, ck, cv, sk, sv, wk, wv, gates, gz = _k1_nsa(x, mod3, vec, w_nsa, g64)
            dqkv = _k1_dil(x, mod3, vec, w_dil, g64)
            wck, pek = _pack_compress(nsa_cmp_wk[j], nsa_cmp_pe_k[j], False)
            wcv, pev = _pack_compress(nsa_cmp_wv[j], nsa_cmp_pe_v[j], True)
            kgain = jnp.concatenate([nsa_kn[j], jnp.zeros((64,), F32)]).reshape(1, 128)
            kc, vc = _compress(ck, cv, wck, wcv, pek, pev, kgain, g64[0:128, 0:128])
            o_c, selb, used = _nsa_cmp(nsa_slopes, q, kc, vc, ovlt, cpos, pos_rows, gates)
            flags = used[:, :, :, 0, :].reshape(b, NSA_KV_HEADS, s // tqn, tqn // min(TQ_CMP, s),
                                                SEL_TILE_STRIDE, LANES // SEL_TILE_STRIDE).max(axis=(3, 5))
            flags = (flags > 0).astype(jnp.int32).reshape(-1)
            nsa_bound = (_normed_len(nsa_qn[j], (HEAD_DIM,)) * _normed_len(nsa_kn[j], (HEAD_DIM,))
                         * (HEAD_DIM ** -0.5 * LOG2E * ROUNDING_MARGIN)).reshape(1).astype(F32)
            safe = (nsa_bound[0] <= SAFE_LOGIT_BOUND) & pos_sorted
            o_s = _nsa_gqa(flags, nsa_slopes, nsa_bound, safe, q, selb, sk, sv, pos2d, pos_col, gates, 1)
            o_w = _nsa_gqa(flags, nsa_slopes, nsa_bound, safe, q, selb, wk, wv, pos2d, pos_col, gates, 2)
            dils, lses = [], []
            for g in range(N_DIL):
                o, lse = _dilated(dil_slopes, dqkv[3 * g], dqkv[3 * g + 1], dqkv[3 * g + 2], pos_f, g)
                dils.append(o)
                lses.append(lse)
            x = _out_odd(x, mod3, o_c, o_s, o_w, dils, lses, gz, od_w_out[j].astype(BF16))
    return x
```

```python
import functools

import numpy as np
import jax
import jax.numpy as jnp
from jax import lax
from jax.experimental import pallas as pl
from jax.experimental.pallas import tpu as pltpu

F32 = jnp.float32
BF16 = jnp.bfloat16

D_MODEL = 1024
HEAD_DIM = 64
NORM_EPS = 1e-6
TINY = 1e-30
SB_HEADS = 8
MLA_HEADS = 8
MLA_Q_RANK = 256
MLA_KV_RANK = 128
MLA_NOPE = 64
MLA_ROPE = 32
MLA_V = 64
ROPE_BASE = 10000.0
NSA_HEADS = 12
NSA_KV_HEADS = 3
NSA_HPG = 4
CMP_LEN = 32
CMP_STRIDE = 16
SEL_LEN = 64
SEL_TOPN = 16
WIN = 512
FORCE_BONUS = 1e3
DIL_CFG = ((128, 1), (512, 4), (2048, 16))
N_DIL = 3
DIL_HEADS = 4
SB_W = SB_HEADS * HEAD_DIM
MLA_OUT = MLA_HEADS * MLA_V
NSA_W = NSA_HEADS * HEAD_DIM
NSA_KV_W = NSA_KV_HEADS * HEAD_DIM
DIL_W = DIL_HEADS * HEAD_DIM

LANES = 128
MXU_DIM = 256
NEG_MASK = -1e30
LOG2E = 1.4426950408889634
LN2 = 0.6931471805599453
SAFE_LOGIT_BOUND = 50.0
ROUNDING_MARGIN = 1.02
SEL_OFF = 2.0 ** 30
VMEM_LIMIT = 56 * 1024 * 1024

TS_PROJ = 512
TQ_SB = 1024
TK_SB = 256
TQ_MLA = 1024
TD_MLA = 512
TQ_CMP = 512
TQ_NSA = 512
TK_NSA = 512
SEL_TILE_STRIDE = LANES * SEL_LEN // TK_NSA
T_DIL = 128


def _dot(a, b):
    return jnp.dot(a, b, preferred_element_type=F32)


def _dot_nt(a, b):
    return lax.dot_general(a, b, (((1,), (1,)), ((), ())), preferred_element_type=F32)


def _split_hl(a):
    hi = a.astype(BF16)
    lo = (a - hi.astype(F32)).astype(BF16)
    return hi, lo


def _dot_hl(a, b):
    hi, lo = _split_hl(a)
    return _dot(hi, b) + _dot(lo, b)


def _sigmoid(z):
    return 1.0 / (1.0 + jnp.exp(-z))


def _cparams(n_axes):
    return pltpu.CompilerParams(dimension_semantics=("arbitrary",) * n_axes,
                                vmem_limit_bytes=VMEM_LIMIT)


def _full(shape):
    n = len(shape)
    return pl.BlockSpec(shape, lambda *a, _n=n: (0,) * _n)


def _mod_kernel(c_ref, w_ref, b_ref, o_ref):
    c = c_ref[...]
    a = c * _sigmoid(c)
    ah, al = _split_hl(a)
    wh, wl = _split_hl(w_ref[0])
    o_ref[0] = _dot(ah, wh) + _dot(ah, wl) + _dot(al, wh) + b_ref[0]


def _modulation(c8, ada_w, ada_b):
    depth, d, n3 = ada_w.shape
    tn = 1024
    return pl.pallas_call(
        _mod_kernel,
        grid=(depth, n3 // tn),
        in_specs=[pl.BlockSpec((8, d), lambda l, j: (0, 0)),
                  pl.BlockSpec((1, d, tn), lambda l, j: (l, 0, j)),
                  pl.BlockSpec((1, 1, tn), lambda l, j: (l, 0, j))],
        out_specs=pl.BlockSpec((1, 8, tn), lambda l, j: (l, 0, j)),
        out_shape=jax.ShapeDtypeStruct((depth, 8, n3), F32),
        compiler_params=_cparams(2),
        name="adaln_mod",
    )(c8, ada_w, ada_b.reshape(depth, 1, n3))


def _modulated(x_ref, mod_ref, ng):
    x = x_ref[0]
    mod = mod_ref[0]
    shift = mod[:, 0:D_MODEL]
    scale = mod[:, D_MODEL:2 * D_MODEL]
    ms = jnp.mean(x * x, axis=-1, keepdims=True)
    h = x * lax.rsqrt(ms + NORM_EPS) * ng
    h = h * (1.0 + scale) + shift
    return h.astype(BF16)


def _group_norm_chunk(t, g, inv_cnt):
    ss = _dot_hl(t * t, g)
    return t * lax.rsqrt(ss * inv_cnt + NORM_EPS)


def _row_rms(t, gain):
    ms = jnp.mean(t * t, axis=-1, keepdims=True)
    return t * lax.rsqrt(ms + NORM_EPS) * gain


EV_COLS = 3200


def _k1_even_kernel(x_ref, mod_ref, vec_ref, w_ref, wq_ref, wqs_ref, wkv_ref, g64_ref, gm_ref,
                    cos_ref, sin_ref, sbq_o, sbk_o, sbv_o, gz_o, mq_o, mk_o, mv_o):
    hb = _modulated(x_ref, mod_ref, vec_ref[0:1, :])
    g64 = g64_ref[...]
    gm = gm_ref[...]
    inv64 = 1.0 / HEAD_DIM

    u = _dot(hb, w_ref[:, 0:1536])
    for c in range(2):
        sl = slice(c * 256, (c + 1) * 256)
        qn = _group_norm_chunk(u[:, c * 256:(c + 1) * 256], g64, inv64)
        sbq_o[0, :, sl] = (qn * vec_ref[1:2, sl]).astype(BF16)
        kn = _group_norm_chunk(u[:, 512 + c * 256:512 + (c + 1) * 256], g64, inv64)
        sbk_o[0, :, sl] = (kn * vec_ref[2:3, sl]).astype(BF16)
    sbv_o[0] = u[:, 1024:1536].astype(BF16)

    uz = _dot(hb, w_ref[:, 1536:2560])
    gz_o[0] = (uz * _sigmoid(uz)).astype(BF16)

    ul = _dot(hb, w_ref[:, 2560:3200])
    qlat = _row_rms(ul[:, 0:256], vec_ref[3:4, 0:256]).astype(BF16)
    kvlat = _row_rms(ul[:, 256:384], vec_ref[4:5, 0:128]).astype(BF16)
    krm = ul[:, 384:512]
    krs = ul[:, 512:640]
    cs = cos_ref[...]
    sn = sin_ref[...]
    cs2 = jnp.concatenate([cs, cs], axis=1)
    sn2 = jnp.concatenate([sn, sn], axis=1)

    tq = _dot(qlat, wq_ref[...])
    tqs = _dot(qlat, wqs_ref[...])
    for p in range(4):
        sl = slice(p * 256, (p + 1) * 256)
        tc = tq[:, p * 256:(p + 1) * 256]
        ss = _dot_hl(tc * tc, gm)
        inv = lax.rsqrt(ss * vec_ref[10:11, sl] + NORM_EPS)
        a = tc * inv * vec_ref[5:6, sl]
        b = tqs[:, p * 256:(p + 1) * 256] * inv * vec_ref[6:7, sl]
        mq_o[0, :, sl] = (a * cs2 + b * sn2).astype(BF16)

    sskr = _dot_hl(krm * krm, gm[0:128, 0:128])
    invr = lax.rsqrt(sskr * vec_ref[10:11, 0:128] + NORM_EPS)
    kr = (krm * invr * vec_ref[8:9, 0:128]) * cs + (krs * invr * vec_ref[9:10, 0:128]) * sn
    kr2 = jnp.concatenate([kr, kr], axis=1)

    kv = _dot(kvlat, wkv_ref[...])
    for p in range(4):
        sl = slice(p * 256, (p + 1) * 256)
        kn = _group_norm_chunk(kv[:, p * 256:(p + 1) * 256], g64, inv64)
        mk_o[0, :, sl] = (kn * vec_ref[7:8, sl] + kr2).astype(BF16)
    mv_o[0] = kv[:, 1024:1536].astype(BF16)


def _k1_even(x, mod3, vec, w, wq, wqs, wkv, g64, gm, cos128, sin128):
    b, s, d = x.shape
    ts = min(TS_PROJ, s)
    row = lambda n: pl.BlockSpec((1, ts, n), lambda i, j: (i, j, 0))
    outs = [(512, BF16), (512, BF16), (512, BF16), (1024, BF16), (1024, BF16), (1024, BF16), (512, BF16)]
    return pl.pallas_call(
        _k1_even_kernel,
        grid=(b, s // ts),
        in_specs=[row(d),
                  pl.BlockSpec((1, 1, 3 * d), lambda i, j: (i, 0, 0)),
                  _full(vec.shape), _full(w.shape), _full(wq.shape), _full(wqs.shape), _full(wkv.shape),
                  _full(g64.shape), _full(gm.shape),
                  pl.BlockSpec((ts, 128), lambda i, j: (j, 0)),
                  pl.BlockSpec((ts, 128), lambda i, j: (j, 0))],
        out_specs=[row(n) for n, _ in outs],
        out_shape=[jax.ShapeDtypeStruct((b, s, n), dt) for n, dt in outs],
        compiler_params=_cparams(2),
        name="inproj_even",
    )(x, mod3, vec, w, wq, wqs, wkv, g64, gm, cos128, sin128)


def _sb_kernel(q_ref, k_ref, v_ref, tri_ref, o_ref, acc_ref, car_ref, *, tq, tk, bounded):
    qi = pl.program_id(2)
    q = q_ref[0]
    lane = lax.broadcasted_iota(jnp.int32, (1, LANES), 1)
    lo_half = lane < HEAD_DIM
    zero = jnp.zeros_like(q)
    qs = (jnp.where(lo_half, q, zero), jnp.where(lo_half, zero, q))
    tri = tri_ref[...]
    acc_ref[...] = jnp.zeros_like(acc_ref)
    car_ref[...] = jnp.zeros_like(car_ref)
    col0 = lax.broadcasted_iota(jnp.int32, (1, tk), 1)

    def tile(kt, r0, masked):
        n = tq - r0
        off = pl.multiple_of(kt * tk, tk)
        k = k_ref[0, pl.ds(off, tk), :]
        v = v_ref[0, pl.ds(off, tk), :]
        if masked:
            rows = qi * tq + r0 + lax.broadcasted_iota(jnp.int32, (n, 1), 0)
            strict = (kt * tk + col0) < rows
        for i in range(2):
            z = _dot_nt(qs[i][r0:tq], k)
            if bounded:
                lom = jnp.log(1.0 + jnp.exp2(z)) * (-LOG2E)
            else:
                lom = -jnp.maximum(z, 0.0) - jnp.log(1.0 + jnp.exp2(-jnp.abs(z))) * LOG2E
            if masked:
                lom = jnp.where(strict, lom, 0.0)
            tt = _dot(lom.astype(BF16), tri)
            car = car_ref[i, r0:tq]
            w = jnp.exp2(tt + z + jnp.concatenate([car] * (tk // LANES), axis=1))
            if masked:
                w = jnp.where(strict, w, 0.0)
            acc_ref[i, r0:tq] += _dot(w.astype(BF16), v)
            car_ref[i, r0:tq] = car + jnp.broadcast_to(tt[:, 0:1], (n, LANES))

    per = tq // tk
    for c in reversed(range(per)):
        tile(qi * per + c, c * tk, True)

    def body(j, carry):
        for u in range(per):
            tile(qi * per - 1 - per * j - u, 0, False)
        return carry

    lax.fori_loop(0, qi, body, 0)
    o_ref[0] = jnp.where(lo_half, acc_ref[0], acc_ref[1])


def _sb_attention(q, k, v, tri, safe):
    b, s, w = q.shape
    tq = min(TQ_SB, s)
    tk = min(TK_SB, tq)

    def call(bounded):
        kern = functools.partial(_sb_kernel, tq=tq, tk=tk, bounded=bounded)
        return pl.pallas_call(
            kern,
            grid=(b, w // LANES, s // tq),
            in_specs=[pl.BlockSpec((1, tq, LANES), lambda i, p, j: (i, j, p)),
                      pl.BlockSpec((1, s, LANES), lambda i, p, j: (i, 0, p)),
                      pl.BlockSpec((1, s, LANES), lambda i, p, j: (i, 0, p)),
                      _full(tri.shape)],
            out_specs=pl.BlockSpec((1, tq, LANES), lambda i, p, j: (i, j, p)),
            out_shape=jax.ShapeDtypeStruct((b, s, w), F32),
            scratch_shapes=[pltpu.VMEM((2, tq, LANES), F32), pltpu.VMEM((2, tq, LANES), F32)],
            compiler_params=_cparams(3),
            name="stickbreak_attn_bounded" if bounded else "stickbreak_attn",
        )(q, k, v, tri)

    return lax.cond(safe, lambda: call(True), lambda: call(False))


def _lane_max(s):
    m = s[:, 0:LANES]
    for c in range(1, s.shape[1] // LANES):
        m = jnp.maximum(m, s[:, c * LANES:(c + 1) * LANES])
    return m


def _mla_kernel(bnd_ref, q_ref, k_ref, v_ref, o_ref, acc_ref, m_ref, *, tq, td, bounded):
    qi = pl.program_id(2)
    q = q_ref[0]
    qs = (q[:, 0:LANES], q[:, LANES:2 * LANES])
    lane = lax.broadcasted_iota(jnp.int32, (1, LANES), 1)
    lo_half = lane < HEAD_DIM
    acc_ref[...] = jnp.zeros_like(acc_ref)
    if bounded:
        m_ref[...] = jnp.full(m_ref.shape, bnd_ref[0], F32)
    else:
        m_ref[...] = jnp.full_like(m_ref, NEG_MASK)

    def scores(off, width, r0, masked):
        k = k_ref[0, pl.ds(off, width), :]
        if masked:
            rows = qi * tq + r0 + lax.broadcasted_iota(jnp.int32, (tq - r0, 1), 0)
            keep = (off + lax.broadcasted_iota(jnp.int32, (1, width), 1)) <= rows
        out = []
        for i in range(2):
            s = _dot_nt(qs[i][r0:tq], k[:, i * LANES:(i + 1) * LANES])
            if masked:
                s = jnp.where(keep, s, NEG_MASK)
            out.append(s)
        return out

    def sweep(fn):
        def body(j, carry):
            fn(pl.multiple_of(j * tq, tq), tq, 0, False)
            return carry
        lax.fori_loop(0, qi, body, 0)
        for c in range(tq // td):
            fn(pl.multiple_of(qi * tq + c * td, td), td, c * td, True)

    def row_max(off, width, r0, masked):
        for i, s in enumerate(scores(off, width, r0, masked)):
            m_ref[i, r0:tq] = jnp.maximum(m_ref[i, r0:tq], _lane_max(s))

    def accumulate(off, width, r0, masked):
        vx = jnp.concatenate([v_ref[0, pl.ds(off, width), :], jnp.ones((width, LANES), BF16)], axis=1)
        for i, s in enumerate(scores(off, width, r0, masked)):
            p = jnp.exp2(s - jnp.concatenate([m_ref[i, r0:tq]] * (width // LANES), axis=1))
            acc_ref[i, r0:tq] += _dot(p.astype(BF16), vx)

    if not bounded:
        sweep(row_max)
        for i in range(2):
            m_ref[i] = jnp.broadcast_to(jnp.max(m_ref[i], axis=-1, keepdims=True), (tq, LANES))
    sweep(accumulate)
    o_ref[0] = jnp.where(lo_half, acc_ref[0, :, 0:LANES] / acc_ref[0, :, LANES:2 * LANES],
                         acc_ref[1, :, 0:LANES] / acc_ref[1, :, LANES:2 * LANES])


def _mla_attention(q, k, v, bound):
    b, s, _ = q.shape
    tq = min(TQ_MLA, s)
    td = min(TD_MLA, tq)

    def call(bounded):
        kern = functools.partial(_mla_kernel, tq=tq, td=td, bounded=bounded)
        return pl.pallas_call(
            kern,
            grid=(b, MLA_HEADS // 2, s // tq),
            in_specs=[pl.BlockSpec(memory_space=pltpu.SMEM),
                      pl.BlockSpec((1, tq, 2 * LANES), lambda i, p, j: (i, j, p)),
                      pl.BlockSpec((1, s, 2 * LANES), lambda i, p, j: (i, 0, p)),
                      pl.BlockSpec((1, s, LANES), lambda i, p, j: (i, 0, p))],
            out_specs=pl.BlockSpec((1, tq, LANES), lambda i, p, j: (i, j, p)),
            out_shape=jax.ShapeDtypeStruct((b, s, MLA_OUT), F32),
            scratch_shapes=[pltpu.VMEM((2, tq, 2 * LANES), F32), pltpu.VMEM((2, tq, LANES), F32)],
            compiler_params=_cparams(3),
            name="mla_attn_bounded" if bounded else "mla_attn",
        )(bound, q, k, v)

    return lax.cond(bound[0] <= SAFE_LOGIT_BOUND, lambda: call(True), lambda: call(False))


def _out_even_kernel(x_ref, mod_ref, osb_ref, omla_ref, gz_ref, w_ref, o_ref):
    gz = gz_ref[0].astype(F32)
    m1 = (osb_ref[0] * gz[:, 0:SB_W]).astype(BF16)
    m2 = (omla_ref[0] * gz[:, SB_W:SB_W + MLA_OUT]).astype(BF16)
    y = _dot(m1, w_ref[0:SB_W, :]) + _dot(m2, w_ref[SB_W:SB_W + MLA_OUT, :])
    gate = mod_ref[0][:, 2 * D_MODEL:3 * D_MODEL]
    o_ref[0] = x_ref[0] + gate * y


def _out_even(x, mod3, osb, omla, gz, w):
    b, s, d = x.shape
    ts = min(TS_PROJ, s)
    row = lambda n: pl.BlockSpec((1, ts, n), lambda i, j: (i, j, 0))
    return pl.pallas_call(
        _out_even_kernel,
        grid=(b, s // ts),
        in_specs=[row(d), pl.BlockSpec((1, 1, 3 * d), lambda i, j: (i, 0, 0)),
                  row(SB_W), row(MLA_OUT), row(SB_W + MLA_OUT), _full(w.shape)],
        out_specs=row(d),
        out_shape=jax.ShapeDtypeStruct((b, s, d), F32),
        compiler_params=_cparams(2),
        name="outproj_even",
    )(x, mod3, osb, omla, gz, w)


NSA_COLS = 5248


def _k1_nsa_kernel(x_ref, mod_ref, vec_ref, w_ref, g64_ref,
                   q_o, ck_o, cv_o, sk_o, sv_o, wk_o, wv_o, gt_o, gz_o, chunk_ref):
    ts = x_ref.shape[1]
    hb = _modulated(x_ref, mod_ref, vec_ref[0:1, 0:D_MODEL])
    g64 = g64_ref[...]
    inv64 = 1.0 / HEAD_DIM

    uq = _dot(hb, w_ref[:, 0:1536])
    for c in range(6):
        sl = slice(c * 256, (c + 1) * 256)
        qn = _group_norm_chunk(uq[:, c * 256:(c + 1) * 256], g64, inv64)
        q_o[0, :, sl] = (qn * vec_ref[1:2, sl]).astype(BF16)

    uc = _dot(hb, w_ref[:, 1536:2304])
    for c in range(2 * NSA_KV_HEADS):
        chunk_ref[c] = uc[:, c * 128:(c + 1) * 128]
    for g in range(NSA_KV_HEADS):
        for tok in range(CMP_STRIDE):
            rows = pl.ds(tok, ts // CMP_STRIDE, stride=CMP_STRIDE)
            sl = slice(tok * 128, (tok + 1) * 128)
            ck_o[0, g, :, sl] = chunk_ref[g, rows, :].astype(BF16)
            cv_o[0, g, :, sl] = chunk_ref[NSA_KV_HEADS + g, rows, :].astype(BF16)

    us = _dot(hb, w_ref[:, 2304:3840])
    kgain = vec_ref[2:3, 0:384]
    g128 = g64[0:128, 0:128]
    for g in range(NSA_KV_HEADS):
        sl = slice(g * 128, (g + 1) * 128)
        t = us[:, g * 128:(g + 1) * 128]
        ss = _dot_hl(t * t, g128)
        sk_o[0, :, sl] = (t * lax.rsqrt(ss * inv64 + NORM_EPS) * kgain[:, sl]).astype(BF16)
        t = us[:, 768 + g * 128:768 + (g + 1) * 128]
        ss = _dot_hl(t * t, g128)
        wk_o[0, :, sl] = (t * lax.rsqrt(ss * inv64 + NORM_EPS) * kgain[:, sl]).astype(BF16)
    ones_hi = (lax.broadcasted_iota(jnp.int32, (1, 384), 1) % 128) >= HEAD_DIM
    sv_o[0] = jnp.where(ones_hi, 1.0, us[:, 384:768]).astype(BF16)
    wv_o[0] = jnp.where(ones_hi, 1.0, us[:, 1152:1536]).astype(BF16)

    ug = _dot(hb, w_ref[:, 3840:4224])
    gt_o[0] = _sigmoid(ug)
    uz = _dot(hb, w_ref[:, 4224:5248])
    gz_o[0] = (uz * _sigmoid(uz)).astype(BF16)


def _k1_nsa(x, mod3, vec, w, g64):
    b, s, d = x.shape
    ts = min(TS_PROJ, s)
    row = lambda n: pl.BlockSpec((1, ts, n), lambda i, j: (i, j, 0))
    cw = CMP_STRIDE * 128
    grp = pl.BlockSpec((1, NSA_KV_HEADS, ts // CMP_STRIDE, cw), lambda i, j: (i, 0, j, 0))
    return pl.pallas_call(
        _k1_nsa_kernel,
        grid=(b, s // ts),
        in_specs=[row(d), pl.BlockSpec((1, 1, 3 * d), lambda i, j: (i, 0, 0)),
                  _full(vec.shape), _full(w.shape), _full(g64.shape)],
        out_specs=[row(1536), grp, grp, row(384), row(384), row(384), row(384), row(384), row(1024)],
        out_shape=[jax.ShapeDtypeStruct((b, s, 1536), BF16),
                   jax.ShapeDtypeStruct((b, NSA_KV_HEADS, s // CMP_STRIDE, cw), BF16),
                   jax.ShapeDtypeStruct((b, NSA_KV_HEADS, s // CMP_STRIDE, cw), BF16),
                   jax.ShapeDtypeStruct((b, s, 384), BF16),
                   jax.ShapeDtypeStruct((b, s, 384), BF16),
                   jax.ShapeDtypeStruct((b, s, 384), BF16),
                   jax.ShapeDtypeStruct((b, s, 384), BF16),
                   jax.ShapeDtypeStruct((b, s, 384), F32),
                   jax.ShapeDtypeStruct((b, s, 1024), BF16)],
        scratch_shapes=[pltpu.VMEM((2 * NSA_KV_HEADS, ts, 128), F32)],
        compiler_params=_cparams(2),
        name="inproj_nsa",
    )(x, mod3, vec, w, g64)


def _k1_dil_kernel(x_ref, mod_ref, vec_ref, w_ref, g64_ref, *refs):
    outs, st = refs[:3 * N_DIL], refs[3 * N_DIL]
    ts = x_ref.shape[1]
    hb = _modulated(x_ref, mod_ref, vec_ref[0:1, 0:D_MODEL])
    g64 = g64_ref[...]
    inv64 = 1.0 / HEAD_DIM
    u = _dot(hb, w_ref[...])
    for g in range(N_DIL):
        dil = DIL_CFG[g][1]
        qn = _group_norm_chunk(u[:, g * 256:(g + 1) * 256], g64, inv64) * vec_ref[3:4, 0:256]
        kn = _group_norm_chunk(u[:, 768 + g * 256:768 + (g + 1) * 256], g64, inv64) * vec_ref[4:5, 0:256]
        vals = (qn, kn, u[:, 1536 + g * 256:1536 + (g + 1) * 256])
        for j, val in enumerate(vals):
            o_ref = outs[3 * g + j]
            if dil == 1:
                o_ref[0] = val.astype(BF16)
                continue
            for h in range(2):
                st[j, h] = val[:, h * 128:(h + 1) * 128]
            for r in range(dil):
                for h in range(2):
                    sl = slice(r * 256 + h * 128, r * 256 + (h + 1) * 128)
                    o_ref[0, :, sl] = st[j, h, pl.ds(r, ts // dil, stride=dil), :].astype(BF16)


def _k1_dil(x, mod3, vec, w, g64):
    b, s, d = x.shape
    ts = min(TS_PROJ, s)
    row = lambda n: pl.BlockSpec((1, ts, n), lambda i, j: (i, j, 0))
    specs, shapes = [], []
    for g in range(N_DIL):
        dil = DIL_CFG[g][1]
        for _ in range(3):
            specs.append(pl.BlockSpec((1, ts // dil, dil * 256), lambda i, j: (i, j, 0)))
            shapes.append(jax.ShapeDtypeStruct((b, s // dil, dil * 256), BF16))
    return pl.pallas_call(
        _k1_dil_kernel,
        grid=(b, s // ts),
        in_specs=[row(d), pl.BlockSpec((1, 1, 3 * d), lambda i, j: (i, 0, 0)),
                  _full(vec.shape), _full(w.shape), _full(g64.shape)],
        out_specs=specs,
        out_shape=shapes,
        scratch_shapes=[pltpu.VMEM((3, 2, ts, 128), F32)],
        compiler_params=_cparams(2),
        name="inproj_dil",
    )(x, mod3, vec, w, g64)


def _compress_kernel(xk_ref, xv_ref, wk_ref, wv_ref, pek_ref, pev_ref, gain_ref, g128_ref, kc_o, vc_o):
    nch = xk_ref.shape[2]

    def comp(x_ref, w_ref, pe_ref):
        x = x_ref[0, 0]
        lo = _dot(x, w_ref[0])
        hi = _dot(x, w_ref[1])
        pec = _dot_hl(pe_ref[0], w_ref[0]) + _dot_hl(pe_ref[1], w_ref[1])
        return lo + pltpu.roll(hi, nch - 1, 0) + pec[0:1, :]

    kc = comp(xk_ref, wk_ref, pek_ref)
    ss = _dot_hl(kc * kc, g128_ref[...])
    kc_o[0, 0] = (kc * lax.rsqrt(ss * (1.0 / HEAD_DIM) + NORM_EPS) * gain_ref[...]).astype(BF16)
    vc_o[0, 0] = comp(xv_ref, wv_ref, pev_ref).T.astype(BF16)


def _compress(ck, cv, wk, wv, pek, pev, gain, g128):
    b, g, nch, _ = ck.shape
    xk, xv = ck, cv
    blk = pl.BlockSpec((1, 1, nch, CMP_STRIDE * 128), lambda i, j: (i, j, 0, 0))
    oblk = pl.BlockSpec((1, 1, nch, 128), lambda i, j: (i, j, 0, 0))
    tblk = pl.BlockSpec((1, 1, 128, nch), lambda i, j: (i, j, 0, 0))
    return pl.pallas_call(
        _compress_kernel,
        grid=(b, g),
        in_specs=[blk, blk, _full(wk.shape), _full(wv.shape), _full(pek.shape), _full(pev.shape),
                  _full(gain.shape), _full(g128.shape)],
        out_specs=[oblk, tblk],
        out_shape=[jax.ShapeDtypeStruct((b, g, nch, 128), BF16), jax.ShapeDtypeStruct((b, g, 128, nch), BF16)],
        compiler_params=_cparams(2),
        name="nsa_compress",
    )(xk, xv, wk, wv, pek, pev, gain, g128)


def _pair_select(lo_half, a, b):
    return jnp.where(lo_half, a, b)


def _nsa_cmp_kernel(slope_ref, q_ref, kc_ref, vct_ref, ovlt_ref, cposc_ref, prow_ref, gt_ref,
                    oc_o, selb_o, any_o, *, tq):
    g = pl.program_id(1)
    qi = pl.program_id(2)
    q4 = q_ref[0]
    kc = kc_ref[0, 0]
    vct = vct_ref[0, 0]
    nch = kc.shape[0]
    t = qi * tq + lax.broadcasted_iota(jnp.int32, (1, tq), 1)
    n_id = lax.broadcasted_iota(jnp.int32, (nch, 1), 0)
    valid = (n_id * CMP_STRIDE + (CMP_LEN - 1)) <= t
    dist = (prow_ref[qi] - cposc_ref[...]) * LOG2E
    gtt = gt_ref[0].T
    row_lo = lax.broadcasted_iota(jnp.int32, (LANES, 1), 0) < HEAD_DIM
    psum = jnp.zeros((nch, tq), F32)
    outs = []
    for hh in range(NSA_HPG):
        s = _dot_nt(kc, q4[:, hh * LANES:(hh + 1) * LANES]) - slope_ref[g * NSA_HPG + hh] * dist
        s = jnp.where(valid, s, -jnp.inf)
        mx = jnp.max(s, axis=0, keepdims=True)
        mx = jnp.where(mx == -jnp.inf, 0.0, mx)
        e = jnp.exp2(s - mx)
        den = jnp.maximum(jnp.sum(e, axis=0, keepdims=True), TINY)
        p = e * (1.0 / den)
        psum = psum + p
        outs.append(_dot(vct, p.astype(BF16)) * gtt[3 * hh:3 * hh + 1, :])
    oc_o[0, :, 0:LANES] = jnp.where(row_lo, outs[0], outs[1]).T
    oc_o[0, :, LANES:2 * LANES] = jnp.where(row_lo, outs[2], outs[3]).T

    hi, lo = _split_hl(psum)
    imp = _dot(ovlt_ref[...], hi) + _dot(ovlt_ref[...], lo)
    blk = lax.broadcasted_iota(jnp.int32, (LANES, 1), 0)
    cur = t >> 6
    forced = (blk == 0) | (blk == cur) | (blk == cur - 1)
    allowed = blk <= cur
    score = jnp.where(allowed, imp + jnp.where(forced, FORCE_BONUS, 0.0), -jnp.inf)
    blk_f = blk.astype(F32)

    def pick(_, carry):
        sc, sel = carry
        mx = jnp.max(sc, axis=0, keepdims=True)
        idx = jnp.min(jnp.where(sc == mx, blk_f, float(LANES)), axis=0, keepdims=True)
        hit = blk_f == idx
        return jnp.where(hit, -jnp.inf, sc), jnp.where(hit, 1.0, sel)

    _, sel = lax.fori_loop(0, SEL_TOPN, pick, (score, jnp.zeros((LANES, tq), F32)))
    chosen = jnp.where((sel > 0.5) & allowed, 1.0, 0.0).T
    selb_o[0, 0] = ((chosen - 1.0) * SEL_OFF).astype(BF16)
    used = jnp.max(chosen, axis=0, keepdims=True)
    any_o[0, 0, 0] = jnp.broadcast_to(used, (8, LANES))


def _nsa_cmp(slopes, q, kc, vct, ovlt, cposc, pos_rows, gates):
    b, s, _ = q.shape
    nch = kc.shape[2]
    tq = pos_rows.shape[2]
    kern = functools.partial(_nsa_cmp_kernel, tq=tq)
    return pl.pallas_call(
        kern,
        grid=(b, NSA_KV_HEADS, s // tq),
        in_specs=[pl.BlockSpec(memory_space=pltpu.SMEM),
                  pl.BlockSpec((1, tq, 4 * LANES), lambda i, g, j: (i, j, g)),
                  pl.BlockSpec((1, 1, nch, LANES), lambda i, g, j: (i, g, 0, 0)),
                  pl.BlockSpec((1, 1, LANES, nch), lambda i, g, j: (i, g, 0, 0)),
                  _full(ovlt.shape), _full(cposc.shape), _full(pos_rows.shape),
                  pl.BlockSpec((1, tq, LANES), lambda i, g, j: (i, j, g))],
        out_specs=[pl.BlockSpec((1, tq, 2 * LANES), lambda i, g, j: (i, j, g)),
                   pl.BlockSpec((1, 1, tq, LANES), lambda i, g, j: (i, g, j, 0)),
                   pl.BlockSpec((1, 1, 1, 8, LANES), lambda i, g, j: (i, g, j, 0, 0))],
        out_shape=[jax.ShapeDtypeStruct((b, s, NSA_W), F32),
                   jax.ShapeDtypeStruct((b, NSA_KV_HEADS, s, LANES), BF16),
                   jax.ShapeDtypeStruct((b, NSA_KV_HEADS, s // tq, 8, LANES), F32)],
        compiler_params=_cparams(3),
        name="nsa_cmp_topk",
    )(slopes, q, kc, vct, ovlt, cposc, pos_rows, gates)


def _gqa_kernel(flag_ref, slope_ref, bnd_ref, q_ref, selb_ref, k_ref, v_ref, pos_ref, pcol_ref, gt_ref, o_ref,
                qa_ref, acc_ref, m_ref, *, tq, tk, branch, bounded):
    bi = pl.program_id(0)
    g = pl.program_id(1)
    qi = pl.program_id(2)
    q4 = q_ref[0]
    lane = lax.broadcasted_iota(jnp.int32, (1, LANES), 1)
    lo_half = lane < HEAD_DIM
    pref = pos_ref[qi * (tq // tk)][:, 0:1]
    for hh in range(NSA_HPG):
        sl = slice(hh * tq, (hh + 1) * tq)
        qa_ref[sl, 0:LANES] = q4[:, hh * LANES:(hh + 1) * LANES]
        if branch == 1:
            qa_ref[sl, LANES:2 * LANES] = selb_ref[0, 0]
        if bounded:
            own = (pcol_ref[...] - pref) * (LOG2E * slope_ref[g * NSA_HPG + hh]) + bnd_ref[0]
            m_ref[sl] = jnp.broadcast_to(own, (tq, LANES))
    acc_ref[...] = jnp.zeros_like(acc_ref)
    if not bounded:
        m_ref[...] = jnp.full_like(m_ref, NEG_MASK)
    rows = qi * tq + lax.broadcasted_iota(jnp.int32, (tq, 1), 0)
    col0 = lax.broadcasted_iota(jnp.int32, (1, tk), 1)
    krow = lax.broadcasted_iota(jnp.int32, (tk, 1), 0)
    per = tq // tk
    nrep = tk // LANES
    flag0 = ((bi * NSA_KV_HEADS + g) * pl.num_programs(2) + qi) * SEL_TILE_STRIDE

    def scores(kt_true, masked):
        kt = jnp.maximum(kt_true, 0)
        off = pl.multiple_of(kt * tk, tk)
        k = k_ref[0, pl.ds(off, tk), :]
        if branch == 1:
            blk = (kt * tk + krow) >> 6
            onehot = jnp.where(lane == blk, 1.0, 0.0).astype(BF16)
            s_all = _dot_nt(qa_ref[...], jnp.concatenate([k, onehot], axis=1))
        else:
            s_all = _dot_nt(qa_ref[:, 0:LANES], k)
        rel = (pos_ref[kt] - pref) * LOG2E
        if masked:
            cols = kt_true * tk + col0
            d = rows - cols
            keep = d >= 0
            if branch == 2:
                keep = keep & (d < WIN) & (cols >= 0)
        out = []
        for hh in range(NSA_HPG):
            s = s_all[hh * tq:(hh + 1) * tq] + slope_ref[g * NSA_HPG + hh] * rel
            if masked:
                s = jnp.where(keep, s, NEG_MASK)
            out.append(s)
        return out

    def sweep(fn):
        if branch == 1:
            def body(j, carry):
                @pl.when(flag_ref[flag0 + j] != 0)
                def _():
                    fn(j, False)
                return carry
            lax.fori_loop(0, qi * per, body, 0)
            for dd in range(per):
                fn(qi * per + dd, True)
        else:
            for dd in range(per + WIN // tk):
                fn((qi + 1) * per - 1 - dd, True)

    def row_max(kt_true, masked):
        for hh, s in enumerate(scores(kt_true, masked)):
            sl = slice(hh * tq, (hh + 1) * tq)
            m_ref[sl] = jnp.maximum(m_ref[sl], _lane_max(s))

    def accumulate(kt_true, masked):
        off = pl.multiple_of(jnp.maximum(kt_true, 0) * tk, tk)
        v = v_ref[0, pl.ds(off, tk), :]
        for hh, s in enumerate(scores(kt_true, masked)):
            sl = slice(hh * tq, (hh + 1) * tq)
            p = jnp.exp2(s - jnp.concatenate([m_ref[sl]] * nrep, axis=1))
            acc_ref[sl] += _dot(p.astype(BF16), v)

    if not bounded:
        sweep(row_max)
        for hh in range(NSA_HPG):
            sl = slice(hh * tq, (hh + 1) * tq)
            m_ref[sl] = jnp.broadcast_to(jnp.max(m_ref[sl], axis=-1, keepdims=True), (tq, LANES))
    sweep(accumulate)

    gt = gt_ref[0]
    for pr in range(2):
        res = []
        for x in range(2):
            hh = 2 * pr + x
            a = acc_ref[hh * tq:(hh + 1) * tq]
            r = pltpu.roll(a, HEAD_DIM, 1)
            o = a / r if x == 0 else r / a
            res.append(o * gt[:, 3 * hh + branch:3 * hh + branch + 1])
        o_ref[0, :, pr * LANES:(pr + 1) * LANES] = _pair_select(lo_half, res[0], res[1])


def _nsa_gqa(flags, slopes, bound, safe, q, selb, k, v, pos2d, pos_col, gates, branch):
    b, s, _ = q.shape
    tq = min(TQ_NSA, s)
    tk = min(TK_NSA, tq)

    def call(bounded):
        kern = functools.partial(_gqa_kernel, tq=tq, tk=tk, branch=branch, bounded=bounded)
        smem = pl.BlockSpec(memory_space=pltpu.SMEM)
        name = ("nsa_sel_attn" if branch == 1 else "nsa_win_attn") + ("_bounded" if bounded else "")
        return pl.pallas_call(
            kern,
            grid=(b, NSA_KV_HEADS, s // tq),
            in_specs=[smem, smem, smem,
                      pl.BlockSpec((1, tq, 4 * LANES), lambda i, g, j: (i, j, g)),
                      pl.BlockSpec((1, 1, tq, LANES), lambda i, g, j: (i, g, j, 0)),
                      pl.BlockSpec((1, s, LANES), lambda i, g, j: (i, 0, g)),
                      pl.BlockSpec((1, s, LANES), lambda i, g, j: (i, 0, g)),
                      _full(pos2d.shape),
                      pl.BlockSpec((tq, 1), lambda i, g, j: (j, 0)),
                      pl.BlockSpec((1, tq, LANES), lambda i, g, j: (i, j, g))],
            out_specs=pl.BlockSpec((1, tq, 2 * LANES), lambda i, g, j: (i, j, g)),
            out_shape=jax.ShapeDtypeStruct((b, s, NSA_W), F32),
            scratch_shapes=[pltpu.VMEM((NSA_HPG * tq, 2 * LANES), BF16),
                            pltpu.VMEM((NSA_HPG * tq, LANES), F32),
                            pltpu.VMEM((NSA_HPG * tq, LANES), F32)],
            compiler_params=_cparams(3),
            name=name,
        )(flags, slopes, bound, q, selb, k, v, pos2d, pos_col, gates)

    return lax.cond(safe, lambda: call(True), lambda: call(False))


def _dil_kernel(slope_ref, bnd_ref, q_ref, k_ref, v_ref, pc_ref, pr_ref, o_ref, lse_ref,
                *, t, span, grp, bounded):
    i = pl.program_id(2)
    q4 = q_ref[0]
    lane = lax.broadcasted_iota(jnp.int32, (1, LANES), 1)
    lo_half = lane < HEAD_DIM
    prev = jnp.maximum(i - 1, 0)
    o_prev = pl.multiple_of(prev * t, t)
    o_cur = pl.multiple_of(i * t, t)
    kk = jnp.concatenate([k_ref[0, pl.ds(o_prev, t), :], k_ref[0, pl.ds(o_cur, t), :]], axis=0)
    vv = jnp.concatenate([v_ref[0, pl.ds(o_prev, t), :], v_ref[0, pl.ds(o_cur, t), :]], axis=0)
    pk = jnp.concatenate([pr_ref[0, prev], pr_ref[0, i]], axis=1)
    dist = (pc_ref[0] - pk) * LOG2E
    rows = i * t + lax.broadcasted_iota(jnp.int32, (t, 1), 0)
    cols = (i - 1) * t + lax.broadcasted_iota(jnp.int32, (1, 2 * t), 1)
    d = rows - cols
    valid = (cols >= 0) & (d >= 0) & (d <= span)
    zero = jnp.zeros((t, LANES), BF16)
    ones = jnp.ones((2 * t, LANES), BF16)
    for p in range(2):
        qp = q4[:, p * LANES:(p + 1) * LANES]
        kp = kk[:, p * LANES:(p + 1) * LANES]
        vx = jnp.concatenate([vv[:, p * LANES:(p + 1) * LANES], ones], axis=1)
        res = []
        for x in range(2):
            qx = jnp.where(lo_half, qp, zero) if x == 0 else jnp.where(lo_half, zero, qp)
            s = _dot_nt(qx, kp) - slope_ref[grp * DIL_HEADS + 2 * p + x] * dist
            s = jnp.where(valid, s, -jnp.inf)
            mx = bnd_ref[0] if bounded else jnp.max(s, axis=-1, keepdims=True)
            acc = _dot(jnp.exp2(s - mx).astype(BF16), vx)
            den = acc[:, LANES:2 * LANES]
            res.append((acc[:, 0:LANES] / den, mx * LN2 + jnp.log(den)))
        sl = slice(p * LANES, (p + 1) * LANES)
        o_ref[0, :, sl] = _pair_select(lo_half, res[0][0], res[1][0])
        lse_ref[0, :, sl] = jnp.where(lo_half, res[0][1], res[1][1])


def _dilated(slopes, bound, safe, qv, kv, vv, pos_f, grp):
    window, dil = DIL_CFG[grp]
    b, sub, wd = qv.shape
    w = wd // dil
    t = min(T_DIL, sub)
    span = window // dil
    assert span <= t
    pres = pos_f.reshape(sub, dil).T
    pc = pres.reshape(dil, sub, 1)
    pr = pres.reshape(dil, sub // t, 1, t)
    shp = jax.ShapeDtypeStruct((b, sub, wd), F32)
    smem = pl.BlockSpec(memory_space=pltpu.SMEM)

    def call(bounded):
        kern = functools.partial(_dil_kernel, t=t, span=span, grp=grp, bounded=bounded)
        return pl.pallas_call(
            kern,
            grid=(b, dil, sub // t),
            in_specs=[smem, smem,
                      pl.BlockSpec((1, t, w), lambda i, r, j: (i, j, r)),
                      pl.BlockSpec((1, sub, w), lambda i, r, j: (i, 0, r)),
                      pl.BlockSpec((1, sub, w), lambda i, r, j: (i, 0, r)),
                      pl.BlockSpec((1, t, 1), lambda i, r, j: (r, j, 0)),
                      pl.BlockSpec((1, sub // t, 1, t), lambda i, r, j: (r, 0, 0, 0))],
            out_specs=[pl.BlockSpec((1, t, w), lambda i, r, j: (i, j, r)),
                       pl.BlockSpec((1, t, w), lambda i, r, j: (i, j, r))],
            out_shape=[shp, shp],
            compiler_params=_cparams(3),
            name="dilated_attn_g%d%s" % (grp, "_bounded" if bounded else ""),
        )(slopes, bound, qv, kv, vv, pc, pr)

    return lax.cond(safe, lambda: call(True), lambda: call(False))


def _out_odd_kernel(x_ref, mod_ref, oc_ref, os_ref, ow_ref, d0_ref, d1_ref, d2_ref,
                    l0_ref, l1_ref, l2_ref, gz_ref, w_ref, o_ref, st):
    ts = x_ref.shape[1]
    gz = gz_ref[0].astype(F32)
    nsa = oc_ref[0] + os_ref[0] + ow_ref[0]
    m1 = (nsa * gz[:, 0:NSA_W]).astype(BF16)

    def token_order(ref, g, slot):
        dil = DIL_CFG[g][1]
        if dil == 1:
            return ref[0]
        for r in range(dil):
            for h in range(2):
                st[slot, h, pl.ds(r, ts // dil, stride=dil), :] = ref[0, :, r * DIL_W + h * 128:r * DIL_W + (h + 1) * 128]
        return jnp.concatenate([st[slot, 0], st[slot, 1]], axis=1)

    d0, d1, d2 = token_order(d0_ref, 0, 0), token_order(d1_ref, 1, 0), token_order(d2_ref, 2, 1)
    l0, l1, l2 = token_order(l0_ref, 0, 2), token_order(l1_ref, 1, 2), token_order(l2_ref, 2, 3)
    mx = jnp.maximum(jnp.maximum(l0, l1), l2)
    e0, e1, e2 = jnp.exp(l0 - mx), jnp.exp(l1 - mx), jnp.exp(l2 - mx)
    dil = (e0 * d0 + e1 * d1 + e2 * d2) / (e0 + e1 + e2)
    m2 = (dil * gz[:, NSA_W:NSA_W + DIL_W]).astype(BF16)
    y = _dot(m1, w_ref[0:NSA_W, :]) + _dot(m2, w_ref[NSA_W:NSA_W + DIL_W, :])
    gate = mod_ref[0][:, 2 * D_MODEL:3 * D_MODEL]
    o_ref[0] = x_ref[0] + gate * y


def _out_odd(x, mod3, oc, os_, ow, dils, lses, gz, w):
    b, s, d = x.shape
    ts = min(TS_PROJ, s)
    row = lambda n: pl.BlockSpec((1, ts, n), lambda i, j: (i, j, 0))
    res = [pl.BlockSpec((1, ts // DIL_CFG[g][1], DIL_CFG[g][1] * DIL_W), lambda i, j: (i, j, 0))
           for g in range(N_DIL)]
    return pl.pallas_call(
        _out_odd_kernel,
        grid=(b, s // ts),
        in_specs=[row(d), pl.BlockSpec((1, 1, 3 * d), lambda i, j: (i, 0, 0)),
                  row(NSA_W), row(NSA_W), row(NSA_W)] + res + res + [row(1024), _full(w.shape)],
        out_specs=row(d),
        out_shape=jax.ShapeDtypeStruct((b, s, d), F32),
        scratch_shapes=[pltpu.VMEM((4, 2, ts, 128), F32)],
        compiler_params=_cparams(2),
        name="outproj_odd",
    )(x, mod3, oc, os_, ow, *dils, *lses, gz, w)


def _pad_cols(w, n):
    return jnp.pad(w, ((0, 0), (0, n - w.shape[1])))


def _pad_vec(v, n=D_MODEL):
    return jnp.pad(v, (0, n - v.shape[0]))


def _group_matrix(sizes, total):
    m = np.zeros((total, total), np.float32)
    off = 0
    for sz, on in sizes:
        if on:
            m[off:off + sz, off:off + sz] = 1.0
        off += sz
    return jnp.asarray(m, BF16)


def _swap_halves(w):
    h = w.shape[-1] // 2
    return jnp.concatenate([w[..., h:], w[..., :h]], axis=-1)


def _pack_even(w_in, norm_g, sb_qn, sb_kn, qa_g, wq_up, kva_g, wkv_up, qn, kn):
    d = w_in.shape[0]
    z = lambda n: jnp.zeros((d, n), w_in.dtype)
    kr = w_in[:, 2432:2464]
    w = jnp.concatenate([
        w_in[:, 0:1536], w_in[:, 1536:2048], w_in[:, 2464:2976], w_in[:, 2048:2304], w_in[:, 2304:2432],
        z(64), kr, z(32), z(64), _swap_halves(kr), z(32)], axis=1).astype(BF16)
    wq3 = wq_up.reshape(MLA_Q_RANK, MLA_HEADS, MLA_NOPE + MLA_ROPE)
    zq = jnp.zeros((MLA_Q_RANK, MLA_HEADS, 32), wq_up.dtype)
    wq = jnp.concatenate([wq3, zq], axis=-1).reshape(MLA_Q_RANK, MLA_HEADS * 128).astype(BF16)
    wqs = jnp.concatenate([jnp.zeros((MLA_Q_RANK, MLA_HEADS, 64), wq_up.dtype),
                           _swap_halves(wq3[..., MLA_NOPE:]), zq], axis=-1)
    wqs = wqs.reshape(MLA_Q_RANK, MLA_HEADS * 128).astype(BF16)
    wkv3 = wkv_up.reshape(MLA_KV_RANK, MLA_HEADS, MLA_NOPE + MLA_V)
    wk = jnp.concatenate([wkv3[..., :MLA_NOPE], jnp.zeros_like(wkv3[..., :MLA_NOPE])], axis=-1)
    wkv = jnp.concatenate([wk.reshape(MLA_KV_RANK, MLA_HEADS * 128),
                           wkv3[..., MLA_NOPE:].reshape(MLA_KV_RANK, MLA_HEADS * MLA_V)], axis=1).astype(BF16)
    z32 = jnp.zeros((32,), F32)
    z64 = jnp.zeros((64,), F32)
    scale = (MLA_NOPE + MLA_ROPE) ** -0.5 * LOG2E
    qg = jnp.tile(jnp.concatenate([qn, z32]), MLA_HEADS) * scale
    qgs = jnp.tile(jnp.concatenate([z64, _swap_halves(qn[MLA_NOPE:]), z32]), MLA_HEADS) * scale
    kg = jnp.tile(jnp.concatenate([kn[:MLA_NOPE], z64]), MLA_HEADS)
    krg = jnp.concatenate([z64, kn[MLA_NOPE:], z32])
    krgs = jnp.concatenate([z64, _swap_halves(kn[MLA_NOPE:]), z32])
    cnt = jnp.tile(jnp.concatenate([jnp.full((64,), 1.0 / 64), jnp.full((32,), 1.0 / 32), jnp.ones((32,))]),
                   MLA_HEADS).astype(F32)
    rows = [norm_g, _pad_vec(jnp.tile(sb_qn, SB_HEADS) * (LOG2E * HEAD_DIM ** -0.5)), _pad_vec(jnp.tile(sb_kn, SB_HEADS)),
            _pad_vec(qa_g), _pad_vec(kva_g), qg, qgs, kg, _pad_vec(krg), _pad_vec(krgs), cnt]
    rows += [jnp.zeros((D_MODEL,), F32)] * (16 - len(rows))
    return w, wq, wqs, wkv, jnp.stack(rows).astype(F32)


def _pack_odd(w_in, norm_g, nsa_qn, nsa_kn, dil_qn, dil_kn):
    d = w_in.shape[0]

    def padded_heads(cols, n):
        c3 = cols.reshape(d, n, HEAD_DIM)
        return jnp.concatenate([c3, jnp.zeros_like(c3)], axis=-1).reshape(d, n * 128)

    def doubled_heads(cols, n):
        c3 = cols.reshape(d, n, HEAD_DIM)
        return jnp.concatenate([c3, c3], axis=-1).reshape(d, n * 128)

    ng = w_in[:, 1920:1956].reshape(d, NSA_KV_HEADS, NSA_HPG * 3)
    ng = jnp.pad(ng, ((0, 0), (0, 0), (0, 128 - NSA_HPG * 3))).reshape(d, NSA_KV_HEADS * 128)
    w_nsa = jnp.concatenate([
        padded_heads(w_in[:, 0:768], NSA_HEADS),
        padded_heads(w_in[:, 768:960], NSA_KV_HEADS), padded_heads(w_in[:, 960:1152], NSA_KV_HEADS),
        padded_heads(w_in[:, 1152:1344], NSA_KV_HEADS), padded_heads(w_in[:, 1344:1536], NSA_KV_HEADS),
        padded_heads(w_in[:, 1536:1728], NSA_KV_HEADS), padded_heads(w_in[:, 1728:1920], NSA_KV_HEADS),
        ng, w_in[:, 1956:2724], w_in[:, 5028:5284]], axis=1).astype(BF16)
    w_dil = w_in[:, 2724:5028].astype(BF16)
    z64 = jnp.zeros((64,), F32)
    n = NSA_HEADS * 128
    rows = [_pad_vec(norm_g, n),
            jnp.tile(jnp.concatenate([nsa_qn * (LOG2E * HEAD_DIM ** -0.5), z64]), NSA_HEADS),
            _pad_vec(jnp.tile(jnp.concatenate([nsa_kn, z64]), NSA_KV_HEADS), n),
            _pad_vec(jnp.tile(dil_qn, DIL_HEADS) * (LOG2E * HEAD_DIM ** -0.5), n),
            _pad_vec(jnp.tile(dil_kn, DIL_HEADS), n)]
    rows += [jnp.zeros((n,), F32)] * (8 - len(rows))
    return w_nsa, w_dil, jnp.stack(rows).astype(F32)


def _pack_compress(w, pe, double):
    w3 = w.reshape(CMP_LEN, HEAD_DIM, HEAD_DIM)
    w3 = jnp.concatenate([w3, jnp.zeros_like(w3)], axis=1)
    w3 = jnp.concatenate([w3, w3 if double else jnp.zeros_like(w3)], axis=2)
    wp = w3.reshape(2, CMP_STRIDE * 128, 128).astype(BF16)
    pe2 = jnp.concatenate([pe, jnp.zeros_like(pe)], axis=1).reshape(2, 1, CMP_STRIDE * 128)
    pe2 = jnp.broadcast_to(pe2, (2, 8, CMP_STRIDE * 128)).astype(F32)
    return wp, pe2


def _normed_len(gain, sizes):
    tot, off = 0.0, 0
    for n in sizes:
        tot = tot + n * jnp.max(jnp.square(gain[off:off + n]))
        off += n
    return jnp.sqrt(tot)


def _mla_logit_bound(qn, kn):
    sizes = (MLA_NOPE, MLA_ROPE)
    scale = (MLA_NOPE + MLA_ROPE) ** -0.5 * LOG2E
    return (_normed_len(qn, sizes) * _normed_len(kn, sizes) * (scale * ROUNDING_MARGIN)).reshape(1).astype(F32)


def _alibi_slopes(n):
    return 2.0 ** (-8.0 * jnp.arange(1, n + 1, dtype=jnp.float32) / n)


def kernel(x, c, positions, ada_w, ada_b, norm_g, ev_w_in, ev_w_out, sb_qn, sb_kn, mla_qa_g, mla_wq_up,
           mla_kva_g, mla_wkv_up, mla_qn, mla_kn, od_w_in, od_w_out, nsa_qn, nsa_kn, nsa_cmp_wk, nsa_cmp_wv,
           nsa_cmp_pe_k, nsa_cmp_pe_v, dil_qn, dil_kn):
    b, s, d = x.shape
    depth = ada_w.shape[0]
    pos_f = positions.astype(F32)

    inv_freq = ROPE_BASE ** (-jnp.arange(0, MLA_ROPE, 2, dtype=F32) / MLA_ROPE)
    ang = pos_f[:, None] * inv_freq[None, :]
    cos, sin = jnp.cos(ang), jnp.sin(ang)
    cos128 = jnp.concatenate([jnp.ones((s, 64), F32), cos, cos, jnp.zeros((s, 32), F32)], axis=1)
    sin128 = jnp.concatenate([jnp.zeros((s, 64), F32), -sin, sin, jnp.zeros((s, 32), F32)], axis=1)
    nsa_slopes = _alibi_slopes(NSA_HEADS)
    dil_slopes = _alibi_slopes(N_DIL * DIL_HEADS)
    nch = s // CMP_STRIDE
    chunk_sum = pos_f.reshape(nch, CMP_STRIDE).sum(axis=1)
    cpos = ((chunk_sum + jnp.roll(chunk_sum, -1)) / CMP_LEN).reshape(nch, 1)
    n_sel = s // SEL_LEN
    cst = np.arange(nch)[:, None] * CMP_STRIDE
    jst = np.arange(LANES)[None, :] * SEL_LEN
    ovl = ((cst <= jst + SEL_LEN - 1) & (cst + CMP_LEN - 1 >= jst) & (np.arange(LANES)[None, :] < n_sel))
    ovlt = jnp.asarray(ovl.astype(np.float32).T, BF16)
    pos_col = pos_f.reshape(s, 1)
    tqc = min(TQ_CMP, s)
    pos_rows = pos_f.reshape(s // tqc, 1, tqc)
    tqn = min(TQ_NSA, s)
    tkn = min(TK_NSA, tqn)
    assert tkn == TK_NSA
    pos2d = pos_f.reshape(s // tkn, 1, tkn)
    pos_sorted = jnp.all(pos_f[1:] >= pos_f[:-1])

    g64 = _group_matrix([(64, 1)] * 4, 256)
    gm = _group_matrix([(64, 1), (32, 1), (32, 0)] * 2, 256)
    tks = min(TK_SB, min(TQ_SB, s))
    tri = jnp.asarray(np.tril(np.ones((tks, tks), np.float32)), BF16)

    c8 = jnp.pad(c, ((0, 8 - b), (0, 0)))
    mod_all = _modulation(c8, ada_w, ada_b)

    for layer in range(depth):
        j = layer // 2
        mod3 = mod_all[layer, :b].reshape(b, 1, 3 * d)
        if layer % 2 == 0:
            w, wq, wqs, wkv, vec = _pack_even(ev_w_in[j], norm_g[layer], sb_qn[j], sb_kn[j], mla_qa_g[j],
                                              mla_wq_up[j], mla_kva_g[j], mla_wkv_up[j], mla_qn[j], mla_kn[j])
            sbq, sbk, sbv, gz, mq, mk, mv = _k1_even(x, mod3, vec, w, wq, wqs, wkv, g64, gm, cos128, sin128)
            sb_bound = (_normed_len(sb_qn[j], (HEAD_DIM,)) * _normed_len(sb_kn[j], (HEAD_DIM,))
                        * (HEAD_DIM ** -0.5 * LOG2E * ROUNDING_MARGIN))
            o_sb = _sb_attention(sbq, sbk, sbv, tri, sb_bound <= SAFE_LOGIT_BOUND)
            o_mla = _mla_attention(mq, mk, mv, _mla_logit_bound(mla_qn[j], mla_kn[j]))
            x = _out_even(x, mod3, o_sb, o_mla, gz, ev_w_out[j].astype(BF16))
        else:
            w_nsa, w_dil, vec = _pack_odd(od_w_in[j], norm_g[layer], nsa_qn[j], nsa_kn[j], dil_qn[j], dil_kn[j])
            q, ck, cv, sk, sv, wk, wv, gates, gz = _k1_nsa(x, mod3, vec, w_nsa, g64)
            dqkv = _k1_dil(x, mod3, vec, w_dil, g64)
            wck, pek = _pack_compress(nsa_cmp_wk[j], nsa_cmp_pe_k[j], False)
            wcv, pev = _pack_compress(nsa_cmp_wv[j], nsa_cmp_pe_v[j], True)
            kgain = jnp.concatenate([nsa_kn[j], jnp.zeros((64,), F32)]).reshape(1, 128)
            kc, vc = _compress(ck, cv, wck, wcv, pek, pev, kgain, g64[0:128, 0:128])
            o_c, selb, used = _nsa_cmp(nsa_slopes, q, kc, vc, ovlt, cpos, pos_rows, gates)
            flags = used[:, :, :, 0, :].reshape(b, NSA_KV_HEADS, s // tqn, tqn // min(TQ_CMP, s),
                                                SEL_TILE_STRIDE, LANES // SEL_TILE_STRIDE).max(axis=(3, 5))
            flags = (flags > 0).astype(jnp.int32).reshape(-1)
            nsa_bound = (_normed_len(nsa_qn[j], (HEAD_DIM,)) * _normed_len(nsa_kn[j], (HEAD_DIM,))
                         * (HEAD_DIM ** -0.5 * LOG2E * ROUNDING_MARGIN)).reshape(1).astype(F32)
            safe = (nsa_bound[0] <= SAFE_LOGIT_BOUND) & pos_sorted
            o_s = _nsa_gqa(flags, nsa_slopes, nsa_bound, safe, q, selb, sk, sv, pos2d, pos_col, gates, 1)
            o_w = _nsa_gqa(flags, nsa_slopes, nsa_bound, safe, q, selb, wk, wv, pos2d, pos_col, gates, 2)
            dil_bound = (_normed_len(dil_qn[j], (HEAD_DIM,)) * _normed_len(dil_kn[j], (HEAD_DIM,))
                         * (HEAD_DIM ** -0.5 * LOG2E * ROUNDING_MARGIN)).reshape(1).astype(F32)
            dil_safe = (dil_bound[0] <= SAFE_LOGIT_BOUND) & pos_sorted
            dils, lses = [], []
            for g in range(N_DIL):
                o, lse = _dilated(dil_slopes, dil_bound, dil_safe, dqkv[3 * g], dqkv[3 * g + 1], dqkv[3 * g + 2],
                                  pos_f, g)
                dils.append(o)
                lses.append(lse)
            x = _out_odd(x, mod3, o_c, o_s, o_w, dils, lses, gz, od_w_out[j].astype(BF16))
    return x
```

```python
import functools

import numpy as np
import jax
import jax.numpy as jnp
from jax import lax
from jax.experimental import pallas as pl
from jax.experimental.pallas import tpu as pltpu

F32 = jnp.float32
BF16 = jnp.bfloat16

D_MODEL = 1024
HEAD_DIM = 64
NORM_EPS = 1e-6
TINY = 1e-30
SB_HEADS = 8
MLA_HEADS = 8
MLA_Q_RANK = 256
MLA_KV_RANK = 128
MLA_NOPE = 64
MLA_ROPE = 32
MLA_V = 64
ROPE_BASE = 10000.0
NSA_HEADS = 12
NSA_KV_HEADS = 3
NSA_HPG = 4
CMP_LEN = 32
CMP_STRIDE = 16
SEL_LEN = 64
SEL_TOPN = 16
WIN = 512
FORCE_BONUS = 1e3
DIL_CFG = ((128, 1), (512, 4), (2048, 16))
N_DIL = 3
DIL_HEADS = 4
SB_W = SB_HEADS * HEAD_DIM
MLA_OUT = MLA_HEADS * MLA_V
NSA_W = NSA_HEADS * HEAD_DIM
NSA_KV_W = NSA_KV_HEADS * HEAD_DIM
DIL_W = DIL_HEADS * HEAD_DIM

LANES = 128
MXU_DIM = 256
NEG_MASK = -1e30
LOG2E = 1.4426950408889634
LN2 = 0.6931471805599453
SAFE_LOGIT_BOUND = 50.0
ROUNDING_MARGIN = 1.02
SEL_OFF = 2.0 ** 30
VMEM_LIMIT = 56 * 1024 * 1024

TS_PROJ = 512
TQ_SB = 1024
TK_SB = 256
TQ_MLA = 1024
TD_MLA = 512
TQ_CMP = 512
TQ_NSA = 512
TK_NSA = 512
SEL_TILE_STRIDE = LANES * SEL_LEN // TK_NSA
T_DIL = 256


def _dot(a, b):
    return jnp.dot(a, b, preferred_element_type=F32)


def _dot_nt(a, b):
    return lax.dot_general(a, b, (((1,), (1,)), ((), ())), preferred_element_type=F32)


def _split_hl(a):
    hi = a.astype(BF16)
    lo = (a - hi.astype(F32)).astype(BF16)
    return hi, lo


def _dot_hl(a, b):
    hi, lo = _split_hl(a)
    return _dot(hi, b) + _dot(lo, b)


def _sigmoid(z):
    return 1.0 / (1.0 + jnp.exp(-z))


def _cparams(n_axes):
    return pltpu.CompilerParams(dimension_semantics=("arbitrary",) * n_axes,
                                vmem_limit_bytes=VMEM_LIMIT)


def _full(shape):
    n = len(shape)
    return pl.BlockSpec(shape, lambda *a, _n=n: (0,) * _n)


def _mod_kernel(c_ref, w_ref, b_ref, o_ref):
    c = c_ref[...]
    a = c * _sigmoid(c)
    ah, al = _split_hl(a)
    wh, wl = _split_hl(w_ref[0])
    o_ref[0] = _dot(ah, wh) + _dot(ah, wl) + _dot(al, wh) + b_ref[0]


def _modulation(c8, ada_w, ada_b):
    depth, d, n3 = ada_w.shape
    tn = 1024
    return pl.pallas_call(
        _mod_kernel,
        grid=(depth, n3 // tn),
        in_specs=[pl.BlockSpec((8, d), lambda l, j: (0, 0)),
                  pl.BlockSpec((1, d, tn), lambda l, j: (l, 0, j)),
                  pl.BlockSpec((1, 1, tn), lambda l, j: (l, 0, j))],
        out_specs=pl.BlockSpec((1, 8, tn), lambda l, j: (l, 0, j)),
        out_shape=jax.ShapeDtypeStruct((depth, 8, n3), F32),
        compiler_params=_cparams(2),
        name="adaln_mod",
    )(c8, ada_w, ada_b.reshape(depth, 1, n3))


def _modulated(x_ref, mod_ref, ng):
    x = x_ref[0]
    mod = mod_ref[0]
    shift = mod[:, 0:D_MODEL]
    scale = mod[:, D_MODEL:2 * D_MODEL]
    ms = jnp.mean(x * x, axis=-1, keepdims=True)
    h = x * lax.rsqrt(ms + NORM_EPS) * ng
    h = h * (1.0 + scale) + shift
    return h.astype(BF16)


def _group_norm_chunk(t, g, inv_cnt):
    ss = _dot_hl(t * t, g)
    return t * lax.rsqrt(ss * inv_cnt + NORM_EPS)


def _row_rms(t, gain):
    ms = jnp.mean(t * t, axis=-1, keepdims=True)
    return t * lax.rsqrt(ms + NORM_EPS) * gain


EV_COLS = 3200


def _k1_even_kernel(x_ref, mod_ref, vec_ref, w_ref, wq_ref, wqs_ref, wkv_ref, g64_ref, gm_ref,
                    cos_ref, sin_ref, sbq_o, sbk_o, sbv_o, gz_o, mq_o, mk_o, mv_o):
    hb = _modulated(x_ref, mod_ref, vec_ref[0:1, :])
    g64 = g64_ref[...]
    gm = gm_ref[...]
    inv64 = 1.0 / HEAD_DIM

    u = _dot(hb, w_ref[:, 0:1536])
    for c in range(2):
        sl = slice(c * 256, (c + 1) * 256)
        qn = _group_norm_chunk(u[:, c * 256:(c + 1) * 256], g64, inv64)
        sbq_o[0, :, sl] = (qn * vec_ref[1:2, sl]).astype(BF16)
        kn = _group_norm_chunk(u[:, 512 + c * 256:512 + (c + 1) * 256], g64, inv64)
        sbk_o[0, :, sl] = (kn * vec_ref[2:3, sl]).astype(BF16)
    sbv_o[0] = u[:, 1024:1536].astype(BF16)

    uz = _dot(hb, w_ref[:, 1536:2560])
    gz_o[0] = (uz * _sigmoid(uz)).astype(BF16)

    ul = _dot(hb, w_ref[:, 2560:3200])
    qlat = _row_rms(ul[:, 0:256], vec_ref[3:4, 0:256]).astype(BF16)
    kvlat = _row_rms(ul[:, 256:384], vec_ref[4:5, 0:128]).astype(BF16)
    krm = ul[:, 384:512]
    krs = ul[:, 512:640]
    cs = cos_ref[...]
    sn = sin_ref[...]
    cs2 = jnp.concatenate([cs, cs], axis=1)
    sn2 = jnp.concatenate([sn, sn], axis=1)

    tq = _dot(qlat, wq_ref[...])
    tqs = _dot(qlat, wqs_ref[...])
    for p in range(4):
        sl = slice(p * 256, (p + 1) * 256)
        tc = tq[:, p * 256:(p + 1) * 256]
        ss = _dot_hl(tc * tc, gm)
        inv = lax.rsqrt(ss * vec_ref[10:11, sl] + NORM_EPS)
        a = tc * inv * vec_ref[5:6, sl]
        b = tqs[:, p * 256:(p + 1) * 256] * inv * vec_ref[6:7, sl]
        mq_o[0, :, sl] = (a * cs2 + b * sn2).astype(BF16)

    sskr = _dot_hl(krm * krm, gm[0:128, 0:128])
    invr = lax.rsqrt(sskr * vec_ref[10:11, 0:128] + NORM_EPS)
    kr = (krm * invr * vec_ref[8:9, 0:128]) * cs + (krs * invr * vec_ref[9:10, 0:128]) * sn
    kr2 = jnp.concatenate([kr, kr], axis=1)

    kv = _dot(kvlat, wkv_ref[...])
    for p in range(4):
        sl = slice(p * 256, (p + 1) * 256)
        kn = _group_norm_chunk(kv[:, p * 256:(p + 1) * 256], g64, inv64)
        mk_o[0, :, sl] = (kn * vec_ref[7:8, sl] + kr2).astype(BF16)
    mv_o[0] = kv[:, 1024:1536].astype(BF16)


def _k1_even(x, mod3, vec, w, wq, wqs, wkv, g64, gm, cos128, sin128):
    b, s, d = x.shape
    ts = min(TS_PROJ, s)
    row = lambda n: pl.BlockSpec((1, ts, n), lambda i, j: (i, j, 0))
    outs = [(512, BF16), (512, BF16), (512, BF16), (1024, BF16), (1024, BF16), (1024, BF16), (512, BF16)]
    return pl.pallas_call(
        _k1_even_kernel,
        grid=(b, s // ts),
        in_specs=[row(d),
                  pl.BlockSpec((1, 1, 3 * d), lambda i, j: (i, 0, 0)),
                  _full(vec.shape), _full(w.shape), _full(wq.shape), _full(wqs.shape), _full(wkv.shape),
                  _full(g64.shape), _full(gm.shape),
                  pl.BlockSpec((ts, 128), lambda i, j: (j, 0)),
                  pl.BlockSpec((ts, 128), lambda i, j: (j, 0))],
        out_specs=[row(n) for n, _ in outs],
        out_shape=[jax.ShapeDtypeStruct((b, s, n), dt) for n, dt in outs],
        compiler_params=_cparams(2),
        name="inproj_even",
    )(x, mod3, vec, w, wq, wqs, wkv, g64, gm, cos128, sin128)


def _sb_kernel(q_ref, k_ref, v_ref, tri_ref, o_ref, acc_ref, car_ref, *, tq, tk, bounded):
    qi = pl.program_id(2)
    q = q_ref[0]
    lane = lax.broadcasted_iota(jnp.int32, (1, LANES), 1)
    lo_half = lane < HEAD_DIM
    zero = jnp.zeros_like(q)
    qs = (jnp.where(lo_half, q, zero), jnp.where(lo_half, zero, q))
    tri = tri_ref[...]
    acc_ref[...] = jnp.zeros_like(acc_ref)
    car_ref[...] = jnp.zeros_like(car_ref)
    col0 = lax.broadcasted_iota(jnp.int32, (1, tk), 1)

    def tile(kt, r0, masked):
        n = tq - r0
        off = pl.multiple_of(kt * tk, tk)
        k = k_ref[0, pl.ds(off, tk), :]
        v = v_ref[0, pl.ds(off, tk), :]
        if masked:
            rows = qi * tq + r0 + lax.broadcasted_iota(jnp.int32, (n, 1), 0)
            strict = (kt * tk + col0) < rows
        for i in range(2):
            z = _dot_nt(qs[i][r0:tq], k)
            if bounded:
                lom = jnp.log(1.0 + jnp.exp2(z)) * (-LOG2E)
            else:
                lom = -jnp.maximum(z, 0.0) - jnp.log(1.0 + jnp.exp2(-jnp.abs(z))) * LOG2E
            if masked:
                lom = jnp.where(strict, lom, 0.0)
            tt = _dot(lom.astype(BF16), tri)
            car = car_ref[i, r0:tq]
            w = jnp.exp2(tt + z + jnp.concatenate([car] * (tk // LANES), axis=1))
            if masked:
                w = jnp.where(strict, w, 0.0)
            acc_ref[i, r0:tq] += _dot(w.astype(BF16), v)
            car_ref[i, r0:tq] = car + jnp.broadcast_to(tt[:, 0:1], (n, LANES))

    per = tq // tk
    for c in reversed(range(per)):
        tile(qi * per + c, c * tk, True)

    def body(j, carry):
        for u in range(per):
            tile(qi * per - 1 - per * j - u, 0, False)
        return carry

    lax.fori_loop(0, qi, body, 0)
    o_ref[0] = jnp.where(lo_half, acc_ref[0], acc_ref[1])


def _sb_attention(q, k, v, tri, safe):
    b, s, w = q.shape
    tq = min(TQ_SB, s)
    tk = min(TK_SB, tq)

    def call(bounded):
        kern = functools.partial(_sb_kernel, tq=tq, tk=tk, bounded=bounded)
        return pl.pallas_call(
            kern,
            grid=(b, w // LANES, s // tq),
            in_specs=[pl.BlockSpec((1, tq, LANES), lambda i, p, j: (i, j, p)),
                      pl.BlockSpec((1, s, LANES), lambda i, p, j: (i, 0, p)),
                      pl.BlockSpec((1, s, LANES), lambda i, p, j: (i, 0, p)),
                      _full(tri.shape)],
            out_specs=pl.BlockSpec((1, tq, LANES), lambda i, p, j: (i, j, p)),
            out_shape=jax.ShapeDtypeStruct((b, s, w), F32),
            scratch_shapes=[pltpu.VMEM((2, tq, LANES), F32), pltpu.VMEM((2, tq, LANES), F32)],
            compiler_params=_cparams(3),
            name="stickbreak_attn_bounded" if bounded else "stickbreak_attn",
        )(q, k, v, tri)

    return lax.cond(safe, lambda: call(True), lambda: call(False))


def _lane_max(s):
    m = s[:, 0:LANES]
    for c in range(1, s.shape[1] // LANES):
        m = jnp.maximum(m, s[:, c * LANES:(c + 1) * LANES])
    return m


def _mla_kernel(bnd_ref, q_ref, k_ref, v_ref, o_ref, acc_ref, m_ref, *, tq, td, bounded):
    qi = pl.program_id(2)
    q = q_ref[0]
    qs = (q[:, 0:LANES], q[:, LANES:2 * LANES])
    lane = lax.broadcasted_iota(jnp.int32, (1, LANES), 1)
    lo_half = lane < HEAD_DIM
    acc_ref[...] = jnp.zeros_like(acc_ref)
    if bounded:
        m_ref[...] = jnp.full(m_ref.shape, bnd_ref[0], F32)
    else:
        m_ref[...] = jnp.full_like(m_ref, NEG_MASK)

    def scores(off, width, r0, masked):
        k = k_ref[0, pl.ds(off, width), :]
        if masked:
            rows = qi * tq + r0 + lax.broadcasted_iota(jnp.int32, (tq - r0, 1), 0)
            keep = (off + lax.broadcasted_iota(jnp.int32, (1, width), 1)) <= rows
        out = []
        for i in range(2):
            s = _dot_nt(qs[i][r0:tq], k[:, i * LANES:(i + 1) * LANES])
            if masked:
                s = jnp.where(keep, s, NEG_MASK)
            out.append(s)
        return out

    def sweep(fn):
        def body(j, carry):
            fn(pl.multiple_of(j * tq, tq), tq, 0, False)
            return carry
        lax.fori_loop(0, qi, body, 0)
        for c in range(tq // td):
            fn(pl.multiple_of(qi * tq + c * td, td), td, c * td, True)

    def row_max(off, width, r0, masked):
        for i, s in enumerate(scores(off, width, r0, masked)):
            m_ref[i, r0:tq] = jnp.maximum(m_ref[i, r0:tq], _lane_max(s))

    def accumulate(off, width, r0, masked):
        vx = jnp.concatenate([v_ref[0, pl.ds(off, width), :], jnp.ones((width, LANES), BF16)], axis=1)
        for i, s in enumerate(scores(off, width, r0, masked)):
            p = jnp.exp2(s - jnp.concatenate([m_ref[i, r0:tq]] * (width // LANES), axis=1))
            acc_ref[i, r0:tq] += _dot(p.astype(BF16), vx)

    if not bounded:
        sweep(row_max)
        for i in range(2):
            m_ref[i] = jnp.broadcast_to(jnp.max(m_ref[i], axis=-1, keepdims=True), (tq, LANES))
    sweep(accumulate)
    o_ref[0] = jnp.where(lo_half, acc_ref[0, :, 0:LANES] / acc_ref[0, :, LANES:2 * LANES],
                         acc_ref[1, :, 0:LANES] / acc_ref[1, :, LANES:2 * LANES])


def _mla_attention(q, k, v, bound):
    b, s, _ = q.shape
    tq = min(TQ_MLA, s)
    td = min(TD_MLA, tq)

    def call(bounded):
        kern = functools.partial(_mla_kernel, tq=tq, td=td, bounded=bounded)
        return pl.pallas_call(
            kern,
            grid=(b, MLA_HEADS // 2, s // tq),
            in_specs=[pl.BlockSpec(memory_space=pltpu.SMEM),
                      pl.BlockSpec((1, tq, 2 * LANES), lambda i, p, j: (i, j, p)),
                      pl.BlockSpec((1, s, 2 * LANES), lambda i, p, j: (i, 0, p)),
                      pl.BlockSpec((1, s, LANES), lambda i, p, j: (i, 0, p))],
            out_specs=pl.BlockSpec((1, tq, LANES), lambda i, p, j: (i, j, p)),
            out_shape=jax.ShapeDtypeStruct((b, s, MLA_OUT), F32),
            scratch_shapes=[pltpu.VMEM((2, tq, 2 * LANES), F32), pltpu.VMEM((2, tq, LANES), F32)],
            compiler_params=_cparams(3),
            name="mla_attn_bounded" if bounded else "mla_attn",
        )(bound, q, k, v)

    return lax.cond(bound[0] <= SAFE_LOGIT_BOUND, lambda: call(True), lambda: call(False))


def _out_even_kernel(x_ref, mod_ref, osb_ref, omla_ref, gz_ref, w_ref, o_ref):
    gz = gz_ref[0].astype(F32)
    m1 = (osb_ref[0] * gz[:, 0:SB_W]).astype(BF16)
    m2 = (omla_ref[0] * gz[:, SB_W:SB_W + MLA_OUT]).astype(BF16)
    y = _dot(m1, w_ref[0:SB_W, :]) + _dot(m2, w_ref[SB_W:SB_W + MLA_OUT, :])
    gate = mod_ref[0][:, 2 * D_MODEL:3 * D_MODEL]
    o_ref[0] = x_ref[0] + gate * y


def _out_even(x, mod3, osb, omla, gz, w):
    b, s, d = x.shape
    ts = min(TS_PROJ, s)
    row = lambda n: pl.BlockSpec((1, ts, n), lambda i, j: (i, j, 0))
    return pl.pallas_call(
        _out_even_kernel,
        grid=(b, s // ts),
        in_specs=[row(d), pl.BlockSpec((1, 1, 3 * d), lambda i, j: (i, 0, 0)),
                  row(SB_W), row(MLA_OUT), row(SB_W + MLA_OUT), _full(w.shape)],
        out_specs=row(d),
        out_shape=jax.ShapeDtypeStruct((b, s, d), F32),
        compiler_params=_cparams(2),
        name="outproj_even",
    )(x, mod3, osb, omla, gz, w)


NSA_COLS = 5248


def _k1_nsa_kernel(x_ref, mod_ref, vec_ref, w_ref, g64_ref,
                   q_o, ck_o, cv_o, sk_o, sv_o, wk_o, wv_o, gt_o, gz_o, chunk_ref):
    ts = x_ref.shape[1]
    hb = _modulated(x_ref, mod_ref, vec_ref[0:1, 0:D_MODEL])
    g64 = g64_ref[...]
    inv64 = 1.0 / HEAD_DIM

    uq = _dot(hb, w_ref[:, 0:1536])
    for c in range(6):
        sl = slice(c * 256, (c + 1) * 256)
        qn = _group_norm_chunk(uq[:, c * 256:(c + 1) * 256], g64, inv64)
        q_o[0, :, sl] = (qn * vec_ref[1:2, sl]).astype(BF16)

    uc = _dot(hb, w_ref[:, 1536:2304])
    for c in range(2 * NSA_KV_HEADS):
        chunk_ref[c] = uc[:, c * 128:(c + 1) * 128]
    for g in range(NSA_KV_HEADS):
        for tok in range(CMP_STRIDE):
            rows = pl.ds(tok, ts // CMP_STRIDE, stride=CMP_STRIDE)
            sl = slice(tok * 128, (tok + 1) * 128)
            ck_o[0, g, :, sl] = chunk_ref[g, rows, :].astype(BF16)
            cv_o[0, g, :, sl] = chunk_ref[NSA_KV_HEADS + g, rows, :].astype(BF16)

    us = _dot(hb, w_ref[:, 2304:3840])
    kgain = vec_ref[2:3, 0:384]
    g128 = g64[0:128, 0:128]
    for g in range(NSA_KV_HEADS):
        sl = slice(g * 128, (g + 1) * 128)
        t = us[:, g * 128:(g + 1) * 128]
        ss = _dot_hl(t * t, g128)
        sk_o[0, :, sl] = (t * lax.rsqrt(ss * inv64 + NORM_EPS) * kgain[:, sl]).astype(BF16)
        t = us[:, 768 + g * 128:768 + (g + 1) * 128]
        ss = _dot_hl(t * t, g128)
        wk_o[0, :, sl] = (t * lax.rsqrt(ss * inv64 + NORM_EPS) * kgain[:, sl]).astype(BF16)
    ones_hi = (lax.broadcasted_iota(jnp.int32, (1, 384), 1) % 128) >= HEAD_DIM
    sv_o[0] = jnp.where(ones_hi, 1.0, us[:, 384:768]).astype(BF16)
    wv_o[0] = jnp.where(ones_hi, 1.0, us[:, 1152:1536]).astype(BF16)

    ug = _dot(hb, w_ref[:, 3840:4224])
    gt_o[0] = _sigmoid(ug)
    uz = _dot(hb, w_ref[:, 4224:5248])
    gz_o[0] = (uz * _sigmoid(uz)).astype(BF16)


def _k1_nsa(x, mod3, vec, w, g64):
    b, s, d = x.shape
    ts = min(TS_PROJ, s)
    row = lambda n: pl.BlockSpec((1, ts, n), lambda i, j: (i, j, 0))
    cw = CMP_STRIDE * 128
    grp = pl.BlockSpec((1, NSA_KV_HEADS, ts // CMP_STRIDE, cw), lambda i, j: (i, 0, j, 0))
    return pl.pallas_call(
        _k1_nsa_kernel,
        grid=(b, s // ts),
        in_specs=[row(d), pl.BlockSpec((1, 1, 3 * d), lambda i, j: (i, 0, 0)),
                  _full(vec.shape), _full(w.shape), _full(g64.shape)],
        out_specs=[row(1536), grp, grp, row(384), row(384), row(384), row(384), row(384), row(1024)],
        out_shape=[jax.ShapeDtypeStruct((b, s, 1536), BF16),
                   jax.ShapeDtypeStruct((b, NSA_KV_HEADS, s // CMP_STRIDE, cw), BF16),
                   jax.ShapeDtypeStruct((b, NSA_KV_HEADS, s // CMP_STRIDE, cw), BF16),
                   jax.ShapeDtypeStruct((b, s, 384), BF16),
                   jax.ShapeDtypeStruct((b, s, 384), BF16),
                   jax.ShapeDtypeStruct((b, s, 384), BF16),
                   jax.ShapeDtypeStruct((b, s, 384), BF16),
                   jax.ShapeDtypeStruct((b, s, 384), F32),
                   jax.ShapeDtypeStruct((b, s, 1024), BF16)],
        scratch_shapes=[pltpu.VMEM((2 * NSA_KV_HEADS, ts, 128), F32)],
        compiler_params=_cparams(2),
        name="inproj_nsa",
    )(x, mod3, vec, w, g64)


def _k1_dil_kernel(x_ref, mod_ref, vec_ref, w_ref, g64_ref, *refs):
    outs, st = refs[:3 * N_DIL], refs[3 * N_DIL]
    ts = x_ref.shape[1]
    hb = _modulated(x_ref, mod_ref, vec_ref[0:1, 0:D_MODEL])
    g64 = g64_ref[...]
    inv64 = 1.0 / HEAD_DIM
    u = _dot(hb, w_ref[...])
    for g in range(N_DIL):
        dil = DIL_CFG[g][1]
        qn = _group_norm_chunk(u[:, g * 256:(g + 1) * 256], g64, inv64) * vec_ref[3:4, 0:256]
        kn = _group_norm_chunk(u[:, 768 + g * 256:768 + (g + 1) * 256], g64, inv64) * vec_ref[4:5, 0:256]
        vals = (qn, kn, u[:, 1536 + g * 256:1536 + (g + 1) * 256])
        for j, val in enumerate(vals):
            o_ref = outs[3 * g + j]
            if dil == 1:
                o_ref[0] = val.astype(BF16)
                continue
            for h in range(2):
                st[j, h] = val[:, h * 128:(h + 1) * 128]
            for r in range(dil):
                for h in range(2):
                    sl = slice(r * 256 + h * 128, r * 256 + (h + 1) * 128)
                    o_ref[0, :, sl] = st[j, h, pl.ds(r, ts // dil, stride=dil), :].astype(BF16)


def _k1_dil(x, mod3, vec, w, g64):
    b, s, d = x.shape
    ts = min(TS_PROJ, s)
    row = lambda n: pl.BlockSpec((1, ts, n), lambda i, j: (i, j, 0))
    specs, shapes = [], []
    for g in range(N_DIL):
        dil = DIL_CFG[g][1]
        for _ in range(3):
            specs.append(pl.BlockSpec((1, ts // dil, dil * 256), lambda i, j: (i, j, 0)))
            shapes.append(jax.ShapeDtypeStruct((b, s // dil, dil * 256), BF16))
    return pl.pallas_call(
        _k1_dil_kernel,
        grid=(b, s // ts),
        in_specs=[row(d), pl.BlockSpec((1, 1, 3 * d), lambda i, j: (i, 0, 0)),
                  _full(vec.shape), _full(w.shape), _full(g64.shape)],
        out_specs=specs,
        out_shape=shapes,
        scratch_shapes=[pltpu.VMEM((3, 2, ts, 128), F32)],
        compiler_params=_cparams(2),
        name="inproj_dil",
    )(x, mod3, vec, w, g64)


def _compress_kernel(xk_ref, xv_ref, wk_ref, wv_ref, pek_ref, pev_ref, gain_ref, g128_ref, kc_o, vc_o):
    nch = xk_ref.shape[2]

    def comp(x_ref, w_ref, pe_ref):
        x = x_ref[0, 0]
        lo = _dot(x, w_ref[0])
        hi = _dot(x, w_ref[1])
        pec = _dot_hl(pe_ref[0], w_ref[0]) + _dot_hl(pe_ref[1], w_ref[1])
        return lo + pltpu.roll(hi, nch - 1, 0) + pec[0:1, :]

    kc = comp(xk_ref, wk_ref, pek_ref)
    ss = _dot_hl(kc * kc, g128_ref[...])
    kc_o[0, 0] = (kc * lax.rsqrt(ss * (1.0 / HEAD_DIM) + NORM_EPS) * gain_ref[...]).astype(BF16)
    vc_o[0, 0] = comp(xv_ref, wv_ref, pev_ref).T.astype(BF16)


def _compress(ck, cv, wk, wv, pek, pev, gain, g128):
    b, g, nch, _ = ck.shape
    xk, xv = ck, cv
    blk = pl.BlockSpec((1, 1, nch, CMP_STRIDE * 128), lambda i, j: (i, j, 0, 0))
    oblk = pl.BlockSpec((1, 1, nch, 128), lambda i, j: (i, j, 0, 0))
    tblk = pl.BlockSpec((1, 1, 128, nch), lambda i, j: (i, j, 0, 0))
    return pl.pallas_call(
        _compress_kernel,
        grid=(b, g),
        in_specs=[blk, blk, _full(wk.shape), _full(wv.shape), _full(pek.shape), _full(pev.shape),
                  _full(gain.shape), _full(g128.shape)],
        out_specs=[oblk, tblk],
        out_shape=[jax.ShapeDtypeStruct((b, g, nch, 128), BF16), jax.ShapeDtypeStruct((b, g, 128, nch), BF16)],
        compiler_params=_cparams(2),
        name="nsa_compress",
    )(xk, xv, wk, wv, pek, pev, gain, g128)


def _pair_select(lo_half, a, b):
    return jnp.where(lo_half, a, b)


def _nsa_cmp_kernel(slope_ref, q_ref, kc_ref, vct_ref, ovlt_ref, cposc_ref, prow_ref, gt_ref,
                    oc_o, selb_o, any_o, *, tq):
    g = pl.program_id(1)
    qi = pl.program_id(2)
    q4 = q_ref[0]
    kc = kc_ref[0, 0]
    vct = vct_ref[0, 0]
    nch = kc.shape[0]
    t = qi * tq + lax.broadcasted_iota(jnp.int32, (1, tq), 1)
    n_id = lax.broadcasted_iota(jnp.int32, (nch, 1), 0)
    valid = (n_id * CMP_STRIDE + (CMP_LEN - 1)) <= t
    dist = (prow_ref[qi] - cposc_ref[...]) * LOG2E
    gtt = gt_ref[0].T
    row_lo = lax.broadcasted_iota(jnp.int32, (LANES, 1), 0) < HEAD_DIM
    psum = jnp.zeros((nch, tq), F32)
    outs = []
    for hh in range(NSA_HPG):
        s = _dot_nt(kc, q4[:, hh * LANES:(hh + 1) * LANES]) - slope_ref[g * NSA_HPG + hh] * dist
        s = jnp.where(valid, s, -jnp.inf)
        mx = jnp.max(s, axis=0, keepdims=True)
        mx = jnp.where(mx == -jnp.inf, 0.0, mx)
        e = jnp.exp2(s - mx)
        den = jnp.maximum(jnp.sum(e, axis=0, keepdims=True), TINY)
        p = e * (1.0 / den)
        psum = psum + p
        outs.append(_dot(vct, p.astype(BF16)) * gtt[3 * hh:3 * hh + 1, :])
    oc_o[0, :, 0:LANES] = jnp.where(row_lo, outs[0], outs[1]).T
    oc_o[0, :, LANES:2 * LANES] = jnp.where(row_lo, outs[2], outs[3]).T

    hi, lo = _split_hl(psum)
    imp = _dot(ovlt_ref[...], hi) + _dot(ovlt_ref[...], lo)
    blk = lax.broadcasted_iota(jnp.int32, (LANES, 1), 0)
    cur = t >> 6
    forced = (blk == 0) | (blk == cur) | (blk == cur - 1)
    allowed = blk <= cur
    score = jnp.where(allowed, imp + jnp.where(forced, FORCE_BONUS, 0.0), -jnp.inf)
    blk_f = blk.astype(F32)

    def pick(_, carry):
        sc, sel = carry
        mx = jnp.max(sc, axis=0, keepdims=True)
        idx = jnp.min(jnp.where(sc == mx, blk_f, float(LANES)), axis=0, keepdims=True)
        hit = blk_f == idx
        return jnp.where(hit, -jnp.inf, sc), jnp.where(hit, 1.0, sel)

    _, sel = lax.fori_loop(0, SEL_TOPN, pick, (score, jnp.zeros((LANES, tq), F32)))
    chosen = jnp.where((sel > 0.5) & allowed, 1.0, 0.0).T
    selb_o[0, 0] = ((chosen - 1.0) * SEL_OFF).astype(BF16)
    used = jnp.max(chosen, axis=0, keepdims=True)
    any_o[0, 0, 0] = jnp.broadcast_to(used, (8, LANES))


def _nsa_cmp(slopes, q, kc, vct, ovlt, cposc, pos_rows, gates):
    b, s, _ = q.shape
    nch = kc.shape[2]
    tq = pos_rows.shape[2]
    kern = functools.partial(_nsa_cmp_kernel, tq=tq)
    return pl.pallas_call(
        kern,
        grid=(b, NSA_KV_HEADS, s // tq),
        in_specs=[pl.BlockSpec(memory_space=pltpu.SMEM),
                  pl.BlockSpec((1, tq, 4 * LANES), lambda i, g, j: (i, j, g)),
                  pl.BlockSpec((1, 1, nch, LANES), lambda i, g, j: (i, g, 0, 0)),
                  pl.BlockSpec((1, 1, LANES, nch), lambda i, g, j: (i, g, 0, 0)),
                  _full(ovlt.shape), _full(cposc.shape), _full(pos_rows.shape),
                  pl.BlockSpec((1, tq, LANES), lambda i, g, j: (i, j, g))],
        out_specs=[pl.BlockSpec((1, tq, 2 * LANES), lambda i, g, j: (i, j, g)),
                   pl.BlockSpec((1, 1, tq, LANES), lambda i, g, j: (i, g, j, 0)),
                   pl.BlockSpec((1, 1, 1, 8, LANES), lambda i, g, j: (i, g, j, 0, 0))],
        out_shape=[jax.ShapeDtypeStruct((b, s, NSA_W), F32),
                   jax.ShapeDtypeStruct((b, NSA_KV_HEADS, s, LANES), BF16),
                   jax.ShapeDtypeStruct((b, NSA_KV_HEADS, s // tq, 8, LANES), F32)],
        compiler_params=_cparams(3),
        name="nsa_cmp_topk",
    )(slopes, q, kc, vct, ovlt, cposc, pos_rows, gates)


def _gqa_kernel(flag_ref, slope_ref, bnd_ref, q_ref, selb_ref, k_ref, v_ref, pos_ref, pcol_ref, gt_ref, o_ref,
                qa_ref, acc_ref, m_ref, *, tq, tk, branch, bounded):
    bi = pl.program_id(0)
    g = pl.program_id(1)
    qi = pl.program_id(2)
    q4 = q_ref[0]
    lane = lax.broadcasted_iota(jnp.int32, (1, LANES), 1)
    lo_half = lane < HEAD_DIM
    pref = pos_ref[qi * (tq // tk)][:, 0:1]
    for hh in range(NSA_HPG):
        sl = slice(hh * tq, (hh + 1) * tq)
        qa_ref[sl, 0:LANES] = q4[:, hh * LANES:(hh + 1) * LANES]
        if branch == 1:
            qa_ref[sl, LANES:2 * LANES] = selb_ref[0, 0]
        if bounded:
            own = (pcol_ref[...] - pref) * (LOG2E * slope_ref[g * NSA_HPG + hh]) + bnd_ref[0]
            m_ref[sl] = jnp.broadcast_to(own, (tq, LANES))
    acc_ref[...] = jnp.zeros_like(acc_ref)
    if not bounded:
        m_ref[...] = jnp.full_like(m_ref, NEG_MASK)
    rows = qi * tq + lax.broadcasted_iota(jnp.int32, (tq, 1), 0)
    col0 = lax.broadcasted_iota(jnp.int32, (1, tk), 1)
    krow = lax.broadcasted_iota(jnp.int32, (tk, 1), 0)
    per = tq // tk
    nrep = tk // LANES
    flag0 = ((bi * NSA_KV_HEADS + g) * pl.num_programs(2) + qi) * SEL_TILE_STRIDE

    def scores(kt_true, masked):
        kt = jnp.maximum(kt_true, 0)
        off = pl.multiple_of(kt * tk, tk)
        k = k_ref[0, pl.ds(off, tk), :]
        if branch == 1:
            blk = (kt * tk + krow) >> 6
            onehot = jnp.where(lane == blk, 1.0, 0.0).astype(BF16)
            s_all = _dot_nt(qa_ref[...], jnp.concatenate([k, onehot], axis=1))
        else:
            s_all = _dot_nt(qa_ref[:, 0:LANES], k)
        rel = (pos_ref[kt] - pref) * LOG2E
        if masked:
            cols = kt_true * tk + col0
            d = rows - cols
            keep = d >= 0
            if branch == 2:
                keep = keep & (d < WIN) & (cols >= 0)
        out = []
        for hh in range(NSA_HPG):
            s = s_all[hh * tq:(hh + 1) * tq] + slope_ref[g * NSA_HPG + hh] * rel
            if masked:
                s = jnp.where(keep, s, NEG_MASK)
            out.append(s)
        return out

    def sweep(fn):
        if branch == 1:
            def body(j, carry):
                @pl.when(flag_ref[flag0 + j] != 0)
                def _():
                    fn(j, False)
                return carry
            lax.fori_loop(0, qi * per, body, 0)
            for dd in range(per):
                fn(qi * per + dd, True)
        else:
            for dd in range(per + WIN // tk):
                fn((qi + 1) * per - 1 - dd, True)

    def row_max(kt_true, masked):
        for hh, s in enumerate(scores(kt_true, masked)):
            sl = slice(hh * tq, (hh + 1) * tq)
            m_ref[sl] = jnp.maximum(m_ref[sl], _lane_max(s))

    def accumulate(kt_true, masked):
        off = pl.multiple_of(jnp.maximum(kt_true, 0) * tk, tk)
        v = v_ref[0, pl.ds(off, tk), :]
        for hh, s in enumerate(scores(kt_true, masked)):
            sl = slice(hh * tq, (hh + 1) * tq)
            p = jnp.exp2(s - jnp.concatenate([m_ref[sl]] * nrep, axis=1))
            acc_ref[sl] += _dot(p.astype(BF16), v)

    if not bounded:
        sweep(row_max)
        for hh in range(NSA_HPG):
            sl = slice(hh * tq, (hh + 1) * tq)
            m_ref[sl] = jnp.broadcast_to(jnp.max(m_ref[sl], axis=-1, keepdims=True), (tq, LANES))
    sweep(accumulate)

    gt = gt_ref[0]
    for pr in range(2):
        res = []
        for x in range(2):
            hh = 2 * pr + x
            a = acc_ref[hh * tq:(hh + 1) * tq]
            r = pltpu.roll(a, HEAD_DIM, 1)
            o = a / r if x == 0 else r / a
            res.append(o * gt[:, 3 * hh + branch:3 * hh + branch + 1])
        o_ref[0, :, pr * LANES:(pr + 1) * LANES] = _pair_select(lo_half, res[0], res[1])


def _nsa_gqa(flags, slopes, bound, safe, q, selb, k, v, pos2d, pos_col, gates, branch):
    b, s, _ = q.shape
    tq = min(TQ_NSA, s)
    tk = min(TK_NSA, tq)

    def call(bounded):
        kern = functools.partial(_gqa_kernel, tq=tq, tk=tk, branch=branch, bounded=bounded)
        smem = pl.BlockSpec(memory_space=pltpu.SMEM)
        name = ("nsa_sel_attn" if branch == 1 else "nsa_win_attn") + ("_bounded" if bounded else "")
        return pl.pallas_call(
            kern,
            grid=(b, NSA_KV_HEADS, s // tq),
            in_specs=[smem, smem, smem,
                      pl.BlockSpec((1, tq, 4 * LANES), lambda i, g, j: (i, j, g)),
                      pl.BlockSpec((1, 1, tq, LANES), lambda i, g, j: (i, g, j, 0)),
                      pl.BlockSpec((1, s, LANES), lambda i, g, j: (i, 0, g)),
                      pl.BlockSpec((1, s, LANES), lambda i, g, j: (i, 0, g)),
                      _full(pos2d.shape),
                      pl.BlockSpec((tq, 1), lambda i, g, j: (j, 0)),
                      pl.BlockSpec((1, tq, LANES), lambda i, g, j: (i, j, g))],
            out_specs=pl.BlockSpec((1, tq, 2 * LANES), lambda i, g, j: (i, j, g)),
            out_shape=jax.ShapeDtypeStruct((b, s, NSA_W), F32),
            scratch_shapes=[pltpu.VMEM((NSA_HPG * tq, 2 * LANES), BF16),
                            pltpu.VMEM((NSA_HPG * tq, LANES), F32),
                            pltpu.VMEM((NSA_HPG * tq, LANES), F32)],
            compiler_params=_cparams(3),
            name=name,
        )(flags, slopes, bound, q, selb, k, v, pos2d, pos_col, gates)

    return lax.cond(safe, lambda: call(True), lambda: call(False))


def _dil_kernel(slope_ref, bnd_ref, q_ref, k_ref, v_ref, pc_ref, pr_ref, o_ref, lse_ref,
                *, t, span, grp, bounded):
    i = pl.program_id(2)
    q4 = q_ref[0]
    lane = lax.broadcasted_iota(jnp.int32, (1, LANES), 1)
    lo_half = lane < HEAD_DIM
    per = t // LANES
    prev = jnp.maximum(i * per - 1, 0)
    o_prev = pl.multiple_of(prev * LANES, LANES)
    o_cur = pl.multiple_of(i * t, t)
    kk = jnp.concatenate([k_ref[0, pl.ds(o_prev, LANES), :], k_ref[0, pl.ds(o_cur, t), :]], axis=0)
    vv = jnp.concatenate([v_ref[0, pl.ds(o_prev, LANES), :], v_ref[0, pl.ds(o_cur, t), :]], axis=0)
    pk = jnp.concatenate([pr_ref[0, prev]] + [pr_ref[0, i * per + c] for c in range(per)], axis=1)
    dist = (pc_ref[0] - pk) * LOG2E
    rows = i * t + lax.broadcasted_iota(jnp.int32, (t, 1), 0)
    cols = i * t - LANES + lax.broadcasted_iota(jnp.int32, (1, t + LANES), 1)
    d = rows - cols
    valid = (cols >= 0) & (d >= 0) & (d <= span)
    zero = jnp.zeros((t, LANES), BF16)
    ones = jnp.ones((t + LANES, LANES), BF16)
    for p in range(2):
        qp = q4[:, p * LANES:(p + 1) * LANES]
        kp = kk[:, p * LANES:(p + 1) * LANES]
        vx = jnp.concatenate([vv[:, p * LANES:(p + 1) * LANES], ones], axis=1)
        res = []
        for x in range(2):
            qx = jnp.where(lo_half, qp, zero) if x == 0 else jnp.where(lo_half, zero, qp)
            s = _dot_nt(qx, kp) - slope_ref[grp * DIL_HEADS + 2 * p + x] * dist
            s = jnp.where(valid, s, -jnp.inf)
            mx = bnd_ref[0] if bounded else jnp.max(s, axis=-1, keepdims=True)
            acc = _dot(jnp.exp2(s - mx).astype(BF16), vx)
            den = acc[:, LANES:2 * LANES]
            res.append((acc[:, 0:LANES] / den, mx * LN2 + jnp.log(den)))
        sl = slice(p * LANES, (p + 1) * LANES)
        o_ref[0, :, sl] = _pair_select(lo_half, res[0][0], res[1][0])
        lse_ref[0, :, sl] = jnp.where(lo_half, res[0][1], res[1][1])


def _dilated(slopes, bound, safe, qv, kv, vv, pos_f, grp):
    window, dil = DIL_CFG[grp]
    b, sub, wd = qv.shape
    w = wd // dil
    t = min(T_DIL, sub)
    span = window // dil
    assert span <= LANES and t % LANES == 0
    pres = pos_f.reshape(sub, dil).T
    pc = pres.reshape(dil, sub, 1)
    pr = pres.reshape(dil, sub // LANES, 1, LANES)
    shp = jax.ShapeDtypeStruct((b, sub, wd), F32)
    smem = pl.BlockSpec(memory_space=pltpu.SMEM)

    def call(bounded):
        kern = functools.partial(_dil_kernel, t=t, span=span, grp=grp, bounded=bounded)
        return pl.pallas_call(
            kern,
            grid=(b, dil, sub // t),
            in_specs=[smem, smem,
                      pl.BlockSpec((1, t, w), lambda i, r, j: (i, j, r)),
                      pl.BlockSpec((1, sub, w), lambda i, r, j: (i, 0, r)),
                      pl.BlockSpec((1, sub, w), lambda i, r, j: (i, 0, r)),
                      pl.BlockSpec((1, t, 1), lambda i, r, j: (r, j, 0)),
                      pl.BlockSpec((1, sub // LANES, 1, LANES), lambda i, r, j: (r, 0, 0, 0))],
            out_specs=[pl.BlockSpec((1, t, w), lambda i, r, j: (i, j, r)),
                       pl.BlockSpec((1, t, w), lambda i, r, j: (i, j, r))],
            out_shape=[shp, shp],
            compiler_params=_cparams(3),
            name="dilated_attn_g%d%s" % (grp, "_bounded" if bounded else ""),
        )(slopes, bound, qv, kv, vv, pc, pr)

    return lax.cond(safe, lambda: call(True), lambda: call(False))


def _out_odd_kernel(x_ref, mod_ref, oc_ref, os_ref, ow_ref, d0_ref, d1_ref, d2_ref,
                    l0_ref, l1_ref, l2_ref, gz_ref, w_ref, o_ref, st):
    ts = x_ref.shape[1]
    gz = gz_ref[0].astype(F32)
    nsa = oc_ref[0] + os_ref[0] + ow_ref[0]
    m1 = (nsa * gz[:, 0:NSA_W]).astype(BF16)

    def token_order(ref, g, slot):
        dil = DIL_CFG[g][1]
        if dil == 1:
            return ref[0]
        for r in range(dil):
            for h in range(2):
                st[slot, h, pl.ds(r, ts // dil, stride=dil), :] = ref[0, :, r * DIL_W + h * 128:r * DIL_W + (h + 1) * 128]
        return jnp.concatenate([st[slot, 0], st[slot, 1]], axis=1)

    d0, d1, d2 = token_order(d0_ref, 0, 0), token_order(d1_ref, 1, 0), token_order(d2_ref, 2, 1)
    l0, l1, l2 = token_order(l0_ref, 0, 2), token_order(l1_ref, 1, 2), token_order(l2_ref, 2, 3)
    mx = jnp.maximum(jnp.maximum(l0, l1), l2)
    e0, e1, e2 = jnp.exp(l0 - mx), jnp.exp(l1 - mx), jnp.exp(l2 - mx)
    dil = (e0 * d0 + e1 * d1 + e2 * d2) / (e0 + e1 + e2)
    m2 = (dil * gz[:, NSA_W:NSA_W + DIL_W]).astype(BF16)
    y = _dot(m1, w_ref[0:NSA_W, :]) + _dot(m2, w_ref[NSA_W:NSA_W + DIL_W, :])
    gate = mod_ref[0][:, 2 * D_MODEL:3 * D_MODEL]
    o_ref[0] = x_ref[0] + gate * y


def _out_odd(x, mod3, oc, os_, ow, dils, lses, gz, w):
    b, s, d = x.shape
    ts = min(TS_PROJ, s)
    row = lambda n: pl.BlockSpec((1, ts, n), lambda i, j: (i, j, 0))
    res = [pl.BlockSpec((1, ts // DIL_CFG[g][1], DIL_CFG[g][1] * DIL_W), lambda i, j: (i, j, 0))
           for g in range(N_DIL)]
    return pl.pallas_call(
        _out_odd_kernel,
        grid=(b, s // ts),
        in_specs=[row(d), pl.BlockSpec((1, 1, 3 * d), lambda i, j: (i, 0, 0)),
                  row(NSA_W), row(NSA_W), row(NSA_W)] + res + res + [row(1024), _full(w.shape)],
        out_specs=row(d),
        out_shape=jax.ShapeDtypeStruct((b, s, d), F32),
        scratch_shapes=[pltpu.VMEM((4, 2, ts, 128), F32)],
        compiler_params=_cparams(2),
        name="outproj_odd",
    )(x, mod3, oc, os_, ow, *dils, *lses, gz, w)


def _pad_cols(w, n):
    return jnp.pad(w, ((0, 0), (0, n - w.shape[1])))


def _pad_vec(v, n=D_MODEL):
    return jnp.pad(v, (0, n - v.shape[0]))


def _group_matrix(sizes, total):
    m = np.zeros((total, total), np.float32)
    off = 0
    for sz, on in sizes:
        if on:
            m[off:off + sz, off:off + sz] = 1.0
        off += sz
    return jnp.asarray(m, BF16)


def _swap_halves(w):
    h = w.shape[-1] // 2
    return jnp.concatenate([w[..., h:], w[..., :h]], axis=-1)


def _column_blocks(segments):
    blocks, cur, room = [], [], LANES
    for src, width in segments:
        while width:
            n = min(width, room)
            cur.append((src, n))
            src = None if src is None else src + n
            width -= n
            room -= n
            if room == 0:
                blocks.append(cur)
                cur, room = [], LANES
    assert not cur
    return blocks


def _pack_cols_kernel(w_ref, o_ref, *, blocks):
    _, rows, n_src = w_ref.shape
    for bi, pieces in enumerate(blocks):
        parts = []
        for src, width in pieces:
            if src is None:
                parts.append(jnp.zeros((rows, width), F32))
                continue
            a0 = src // LANES * LANES
            a1 = min(-(-(src + width) // LANES) * LANES, n_src)
            parts.append(w_ref[0, :, a0:a1][:, src - a0:src - a0 + width])
        blk = parts[0] if len(parts) == 1 else jnp.concatenate(parts, axis=1)
        o_ref[:, bi * LANES:(bi + 1) * LANES] = blk.astype(BF16)


def _pack_cols(w, layer, segments):
    blocks = _column_blocks(segments)
    _, d, n_src = w.shape
    tr = 256
    return pl.pallas_call(
        functools.partial(_pack_cols_kernel, blocks=blocks),
        grid=(d // tr,),
        in_specs=[pl.BlockSpec((1, tr, n_src), lambda i: (layer, i, 0))],
        out_specs=pl.BlockSpec((tr, len(blocks) * LANES), lambda i: (i, 0)),
        out_shape=jax.ShapeDtypeStruct((d, len(blocks) * LANES), BF16),
        compiler_params=_cparams(1),
        name="pack_weight_columns",
    )(w)


EVEN_SEGMENTS = ((0, 1536), (1536, 512), (2464, 512), (2048, 256), (2304, 128),
                 (None, 64), (2432, 32), (None, 32), (None, 64), (2448, 16), (2432, 16), (None, 32))


def _pack_even(w_in_all, layer, norm_g, sb_qn, sb_kn, qa_g, wq_up, kva_g, wkv_up, qn, kn):
    w = _pack_cols(w_in_all, layer, EVEN_SEGMENTS)
    wq3 = wq_up.reshape(MLA_Q_RANK, MLA_HEADS, MLA_NOPE + MLA_ROPE)
    zq = jnp.zeros((MLA_Q_RANK, MLA_HEADS, 32), wq_up.dtype)
    wq = jnp.concatenate([wq3, zq], axis=-1).reshape(MLA_Q_RANK, MLA_HEADS * 128).astype(BF16)
    wqs = jnp.concatenate([jnp.zeros((MLA_Q_RANK, MLA_HEADS, 64), wq_up.dtype),
                           _swap_halves(wq3[..., MLA_NOPE:]), zq], axis=-1)
    wqs = wqs.reshape(MLA_Q_RANK, MLA_HEADS * 128).astype(BF16)
    wkv3 = wkv_up.reshape(MLA_KV_RANK, MLA_HEADS, MLA_NOPE + MLA_V)
    wk = jnp.concatenate([wkv3[..., :MLA_NOPE], jnp.zeros_like(wkv3[..., :MLA_NOPE])], axis=-1)
    wkv = jnp.concatenate([wk.reshape(MLA_KV_RANK, MLA_HEADS * 128),
                           wkv3[..., MLA_NOPE:].reshape(MLA_KV_RANK, MLA_HEADS * MLA_V)], axis=1).astype(BF16)
    z32 = jnp.zeros((32,), F32)
    z64 = jnp.zeros((64,), F32)
    scale = (MLA_NOPE + MLA_ROPE) ** -0.5 * LOG2E
    qg = jnp.tile(jnp.concatenate([qn, z32]), MLA_HEADS) * scale
    qgs = jnp.tile(jnp.concatenate([z64, _swap_halves(qn[MLA_NOPE:]), z32]), MLA_HEADS) * scale
    kg = jnp.tile(jnp.concatenate([kn[:MLA_NOPE], z64]), MLA_HEADS)
    krg = jnp.concatenate([z64, kn[MLA_NOPE:], z32])
    krgs = jnp.concatenate([z64, _swap_halves(kn[MLA_NOPE:]), z32])
    cnt = jnp.tile(jnp.concatenate([jnp.full((64,), 1.0 / 64), jnp.full((32,), 1.0 / 32), jnp.ones((32,))]),
                   MLA_HEADS).astype(F32)
    rows = [norm_g, _pad_vec(jnp.tile(sb_qn, SB_HEADS) * (LOG2E * HEAD_DIM ** -0.5)), _pad_vec(jnp.tile(sb_kn, SB_HEADS)),
            _pad_vec(qa_g), _pad_vec(kva_g), qg, qgs, kg, _pad_vec(krg), _pad_vec(krgs), cnt]
    rows += [jnp.zeros((D_MODEL,), F32)] * (16 - len(rows))
    return w, wq, wqs, wkv, jnp.stack(rows).astype(F32)


def _padded_heads(start, n):
    return tuple(seg for h in range(n) for seg in ((start + h * HEAD_DIM, HEAD_DIM), (None, HEAD_DIM)))


NSA_SEGMENTS = (_padded_heads(0, NSA_HEADS)
                + tuple(seg for c in range(6) for seg in _padded_heads(768 + c * NSA_KV_W, NSA_KV_HEADS))
                + tuple(seg for g in range(NSA_KV_HEADS)
                        for seg in ((1920 + g * NSA_HPG * 3, NSA_HPG * 3), (None, LANES - NSA_HPG * 3)))
                + ((1956, NSA_W), (5028, DIL_W)))
DIL_SEGMENTS = ((2724, 3 * N_DIL * DIL_W),)


def _pack_odd(w_in_all, layer, norm_g, nsa_qn, nsa_kn, dil_qn, dil_kn):
    w_nsa = _pack_cols(w_in_all, layer, NSA_SEGMENTS)
    w_dil = _pack_cols(w_in_all, layer, DIL_SEGMENTS)
    z64 = jnp.zeros((64,), F32)
    n = NSA_HEADS * 128
    rows = [_pad_vec(norm_g, n),
            jnp.tile(jnp.concatenate([nsa_qn * (LOG2E * HEAD_DIM ** -0.5), z64]), NSA_HEADS),
            _pad_vec(jnp.tile(jnp.concatenate([nsa_kn, z64]), NSA_KV_HEADS), n),
            _pad_vec(jnp.tile(dil_qn, DIL_HEADS) * (LOG2E * HEAD_DIM ** -0.5), n),
            _pad_vec(jnp.tile(dil_kn, DIL_HEADS), n)]
    rows += [jnp.zeros((n,), F32)] * (8 - len(rows))
    return w_nsa, w_dil, jnp.stack(rows).astype(F32)


def _pack_compress(w, pe, double):
    w3 = w.reshape(CMP_LEN, HEAD_DIM, HEAD_DIM)
    w3 = jnp.concatenate([w3, jnp.zeros_like(w3)], axis=1)
    w3 = jnp.concatenate([w3, w3 if double else jnp.zeros_like(w3)], axis=2)
    wp = w3.reshape(2, CMP_STRIDE * 128, 128).astype(BF16)
    pe2 = jnp.concatenate([pe, jnp.zeros_like(pe)], axis=1).reshape(2, 1, CMP_STRIDE * 128)
    pe2 = jnp.broadcast_to(pe2, (2, 8, CMP_STRIDE * 128)).astype(F32)
    return wp, pe2


def _normed_len(gain, sizes):
    tot, off = 0.0, 0
    for n in sizes:
        tot = tot + n * jnp.max(jnp.square(gain[off:off + n]))
        off += n
    return jnp.sqrt(tot)


def _mla_logit_bound(qn, kn):
    sizes = (MLA_NOPE, MLA_ROPE)
    scale = (MLA_NOPE + MLA_ROPE) ** -0.5 * LOG2E
    return (_normed_len(qn, sizes) * _normed_len(kn, sizes) * (scale * ROUNDING_MARGIN)).reshape(1).astype(F32)


def _alibi_slopes(n):
    return 2.0 ** (-8.0 * jnp.arange(1, n + 1, dtype=jnp.float32) / n)


def kernel(x, c, positions, ada_w, ada_b, norm_g, ev_w_in, ev_w_out, sb_qn, sb_kn, mla_qa_g, mla_wq_up,
           mla_kva_g, mla_wkv_up, mla_qn, mla_kn, od_w_in, od_w_out, nsa_qn, nsa_kn, nsa_cmp_wk, nsa_cmp_wv,
           nsa_cmp_pe_k, nsa_cmp_pe_v, dil_qn, dil_kn):
    b, s, d = x.shape
    depth = ada_w.shape[0]
    pos_f = positions.astype(F32)

    inv_freq = ROPE_BASE ** (-jnp.arange(0, MLA_ROPE, 2, dtype=F32) / MLA_ROPE)
    ang = pos_f[:, None] * inv_freq[None, :]
    cos, sin = jnp.cos(ang), jnp.sin(ang)
    cos128 = jnp.concatenate([jnp.ones((s, 64), F32), cos, cos, jnp.zeros((s, 32), F32)], axis=1)
    sin128 = jnp.concatenate([jnp.zeros((s, 64), F32), -sin, sin, jnp.zeros((s, 32), F32)], axis=1)
    nsa_slopes = _alibi_slopes(NSA_HEADS)
    dil_slopes = _alibi_slopes(N_DIL * DIL_HEADS)
    nch = s // CMP_STRIDE
    chunk_sum = pos_f.reshape(nch, CMP_STRIDE).sum(axis=1)
    cpos = ((chunk_sum + jnp.roll(chunk_sum, -1)) / CMP_LEN).reshape(nch, 1)
    n_sel = s // SEL_LEN
    cst = np.arange(nch)[:, None] * CMP_STRIDE
    jst = np.arange(LANES)[None, :] * SEL_LEN
    ovl = ((cst <= jst + SEL_LEN - 1) & (cst + CMP_LEN - 1 >= jst) & (np.arange(LANES)[None, :] < n_sel))
    ovlt = jnp.asarray(ovl.astype(np.float32).T, BF16)
    pos_col = pos_f.reshape(s, 1)
    tqc = min(TQ_CMP, s)
    pos_rows = pos_f.reshape(s // tqc, 1, tqc)
    tqn = min(TQ_NSA, s)
    tkn = min(TK_NSA, tqn)
    assert tkn == TK_NSA
    pos2d = pos_f.reshape(s // tkn, 1, tkn)
    pos_sorted = jnp.all(pos_f[1:] >= pos_f[:-1])

    g64 = _group_matrix([(64, 1)] * 4, 256)
    gm = _group_matrix([(64, 1), (32, 1), (32, 0)] * 2, 256)
    tks = min(TK_SB, min(TQ_SB, s))
    tri = jnp.asarray(np.tril(np.ones((tks, tks), np.float32)), BF16)

    c8 = jnp.pad(c, ((0, 8 - b), (0, 0)))
    mod_all = _modulation(c8, ada_w, ada_b)

    for layer in range(depth):
        j = layer // 2
        mod3 = mod_all[layer, :b].reshape(b, 1, 3 * d)
        if layer % 2 == 0:
            w, wq, wqs, wkv, vec = _pack_even(ev_w_in, j, norm_g[layer], sb_qn[j], sb_kn[j], mla_qa_g[j],
                                              mla_wq_up[j], mla_kva_g[j], mla_wkv_up[j], mla_qn[j], mla_kn[j])
            sbq, sbk, sbv, gz, mq, mk, mv = _k1_even(x, mod3, vec, w, wq, wqs, wkv, g64, gm, cos128, sin128)
            sb_bound = (_normed_len(sb_qn[j], (HEAD_DIM,)) * _normed_len(sb_kn[j], (HEAD_DIM,))
                        * (HEAD_DIM ** -0.5 * LOG2E * ROUNDING_MARGIN))
            o_sb = _sb_attention(sbq, sbk, sbv, tri, sb_bound <= SAFE_LOGIT_BOUND)
            o_mla = _mla_attention(mq, mk, mv, _mla_logit_bound(mla_qn[j], mla_kn[j]))
            x = _out_even(x, mod3, o_sb, o_mla, gz, ev_w_out[j].astype(BF16))
        else:
            w_nsa, w_dil, vec = _pack_odd(od_w_in, j, norm_g[layer], nsa_qn[j], nsa_kn[j], dil_qn[j], dil_kn[j])
            q, ck, cv, sk, sv, wk, wv, gates, gz = _k1_nsa(x, mod3, vec, w_nsa, g64)
            dqkv = _k1_dil(x, mod3, vec, w_dil, g64)
            wck, pek = _pack_compress(nsa_cmp_wk[j], nsa_cmp_pe_k[j], False)
            wcv, pev = _pack_compress(nsa_cmp_wv[j], nsa_cmp_pe_v[j], True)
            kgain = jnp.concatenate([nsa_kn[j], jnp.zeros((64,), F32)]).reshape(1, 128)
            kc, vc = _compress(ck, cv, wck, wcv, pek, pev, kgain, g64[0:128, 0:128])
            o_c, selb, used = _nsa_cmp(nsa_slopes, q, kc, vc, ovlt, cpos, pos_rows, gates)
            flags = used[:, :, :, 0, :].reshape(b, NSA_KV_HEADS, s // tqn, tqn // min(TQ_CMP, s),
                                                SEL_TILE_STRIDE, LANES // SEL_TILE_STRIDE).max(axis=(3, 5))
            flags = (flags > 0).astype(jnp.int32).reshape(-1)
            nsa_bound = (_normed_len(nsa_qn[j], (HEAD_DIM,)) * _normed_len(nsa_kn[j], (HEAD_DIM,))
                         * (HEAD_DIM ** -0.5 * LOG2E * ROUNDING_MARGIN)).reshape(1).astype(F32)
            safe = (nsa_bound[0] <= SAFE_LOGIT_BOUND) & pos_sorted
            o_s = _nsa_gqa(flags, nsa_slopes, nsa_bound, safe, q, selb, sk, sv, pos2d, pos_col, gates, 1)
            o_w = _nsa_gqa(flags, nsa_slopes, nsa_bound, safe, q, selb, wk, wv, pos2d, pos_col, gates, 2)
            dil_bound = (_normed_len(dil_qn[j], (HEAD_DIM,)) * _normed_len(dil_kn[j], (HEAD_DIM,))
                         * (HEAD_DIM ** -0.5 * LOG2E * ROUNDING_MARGIN)).reshape(1).astype(F32)
            dil_safe = (dil_bound[0] <= SAFE_LOGIT_BOUND) & pos_sorted
            dils, lses = [], []
            for g in range(N_DIL):
                o, lse = _dilated(dil_slopes, dil_bound, dil_safe, dqkv[3 * g], dqkv[3 * g + 1], dqkv[3 * g + 2],
                                  pos_f, g)
                dils.append(o)
                lses.append(lse)
            x = _out_odd(x, mod3, o_c, o_s, o_w, dils, lses, gz, od_w_out[j].astype(BF16))
    return x
```

```python
import functools

import numpy as np
import jax
import jax.numpy as jnp
from jax import lax
from jax.experimental import pallas as pl
from jax.experimental.pallas import tpu as pltpu

F32 = jnp.float32
BF16 = jnp.bfloat16

D_MODEL = 1024
HEAD_DIM = 64
NORM_EPS = 1e-6
TINY = 1e-30
SB_HEADS = 8
MLA_HEADS = 8
MLA_Q_RANK = 256
MLA_KV_RANK = 128
MLA_NOPE = 64
MLA_ROPE = 32
MLA_V = 64
ROPE_BASE = 10000.0
NSA_HEADS = 12
NSA_KV_HEADS = 3
NSA_HPG = 4
CMP_LEN = 32
CMP_STRIDE = 16
SEL_LEN = 64
SEL_TOPN = 16
WIN = 512
FORCE_BONUS = 1e3
DIL_CFG = ((128, 1), (512, 4), (2048, 16))
N_DIL = 3
DIL_HEADS = 4
SB_W = SB_HEADS * HEAD_DIM
MLA_OUT = MLA_HEADS * MLA_V
NSA_W = NSA_HEADS * HEAD_DIM
NSA_KV_W = NSA_KV_HEADS * HEAD_DIM
DIL_W = DIL_HEADS * HEAD_DIM

LANES = 128
MXU_DIM = 256
NEG_MASK = -1e30
LOG2E = 1.4426950408889634
LN2 = 0.6931471805599453
F32_ZERO_EXP = 150.0
SAFE_LOGIT_BOUND = 50.0
ROUNDING_MARGIN = 1.02
SEL_OFF = 2.0 ** 30
VMEM_LIMIT = 56 * 1024 * 1024

TS_PROJ = 512
TQ_SB = 1024
TK_SB = 256
TQ_MLA = 1024
TD_MLA = 512
TQ_CMP = 512
TQ_NSA = 512
TK_NSA = 512
SEL_TILE_STRIDE = LANES * SEL_LEN // TK_NSA
T_DIL = 256


def _dot(a, b):
    return jnp.dot(a, b, preferred_element_type=F32)


def _dot_nt(a, b):
    return lax.dot_general(a, b, (((1,), (1,)), ((), ())), preferred_element_type=F32)


def _split_hl(a):
    hi = a.astype(BF16)
    lo = (a - hi.astype(F32)).astype(BF16)
    return hi, lo


def _dot_hl(a, b):
    hi, lo = _split_hl(a)
    return _dot(hi, b) + _dot(lo, b)


def _sigmoid(z):
    return 1.0 / (1.0 + jnp.exp(-z))


def _cparams(n_axes):
    return pltpu.CompilerParams(dimension_semantics=("arbitrary",) * n_axes,
                                vmem_limit_bytes=VMEM_LIMIT)


def _full(shape):
    n = len(shape)
    return pl.BlockSpec(shape, lambda *a, _n=n: (0,) * _n)


def _mod_kernel(c_ref, w_ref, b_ref, o_ref):
    c = c_ref[...]
    a = c * _sigmoid(c)
    ah, al = _split_hl(a)
    wh, wl = _split_hl(w_ref[0])
    o_ref[0] = _dot(ah, wh) + _dot(ah, wl) + _dot(al, wh) + b_ref[0]


def _modulation(c8, ada_w, ada_b):
    depth, d, n3 = ada_w.shape
    tn = 1024
    return pl.pallas_call(
        _mod_kernel,
        grid=(depth, n3 // tn),
        in_specs=[pl.BlockSpec((8, d), lambda l, j: (0, 0)),
                  pl.BlockSpec((1, d, tn), lambda l, j: (l, 0, j)),
                  pl.BlockSpec((1, 1, tn), lambda l, j: (l, 0, j))],
        out_specs=pl.BlockSpec((1, 8, tn), lambda l, j: (l, 0, j)),
        out_shape=jax.ShapeDtypeStruct((depth, 8, n3), F32),
        compiler_params=_cparams(2),
        name="adaln_mod",
    )(c8, ada_w, ada_b.reshape(depth, 1, n3))


def _modulated(x_ref, mod_ref, ng):
    x = x_ref[0]
    mod = mod_ref[0]
    shift = mod[:, 0:D_MODEL]
    scale = mod[:, D_MODEL:2 * D_MODEL]
    ms = jnp.mean(x * x, axis=-1, keepdims=True)
    h = x * lax.rsqrt(ms + NORM_EPS) * ng
    h = h * (1.0 + scale) + shift
    return h.astype(BF16)


def _group_norm_chunk(t, g, inv_cnt):
    ss = _dot_hl(t * t, g)
    return t * lax.rsqrt(ss * inv_cnt + NORM_EPS)


def _row_rms(t, gain):
    ms = jnp.mean(t * t, axis=-1, keepdims=True)
    return t * lax.rsqrt(ms + NORM_EPS) * gain


EV_COLS = 3200


def _k1_even_kernel(x_ref, mod_ref, vec_ref, w_ref, wq_ref, wqs_ref, wkv_ref, g64_ref, gm_ref,
                    cos_ref, sin_ref, sbq_o, sbk_o, sbv_o, gz_o, mq_o, mk_o, mv_o):
    hb = _modulated(x_ref, mod_ref, vec_ref[0:1, :])
    g64 = g64_ref[...]
    gm = gm_ref[...]
    inv64 = 1.0 / HEAD_DIM

    u = _dot(hb, w_ref[:, 0:1536])
    for c in range(2):
        sl = slice(c * 256, (c + 1) * 256)
        qn = _group_norm_chunk(u[:, c * 256:(c + 1) * 256], g64, inv64)
        sbq_o[0, :, sl] = (qn * vec_ref[1:2, sl]).astype(BF16)
        kn = _group_norm_chunk(u[:, 512 + c * 256:512 + (c + 1) * 256], g64, inv64)
        sbk_o[0, :, sl] = (kn * vec_ref[2:3, sl]).astype(BF16)
    sbv_o[0] = u[:, 1024:1536].astype(BF16)

    uz = _dot(hb, w_ref[:, 1536:2560])
    gz_o[0] = (uz * _sigmoid(uz)).astype(BF16)

    ul = _dot(hb, w_ref[:, 2560:3200])
    qlat = _row_rms(ul[:, 0:256], vec_ref[3:4, 0:256]).astype(BF16)
    kvlat = _row_rms(ul[:, 256:384], vec_ref[4:5, 0:128]).astype(BF16)
    krm = ul[:, 384:512]
    krs = ul[:, 512:640]
    cs = cos_ref[...]
    sn = sin_ref[...]
    cs2 = jnp.concatenate([cs, cs], axis=1)
    sn2 = jnp.concatenate([sn, sn], axis=1)

    tq = _dot(qlat, wq_ref[...])
    tqs = _dot(qlat, wqs_ref[...])
    for p in range(4):
        sl = slice(p * 256, (p + 1) * 256)
        tc = tq[:, p * 256:(p + 1) * 256]
        ss = _dot_hl(tc * tc, gm)
        inv = lax.rsqrt(ss * vec_ref[10:11, sl] + NORM_EPS)
        a = tc * inv * vec_ref[5:6, sl]
        b = tqs[:, p * 256:(p + 1) * 256] * inv * vec_ref[6:7, sl]
        mq_o[0, :, sl] = (a * cs2 + b * sn2).astype(BF16)

    sskr = _dot_hl(krm * krm, gm[0:128, 0:128])
    invr = lax.rsqrt(sskr * vec_ref[10:11, 0:128] + NORM_EPS)
    kr = (krm * invr * vec_ref[8:9, 0:128]) * cs + (krs * invr * vec_ref[9:10, 0:128]) * sn
    kr2 = jnp.concatenate([kr, kr], axis=1)

    kv = _dot(kvlat, wkv_ref[...])
    for p in range(4):
        sl = slice(p * 256, (p + 1) * 256)
        kn = _group_norm_chunk(kv[:, p * 256:(p + 1) * 256], g64, inv64)
        mk_o[0, :, sl] = (kn * vec_ref[7:8, sl] + kr2).astype(BF16)
    mv_o[0] = kv[:, 1024:1536].astype(BF16)


def _k1_even(x, mod3, vec, w, wq, wqs, wkv, g64, gm, cos128, sin128):
    b, s, d = x.shape
    ts = min(TS_PROJ, s)
    row = lambda n: pl.BlockSpec((1, ts, n), lambda i, j: (i, j, 0))
    outs = [(512, BF16), (512, BF16), (512, BF16), (1024, BF16), (1024, BF16), (1024, BF16), (512, BF16)]
    return pl.pallas_call(
        _k1_even_kernel,
        grid=(b, s // ts),
        in_specs=[row(d),
                  pl.BlockSpec((1, 1, 3 * d), lambda i, j: (i, 0, 0)),
                  _full(vec.shape), _full(w.shape), _full(wq.shape), _full(wqs.shape), _full(wkv.shape),
                  _full(g64.shape), _full(gm.shape),
                  pl.BlockSpec((ts, 128), lambda i, j: (j, 0)),
                  pl.BlockSpec((ts, 128), lambda i, j: (j, 0))],
        out_specs=[row(n) for n, _ in outs],
        out_shape=[jax.ShapeDtypeStruct((b, s, n), dt) for n, dt in outs],
        compiler_params=_cparams(2),
        name="inproj_even",
    )(x, mod3, vec, w, wq, wqs, wkv, g64, gm, cos128, sin128)


def _sb_kernel(bnd_ref, q_ref, k_ref, v_ref, tri_ref, o_ref, acc_ref, car_ref, *, tq, tk, bounded):
    qi = pl.program_id(2)
    q = q_ref[0]
    lane = lax.broadcasted_iota(jnp.int32, (1, LANES), 1)
    lo_half = lane < HEAD_DIM
    zero = jnp.zeros_like(q)
    qs = (jnp.where(lo_half, q, zero), jnp.where(lo_half, zero, q))
    tri = tri_ref[...]
    acc_ref[...] = jnp.zeros_like(acc_ref)
    car_ref[...] = jnp.zeros_like(car_ref)
    col0 = lax.broadcasted_iota(jnp.int32, (1, tk), 1)

    def tile(kt, r0, masked):
        n = tq - r0
        off = pl.multiple_of(kt * tk, tk)
        k = k_ref[0, pl.ds(off, tk), :]
        v = v_ref[0, pl.ds(off, tk), :]
        if masked:
            rows = qi * tq + r0 + lax.broadcasted_iota(jnp.int32, (n, 1), 0)
            strict = (kt * tk + col0) < rows
        for i in range(2):
            z = _dot_nt(qs[i][r0:tq], k)
            if bounded:
                lom = jnp.log(1.0 + jnp.exp2(z)) * (-LOG2E)
            else:
                lom = -jnp.maximum(z, 0.0) - jnp.log(1.0 + jnp.exp2(-jnp.abs(z))) * LOG2E
            if masked:
                lom = jnp.where(strict, lom, 0.0)
            tt = _dot(lom.astype(BF16), tri)
            car = car_ref[i, r0:tq]
            w = jnp.exp2(tt + z + jnp.concatenate([car] * (tk // LANES), axis=1))
            if masked:
                w = jnp.where(strict, w, 0.0)
            acc_ref[i, r0:tq] += _dot(w.astype(BF16), v)
            car_ref[i, r0:tq] = car + jnp.broadcast_to(tt[:, 0:1], (n, LANES))

    per = tq // tk
    for c in reversed(range(per)):
        tile(qi * per + c, c * tk, True)

    if bounded:
        limit = -(F32_ZERO_EXP + bnd_ref[0])

        def still_live():
            return jnp.max(jnp.maximum(car_ref[0], car_ref[1])) > limit

        def single(c):
            j, _ = c
            tile(qi * per - 1 - j, 0, False)
            return j + 1, still_live()

        _, live = lax.while_loop(lambda c: (c[0] < per) & c[1], single, (0, qi > 0))

        def group(c):
            g, _ = c
            for u in range(per):
                tile(qi * per - 1 - per * (g + 1) - u, 0, False)
            return g + 1, still_live()

        lax.while_loop(lambda c: (c[0] < qi - 1) & c[1], group, (0, live))
    else:
        def body(j, carry):
            for u in range(per):
                tile(qi * per - 1 - per * j - u, 0, False)
            return carry

        lax.fori_loop(0, qi, body, 0)
    o_ref[0] = jnp.where(lo_half, acc_ref[0], acc_ref[1])


def _sb_attention(q, k, v, tri, bound, safe):
    b, s, w = q.shape
    tq = min(TQ_SB, s)
    tk = min(TK_SB, tq)

    def call(bounded):
        kern = functools.partial(_sb_kernel, tq=tq, tk=tk, bounded=bounded)
        return pl.pallas_call(
            kern,
            grid=(b, w // LANES, s // tq),
            in_specs=[pl.BlockSpec(memory_space=pltpu.SMEM),
                      pl.BlockSpec((1, tq, LANES), lambda i, p, j: (i, j, p)),
                      pl.BlockSpec((1, s, LANES), lambda i, p, j: (i, 0, p)),
                      pl.BlockSpec((1, s, LANES), lambda i, p, j: (i, 0, p)),
                      _full(tri.shape)],
            out_specs=pl.BlockSpec((1, tq, LANES), lambda i, p, j: (i, j, p)),
            out_shape=jax.ShapeDtypeStruct((b, s, w), F32),
            scratch_shapes=[pltpu.VMEM((2, tq, LANES), F32), pltpu.VMEM((2, tq, LANES), F32)],
            compiler_params=_cparams(3),
            name="stickbreak_attn_bounded" if bounded else "stickbreak_attn",
        )(bound, q, k, v, tri)

    return lax.cond(safe, lambda: call(True), lambda: call(False))


def _lane_max(s):
    m = s[:, 0:LANES]
    for c in range(1, s.shape[1] // LANES):
        m = jnp.maximum(m, s[:, c * LANES:(c + 1) * LANES])
    return m


def _mla_kernel(bnd_ref, q_ref, k_ref, v_ref, o_ref, acc_ref, m_ref, *, tq, td, bounded):
    qi = pl.program_id(2)
    q = q_ref[0]
    qs = (q[:, 0:LANES], q[:, LANES:2 * LANES])
    lane = lax.broadcasted_iota(jnp.int32, (1, LANES), 1)
    lo_half = lane < HEAD_DIM
    acc_ref[...] = jnp.zeros_like(acc_ref)
    if bounded:
        m_ref[...] = jnp.full(m_ref.shape, bnd_ref[0], F32)
    else:
        m_ref[...] = jnp.full_like(m_ref, NEG_MASK)

    def scores(off, width, r0, masked):
        k = k_ref[0, pl.ds(off, width), :]
        if masked:
            rows = qi * tq + r0 + lax.broadcasted_iota(jnp.int32, (tq - r0, 1), 0)
            keep = (off + lax.broadcasted_iota(jnp.int32, (1, width), 1)) <= rows
        out = []
        for i in range(2):
            s = _dot_nt(qs[i][r0:tq], k[:, i * LANES:(i + 1) * LANES])
            if masked:
                s = jnp.where(keep, s, NEG_MASK)
            out.append(s)
        return out

    def sweep(fn):
        def body(j, carry):
            fn(pl.multiple_of(j * tq, tq), tq, 0, False)
            return carry
        lax.fori_loop(0, qi, body, 0)
        for c in range(tq // td):
            fn(pl.multiple_of(qi * tq + c * td, td), td, c * td, True)

    def row_max(off, width, r0, masked):
        for i, s in enumerate(scores(off, width, r0, masked)):
            m_ref[i, r0:tq] = jnp.maximum(m_ref[i, r0:tq], _lane_max(s))

    def accumulate(off, width, r0, masked):
        vx = jnp.concatenate([v_ref[0, pl.ds(off, width), :], jnp.ones((width, LANES), BF16)], axis=1)
        for i, s in enumerate(scores(off, width, r0, masked)):
            p = jnp.exp2(s - jnp.concatenate([m_ref[i, r0:tq]] * (width // LANES), axis=1))
            acc_ref[i, r0:tq] += _dot(p.astype(BF16), vx)

    if not bounded:
        sweep(row_max)
        for i in range(2):
            m_ref[i] = jnp.broadcast_to(jnp.max(m_ref[i], axis=-1, keepdims=True), (tq, LANES))
    sweep(accumulate)
    o_ref[0] = jnp.where(lo_half, acc_ref[0, :, 0:LANES] / acc_ref[0, :, LANES:2 * LANES],
                         acc_ref[1, :, 0:LANES] / acc_ref[1, :, LANES:2 * LANES])


def _mla_attention(q, k, v, bound):
    b, s, _ = q.shape
    tq = min(TQ_MLA, s)
    td = min(TD_MLA, tq)

    def call(bounded):
        kern = functools.partial(_mla_kernel, tq=tq, td=td, bounded=bounded)
        return pl.pallas_call(
            kern,
            grid=(b, MLA_HEADS // 2, s // tq),
            in_specs=[pl.BlockSpec(memory_space=pltpu.SMEM),
                      pl.BlockSpec((1, tq, 2 * LANES), lambda i, p, j: (i, j, p)),
                      pl.BlockSpec((1, s, 2 * LANES), lambda i, p, j: (i, 0, p)),
                      pl.BlockSpec((1, s, LANES), lambda i, p, j: (i, 0, p))],
            out_specs=pl.BlockSpec((1, tq, LANES), lambda i, p, j: (i, j, p)),
            out_shape=jax.ShapeDtypeStruct((b, s, MLA_OUT), F32),
            scratch_shapes=[pltpu.VMEM((2, tq, 2 * LANES), F32), pltpu.VMEM((2, tq, LANES), F32)],
            compiler_params=_cparams(3),
            name="mla_attn_bounded" if bounded else "mla_attn",
        )(bound, q, k, v)

    return lax.cond(bound[0] <= SAFE_LOGIT_BOUND, lambda: call(True), lambda: call(False))


def _out_even_kernel(x_ref, mod_ref, osb_ref, omla_ref, gz_ref, w_ref, o_ref):
    gz = gz_ref[0].astype(F32)
    m1 = (osb_ref[0] * gz[:, 0:SB_W]).astype(BF16)
    m2 = (omla_ref[0] * gz[:, SB_W:SB_W + MLA_OUT]).astype(BF16)
    y = _dot(m1, w_ref[0:SB_W, :]) + _dot(m2, w_ref[SB_W:SB_W + MLA_OUT, :])
    gate = mod_ref[0][:, 2 * D_MODEL:3 * D_MODEL]
    o_ref[0] = x_ref[0] + gate * y


def _out_even(x, mod3, osb, omla, gz, w):
    b, s, d = x.shape
    ts = min(TS_PROJ, s)
    row = lambda n: pl.BlockSpec((1, ts, n), lambda i, j: (i, j, 0))
    return pl.pallas_call(
        _out_even_kernel,
        grid=(b, s // ts),
        in_specs=[row(d), pl.BlockSpec((1, 1, 3 * d), lambda i, j: (i, 0, 0)),
                  row(SB_W), row(MLA_OUT), row(SB_W + MLA_OUT), _full(w.shape)],
        out_specs=row(d),
        out_shape=jax.ShapeDtypeStruct((b, s, d), F32),
        compiler_params=_cparams(2),
        name="outproj_even",
    )(x, mod3, osb, omla, gz, w)


NSA_COLS = 5248


def _k1_nsa_kernel(x_ref, mod_ref, vec_ref, w_ref, g64_ref,
                   q_o, ck_o, cv_o, sk_o, sv_o, wk_o, wv_o, gt_o, gz_o, chunk_ref):
    ts = x_ref.shape[1]
    hb = _modulated(x_ref, mod_ref, vec_ref[0:1, 0:D_MODEL])
    g64 = g64_ref[...]
    inv64 = 1.0 / HEAD_DIM

    uq = _dot(hb, w_ref[:, 0:1536])
    for c in range(6):
        sl = slice(c * 256, (c + 1) * 256)
        qn = _group_norm_chunk(uq[:, c * 256:(c + 1) * 256], g64, inv64)
        q_o[0, :, sl] = (qn * vec_ref[1:2, sl]).astype(BF16)

    uc = _dot(hb, w_ref[:, 1536:2304])
    for c in range(2 * NSA_KV_HEADS):
        chunk_ref[c] = uc[:, c * 128:(c + 1) * 128]
    for g in range(NSA_KV_HEADS):
        for tok in range(CMP_STRIDE):
            rows = pl.ds(tok, ts // CMP_STRIDE, stride=CMP_STRIDE)
            sl = slice(tok * 128, (tok + 1) * 128)
            ck_o[0, g, :, sl] = chunk_ref[g, rows, :].astype(BF16)
            cv_o[0, g, :, sl] = chunk_ref[NSA_KV_HEADS + g, rows, :].astype(BF16)

    us = _dot(hb, w_ref[:, 2304:3840])
    kgain = vec_ref[2:3, 0:384]
    g128 = g64[0:128, 0:128]
    for g in range(NSA_KV_HEADS):
        sl = slice(g * 128, (g + 1) * 128)
        t = us[:, g * 128:(g + 1) * 128]
        ss = _dot_hl(t * t, g128)
        sk_o[0, :, sl] = (t * lax.rsqrt(ss * inv64 + NORM_EPS) * kgain[:, sl]).astype(BF16)
        t = us[:, 768 + g * 128:768 + (g + 1) * 128]
        ss = _dot_hl(t * t, g128)
        wk_o[0, :, sl] = (t * lax.rsqrt(ss * inv64 + NORM_EPS) * kgain[:, sl]).astype(BF16)
    ones_hi = (lax.broadcasted_iota(jnp.int32, (1, 384), 1) % 128) >= HEAD_DIM
    sv_o[0] = jnp.where(ones_hi, 1.0, us[:, 384:768]).astype(BF16)
    wv_o[0] = jnp.where(ones_hi, 1.0, us[:, 1152:1536]).astype(BF16)

    ug = _dot(hb, w_ref[:, 3840:4224])
    gt_o[0] = _sigmoid(ug)
    uz = _dot(hb, w_ref[:, 4224:5248])
    gz_o[0] = (uz * _sigmoid(uz)).astype(BF16)


def _k1_nsa(x, mod3, vec, w, g64):
    b, s, d = x.shape
    ts = min(TS_PROJ, s)
    row = lambda n: pl.BlockSpec((1, ts, n), lambda i, j: (i, j, 0))
    cw = CMP_STRIDE * 128
    grp = pl.BlockSpec((1, NSA_KV_HEADS, ts // CMP_STRIDE, cw), lambda i, j: (i, 0, j, 0))
    return pl.pallas_call(
        _k1_nsa_kernel,
        grid=(b, s // ts),
        in_specs=[row(d), pl.BlockSpec((1, 1, 3 * d), lambda i, j: (i, 0, 0)),
                  _full(vec.shape), _full(w.shape), _full(g64.shape)],
        out_specs=[row(1536), grp, grp, row(384), row(384), row(384), row(384), row(384), row(1024)],
        out_shape=[jax.ShapeDtypeStruct((b, s, 1536), BF16),
                   jax.ShapeDtypeStruct((b, NSA_KV_HEADS, s // CMP_STRIDE, cw), BF16),
                   jax.ShapeDtypeStruct((b, NSA_KV_HEADS, s // CMP_STRIDE, cw), BF16),
                   jax.ShapeDtypeStruct((b, s, 384), BF16),
                   jax.ShapeDtypeStruct((b, s, 384), BF16),
                   jax.ShapeDtypeStruct((b, s, 384), BF16),
                   jax.ShapeDtypeStruct((b, s, 384), BF16),
                   jax.ShapeDtypeStruct((b, s, 384), F32),
                   jax.ShapeDtypeStruct((b, s, 1024), BF16)],
        scratch_shapes=[pltpu.VMEM((2 * NSA_KV_HEADS, ts, 128), F32)],
        compiler_params=_cparams(2),
        name="inproj_nsa",
    )(x, mod3, vec, w, g64)


def _k1_dil_kernel(x_ref, mod_ref, vec_ref, w_ref, g64_ref, *refs):
    outs, st = refs[:3 * N_DIL], refs[3 * N_DIL]
    ts = x_ref.shape[1]
    hb = _modulated(x_ref, mod_ref, vec_ref[0:1, 0:D_MODEL])
    g64 = g64_ref[...]
    inv64 = 1.0 / HEAD_DIM
    u = _dot(hb, w_ref[...])
    for g in range(N_DIL):
        dil = DIL_CFG[g][1]
        qn = _group_norm_chunk(u[:, g * 256:(g + 1) * 256], g64, inv64) * vec_ref[3:4, 0:256]
        kn = _group_norm_chunk(u[:, 768 + g * 256:768 + (g + 1) * 256], g64, inv64) * vec_ref[4:5, 0:256]
        vals = (qn, kn, u[:, 1536 + g * 256:1536 + (g + 1) * 256])
        for j, val in enumerate(vals):
            o_ref = outs[3 * g + j]
            if dil == 1:
                o_ref[0] = val.astype(BF16)
                continue
            for h in range(2):
                st[j, h] = val[:, h * 128:(h + 1) * 128]
            for r in range(dil):
                for h in range(2):
                    sl = slice(r * 256 + h * 128, r * 256 + (h + 1) * 128)
                    o_ref[0, :, sl] = st[j, h, pl.ds(r, ts // dil, stride=dil), :].astype(BF16)


def _k1_dil(x, mod3, vec, w, g64):
    b, s, d = x.shape
    ts = min(TS_PROJ, s)
    row = lambda n: pl.BlockSpec((1, ts, n), lambda i, j: (i, j, 0))
    specs, shapes = [], []
    for g in range(N_DIL):
        dil = DIL_CFG[g][1]
        for _ in range(3):
            specs.append(pl.BlockSpec((1, ts // dil, dil * 256), lambda i, j: (i, j, 0)))
            shapes.append(jax.ShapeDtypeStruct((b, s // dil, dil * 256), BF16))
    return pl.pallas_call(
        _k1_dil_kernel,
        grid=(b, s // ts),
        in_specs=[row(d), pl.BlockSpec((1, 1, 3 * d), lambda i, j: (i, 0, 0)),
                  _full(vec.shape), _full(w.shape), _full(g64.shape)],
        out_specs=specs,
        out_shape=shapes,
        scratch_shapes=[pltpu.VMEM((3, 2, ts, 128), F32)],
        compiler_params=_cparams(2),
        name="inproj_dil",
    )(x, mod3, vec, w, g64)


def _compress_kernel(xk_ref, xv_ref, wk_ref, wv_ref, pek_ref, pev_ref, gain_ref, g128_ref, kc_o, vc_o):
    nch = xk_ref.shape[2]

    def comp(x_ref, w_ref, pe_ref):
        x = x_ref[0, 0]
        lo = _dot(x, w_ref[0])
        hi = _dot(x, w_ref[1])
        pec = _dot_hl(pe_ref[0], w_ref[0]) + _dot_hl(pe_ref[1], w_ref[1])
        return lo + pltpu.roll(hi, nch - 1, 0) + pec[0:1, :]

    kc = comp(xk_ref, wk_ref, pek_ref)
    ss = _dot_hl(kc * kc, g128_ref[...])
    kc_o[0, 0] = (kc * lax.rsqrt(ss * (1.0 / HEAD_DIM) + NORM_EPS) * gain_ref[...]).astype(BF16)
    vc_o[0, 0] = comp(xv_ref, wv_ref, pev_ref).T.astype(BF16)


def _compress(ck, cv, wk, wv, pek, pev, gain, g128):
    b, g, nch, _ = ck.shape
    xk, xv = ck, cv
    blk = pl.BlockSpec((1, 1, nch, CMP_STRIDE * 128), lambda i, j: (i, j, 0, 0))
    oblk = pl.BlockSpec((1, 1, nch, 128), lambda i, j: (i, j, 0, 0))
    tblk = pl.BlockSpec((1, 1, 128, nch), lambda i, j: (i, j, 0, 0))
    return pl.pallas_call(
        _compress_kernel,
        grid=(b, g),
        in_specs=[blk, blk, _full(wk.shape), _full(wv.shape), _full(pek.shape), _full(pev.shape),
                  _full(gain.shape), _full(g128.shape)],
        out_specs=[oblk, tblk],
        out_shape=[jax.ShapeDtypeStruct((b, g, nch, 128), BF16), jax.ShapeDtypeStruct((b, g, 128, nch), BF16)],
        compiler_params=_cparams(2),
        name="nsa_compress",
    )(xk, xv, wk, wv, pek, pev, gain, g128)


def _pair_select(lo_half, a, b):
    return jnp.where(lo_half, a, b)


def _nsa_cmp_kernel(slope_ref, q_ref, kc_ref, vct_ref, ovlt_ref, cposc_ref, prow_ref, gt_ref,
                    oc_o, selb_o, any_o, *, tq):
    g = pl.program_id(1)
    qi = pl.program_id(2)
    q4 = q_ref[0]
    kc = kc_ref[0, 0]
    vct = vct_ref[0, 0]
    nch = kc.shape[0]
    t = qi * tq + lax.broadcasted_iota(jnp.int32, (1, tq), 1)
    n_id = lax.broadcasted_iota(jnp.int32, (nch, 1), 0)
    valid = (n_id * CMP_STRIDE + (CMP_LEN - 1)) <= t
    dist = (prow_ref[qi] - cposc_ref[...]) * LOG2E
    gtt = gt_ref[0].T
    row_lo = lax.broadcasted_iota(jnp.int32, (LANES, 1), 0) < HEAD_DIM
    psum = jnp.zeros((nch, tq), F32)
    outs = []
    for hh in range(NSA_HPG):
        s = _dot_nt(kc, q4[:, hh * LANES:(hh + 1) * LANES]) - slope_ref[g * NSA_HPG + hh] * dist
        s = jnp.where(valid, s, -jnp.inf)
        mx = jnp.max(s, axis=0, keepdims=True)
        mx = jnp.where(mx == -jnp.inf, 0.0, mx)
        e = jnp.exp2(s - mx)
        den = jnp.maximum(jnp.sum(e, axis=0, keepdims=True), TINY)
        p = e * (1.0 / den)
        psum = psum + p
        outs.append(_dot(vct, p.astype(BF16)) * gtt[3 * hh:3 * hh + 1, :])
    oc_o[0, :, 0:LANES] = jnp.where(row_lo, outs[0], outs[1]).T
    oc_o[0, :, LANES:2 * LANES] = jnp.where(row_lo, outs[2], outs[3]).T

    hi, lo = _split_hl(psum)
    imp = _dot(ovlt_ref[...], hi) + _dot(ovlt_ref[...], lo)
    blk = lax.broadcasted_iota(jnp.int32, (LANES, 1), 0)
    cur = t >> 6
    forced = (blk == 0) | (blk == cur) | (blk == cur - 1)
    allowed = blk <= cur
    score = jnp.where(allowed, imp + jnp.where(forced, FORCE_BONUS, 0.0), -jnp.inf)
    blk_f = blk.astype(F32)

    def pick(_, carry):
        sc, sel = carry
        mx = jnp.max(sc, axis=0, keepdims=True)
        idx = jnp.min(jnp.where(sc == mx, blk_f, float(LANES)), axis=0, keepdims=True)
        hit = blk_f == idx
        return jnp.where(hit, -jnp.inf, sc), jnp.where(hit, 1.0, sel)

    _, sel = lax.fori_loop(0, SEL_TOPN, pick, (score, jnp.zeros((LANES, tq), F32)))
    chosen = jnp.where((sel > 0.5) & allowed, 1.0, 0.0).T
    selb_o[0, 0] = ((chosen - 1.0) * SEL_OFF).astype(BF16)
    used = jnp.max(chosen, axis=0, keepdims=True)
    any_o[0, 0, 0] = jnp.broadcast_to(used, (8, LANES))


def _nsa_cmp(slopes, q, kc, vct, ovlt, cposc, pos_rows, gates):
    b, s, _ = q.shape
    nch = kc.shape[2]
    tq = pos_rows.shape[2]
    kern = functools.partial(_nsa_cmp_kernel, tq=tq)
    return pl.pallas_call(
        kern,
        grid=(b, NSA_KV_HEADS, s // tq),
        in_specs=[pl.BlockSpec(memory_space=pltpu.SMEM),
                  pl.BlockSpec((1, tq, 4 * LANES), lambda i, g, j: (i, j, g)),
                  pl.BlockSpec((1, 1, nch, LANES), lambda i, g, j: (i, g, 0, 0)),
                  pl.BlockSpec((1, 1, LANES, nch), lambda i, g, j: (i, g, 0, 0)),
                  _full(ovlt.shape), _full(cposc.shape), _full(pos_rows.shape),
                  pl.BlockSpec((1, tq, LANES), lambda i, g, j: (i, j, g))],
        out_specs=[pl.BlockSpec((1, tq, 2 * LANES), lambda i, g, j: (i, j, g)),
                   pl.BlockSpec((1, 1, tq, LANES), lambda i, g, j: (i, g, j, 0)),
                   pl.BlockSpec((1, 1, 1, 8, LANES), lambda i, g, j: (i, g, j, 0, 0))],
        out_shape=[jax.ShapeDtypeStruct((b, s, NSA_W), F32),
                   jax.ShapeDtypeStruct((b, NSA_KV_HEADS, s, LANES), BF16),
                   jax.ShapeDtypeStruct((b, NSA_KV_HEADS, s // tq, 8, LANES), F32)],
        compiler_params=_cparams(3),
        name="nsa_cmp_topk",
    )(slopes, q, kc, vct, ovlt, cposc, pos_rows, gates)


def _gqa_kernel(flag_ref, slope_ref, bnd_ref, q_ref, selb_ref, k_ref, v_ref, pos_ref, pcol_ref, gt_ref, o_ref,
                qa_ref, acc_ref, m_ref, *, tq, tk, branch, bounded):
    bi = pl.program_id(0)
    g = pl.program_id(1)
    qi = pl.program_id(2)
    q4 = q_ref[0]
    lane = lax.broadcasted_iota(jnp.int32, (1, LANES), 1)
    lo_half = lane < HEAD_DIM
    pref = pos_ref[qi * (tq // tk)][:, 0:1]
    for hh in range(NSA_HPG):
        sl = slice(hh * tq, (hh + 1) * tq)
        qa_ref[sl, 0:LANES] = q4[:, hh * LANES:(hh + 1) * LANES]
        if branch == 1:
            qa_ref[sl, LANES:2 * LANES] = selb_ref[0, 0]
        if bounded:
            own = (pcol_ref[...] - pref) * (LOG2E * slope_ref[g * NSA_HPG + hh]) + bnd_ref[0]
            m_ref[sl] = jnp.broadcast_to(own, (tq, LANES))
    acc_ref[...] = jnp.zeros_like(acc_ref)
    if not bounded:
        m_ref[...] = jnp.full_like(m_ref, NEG_MASK)
    rows = qi * tq + lax.broadcasted_iota(jnp.int32, (tq, 1), 0)
    col0 = lax.broadcasted_iota(jnp.int32, (1, tk), 1)
    krow = lax.broadcasted_iota(jnp.int32, (tk, 1), 0)
    per = tq // tk
    nrep = tk // LANES
    flag0 = ((bi * NSA_KV_HEADS + g) * pl.num_programs(2) + qi) * SEL_TILE_STRIDE

    def scores(kt_true, masked):
        kt = jnp.maximum(kt_true, 0)
        off = pl.multiple_of(kt * tk, tk)
        k = k_ref[0, pl.ds(off, tk), :]
        if branch == 1:
            blk = (kt * tk + krow) >> 6
            onehot = jnp.where(lane == blk, 1.0, 0.0).astype(BF16)
            s_all = _dot_nt(qa_ref[...], jnp.concatenate([k, onehot], axis=1))
        else:
            s_all = _dot_nt(qa_ref[:, 0:LANES], k)
        rel = (pos_ref[kt] - pref) * LOG2E
        if masked:
            cols = kt_true * tk + col0
            d = rows - cols
            keep = d >= 0
            if branch == 2:
                keep = keep & (d < WIN) & (cols >= 0)
        out = []
        for hh in range(NSA_HPG):
            s = s_all[hh * tq:(hh + 1) * tq] + slope_ref[g * NSA_HPG + hh] * rel
            if masked:
                s = jnp.where(keep, s, NEG_MASK)
            out.append(s)
        return out

    def sweep(fn):
        if branch == 1:
            def body(j, carry):
                @pl.when(flag_ref[flag0 + j] != 0)
                def _():
                    fn(j, False)
                return carry
            lax.fori_loop(0, qi * per, body, 0)
            for dd in range(per):
                fn(qi * per + dd, True)
        else:
            for dd in range(per + WIN // tk):
                fn((qi + 1) * per - 1 - dd, True)

    def row_max(kt_true, masked):
        for hh, s in enumerate(scores(kt_true, masked)):
            sl = slice(hh * tq, (hh + 1) * tq)
            m_ref[sl] = jnp.maximum(m_ref[sl], _lane_max(s))

    def accumulate(kt_true, masked):
        off = pl.multiple_of(jnp.maximum(kt_true, 0) * tk, tk)
        v = v_ref[0, pl.ds(off, tk), :]
        for hh, s in enumerate(scores(kt_true, masked)):
            sl = slice(hh * tq, (hh + 1) * tq)
            p = jnp.exp2(s - jnp.concatenate([m_ref[sl]] * nrep, axis=1))
            acc_ref[sl] += _dot(p.astype(BF16), v)

    if not bounded:
        sweep(row_max)
        for hh in range(NSA_HPG):
            sl = slice(hh * tq, (hh + 1) * tq)
            m_ref[sl] = jnp.broadcast_to(jnp.max(m_ref[sl], axis=-1, keepdims=True), (tq, LANES))
    sweep(accumulate)

    gt = gt_ref[0]
    for pr in range(2):
        res = []
        for x in range(2):
            hh = 2 * pr + x
            a = acc_ref[hh * tq:(hh + 1) * tq]
            r = pltpu.roll(a, HEAD_DIM, 1)
            o = a / r if x == 0 else r / a
            res.append(o * gt[:, 3 * hh + branch:3 * hh + branch + 1])
        o_ref[0, :, pr * LANES:(pr + 1) * LANES] = _pair_select(lo_half, res[0], res[1])


def _nsa_gqa(flags, slopes, bound, safe, q, selb, k, v, pos2d, pos_col, gates, branch):
    b, s, _ = q.shape
    tq = min(TQ_NSA, s)
    tk = min(TK_NSA, tq)

    def call(bounded):
        kern = functools.partial(_gqa_kernel, tq=tq, tk=tk, branch=branch, bounded=bounded)
        smem = pl.BlockSpec(memory_space=pltpu.SMEM)
        name = ("nsa_sel_attn" if branch == 1 else "nsa_win_attn") + ("_bounded" if bounded else "")
        return pl.pallas_call(
            kern,
            grid=(b, NSA_KV_HEADS, s // tq),
            in_specs=[smem, smem, smem,
                      pl.BlockSpec((1, tq, 4 * LANES), lambda i, g, j: (i, j, g)),
                      pl.BlockSpec((1, 1, tq, LANES), lambda i, g, j: (i, g, j, 0)),
                      pl.BlockSpec((1, s, LANES), lambda i, g, j: (i, 0, g)),
                      pl.BlockSpec((1, s, LANES), lambda i, g, j: (i, 0, g)),
                      _full(pos2d.shape),
                      pl.BlockSpec((tq, 1), lambda i, g, j: (j, 0)),
                      pl.BlockSpec((1, tq, LANES), lambda i, g, j: (i, j, g))],
            out_specs=pl.BlockSpec((1, tq, 2 * LANES), lambda i, g, j: (i, j, g)),
            out_shape=jax.ShapeDtypeStruct((b, s, NSA_W), F32),
            scratch_shapes=[pltpu.VMEM((NSA_HPG * tq, 2 * LANES), BF16),
                            pltpu.VMEM((NSA_HPG * tq, LANES), F32),
                            pltpu.VMEM((NSA_HPG * tq, LANES), F32)],
            compiler_params=_cparams(3),
            name=name,
        )(flags, slopes, bound, q, selb, k, v, pos2d, pos_col, gates)

    return lax.cond(safe, lambda: call(True), lambda: call(False))


def _dil_kernel(slope_ref, bnd_ref, q_ref, k_ref, v_ref, pc_ref, pr_ref, o_ref, lse_ref,
                *, t, span, grp, bounded):
    i = pl.program_id(2)
    q4 = q_ref[0]
    lane = lax.broadcasted_iota(jnp.int32, (1, LANES), 1)
    lo_half = lane < HEAD_DIM
    per = t // LANES
    prev = jnp.maximum(i * per - 1, 0)
    o_prev = pl.multiple_of(prev * LANES, LANES)
    o_cur = pl.multiple_of(i * t, t)
    kk = jnp.concatenate([k_ref[0, pl.ds(o_prev, LANES), :], k_ref[0, pl.ds(o_cur, t), :]], axis=0)
    vv = jnp.concatenate([v_ref[0, pl.ds(o_prev, LANES), :], v_ref[0, pl.ds(o_cur, t), :]], axis=0)
    pk = jnp.concatenate([pr_ref[0, prev]] + [pr_ref[0, i * per + c] for c in range(per)], axis=1)
    dist = (pc_ref[0] - pk) * LOG2E
    rows = i * t + lax.broadcasted_iota(jnp.int32, (t, 1), 0)
    cols = i * t - LANES + lax.broadcasted_iota(jnp.int32, (1, t + LANES), 1)
    d = rows - cols
    valid = (cols >= 0) & (d >= 0) & (d <= span)
    zero = jnp.zeros((t, LANES), BF16)
    ones = jnp.ones((t + LANES, LANES), BF16)
    for p in range(2):
        qp = q4[:, p * LANES:(p + 1) * LANES]
        kp = kk[:, p * LANES:(p + 1) * LANES]
        vx = jnp.concatenate([vv[:, p * LANES:(p + 1) * LANES], ones], axis=1)
        res = []
        for x in range(2):
            qx = jnp.where(lo_half, qp, zero) if x == 0 else jnp.where(lo_half, zero, qp)
            s = _dot_nt(qx, kp) - slope_ref[grp * DIL_HEADS + 2 * p + x] * dist
            s = jnp.where(valid, s, -jnp.inf)
            mx = bnd_ref[0] if bounded else jnp.max(s, axis=-1, keepdims=True)
            acc = _dot(jnp.exp2(s - mx).astype(BF16), vx)
            den = acc[:, LANES:2 * LANES]
            res.append((acc[:, 0:LANES] / den, mx * LN2 + jnp.log(den)))
        sl = slice(p * LANES, (p + 1) * LANES)
        o_ref[0, :, sl] = _pair_select(lo_half, res[0][0], res[1][0])
        lse_ref[0, :, sl] = jnp.where(lo_half, res[0][1], res[1][1])


def _dilated(slopes, bound, safe, qv, kv, vv, pos_f, grp):
    window, dil = DIL_CFG[grp]
    b, sub, wd = qv.shape
    w = wd // dil
    t = min(T_DIL, sub)
    span = window // dil
    assert span <= LANES and t % LANES == 0
    pres = pos_f.reshape(sub, dil).T
    pc = pres.reshape(dil, sub, 1)
    pr = pres.reshape(dil, sub // LANES, 1, LANES)
    shp = jax.ShapeDtypeStruct((b, sub, wd), F32)
    smem = pl.BlockSpec(memory_space=pltpu.SMEM)

    def call(bounded):
        kern = functools.partial(_dil_kernel, t=t, span=span, grp=grp, bounded=bounded)
        return pl.pallas_call(
            kern,
            grid=(b, dil, sub // t),
            in_specs=[smem, smem,
                      pl.BlockSpec((1, t, w), lambda i, r, j: (i, j, r)),
                      pl.BlockSpec((1, sub, w), lambda i, r, j: (i, 0, r)),
                      pl.BlockSpec((1, sub, w), lambda i, r, j: (i, 0, r)),
                      pl.BlockSpec((1, t, 1), lambda i, r, j: (r, j, 0)),
                      pl.BlockSpec((1, sub // LANES, 1, LANES), lambda i, r, j: (r, 0, 0, 0))],
            out_specs=[pl.BlockSpec((1, t, w), lambda i, r, j: (i, j, r)),
                       pl.BlockSpec((1, t, w), lambda i, r, j: (i, j, r))],
            out_shape=[shp, shp],
            compiler_params=_cparams(3),
            name="dilated_attn_g%d%s" % (grp, "_bounded" if bounded else ""),
        )(slopes, bound, qv, kv, vv, pc, pr)

    return lax.cond(safe, lambda: call(True), lambda: call(False))


def _out_odd_kernel(x_ref, mod_ref, oc_ref, os_ref, ow_ref, d0_ref, d1_ref, d2_ref,
                    l0_ref, l1_ref, l2_ref, gz_ref, w_ref, o_ref, st):
    ts = x_ref.shape[1]
    gz = gz_ref[0].astype(F32)
    nsa = oc_ref[0] + os_ref[0] + ow_ref[0]
    m1 = (nsa * gz[:, 0:NSA_W]).astype(BF16)

    def token_order(ref, g, slot):
        dil = DIL_CFG[g][1]
        if dil == 1:
            return ref[0]
        for r in range(dil):
            for h in range(2):
                st[slot, h, pl.ds(r, ts // dil, stride=dil), :] = ref[0, :, r * DIL_W + h * 128:r * DIL_W + (h + 1) * 128]
        return jnp.concatenate([st[slot, 0], st[slot, 1]], axis=1)

    d0, d1, d2 = token_order(d0_ref, 0, 0), token_order(d1_ref, 1, 0), token_order(d2_ref, 2, 1)
    l0, l1, l2 = token_order(l0_ref, 0, 2), token_order(l1_ref, 1, 2), token_order(l2_ref, 2, 3)
    mx = jnp.maximum(jnp.maximum(l0, l1), l2)
    e0, e1, e2 = jnp.exp(l0 - mx), jnp.exp(l1 - mx), jnp.exp(l2 - mx)
    dil = (e0 * d0 + e1 * d1 + e2 * d2) / (e0 + e1 + e2)
    m2 = (dil * gz[:, NSA_W:NSA_W + DIL_W]).astype(BF16)
    y = _dot(m1, w_ref[0:NSA_W, :]) + _dot(m2, w_ref[NSA_W:NSA_W + DIL_W, :])
    gate = mod_ref[0][:, 2 * D_MODEL:3 * D_MODEL]
    o_ref[0] = x_ref[0] + gate * y


def _out_odd(x, mod3, oc, os_, ow, dils, lses, gz, w):
    b, s, d = x.shape
    ts = min(TS_PROJ, s)
    row = lambda n: pl.BlockSpec((1, ts, n), lambda i, j: (i, j, 0))
    res = [pl.BlockSpec((1, ts // DIL_CFG[g][1], DIL_CFG[g][1] * DIL_W), lambda i, j: (i, j, 0))
           for g in range(N_DIL)]
    return pl.pallas_call(
        _out_odd_kernel,
        grid=(b, s // ts),
        in_specs=[row(d), pl.BlockSpec((1, 1, 3 * d), lambda i, j: (i, 0, 0)),
                  row(NSA_W), row(NSA_W), row(NSA_W)] + res + res + [row(1024), _full(w.shape)],
        out_specs=row(d),
        out_shape=jax.ShapeDtypeStruct((b, s, d), F32),
        scratch_shapes=[pltpu.VMEM((4, 2, ts, 128), F32)],
        compiler_params=_cparams(2),
        name="outproj_odd",
    )(x, mod3, oc, os_, ow, *dils, *lses, gz, w)


def _pad_cols(w, n):
    return jnp.pad(w, ((0, 0), (0, n - w.shape[1])))


def _pad_vec(v, n=D_MODEL):
    return jnp.pad(v, (0, n - v.shape[0]))


def _group_matrix(sizes, total):
    m = np.zeros((total, total), np.float32)
    off = 0
    for sz, on in sizes:
        if on:
            m[off:off + sz, off:off + sz] = 1.0
        off += sz
    return jnp.asarray(m, BF16)


def _swap_halves(w):
    h = w.shape[-1] // 2
    return jnp.concatenate([w[..., h:], w[..., :h]], axis=-1)


def _column_blocks(segments):
    blocks, cur, room = [], [], LANES
    for src, width in segments:
        while width:
            n = min(width, room)
            cur.append((src, n))
            src = None if src is None else src + n
            width -= n
            room -= n
            if room == 0:
                blocks.append(cur)
                cur, room = [], LANES
    assert not cur
    return blocks


def _pack_cols_kernel(w_ref, o_ref, *, blocks):
    _, rows, n_src = w_ref.shape
    for bi, pieces in enumerate(blocks):
        parts = []
        for src, width in pieces:
            if src is None:
                parts.append(jnp.zeros((rows, width), F32))
                continue
            a0 = src // LANES * LANES
            a1 = min(-(-(src + width) // LANES) * LANES, n_src)
            parts.append(w_ref[0, :, a0:a1][:, src - a0:src - a0 + width])
        blk = parts[0] if len(parts) == 1 else jnp.concatenate(parts, axis=1)
        o_ref[:, bi * LANES:(bi + 1) * LANES] = blk.astype(BF16)


def _pack_cols(w, layer, segments):
    blocks = _column_blocks(segments)
    _, d, n_src = w.shape
    tr = 256
    return pl.pallas_call(
        functools.partial(_pack_cols_kernel, blocks=blocks),
        grid=(d // tr,),
        in_specs=[pl.BlockSpec((1, tr, n_src), lambda i: (layer, i, 0))],
        out_specs=pl.BlockSpec((tr, len(blocks) * LANES), lambda i: (i, 0)),
        out_shape=jax.ShapeDtypeStruct((d, len(blocks) * LANES), BF16),
        compiler_params=_cparams(1),
        name="pack_weight_columns",
    )(w)


EVEN_SEGMENTS = ((0, 1536), (1536, 512), (2464, 512), (2048, 256), (2304, 128),
                 (None, 64), (2432, 32), (None, 32), (None, 64), (2448, 16), (2432, 16), (None, 32))


def _pack_even(w_in_all, layer, norm_g, sb_qn, sb_kn, qa_g, wq_up, kva_g, wkv_up, qn, kn):
    w = _pack_cols(w_in_all, layer, EVEN_SEGMENTS)
    wq3 = wq_up.reshape(MLA_Q_RANK, MLA_HEADS, MLA_NOPE + MLA_ROPE)
    zq = jnp.zeros((MLA_Q_RANK, MLA_HEADS, 32), wq_up.dtype)
    wq = jnp.concatenate([wq3, zq], axis=-1).reshape(MLA_Q_RANK, MLA_HEADS * 128).astype(BF16)
    wqs = jnp.concatenate([jnp.zeros((MLA_Q_RANK, MLA_HEADS, 64), wq_up.dtype),
                           _swap_halves(wq3[..., MLA_NOPE:]), zq], axis=-1)
    wqs = wqs.reshape(MLA_Q_RANK, MLA_HEADS * 128).astype(BF16)
    wkv3 = wkv_up.reshape(MLA_KV_RANK, MLA_HEADS, MLA_NOPE + MLA_V)
    wk = jnp.concatenate([wkv3[..., :MLA_NOPE], jnp.zeros_like(wkv3[..., :MLA_NOPE])], axis=-1)
    wkv = jnp.concatenate([wk.reshape(MLA_KV_RANK, MLA_HEADS * 128),
                           wkv3[..., MLA_NOPE:].reshape(MLA_KV_RANK, MLA_HEADS * MLA_V)], axis=1).astype(BF16)
    z32 = jnp.zeros((32,), F32)
    z64 = jnp.zeros((64,), F32)
    scale = (MLA_NOPE + MLA_ROPE) ** -0.5 * LOG2E
    qg = jnp.tile(jnp.concatenate([qn, z32]), MLA_HEADS) * scale
    qgs = jnp.tile(jnp.concatenate([z64, _swap_halves(qn[MLA_NOPE:]), z32]), MLA_HEADS) * scale
    kg = jnp.tile(jnp.concatenate([kn[:MLA_NOPE], z64]), MLA_HEADS)
    krg = jnp.concatenate([z64, kn[MLA_NOPE:], z32])
    krgs = jnp.concatenate([z64, _swap_halves(kn[MLA_NOPE:]), z32])
    cnt = jnp.tile(jnp.concatenate([jnp.full((64,), 1.0 / 64), jnp.full((32,), 1.0 / 32), jnp.ones((32,))]),
                   MLA_HEADS).astype(F32)
    rows = [norm_g, _pad_vec(jnp.tile(sb_qn, SB_HEADS) * (LOG2E * HEAD_DIM ** -0.5)), _pad_vec(jnp.tile(sb_kn, SB_HEADS)),
            _pad_vec(qa_g), _pad_vec(kva_g), qg, qgs, kg, _pad_vec(krg), _pad_vec(krgs), cnt]
    rows += [jnp.zeros((D_MODEL,), F32)] * (16 - len(rows))
    return w, wq, wqs, wkv, jnp.stack(rows).astype(F32)


def _padded_heads(start, n):
    return tuple(seg for h in range(n) for seg in ((start + h * HEAD_DIM, HEAD_DIM), (None, HEAD_DIM)))


NSA_SEGMENTS = (_padded_heads(0, NSA_HEADS)
                + tuple(seg for c in range(6) for seg in _padded_heads(768 + c * NSA_KV_W, NSA_KV_HEADS))
                + tuple(seg for g in range(NSA_KV_HEADS)
                        for seg in ((1920 + g * NSA_HPG * 3, NSA_HPG * 3), (None, LANES - NSA_HPG * 3)))
                + ((1956, NSA_W), (5028, DIL_W)))
DIL_SEGMENTS = ((2724, 3 * N_DIL * DIL_W),)


def _pack_odd(w_in_all, layer, norm_g, nsa_qn, nsa_kn, dil_qn, dil_kn):
    w_nsa = _pack_cols(w_in_all, layer, NSA_SEGMENTS)
    w_dil = _pack_cols(w_in_all, layer, DIL_SEGMENTS)
    z64 = jnp.zeros((64,), F32)
    n = NSA_HEADS * 128
    rows = [_pad_vec(norm_g, n),
            jnp.tile(jnp.concatenate([nsa_qn * (LOG2E * HEAD_DIM ** -0.5), z64]), NSA_HEADS),
            _pad_vec(jnp.tile(jnp.concatenate([nsa_kn, z64]), NSA_KV_HEADS), n),
            _pad_vec(jnp.tile(dil_qn, DIL_HEADS) * (LOG2E * HEAD_DIM ** -0.5), n),
            _pad_vec(jnp.tile(dil_kn, DIL_HEADS), n)]
    rows += [jnp.zeros((n,), F32)] * (8 - len(rows))
    return w_nsa, w_dil, jnp.stack(rows).astype(F32)


def _pack_compress(w, pe, double):
    w3 = w.reshape(CMP_LEN, HEAD_DIM, HEAD_DIM)
    w3 = jnp.concatenate([w3, jnp.zeros_like(w3)], axis=1)
    w3 = jnp.concatenate([w3, w3 if double else jnp.zeros_like(w3)], axis=2)
    wp = w3.reshape(2, CMP_STRIDE * 128, 128).astype(BF16)
    pe2 = jnp.concatenate([pe, jnp.zeros_like(pe)], axis=1).reshape(2, 1, CMP_STRIDE * 128)
    pe2 = jnp.broadcast_to(pe2, (2, 8, CMP_STRIDE * 128)).astype(F32)
    return wp, pe2


def _normed_len(gain, sizes):
    tot, off = 0.0, 0
    for n in sizes:
        tot = tot + n * jnp.max(jnp.square(gain[off:off + n]))
        off += n
    return jnp.sqrt(tot)


def _mla_logit_bound(qn, kn):
    sizes = (MLA_NOPE, MLA_ROPE)
    scale = (MLA_NOPE + MLA_ROPE) ** -0.5 * LOG2E
    return (_normed_len(qn, sizes) * _normed_len(kn, sizes) * (scale * ROUNDING_MARGIN)).reshape(1).astype(F32)


def _alibi_slopes(n):
    return 2.0 ** (-8.0 * jnp.arange(1, n + 1, dtype=jnp.float32) / n)


def kernel(x, c, positions, ada_w, ada_b, norm_g, ev_w_in, ev_w_out, sb_qn, sb_kn, mla_qa_g, mla_wq_up,
           mla_kva_g, mla_wkv_up, mla_qn, mla_kn, od_w_in, od_w_out, nsa_qn, nsa_kn, nsa_cmp_wk, nsa_cmp_wv,
           nsa_cmp_pe_k, nsa_cmp_pe_v, dil_qn, dil_kn):
    b, s, d = x.shape
    depth = ada_w.shape[0]
    pos_f = positions.astype(F32)

    inv_freq = ROPE_BASE ** (-jnp.arange(0, MLA_ROPE, 2, dtype=F32) / MLA_ROPE)
    ang = pos_f[:, None] * inv_freq[None, :]
    cos, sin = jnp.cos(ang), jnp.sin(ang)
    cos128 = jnp.concatenate([jnp.ones((s, 64), F32), cos, cos, jnp.zeros((s, 32), F32)], axis=1)
    sin128 = jnp.concatenate([jnp.zeros((s, 64), F32), -sin, sin, jnp.zeros((s, 32), F32)], axis=1)
    nsa_slopes = _alibi_slopes(NSA_HEADS)
    dil_slopes = _alibi_slopes(N_DIL * DIL_HEADS)
    nch = s // CMP_STRIDE
    chunk_sum = pos_f.reshape(nch, CMP_STRIDE).sum(axis=1)
    cpos = ((chunk_sum + jnp.roll(chunk_sum, -1)) / CMP_LEN).reshape(nch, 1)
    n_sel = s // SEL_LEN
    cst = np.arange(nch)[:, None] * CMP_STRIDE
    jst = np.arange(LANES)[None, :] * SEL_LEN
    ovl = ((cst <= jst + SEL_LEN - 1) & (cst + CMP_LEN - 1 >= jst) & (np.arange(LANES)[None, :] < n_sel))
    ovlt = jnp.asarray(ovl.astype(np.float32).T, BF16)
    pos_col = pos_f.reshape(s, 1)
    tqc = min(TQ_CMP, s)
    pos_rows = pos_f.reshape(s // tqc, 1, tqc)
    tqn = min(TQ_NSA, s)
    tkn = min(TK_NSA, tqn)
    assert tkn == TK_NSA
    pos2d = pos_f.reshape(s // tkn, 1, tkn)
    pos_sorted = jnp.all(pos_f[1:] >= pos_f[:-1])

    g64 = _group_matrix([(64, 1)] * 4, 256)
    gm = _group_matrix([(64, 1), (32, 1), (32, 0)] * 2, 256)
    tks = min(TK_SB, min(TQ_SB, s))
    tri = jnp.asarray(np.tril(np.ones((tks, tks), np.float32)), BF16)

    c8 = jnp.pad(c, ((0, 8 - b), (0, 0)))
    mod_all = _modulation(c8, ada_w, ada_b)

    for layer in range(depth):
        j = layer // 2
        mod3 = mod_all[layer, :b].reshape(b, 1, 3 * d)
        if layer % 2 == 0:
            w, wq, wqs, wkv, vec = _pack_even(ev_w_in, j, norm_g[layer], sb_qn[j], sb_kn[j], mla_qa_g[j],
                                              mla_wq_up[j], mla_kva_g[j], mla_wkv_up[j], mla_qn[j], mla_kn[j])
            sbq, sbk, sbv, gz, mq, mk, mv = _k1_even(x, mod3, vec, w, wq, wqs, wkv, g64, gm, cos128, sin128)
            sb_bound = (_normed_len(sb_qn[j], (HEAD_DIM,)) * _normed_len(sb_kn[j], (HEAD_DIM,))
                        * (HEAD_DIM ** -0.5 * LOG2E * ROUNDING_MARGIN)).reshape(1).astype(F32)
            o_sb = _sb_attention(sbq, sbk, sbv, tri, sb_bound, sb_bound[0] <= SAFE_LOGIT_BOUND)
            o_mla = _mla_attention(mq, mk, mv, _mla_logit_bound(mla_qn[j], mla_kn[j]))
            x = _out_even(x, mod3, o_sb, o_mla, gz, ev_w_out[j].astype(BF16))
        else:
            w_nsa, w_dil, vec = _pack_odd(od_w_in, j, norm_g[layer], nsa_qn[j], nsa_kn[j], dil_qn[j], dil_kn[j])
            q, ck, cv, sk, sv, wk, wv, gates, gz = _k1_nsa(x, mod3, vec, w_nsa, g64)
            dqkv = _k1_dil(x, mod3, vec, w_dil, g64)
            wck, pek = _pack_compress(nsa_cmp_wk[j], nsa_cmp_pe_k[j], False)
            wcv, pev = _pack_compress(nsa_cmp_wv[j], nsa_cmp_pe_v[j], True)
            kgain = jnp.concatenate([nsa_kn[j], jnp.zeros((64,), F32)]).reshape(1, 128)
            kc, vc = _compress(ck, cv, wck, wcv, pek, pev, kgain, g64[0:128, 0:128])
            o_c, selb, used = _nsa_cmp(nsa_slopes, q, kc, vc, ovlt, cpos, pos_rows, gates)
            flags = used[:, :, :, 0, :].reshape(b, NSA_KV_HEADS, s // tqn, tqn // min(TQ_CMP, s),
                                                SEL_TILE_STRIDE, LANES // SEL_TILE_STRIDE).max(axis=(3, 5))
            flags = (flags > 0).astype(jnp.int32).reshape(-1)
            nsa_bound = (_normed_len(nsa_qn[j], (HEAD_DIM,)) * _normed_len(nsa_kn[j], (HEAD_DIM,))
                         * (HEAD_DIM ** -0.5 * LOG2E * ROUNDING_MARGIN)).reshape(1).astype(F32)
            safe = (nsa_bound[0] <= SAFE_LOGIT_BOUND) & pos_sorted
            o_s = _nsa_gqa(flags, nsa_slopes, nsa_bound, safe, q, selb, sk, sv, pos2d, pos_col, gates, 1)
            o_w = _nsa_gqa(flags, nsa_slopes, nsa_bound, safe, q, selb, wk, wv, pos2d, pos_col, gates, 2)
            dil_bound = (_normed_len(dil_qn[j], (HEAD_DIM,)) * _normed_len(dil_kn[j], (HEAD_DIM,))
                         * (HEAD_DIM ** -0.5 * LOG2E * ROUNDING_MARGIN)).reshape(1).astype(F32)
            dil_safe = (dil_bound[0] <= SAFE_LOGIT_BOUND) & pos_sorted
            dils, lses = [], []
            for g in range(N_DIL):
                o, lse = _dilated(dil_slopes, dil_bound, dil_safe, dqkv[3 * g], dqkv[3 * g + 1], dqkv[3 * g + 2],
                                  pos_f, g)
                dils.append(o)
                lses.append(lse)
            x = _out_odd(x, mod3, o_c, o_s, o_w, dils, lses, gz, od_w_out[j].astype(BF16))
    return x
```

```python
import functools

import numpy as np
import jax
import jax.numpy as jnp
from jax import lax
from jax.experimental import pallas as pl
from jax.experimental.pallas import tpu as pltpu

F32 = jnp.float32
BF16 = jnp.bfloat16

D_MODEL = 1024
HEAD_DIM = 64
NORM_EPS = 1e-6
TINY = 1e-30
SB_HEADS = 8
MLA_HEADS = 8
MLA_Q_RANK = 256
MLA_KV_RANK = 128
MLA_NOPE = 64
MLA_ROPE = 32
MLA_V = 64
ROPE_BASE = 10000.0
NSA_HEADS = 12
NSA_KV_HEADS = 3
NSA_HPG = 4
CMP_LEN = 32
CMP_STRIDE = 16
SEL_LEN = 64
SEL_TOPN = 16
WIN = 512
FORCE_BONUS = 1e3
DIL_CFG = ((128, 1), (512, 4), (2048, 16))
N_DIL = 3
DIL_HEADS = 4
SB_W = SB_HEADS * HEAD_DIM
MLA_OUT = MLA_HEADS * MLA_V
NSA_W = NSA_HEADS * HEAD_DIM
NSA_KV_W = NSA_KV_HEADS * HEAD_DIM
DIL_W = DIL_HEADS * HEAD_DIM

LANES = 128
MXU_DIM = 256
NEG_MASK = -1e30
LOG2E = 1.4426950408889634
LN2 = 0.6931471805599453
F32_ZERO_EXP = 150.0
SAFE_LOGIT_BOUND = 50.0
ROUNDING_MARGIN = 1.02
SEL_OFF = 2.0 ** 30
VMEM_LIMIT = 56 * 1024 * 1024

TS_PROJ = 512
TQ_SB = 1024
TK_SB = 256
TQ_MLA = 1024
TD_MLA = 512
TQ_CMP = 512
TQ_NSA = 512
TK_NSA = 512
TK_WIN = 256
SEL_TILE_STRIDE = LANES * SEL_LEN // TK_NSA
T_DIL = 256


def _dot(a, b):
    return jnp.dot(a, b, preferred_element_type=F32)


def _dot_nt(a, b):
    return lax.dot_general(a, b, (((1,), (1,)), ((), ())), preferred_element_type=F32)


def _split_hl(a):
    hi = a.astype(BF16)
    lo = (a - hi.astype(F32)).astype(BF16)
    return hi, lo


def _dot_hl(a, b):
    hi, lo = _split_hl(a)
    return _dot(hi, b) + _dot(lo, b)


def _sigmoid(z):
    return 1.0 / (1.0 + jnp.exp(-z))


def _cparams(n_axes):
    return pltpu.CompilerParams(dimension_semantics=("arbitrary",) * n_axes,
                                vmem_limit_bytes=VMEM_LIMIT)


def _full(shape):
    n = len(shape)
    return pl.BlockSpec(shape, lambda *a, _n=n: (0,) * _n)


def _mod_kernel(c_ref, w_ref, b_ref, o_ref):
    c = c_ref[...]
    a = c * _sigmoid(c)
    ah, al = _split_hl(a)
    wh, wl = _split_hl(w_ref[0])
    o_ref[0] = _dot(ah, wh) + _dot(ah, wl) + _dot(al, wh) + b_ref[0]


def _modulation(c8, ada_w, ada_b):
    depth, d, n3 = ada_w.shape
    tn = 1024
    return pl.pallas_call(
        _mod_kernel,
        grid=(depth, n3 // tn),
        in_specs=[pl.BlockSpec((8, d), lambda l, j: (0, 0)),
                  pl.BlockSpec((1, d, tn), lambda l, j: (l, 0, j)),
                  pl.BlockSpec((1, 1, tn), lambda l, j: (l, 0, j))],
        out_specs=pl.BlockSpec((1, 8, tn), lambda l, j: (l, 0, j)),
        out_shape=jax.ShapeDtypeStruct((depth, 8, n3), F32),
        compiler_params=_cparams(2),
        name="adaln_mod",
    )(c8, ada_w, ada_b.reshape(depth, 1, n3))


def _modulated(x_ref, mod_ref, ng):
    x = x_ref[0]
    mod = mod_ref[0]
    shift = mod[:, 0:D_MODEL]
    scale = mod[:, D_MODEL:2 * D_MODEL]
    ms = jnp.mean(x * x, axis=-1, keepdims=True)
    h = x * lax.rsqrt(ms + NORM_EPS) * ng
    h = h * (1.0 + scale) + shift
    return h.astype(BF16)


def _group_norm_chunk(t, g, inv_cnt):
    ss = _dot_hl(t * t, g)
    return t * lax.rsqrt(ss * inv_cnt + NORM_EPS)


def _row_rms(t, gain):
    ms = jnp.mean(t * t, axis=-1, keepdims=True)
    return t * lax.rsqrt(ms + NORM_EPS) * gain


EV_COLS = 3200


def _k1_even_kernel(x_ref, mod_ref, vec_ref, w_ref, wq_ref, wqs_ref, wkv_ref, g64_ref, gm_ref,
                    cos_ref, sin_ref, sbq_o, sbk_o, sbv_o, gz_o, mq_o, mk_o, mv_o):
    hb = _modulated(x_ref, mod_ref, vec_ref[0:1, :])
    g64 = g64_ref[...]
    gm = gm_ref[...]
    inv64 = 1.0 / HEAD_DIM

    u = _dot(hb, w_ref[:, 0:1536])
    for c in range(2):
        sl = slice(c * 256, (c + 1) * 256)
        qn = _group_norm_chunk(u[:, c * 256:(c + 1) * 256], g64, inv64)
        sbq_o[0, :, sl] = (qn * vec_ref[1:2, sl]).astype(BF16)
        kn = _group_norm_chunk(u[:, 512 + c * 256:512 + (c + 1) * 256], g64, inv64)
        sbk_o[0, :, sl] = (kn * vec_ref[2:3, sl]).astype(BF16)
    sbv_o[0] = u[:, 1024:1536].astype(BF16)

    uz = _dot(hb, w_ref[:, 1536:2560])
    gz_o[0] = (uz * _sigmoid(uz)).astype(BF16)

    ul = _dot(hb, w_ref[:, 2560:3200])
    qlat = _row_rms(ul[:, 0:256], vec_ref[3:4, 0:256]).astype(BF16)
    kvlat = _row_rms(ul[:, 256:384], vec_ref[4:5, 0:128]).astype(BF16)
    krm = ul[:, 384:512]
    krs = ul[:, 512:640]
    cs = cos_ref[...]
    sn = sin_ref[...]
    cs2 = jnp.concatenate([cs, cs], axis=1)
    sn2 = jnp.concatenate([sn, sn], axis=1)

    tq = _dot(qlat, wq_ref[...])
    tqs = _dot(qlat, wqs_ref[...])
    for p in range(4):
        sl = slice(p * 256, (p + 1) * 256)
        tc = tq[:, p * 256:(p + 1) * 256]
        ss = _dot_hl(tc * tc, gm)
        inv = lax.rsqrt(ss * vec_ref[10:11, sl] + NORM_EPS)
        a = tc * inv * vec_ref[5:6, sl]
        b = tqs[:, p * 256:(p + 1) * 256] * inv * vec_ref[6:7, sl]
        mq_o[0, :, sl] = (a * cs2 + b * sn2).astype(BF16)

    sskr = _dot_hl(krm * krm, gm[0:128, 0:128])
    invr = lax.rsqrt(sskr * vec_ref[10:11, 0:128] + NORM_EPS)
    kr = (krm * invr * vec_ref[8:9, 0:128]) * cs + (krs * invr * vec_ref[9:10, 0:128]) * sn
    kr2 = jnp.concatenate([kr, kr], axis=1)

    kv = _dot(kvlat, wkv_ref[...])
    for p in range(4):
        sl = slice(p * 256, (p + 1) * 256)
        kn = _group_norm_chunk(kv[:, p * 256:(p + 1) * 256], g64, inv64)
        mk_o[0, :, sl] = (kn * vec_ref[7:8, sl] + kr2).astype(BF16)
    mv_o[0] = kv[:, 1024:1536].astype(BF16)


def _k1_even(x, mod3, vec, w, wq, wqs, wkv, g64, gm, cos128, sin128):
    b, s, d = x.shape
    ts = min(TS_PROJ, s)
    row = lambda n: pl.BlockSpec((1, ts, n), lambda i, j: (i, j, 0))
    outs = [(512, BF16), (512, BF16), (512, BF16), (1024, BF16), (1024, BF16), (1024, BF16), (512, BF16)]
    return pl.pallas_call(
        _k1_even_kernel,
        grid=(b, s // ts),
        in_specs=[row(d),
                  pl.BlockSpec((1, 1, 3 * d), lambda i, j: (i, 0, 0)),
                  _full(vec.shape), _full(w.shape), _full(wq.shape), _full(wqs.shape), _full(wkv.shape),
                  _full(g64.shape), _full(gm.shape),
                  pl.BlockSpec((ts, 128), lambda i, j: (j, 0)),
                  pl.BlockSpec((ts, 128), lambda i, j: (j, 0))],
        out_specs=[row(n) for n, _ in outs],
        out_shape=[jax.ShapeDtypeStruct((b, s, n), dt) for n, dt in outs],
        compiler_params=_cparams(2),
        name="inproj_even",
    )(x, mod3, vec, w, wq, wqs, wkv, g64, gm, cos128, sin128)


def _sb_kernel(bnd_ref, q_ref, k_ref, v_ref, tri_ref, o_ref, acc_ref, car_ref, *, tq, tk, bounded):
    qi = pl.program_id(2)
    q = q_ref[0]
    lane = lax.broadcasted_iota(jnp.int32, (1, LANES), 1)
    lo_half = lane < HEAD_DIM
    zero = jnp.zeros_like(q)
    qs = (jnp.where(lo_half, q, zero), jnp.where(lo_half, zero, q))
    tri = tri_ref[...]
    acc_ref[...] = jnp.zeros_like(acc_ref)
    car_ref[...] = jnp.zeros_like(car_ref)
    col0 = lax.broadcasted_iota(jnp.int32, (1, tk), 1)

    def tile(kt, r0, masked):
        n = tq - r0
        off = pl.multiple_of(kt * tk, tk)
        k = k_ref[0, pl.ds(off, tk), :]
        v = v_ref[0, pl.ds(off, tk), :]
        if masked:
            rows = qi * tq + r0 + lax.broadcasted_iota(jnp.int32, (n, 1), 0)
            strict = (kt * tk + col0) < rows
        for i in range(2):
            z = _dot_nt(qs[i][r0:tq], k)
            if bounded:
                lom = jnp.log(1.0 + jnp.exp2(z)) * (-LOG2E)
            else:
                lom = -jnp.maximum(z, 0.0) - jnp.log(1.0 + jnp.exp2(-jnp.abs(z))) * LOG2E
            if masked:
                lom = jnp.where(strict, lom, 0.0)
            tt = _dot(lom.astype(BF16), tri)
            car = car_ref[i, r0:tq]
            w = jnp.exp2(tt + z + jnp.concatenate([car] * (tk // LANES), axis=1))
            if masked:
                w = jnp.where(strict, w, 0.0)
            acc_ref[i, r0:tq] += _dot(w.astype(BF16), v)
            car_ref[i, r0:tq] = car + jnp.broadcast_to(tt[:, 0:1], (n, LANES))

    per = tq // tk
    for c in reversed(range(per)):
        tile(qi * per + c, c * tk, True)

    if bounded:
        limit = -(F32_ZERO_EXP + bnd_ref[0])

        def still_live():
            return jnp.max(jnp.maximum(car_ref[0], car_ref[1])) > limit

        def single(c):
            j, _ = c
            tile(qi * per - 1 - j, 0, False)
            return j + 1, still_live()

        _, live = lax.while_loop(lambda c: (c[0] < per) & c[1], single, (0, qi > 0))

        def group(c):
            g, _ = c
            for u in range(per):
                tile(qi * per - 1 - per * (g + 1) - u, 0, False)
            return g + 1, still_live()

        lax.while_loop(lambda c: (c[0] < qi - 1) & c[1], group, (0, live))
    else:
        def body(j, carry):
            for u in range(per):
                tile(qi * per - 1 - per * j - u, 0, False)
            return carry

        lax.fori_loop(0, qi, body, 0)
    o_ref[0] = jnp.where(lo_half, acc_ref[0], acc_ref[1])


def _sb_attention(q, k, v, tri, bound, safe):
    b, s, w = q.shape
    tq = min(TQ_SB, s)
    tk = min(TK_SB, tq)

    def call(bounded):
        kern = functools.partial(_sb_kernel, tq=tq, tk=tk, bounded=bounded)
        return pl.pallas_call(
            kern,
            grid=(b, w // LANES, s // tq),
            in_specs=[pl.BlockSpec(memory_space=pltpu.SMEM),
                      pl.BlockSpec((1, tq, LANES), lambda i, p, j: (i, j, p)),
                      pl.BlockSpec((1, s, LANES), lambda i, p, j: (i, 0, p)),
                      pl.BlockSpec((1, s, LANES), lambda i, p, j: (i, 0, p)),
                      _full(tri.shape)],
            out_specs=pl.BlockSpec((1, tq, LANES), lambda i, p, j: (i, j, p)),
            out_shape=jax.ShapeDtypeStruct((b, s, w), F32),
            scratch_shapes=[pltpu.VMEM((2, tq, LANES), F32), pltpu.VMEM((2, tq, LANES), F32)],
            compiler_params=_cparams(3),
            name="stickbreak_attn_bounded" if bounded else "stickbreak_attn",
        )(bound, q, k, v, tri)

    return lax.cond(safe, lambda: call(True), lambda: call(False))


def _lane_max(s):
    m = s[:, 0:LANES]
    for c in range(1, s.shape[1] // LANES):
        m = jnp.maximum(m, s[:, c * LANES:(c + 1) * LANES])
    return m


def _mla_kernel(bnd_ref, q_ref, k_ref, v_ref, o_ref, acc_ref, m_ref, *, tq, td, bounded):
    qi = pl.program_id(2)
    q = q_ref[0]
    qs = (q[:, 0:LANES], q[:, LANES:2 * LANES])
    lane = lax.broadcasted_iota(jnp.int32, (1, LANES), 1)
    lo_half = lane < HEAD_DIM
    acc_ref[...] = jnp.zeros_like(acc_ref)
    if bounded:
        m_ref[...] = jnp.full(m_ref.shape, bnd_ref[0], F32)
    else:
        m_ref[...] = jnp.full_like(m_ref, NEG_MASK)

    def scores(off, width, r0, masked):
        k = k_ref[0, pl.ds(off, width), :]
        if masked:
            rows = qi * tq + r0 + lax.broadcasted_iota(jnp.int32, (tq - r0, 1), 0)
            keep = (off + lax.broadcasted_iota(jnp.int32, (1, width), 1)) <= rows
        out = []
        for i in range(2):
            s = _dot_nt(qs[i][r0:tq], k[:, i * LANES:(i + 1) * LANES])
            if masked:
                s = jnp.where(keep, s, NEG_MASK)
            out.append(s)
        return out

    def sweep(fn):
        def body(j, carry):
            fn(pl.multiple_of(j * tq, tq), tq, 0, False)
            return carry
        lax.fori_loop(0, qi, body, 0)
        for c in range(tq // td):
            fn(pl.multiple_of(qi * tq + c * td, td), td, c * td, True)

    def row_max(off, width, r0, masked):
        for i, s in enumerate(scores(off, width, r0, masked)):
            m_ref[i, r0:tq] = jnp.maximum(m_ref[i, r0:tq], _lane_max(s))

    def accumulate(off, width, r0, masked):
        vx = jnp.concatenate([v_ref[0, pl.ds(off, width), :], jnp.ones((width, LANES), BF16)], axis=1)
        for i, s in enumerate(scores(off, width, r0, masked)):
            p = jnp.exp2(s - jnp.concatenate([m_ref[i, r0:tq]] * (width // LANES), axis=1))
            acc_ref[i, r0:tq] += _dot(p.astype(BF16), vx)

    if not bounded:
        sweep(row_max)
        for i in range(2):
            m_ref[i] = jnp.broadcast_to(jnp.max(m_ref[i], axis=-1, keepdims=True), (tq, LANES))
    sweep(accumulate)
    o_ref[0] = jnp.where(lo_half, acc_ref[0, :, 0:LANES] / acc_ref[0, :, LANES:2 * LANES],
                         acc_ref[1, :, 0:LANES] / acc_ref[1, :, LANES:2 * LANES])


def _mla_attention(q, k, v, bound):
    b, s, _ = q.shape
    tq = min(TQ_MLA, s)
    td = min(TD_MLA, tq)

    def call(bounded):
        kern = functools.partial(_mla_kernel, tq=tq, td=td, bounded=bounded)
        return pl.pallas_call(
            kern,
            grid=(b, MLA_HEADS // 2, s // tq),
            in_specs=[pl.BlockSpec(memory_space=pltpu.SMEM),
                      pl.BlockSpec((1, tq, 2 * LANES), lambda i, p, j: (i, j, p)),
                      pl.BlockSpec((1, s, 2 * LANES), lambda i, p, j: (i, 0, p)),
                      pl.BlockSpec((1, s, LANES), lambda i, p, j: (i, 0, p))],
            out_specs=pl.BlockSpec((1, tq, LANES), lambda i, p, j: (i, j, p)),
            out_shape=jax.ShapeDtypeStruct((b, s, MLA_OUT), F32),
            scratch_shapes=[pltpu.VMEM((2, tq, 2 * LANES), F32), pltpu.VMEM((2, tq, LANES), F32)],
            compiler_params=_cparams(3),
            name="mla_attn_bounded" if bounded else "mla_attn",
        )(bound, q, k, v)

    return lax.cond(bound[0] <= SAFE_LOGIT_BOUND, lambda: call(True), lambda: call(False))


def _out_even_kernel(x_ref, mod_ref, osb_ref, omla_ref, gz_ref, w_ref, o_ref):
    gz = gz_ref[0].astype(F32)
    m1 = (osb_ref[0] * gz[:, 0:SB_W]).astype(BF16)
    m2 = (omla_ref[0] * gz[:, SB_W:SB_W + MLA_OUT]).astype(BF16)
    y = _dot(m1, w_ref[0:SB_W, :]) + _dot(m2, w_ref[SB_W:SB_W + MLA_OUT, :])
    gate = mod_ref[0][:, 2 * D_MODEL:3 * D_MODEL]
    o_ref[0] = x_ref[0] + gate * y


def _out_even(x, mod3, osb, omla, gz, w):
    b, s, d = x.shape
    ts = min(TS_PROJ, s)
    row = lambda n: pl.BlockSpec((1, ts, n), lambda i, j: (i, j, 0))
    return pl.pallas_call(
        _out_even_kernel,
        grid=(b, s // ts),
        in_specs=[row(d), pl.BlockSpec((1, 1, 3 * d), lambda i, j: (i, 0, 0)),
                  row(SB_W), row(MLA_OUT), row(SB_W + MLA_OUT), _full(w.shape)],
        out_specs=row(d),
        out_shape=jax.ShapeDtypeStruct((b, s, d), F32),
        compiler_params=_cparams(2),
        name="outproj_even",
    )(x, mod3, osb, omla, gz, w)


NSA_COLS = 5248


def _k1_nsa_kernel(x_ref, mod_ref, vec_ref, w_ref, g64_ref,
                   q_o, ck_o, cv_o, sk_o, sv_o, wk_o, wv_o, gt_o, gz_o, chunk_ref):
    ts = x_ref.shape[1]
    hb = _modulated(x_ref, mod_ref, vec_ref[0:1, 0:D_MODEL])
    g64 = g64_ref[...]
    inv64 = 1.0 / HEAD_DIM

    uq = _dot(hb, w_ref[:, 0:1536])
    for c in range(6):
        sl = slice(c * 256, (c + 1) * 256)
        qn = _group_norm_chunk(uq[:, c * 256:(c + 1) * 256], g64, inv64)
        q_o[0, :, sl] = (qn * vec_ref[1:2, sl]).astype(BF16)

    uc = _dot(hb, w_ref[:, 1536:2304])
    for c in range(2 * NSA_KV_HEADS):
        chunk_ref[c] = uc[:, c * 128:(c + 1) * 128]
    for g in range(NSA_KV_HEADS):
        for tok in range(CMP_STRIDE):
            rows = pl.ds(tok, ts // CMP_STRIDE, stride=CMP_STRIDE)
            sl = slice(tok * 128, (tok + 1) * 128)
            ck_o[0, g, :, sl] = chunk_ref[g, rows, :].astype(BF16)
            cv_o[0, g, :, sl] = chunk_ref[NSA_KV_HEADS + g, rows, :].astype(BF16)

    us = _dot(hb, w_ref[:, 2304:3840])
    kgain = vec_ref[2:3, 0:384]
    g128 = g64[0:128, 0:128]
    for g in range(NSA_KV_HEADS):
        sl = slice(g * 128, (g + 1) * 128)
        t = us[:, g * 128:(g + 1) * 128]
        ss = _dot_hl(t * t, g128)
        sk_o[0, :, sl] = (t * lax.rsqrt(ss * inv64 + NORM_EPS) * kgain[:, sl]).astype(BF16)
        t = us[:, 768 + g * 128:768 + (g + 1) * 128]
        ss = _dot_hl(t * t, g128)
        wk_o[0, :, sl] = (t * lax.rsqrt(ss * inv64 + NORM_EPS) * kgain[:, sl]).astype(BF16)
    ones_hi = (lax.broadcasted_iota(jnp.int32, (1, 384), 1) % 128) >= HEAD_DIM
    sv_o[0] = jnp.where(ones_hi, 1.0, us[:, 384:768]).astype(BF16)
    wv_o[0] = jnp.where(ones_hi, 1.0, us[:, 1152:1536]).astype(BF16)

    ug = _dot(hb, w_ref[:, 3840:4224])
    gt_o[0] = _sigmoid(ug)
    uz = _dot(hb, w_ref[:, 4224:5248])
    gz_o[0] = (uz * _sigmoid(uz)).astype(BF16)


def _k1_nsa(x, mod3, vec, w, g64):
    b, s, d = x.shape
    ts = min(TS_PROJ, s)
    row = lambda n: pl.BlockSpec((1, ts, n), lambda i, j: (i, j, 0))
    cw = CMP_STRIDE * 128
    grp = pl.BlockSpec((1, NSA_KV_HEADS, ts // CMP_STRIDE, cw), lambda i, j: (i, 0, j, 0))
    return pl.pallas_call(
        _k1_nsa_kernel,
        grid=(b, s // ts),
        in_specs=[row(d), pl.BlockSpec((1, 1, 3 * d), lambda i, j: (i, 0, 0)),
                  _full(vec.shape), _full(w.shape), _full(g64.shape)],
        out_specs=[row(1536), grp, grp, row(384), row(384), row(384), row(384), row(384), row(1024)],
        out_shape=[jax.ShapeDtypeStruct((b, s, 1536), BF16),
                   jax.ShapeDtypeStruct((b, NSA_KV_HEADS, s // CMP_STRIDE, cw), BF16),
                   jax.ShapeDtypeStruct((b, NSA_KV_HEADS, s // CMP_STRIDE, cw), BF16),
                   jax.ShapeDtypeStruct((b, s, 384), BF16),
                   jax.ShapeDtypeStruct((b, s, 384), BF16),
                   jax.ShapeDtypeStruct((b, s, 384), BF16),
                   jax.ShapeDtypeStruct((b, s, 384), BF16),
                   jax.ShapeDtypeStruct((b, s, 384), F32),
                   jax.ShapeDtypeStruct((b, s, 1024), BF16)],
        scratch_shapes=[pltpu.VMEM((2 * NSA_KV_HEADS, ts, 128), F32)],
        compiler_params=_cparams(2),
        name="inproj_nsa",
    )(x, mod3, vec, w, g64)


def _k1_dil_kernel(x_ref, mod_ref, vec_ref, w_ref, g64_ref, *refs):
    outs, st = refs[:3 * N_DIL], refs[3 * N_DIL]
    ts = x_ref.shape[1]
    hb = _modulated(x_ref, mod_ref, vec_ref[0:1, 0:D_MODEL])
    g64 = g64_ref[...]
    inv64 = 1.0 / HEAD_DIM
    u = _dot(hb, w_ref[...])
    for g in range(N_DIL):
        dil = DIL_CFG[g][1]
        qn = _group_norm_chunk(u[:, g * 256:(g + 1) * 256], g64, inv64) * vec_ref[3:4, 0:256]
        kn = _group_norm_chunk(u[:, 768 + g * 256:768 + (g + 1) * 256], g64, inv64) * vec_ref[4:5, 0:256]
        vals = (qn, kn, u[:, 1536 + g * 256:1536 + (g + 1) * 256])
        for j, val in enumerate(vals):
            o_ref = outs[3 * g + j]
            if dil == 1:
                o_ref[0] = val.astype(BF16)
                continue
            for h in range(2):
                st[j, h] = val[:, h * 128:(h + 1) * 128]
            for r in range(dil):
                for h in range(2):
                    sl = slice(r * 256 + h * 128, r * 256 + (h + 1) * 128)
                    o_ref[0, :, sl] = st[j, h, pl.ds(r, ts // dil, stride=dil), :].astype(BF16)


def _k1_dil(x, mod3, vec, w, g64):
    b, s, d = x.shape
    ts = min(TS_PROJ, s)
    row = lambda n: pl.BlockSpec((1, ts, n), lambda i, j: (i, j, 0))
    specs, shapes = [], []
    for g in range(N_DIL):
        dil = DIL_CFG[g][1]
        for _ in range(3):
            specs.append(pl.BlockSpec((1, ts // dil, dil * 256), lambda i, j: (i, j, 0)))
            shapes.append(jax.ShapeDtypeStruct((b, s // dil, dil * 256), BF16))
    return pl.pallas_call(
        _k1_dil_kernel,
        grid=(b, s // ts),
        in_specs=[row(d), pl.BlockSpec((1, 1, 3 * d), lambda i, j: (i, 0, 0)),
                  _full(vec.shape), _full(w.shape), _full(g64.shape)],
        out_specs=specs,
        out_shape=shapes,
        scratch_shapes=[pltpu.VMEM((3, 2, ts, 128), F32)],
        compiler_params=_cparams(2),
        name="inproj_dil",
    )(x, mod3, vec, w, g64)


def _compress_kernel(xk_ref, xv_ref, wk_ref, wv_ref, pek_ref, pev_ref, gain_ref, g128_ref, kc_o, vc_o):
    nch = xk_ref.shape[2]

    def comp(x_ref, w_ref, pe_ref):
        x = x_ref[0, 0]
        lo = _dot(x, w_ref[0])
        hi = _dot(x, w_ref[1])
        pec = _dot_hl(pe_ref[0], w_ref[0]) + _dot_hl(pe_ref[1], w_ref[1])
        return lo + pltpu.roll(hi, nch - 1, 0) + pec[0:1, :]

    kc = comp(xk_ref, wk_ref, pek_ref)
    ss = _dot_hl(kc * kc, g128_ref[...])
    kc_o[0, 0] = (kc * lax.rsqrt(ss * (1.0 / HEAD_DIM) + NORM_EPS) * gain_ref[...]).astype(BF16)
    vc_o[0, 0] = comp(xv_ref, wv_ref, pev_ref).T.astype(BF16)


def _compress(ck, cv, wk, wv, pek, pev, gain, g128):
    b, g, nch, _ = ck.shape
    xk, xv = ck, cv
    blk = pl.BlockSpec((1, 1, nch, CMP_STRIDE * 128), lambda i, j: (i, j, 0, 0))
    oblk = pl.BlockSpec((1, 1, nch, 128), lambda i, j: (i, j, 0, 0))
    tblk = pl.BlockSpec((1, 1, 128, nch), lambda i, j: (i, j, 0, 0))
    return pl.pallas_call(
        _compress_kernel,
        grid=(b, g),
        in_specs=[blk, blk, _full(wk.shape), _full(wv.shape), _full(pek.shape), _full(pev.shape),
                  _full(gain.shape), _full(g128.shape)],
        out_specs=[oblk, tblk],
        out_shape=[jax.ShapeDtypeStruct((b, g, nch, 128), BF16), jax.ShapeDtypeStruct((b, g, 128, nch), BF16)],
        compiler_params=_cparams(2),
        name="nsa_compress",
    )(xk, xv, wk, wv, pek, pev, gain, g128)


def _pair_select(lo_half, a, b):
    return jnp.where(lo_half, a, b)


def _nsa_cmp_kernel(slope_ref, q_ref, kc_ref, vct_ref, ovlt_ref, cposc_ref, prow_ref, gt_ref,
                    oc_o, selb_o, any_o, *, tq):
    g = pl.program_id(1)
    qi = pl.program_id(2)
    q4 = q_ref[0]
    kc = kc_ref[0, 0]
    vct = vct_ref[0, 0]
    nch = kc.shape[0]
    t = qi * tq + lax.broadcasted_iota(jnp.int32, (1, tq), 1)
    n_id = lax.broadcasted_iota(jnp.int32, (nch, 1), 0)
    valid = (n_id * CMP_STRIDE + (CMP_LEN - 1)) <= t
    dist = (prow_ref[qi] - cposc_ref[...]) * LOG2E
    gtt = gt_ref[0].T
    row_lo = lax.broadcasted_iota(jnp.int32, (LANES, 1), 0) < HEAD_DIM
    psum = jnp.zeros((nch, tq), F32)
    outs = []
    for hh in range(NSA_HPG):
        s = _dot_nt(kc, q4[:, hh * LANES:(hh + 1) * LANES]) - slope_ref[g * NSA_HPG + hh] * dist
        s = jnp.where(valid, s, -jnp.inf)
        mx = jnp.max(s, axis=0, keepdims=True)
        mx = jnp.where(mx == -jnp.inf, 0.0, mx)
        e = jnp.exp2(s - mx)
        den = jnp.maximum(jnp.sum(e, axis=0, keepdims=True), TINY)
        p = e * (1.0 / den)
        psum = psum + p
        outs.append(_dot(vct, p.astype(BF16)) * gtt[3 * hh:3 * hh + 1, :])
    oc_o[0, :, 0:LANES] = jnp.where(row_lo, outs[0], outs[1]).T
    oc_o[0, :, LANES:2 * LANES] = jnp.where(row_lo, outs[2], outs[3]).T

    hi, lo = _split_hl(psum)
    imp = _dot(ovlt_ref[...], hi) + _dot(ovlt_ref[...], lo)
    blk = lax.broadcasted_iota(jnp.int32, (LANES, 1), 0)
    cur = t >> 6
    forced = (blk == 0) | (blk == cur) | (blk == cur - 1)
    allowed = blk <= cur
    score = jnp.where(allowed, imp + jnp.where(forced, FORCE_BONUS, 0.0), -jnp.inf)
    blk_f = blk.astype(F32)

    def pick(_, carry):
        sc, sel = carry
        mx = jnp.max(sc, axis=0, keepdims=True)
        idx = jnp.min(jnp.where(sc == mx, blk_f, float(LANES)), axis=0, keepdims=True)
        hit = blk_f == idx
        return jnp.where(hit, -jnp.inf, sc), jnp.where(hit, 1.0, sel)

    _, sel = lax.fori_loop(0, SEL_TOPN, pick, (score, jnp.zeros((LANES, tq), F32)))
    chosen = jnp.where((sel > 0.5) & allowed, 1.0, 0.0).T
    selb_o[0, 0] = ((chosen - 1.0) * SEL_OFF).astype(BF16)
    used = jnp.max(chosen, axis=0, keepdims=True)
    any_o[0, 0, 0] = jnp.broadcast_to(used, (8, LANES))


def _nsa_cmp(slopes, q, kc, vct, ovlt, cposc, pos_rows, gates):
    b, s, _ = q.shape
    nch = kc.shape[2]
    tq = pos_rows.shape[2]
    kern = functools.partial(_nsa_cmp_kernel, tq=tq)
    return pl.pallas_call(
        kern,
        grid=(b, NSA_KV_HEADS, s // tq),
        in_specs=[pl.BlockSpec(memory_space=pltpu.SMEM),
                  pl.BlockSpec((1, tq, 4 * LANES), lambda i, g, j: (i, j, g)),
                  pl.BlockSpec((1, 1, nch, LANES), lambda i, g, j: (i, g, 0, 0)),
                  pl.BlockSpec((1, 1, LANES, nch), lambda i, g, j: (i, g, 0, 0)),
                  _full(ovlt.shape), _full(cposc.shape), _full(pos_rows.shape),
                  pl.BlockSpec((1, tq, LANES), lambda i, g, j: (i, j, g))],
        out_specs=[pl.BlockSpec((1, tq, 2 * LANES), lambda i, g, j: (i, j, g)),
                   pl.BlockSpec((1, 1, tq, LANES), lambda i, g, j: (i, g, j, 0)),
                   pl.BlockSpec((1, 1, 1, 8, LANES), lambda i, g, j: (i, g, j, 0, 0))],
        out_shape=[jax.ShapeDtypeStruct((b, s, NSA_W), F32),
                   jax.ShapeDtypeStruct((b, NSA_KV_HEADS, s, LANES), BF16),
                   jax.ShapeDtypeStruct((b, NSA_KV_HEADS, s // tq, 8, LANES), F32)],
        compiler_params=_cparams(3),
        name="nsa_cmp_topk",
    )(slopes, q, kc, vct, ovlt, cposc, pos_rows, gates)


def _gqa_kernel(flag_ref, slope_ref, bnd_ref, q_ref, selb_ref, k_ref, v_ref, pos_ref, pcol_ref, gt_ref, o_ref,
                qa_ref, acc_ref, m_ref, *, tq, tk, branch, bounded):
    bi = pl.program_id(0)
    g = pl.program_id(1)
    qi = pl.program_id(2)
    q4 = q_ref[0]
    lane = lax.broadcasted_iota(jnp.int32, (1, LANES), 1)
    lo_half = lane < HEAD_DIM
    pref = pos_ref[qi * (tq // tk)][:, 0:1]
    for hh in range(NSA_HPG):
        sl = slice(hh * tq, (hh + 1) * tq)
        qa_ref[sl, 0:LANES] = q4[:, hh * LANES:(hh + 1) * LANES]
        if branch == 1:
            qa_ref[sl, LANES:2 * LANES] = selb_ref[0, 0]
        if bounded:
            own = (pcol_ref[...] - pref) * (LOG2E * slope_ref[g * NSA_HPG + hh]) + bnd_ref[0]
            m_ref[sl] = jnp.broadcast_to(own, (tq, LANES))
    acc_ref[...] = jnp.zeros_like(acc_ref)
    if not bounded:
        m_ref[...] = jnp.full_like(m_ref, NEG_MASK)
    rows = qi * tq + lax.broadcasted_iota(jnp.int32, (tq, 1), 0)
    col0 = lax.broadcasted_iota(jnp.int32, (1, tk), 1)
    krow = lax.broadcasted_iota(jnp.int32, (tk, 1), 0)
    per = tq // tk
    nrep = tk // LANES
    flag0 = ((bi * NSA_KV_HEADS + g) * pl.num_programs(2) + qi) * SEL_TILE_STRIDE

    def scores(kt_true, masked, r_lo, r_hi):
        kt = jnp.maximum(kt_true, 0)
        off = pl.multiple_of(kt * tk, tk)
        k = k_ref[0, pl.ds(off, tk), :]
        if branch == 1:
            blk = (kt * tk + krow) >> 6
            onehot = jnp.where(lane == blk, 1.0, 0.0).astype(BF16)
            s_all = _dot_nt(qa_ref[...], jnp.concatenate([k, onehot], axis=1))
            parts = [s_all[hh * tq:(hh + 1) * tq] for hh in range(NSA_HPG)]
        else:
            parts = [_dot_nt(qa_ref[hh * tq + r_lo:hh * tq + r_hi, 0:LANES], k) for hh in range(NSA_HPG)]
        rel = (pos_ref[kt] - pref) * LOG2E
        if masked:
            cols = kt_true * tk + col0
            d = rows[r_lo:r_hi] - cols
            keep = d >= 0
            if branch == 2:
                keep = keep & (d < WIN) & (cols >= 0)
        out = []
        for hh in range(NSA_HPG):
            s = parts[hh] + slope_ref[g * NSA_HPG + hh] * rel
            if masked:
                s = jnp.where(keep, s, NEG_MASK)
            out.append(s)
        return out

    def sweep(fn):
        if branch == 1:
            def body(j, carry):
                @pl.when(flag_ref[flag0 + j] != 0)
                def _():
                    fn(j, False, 0, tq)
                return carry
            lax.fori_loop(0, qi * per, body, 0)
            for dd in range(per):
                fn(qi * per + dd, True, 0, tq)
        else:
            for c in range(-(WIN // tk), per):
                r_lo = max(0, c * tk)
                r_hi = min(tq, -(-(c * tk + tk + WIN - 1) // 8) * 8)
                fn(qi * per + c, True, r_lo, r_hi)

    def row_max(kt_true, masked, r_lo, r_hi):
        for hh, s in enumerate(scores(kt_true, masked, r_lo, r_hi)):
            sl = slice(hh * tq + r_lo, hh * tq + r_hi)
            m_ref[sl] = jnp.maximum(m_ref[sl], _lane_max(s))

    def accumulate(kt_true, masked, r_lo, r_hi):
        off = pl.multiple_of(jnp.maximum(kt_true, 0) * tk, tk)
        v = v_ref[0, pl.ds(off, tk), :]
        for hh, s in enumerate(scores(kt_true, masked, r_lo, r_hi)):
            sl = slice(hh * tq + r_lo, hh * tq + r_hi)
            p = jnp.exp2(s - jnp.concatenate([m_ref[sl]] * nrep, axis=1))
            acc_ref[sl] += _dot(p.astype(BF16), v)

    if not bounded:
        sweep(row_max)
        for hh in range(NSA_HPG):
            sl = slice(hh * tq, (hh + 1) * tq)
            m_ref[sl] = jnp.broadcast_to(jnp.max(m_ref[sl], axis=-1, keepdims=True), (tq, LANES))
    sweep(accumulate)

    gt = gt_ref[0]
    for pr in range(2):
        res = []
        for x in range(2):
            hh = 2 * pr + x
            a = acc_ref[hh * tq:(hh + 1) * tq]
            r = pltpu.roll(a, HEAD_DIM, 1)
            o = a / r if x == 0 else r / a
            res.append(o * gt[:, 3 * hh + branch:3 * hh + branch + 1])
        o_ref[0, :, pr * LANES:(pr + 1) * LANES] = _pair_select(lo_half, res[0], res[1])


def _nsa_gqa(flags, slopes, bound, safe, q, selb, k, v, pos2d, pos_col, gates, branch):
    b, s, _ = q.shape
    tq = min(TQ_NSA, s)
    tk = pos2d.shape[2]

    def call(bounded):
        kern = functools.partial(_gqa_kernel, tq=tq, tk=tk, branch=branch, bounded=bounded)
        smem = pl.BlockSpec(memory_space=pltpu.SMEM)
        name = ("nsa_sel_attn" if branch == 1 else "nsa_win_attn") + ("_bounded" if bounded else "")
        return pl.pallas_call(
            kern,
            grid=(b, NSA_KV_HEADS, s // tq),
            in_specs=[smem, smem, smem,
                      pl.BlockSpec((1, tq, 4 * LANES), lambda i, g, j: (i, j, g)),
                      pl.BlockSpec((1, 1, tq, LANES), lambda i, g, j: (i, g, j, 0)),
                      pl.BlockSpec((1, s, LANES), lambda i, g, j: (i, 0, g)),
                      pl.BlockSpec((1, s, LANES), lambda i, g, j: (i, 0, g)),
                      _full(pos2d.shape),
                      pl.BlockSpec((tq, 1), lambda i, g, j: (j, 0)),
                      pl.BlockSpec((1, tq, LANES), lambda i, g, j: (i, j, g))],
            out_specs=pl.BlockSpec((1, tq, 2 * LANES), lambda i, g, j: (i, j, g)),
            out_shape=jax.ShapeDtypeStruct((b, s, NSA_W), F32),
            scratch_shapes=[pltpu.VMEM((NSA_HPG * tq, 2 * LANES), BF16),
                            pltpu.VMEM((NSA_HPG * tq, LANES), F32),
                            pltpu.VMEM((NSA_HPG * tq, LANES), F32)],
            compiler_params=_cparams(3),
            name=name,
        )(flags, slopes, bound, q, selb, k, v, pos2d, pos_col, gates)

    return lax.cond(safe, lambda: call(True), lambda: call(False))


def _dil_kernel(slope_ref, bnd_ref, q_ref, k_ref, v_ref, pc_ref, pr_ref, o_ref, lse_ref,
                *, t, span, grp, bounded):
    i = pl.program_id(2)
    q4 = q_ref[0]
    lane = lax.broadcasted_iota(jnp.int32, (1, LANES), 1)
    lo_half = lane < HEAD_DIM
    per = t // LANES
    prev = jnp.maximum(i * per - 1, 0)
    o_prev = pl.multiple_of(prev * LANES, LANES)
    o_cur = pl.multiple_of(i * t, t)
    kk = jnp.concatenate([k_ref[0, pl.ds(o_prev, LANES), :], k_ref[0, pl.ds(o_cur, t), :]], axis=0)
    vv = jnp.concatenate([v_ref[0, pl.ds(o_prev, LANES), :], v_ref[0, pl.ds(o_cur, t), :]], axis=0)
    pk = jnp.concatenate([pr_ref[0, prev]] + [pr_ref[0, i * per + c] for c in range(per)], axis=1)
    dist = (pc_ref[0] - pk) * LOG2E
    rows = i * t + lax.broadcasted_iota(jnp.int32, (t, 1), 0)
    cols = i * t - LANES + lax.broadcasted_iota(jnp.int32, (1, t + LANES), 1)
    d = rows - cols
    valid = (cols >= 0) & (d >= 0) & (d <= span)
    zero = jnp.zeros((t, LANES), BF16)
    ones = jnp.ones((t + LANES, LANES), BF16)
    for p in range(2):
        qp = q4[:, p * LANES:(p + 1) * LANES]
        kp = kk[:, p * LANES:(p + 1) * LANES]
        vx = jnp.concatenate([vv[:, p * LANES:(p + 1) * LANES], ones], axis=1)
        res = []
        for x in range(2):
            qx = jnp.where(lo_half, qp, zero) if x == 0 else jnp.where(lo_half, zero, qp)
            s = _dot_nt(qx, kp) - slope_ref[grp * DIL_HEADS + 2 * p + x] * dist
            s = jnp.where(valid, s, -jnp.inf)
            mx = bnd_ref[0] if bounded else jnp.max(s, axis=-1, keepdims=True)
            acc = _dot(jnp.exp2(s - mx).astype(BF16), vx)
            den = acc[:, LANES:2 * LANES]
            res.append((acc[:, 0:LANES] / den, mx * LN2 + jnp.log(den)))
        sl = slice(p * LANES, (p + 1) * LANES)
        o_ref[0, :, sl] = _pair_select(lo_half, res[0][0], res[1][0])
        lse_ref[0, :, sl] = jnp.where(lo_half, res[0][1], res[1][1])


def _dilated(slopes, bound, safe, qv, kv, vv, pos_f, grp):
    window, dil = DIL_CFG[grp]
    b, sub, wd = qv.shape
    w = wd // dil
    t = min(T_DIL, sub)
    span = window // dil
    assert span <= LANES and t % LANES == 0
    pres = pos_f.reshape(sub, dil).T
    pc = pres.reshape(dil, sub, 1)
    pr = pres.reshape(dil, sub // LANES, 1, LANES)
    shp = jax.ShapeDtypeStruct((b, sub, wd), F32)
    smem = pl.BlockSpec(memory_space=pltpu.SMEM)

    def call(bounded):
        kern = functools.partial(_dil_kernel, t=t, span=span, grp=grp, bounded=bounded)
        return pl.pallas_call(
            kern,
            grid=(b, dil, sub // t),
            in_specs=[smem, smem,
                      pl.BlockSpec((1, t, w), lambda i, r, j: (i, j, r)),
                      pl.BlockSpec((1, sub, w), lambda i, r, j: (i, 0, r)),
                      pl.BlockSpec((1, sub, w), lambda i, r, j: (i, 0, r)),
                      pl.BlockSpec((1, t, 1), lambda i, r, j: (r, j, 0)),
                      pl.BlockSpec((1, sub // LANES, 1, LANES), lambda i, r, j: (r, 0, 0, 0))],
            out_specs=[pl.BlockSpec((1, t, w), lambda i, r, j: (i, j, r)),
                       pl.BlockSpec((1, t, w), lambda i, r, j: (i, j, r))],
            out_shape=[shp, shp],
            compiler_params=_cparams(3),
            name="dilated_attn_g%d%s" % (grp, "_bounded" if bounded else ""),
        )(slopes, bound, qv, kv, vv, pc, pr)

    return lax.cond(safe, lambda: call(True), lambda: call(False))


def _out_odd_kernel(x_ref, mod_ref, oc_ref, os_ref, ow_ref, d0_ref, d1_ref, d2_ref,
                    l0_ref, l1_ref, l2_ref, gz_ref, w_ref, o_ref, st):
    ts = x_ref.shape[1]
    gz = gz_ref[0].astype(F32)
    nsa = oc_ref[0] + os_ref[0] + ow_ref[0]
    m1 = (nsa * gz[:, 0:NSA_W]).astype(BF16)

    def token_order(ref, g, slot):
        dil = DIL_CFG[g][1]
        if dil == 1:
            return ref[0]
        for r in range(dil):
            for h in range(2):
                st[slot, h, pl.ds(r, ts // dil, stride=dil), :] = ref[0, :, r * DIL_W + h * 128:r * DIL_W + (h + 1) * 128]
        return jnp.concatenate([st[slot, 0], st[slot, 1]], axis=1)

    d0, d1, d2 = token_order(d0_ref, 0, 0), token_order(d1_ref, 1, 0), token_order(d2_ref, 2, 1)
    l0, l1, l2 = token_order(l0_ref, 0, 2), token_order(l1_ref, 1, 2), token_order(l2_ref, 2, 3)
    mx = jnp.maximum(jnp.maximum(l0, l1), l2)
    e0, e1, e2 = jnp.exp(l0 - mx), jnp.exp(l1 - mx), jnp.exp(l2 - mx)
    dil = (e0 * d0 + e1 * d1 + e2 * d2) / (e0 + e1 + e2)
    m2 = (dil * gz[:, NSA_W:NSA_W + DIL_W]).astype(BF16)
    y = _dot(m1, w_ref[0:NSA_W, :]) + _dot(m2, w_ref[NSA_W:NSA_W + DIL_W, :])
    gate = mod_ref[0][:, 2 * D_MODEL:3 * D_MODEL]
    o_ref[0] = x_ref[0] + gate * y


def _out_odd(x, mod3, oc, os_, ow, dils, lses, gz, w):
    b, s, d = x.shape
    ts = min(TS_PROJ, s)
    row = lambda n: pl.BlockSpec((1, ts, n), lambda i, j: (i, j, 0))
    res = [pl.BlockSpec((1, ts // DIL_CFG[g][1], DIL_CFG[g][1] * DIL_W), lambda i, j: (i, j, 0))
           for g in range(N_DIL)]
    return pl.pallas_call(
        _out_odd_kernel,
        grid=(b, s // ts),
        in_specs=[row(d), pl.BlockSpec((1, 1, 3 * d), lambda i, j: (i, 0, 0)),
                  row(NSA_W), row(NSA_W), row(NSA_W)] + res + res + [row(1024), _full(w.shape)],
        out_specs=row(d),
        out_shape=jax.ShapeDtypeStruct((b, s, d), F32),
        scratch_shapes=[pltpu.VMEM((4, 2, ts, 128), F32)],
        compiler_params=_cparams(2),
        name="outproj_odd",
    )(x, mod3, oc, os_, ow, *dils, *lses, gz, w)


def _pad_cols(w, n):
    return jnp.pad(w, ((0, 0), (0, n - w.shape[1])))


def _pad_vec(v, n=D_MODEL):
    return jnp.pad(v, (0, n - v.shape[0]))


def _group_matrix(sizes, total):
    m = np.zeros((total, total), np.float32)
    off = 0
    for sz, on in sizes:
        if on:
            m[off:off + sz, off:off + sz] = 1.0
        off += sz
    return jnp.asarray(m, BF16)


def _swap_halves(w):
    h = w.shape[-1] // 2
    return jnp.concatenate([w[..., h:], w[..., :h]], axis=-1)


def _column_blocks(segments):
    blocks, cur, room = [], [], LANES
    for src, width in segments:
        while width:
            n = min(width, room)
            cur.append((src, n))
            src = None if src is None else src + n
            width -= n
            room -= n
            if room == 0:
                blocks.append(cur)
                cur, room = [], LANES
    assert not cur
    return blocks


def _pack_cols_kernel(w_ref, o_ref, *, blocks):
    _, rows, n_src = w_ref.shape
    for bi, pieces in enumerate(blocks):
        parts = []
        for src, width in pieces:
            if src is None:
                parts.append(jnp.zeros((rows, width), F32))
                continue
            a0 = src // LANES * LANES
            a1 = min(-(-(src + width) // LANES) * LANES, n_src)
            parts.append(w_ref[0, :, a0:a1][:, src - a0:src - a0 + width])
        blk = parts[0] if len(parts) == 1 else jnp.concatenate(parts, axis=1)
        o_ref[:, bi * LANES:(bi + 1) * LANES] = blk.astype(BF16)


def _pack_cols(w, layer, segments):
    blocks = _column_blocks(segments)
    _, d, n_src = w.shape
    tr = 256
    return pl.pallas_call(
        functools.partial(_pack_cols_kernel, blocks=blocks),
        grid=(d // tr,),
        in_specs=[pl.BlockSpec((1, tr, n_src), lambda i: (layer, i, 0))],
        out_specs=pl.BlockSpec((tr, len(blocks) * LANES), lambda i: (i, 0)),
        out_shape=jax.ShapeDtypeStruct((d, len(blocks) * LANES), BF16),
        compiler_params=_cparams(1),
        name="pack_weight_columns",
    )(w)


EVEN_SEGMENTS = ((0, 1536), (1536, 512), (2464, 512), (2048, 256), (2304, 128),
                 (None, 64), (2432, 32), (None, 32), (None, 64), (2448, 16), (2432, 16), (None, 32))


def _pack_even(w_in_all, layer, norm_g, sb_qn, sb_kn, qa_g, wq_up, kva_g, wkv_up, qn, kn):
    w = _pack_cols(w_in_all, layer, EVEN_SEGMENTS)
    wq3 = wq_up.reshape(MLA_Q_RANK, MLA_HEADS, MLA_NOPE + MLA_ROPE)
    zq = jnp.zeros((MLA_Q_RANK, MLA_HEADS, 32), wq_up.dtype)
    wq = jnp.concatenate([wq3, zq], axis=-1).reshape(MLA_Q_RANK, MLA_HEADS * 128).astype(BF16)
    wqs = jnp.concatenate([jnp.zeros((MLA_Q_RANK, MLA_HEADS, 64), wq_up.dtype),
                           _swap_halves(wq3[..., MLA_NOPE:]), zq], axis=-1)
    wqs = wqs.reshape(MLA_Q_RANK, MLA_HEADS * 128).astype(BF16)
    wkv3 = wkv_up.reshape(MLA_KV_RANK, MLA_HEADS, MLA_NOPE + MLA_V)
    wk = jnp.concatenate([wkv3[..., :MLA_NOPE], jnp.zeros_like(wkv3[..., :MLA_NOPE])], axis=-1)
    wkv = jnp.concatenate([wk.reshape(MLA_KV_RANK, MLA_HEADS * 128),
                           wkv3[..., MLA_NOPE:].reshape(MLA_KV_RANK, MLA_HEADS * MLA_V)], axis=1).astype(BF16)
    z32 = jnp.zeros((32,), F32)
    z64 = jnp.zeros((64,), F32)
    scale = (MLA_NOPE + MLA_ROPE) ** -0.5 * LOG2E
    qg = jnp.tile(jnp.concatenate([qn, z32]), MLA_HEADS) * scale
    qgs = jnp.tile(jnp.concatenate([z64, _swap_halves(qn[MLA_NOPE:]), z32]), MLA_HEADS) * scale
    kg = jnp.tile(jnp.concatenate([kn[:MLA_NOPE], z64]), MLA_HEADS)
    krg = jnp.concatenate([z64, kn[MLA_NOPE:], z32])
    krgs = jnp.concatenate([z64, _swap_halves(kn[MLA_NOPE:]), z32])
    cnt = jnp.tile(jnp.concatenate([jnp.full((64,), 1.0 / 64), jnp.full((32,), 1.0 / 32), jnp.ones((32,))]),
                   MLA_HEADS).astype(F32)
    rows = [norm_g, _pad_vec(jnp.tile(sb_qn, SB_HEADS) * (LOG2E * HEAD_DIM ** -0.5)), _pad_vec(jnp.tile(sb_kn, SB_HEADS)),
            _pad_vec(qa_g), _pad_vec(kva_g), qg, qgs, kg, _pad_vec(krg), _pad_vec(krgs), cnt]
    rows += [jnp.zeros((D_MODEL,), F32)] * (16 - len(rows))
    return w, wq, wqs, wkv, jnp.stack(rows).astype(F32)


def _padded_heads(start, n):
    return tuple(seg for h in range(n) for seg in ((start + h * HEAD_DIM, HEAD_DIM), (None, HEAD_DIM)))


NSA_SEGMENTS = (_padded_heads(0, NSA_HEADS)
                + tuple(seg for c in range(6) for seg in _padded_heads(768 + c * NSA_KV_W, NSA_KV_HEADS))
                + tuple(seg for g in range(NSA_KV_HEADS)
                        for seg in ((1920 + g * NSA_HPG * 3, NSA_HPG * 3), (None, LANES - NSA_HPG * 3)))
                + ((1956, NSA_W), (5028, DIL_W)))
DIL_SEGMENTS = ((2724, 3 * N_DIL * DIL_W),)


def _pack_odd(w_in_all, layer, norm_g, nsa_qn, nsa_kn, dil_qn, dil_kn):
    w_nsa = _pack_cols(w_in_all, layer, NSA_SEGMENTS)
    w_dil = _pack_cols(w_in_all, layer, DIL_SEGMENTS)
    z64 = jnp.zeros((64,), F32)
    n = NSA_HEADS * 128
    rows = [_pad_vec(norm_g, n),
            jnp.tile(jnp.concatenate([nsa_qn * (LOG2E * HEAD_DIM ** -0.5), z64]), NSA_HEADS),
            _pad_vec(jnp.tile(jnp.concatenate([nsa_kn, z64]), NSA_KV_HEADS), n),
            _pad_vec(jnp.tile(dil_qn, DIL_HEADS) * (LOG2E * HEAD_DIM ** -0.5), n),
            _pad_vec(jnp.tile(dil_kn, DIL_HEADS), n)]
    rows += [jnp.zeros((n,), F32)] * (8 - len(rows))
    return w_nsa, w_dil, jnp.stack(rows).astype(F32)


def _pack_compress(w, pe, double):
    w3 = w.reshape(CMP_LEN, HEAD_DIM, HEAD_DIM)
    w3 = jnp.concatenate([w3, jnp.zeros_like(w3)], axis=1)
    w3 = jnp.concatenate([w3, w3 if double else jnp.zeros_like(w3)], axis=2)
    wp = w3.reshape(2, CMP_STRIDE * 128, 128).astype(BF16)
    pe2 = jnp.concatenate([pe, jnp.zeros_like(pe)], axis=1).reshape(2, 1, CMP_STRIDE * 128)
    pe2 = jnp.broadcast_to(pe2, (2, 8, CMP_STRIDE * 128)).astype(F32)
    return wp, pe2


def _normed_len(gain, sizes):
    tot, off = 0.0, 0
    for n in sizes:
        tot = tot + n * jnp.max(jnp.square(gain[off:off + n]))
        off += n
    return jnp.sqrt(tot)


def _mla_logit_bound(qn, kn):
    sizes = (MLA_NOPE, MLA_ROPE)
    scale = (MLA_NOPE + MLA_ROPE) ** -0.5 * LOG2E
    return (_normed_len(qn, sizes) * _normed_len(kn, sizes) * (scale * ROUNDING_MARGIN)).reshape(1).astype(F32)


def _alibi_slopes(n):
    return 2.0 ** (-8.0 * jnp.arange(1, n + 1, dtype=jnp.float32) / n)


def kernel(x, c, positions, ada_w, ada_b, norm_g, ev_w_in, ev_w_out, sb_qn, sb_kn, mla_qa_g, mla_wq_up,
           mla_kva_g, mla_wkv_up, mla_qn, mla_kn, od_w_in, od_w_out, nsa_qn, nsa_kn, nsa_cmp_wk, nsa_cmp_wv,
           nsa_cmp_pe_k, nsa_cmp_pe_v, dil_qn, dil_kn):
    b, s, d = x.shape
    depth = ada_w.shape[0]
    pos_f = positions.astype(F32)

    inv_freq = ROPE_BASE ** (-jnp.arange(0, MLA_ROPE, 2, dtype=F32) / MLA_ROPE)
    ang = pos_f[:, None] * inv_freq[None, :]
    cos, sin = jnp.cos(ang), jnp.sin(ang)
    cos128 = jnp.concatenate([jnp.ones((s, 64), F32), cos, cos, jnp.zeros((s, 32), F32)], axis=1)
    sin128 = jnp.concatenate([jnp.zeros((s, 64), F32), -sin, sin, jnp.zeros((s, 32), F32)], axis=1)
    nsa_slopes = _alibi_slopes(NSA_HEADS)
    dil_slopes = _alibi_slopes(N_DIL * DIL_HEADS)
    nch = s // CMP_STRIDE
    chunk_sum = pos_f.reshape(nch, CMP_STRIDE).sum(axis=1)
    cpos = ((chunk_sum + jnp.roll(chunk_sum, -1)) / CMP_LEN).reshape(nch, 1)
    n_sel = s // SEL_LEN
    cst = np.arange(nch)[:, None] * CMP_STRIDE
    jst = np.arange(LANES)[None, :] * SEL_LEN
    ovl = ((cst <= jst + SEL_LEN - 1) & (cst + CMP_LEN - 1 >= jst) & (np.arange(LANES)[None, :] < n_sel))
    ovlt = jnp.asarray(ovl.astype(np.float32).T, BF16)
    pos_col = pos_f.reshape(s, 1)
    tqc = min(TQ_CMP, s)
    pos_rows = pos_f.reshape(s // tqc, 1, tqc)
    tqn = min(TQ_NSA, s)
    tkn = min(TK_NSA, tqn)
    assert tkn == TK_NSA
    pos2d = pos_f.reshape(s // tkn, 1, tkn)
    pos2d_win = pos_f.reshape(s // TK_WIN, 1, TK_WIN)
    q_first = pos_f[::tqn]
    k_last = pos_f[jnp.minimum((jnp.arange(SEL_TILE_STRIDE) + 1) * tkn - 1, s - 1)]
    group_slope = jnp.min(nsa_slopes.reshape(NSA_KV_HEADS, NSA_HPG), axis=1) * LOG2E
    sel_dead = group_slope[:, None, None] * (q_first[:, None] - k_last[None, :])[None] > F32_ZERO_EXP
    pos_sorted = jnp.all(pos_f[1:] >= pos_f[:-1])

    g64 = _group_matrix([(64, 1)] * 4, 256)
    gm = _group_matrix([(64, 1), (32, 1), (32, 0)] * 2, 256)
    tks = min(TK_SB, min(TQ_SB, s))
    tri = jnp.asarray(np.tril(np.ones((tks, tks), np.float32)), BF16)

    c8 = jnp.pad(c, ((0, 8 - b), (0, 0)))
    mod_all = _modulation(c8, ada_w, ada_b)

    for layer in range(depth):
        j = layer // 2
        mod3 = mod_all[layer, :b].reshape(b, 1, 3 * d)
        if layer % 2 == 0:
            w, wq, wqs, wkv, vec = _pack_even(ev_w_in, j, norm_g[layer], sb_qn[j], sb_kn[j], mla_qa_g[j],
                                              mla_wq_up[j], mla_kva_g[j], mla_wkv_up[j], mla_qn[j], mla_kn[j])
            sbq, sbk, sbv, gz, mq, mk, mv = _k1_even(x, mod3, vec, w, wq, wqs, wkv, g64, gm, cos128, sin128)
            sb_bound = (_normed_len(sb_qn[j], (HEAD_DIM,)) * _normed_len(sb_kn[j], (HEAD_DIM,))
                        * (HEAD_DIM ** -0.5 * LOG2E * ROUNDING_MARGIN)).reshape(1).astype(F32)
            o_sb = _sb_attention(sbq, sbk, sbv, tri, sb_bound, sb_bound[0] <= SAFE_LOGIT_BOUND)
            o_mla = _mla_attention(mq, mk, mv, _mla_logit_bound(mla_qn[j], mla_kn[j]))
            x = _out_even(x, mod3, o_sb, o_mla, gz, ev_w_out[j].astype(BF16))
        else:
            w_nsa, w_dil, vec = _pack_odd(od_w_in, j, norm_g[layer], nsa_qn[j], nsa_kn[j], dil_qn[j], dil_kn[j])
            q, ck, cv, sk, sv, wk, wv, gates, gz = _k1_nsa(x, mod3, vec, w_nsa, g64)
            dqkv = _k1_dil(x, mod3, vec, w_dil, g64)
            wck, pek = _pack_compress(nsa_cmp_wk[j], nsa_cmp_pe_k[j], False)
            wcv, pev = _pack_compress(nsa_cmp_wv[j], nsa_cmp_pe_v[j], True)
            kgain = jnp.concatenate([nsa_kn[j], jnp.zeros((64,), F32)]).reshape(1, 128)
            kc, vc = _compress(ck, cv, wck, wcv, pek, pev, kgain, g64[0:128, 0:128])
            o_c, selb, used = _nsa_cmp(nsa_slopes, q, kc, vc, ovlt, cpos, pos_rows, gates)
            flags = used[:, :, :, 0, :].reshape(b, NSA_KV_HEADS, s // tqn, tqn // min(TQ_CMP, s),
                                                SEL_TILE_STRIDE, LANES // SEL_TILE_STRIDE).max(axis=(3, 5))
            nsa_bound = (_normed_len(nsa_qn[j], (HEAD_DIM,)) * _normed_len(nsa_kn[j], (HEAD_DIM,))
                         * (HEAD_DIM ** -0.5 * LOG2E * ROUNDING_MARGIN)).reshape(1).astype(F32)
            safe = (nsa_bound[0] <= SAFE_LOGIT_BOUND) & pos_sorted
            flags = (flags > 0) & ~(safe & sel_dead[None])
            flags = flags.astype(jnp.int32).reshape(-1)
            o_s = _nsa_gqa(flags, nsa_slopes, nsa_bound, safe, q, selb, sk, sv, pos2d, pos_col, gates, 1)
            o_w = _nsa_gqa(flags, nsa_slopes, nsa_bound, safe, q, selb, wk, wv, pos2d_win, pos_col, gates, 2)
            dil_bound = (_normed_len(dil_qn[j], (HEAD_DIM,)) * _normed_len(dil_kn[j], (HEAD_DIM,))
                         * (HEAD_DIM ** -0.5 * LOG2E * ROUNDING_MARGIN)).reshape(1).astype(F32)
            dil_safe = (dil_bound[0] <= SAFE_LOGIT_BOUND) & pos_sorted
            dils, lses = [], []
            for g in range(N_DIL):
                o, lse = _dilated(dil_slopes, dil_bound, dil_safe, dqkv[3 * g], dqkv[3 * g + 1], dqkv[3 * g + 2],
                                  pos_f, g)
                dils.append(o)
                lses.append(lse)
            x = _out_odd(x, mod3, o_c, o_s, o_w, dils, lses, gz, od_w_out[j].astype(BF16))
    return x
```

```python
import functools

import numpy as np
import jax
import jax.numpy as jnp
from jax import lax
from jax.experimental import pallas as pl
from jax.experimental.pallas import tpu as pltpu

F32 = jnp.float32
BF16 = jnp.bfloat16

D_MODEL = 1024
HEAD_DIM = 64
NORM_EPS = 1e-6
TINY = 1e-30
SB_HEADS = 8
MLA_HEADS = 8
MLA_Q_RANK = 256
MLA_KV_RANK = 128
MLA_NOPE = 64
MLA_ROPE = 32
MLA_V = 64
ROPE_BASE = 10000.0
NSA_HEADS = 12
NSA_KV_HEADS = 3
NSA_HPG = 4
CMP_LEN = 32
CMP_STRIDE = 16
SEL_LEN = 64
SEL_TOPN = 16
WIN = 512
FORCE_BONUS = 1e3
DIL_CFG = ((128, 1), (512, 4), (2048, 16))
N_DIL = 3
DIL_HEADS = 4
SB_W = SB_HEADS * HEAD_DIM
MLA_OUT = MLA_HEADS * MLA_V
NSA_W = NSA_HEADS * HEAD_DIM
NSA_KV_W = NSA_KV_HEADS * HEAD_DIM
DIL_W = DIL_HEADS * HEAD_DIM

LANES = 128
MXU_DIM = 256
NEG_MASK = -1e30
LOG2E = 1.4426950408889634
LN2 = 0.6931471805599453
F32_ZERO_EXP = 150.0
SAFE_LOGIT_BOUND = 50.0
ROUNDING_MARGIN = 1.02
SEL_OFF = 2.0 ** 30
VMEM_LIMIT = 56 * 1024 * 1024

TS_PROJ = 512
TQ_SB = 1024
TK_SB = 256
TQ_MLA = 1024
TD_MLA = 512
TQ_CMP = 512
TQ_NSA = 512
TK_NSA = 512
TK_WIN = 256
SEL_TILE_STRIDE = LANES * SEL_LEN // TK_NSA
T_DIL = 256


def _dot(a, b):
    return jnp.dot(a, b, preferred_element_type=F32)


def _dot_nt(a, b):
    return lax.dot_general(a, b, (((1,), (1,)), ((), ())), preferred_element_type=F32)


def _split_hl(a):
    hi = a.astype(BF16)
    lo = (a - hi.astype(F32)).astype(BF16)
    return hi, lo


def _dot_hl(a, b):
    hi, lo = _split_hl(a)
    return _dot(hi, b) + _dot(lo, b)


def _sigmoid(z):
    return 1.0 / (1.0 + jnp.exp(-z))


def _cparams(n_axes):
    return pltpu.CompilerParams(dimension_semantics=("arbitrary",) * n_axes,
                                vmem_limit_bytes=VMEM_LIMIT)


def _full(shape):
    n = len(shape)
    return pl.BlockSpec(shape, lambda *a, _n=n: (0,) * _n)


def _mod_kernel(c_ref, w_ref, b_ref, o_ref):
    c = c_ref[...]
    a = c * _sigmoid(c)
    ah, al = _split_hl(a)
    wh, wl = _split_hl(w_ref[0])
    o_ref[0] = _dot(ah, wh) + _dot(ah, wl) + _dot(al, wh) + b_ref[0]


def _modulation(c8, ada_w, ada_b):
    depth, d, n3 = ada_w.shape
    tn = 1024
    return pl.pallas_call(
        _mod_kernel,
        grid=(depth, n3 // tn),
        in_specs=[pl.BlockSpec((8, d), lambda l, j: (0, 0)),
                  pl.BlockSpec((1, d, tn), lambda l, j: (l, 0, j)),
                  pl.BlockSpec((1, 1, tn), lambda l, j: (l, 0, j))],
        out_specs=pl.BlockSpec((1, 8, tn), lambda l, j: (l, 0, j)),
        out_shape=jax.ShapeDtypeStruct((depth, 8, n3), F32),
        compiler_params=_cparams(2),
        name="adaln_mod",
    )(c8, ada_w, ada_b.reshape(depth, 1, n3))


def _modulated(x_ref, mod_ref, ng):
    x = x_ref[0]
    mod = mod_ref[0]
    shift = mod[:, 0:D_MODEL]
    scale = mod[:, D_MODEL:2 * D_MODEL]
    ms = jnp.mean(x * x, axis=-1, keepdims=True)
    h = x * lax.rsqrt(ms + NORM_EPS) * ng
    h = h * (1.0 + scale) + shift
    return h.astype(BF16)


def _group_norm_chunk(t, g, inv_cnt):
    ss = _dot_hl(t * t, g)
    return t * lax.rsqrt(ss * inv_cnt + NORM_EPS)


def _row_rms(t, gain):
    ms = jnp.mean(t * t, axis=-1, keepdims=True)
    return t * lax.rsqrt(ms + NORM_EPS) * gain


EV_COLS = 3200


def _k1_even_kernel(x_ref, mod_ref, vec_ref, w_ref, wq_ref, wqs_ref, wkv_ref, g64_ref, gm_ref,
                    cos_ref, sin_ref, sbq_o, sbk_o, sbv_o, gz_o, mq_o, mk_o, mv_o):
    hb = _modulated(x_ref, mod_ref, vec_ref[0:1, :])
    g64 = g64_ref[...]
    gm = gm_ref[...]
    inv64 = 1.0 / HEAD_DIM

    u = _dot(hb, w_ref[:, 0:1536])
    for c in range(2):
        sl = slice(c * 256, (c + 1) * 256)
        qn = _group_norm_chunk(u[:, c * 256:(c + 1) * 256], g64, inv64)
        sbq_o[0, :, sl] = (qn * vec_ref[1:2, sl]).astype(BF16)
        kn = _group_norm_chunk(u[:, 512 + c * 256:512 + (c + 1) * 256], g64, inv64)
        sbk_o[0, :, sl] = (kn * vec_ref[2:3, sl]).astype(BF16)
    sbv_o[0] = u[:, 1024:1536].astype(BF16)

    uz = _dot(hb, w_ref[:, 1536:2560])
    gz_o[0] = (uz * _sigmoid(uz)).astype(BF16)

    ul = _dot(hb, w_ref[:, 2560:3200])
    qlat = _row_rms(ul[:, 0:256], vec_ref[3:4, 0:256]).astype(BF16)
    kvlat = _row_rms(ul[:, 256:384], vec_ref[4:5, 0:128]).astype(BF16)
    krm = ul[:, 384:512]
    krs = ul[:, 512:640]
    cs = cos_ref[...]
    sn = sin_ref[...]
    cs2 = jnp.concatenate([cs, cs], axis=1)
    sn2 = jnp.concatenate([sn, sn], axis=1)

    tq = _dot(qlat, wq_ref[...])
    tqs = _dot(qlat, wqs_ref[...])
    for p in range(4):
        sl = slice(p * 256, (p + 1) * 256)
        tc = tq[:, p * 256:(p + 1) * 256]
        ss = _dot_hl(tc * tc, gm)
        inv = lax.rsqrt(ss * vec_ref[10:11, sl] + NORM_EPS)
        a = tc * inv * vec_ref[5:6, sl]
        b = tqs[:, p * 256:(p + 1) * 256] * inv * vec_ref[6:7, sl]
        mq_o[0, :, sl] = (a * cs2 + b * sn2).astype(BF16)

    sskr = _dot_hl(krm * krm, gm[0:128, 0:128])
    invr = lax.rsqrt(sskr * vec_ref[10:11, 0:128] + NORM_EPS)
    kr = (krm * invr * vec_ref[8:9, 0:128]) * cs + (krs * invr * vec_ref[9:10, 0:128]) * sn
    kr2 = jnp.concatenate([kr, kr], axis=1)

    kv = _dot(kvlat, wkv_ref[...])
    for p in range(4):
        sl = slice(p * 256, (p + 1) * 256)
        kn = _group_norm_chunk(kv[:, p * 256:(p + 1) * 256], g64, inv64)
        mk_o[0, :, sl] = (kn * vec_ref[7:8, sl] + kr2).astype(BF16)
    mv_o[0] = kv[:, 1024:1536].astype(BF16)


def _k1_even(x, mod3, vec, w, wq, wqs, wkv, g64, gm, cos128, sin128):
    b, s, d = x.shape
    ts = min(TS_PROJ, s)
    row = lambda n: pl.BlockSpec((1, ts, n), lambda i, j: (i, j, 0))
    outs = [(512, BF16), (512, BF16), (512, BF16), (1024, BF16), (1024, BF16), (1024, BF16), (512, BF16)]
    return pl.pallas_call(
        _k1_even_kernel,
        grid=(b, s // ts),
        in_specs=[row(d),
                  pl.BlockSpec((1, 1, 3 * d), lambda i, j: (i, 0, 0)),
                  _full(vec.shape), _full(w.shape), _full(wq.shape), _full(wqs.shape), _full(wkv.shape),
                  _full(g64.shape), _full(gm.shape),
                  pl.BlockSpec((ts, 128), lambda i, j: (j, 0)),
                  pl.BlockSpec((ts, 128), lambda i, j: (j, 0))],
        out_specs=[row(n) for n, _ in outs],
        out_shape=[jax.ShapeDtypeStruct((b, s, n), dt) for n, dt in outs],
        compiler_params=_cparams(2),
        name="inproj_even",
    )(x, mod3, vec, w, wq, wqs, wkv, g64, gm, cos128, sin128)


def _sb_kernel(bnd_ref, q_ref, k_ref, v_ref, tri_ref, o_ref, acc_ref, car_ref, *, tq, tk, bounded):
    qi = pl.program_id(2)
    q = q_ref[0]
    lane = lax.broadcasted_iota(jnp.int32, (1, LANES), 1)
    lo_half = lane < HEAD_DIM
    zero = jnp.zeros_like(q)
    qs = (jnp.where(lo_half, q, zero), jnp.where(lo_half, zero, q))
    tri = tri_ref[...]
    acc_ref[...] = jnp.zeros_like(acc_ref)
    car_ref[...] = jnp.zeros_like(car_ref)
    col0 = lax.broadcasted_iota(jnp.int32, (1, tk), 1)

    def tile(kt, r0, masked):
        n = tq - r0
        off = pl.multiple_of(kt * tk, tk)
        k = k_ref[0, pl.ds(off, tk), :]
        v = v_ref[0, pl.ds(off, tk), :]
        if masked:
            rows = qi * tq + r0 + lax.broadcasted_iota(jnp.int32, (n, 1), 0)
            strict = (kt * tk + col0) < rows
        for i in range(2):
            z = _dot_nt(qs[i][r0:tq], k)
            if bounded:
                lom = jnp.log(1.0 + jnp.exp2(z)) * (-LOG2E)
            else:
                lom = -jnp.maximum(z, 0.0) - jnp.log(1.0 + jnp.exp2(-jnp.abs(z))) * LOG2E
            if masked:
                lom = jnp.where(strict, lom, 0.0)
            tt = _dot(lom.astype(BF16), tri)
            car = car_ref[i, r0:tq]
            w = jnp.exp2(tt + z + jnp.concatenate([car] * (tk // LANES), axis=1))
            if masked:
                w = jnp.where(strict, w, 0.0)
            acc_ref[i, r0:tq] += _dot(w.astype(BF16), v)
            car_ref[i, r0:tq] = car + jnp.broadcast_to(tt[:, 0:1], (n, LANES))

    per = tq // tk
    for c in reversed(range(per)):
        tile(qi * per + c, c * tk, True)

    if bounded:
        limit = -(F32_ZERO_EXP + bnd_ref[0])

        def still_live():
            return jnp.max(jnp.maximum(car_ref[0], car_ref[1])) > limit

        def single(c):
            j, _ = c
            tile(qi * per - 1 - j, 0, False)
            return j + 1, still_live()

        _, live = lax.while_loop(lambda c: (c[0] < per) & c[1], single, (0, qi > 0))

        def group(c):
            g, _ = c
            for u in range(per):
                tile(qi * per - 1 - per * (g + 1) - u, 0, False)
            return g + 1, still_live()

        lax.while_loop(lambda c: (c[0] < qi - 1) & c[1], group, (0, live))
    else:
        def body(j, carry):
            for u in range(per):
                tile(qi * per - 1 - per * j - u, 0, False)
            return carry

        lax.fori_loop(0, qi, body, 0)
    o_ref[0] = jnp.where(lo_half, acc_ref[0], acc_ref[1])


def _sb_attention(q, k, v, tri, bound, safe):
    b, s, w = q.shape
    tq = min(TQ_SB, s)
    tk = min(TK_SB, tq)

    def call(bounded):
        kern = functools.partial(_sb_kernel, tq=tq, tk=tk, bounded=bounded)
        return pl.pallas_call(
            kern,
            grid=(b, w // LANES, s // tq),
            in_specs=[pl.BlockSpec(memory_space=pltpu.SMEM),
                      pl.BlockSpec((1, tq, LANES), lambda i, p, j: (i, j, p)),
                      pl.BlockSpec((1, s, LANES), lambda i, p, j: (i, 0, p)),
                      pl.BlockSpec((1, s, LANES), lambda i, p, j: (i, 0, p)),
                      _full(tri.shape)],
            out_specs=pl.BlockSpec((1, tq, LANES), lambda i, p, j: (i, j, p)),
            out_shape=jax.ShapeDtypeStruct((b, s, w), F32),
            scratch_shapes=[pltpu.VMEM((2, tq, LANES), F32), pltpu.VMEM((2, tq, LANES), F32)],
            compiler_params=_cparams(3),
            name="stickbreak_attn_bounded" if bounded else "stickbreak_attn",
        )(bound, q, k, v, tri)

    return lax.cond(safe, lambda: call(True), lambda: call(False))


def _lane_max(s):
    m = s[:, 0:LANES]
    for c in range(1, s.shape[1] // LANES):
        m = jnp.maximum(m, s[:, c * LANES:(c + 1) * LANES])
    return m


def _mla_kernel(bnd_ref, q_ref, k_ref, v_ref, o_ref, acc_ref, m_ref, *, tq, td, bounded):
    qi = pl.program_id(2)
    q = q_ref[0]
    qs = (q[:, 0:LANES], q[:, LANES:2 * LANES])
    lane = lax.broadcasted_iota(jnp.int32, (1, LANES), 1)
    lo_half = lane < HEAD_DIM
    acc_ref[...] = jnp.zeros_like(acc_ref)
    if bounded:
        m_ref[...] = jnp.full(m_ref.shape, bnd_ref[0], F32)
    else:
        m_ref[...] = jnp.full_like(m_ref, NEG_MASK)

    def scores(off, width, r0, masked):
        k = k_ref[0, pl.ds(off, width), :]
        if masked:
            rows = qi * tq + r0 + lax.broadcasted_iota(jnp.int32, (tq - r0, 1), 0)
            keep = (off + lax.broadcasted_iota(jnp.int32, (1, width), 1)) <= rows
        out = []
        for i in range(2):
            s = _dot_nt(qs[i][r0:tq], k[:, i * LANES:(i + 1) * LANES])
            if masked:
                s = jnp.where(keep, s, NEG_MASK)
            out.append(s)
        return out

    def sweep(fn):
        def body(j, carry):
            fn(pl.multiple_of(j * tq, tq), tq, 0, False)
            return carry
        lax.fori_loop(0, qi, body, 0)
        for c in range(tq // td):
            fn(pl.multiple_of(qi * tq + c * td, td), td, c * td, True)

    def row_max(off, width, r0, masked):
        for i, s in enumerate(scores(off, width, r0, masked)):
            m_ref[i, r0:tq] = jnp.maximum(m_ref[i, r0:tq], _lane_max(s))

    def accumulate(off, width, r0, masked):
        vx = jnp.concatenate([v_ref[0, pl.ds(off, width), :], jnp.ones((width, LANES), BF16)], axis=1)
        for i, s in enumerate(scores(off, width, r0, masked)):
            p = jnp.exp2(s - jnp.concatenate([m_ref[i, r0:tq]] * (width // LANES), axis=1))
            acc_ref[i, r0:tq] += _dot(p.astype(BF16), vx)

    if not bounded:
        sweep(row_max)
        for i in range(2):
            m_ref[i] = jnp.broadcast_to(jnp.max(m_ref[i], axis=-1, keepdims=True), (tq, LANES))
    sweep(accumulate)
    o_ref[0] = jnp.where(lo_half, acc_ref[0, :, 0:LANES] / acc_ref[0, :, LANES:2 * LANES],
                         acc_ref[1, :, 0:LANES] / acc_ref[1, :, LANES:2 * LANES])


def _mla_attention(q, k, v, bound):
    b, s, _ = q.shape
    tq = min(TQ_MLA, s)
    td = min(TD_MLA, tq)

    def call(bounded):
        kern = functools.partial(_mla_kernel, tq=tq, td=td, bounded=bounded)
        return pl.pallas_call(
            kern,
            grid=(b, MLA_HEADS // 2, s // tq),
            in_specs=[pl.BlockSpec(memory_space=pltpu.SMEM),
                      pl.BlockSpec((1, tq, 2 * LANES), lambda i, p, j: (i, j, p)),
                      pl.BlockSpec((1, s, 2 * LANES), lambda i, p, j: (i, 0, p)),
                      pl.BlockSpec((1, s, LANES), lambda i, p, j: (i, 0, p))],
            out_specs=pl.BlockSpec((1, tq, LANES), lambda i, p, j: (i, j, p)),
            out_shape=jax.ShapeDtypeStruct((b, s, MLA_OUT), F32),
            scratch_shapes=[pltpu.VMEM((2, tq, 2 * LANES), F32), pltpu.VMEM((2, tq, LANES), F32)],
            compiler_params=_cparams(3),
            name="mla_attn_bounded" if bounded else "mla_attn",
        )(bound, q, k, v)

    return lax.cond(bound[0] <= SAFE_LOGIT_BOUND, lambda: call(True), lambda: call(False))


def _out_even_kernel(x_ref, mod_ref, osb_ref, omla_ref, gz_ref, w_ref, o_ref):
    gz = gz_ref[0].astype(F32)
    m1 = (osb_ref[0] * gz[:, 0:SB_W]).astype(BF16)
    m2 = (omla_ref[0] * gz[:, SB_W:SB_W + MLA_OUT]).astype(BF16)
    y = _dot(m1, w_ref[0:SB_W, :]) + _dot(m2, w_ref[SB_W:SB_W + MLA_OUT, :])
    gate = mod_ref[0][:, 2 * D_MODEL:3 * D_MODEL]
    o_ref[0] = x_ref[0] + gate * y


def _out_even(x, mod3, osb, omla, gz, w):
    b, s, d = x.shape
    ts = min(TS_PROJ, s)
    row = lambda n: pl.BlockSpec((1, ts, n), lambda i, j: (i, j, 0))
    return pl.pallas_call(
        _out_even_kernel,
        grid=(b, s // ts),
        in_specs=[row(d), pl.BlockSpec((1, 1, 3 * d), lambda i, j: (i, 0, 0)),
                  row(SB_W), row(MLA_OUT), row(SB_W + MLA_OUT), _full(w.shape)],
        out_specs=row(d),
        out_shape=jax.ShapeDtypeStruct((b, s, d), F32),
        compiler_params=_cparams(2),
        name="outproj_even",
    )(x, mod3, osb, omla, gz, w)


NSA_COLS = 5248


def _k1_nsa_kernel(x_ref, mod_ref, vec_ref, w_ref, g64_ref,
                   q_o, ck_o, cv_o, sk_o, sv_o, wk_o, wv_o, gt_o, gz_o, chunk_ref):
    ts = x_ref.shape[1]
    hb = _modulated(x_ref, mod_ref, vec_ref[0:1, 0:D_MODEL])
    g64 = g64_ref[...]
    inv64 = 1.0 / HEAD_DIM

    uq = _dot(hb, w_ref[:, 0:1536])
    for c in range(6):
        sl = slice(c * 256, (c + 1) * 256)
        qn = _group_norm_chunk(uq[:, c * 256:(c + 1) * 256], g64, inv64)
        q_o[0, :, sl] = (qn * vec_ref[1:2, sl]).astype(BF16)

    uc = _dot(hb, w_ref[:, 1536:2304])
    for c in range(2 * NSA_KV_HEADS):
        chunk_ref[c] = uc[:, c * 128:(c + 1) * 128]
    for g in range(NSA_KV_HEADS):
        for tok in range(CMP_STRIDE):
            rows = pl.ds(tok, ts // CMP_STRIDE, stride=CMP_STRIDE)
            sl = slice(tok * 128, (tok + 1) * 128)
            ck_o[0, g, :, sl] = chunk_ref[g, rows, :].astype(BF16)
            cv_o[0, g, :, sl] = chunk_ref[NSA_KV_HEADS + g, rows, :].astype(BF16)

    us = _dot(hb, w_ref[:, 2304:3840])
    kgain = vec_ref[2:3, 0:384]
    g128 = g64[0:128, 0:128]
    for g in range(NSA_KV_HEADS):
        sl = slice(g * 128, (g + 1) * 128)
        t = us[:, g * 128:(g + 1) * 128]
        ss = _dot_hl(t * t, g128)
        sk_o[0, :, sl] = (t * lax.rsqrt(ss * inv64 + NORM_EPS) * kgain[:, sl]).astype(BF16)
        t = us[:, 768 + g * 128:768 + (g + 1) * 128]
        ss = _dot_hl(t * t, g128)
        wk_o[0, :, sl] = (t * lax.rsqrt(ss * inv64 + NORM_EPS) * kgain[:, sl]).astype(BF16)
    ones_hi = (lax.broadcasted_iota(jnp.int32, (1, 384), 1) % 128) >= HEAD_DIM
    sv_o[0] = jnp.where(ones_hi, 1.0, us[:, 384:768]).astype(BF16)
    wv_o[0] = jnp.where(ones_hi, 1.0, us[:, 1152:1536]).astype(BF16)

    ug = _dot(hb, w_ref[:, 3840:4224])
    gt_o[0] = _sigmoid(ug)
    uz = _dot(hb, w_ref[:, 4224:5248])
    gz_o[0] = (uz * _sigmoid(uz)).astype(BF16)


def _k1_nsa(x, mod3, vec, w, g64):
    b, s, d = x.shape
    ts = min(TS_PROJ, s)
    row = lambda n: pl.BlockSpec((1, ts, n), lambda i, j: (i, j, 0))
    cw = CMP_STRIDE * 128
    grp = pl.BlockSpec((1, NSA_KV_HEADS, ts // CMP_STRIDE, cw), lambda i, j: (i, 0, j, 0))
    return pl.pallas_call(
        _k1_nsa_kernel,
        grid=(b, s // ts),
        in_specs=[row(d), pl.BlockSpec((1, 1, 3 * d), lambda i, j: (i, 0, 0)),
                  _full(vec.shape), _full(w.shape), _full(g64.shape)],
        out_specs=[row(1536), grp, grp, row(384), row(384), row(384), row(384), row(384), row(1024)],
        out_shape=[jax.ShapeDtypeStruct((b, s, 1536), BF16),
                   jax.ShapeDtypeStruct((b, NSA_KV_HEADS, s // CMP_STRIDE, cw), BF16),
                   jax.ShapeDtypeStruct((b, NSA_KV_HEADS, s // CMP_STRIDE, cw), BF16),
                   jax.ShapeDtypeStruct((b, s, 384), BF16),
                   jax.ShapeDtypeStruct((b, s, 384), BF16),
                   jax.ShapeDtypeStruct((b, s, 384), BF16),
                   jax.ShapeDtypeStruct((b, s, 384), BF16),
                   jax.ShapeDtypeStruct((b, s, 384), F32),
                   jax.ShapeDtypeStruct((b, s, 1024), BF16)],
        scratch_shapes=[pltpu.VMEM((2 * NSA_KV_HEADS, ts, 128), F32)],
        compiler_params=_cparams(2),
        name="inproj_nsa",
    )(x, mod3, vec, w, g64)


def _k1_dil_kernel(x_ref, mod_ref, vec_ref, w_ref, g64_ref, *refs):
    outs, st = refs[:3 * N_DIL], refs[3 * N_DIL]
    ts = x_ref.shape[1]
    hb = _modulated(x_ref, mod_ref, vec_ref[0:1, 0:D_MODEL])
    g64 = g64_ref[...]
    inv64 = 1.0 / HEAD_DIM
    u = _dot(hb, w_ref[...])
    for g in range(N_DIL):
        dil = DIL_CFG[g][1]
        qn = _group_norm_chunk(u[:, g * 256:(g + 1) * 256], g64, inv64) * vec_ref[3:4, 0:256]
        kn = _group_norm_chunk(u[:, 768 + g * 256:768 + (g + 1) * 256], g64, inv64) * vec_ref[4:5, 0:256]
        vals = (qn, kn, u[:, 1536 + g * 256:1536 + (g + 1) * 256])
        for j, val in enumerate(vals):
            o_ref = outs[3 * g + j]
            if dil == 1:
                o_ref[0] = val.astype(BF16)
                continue
            for h in range(2):
                st[j, h] = val[:, h * 128:(h + 1) * 128]
            for r in range(dil):
                for h in range(2):
                    sl = slice(r * 256 + h * 128, r * 256 + (h + 1) * 128)
                    o_ref[0, :, sl] = st[j, h, pl.ds(r, ts // dil, stride=dil), :].astype(BF16)


def _k1_dil(x, mod3, vec, w, g64):
    b, s, d = x.shape
    ts = min(TS_PROJ, s)
    row = lambda n: pl.BlockSpec((1, ts, n), lambda i, j: (i, j, 0))
    specs, shapes = [], []
    for g in range(N_DIL):
        dil = DIL_CFG[g][1]
        for _ in range(3):
            specs.append(pl.BlockSpec((1, ts // dil, dil * 256), lambda i, j: (i, j, 0)))
            shapes.append(jax.ShapeDtypeStruct((b, s // dil, dil * 256), BF16))
    return pl.pallas_call(
        _k1_dil_kernel,
        grid=(b, s // ts),
        in_specs=[row(d), pl.BlockSpec((1, 1, 3 * d), lambda i, j: (i, 0, 0)),
                  _full(vec.shape), _full(w.shape), _full(g64.shape)],
        out_specs=specs,
        out_shape=shapes,
        scratch_shapes=[pltpu.VMEM((3, 2, ts, 128), F32)],
        compiler_params=_cparams(2),
        name="inproj_dil",
    )(x, mod3, vec, w, g64)


def _compress_kernel(xk_ref, xv_ref, wk_ref, wv_ref, pek_ref, pev_ref, gain_ref, g128_ref, kc_o, vc_o):
    nch = xk_ref.shape[2]

    def comp(x_ref, w_ref, pe_ref):
        x = x_ref[0, 0]
        lo = _dot(x, w_ref[0])
        hi = _dot(x, w_ref[1])
        pec = _dot_hl(pe_ref[0], w_ref[0]) + _dot_hl(pe_ref[1], w_ref[1])
        return lo + pltpu.roll(hi, nch - 1, 0) + pec[0:1, :]

    kc = comp(xk_ref, wk_ref, pek_ref)
    ss = _dot_hl(kc * kc, g128_ref[...])
    kc_o[0, 0] = (kc * lax.rsqrt(ss * (1.0 / HEAD_DIM) + NORM_EPS) * gain_ref[...]).astype(BF16)
    vc_o[0, 0] = comp(xv_ref, wv_ref, pev_ref).T.astype(BF16)


def _compress(ck, cv, wk, wv, pek, pev, gain, g128):
    b, g, nch, _ = ck.shape
    xk, xv = ck, cv
    blk = pl.BlockSpec((1, 1, nch, CMP_STRIDE * 128), lambda i, j: (i, j, 0, 0))
    oblk = pl.BlockSpec((1, 1, nch, 128), lambda i, j: (i, j, 0, 0))
    tblk = pl.BlockSpec((1, 1, 128, nch), lambda i, j: (i, j, 0, 0))
    return pl.pallas_call(
        _compress_kernel,
        grid=(b, g),
        in_specs=[blk, blk, _full(wk.shape), _full(wv.shape), _full(pek.shape), _full(pev.shape),
                  _full(gain.shape), _full(g128.shape)],
        out_specs=[oblk, tblk],
        out_shape=[jax.ShapeDtypeStruct((b, g, nch, 128), BF16), jax.ShapeDtypeStruct((b, g, 128, nch), BF16)],
        compiler_params=_cparams(2),
        name="nsa_compress",
    )(xk, xv, wk, wv, pek, pev, gain, g128)


def _pair_select(lo_half, a, b):
    return jnp.where(lo_half, a, b)


def _nsa_cmp_kernel(slope_ref, q_ref, kc_ref, vct_ref, ovlt_ref, cposc_ref, prow_ref, gt_ref,
                    oc_o, selb_o, any_o, *, tq):
    g = pl.program_id(1)
    qi = pl.program_id(2)
    q4 = q_ref[0]
    kc = kc_ref[0, 0]
    vct = vct_ref[0, 0]
    nch = kc.shape[0]
    t = qi * tq + lax.broadcasted_iota(jnp.int32, (1, tq), 1)
    n_id = lax.broadcasted_iota(jnp.int32, (nch, 1), 0)
    valid = (n_id * CMP_STRIDE + (CMP_LEN - 1)) <= t
    rel = (cposc_ref[...] - prow_ref[qi][:, 0:1]) * LOG2E
    gtt = gt_ref[0].T
    row_lo = lax.broadcasted_iota(jnp.int32, (LANES, 1), 0) < HEAD_DIM
    psum = jnp.zeros((nch, tq), F32)
    outs = []
    for hh in range(NSA_HPG):
        s = _dot_nt(kc, q4[:, hh * LANES:(hh + 1) * LANES]) + slope_ref[g * NSA_HPG + hh] * rel
        s = jnp.where(valid, s, -jnp.inf)
        mx = jnp.max(s, axis=0, keepdims=True)
        mx = jnp.where(mx == -jnp.inf, 0.0, mx)
        e = jnp.exp2(s - mx)
        den = jnp.maximum(jnp.sum(e, axis=0, keepdims=True), TINY)
        p = e * (1.0 / den)
        psum = psum + p
        outs.append(_dot(vct, p.astype(BF16)) * gtt[3 * hh:3 * hh + 1, :])
    oc_o[0, :, 0:LANES] = jnp.where(row_lo, outs[0], outs[1]).T
    oc_o[0, :, LANES:2 * LANES] = jnp.where(row_lo, outs[2], outs[3]).T

    hi, lo = _split_hl(psum)
    imp = _dot(ovlt_ref[...], hi) + _dot(ovlt_ref[...], lo)
    blk = lax.broadcasted_iota(jnp.int32, (LANES, 1), 0)
    cur = t >> 6
    forced = (blk == 0) | (blk == cur) | (blk == cur - 1)
    allowed = blk <= cur
    score = jnp.where(allowed, imp + jnp.where(forced, FORCE_BONUS, 0.0), -jnp.inf)
    blk_f = blk.astype(F32)

    def pick(_, sc):
        mx = jnp.max(sc, axis=0, keepdims=True)
        idx = jnp.min(jnp.where(sc == mx, blk_f, float(LANES)), axis=0, keepdims=True)
        return jnp.where(blk_f == idx, -jnp.inf, sc)

    left = lax.fori_loop(0, SEL_TOPN, pick, score)
    chosen = jnp.where((left == -jnp.inf) & allowed, 1.0, 0.0).T
    selb_o[0, 0] = ((chosen - 1.0) * SEL_OFF).astype(BF16)
    used = jnp.max(chosen, axis=0, keepdims=True)
    any_o[0, 0, 0] = jnp.broadcast_to(used, (8, LANES))


def _nsa_cmp(slopes, q, kc, vct, ovlt, cposc, pos_rows, gates):
    b, s, _ = q.shape
    nch = kc.shape[2]
    tq = pos_rows.shape[2]
    kern = functools.partial(_nsa_cmp_kernel, tq=tq)
    return pl.pallas_call(
        kern,
        grid=(b, NSA_KV_HEADS, s // tq),
        in_specs=[pl.BlockSpec(memory_space=pltpu.SMEM),
                  pl.BlockSpec((1, tq, 4 * LANES), lambda i, g, j: (i, j, g)),
                  pl.BlockSpec((1, 1, nch, LANES), lambda i, g, j: (i, g, 0, 0)),
                  pl.BlockSpec((1, 1, LANES, nch), lambda i, g, j: (i, g, 0, 0)),
                  _full(ovlt.shape), _full(cposc.shape), _full(pos_rows.shape),
                  pl.BlockSpec((1, tq, LANES), lambda i, g, j: (i, j, g))],
        out_specs=[pl.BlockSpec((1, tq, 2 * LANES), lambda i, g, j: (i, j, g)),
                   pl.BlockSpec((1, 1, tq, LANES), lambda i, g, j: (i, g, j, 0)),
                   pl.BlockSpec((1, 1, 1, 8, LANES), lambda i, g, j: (i, g, j, 0, 0))],
        out_shape=[jax.ShapeDtypeStruct((b, s, NSA_W), F32),
                   jax.ShapeDtypeStruct((b, NSA_KV_HEADS, s, LANES), BF16),
                   jax.ShapeDtypeStruct((b, NSA_KV_HEADS, s // tq, 8, LANES), F32)],
        compiler_params=_cparams(3),
        name="nsa_cmp_topk",
    )(slopes, q, kc, vct, ovlt, cposc, pos_rows, gates)


def _gqa_kernel(flag_ref, slope_ref, bnd_ref, q_ref, selb_ref, k_ref, v_ref, pos_ref, pcol_ref, gt_ref, o_ref,
                qa_ref, acc_ref, m_ref, *, tq, tk, branch, bounded):
    bi = pl.program_id(0)
    g = pl.program_id(1)
    qi = pl.program_id(2)
    q4 = q_ref[0]
    lane = lax.broadcasted_iota(jnp.int32, (1, LANES), 1)
    lo_half = lane < HEAD_DIM
    pref = pos_ref[qi * (tq // tk)][:, 0:1]
    for hh in range(NSA_HPG):
        sl = slice(hh * tq, (hh + 1) * tq)
        qa_ref[sl, 0:LANES] = q4[:, hh * LANES:(hh + 1) * LANES]
        if branch == 1:
            qa_ref[sl, LANES:2 * LANES] = selb_ref[0, 0]
        if bounded:
            own = (pcol_ref[...] - pref) * (LOG2E * slope_ref[g * NSA_HPG + hh]) + bnd_ref[0]
            m_ref[sl] = jnp.broadcast_to(own, (tq, LANES))
    acc_ref[...] = jnp.zeros_like(acc_ref)
    if not bounded:
        m_ref[...] = jnp.full_like(m_ref, NEG_MASK)
    rows = qi * tq + lax.broadcasted_iota(jnp.int32, (tq, 1), 0)
    col0 = lax.broadcasted_iota(jnp.int32, (1, tk), 1)
    krow = lax.broadcasted_iota(jnp.int32, (tk, 1), 0)
    per = tq // tk
    nrep = tk // LANES
    flag0 = ((bi * NSA_KV_HEADS + g) * pl.num_programs(2) + qi) * SEL_TILE_STRIDE

    def scores(kt_true, masked, r_lo, r_hi):
        kt = jnp.maximum(kt_true, 0)
        off = pl.multiple_of(kt * tk, tk)
        k = k_ref[0, pl.ds(off, tk), :]
        if branch == 1:
            blk = (kt * tk + krow) >> 6
            onehot = jnp.where(lane == blk, 1.0, 0.0).astype(BF16)
            s_all = _dot_nt(qa_ref[...], jnp.concatenate([k, onehot], axis=1))
            parts = [s_all[hh * tq:(hh + 1) * tq] for hh in range(NSA_HPG)]
        else:
            parts = [_dot_nt(qa_ref[hh * tq + r_lo:hh * tq + r_hi, 0:LANES], k) for hh in range(NSA_HPG)]
        rel = (pos_ref[kt] - pref) * LOG2E
        if masked:
            cols = kt_true * tk + col0
            d = rows[r_lo:r_hi] - cols
            keep = d >= 0
            if branch == 2:
                keep = keep & (d < WIN) & (cols >= 0)
        out = []
        for hh in range(NSA_HPG):
            s = parts[hh] + slope_ref[g * NSA_HPG + hh] * rel
            if masked:
                s = jnp.where(keep, s, NEG_MASK)
            out.append(s)
        return out

    def sweep(fn):
        if branch == 1:
            def body(j, carry):
                @pl.when(flag_ref[flag0 + j] != 0)
                def _():
                    fn(j, False, 0, tq)
                return carry
            lax.fori_loop(0, qi * per, body, 0)
            for dd in range(per):
                fn(qi * per + dd, True, 0, tq)
        else:
            for c in range(-(WIN // tk), per):
                r_lo = max(0, c * tk)
                r_hi = min(tq, -(-(c * tk + tk + WIN - 1) // 8) * 8)
                fn(qi * per + c, True, r_lo, r_hi)

    def row_max(kt_true, masked, r_lo, r_hi):
        for hh, s in enumerate(scores(kt_true, masked, r_lo, r_hi)):
            sl = slice(hh * tq + r_lo, hh * tq + r_hi)
            m_ref[sl] = jnp.maximum(m_ref[sl], _lane_max(s))

    def accumulate(kt_true, masked, r_lo, r_hi):
        off = pl.multiple_of(jnp.maximum(kt_true, 0) * tk, tk)
        v = v_ref[0, pl.ds(off, tk), :]
        for hh, s in enumerate(scores(kt_true, masked, r_lo, r_hi)):
            sl = slice(hh * tq + r_lo, hh * tq + r_hi)
            p = jnp.exp2(s - jnp.concatenate([m_ref[sl]] * nrep, axis=1))
            acc_ref[sl] += _dot(p.astype(BF16), v)

    if not bounded:
        sweep(row_max)
        for hh in range(NSA_HPG):
            sl = slice(hh * tq, (hh + 1) * tq)
            m_ref[sl] = jnp.broadcast_to(jnp.max(m_ref[sl], axis=-1, keepdims=True), (tq, LANES))
    sweep(accumulate)

    gt = gt_ref[0]
    for pr in range(2):
        res = []
        for x in range(2):
            hh = 2 * pr + x
            a = acc_ref[hh * tq:(hh + 1) * tq]
            r = pltpu.roll(a, HEAD_DIM, 1)
            o = a / r if x == 0 else r / a
            res.append(o * gt[:, 3 * hh + branch:3 * hh + branch + 1])
        o_ref[0, :, pr * LANES:(pr + 1) * LANES] = _pair_select(lo_half, res[0], res[1])


def _nsa_gqa(flags, slopes, bound, safe, q, selb, k, v, pos2d, pos_col, gates, branch):
    b, s, _ = q.shape
    tq = min(TQ_NSA, s)
    tk = pos2d.shape[2]

    def call(bounded):
        kern = functools.partial(_gqa_kernel, tq=tq, tk=tk, branch=branch, bounded=bounded)
        smem = pl.BlockSpec(memory_space=pltpu.SMEM)
        name = ("nsa_sel_attn" if branch == 1 else "nsa_win_attn") + ("_bounded" if bounded else "")
        return pl.pallas_call(
            kern,
            grid=(b, NSA_KV_HEADS, s // tq),
            in_specs=[smem, smem, smem,
                      pl.BlockSpec((1, tq, 4 * LANES), lambda i, g, j: (i, j, g)),
                      pl.BlockSpec((1, 1, tq, LANES), lambda i, g, j: (i, g, j, 0)),
                      pl.BlockSpec((1, s, LANES), lambda i, g, j: (i, 0, g)),
                      pl.BlockSpec((1, s, LANES), lambda i, g, j: (i, 0, g)),
                      _full(pos2d.shape),
                      pl.BlockSpec((tq, 1), lambda i, g, j: (j, 0)),
                      pl.BlockSpec((1, tq, LANES), lambda i, g, j: (i, j, g))],
            out_specs=pl.BlockSpec((1, tq, 2 * LANES), lambda i, g, j: (i, j, g)),
            out_shape=jax.ShapeDtypeStruct((b, s, NSA_W), F32),
            scratch_shapes=[pltpu.VMEM((NSA_HPG * tq, 2 * LANES), BF16),
                            pltpu.VMEM((NSA_HPG * tq, LANES), F32),
                            pltpu.VMEM((NSA_HPG * tq, LANES), F32)],
            compiler_params=_cparams(3),
            name=name,
        )(flags, slopes, bound, q, selb, k, v, pos2d, pos_col, gates)

    return lax.cond(safe, lambda: call(True), lambda: call(False))


def _dil_kernel(slope_ref, bnd_ref, q_ref, k_ref, v_ref, pc_ref, pr_ref, o_ref, lse_ref,
                *, t, span, grp, bounded):
    i = pl.program_id(2)
    q4 = q_ref[0]
    lane = lax.broadcasted_iota(jnp.int32, (1, LANES), 1)
    lo_half = lane < HEAD_DIM
    per = t // LANES
    prev = jnp.maximum(i * per - 1, 0)
    o_prev = pl.multiple_of(prev * LANES, LANES)
    o_cur = pl.multiple_of(i * t, t)
    kk = jnp.concatenate([k_ref[0, pl.ds(o_prev, LANES), :], k_ref[0, pl.ds(o_cur, t), :]], axis=0)
    vv = jnp.concatenate([v_ref[0, pl.ds(o_prev, LANES), :], v_ref[0, pl.ds(o_cur, t), :]], axis=0)
    pk = jnp.concatenate([pr_ref[0, prev]] + [pr_ref[0, i * per + c] for c in range(per)], axis=1)
    dist = (pc_ref[0] - pk) * LOG2E
    rows = i * t + lax.broadcasted_iota(jnp.int32, (t, 1), 0)
    cols = i * t - LANES + lax.broadcasted_iota(jnp.int32, (1, t + LANES), 1)
    d = rows - cols
    valid = (cols >= 0) & (d >= 0) & (d <= span)
    zero = jnp.zeros((t, LANES), BF16)
    ones = jnp.ones((t + LANES, LANES), BF16)
    for p in range(2):
        qp = q4[:, p * LANES:(p + 1) * LANES]
        kp = kk[:, p * LANES:(p + 1) * LANES]
        vx = jnp.concatenate([vv[:, p * LANES:(p + 1) * LANES], ones], axis=1)
        res = []
        for x in range(2):
            qx = jnp.where(lo_half, qp, zero) if x == 0 else jnp.where(lo_half, zero, qp)
            s = _dot_nt(qx, kp) - slope_ref[grp * DIL_HEADS + 2 * p + x] * dist
            s = jnp.where(valid, s, -jnp.inf)
            mx = bnd_ref[0] if bounded else jnp.max(s, axis=-1, keepdims=True)
            acc = _dot(jnp.exp2(s - mx).astype(BF16), vx)
            den = acc[:, LANES:2 * LANES]
            res.append((acc[:, 0:LANES] / den, mx * LN2 + jnp.log(den)))
        sl = slice(p * LANES, (p + 1) * LANES)
        o_ref[0, :, sl] = _pair_select(lo_half, res[0][0], res[1][0])
        lse_ref[0, :, sl] = jnp.where(lo_half, res[0][1], res[1][1])


def _dilated(slopes, bound, safe, qv, kv, vv, pos_f, grp):
    window, dil = DIL_CFG[grp]
    b, sub, wd = qv.shape
    w = wd // dil
    t = min(T_DIL, sub)
    span = window // dil
    assert span <= LANES and t % LANES == 0
    pres = pos_f.reshape(sub, dil).T
    pc = pres.reshape(dil, sub, 1)
    pr = pres.reshape(dil, sub // LANES, 1, LANES)
    shp = jax.ShapeDtypeStruct((b, sub, wd), F32)
    smem = pl.BlockSpec(memory_space=pltpu.SMEM)

    def call(bounded):
        kern = functools.partial(_dil_kernel, t=t, span=span, grp=grp, bounded=bounded)
        return pl.pallas_call(
            kern,
            grid=(b, dil, sub // t),
            in_specs=[smem, smem,
                      pl.BlockSpec((1, t, w), lambda i, r, j: (i, j, r)),
                      pl.BlockSpec((1, sub, w), lambda i, r, j: (i, 0, r)),
                      pl.BlockSpec((1, sub, w), lambda i, r, j: (i, 0, r)),
                      pl.BlockSpec((1, t, 1), lambda i, r, j: (r, j, 0)),
                      pl.BlockSpec((1, sub // LANES, 1, LANES), lambda i, r, j: (r, 0, 0, 0))],
            out_specs=[pl.BlockSpec((1, t, w), lambda i, r, j: (i, j, r)),
                       pl.BlockSpec((1, t, w), lambda i, r, j: (i, j, r))],
            out_shape=[shp, shp],
            compiler_params=_cparams(3),
            name="dilated_attn_g%d%s" % (grp, "_bounded" if bounded else ""),
        )(slopes, bound, qv, kv, vv, pc, pr)

    return lax.cond(safe, lambda: call(True), lambda: call(False))


def _out_odd_kernel(x_ref, mod_ref, oc_ref, os_ref, ow_ref, d0_ref, d1_ref, d2_ref,
                    l0_ref, l1_ref, l2_ref, gz_ref, w_ref, o_ref, st):
    ts = x_ref.shape[1]
    gz = gz_ref[0].astype(F32)
    nsa = oc_ref[0] + os_ref[0] + ow_ref[0]
    m1 = (nsa * gz[:, 0:NSA_W]).astype(BF16)

    def token_order(ref, g, slot):
        dil = DIL_CFG[g][1]
        if dil == 1:
            return ref[0]
        for r in range(dil):
            for h in range(2):
                st[slot, h, pl.ds(r, ts // dil, stride=dil), :] = ref[0, :, r * DIL_W + h * 128:r * DIL_W + (h + 1) * 128]
        return jnp.concatenate([st[slot, 0], st[slot, 1]], axis=1)

    d0, d1, d2 = token_order(d0_ref, 0, 0), token_order(d1_ref, 1, 0), token_order(d2_ref, 2, 1)
    l0, l1, l2 = token_order(l0_ref, 0, 2), token_order(l1_ref, 1, 2), token_order(l2_ref, 2, 3)
    mx = jnp.maximum(jnp.maximum(l0, l1), l2)
    e0, e1, e2 = jnp.exp(l0 - mx), jnp.exp(l1 - mx), jnp.exp(l2 - mx)
    dil = (e0 * d0 + e1 * d1 + e2 * d2) / (e0 + e1 + e2)
    m2 = (dil * gz[:, NSA_W:NSA_W + DIL_W]).astype(BF16)
    y = _dot(m1, w_ref[0:NSA_W, :]) + _dot(m2, w_ref[NSA_W:NSA_W + DIL_W, :])
    gate = mod_ref[0][:, 2 * D_MODEL:3 * D_MODEL]
    o_ref[0] = x_ref[0] + gate * y


def _out_odd(x, mod3, oc, os_, ow, dils, lses, gz, w):
    b, s, d = x.shape
    ts = min(TS_PROJ, s)
    row = lambda n: pl.BlockSpec((1, ts, n), lambda i, j: (i, j, 0))
    res = [pl.BlockSpec((1, ts // DIL_CFG[g][1], DIL_CFG[g][1] * DIL_W), lambda i, j: (i, j, 0))
           for g in range(N_DIL)]
    return pl.pallas_call(
        _out_odd_kernel,
        grid=(b, s // ts),
        in_specs=[row(d), pl.BlockSpec((1, 1, 3 * d), lambda i, j: (i, 0, 0)),
                  row(NSA_W), row(NSA_W), row(NSA_W)] + res + res + [row(1024), _full(w.shape)],
        out_specs=row(d),
        out_shape=jax.ShapeDtypeStruct((b, s, d), F32),
        scratch_shapes=[pltpu.VMEM((4, 2, ts, 128), F32)],
        compiler_params=_cparams(2),
        name="outproj_odd",
    )(x, mod3, oc, os_, ow, *dils, *lses, gz, w)


def _pad_cols(w, n):
    return jnp.pad(w, ((0, 0), (0, n - w.shape[1])))


def _pad_vec(v, n=D_MODEL):
    return jnp.pad(v, (0, n - v.shape[0]))


def _group_matrix(sizes, total):
    m = np.zeros((total, total), np.float32)
    off = 0
    for sz, on in sizes:
        if on:
            m[off:off + sz, off:off + sz] = 1.0
        off += sz
    return jnp.asarray(m, BF16)


def _swap_halves(w):
    h = w.shape[-1] // 2
    return jnp.concatenate([w[..., h:], w[..., :h]], axis=-1)


def _column_blocks(segments):
    blocks, cur, room = [], [], LANES
    for src, width in segments:
        while width:
            n = min(width, room)
            cur.append((src, n))
            src = None if src is None else src + n
            width -= n
            room -= n
            if room == 0:
                blocks.append(cur)
                cur, room = [], LANES
    assert not cur
    return blocks


def _pack_cols_kernel(w_ref, o_ref, *, blocks):
    _, rows, n_src = w_ref.shape
    for bi, pieces in enumerate(blocks):
        parts = []
        for src, width in pieces:
            if src is None:
                parts.append(jnp.zeros((rows, width), F32))
                continue
            a0 = src // LANES * LANES
            a1 = min(-(-(src + width) // LANES) * LANES, n_src)
            parts.append(w_ref[0, :, a0:a1][:, src - a0:src - a0 + width])
        blk = parts[0] if len(parts) == 1 else jnp.concatenate(parts, axis=1)
        o_ref[:, bi * LANES:(bi + 1) * LANES] = blk.astype(BF16)


def _pack_cols(w, layer, segments):
    blocks = _column_blocks(segments)
    _, d, n_src = w.shape
    tr = 256
    return pl.pallas_call(
        functools.partial(_pack_cols_kernel, blocks=blocks),
        grid=(d // tr,),
        in_specs=[pl.BlockSpec((1, tr, n_src), lambda i: (layer, i, 0))],
        out_specs=pl.BlockSpec((tr, len(blocks) * LANES), lambda i: (i, 0)),
        out_shape=jax.ShapeDtypeStruct((d, len(blocks) * LANES), BF16),
        compiler_params=_cparams(1),
        name="pack_weight_columns",
    )(w)


EVEN_SEGMENTS = ((0, 1536), (1536, 512), (2464, 512), (2048, 256), (2304, 128),
                 (None, 64), (2432, 32), (None, 32), (None, 64), (2448, 16), (2432, 16), (None, 32))


def _pack_even(w_in_all, layer, norm_g, sb_qn, sb_kn, qa_g, wq_up, kva_g, wkv_up, qn, kn):
    w = _pack_cols(w_in_all, layer, EVEN_SEGMENTS)
    wq3 = wq_up.reshape(MLA_Q_RANK, MLA_HEADS, MLA_NOPE + MLA_ROPE)
    zq = jnp.zeros((MLA_Q_RANK, MLA_HEADS, 32), wq_up.dtype)
    wq = jnp.concatenate([wq3, zq], axis=-1).reshape(MLA_Q_RANK, MLA_HEADS * 128).astype(BF16)
    wqs = jnp.concatenate([jnp.zeros((MLA_Q_RANK, MLA_HEADS, 64), wq_up.dtype),
                           _swap_halves(wq3[..., MLA_NOPE:]), zq], axis=-1)
    wqs = wqs.reshape(MLA_Q_RANK, MLA_HEADS * 128).astype(BF16)
    wkv3 = wkv_up.reshape(MLA_KV_RANK, MLA_HEADS, MLA_NOPE + MLA_V)
    wk = jnp.concatenate([wkv3[..., :MLA_NOPE], jnp.zeros_like(wkv3[..., :MLA_NOPE])], axis=-1)
    wkv = jnp.concatenate([wk.reshape(MLA_KV_RANK, MLA_HEADS * 128),
                           wkv3[..., MLA_NOPE:].reshape(MLA_KV_RANK, MLA_HEADS * MLA_V)], axis=1).astype(BF16)
    z32 = jnp.zeros((32,), F32)
    z64 = jnp.zeros((64,), F32)
    scale = (MLA_NOPE + MLA_ROPE) ** -0.5 * LOG2E
    qg = jnp.tile(jnp.concatenate([qn, z32]), MLA_HEADS) * scale
    qgs = jnp.tile(jnp.concatenate([z64, _swap_halves(qn[MLA_NOPE:]), z32]), MLA_HEADS) * scale
    kg = jnp.tile(jnp.concatenate([kn[:MLA_NOPE], z64]), MLA_HEADS)
    krg = jnp.concatenate([z64, kn[MLA_NOPE:], z32])
    krgs = jnp.concatenate([z64, _swap_halves(kn[MLA_NOPE:]), z32])
    cnt = jnp.tile(jnp.concatenate([jnp.full((64,), 1.0 / 64), jnp.full((32,), 1.0 / 32), jnp.ones((32,))]),
                   MLA_HEADS).astype(F32)
    rows = [norm_g, _pad_vec(jnp.tile(sb_qn, SB_HEADS) * (LOG2E * HEAD_DIM ** -0.5)), _pad_vec(jnp.tile(sb_kn, SB_HEADS)),
            _pad_vec(qa_g), _pad_vec(kva_g), qg, qgs, kg, _pad_vec(krg), _pad_vec(krgs), cnt]
    rows += [jnp.zeros((D_MODEL,), F32)] * (16 - len(rows))
    return w, wq, wqs, wkv, jnp.stack(rows).astype(F32)


def _padded_heads(start, n):
    return tuple(seg for h in range(n) for seg in ((start + h * HEAD_DIM, HEAD_DIM), (None, HEAD_DIM)))


NSA_SEGMENTS = (_padded_heads(0, NSA_HEADS)
                + tuple(seg for c in range(6) for seg in _padded_heads(768 + c * NSA_KV_W, NSA_KV_HEADS))
                + tuple(seg for g in range(NSA_KV_HEADS)
                        for seg in ((1920 + g * NSA_HPG * 3, NSA_HPG * 3), (None, LANES - NSA_HPG * 3)))
                + ((1956, NSA_W), (5028, DIL_W)))
DIL_SEGMENTS = ((2724, 3 * N_DIL * DIL_W),)


def _pack_odd(w_in_all, layer, norm_g, nsa_qn, nsa_kn, dil_qn, dil_kn):
    w_nsa = _pack_cols(w_in_all, layer, NSA_SEGMENTS)
    w_dil = _pack_cols(w_in_all, layer, DIL_SEGMENTS)
    z64 = jnp.zeros((64,), F32)
    n = NSA_HEADS * 128
    rows = [_pad_vec(norm_g, n),
            jnp.tile(jnp.concatenate([nsa_qn * (LOG2E * HEAD_DIM ** -0.5), z64]), NSA_HEADS),
            _pad_vec(jnp.tile(jnp.concatenate([nsa_kn, z64]), NSA_KV_HEADS), n),
            _pad_vec(jnp.tile(dil_qn, DIL_HEADS) * (LOG2E * HEAD_DIM ** -0.5), n),
            _pad_vec(jnp.tile(dil_kn, DIL_HEADS), n)]
    rows += [jnp.zeros((n,), F32)] * (8 - len(rows))
    return w_nsa, w_dil, jnp.stack(rows).astype(F32)


def _pack_compress(w, pe, double):
    w3 = w.reshape(CMP_LEN, HEAD_DIM, HEAD_DIM)
    w3 = jnp.concatenate([w3, jnp.zeros_like(w3)], axis=1)
    w3 = jnp.concatenate([w3, w3 if double else jnp.zeros_like(w3)], axis=2)
    wp = w3.reshape(2, CMP_STRIDE * 128, 128).astype(BF16)
    pe2 = jnp.concatenate([pe, jnp.zeros_like(pe)], axis=1).reshape(2, 1, CMP_STRIDE * 128)
    pe2 = jnp.broadcast_to(pe2, (2, 8, CMP_STRIDE * 128)).astype(F32)
    return wp, pe2


def _normed_len(gain, sizes):
    tot, off = 0.0, 0
    for n in sizes:
        tot = tot + n * jnp.max(jnp.square(gain[off:off + n]))
        off += n
    return jnp.sqrt(tot)


def _mla_logit_bound(qn, kn):
    sizes = (MLA_NOPE, MLA_ROPE)
    scale = (MLA_NOPE + MLA_ROPE) ** -0.5 * LOG2E
    return (_normed_len(qn, sizes) * _normed_len(kn, sizes) * (scale * ROUNDING_MARGIN)).reshape(1).astype(F32)


def _alibi_slopes(n):
    return 2.0 ** (-8.0 * jnp.arange(1, n + 1, dtype=jnp.float32) / n)


def kernel(x, c, positions, ada_w, ada_b, norm_g, ev_w_in, ev_w_out, sb_qn, sb_kn, mla_qa_g, mla_wq_up,
           mla_kva_g, mla_wkv_up, mla_qn, mla_kn, od_w_in, od_w_out, nsa_qn, nsa_kn, nsa_cmp_wk, nsa_cmp_wv,
           nsa_cmp_pe_k, nsa_cmp_pe_v, dil_qn, dil_kn):
    b, s, d = x.shape
    depth = ada_w.shape[0]
    pos_f = positions.astype(F32)

    inv_freq = ROPE_BASE ** (-jnp.arange(0, MLA_ROPE, 2, dtype=F32) / MLA_ROPE)
    ang = pos_f[:, None] * inv_freq[None, :]
    cos, sin = jnp.cos(ang), jnp.sin(ang)
    cos128 = jnp.concatenate([jnp.ones((s, 64), F32), cos, cos, jnp.zeros((s, 32), F32)], axis=1)
    sin128 = jnp.concatenate([jnp.zeros((s, 64), F32), -sin, sin, jnp.zeros((s, 32), F32)], axis=1)
    nsa_slopes = _alibi_slopes(NSA_HEADS)
    dil_slopes = _alibi_slopes(N_DIL * DIL_HEADS)
    nch = s // CMP_STRIDE
    chunk_sum = pos_f.reshape(nch, CMP_STRIDE).sum(axis=1)
    cpos = ((chunk_sum + jnp.roll(chunk_sum, -1)) / CMP_LEN).reshape(nch, 1)
    n_sel = s // SEL_LEN
    cst = np.arange(nch)[:, None] * CMP_STRIDE
    jst = np.arange(LANES)[None, :] * SEL_LEN
    ovl = ((cst <= jst + SEL_LEN - 1) & (cst + CMP_LEN - 1 >= jst) & (np.arange(LANES)[None, :] < n_sel))
    ovlt = jnp.asarray(ovl.astype(np.float32).T, BF16)
    pos_col = pos_f.reshape(s, 1)
    tqc = min(TQ_CMP, s)
    pos_rows = pos_f.reshape(s // tqc, 1, tqc)
    tqn = min(TQ_NSA, s)
    tkn = min(TK_NSA, tqn)
    assert tkn == TK_NSA
    pos2d = pos_f.reshape(s // tkn, 1, tkn)
    pos2d_win = pos_f.reshape(s // TK_WIN, 1, TK_WIN)
    q_first = pos_f[::tqn]
    k_last = pos_f[jnp.minimum((jnp.arange(SEL_TILE_STRIDE) + 1) * tkn - 1, s - 1)]
    group_slope = jnp.min(nsa_slopes.reshape(NSA_KV_HEADS, NSA_HPG), axis=1) * LOG2E
    sel_dead = group_slope[:, None, None] * (q_first[:, None] - k_last[None, :])[None] > F32_ZERO_EXP
    pos_sorted = jnp.all(pos_f[1:] >= pos_f[:-1])

    g64 = _group_matrix([(64, 1)] * 4, 256)
    gm = _group_matrix([(64, 1), (32, 1), (32, 0)] * 2, 256)
    tks = min(TK_SB, min(TQ_SB, s))
    tri = jnp.asarray(np.tril(np.ones((tks, tks), np.float32)), BF16)

    c8 = jnp.pad(c, ((0, 8 - b), (0, 0)))
    mod_all = _modulation(c8, ada_w, ada_b)

    for layer in range(depth):
        j = layer // 2
        mod3 = mod_all[layer, :b].reshape(b, 1, 3 * d)
        if layer % 2 == 0:
            w, wq, wqs, wkv, vec = _pack_even(ev_w_in, j, norm_g[layer], sb_qn[j], sb_kn[j], mla_qa_g[j],
                                              mla_wq_up[j], mla_kva_g[j], mla_wkv_up[j], mla_qn[j], mla_kn[j])
            sbq, sbk, sbv, gz, mq, mk, mv = _k1_even(x, mod3, vec, w, wq, wqs, wkv, g64, gm, cos128, sin128)
            sb_bound = (_normed_len(sb_qn[j], (HEAD_DIM,)) * _normed_len(sb_kn[j], (HEAD_DIM,))
                        * (HEAD_DIM ** -0.5 * LOG2E * ROUNDING_MARGIN)).reshape(1).astype(F32)
            o_sb = _sb_attention(sbq, sbk, sbv, tri, sb_bound, sb_bound[0] <= SAFE_LOGIT_BOUND)
            o_mla = _mla_attention(mq, mk, mv, _mla_logit_bound(mla_qn[j], mla_kn[j]))
            x = _out_even(x, mod3, o_sb, o_mla, gz, ev_w_out[j].astype(BF16))
        else:
            w_nsa, w_dil, vec = _pack_odd(od_w_in, j, norm_g[layer], nsa_qn[j], nsa_kn[j], dil_qn[j], dil_kn[j])
            q, ck, cv, sk, sv, wk, wv, gates, gz = _k1_nsa(x, mod3, vec, w_nsa, g64)
            dqkv = _k1_dil(x, mod3, vec, w_dil, g64)
            wck, pek = _pack_compress(nsa_cmp_wk[j], nsa_cmp_pe_k[j], False)
            wcv, pev = _pack_compress(nsa_cmp_wv[j], nsa_cmp_pe_v[j], True)
            kgain = jnp.concatenate([nsa_kn[j], jnp.zeros((64,), F32)]).reshape(1, 128)
            kc, vc = _compress(ck, cv, wck, wcv, pek, pev, kgain, g64[0:128, 0:128])
            o_c, selb, used = _nsa_cmp(nsa_slopes, q, kc, vc, ovlt, cpos, pos_rows, gates)
            flags = used[:, :, :, 0, :].reshape(b, NSA_KV_HEADS, s // tqn, tqn // min(TQ_CMP, s),
                                                SEL_TILE_STRIDE, LANES // SEL_TILE_STRIDE).max(axis=(3, 5))
            nsa_bound = (_normed_len(nsa_qn[j], (HEAD_DIM,)) * _normed_len(nsa_kn[j], (HEAD_DIM,))
                         * (HEAD_DIM ** -0.5 * LOG2E * ROUNDING_MARGIN)).reshape(1).astype(F32)
            safe = (nsa_bound[0] <= SAFE_LOGIT_BOUND) & pos_sorted
            flags = (flags > 0) & ~(safe & sel_dead[None])
            flags = flags.astype(jnp.int32).reshape(-1)
            o_s = _nsa_gqa(flags, nsa_slopes, nsa_bound, safe, q, selb, sk, sv, pos2d, pos_col, gates, 1)
            o_w = _nsa_gqa(flags, nsa_slopes, nsa_bound, safe, q, selb, wk, wv, pos2d_win, pos_col, gates, 2)
            dil_bound = (_normed_len(dil_qn[j], (HEAD_DIM,)) * _normed_len(dil_kn[j], (HEAD_DIM,))
                         * (HEAD_DIM ** -0.5 * LOG2E * ROUNDING_MARGIN)).reshape(1).astype(F32)
            dil_safe = (dil_bound[0] <= SAFE_LOGIT_BOUND) & pos_sorted
            dils, lses = [], []
            for g in range(N_DIL):
                o, lse = _dilated(dil_slopes, dil_bound, dil_safe, dqkv[3 * g], dqkv[3 * g + 1], dqkv[3 * g + 2],
                                  pos_f, g)
                dils.append(o)
                lses.append(lse)
            x = _out_odd(x, mod3, o_c, o_s, o_w, dils, lses, gz, od_w_out[j].astype(BF16))
    return x
```

```python
import functools

import numpy as np
import jax
import jax.numpy as jnp
from jax import lax
from jax.experimental import pallas as pl
from jax.experimental.pallas import tpu as pltpu

F32 = jnp.float32
BF16 = jnp.bfloat16

D_MODEL = 1024
HEAD_DIM = 64
NORM_EPS = 1e-6
TINY = 1e-30
SB_HEADS = 8
MLA_HEADS = 8
MLA_Q_RANK = 256
MLA_KV_RANK = 128
MLA_NOPE = 64
MLA_ROPE = 32
MLA_V = 64
ROPE_BASE = 10000.0
NSA_HEADS = 12
NSA_KV_HEADS = 3
NSA_HPG = 4
CMP_LEN = 32
CMP_STRIDE = 16
SEL_LEN = 64
SEL_TOPN = 16
WIN = 512
FORCE_BONUS = 1e3
DIL_CFG = ((128, 1), (512, 4), (2048, 16))
N_DIL = 3
DIL_HEADS = 4
SB_W = SB_HEADS * HEAD_DIM
MLA_OUT = MLA_HEADS * MLA_V
NSA_W = NSA_HEADS * HEAD_DIM
NSA_KV_W = NSA_KV_HEADS * HEAD_DIM
DIL_W = DIL_HEADS * HEAD_DIM

LANES = 128
MXU_DIM = 256
NEG_MASK = -1e30
LOG2E = 1.4426950408889634
LN2 = 0.6931471805599453
F32_ZERO_EXP = 150.0
SAFE_LOGIT_BOUND = 50.0
ROUNDING_MARGIN = 1.02
SEL_OFF = 2.0 ** 30
VMEM_LIMIT = 56 * 1024 * 1024

TS_PROJ = 512
TQ_SB = 1024
TK_SB = 256
TQ_MLA = 1024
TD_MLA = 512
TQ_CMP = 512
TQ_NSA = 512
TK_NSA = 512
TK_WIN = 256
SEL_TILE_STRIDE = LANES * SEL_LEN // TK_NSA
T_DIL = 256


def _dot(a, b):
    return jnp.dot(a, b, preferred_element_type=F32)


def _dot_nt(a, b):
    return lax.dot_general(a, b, (((1,), (1,)), ((), ())), preferred_element_type=F32)


def _split_hl(a):
    hi = a.astype(BF16)
    lo = (a - hi.astype(F32)).astype(BF16)
    return hi, lo


def _dot_hl(a, b):
    hi, lo = _split_hl(a)
    return _dot(hi, b) + _dot(lo, b)


def _sigmoid(z):
    return 1.0 / (1.0 + jnp.exp(-z))


def _cparams(n_axes):
    return pltpu.CompilerParams(dimension_semantics=("arbitrary",) * n_axes,
                                vmem_limit_bytes=VMEM_LIMIT)


def _full(shape):
    n = len(shape)
    return pl.BlockSpec(shape, lambda *a, _n=n: (0,) * _n)


def _mod_kernel(c_ref, w_ref, b_ref, o_ref):
    c = c_ref[...]
    a = c * _sigmoid(c)
    ah, al = _split_hl(a)
    wh, wl = _split_hl(w_ref[0])
    o_ref[0] = _dot(ah, wh) + _dot(ah, wl) + _dot(al, wh) + b_ref[0]


def _modulation(c8, ada_w, ada_b):
    depth, d, n3 = ada_w.shape
    tn = 1024
    return pl.pallas_call(
        _mod_kernel,
        grid=(depth, n3 // tn),
        in_specs=[pl.BlockSpec((8, d), lambda l, j: (0, 0)),
                  pl.BlockSpec((1, d, tn), lambda l, j: (l, 0, j)),
                  pl.BlockSpec((1, 1, tn), lambda l, j: (l, 0, j))],
        out_specs=pl.BlockSpec((1, 8, tn), lambda l, j: (l, 0, j)),
        out_shape=jax.ShapeDtypeStruct((depth, 8, n3), F32),
        compiler_params=_cparams(2),
        name="adaln_mod",
    )(c8, ada_w, ada_b.reshape(depth, 1, n3))


def _modulated(x_ref, mod_ref, ng):
    x = x_ref[0]
    mod = mod_ref[0]
    shift = mod[:, 0:D_MODEL]
    scale = mod[:, D_MODEL:2 * D_MODEL]
    ms = jnp.mean(x * x, axis=-1, keepdims=True)
    h = x * lax.rsqrt(ms + NORM_EPS) * ng
    h = h * (1.0 + scale) + shift
    return h.astype(BF16)


def _group_sumsq(t, g):
    return _dot((t * t).astype(BF16), g)


def _group_norm_chunk(t, g, inv_cnt):
    return t * lax.rsqrt(_group_sumsq(t, g) * inv_cnt + NORM_EPS)


def _row_rms(t, gain):
    ms = jnp.mean(t * t, axis=-1, keepdims=True)
    return t * lax.rsqrt(ms + NORM_EPS) * gain


EV_COLS = 3200


def _k1_even_kernel(x_ref, mod_ref, vec_ref, w_ref, wq_ref, wqs_ref, wkv_ref, g64_ref, gm_ref,
                    cos_ref, sin_ref, sbq_o, sbk_o, sbv_o, gz_o, mq_o, mk_o, mv_o):
    hb = _modulated(x_ref, mod_ref, vec_ref[0:1, :])
    g64 = g64_ref[...]
    gm = gm_ref[...]
    inv64 = 1.0 / HEAD_DIM

    u = _dot(hb, w_ref[:, 0:1536])
    for c in range(2):
        sl = slice(c * 256, (c + 1) * 256)
        qn = _group_norm_chunk(u[:, c * 256:(c + 1) * 256], g64, inv64)
        sbq_o[0, :, sl] = (qn * vec_ref[1:2, sl]).astype(BF16)
        kn = _group_norm_chunk(u[:, 512 + c * 256:512 + (c + 1) * 256], g64, inv64)
        sbk_o[0, :, sl] = (kn * vec_ref[2:3, sl]).astype(BF16)
    sbv_o[0] = u[:, 1024:1536].astype(BF16)

    uz = _dot(hb, w_ref[:, 1536:2560])
    gz_o[0] = (uz * _sigmoid(uz)).astype(BF16)

    ul = _dot(hb, w_ref[:, 2560:3200])
    qlat = _row_rms(ul[:, 0:256], vec_ref[3:4, 0:256]).astype(BF16)
    kvlat = _row_rms(ul[:, 256:384], vec_ref[4:5, 0:128]).astype(BF16)
    krm = ul[:, 384:512]
    krs = ul[:, 512:640]
    cs = cos_ref[...]
    sn = sin_ref[...]
    cs2 = jnp.concatenate([cs, cs], axis=1)
    sn2 = jnp.concatenate([sn, sn], axis=1)

    tq = _dot(qlat, wq_ref[...])
    tqs = _dot(qlat, wqs_ref[...])
    for p in range(4):
        sl = slice(p * 256, (p + 1) * 256)
        tc = tq[:, p * 256:(p + 1) * 256]
        ss = _group_sumsq(tc, gm)
        inv = lax.rsqrt(ss * vec_ref[10:11, sl] + NORM_EPS)
        a = tc * inv * vec_ref[5:6, sl]
        b = tqs[:, p * 256:(p + 1) * 256] * inv * vec_ref[6:7, sl]
        mq_o[0, :, sl] = (a * cs2 + b * sn2).astype(BF16)

    sskr = _group_sumsq(krm, gm[0:128, 0:128])
    invr = lax.rsqrt(sskr * vec_ref[10:11, 0:128] + NORM_EPS)
    kr = (krm * invr * vec_ref[8:9, 0:128]) * cs + (krs * invr * vec_ref[9:10, 0:128]) * sn
    kr2 = jnp.concatenate([kr, kr], axis=1)

    kv = _dot(kvlat, wkv_ref[...])
    for p in range(4):
        sl = slice(p * 256, (p + 1) * 256)
        kn = _group_norm_chunk(kv[:, p * 256:(p + 1) * 256], g64, inv64)
        mk_o[0, :, sl] = (kn * vec_ref[7:8, sl] + kr2).astype(BF16)
    mv_o[0] = kv[:, 1024:1536].astype(BF16)


def _k1_even(x, mod3, vec, w, wq, wqs, wkv, g64, gm, cos128, sin128):
    b, s, d = x.shape
    ts = min(TS_PROJ, s)
    row = lambda n: pl.BlockSpec((1, ts, n), lambda i, j: (i, j, 0))
    outs = [(512, BF16), (512, BF16), (512, BF16), (1024, BF16), (1024, BF16), (1024, BF16), (512, BF16)]
    return pl.pallas_call(
        _k1_even_kernel,
        grid=(b, s // ts),
        in_specs=[row(d),
                  pl.BlockSpec((1, 1, 3 * d), lambda i, j: (i, 0, 0)),
                  _full(vec.shape), _full(w.shape), _full(wq.shape), _full(wqs.shape), _full(wkv.shape),
                  _full(g64.shape), _full(gm.shape),
                  pl.BlockSpec((ts, 128), lambda i, j: (j, 0)),
                  pl.BlockSpec((ts, 128), lambda i, j: (j, 0))],
        out_specs=[row(n) for n, _ in outs],
        out_shape=[jax.ShapeDtypeStruct((b, s, n), dt) for n, dt in outs],
        compiler_params=_cparams(2),
        name="inproj_even",
    )(x, mod3, vec, w, wq, wqs, wkv, g64, gm, cos128, sin128)


def _sb_kernel(bnd_ref, q_ref, k_ref, v_ref, tri_ref, o_ref, acc_ref, car_ref, *, tq, tk, bounded):
    qi = pl.program_id(2)
    q = q_ref[0]
    lane = lax.broadcasted_iota(jnp.int32, (1, LANES), 1)
    lo_half = lane < HEAD_DIM
    zero = jnp.zeros_like(q)
    qs = (jnp.where(lo_half, q, zero), jnp.where(lo_half, zero, q))
    tri = tri_ref[...]
    acc_ref[...] = jnp.zeros_like(acc_ref)
    car_ref[...] = jnp.zeros_like(car_ref)
    col0 = lax.broadcasted_iota(jnp.int32, (1, tk), 1)

    def tile(kt, r0, masked):
        n = tq - r0
        off = pl.multiple_of(kt * tk, tk)
        k = k_ref[0, pl.ds(off, tk), :]
        v = v_ref[0, pl.ds(off, tk), :]
        if masked:
            rows = qi * tq + r0 + lax.broadcasted_iota(jnp.int32, (n, 1), 0)
            strict = (kt * tk + col0) < rows
        for i in range(2):
            z = _dot_nt(qs[i][r0:tq], k)
            if bounded:
                lom = jnp.log(1.0 + jnp.exp2(z)) * (-LOG2E)
            else:
                lom = -jnp.maximum(z, 0.0) - jnp.log(1.0 + jnp.exp2(-jnp.abs(z))) * LOG2E
            if masked:
                lom = jnp.where(strict, lom, 0.0)
            tt = _dot(lom.astype(BF16), tri)
            car = car_ref[i, r0:tq]
            w = jnp.exp2(tt + z + jnp.concatenate([car] * (tk // LANES), axis=1))
            if masked:
                w = jnp.where(strict, w, 0.0)
            acc_ref[i, r0:tq] += _dot(w.astype(BF16), v)
            car_ref[i, r0:tq] = car + jnp.broadcast_to(tt[:, 0:1], (n, LANES))

    per = tq // tk
    for c in reversed(range(per)):
        tile(qi * per + c, c * tk, True)

    if bounded:
        limit = -(F32_ZERO_EXP + bnd_ref[0])

        def still_live():
            return jnp.max(jnp.maximum(car_ref[0], car_ref[1])) > limit

        def single(c):
            j, _ = c
            tile(qi * per - 1 - j, 0, False)
            return j + 1, still_live()

        _, live = lax.while_loop(lambda c: (c[0] < per) & c[1], single, (0, qi > 0))

        def group(c):
            g, _ = c
            for u in range(per):
                tile(qi * per - 1 - per * (g + 1) - u, 0, False)
            return g + 1, still_live()

        lax.while_loop(lambda c: (c[0] < qi - 1) & c[1], group, (0, live))
    else:
        def body(j, carry):
            for u in range(per):
                tile(qi * per - 1 - per * j - u, 0, False)
            return carry

        lax.fori_loop(0, qi, body, 0)
    o_ref[0] = jnp.where(lo_half, acc_ref[0], acc_ref[1])


def _sb_attention(q, k, v, tri, bound, safe):
    b, s, w = q.shape
    tq = min(TQ_SB, s)
    tk = min(TK_SB, tq)

    def call(bounded):
        kern = functools.partial(_sb_kernel, tq=tq, tk=tk, bounded=bounded)
        return pl.pallas_call(
            kern,
            grid=(b, w // LANES, s // tq),
            in_specs=[pl.BlockSpec(memory_space=pltpu.SMEM),
                      pl.BlockSpec((1, tq, LANES), lambda i, p, j: (i, j, p)),
                      pl.BlockSpec((1, s, LANES), lambda i, p, j: (i, 0, p)),
                      pl.BlockSpec((1, s, LANES), lambda i, p, j: (i, 0, p)),
                      _full(tri.shape)],
            out_specs=pl.BlockSpec((1, tq, LANES), lambda i, p, j: (i, j, p)),
            out_shape=jax.ShapeDtypeStruct((b, s, w), F32),
            scratch_shapes=[pltpu.VMEM((2, tq, LANES), F32), pltpu.VMEM((2, tq, LANES), F32)],
            compiler_params=_cparams(3),
            name="stickbreak_attn_bounded" if bounded else "stickbreak_attn",
        )(bound, q, k, v, tri)

    return lax.cond(safe, lambda: call(True), lambda: call(False))


def _lane_max(s):
    m = s[:, 0:LANES]
    for c in range(1, s.shape[1] // LANES):
        m = jnp.maximum(m, s[:, c * LANES:(c + 1) * LANES])
    return m


def _mla_kernel(bnd_ref, q_ref, k_ref, v_ref, o_ref, acc_ref, m_ref, *, tq, td, bounded):
    qi = pl.program_id(2)
    q = q_ref[0]
    qs = (q[:, 0:LANES], q[:, LANES:2 * LANES])
    lane = lax.broadcasted_iota(jnp.int32, (1, LANES), 1)
    lo_half = lane < HEAD_DIM
    acc_ref[...] = jnp.zeros_like(acc_ref)
    if bounded:
        m_ref[...] = jnp.full(m_ref.shape, bnd_ref[0], F32)
    else:
        m_ref[...] = jnp.full_like(m_ref, NEG_MASK)

    def scores(off, width, r0, masked):
        k = k_ref[0, pl.ds(off, width), :]
        if masked:
            rows = qi * tq + r0 + lax.broadcasted_iota(jnp.int32, (tq - r0, 1), 0)
            keep = (off + lax.broadcasted_iota(jnp.int32, (1, width), 1)) <= rows
        out = []
        for i in range(2):
            s = _dot_nt(qs[i][r0:tq], k[:, i * LANES:(i + 1) * LANES])
            if masked:
                s = jnp.where(keep, s, NEG_MASK)
            out.append(s)
        return out

    def sweep(fn):
        def body(j, carry):
            fn(pl.multiple_of(j * tq, tq), tq, 0, False)
            return carry
        lax.fori_loop(0, qi, body, 0)
        for c in range(tq // td):
            fn(pl.multiple_of(qi * tq + c * td, td), td, c * td, True)

    def row_max(off, width, r0, masked):
        for i, s in enumerate(scores(off, width, r0, masked)):
            m_ref[i, r0:tq] = jnp.maximum(m_ref[i, r0:tq], _lane_max(s))

    def accumulate(off, width, r0, masked):
        vx = jnp.concatenate([v_ref[0, pl.ds(off, width), :], jnp.ones((width, LANES), BF16)], axis=1)
        for i, s in enumerate(scores(off, width, r0, masked)):
            p = jnp.exp2(s - jnp.concatenate([m_ref[i, r0:tq]] * (width // LANES), axis=1))
            acc_ref[i, r0:tq] += _dot(p.astype(BF16), vx)

    if not bounded:
        sweep(row_max)
        for i in range(2):
            m_ref[i] = jnp.broadcast_to(jnp.max(m_ref[i], axis=-1, keepdims=True), (tq, LANES))
    sweep(accumulate)
    o_ref[0] = jnp.where(lo_half, acc_ref[0, :, 0:LANES] / acc_ref[0, :, LANES:2 * LANES],
                         acc_ref[1, :, 0:LANES] / acc_ref[1, :, LANES:2 * LANES])


def _mla_attention(q, k, v, bound):
    b, s, _ = q.shape
    tq = min(TQ_MLA, s)
    td = min(TD_MLA, tq)

    def call(bounded):
        kern = functools.partial(_mla_kernel, tq=tq, td=td, bounded=bounded)
        return pl.pallas_call(
            kern,
            grid=(b, MLA_HEADS // 2, s // tq),
            in_specs=[pl.BlockSpec(memory_space=pltpu.SMEM),
                      pl.BlockSpec((1, tq, 2 * LANES), lambda i, p, j: (i, j, p)),
                      pl.BlockSpec((1, s, 2 * LANES), lambda i, p, j: (i, 0, p)),
                      pl.BlockSpec((1, s, LANES), lambda i, p, j: (i, 0, p))],
            out_specs=pl.BlockSpec((1, tq, LANES), lambda i, p, j: (i, j, p)),
            out_shape=jax.ShapeDtypeStruct((b, s, MLA_OUT), F32),
            scratch_shapes=[pltpu.VMEM((2, tq, 2 * LANES), F32), pltpu.VMEM((2, tq, LANES), F32)],
            compiler_params=_cparams(3),
            name="mla_attn_bounded" if bounded else "mla_attn",
        )(bound, q, k, v)

    return lax.cond(bound[0] <= SAFE_LOGIT_BOUND, lambda: call(True), lambda: call(False))


def _out_even_kernel(x_ref, mod_ref, osb_ref, omla_ref, gz_ref, w_ref, o_ref):
    gz = gz_ref[0].astype(F32)
    m1 = (osb_ref[0] * gz[:, 0:SB_W]).astype(BF16)
    m2 = (omla_ref[0] * gz[:, SB_W:SB_W + MLA_OUT]).astype(BF16)
    y = _dot(m1, w_ref[0:SB_W, :]) + _dot(m2, w_ref[SB_W:SB_W + MLA_OUT, :])
    gate = mod_ref[0][:, 2 * D_MODEL:3 * D_MODEL]
    o_ref[0] = x_ref[0] + gate * y


def _out_even(x, mod3, osb, omla, gz, w):
    b, s, d = x.shape
    ts = min(TS_PROJ, s)
    row = lambda n: pl.BlockSpec((1, ts, n), lambda i, j: (i, j, 0))
    return pl.pallas_call(
        _out_even_kernel,
        grid=(b, s // ts),
        in_specs=[row(d), pl.BlockSpec((1, 1, 3 * d), lambda i, j: (i, 0, 0)),
                  row(SB_W), row(MLA_OUT), row(SB_W + MLA_OUT), _full(w.shape)],
        out_specs=row(d),
        out_shape=jax.ShapeDtypeStruct((b, s, d), F32),
        compiler_params=_cparams(2),
        name="outproj_even",
    )(x, mod3, osb, omla, gz, w)


NSA_COLS = 5248


def _k1_nsa_kernel(x_ref, mod_ref, vec_ref, w_ref, g64_ref,
                   q_o, ck_o, cv_o, sk_o, sv_o, wk_o, wv_o, gt_o, gz_o, chunk_ref):
    ts = x_ref.shape[1]
    hb = _modulated(x_ref, mod_ref, vec_ref[0:1, 0:D_MODEL])
    g64 = g64_ref[...]
    inv64 = 1.0 / HEAD_DIM

    uq = _dot(hb, w_ref[:, 0:1536])
    for c in range(6):
        sl = slice(c * 256, (c + 1) * 256)
        qn = _group_norm_chunk(uq[:, c * 256:(c + 1) * 256], g64, inv64)
        q_o[0, :, sl] = (qn * vec_ref[1:2, sl]).astype(BF16)

    uc = _dot(hb, w_ref[:, 1536:2304])
    for c in range(2 * NSA_KV_HEADS):
        chunk_ref[c] = uc[:, c * 128:(c + 1) * 128]
    for g in range(NSA_KV_HEADS):
        for tok in range(CMP_STRIDE):
            rows = pl.ds(tok, ts // CMP_STRIDE, stride=CMP_STRIDE)
            sl = slice(tok * 128, (tok + 1) * 128)
            ck_o[0, g, :, sl] = chunk_ref[g, rows, :].astype(BF16)
            cv_o[0, g, :, sl] = chunk_ref[NSA_KV_HEADS + g, rows, :].astype(BF16)

    us = _dot(hb, w_ref[:, 2304:3840])
    kgain = vec_ref[2:3, 0:384]
    g128 = g64[0:128, 0:128]
    for g in range(NSA_KV_HEADS):
        sl = slice(g * 128, (g + 1) * 128)
        t = us[:, g * 128:(g + 1) * 128]
        ss = _group_sumsq(t, g128)
        sk_o[0, :, sl] = (t * lax.rsqrt(ss * inv64 + NORM_EPS) * kgain[:, sl]).astype(BF16)
        t = us[:, 768 + g * 128:768 + (g + 1) * 128]
        ss = _group_sumsq(t, g128)
        wk_o[0, :, sl] = (t * lax.rsqrt(ss * inv64 + NORM_EPS) * kgain[:, sl]).astype(BF16)
    ones_hi = (lax.broadcasted_iota(jnp.int32, (1, 384), 1) % 128) >= HEAD_DIM
    sv_o[0] = jnp.where(ones_hi, 1.0, us[:, 384:768]).astype(BF16)
    wv_o[0] = jnp.where(ones_hi, 1.0, us[:, 1152:1536]).astype(BF16)

    ug = _dot(hb, w_ref[:, 3840:4224])
    gt_o[0] = _sigmoid(ug)
    uz = _dot(hb, w_ref[:, 4224:5248])
    gz_o[0] = (uz * _sigmoid(uz)).astype(BF16)


def _k1_nsa(x, mod3, vec, w, g64):
    b, s, d = x.shape
    ts = min(TS_PROJ, s)
    row = lambda n: pl.BlockSpec((1, ts, n), lambda i, j: (i, j, 0))
    cw = CMP_STRIDE * 128
    grp = pl.BlockSpec((1, NSA_KV_HEADS, ts // CMP_STRIDE, cw), lambda i, j: (i, 0, j, 0))
    return pl.pallas_call(
        _k1_nsa_kernel,
        grid=(b, s // ts),
        in_specs=[row(d), pl.BlockSpec((1, 1, 3 * d), lambda i, j: (i, 0, 0)),
                  _full(vec.shape), _full(w.shape), _full(g64.shape)],
        out_specs=[row(1536), grp, grp, row(384), row(384), row(384), row(384), row(384), row(1024)],
        out_shape=[jax.ShapeDtypeStruct((b, s, 1536), BF16),
                   jax.ShapeDtypeStruct((b, NSA_KV_HEADS, s // CMP_STRIDE, cw), BF16),
                   jax.ShapeDtypeStruct((b, NSA_KV_HEADS, s // CMP_STRIDE, cw), BF16),
                   jax.ShapeDtypeStruct((b, s, 384), BF16),
                   jax.ShapeDtypeStruct((b, s, 384), BF16),
                   jax.ShapeDtypeStruct((b, s, 384), BF16),
                   jax.ShapeDtypeStruct((b, s, 384), BF16),
                   jax.ShapeDtypeStruct((b, s, 384), F32),
                   jax.ShapeDtypeStruct((b, s, 1024), BF16)],
        scratch_shapes=[pltpu.VMEM((2 * NSA_KV_HEADS, ts, 128), F32)],
        compiler_params=_cparams(2),
        name="inproj_nsa",
    )(x, mod3, vec, w, g64)


def _k1_dil_kernel(x_ref, mod_ref, vec_ref, w_ref, g64_ref, *refs):
    outs, st = refs[:3 * N_DIL], refs[3 * N_DIL]
    ts = x_ref.shape[1]
    hb = _modulated(x_ref, mod_ref, vec_ref[0:1, 0:D_MODEL])
    g64 = g64_ref[...]
    inv64 = 1.0 / HEAD_DIM
    u = _dot(hb, w_ref[...])
    for g in range(N_DIL):
        dil = DIL_CFG[g][1]
        qn = _group_norm_chunk(u[:, g * 256:(g + 1) * 256], g64, inv64) * vec_ref[3:4, 0:256]
        kn = _group_norm_chunk(u[:, 768 + g * 256:768 + (g + 1) * 256], g64, inv64) * vec_ref[4:5, 0:256]
        vals = (qn, kn, u[:, 1536 + g * 256:1536 + (g + 1) * 256])
        for j, val in enumerate(vals):
            o_ref = outs[3 * g + j]
            if dil == 1:
                o_ref[0] = val.astype(BF16)
                continue
            for h in range(2):
                st[j, h] = val[:, h * 128:(h + 1) * 128]
            for r in range(dil):
                for h in range(2):
                    sl = slice(r * 256 + h * 128, r * 256 + (h + 1) * 128)
                    o_ref[0, :, sl] = st[j, h, pl.ds(r, ts // dil, stride=dil), :].astype(BF16)


def _k1_dil(x, mod3, vec, w, g64):
    b, s, d = x.shape
    ts = min(TS_PROJ, s)
    row = lambda n: pl.BlockSpec((1, ts, n), lambda i, j: (i, j, 0))
    specs, shapes = [], []
    for g in range(N_DIL):
        dil = DIL_CFG[g][1]
        for _ in range(3):
            specs.append(pl.BlockSpec((1, ts // dil, dil * 256), lambda i, j: (i, j, 0)))
            shapes.append(jax.ShapeDtypeStruct((b, s // dil, dil * 256), BF16))
    return pl.pallas_call(
        _k1_dil_kernel,
        grid=(b, s // ts),
        in_specs=[row(d), pl.BlockSpec((1, 1, 3 * d), lambda i, j: (i, 0, 0)),
                  _full(vec.shape), _full(w.shape), _full(g64.shape)],
        out_specs=specs,
        out_shape=shapes,
        scratch_shapes=[pltpu.VMEM((3, 2, ts, 128), F32)],
        compiler_params=_cparams(2),
        name="inproj_dil",
    )(x, mod3, vec, w, g64)


def _compress_kernel(xk_ref, xv_ref, wk_ref, wv_ref, pek_ref, pev_ref, gain_ref, g128_ref, kc_o, vc_o):
    nch = xk_ref.shape[2]

    def comp(x_ref, w_ref, pe_ref):
        x = x_ref[0, 0]
        lo = _dot(x, w_ref[0])
        hi = _dot(x, w_ref[1])
        pec = _dot_hl(pe_ref[0], w_ref[0]) + _dot_hl(pe_ref[1], w_ref[1])
        return lo + pltpu.roll(hi, nch - 1, 0) + pec[0:1, :]

    kc = comp(xk_ref, wk_ref, pek_ref)
    ss = _dot_hl(kc * kc, g128_ref[...])
    kc_o[0, 0] = (kc * lax.rsqrt(ss * (1.0 / HEAD_DIM) + NORM_EPS) * gain_ref[...]).astype(BF16)
    vc_o[0, 0] = comp(xv_ref, wv_ref, pev_ref).T.astype(BF16)


def _compress(ck, cv, wk, wv, pek, pev, gain, g128):
    b, g, nch, _ = ck.shape
    xk, xv = ck, cv
    blk = pl.BlockSpec((1, 1, nch, CMP_STRIDE * 128), lambda i, j: (i, j, 0, 0))
    oblk = pl.BlockSpec((1, 1, nch, 128), lambda i, j: (i, j, 0, 0))
    tblk = pl.BlockSpec((1, 1, 128, nch), lambda i, j: (i, j, 0, 0))
    return pl.pallas_call(
        _compress_kernel,
        grid=(b, g),
        in_specs=[blk, blk, _full(wk.shape), _full(wv.shape), _full(pek.shape), _full(pev.shape),
                  _full(gain.shape), _full(g128.shape)],
        out_specs=[oblk, tblk],
        out_shape=[jax.ShapeDtypeStruct((b, g, nch, 128), BF16), jax.ShapeDtypeStruct((b, g, 128, nch), BF16)],
        compiler_params=_cparams(2),
        name="nsa_compress",
    )(xk, xv, wk, wv, pek, pev, gain, g128)


def _pair_select(lo_half, a, b):
    return jnp.where(lo_half, a, b)


def _nsa_cmp_kernel(slope_ref, q_ref, kc_ref, vct_ref, ovlt_ref, cposc_ref, prow_ref, gt_ref,
                    oc_o, selb_o, any_o, *, tq):
    g = pl.program_id(1)
    qi = pl.program_id(2)
    q4 = q_ref[0]
    kc = kc_ref[0, 0]
    vct = vct_ref[0, 0]
    nch = kc.shape[0]
    t = qi * tq + lax.broadcasted_iota(jnp.int32, (1, tq), 1)
    n_id = lax.broadcasted_iota(jnp.int32, (nch, 1), 0)
    valid = (n_id * CMP_STRIDE + (CMP_LEN - 1)) <= t
    rel = (cposc_ref[...] - prow_ref[qi][:, 0:1]) * LOG2E
    gtt = gt_ref[0].T
    row_lo = lax.broadcasted_iota(jnp.int32, (LANES, 1), 0) < HEAD_DIM
    psum = jnp.zeros((nch, tq), F32)
    outs = []
    for hh in range(NSA_HPG):
        s = _dot_nt(kc, q4[:, hh * LANES:(hh + 1) * LANES]) + slope_ref[g * NSA_HPG + hh] * rel
        s = jnp.where(valid, s, -jnp.inf)
        mx = jnp.max(s, axis=0, keepdims=True)
        mx = jnp.where(mx == -jnp.inf, 0.0, mx)
        e = jnp.exp2(s - mx)
        den = jnp.maximum(jnp.sum(e, axis=0, keepdims=True), TINY)
        p = e * (1.0 / den)
        psum = psum + p
        outs.append(_dot(vct, p.astype(BF16)) * gtt[3 * hh:3 * hh + 1, :])
    oc_o[0, :, 0:LANES] = jnp.where(row_lo, outs[0], outs[1]).T
    oc_o[0, :, LANES:2 * LANES] = jnp.where(row_lo, outs[2], outs[3]).T

    hi, lo = _split_hl(psum)
    imp = _dot(ovlt_ref[...], hi) + _dot(ovlt_ref[...], lo)
    blk = lax.broadcasted_iota(jnp.int32, (LANES, 1), 0)
    cur = t >> 6
    forced = (blk == 0) | (blk == cur) | (blk == cur - 1)
    allowed = blk <= cur
    score = jnp.where(allowed, imp + jnp.where(forced, FORCE_BONUS, 0.0), -jnp.inf)
    blk_f = blk.astype(F32)

    def pick(_, sc):
        mx = jnp.max(sc, axis=0, keepdims=True)
        idx = jnp.min(jnp.where(sc == mx, blk_f, float(LANES)), axis=0, keepdims=True)
        return jnp.where(blk_f == idx, -jnp.inf, sc)

    left = lax.fori_loop(0, SEL_TOPN, pick, score)
    chosen = jnp.where((left == -jnp.inf) & allowed, 1.0, 0.0).T
    selb_o[0, 0] = ((chosen - 1.0) * SEL_OFF).astype(BF16)
    used = jnp.max(chosen, axis=0, keepdims=True)
    any_o[0, 0, 0] = jnp.broadcast_to(used, (8, LANES))


def _nsa_cmp(slopes, q, kc, vct, ovlt, cposc, pos_rows, gates):
    b, s, _ = q.shape
    nch = kc.shape[2]
    tq = pos_rows.shape[2]
    kern = functools.partial(_nsa_cmp_kernel, tq=tq)
    return pl.pallas_call(
        kern,
        grid=(b, NSA_KV_HEADS, s // tq),
        in_specs=[pl.BlockSpec(memory_space=pltpu.SMEM),
                  pl.BlockSpec((1, tq, 4 * LANES), lambda i, g, j: (i, j, g)),
                  pl.BlockSpec((1, 1, nch, LANES), lambda i, g, j: (i, g, 0, 0)),
                  pl.BlockSpec((1, 1, LANES, nch), lambda i, g, j: (i, g, 0, 0)),
                  _full(ovlt.shape), _full(cposc.shape), _full(pos_rows.shape),
                  pl.BlockSpec((1, tq, LANES), lambda i, g, j: (i, j, g))],
        out_specs=[pl.BlockSpec((1, tq, 2 * LANES), lambda i, g, j: (i, j, g)),
                   pl.BlockSpec((1, 1, tq, LANES), lambda i, g, j: (i, g, j, 0)),
                   pl.BlockSpec((1, 1, 1, 8, LANES), lambda i, g, j: (i, g, j, 0, 0))],
        out_shape=[jax.ShapeDtypeStruct((b, s, NSA_W), F32),
                   jax.ShapeDtypeStruct((b, NSA_KV_HEADS, s, LANES), BF16),
                   jax.ShapeDtypeStruct((b, NSA_KV_HEADS, s // tq, 8, LANES), F32)],
        compiler_params=_cparams(3),
        name="nsa_cmp_topk",
    )(slopes, q, kc, vct, ovlt, cposc, pos_rows, gates)


def _gqa_kernel(flag_ref, slope_ref, bnd_ref, q_ref, selb_ref, k_ref, v_ref, pos_ref, pcol_ref, gt_ref, o_ref,
                qa_ref, acc_ref, m_ref, *, tq, tk, branch, bounded):
    bi = pl.program_id(0)
    g = pl.program_id(1)
    qi = pl.program_id(2)
    q4 = q_ref[0]
    lane = lax.broadcasted_iota(jnp.int32, (1, LANES), 1)
    lo_half = lane < HEAD_DIM
    pref = pos_ref[qi * (tq // tk)][:, 0:1]
    for hh in range(NSA_HPG):
        sl = slice(hh * tq, (hh + 1) * tq)
        qa_ref[sl, 0:LANES] = q4[:, hh * LANES:(hh + 1) * LANES]
        if branch == 1:
            qa_ref[sl, LANES:2 * LANES] = selb_ref[0, 0]
        if bounded:
            own = (pcol_ref[...] - pref) * (LOG2E * slope_ref[g * NSA_HPG + hh]) + bnd_ref[0]
            m_ref[sl] = jnp.broadcast_to(own, (tq, LANES))
    acc_ref[...] = jnp.zeros_like(acc_ref)
    if not bounded:
        m_ref[...] = jnp.full_like(m_ref, NEG_MASK)
    rows = qi * tq + lax.broadcasted_iota(jnp.int32, (tq, 1), 0)
    col0 = lax.broadcasted_iota(jnp.int32, (1, tk), 1)
    krow = lax.broadcasted_iota(jnp.int32, (tk, 1), 0)
    per = tq // tk
    nrep = tk // LANES
    flag0 = ((bi * NSA_KV_HEADS + g) * pl.num_programs(2) + qi) * SEL_TILE_STRIDE

    def scores(kt_true, masked, r_lo, r_hi):
        kt = jnp.maximum(kt_true, 0)
        off = pl.multiple_of(kt * tk, tk)
        k = k_ref[0, pl.ds(off, tk), :]
        if branch == 1:
            blk = (kt * tk + krow) >> 6
            onehot = jnp.where(lane == blk, 1.0, 0.0).astype(BF16)
            s_all = _dot_nt(qa_ref[...], jnp.concatenate([k, onehot], axis=1))
            parts = [s_all[hh * tq:(hh + 1) * tq] for hh in range(NSA_HPG)]
        else:
            parts = [_dot_nt(qa_ref[hh * tq + r_lo:hh * tq + r_hi, 0:LANES], k) for hh in range(NSA_HPG)]
        rel = (pos_ref[kt] - pref) * LOG2E
        if masked:
            cols = kt_true * tk + col0
            d = rows[r_lo:r_hi] - cols
            keep = d >= 0
            if branch == 2:
                keep = keep & (d < WIN) & (cols >= 0)
        out = []
        for hh in range(NSA_HPG):
            s = parts[hh] + slope_ref[g * NSA_HPG + hh] * rel
            if masked:
                s = jnp.where(keep, s, NEG_MASK)
            out.append(s)
        return out

    def sweep(fn):
        if branch == 1:
            def body(j, carry):
                @pl.when(flag_ref[flag0 + j] != 0)
                def _():
                    fn(j, False, 0, tq)
                return carry
            lax.fori_loop(0, qi * per, body, 0)
            for dd in range(per):
                fn(qi * per + dd, True, 0, tq)
        else:
            for c in range(-(WIN // tk), per):
                r_lo = max(0, c * tk)
                r_hi = min(tq, -(-(c * tk + tk + WIN - 1) // 8) * 8)
                fn(qi * per + c, True, r_lo, r_hi)

    def row_max(kt_true, masked, r_lo, r_hi):
        for hh, s in enumerate(scores(kt_true, masked, r_lo, r_hi)):
            sl = slice(hh * tq + r_lo, hh * tq + r_hi)
            m_ref[sl] = jnp.maximum(m_ref[sl], _lane_max(s))

    def accumulate(kt_true, masked, r_lo, r_hi):
        off = pl.multiple_of(jnp.maximum(kt_true, 0) * tk, tk)
        v = v_ref[0, pl.ds(off, tk), :]
        for hh, s in enumerate(scores(kt_true, masked, r_lo, r_hi)):
            sl = slice(hh * tq + r_lo, hh * tq + r_hi)
            p = jnp.exp2(s - jnp.concatenate([m_ref[sl]] * nrep, axis=1))
            acc_ref[sl] += _dot(p.astype(BF16), v)

    if not bounded:
        sweep(row_max)
        for hh in range(NSA_HPG):
            sl = slice(hh * tq, (hh + 1) * tq)
            m_ref[sl] = jnp.broadcast_to(jnp.max(m_ref[sl], axis=-1, keepdims=True), (tq, LANES))
    sweep(accumulate)

    gt = gt_ref[0]
    for pr in range(2):
        res = []
        for x in range(2):
            hh = 2 * pr + x
            a = acc_ref[hh * tq:(hh + 1) * tq]
            r = pltpu.roll(a, HEAD_DIM, 1)
            o = a / r if x == 0 else r / a
            res.append(o * gt[:, 3 * hh + branch:3 * hh + branch + 1])
        o_ref[0, :, pr * LANES:(pr + 1) * LANES] = _pair_select(lo_half, res[0], res[1])


def _nsa_gqa(flags, slopes, bound, safe, q, selb, k, v, pos2d, pos_col, gates, branch):
    b, s, _ = q.shape
    tq = min(TQ_NSA, s)
    tk = pos2d.shape[2]

    def call(bounded):
        kern = functools.partial(_gqa_kernel, tq=tq, tk=tk, branch=branch, bounded=bounded)
        smem = pl.BlockSpec(memory_space=pltpu.SMEM)
        name = ("nsa_sel_attn" if branch == 1 else "nsa_win_attn") + ("_bounded" if bounded else "")
        return pl.pallas_call(
            kern,
            grid=(b, NSA_KV_HEADS, s // tq),
            in_specs=[smem, smem, smem,
                      pl.BlockSpec((1, tq, 4 * LANES), lambda i, g, j: (i, j, g)),
                      pl.BlockSpec((1, 1, tq, LANES), lambda i, g, j: (i, g, j, 0)),
                      pl.BlockSpec((1, s, LANES), lambda i, g, j: (i, 0, g)),
                      pl.BlockSpec((1, s, LANES), lambda i, g, j: (i, 0, g)),
                      _full(pos2d.shape),
                      pl.BlockSpec((tq, 1), lambda i, g, j: (j, 0)),
                      pl.BlockSpec((1, tq, LANES), lambda i, g, j: (i, j, g))],
            out_specs=pl.BlockSpec((1, tq, 2 * LANES), lambda i, g, j: (i, j, g)),
            out_shape=jax.ShapeDtypeStruct((b, s, NSA_W), F32),
            scratch_shapes=[pltpu.VMEM((NSA_HPG * tq, 2 * LANES), BF16),
                            pltpu.VMEM((NSA_HPG * tq, LANES), F32),
                            pltpu.VMEM((NSA_HPG * tq, LANES), F32)],
            compiler_params=_cparams(3),
            name=name,
        )(flags, slopes, bound, q, selb, k, v, pos2d, pos_col, gates)

    return lax.cond(safe, lambda: call(True), lambda: call(False))


def _dil_kernel(slope_ref, bnd_ref, q_ref, k_ref, v_ref, pc_ref, pr_ref, o_ref, lse_ref,
                *, t, span, grp, bounded):
    i = pl.program_id(2)
    q4 = q_ref[0]
    lane = lax.broadcasted_iota(jnp.int32, (1, LANES), 1)
    lo_half = lane < HEAD_DIM
    per = t // LANES
    prev = jnp.maximum(i * per - 1, 0)
    o_prev = pl.multiple_of(prev * LANES, LANES)
    o_cur = pl.multiple_of(i * t, t)
    kk = jnp.concatenate([k_ref[0, pl.ds(o_prev, LANES), :], k_ref[0, pl.ds(o_cur, t), :]], axis=0)
    vv = jnp.concatenate([v_ref[0, pl.ds(o_prev, LANES), :], v_ref[0, pl.ds(o_cur, t), :]], axis=0)
    pk = jnp.concatenate([pr_ref[0, prev]] + [pr_ref[0, i * per + c] for c in range(per)], axis=1)
    dist = (pc_ref[0] - pk) * LOG2E
    rows = i * t + lax.broadcasted_iota(jnp.int32, (t, 1), 0)
    cols = i * t - LANES + lax.broadcasted_iota(jnp.int32, (1, t + LANES), 1)
    d = rows - cols
    valid = (cols >= 0) & (d >= 0) & (d <= span)
    zero = jnp.zeros((t, LANES), BF16)
    ones = jnp.ones((t + LANES, LANES), BF16)
    for p in range(2):
        qp = q4[:, p * LANES:(p + 1) * LANES]
        kp = kk[:, p * LANES:(p + 1) * LANES]
        vx = jnp.concatenate([vv[:, p * LANES:(p + 1) * LANES], ones], axis=1)
        res = []
        for x in range(2):
            qx = jnp.where(lo_half, qp, zero) if x == 0 else jnp.where(lo_half, zero, qp)
            s = _dot_nt(qx, kp) - slope_ref[grp * DIL_HEADS + 2 * p + x] * dist
            s = jnp.where(valid, s, -jnp.inf)
            mx = bnd_ref[0] if bounded else jnp.max(s, axis=-1, keepdims=True)
            acc = _dot(jnp.exp2(s - mx).astype(BF16), vx)
            den = acc[:, LANES:2 * LANES]
            res.append((acc[:, 0:LANES] / den, mx * LN2 + jnp.log(den)))
        sl = slice(p * LANES, (p + 1) * LANES)
        o_ref[0, :, sl] = _pair_select(lo_half, res[0][0], res[1][0])
        lse_ref[0, :, sl] = jnp.where(lo_half, res[0][1], res[1][1])


def _dilated(slopes, bound, safe, qv, kv, vv, pos_f, grp):
    window, dil = DIL_CFG[grp]
    b, sub, wd = qv.shape
    w = wd // dil
    t = min(T_DIL, sub)
    span = window // dil
    assert span <= LANES and t % LANES == 0
    pres = pos_f.reshape(sub, dil).T
    pc = pres.reshape(dil, sub, 1)
    pr = pres.reshape(dil, sub // LANES, 1, LANES)
    shp = jax.ShapeDtypeStruct((b, sub, wd), F32)
    smem = pl.BlockSpec(memory_space=pltpu.SMEM)

    def call(bounded):
        kern = functools.partial(_dil_kernel, t=t, span=span, grp=grp, bounded=bounded)
        return pl.pallas_call(
            kern,
            grid=(b, dil, sub // t),
            in_specs=[smem, smem,
                      pl.BlockSpec((1, t, w), lambda i, r, j: (i, j, r)),
                      pl.BlockSpec((1, sub, w), lambda i, r, j: (i, 0, r)),
                      pl.BlockSpec((1, sub, w), lambda i, r, j: (i, 0, r)),
                      pl.BlockSpec((1, t, 1), lambda i, r, j: (r, j, 0)),
                      pl.BlockSpec((1, sub // LANES, 1, LANES), lambda i, r, j: (r, 0, 0, 0))],
            out_specs=[pl.BlockSpec((1, t, w), lambda i, r, j: (i, j, r)),
                       pl.BlockSpec((1, t, w), lambda i, r, j: (i, j, r))],
            out_shape=[shp, shp],
            compiler_params=_cparams(3),
            name="dilated_attn_g%d%s" % (grp, "_bounded" if bounded else ""),
        )(slopes, bound, qv, kv, vv, pc, pr)

    return lax.cond(safe, lambda: call(True), lambda: call(False))


def _out_odd_kernel(x_ref, mod_ref, oc_ref, os_ref, ow_ref, d0_ref, d1_ref, d2_ref,
                    l0_ref, l1_ref, l2_ref, gz_ref, w_ref, o_ref, st):
    ts = x_ref.shape[1]
    gz = gz_ref[0].astype(F32)
    nsa = oc_ref[0] + os_ref[0] + ow_ref[0]
    m1 = (nsa * gz[:, 0:NSA_W]).astype(BF16)

    def token_order(ref, g, slot):
        dil = DIL_CFG[g][1]
        if dil == 1:
            return ref[0]
        for r in range(dil):
            for h in range(2):
                st[slot, h, pl.ds(r, ts // dil, stride=dil), :] = ref[0, :, r * DIL_W + h * 128:r * DIL_W + (h + 1) * 128]
        return jnp.concatenate([st[slot, 0], st[slot, 1]], axis=1)

    d0, d1, d2 = token_order(d0_ref, 0, 0), token_order(d1_ref, 1, 0), token_order(d2_ref, 2, 1)
    l0, l1, l2 = token_order(l0_ref, 0, 2), token_order(l1_ref, 1, 2), token_order(l2_ref, 2, 3)
    mx = jnp.maximum(jnp.maximum(l0, l1), l2)
    e0, e1, e2 = jnp.exp(l0 - mx), jnp.exp(l1 - mx), jnp.exp(l2 - mx)
    dil = (e0 * d0 + e1 * d1 + e2 * d2) / (e0 + e1 + e2)
    m2 = (dil * gz[:, NSA_W:NSA_W + DIL_W]).astype(BF16)
    y = _dot(m1, w_ref[0:NSA_W, :]) + _dot(m2, w_ref[NSA_W:NSA_W + DIL_W, :])
    gate = mod_ref[0][:, 2 * D_MODEL:3 * D_MODEL]
    o_ref[0] = x_ref[0] + gate * y


def _out_odd(x, mod3, oc, os_, ow, dils, lses, gz, w):
    b, s, d = x.shape
    ts = min(TS_PROJ, s)
    row = lambda n: pl.BlockSpec((1, ts, n), lambda i, j: (i, j, 0))
    res = [pl.BlockSpec((1, ts // DIL_CFG[g][1], DIL_CFG[g][1] * DIL_W), lambda i, j: (i, j, 0))
           for g in range(N_DIL)]
    return pl.pallas_call(
        _out_odd_kernel,
        grid=(b, s // ts),
        in_specs=[row(d), pl.BlockSpec((1, 1, 3 * d), lambda i, j: (i, 0, 0)),
                  row(NSA_W), row(NSA_W), row(NSA_W)] + res + res + [row(1024), _full(w.shape)],
        out_specs=row(d),
        out_shape=jax.ShapeDtypeStruct((b, s, d), F32),
        scratch_shapes=[pltpu.VMEM((4, 2, ts, 128), F32)],
        compiler_params=_cparams(2),
        name="outproj_odd",
    )(x, mod3, oc, os_, ow, *dils, *lses, gz, w)


def _pad_cols(w, n):
    return jnp.pad(w, ((0, 0), (0, n - w.shape[1])))


def _pad_vec(v, n=D_MODEL):
    return jnp.pad(v, (0, n - v.shape[0]))


def _group_matrix(sizes, total):
    m = np.zeros((total, total), np.float32)
    off = 0
    for sz, on in sizes:
        if on:
            m[off:off + sz, off:off + sz] = 1.0
        off += sz
    return jnp.asarray(m, BF16)


def _swap_halves(w):
    h = w.shape[-1] // 2
    return jnp.concatenate([w[..., h:], w[..., :h]], axis=-1)


def _column_blocks(segments):
    blocks, cur, room = [], [], LANES
    for src, width in segments:
        while width:
            n = min(width, room)
            cur.append((src, n))
            src = None if src is None else src + n
            width -= n
            room -= n
            if room == 0:
                blocks.append(cur)
                cur, room = [], LANES
    assert not cur
    return blocks


def _pack_cols_kernel(w_ref, o_ref, *, blocks):
    _, rows, n_src = w_ref.shape
    for bi, pieces in enumerate(blocks):
        parts = []
        for src, width in pieces:
            if src is None:
                parts.append(jnp.zeros((rows, width), F32))
                continue
            a0 = src // LANES * LANES
            a1 = min(-(-(src + width) // LANES) * LANES, n_src)
            parts.append(w_ref[0, :, a0:a1][:, src - a0:src - a0 + width])
        blk = parts[0] if len(parts) == 1 else jnp.concatenate(parts, axis=1)
        o_ref[:, bi * LANES:(bi + 1) * LANES] = blk.astype(BF16)


def _pack_cols(w, layer, segments):
    blocks = _column_blocks(segments)
    _, d, n_src = w.shape
    tr = 256
    return pl.pallas_call(
        functools.partial(_pack_cols_kernel, blocks=blocks),
        grid=(d // tr,),
        in_specs=[pl.BlockSpec((1, tr, n_src), lambda i: (layer, i, 0))],
        out_specs=pl.BlockSpec((tr, len(blocks) * LANES), lambda i: (i, 0)),
        out_shape=jax.ShapeDtypeStruct((d, len(blocks) * LANES), BF16),
        compiler_params=_cparams(1),
        name="pack_weight_columns",
    )(w)


EVEN_SEGMENTS = ((0, 1536), (1536, 512), (2464, 512), (2048, 256), (2304, 128),
                 (None, 64), (2432, 32), (None, 32), (None, 64), (2448, 16), (2432, 16), (None, 32))


def _pack_even(w_in_all, layer, norm_g, sb_qn, sb_kn, qa_g, wq_up, kva_g, wkv_up, qn, kn):
    w = _pack_cols(w_in_all, layer, EVEN_SEGMENTS)
    wq3 = wq_up.reshape(MLA_Q_RANK, MLA_HEADS, MLA_NOPE + MLA_ROPE)
    zq = jnp.zeros((MLA_Q_RANK, MLA_HEADS, 32), wq_up.dtype)
    wq = jnp.concatenate([wq3, zq], axis=-1).reshape(MLA_Q_RANK, MLA_HEADS * 128).astype(BF16)
    wqs = jnp.concatenate([jnp.zeros((MLA_Q_RANK, MLA_HEADS, 64), wq_up.dtype),
                           _swap_halves(wq3[..., MLA_NOPE:]), zq], axis=-1)
    wqs = wqs.reshape(MLA_Q_RANK, MLA_HEADS * 128).astype(BF16)
    wkv3 = wkv_up.reshape(MLA_KV_RANK, MLA_HEADS, MLA_NOPE + MLA_V)
    wk = jnp.concatenate([wkv3[..., :MLA_NOPE], jnp.zeros_like(wkv3[..., :MLA_NOPE])], axis=-1)
    wkv = jnp.concatenate([wk.reshape(MLA_KV_RANK, MLA_HEADS * 128),
                           wkv3[..., MLA_NOPE:].reshape(MLA_KV_RANK, MLA_HEADS * MLA_V)], axis=1).astype(BF16)
    z32 = jnp.zeros((32,), F32)
    z64 = jnp.zeros((64,), F32)
    scale = (MLA_NOPE + MLA_ROPE) ** -0.5 * LOG2E
    qg = jnp.tile(jnp.concatenate([qn, z32]), MLA_HEADS) * scale
    qgs = jnp.tile(jnp.concatenate([z64, _swap_halves(qn[MLA_NOPE:]), z32]), MLA_HEADS) * scale
    kg = jnp.tile(jnp.concatenate([kn[:MLA_NOPE], z64]), MLA_HEADS)
    krg = jnp.concatenate([z64, kn[MLA_NOPE:], z32])
    krgs = jnp.concatenate([z64, _swap_halves(kn[MLA_NOPE:]), z32])
    cnt = jnp.tile(jnp.concatenate([jnp.full((64,), 1.0 / 64), jnp.full((32,), 1.0 / 32), jnp.ones((32,))]),
                   MLA_HEADS).astype(F32)
    rows = [norm_g, _pad_vec(jnp.tile(sb_qn, SB_HEADS) * (LOG2E * HEAD_DIM ** -0.5)), _pad_vec(jnp.tile(sb_kn, SB_HEADS)),
            _pad_vec(qa_g), _pad_vec(kva_g), qg, qgs, kg, _pad_vec(krg), _pad_vec(krgs), cnt]
    rows += [jnp.zeros((D_MODEL,), F32)] * (16 - len(rows))
    return w, wq, wqs, wkv, jnp.stack(rows).astype(F32)


def _padded_heads(start, n):
    return tuple(seg for h in range(n) for seg in ((start + h * HEAD_DIM, HEAD_DIM), (None, HEAD_DIM)))


NSA_SEGMENTS = (_padded_heads(0, NSA_HEADS)
                + tuple(seg for c in range(6) for seg in _padded_heads(768 + c * NSA_KV_W, NSA_KV_HEADS))
                + tuple(seg for g in range(NSA_KV_HEADS)
                        for seg in ((1920 + g * NSA_HPG * 3, NSA_HPG * 3), (None, LANES - NSA_HPG * 3)))
                + ((1956, NSA_W), (5028, DIL_W)))
DIL_SEGMENTS = ((2724, 3 * N_DIL * DIL_W),)


def _pack_odd(w_in_all, layer, norm_g, nsa_qn, nsa_kn, dil_qn, dil_kn):
    w_nsa = _pack_cols(w_in_all, layer, NSA_SEGMENTS)
    w_dil = _pack_cols(w_in_all, layer, DIL_SEGMENTS)
    z64 = jnp.zeros((64,), F32)
    n = NSA_HEADS * 128
    rows = [_pad_vec(norm_g, n),
            jnp.tile(jnp.concatenate([nsa_qn * (LOG2E * HEAD_DIM ** -0.5), z64]), NSA_HEADS),
            _pad_vec(jnp.tile(jnp.concatenate([nsa_kn, z64]), NSA_KV_HEADS), n),
            _pad_vec(jnp.tile(dil_qn, DIL_HEADS) * (LOG2E * HEAD_DIM ** -0.5), n),
            _pad_vec(jnp.tile(dil_kn, DIL_HEADS), n)]
    rows += [jnp.zeros((n,), F32)] * (8 - len(rows))
    return w_nsa, w_dil, jnp.stack(rows).astype(F32)


def _pack_compress(w, pe, double):
    w3 = w.reshape(CMP_LEN, HEAD_DIM, HEAD_DIM)
    w3 = jnp.concatenate([w3, jnp.zeros_like(w3)], axis=1)
    w3 = jnp.concatenate([w3, w3 if double else jnp.zeros_like(w3)], axis=2)
    wp = w3.reshape(2, CMP_STRIDE * 128, 128).astype(BF16)
    pe2 = jnp.concatenate([pe, jnp.zeros_like(pe)], axis=1).reshape(2, 1, CMP_STRIDE * 128)
    pe2 = jnp.broadcast_to(pe2, (2, 8, CMP_STRIDE * 128)).astype(F32)
    return wp, pe2


def _normed_len(gain, sizes):
    tot, off = 0.0, 0
    for n in sizes:
        tot = tot + n * jnp.max(jnp.square(gain[off:off + n]))
        off += n
    return jnp.sqrt(tot)


def _mla_logit_bound(qn, kn):
    sizes = (MLA_NOPE, MLA_ROPE)
    scale = (MLA_NOPE + MLA_ROPE) ** -0.5 * LOG2E
    return (_normed_len(qn, sizes) * _normed_len(kn, sizes) * (scale * ROUNDING_MARGIN)).reshape(1).astype(F32)


def _alibi_slopes(n):
    return 2.0 ** (-8.0 * jnp.arange(1, n + 1, dtype=jnp.float32) / n)


def kernel(x, c, positions, ada_w, ada_b, norm_g, ev_w_in, ev_w_out, sb_qn, sb_kn, mla_qa_g, mla_wq_up,
           mla_kva_g, mla_wkv_up, mla_qn, mla_kn, od_w_in, od_w_out, nsa_qn, nsa_kn, nsa_cmp_wk, nsa_cmp_wv,
           nsa_cmp_pe_k, nsa_cmp_pe_v, dil_qn, dil_kn):
    b, s, d = x.shape
    depth = ada_w.shape[0]
    pos_f = positions.astype(F32)

    inv_freq = ROPE_BASE ** (-jnp.arange(0, MLA_ROPE, 2, dtype=F32) / MLA_ROPE)
    ang = pos_f[:, None] * inv_freq[None, :]
    cos, sin = jnp.cos(ang), jnp.sin(ang)
    cos128 = jnp.concatenate([jnp.ones((s, 64), F32), cos, cos, jnp.zeros((s, 32), F32)], axis=1)
    sin128 = jnp.concatenate([jnp.zeros((s, 64), F32), -sin, sin, jnp.zeros((s, 32), F32)], axis=1)
    nsa_slopes = _alibi_slopes(NSA_HEADS)
    dil_slopes = _alibi_slopes(N_DIL * DIL_HEADS)
    nch = s // CMP_STRIDE
    chunk_sum = pos_f.reshape(nch, CMP_STRIDE).sum(axis=1)
    cpos = ((chunk_sum + jnp.roll(chunk_sum, -1)) / CMP_LEN).reshape(nch, 1)
    n_sel = s // SEL_LEN
    cst = np.arange(nch)[:, None] * CMP_STRIDE
    jst = np.arange(LANES)[None, :] * SEL_LEN
    ovl = ((cst <= jst + SEL_LEN - 1) & (cst + CMP_LEN - 1 >= jst) & (np.arange(LANES)[None, :] < n_sel))
    ovlt = jnp.asarray(ovl.astype(np.float32).T, BF16)
    pos_col = pos_f.reshape(s, 1)
    tqc = min(TQ_CMP, s)
    pos_rows = pos_f.reshape(s // tqc, 1, tqc)
    tqn = min(TQ_NSA, s)
    tkn = min(TK_NSA, tqn)
    assert tkn == TK_NSA
    pos2d = pos_f.reshape(s // tkn, 1, tkn)
    pos2d_win = pos_f.reshape(s // TK_WIN, 1, TK_WIN)
    q_first = pos_f[::tqn]
    k_last = pos_f[jnp.minimum((jnp.arange(SEL_TILE_STRIDE) + 1) * tkn - 1, s - 1)]
    group_slope = jnp.min(nsa_slopes.reshape(NSA_KV_HEADS, NSA_HPG), axis=1) * LOG2E
    sel_dead = group_slope[:, None, None] * (q_first[:, None] - k_last[None, :])[None] > F32_ZERO_EXP
    pos_sorted = jnp.all(pos_f[1:] >= pos_f[:-1])

    g64 = _group_matrix([(64, 1)] * 4, 256)
    gm = _group_matrix([(64, 1), (32, 1), (32, 0)] * 2, 256)
    tks = min(TK_SB, min(TQ_SB, s))
    tri = jnp.asarray(np.tril(np.ones((tks, tks), np.float32)), BF16)

    c8 = jnp.pad(c, ((0, 8 - b), (0, 0)))
    mod_all = _modulation(c8, ada_w, ada_b)

    for layer in range(depth):
        j = layer // 2
        mod3 = mod_all[layer, :b].reshape(b, 1, 3 * d)
        if layer % 2 == 0:
            w, wq, wqs, wkv, vec = _pack_even(ev_w_in, j, norm_g[layer], sb_qn[j], sb_kn[j], mla_qa_g[j],
                                              mla_wq_up[j], mla_kva_g[j], mla_wkv_up[j], mla_qn[j], mla_kn[j])
            sbq, sbk, sbv, gz, mq, mk, mv = _k1_even(x, mod3, vec, w, wq, wqs, wkv, g64, gm, cos128, sin128)
            sb_bound = (_normed_len(sb_qn[j], (HEAD_DIM,)) * _normed_len(sb_kn[j], (HEAD_DIM,))
                        * (HEAD_DIM ** -0.5 * LOG2E * ROUNDING_MARGIN)).reshape(1).astype(F32)
            o_sb = _sb_attention(sbq, sbk, sbv, tri, sb_bound, sb_bound[0] <= SAFE_LOGIT_BOUND)
            o_mla = _mla_attention(mq, mk, mv, _mla_logit_bound(mla_qn[j], mla_kn[j]))
            x = _out_even(x, mod3, o_sb, o_mla, gz, ev_w_out[j].astype(BF16))
        else:
            w_nsa, w_dil, vec = _pack_odd(od_w_in, j, norm_g[layer], nsa_qn[j], nsa_kn[j], dil_qn[j], dil_kn[j])
            q, ck, cv, sk, sv, wk, wv, gates, gz = _k1_nsa(x, mod3, vec, w_nsa, g64)
            dqkv = _k1_dil(x, mod3, vec, w_dil, g64)
            wck, pek = _pack_compress(nsa_cmp_wk[j], nsa_cmp_pe_k[j], False)
            wcv, pev = _pack_compress(nsa_cmp_wv[j], nsa_cmp_pe_v[j], True)
            kgain = jnp.concatenate([nsa_kn[j], jnp.zeros((64,), F32)]).reshape(1, 128)
            kc, vc = _compress(ck, cv, wck, wcv, pek, pev, kgain, g64[0:128, 0:128])
            o_c, selb, used = _nsa_cmp(nsa_slopes, q, kc, vc, ovlt, cpos, pos_rows, gates)
            flags = used[:, :, :, 0, :].reshape(b, NSA_KV_HEADS, s // tqn, tqn // min(TQ_CMP, s),
                                                SEL_TILE_STRIDE, LANES // SEL_TILE_STRIDE).max(axis=(3, 5))
            nsa_bound = (_normed_len(nsa_qn[j], (HEAD_DIM,)) * _normed_len(nsa_kn[j], (HEAD_DIM,))
                         * (HEAD_DIM ** -0.5 * LOG2E * ROUNDING_MARGIN)).reshape(1).astype(F32)
            safe = (nsa_bound[0] <= SAFE_LOGIT_BOUND) & pos_sorted
            flags = (flags > 0) & ~(safe & sel_dead[None])
            flags = flags.astype(jnp.int32).reshape(-1)
            o_s = _nsa_gqa(flags, nsa_slopes, nsa_bound, safe, q, selb, sk, sv, pos2d, pos_col, gates, 1)
            o_w = _nsa_gqa(flags, nsa_slopes, nsa_bound, safe, q, selb, wk, wv, pos2d_win, pos_col, gates, 2)
            dil_bound = (_normed_len(dil_qn[j], (HEAD_DIM,)) * _normed_len(dil_kn[j], (HEAD_DIM,))
                         * (HEAD_DIM ** -0.5 * LOG2E * ROUNDING_MARGIN)).reshape(1).astype(F32)
            dil_safe = (dil_bound[0] <= SAFE_LOGIT_BOUND) & pos_sorted
            dils, lses = [], []
            for g in range(N_DIL):
                o, lse = _dilated(dil_slopes, dil_bound, dil_safe, dqkv[3 * g], dqkv[3 * g + 1], dqkv[3 * g + 2],
                                  pos_f, g)
                dils.append(o)
                lses.append(lse)
            x = _out_odd(x, mod3, o_c, o_s, o_w, dils, lses, gz, od_w_out[j].astype(BF16))
    return x
```

```python
import functools

import numpy as np
import jax
import jax.numpy as jnp
from jax import lax
from jax.experimental import pallas as pl
from jax.experimental.pallas import tpu as pltpu

F32 = jnp.float32
BF16 = jnp.bfloat16

D_MODEL = 1024
HEAD_DIM = 64
NORM_EPS = 1e-6
TINY = 1e-30
SB_HEADS = 8
MLA_HEADS = 8
MLA_Q_RANK = 256
MLA_KV_RANK = 128
MLA_NOPE = 64
MLA_ROPE = 32
MLA_V = 64
ROPE_BASE = 10000.0
NSA_HEADS = 12
NSA_KV_HEADS = 3
NSA_HPG = 4
CMP_LEN = 32
CMP_STRIDE = 16
SEL_LEN = 64
SEL_TOPN = 16
WIN = 512
FORCE_BONUS = 1e3
DIL_CFG = ((128, 1), (512, 4), (2048, 16))
N_DIL = 3
DIL_HEADS = 4
SB_W = SB_HEADS * HEAD_DIM
MLA_OUT = MLA_HEADS * MLA_V
NSA_W = NSA_HEADS * HEAD_DIM
NSA_KV_W = NSA_KV_HEADS * HEAD_DIM
DIL_W = DIL_HEADS * HEAD_DIM

LANES = 128
NEG_MASK = -1e30
LOG2E = 1.4426950408889634
LN2 = 0.6931471805599453
F32_ZERO_EXP = 150.0
SAFE_LOGIT_BOUND = 50.0
ROUNDING_MARGIN = 1.02
SEL_OFF = 2.0 ** 30
VMEM_LIMIT = 56 * 1024 * 1024

TS_PROJ = 512
TQ_SB = 1024
TK_SB = 256
TQ_MLA = 1024
TD_MLA = 512
TQ_CMP = 512
TQ_NSA = 512
TK_NSA = 512
TK_WIN = 256
SEL_TILE_STRIDE = LANES * SEL_LEN // TK_NSA
T_DIL = 256


def _dot(a, b):
    return jnp.dot(a, b, preferred_element_type=F32)


def _dot_nt(a, b):
    return lax.dot_general(a, b, (((1,), (1,)), ((), ())), preferred_element_type=F32)


def _split_hl(a):
    hi = a.astype(BF16)
    lo = (a - hi.astype(F32)).astype(BF16)
    return hi, lo


def _dot_hl(a, b):
    hi, lo = _split_hl(a)
    return _dot(hi, b) + _dot(lo, b)


def _sigmoid(z):
    return 1.0 / (1.0 + jnp.exp(-z))


def _cparams(n_axes):
    return pltpu.CompilerParams(dimension_semantics=("arbitrary",) * n_axes,
                                vmem_limit_bytes=VMEM_LIMIT)


def _full(shape):
    n = len(shape)
    return pl.BlockSpec(shape, lambda *a, _n=n: (0,) * _n)


def _mod_kernel(c_ref, w_ref, b_ref, o_ref):
    c = c_ref[...]
    a = c * _sigmoid(c)
    ah, al = _split_hl(a)
    wh, wl = _split_hl(w_ref[0])
    o_ref[0] = _dot(ah, wh) + _dot(ah, wl) + _dot(al, wh) + b_ref[0]


def _modulation(c8, ada_w, ada_b):
    depth, d, n3 = ada_w.shape
    tn = 1024
    return pl.pallas_call(
        _mod_kernel,
        grid=(depth, n3 // tn),
        in_specs=[pl.BlockSpec((8, d), lambda l, j: (0, 0)),
                  pl.BlockSpec((1, d, tn), lambda l, j: (l, 0, j)),
                  pl.BlockSpec((1, 1, tn), lambda l, j: (l, 0, j))],
        out_specs=pl.BlockSpec((1, 8, tn), lambda l, j: (l, 0, j)),
        out_shape=jax.ShapeDtypeStruct((depth, 8, n3), F32),
        compiler_params=_cparams(2),
        name="adaln_mod",
    )(c8, ada_w, ada_b.reshape(depth, 1, n3))


def _modulated(x_ref, mod_ref, ng):
    x = x_ref[0]
    mod = mod_ref[0]
    shift = mod[:, 0:D_MODEL]
    scale = mod[:, D_MODEL:2 * D_MODEL]
    ms = jnp.mean(x * x, axis=-1, keepdims=True)
    h = x * lax.rsqrt(ms + NORM_EPS) * ng
    h = h * (1.0 + scale) + shift
    return h.astype(BF16)


def _group_sumsq(t, g):
    return _dot((t * t).astype(BF16), g)


def _group_norm_chunk(t, g, inv_cnt):
    return t * lax.rsqrt(_group_sumsq(t, g) * inv_cnt + NORM_EPS)


def _row_rms(t, gain):
    ms = jnp.mean(t * t, axis=-1, keepdims=True)
    return t * lax.rsqrt(ms + NORM_EPS) * gain


EV_COLS = 3200


def _k1_even_kernel(x_ref, mod_ref, vec_ref, w_ref, wq_ref, wqs_ref, wkv_ref, g64_ref, gm_ref,
                    cos_ref, sin_ref, sbq_o, sbk_o, sbv_o, gz_o, mq_o, mk_o, mv_o):
    hb = _modulated(x_ref, mod_ref, vec_ref[0:1, :])
    g64 = g64_ref[...]
    gm = gm_ref[...]
    inv64 = 1.0 / HEAD_DIM

    u = _dot(hb, w_ref[:, 0:1536])
    for c in range(2):
        sl = slice(c * 256, (c + 1) * 256)
        qn = _group_norm_chunk(u[:, c * 256:(c + 1) * 256], g64, inv64)
        sbq_o[0, :, sl] = (qn * vec_ref[1:2, sl]).astype(BF16)
        kn = _group_norm_chunk(u[:, 512 + c * 256:512 + (c + 1) * 256], g64, inv64)
        sbk_o[0, :, sl] = (kn * vec_ref[2:3, sl]).astype(BF16)
    sbv_o[0] = u[:, 1024:1536].astype(BF16)

    uz = _dot(hb, w_ref[:, 1536:2560])
    gz_o[0] = (uz * _sigmoid(uz)).astype(BF16)

    ul = _dot(hb, w_ref[:, 2560:3200])
    qlat = _row_rms(ul[:, 0:256], vec_ref[3:4, 0:256]).astype(BF16)
    kvlat = _row_rms(ul[:, 256:384], vec_ref[4:5, 0:128]).astype(BF16)
    krm = ul[:, 384:512]
    krs = ul[:, 512:640]
    cs = cos_ref[...]
    sn = sin_ref[...]
    cs2 = jnp.concatenate([cs, cs], axis=1)
    sn2 = jnp.concatenate([sn, sn], axis=1)

    tq = _dot(qlat, wq_ref[...])
    tqs = _dot(qlat, wqs_ref[...])
    for p in range(4):
        sl = slice(p * 256, (p + 1) * 256)
        tc = tq[:, p * 256:(p + 1) * 256]
        ss = _group_sumsq(tc, gm)
        inv = lax.rsqrt(ss * vec_ref[10:11, sl] + NORM_EPS)
        a = tc * inv * vec_ref[5:6, sl]
        b = tqs[:, p * 256:(p + 1) * 256] * inv * vec_ref[6:7, sl]
        mq_o[0, :, sl] = (a * cs2 + b * sn2).astype(BF16)

    sskr = _group_sumsq(krm, gm[0:128, 0:128])
    invr = lax.rsqrt(sskr * vec_ref[10:11, 0:128] + NORM_EPS)
    kr = (krm * invr * vec_ref[8:9, 0:128]) * cs + (krs * invr * vec_ref[9:10, 0:128]) * sn
    kr2 = jnp.concatenate([kr, kr], axis=1)

    kv = _dot(kvlat, wkv_ref[...])
    for p in range(4):
        sl = slice(p * 256, (p + 1) * 256)
        kn = _group_norm_chunk(kv[:, p * 256:(p + 1) * 256], g64, inv64)
        mk_o[0, :, sl] = (kn * vec_ref[7:8, sl] + kr2).astype(BF16)
    mv_o[0] = kv[:, 1024:1536].astype(BF16)


def _k1_even(x, mod3, vec, w, wq, wqs, wkv, g64, gm, cos128, sin128):
    b, s, d = x.shape
    ts = min(TS_PROJ, s)
    row = lambda n: pl.BlockSpec((1, ts, n), lambda i, j: (i, j, 0))
    outs = [(512, BF16), (512, BF16), (512, BF16), (1024, BF16), (1024, BF16), (1024, BF16), (512, BF16)]
    return pl.pallas_call(
        _k1_even_kernel,
        grid=(b, s // ts),
        in_specs=[row(d),
                  pl.BlockSpec((1, 1, 3 * d), lambda i, j: (i, 0, 0)),
                  _full(vec.shape), _full(w.shape), _full(wq.shape), _full(wqs.shape), _full(wkv.shape),
                  _full(g64.shape), _full(gm.shape),
                  pl.BlockSpec((ts, 128), lambda i, j: (j, 0)),
                  pl.BlockSpec((ts, 128), lambda i, j: (j, 0))],
        out_specs=[row(n) for n, _ in outs],
        out_shape=[jax.ShapeDtypeStruct((b, s, n), dt) for n, dt in outs],
        compiler_params=_cparams(2),
        name="inproj_even",
    )(x, mod3, vec, w, wq, wqs, wkv, g64, gm, cos128, sin128)


def _sb_kernel(bnd_ref, q_ref, k_ref, v_ref, tri_ref, o_ref, acc_ref, car_ref, *, tq, tk, bounded):
    qi = pl.program_id(2)
    q = q_ref[0]
    lane = lax.broadcasted_iota(jnp.int32, (1, LANES), 1)
    lo_half = lane < HEAD_DIM
    zero = jnp.zeros_like(q)
    qs = (jnp.where(lo_half, q, zero), jnp.where(lo_half, zero, q))
    tri = tri_ref[...]
    acc_ref[...] = jnp.zeros_like(acc_ref)
    car_ref[...] = jnp.zeros_like(car_ref)
    col0 = lax.broadcasted_iota(jnp.int32, (1, tk), 1)

    def tile(kt, r0, masked):
        n = tq - r0
        off = pl.multiple_of(kt * tk, tk)
        k = k_ref[0, pl.ds(off, tk), :]
        v = v_ref[0, pl.ds(off, tk), :]
        if masked:
            rows = qi * tq + r0 + lax.broadcasted_iota(jnp.int32, (n, 1), 0)
            strict = (kt * tk + col0) < rows
        for i in range(2):
            z = _dot_nt(qs[i][r0:tq], k)
            if bounded:
                lom = jnp.log(1.0 + jnp.exp2(z)) * (-LOG2E)
            else:
                lom = -jnp.maximum(z, 0.0) - jnp.log(1.0 + jnp.exp2(-jnp.abs(z))) * LOG2E
            if masked:
                lom = jnp.where(strict, lom, 0.0)
            tt = _dot(lom.astype(BF16), tri)
            car = car_ref[i, r0:tq]
            w = jnp.exp2(tt + z + jnp.concatenate([car] * (tk // LANES), axis=1))
            if masked:
                w = jnp.where(strict, w, 0.0)
            acc_ref[i, r0:tq] += _dot(w.astype(BF16), v)
            car_ref[i, r0:tq] = car + jnp.broadcast_to(tt[:, 0:1], (n, LANES))

    per = tq // tk
    for c in reversed(range(per)):
        tile(qi * per + c, c * tk, True)

    if bounded:
        limit = -(F32_ZERO_EXP + bnd_ref[0])

        def still_live():
            return jnp.max(jnp.maximum(car_ref[0], car_ref[1])) > limit

        def single(c):
            j, _ = c
            tile(qi * per - 1 - j, 0, False)
            return j + 1, still_live()

        _, live = lax.while_loop(lambda c: (c[0] < per) & c[1], single, (0, qi > 0))

        def group(c):
            g, _ = c
            for u in range(per):
                tile(qi * per - 1 - per * (g + 1) - u, 0, False)
            return g + 1, still_live()

        lax.while_loop(lambda c: (c[0] < qi - 1) & c[1], group, (0, live))
    else:
        def body(j, carry):
            for u in range(per):
                tile(qi * per - 1 - per * j - u, 0, False)
            return carry

        lax.fori_loop(0, qi, body, 0)
    o_ref[0] = jnp.where(lo_half, acc_ref[0], acc_ref[1])


def _sb_attention(q, k, v, tri, bound, safe):
    b, s, w = q.shape
    tq = min(TQ_SB, s)
    tk = min(TK_SB, tq)

    def call(bounded):
        kern = functools.partial(_sb_kernel, tq=tq, tk=tk, bounded=bounded)
        return pl.pallas_call(
            kern,
            grid=(b, w // LANES, s // tq),
            in_specs=[pl.BlockSpec(memory_space=pltpu.SMEM),
                      pl.BlockSpec((1, tq, LANES), lambda i, p, j: (i, j, p)),
                      pl.BlockSpec((1, s, LANES), lambda i, p, j: (i, 0, p)),
                      pl.BlockSpec((1, s, LANES), lambda i, p, j: (i, 0, p)),
                      _full(tri.shape)],
            out_specs=pl.BlockSpec((1, tq, LANES), lambda i, p, j: (i, j, p)),
            out_shape=jax.ShapeDtypeStruct((b, s, w), F32),
            scratch_shapes=[pltpu.VMEM((2, tq, LANES), F32), pltpu.VMEM((2, tq, LANES), F32)],
            compiler_params=_cparams(3),
            name="stickbreak_attn_bounded" if bounded else "stickbreak_attn",
        )(bound, q, k, v, tri)

    return lax.cond(safe, lambda: call(True), lambda: call(False))


def _lane_max(s):
    m = s[:, 0:LANES]
    for c in range(1, s.shape[1] // LANES):
        m = jnp.maximum(m, s[:, c * LANES:(c + 1) * LANES])
    return m


def _mla_kernel(bnd_ref, q_ref, k_ref, v_ref, o_ref, acc_ref, m_ref, *, tq, td, bounded):
    qi = pl.program_id(2)
    q = q_ref[0]
    qs = (q[:, 0:LANES], q[:, LANES:2 * LANES])
    lane = lax.broadcasted_iota(jnp.int32, (1, LANES), 1)
    lo_half = lane < HEAD_DIM
    acc_ref[...] = jnp.zeros_like(acc_ref)
    if bounded:
        m_ref[...] = jnp.full(m_ref.shape, bnd_ref[0], F32)
    else:
        m_ref[...] = jnp.full_like(m_ref, NEG_MASK)

    def scores(off, width, r0, masked):
        k = k_ref[0, pl.ds(off, width), :]
        if masked:
            rows = qi * tq + r0 + lax.broadcasted_iota(jnp.int32, (tq - r0, 1), 0)
            keep = (off + lax.broadcasted_iota(jnp.int32, (1, width), 1)) <= rows
        out = []
        for i in range(2):
            s = _dot_nt(qs[i][r0:tq], k[:, i * LANES:(i + 1) * LANES])
            if masked:
                s = jnp.where(keep, s, NEG_MASK)
            out.append(s)
        return out

    def sweep(fn):
        def body(j, carry):
            fn(pl.multiple_of(j * tq, tq), tq, 0, False)
            return carry
        lax.fori_loop(0, qi, body, 0)
        for c in range(tq // td):
            fn(pl.multiple_of(qi * tq + c * td, td), td, c * td, True)

    def row_max(off, width, r0, masked):
        for i, s in enumerate(scores(off, width, r0, masked)):
            m_ref[i, r0:tq] = jnp.maximum(m_ref[i, r0:tq], _lane_max(s))

    def accumulate(off, width, r0, masked):
        vx = jnp.concatenate([v_ref[0, pl.ds(off, width), :], jnp.ones((width, LANES), BF16)], axis=1)
        for i, s in enumerate(scores(off, width, r0, masked)):
            p = jnp.exp2(s - jnp.concatenate([m_ref[i, r0:tq]] * (width // LANES), axis=1))
            acc_ref[i, r0:tq] += _dot(p.astype(BF16), vx)

    if not bounded:
        sweep(row_max)
        for i in range(2):
            m_ref[i] = jnp.broadcast_to(jnp.max(m_ref[i], axis=-1, keepdims=True), (tq, LANES))
    sweep(accumulate)
    o_ref[0] = jnp.where(lo_half, acc_ref[0, :, 0:LANES] / acc_ref[0, :, LANES:2 * LANES],
                         acc_ref[1, :, 0:LANES] / acc_ref[1, :, LANES:2 * LANES])


def _mla_attention(q, k, v, bound):
    b, s, _ = q.shape
    tq = min(TQ_MLA, s)
    td = min(TD_MLA, tq)

    def call(bounded):
        kern = functools.partial(_mla_kernel, tq=tq, td=td, bounded=bounded)
        return pl.pallas_call(
            kern,
            grid=(b, MLA_HEADS // 2, s // tq),
            in_specs=[pl.BlockSpec(memory_space=pltpu.SMEM),
                      pl.BlockSpec((1, tq, 2 * LANES), lambda i, p, j: (i, j, p)),
                      pl.BlockSpec((1, s, 2 * LANES), lambda i, p, j: (i, 0, p)),
                      pl.BlockSpec((1, s, LANES), lambda i, p, j: (i, 0, p))],
            out_specs=pl.BlockSpec((1, tq, LANES), lambda i, p, j: (i, j, p)),
            out_shape=jax.ShapeDtypeStruct((b, s, MLA_OUT), F32),
            scratch_shapes=[pltpu.VMEM((2, tq, 2 * LANES), F32), pltpu.VMEM((2, tq, LANES), F32)],
            compiler_params=_cparams(3),
            name="mla_attn_bounded" if bounded else "mla_attn",
        )(bound, q, k, v)

    return lax.cond(bound[0] <= SAFE_LOGIT_BOUND, lambda: call(True), lambda: call(False))


def _out_even_kernel(x_ref, mod_ref, osb_ref, omla_ref, gz_ref, w_ref, o_ref):
    gz = gz_ref[0].astype(F32)
    m1 = (osb_ref[0] * gz[:, 0:SB_W]).astype(BF16)
    m2 = (omla_ref[0] * gz[:, SB_W:SB_W + MLA_OUT]).astype(BF16)
    y = _dot(m1, w_ref[0:SB_W, :]) + _dot(m2, w_ref[SB_W:SB_W + MLA_OUT, :])
    gate = mod_ref[0][:, 2 * D_MODEL:3 * D_MODEL]
    o_ref[0] = x_ref[0] + gate * y


def _out_even(x, mod3, osb, omla, gz, w):
    b, s, d = x.shape
    ts = min(TS_PROJ, s)
    row = lambda n: pl.BlockSpec((1, ts, n), lambda i, j: (i, j, 0))
    return pl.pallas_call(
        _out_even_kernel,
        grid=(b, s // ts),
        in_specs=[row(d), pl.BlockSpec((1, 1, 3 * d), lambda i, j: (i, 0, 0)),
                  row(SB_W), row(MLA_OUT), row(SB_W + MLA_OUT), _full(w.shape)],
        out_specs=row(d),
        out_shape=jax.ShapeDtypeStruct((b, s, d), F32),
        compiler_params=_cparams(2),
        name="outproj_even",
    )(x, mod3, osb, omla, gz, w)


NSA_COLS = 5248


def _k1_nsa_kernel(x_ref, mod_ref, vec_ref, w_ref, g64_ref,
                   q_o, ck_o, cv_o, sk_o, sv_o, wk_o, wv_o, gt_o, gz_o, chunk_ref):
    ts = x_ref.shape[1]
    hb = _modulated(x_ref, mod_ref, vec_ref[0:1, 0:D_MODEL])
    g64 = g64_ref[...]
    inv64 = 1.0 / HEAD_DIM

    uq = _dot(hb, w_ref[:, 0:1536])
    for c in range(6):
        sl = slice(c * 256, (c + 1) * 256)
        qn = _group_norm_chunk(uq[:, c * 256:(c + 1) * 256], g64, inv64)
        q_o[0, :, sl] = (qn * vec_ref[1:2, sl]).astype(BF16)

    uc = _dot(hb, w_ref[:, 1536:2304])
    for c in range(2 * NSA_KV_HEADS):
        chunk_ref[c] = uc[:, c * 128:(c + 1) * 128]
    for g in range(NSA_KV_HEADS):
        for tok in range(CMP_STRIDE):
            rows = pl.ds(tok, ts // CMP_STRIDE, stride=CMP_STRIDE)
            sl = slice(tok * 128, (tok + 1) * 128)
            ck_o[0, g, :, sl] = chunk_ref[g, rows, :].astype(BF16)
            cv_o[0, g, :, sl] = chunk_ref[NSA_KV_HEADS + g, rows, :].astype(BF16)

    us = _dot(hb, w_ref[:, 2304:3840])
    kgain = vec_ref[2:3, 0:384]
    g128 = g64[0:128, 0:128]
    for g in range(NSA_KV_HEADS):
        sl = slice(g * 128, (g + 1) * 128)
        t = us[:, g * 128:(g + 1) * 128]
        ss = _group_sumsq(t, g128)
        sk_o[0, :, sl] = (t * lax.rsqrt(ss * inv64 + NORM_EPS) * kgain[:, sl]).astype(BF16)
        t = us[:, 768 + g * 128:768 + (g + 1) * 128]
        ss = _group_sumsq(t, g128)
        wk_o[0, :, sl] = (t * lax.rsqrt(ss * inv64 + NORM_EPS) * kgain[:, sl]).astype(BF16)
    ones_hi = (lax.broadcasted_iota(jnp.int32, (1, 384), 1) % 128) >= HEAD_DIM
    sv_o[0] = jnp.where(ones_hi, 1.0, us[:, 384:768]).astype(BF16)
    wv_o[0] = jnp.where(ones_hi, 1.0, us[:, 1152:1536]).astype(BF16)

    ug = _dot(hb, w_ref[:, 3840:4224])
    gt_o[0] = _sigmoid(ug)
    uz = _dot(hb, w_ref[:, 4224:5248])
    gz_o[0] = (uz * _sigmoid(uz)).astype(BF16)


def _k1_nsa(x, mod3, vec, w, g64):
    b, s, d = x.shape
    ts = min(TS_PROJ, s)
    row = lambda n: pl.BlockSpec((1, ts, n), lambda i, j: (i, j, 0))
    cw = CMP_STRIDE * 128
    grp = pl.BlockSpec((1, NSA_KV_HEADS, ts // CMP_STRIDE, cw), lambda i, j: (i, 0, j, 0))
    return pl.pallas_call(
        _k1_nsa_kernel,
        grid=(b, s // ts),
        in_specs=[row(d), pl.BlockSpec((1, 1, 3 * d), lambda i, j: (i, 0, 0)),
                  _full(vec.shape), _full(w.shape), _full(g64.shape)],
        out_specs=[row(1536), grp, grp, row(384), row(384), row(384), row(384), row(384), row(1024)],
        out_shape=[jax.ShapeDtypeStruct((b, s, 1536), BF16),
                   jax.ShapeDtypeStruct((b, NSA_KV_HEADS, s // CMP_STRIDE, cw), BF16),
                   jax.ShapeDtypeStruct((b, NSA_KV_HEADS, s // CMP_STRIDE, cw), BF16),
                   jax.ShapeDtypeStruct((b, s, 384), BF16),
                   jax.ShapeDtypeStruct((b, s, 384), BF16),
                   jax.ShapeDtypeStruct((b, s, 384), BF16),
                   jax.ShapeDtypeStruct((b, s, 384), BF16),
                   jax.ShapeDtypeStruct((b, s, 384), F32),
                   jax.ShapeDtypeStruct((b, s, 1024), BF16)],
        scratch_shapes=[pltpu.VMEM((2 * NSA_KV_HEADS, ts, 128), F32)],
        compiler_params=_cparams(2),
        name="inproj_nsa",
    )(x, mod3, vec, w, g64)


def _k1_dil_kernel(x_ref, mod_ref, vec_ref, w_ref, g64_ref, *refs):
    outs, st = refs[:3 * N_DIL], refs[3 * N_DIL]
    ts = x_ref.shape[1]
    hb = _modulated(x_ref, mod_ref, vec_ref[0:1, 0:D_MODEL])
    g64 = g64_ref[...]
    inv64 = 1.0 / HEAD_DIM
    u = _dot(hb, w_ref[...])
    for g in range(N_DIL):
        dil = DIL_CFG[g][1]
        qn = _group_norm_chunk(u[:, g * 256:(g + 1) * 256], g64, inv64) * vec_ref[3:4, 0:256]
        kn = _group_norm_chunk(u[:, 768 + g * 256:768 + (g + 1) * 256], g64, inv64) * vec_ref[4:5, 0:256]
        vals = (qn, kn, u[:, 1536 + g * 256:1536 + (g + 1) * 256])
        for j, val in enumerate(vals):
            o_ref = outs[3 * g + j]
            if dil == 1:
                o_ref[0] = val.astype(BF16)
                continue
            for h in range(2):
                st[j, h] = val[:, h * 128:(h + 1) * 128]
            for r in range(dil):
                for h in range(2):
                    sl = slice(r * 256 + h * 128, r * 256 + (h + 1) * 128)
                    o_ref[0, :, sl] = st[j, h, pl.ds(r, ts // dil, stride=dil), :].astype(BF16)


def _k1_dil(x, mod3, vec, w, g64):
    b, s, d = x.shape
    ts = min(TS_PROJ, s)
    row = lambda n: pl.BlockSpec((1, ts, n), lambda i, j: (i, j, 0))
    specs, shapes = [], []
    for g in range(N_DIL):
        dil = DIL_CFG[g][1]
        for _ in range(3):
            specs.append(pl.BlockSpec((1, ts // dil, dil * 256), lambda i, j: (i, j, 0)))
            shapes.append(jax.ShapeDtypeStruct((b, s // dil, dil * 256), BF16))
    return pl.pallas_call(
        _k1_dil_kernel,
        grid=(b, s // ts),
        in_specs=[row(d), pl.BlockSpec((1, 1, 3 * d), lambda i, j: (i, 0, 0)),
                  _full(vec.shape), _full(w.shape), _full(g64.shape)],
        out_specs=specs,
        out_shape=shapes,
        scratch_shapes=[pltpu.VMEM((3, 2, ts, 128), F32)],
        compiler_params=_cparams(2),
        name="inproj_dil",
    )(x, mod3, vec, w, g64)


def _compress_kernel(xk_ref, xv_ref, wk_ref, wv_ref, pek_ref, pev_ref, gain_ref, g128_ref, kc_o, vc_o):
    nch = xk_ref.shape[2]

    def comp(x_ref, w_ref, pe_ref):
        x = x_ref[0, 0]
        lo = _dot(x, w_ref[0])
        hi = _dot(x, w_ref[1])
        pec = _dot_hl(pe_ref[0], w_ref[0]) + _dot_hl(pe_ref[1], w_ref[1])
        return lo + pltpu.roll(hi, nch - 1, 0) + pec[0:1, :]

    kc = comp(xk_ref, wk_ref, pek_ref)
    ss = _dot_hl(kc * kc, g128_ref[...])
    kc_o[0, 0] = (kc * lax.rsqrt(ss * (1.0 / HEAD_DIM) + NORM_EPS) * gain_ref[...]).astype(BF16)
    vc_o[0, 0] = comp(xv_ref, wv_ref, pev_ref).T.astype(BF16)


def _compress(ck, cv, wk, wv, pek, pev, gain, g128):
    b, g, nch, _ = ck.shape
    xk, xv = ck, cv
    blk = pl.BlockSpec((1, 1, nch, CMP_STRIDE * 128), lambda i, j: (i, j, 0, 0))
    oblk = pl.BlockSpec((1, 1, nch, 128), lambda i, j: (i, j, 0, 0))
    tblk = pl.BlockSpec((1, 1, 128, nch), lambda i, j: (i, j, 0, 0))
    return pl.pallas_call(
        _compress_kernel,
        grid=(b, g),
        in_specs=[blk, blk, _full(wk.shape), _full(wv.shape), _full(pek.shape), _full(pev.shape),
                  _full(gain.shape), _full(g128.shape)],
        out_specs=[oblk, tblk],
        out_shape=[jax.ShapeDtypeStruct((b, g, nch, 128), BF16), jax.ShapeDtypeStruct((b, g, 128, nch), BF16)],
        compiler_params=_cparams(2),
        name="nsa_compress",
    )(xk, xv, wk, wv, pek, pev, gain, g128)


def _pair_select(lo_half, a, b):
    return jnp.where(lo_half, a, b)


def _nsa_cmp_kernel(slope_ref, q_ref, kc_ref, vct_ref, ovlt_ref, cposc_ref, prow_ref, gt_ref,
                    oc_o, selb_o, any_o, *, tq):
    g = pl.program_id(1)
    qi = pl.program_id(2)
    q4 = q_ref[0]
    kc = kc_ref[0, 0]
    vct = vct_ref[0, 0]
    nch = kc.shape[0]
    t = qi * tq + lax.broadcasted_iota(jnp.int32, (1, tq), 1)
    n_id = lax.broadcasted_iota(jnp.int32, (nch, 1), 0)
    valid = (n_id * CMP_STRIDE + (CMP_LEN - 1)) <= t
    rel = (cposc_ref[...] - prow_ref[qi][:, 0:1]) * LOG2E
    gtt = gt_ref[0].T
    row_lo = lax.broadcasted_iota(jnp.int32, (LANES, 1), 0) < HEAD_DIM
    psum = jnp.zeros((nch, tq), F32)
    outs = []
    for hh in range(NSA_HPG):
        s = _dot_nt(kc, q4[:, hh * LANES:(hh + 1) * LANES]) + slope_ref[g * NSA_HPG + hh] * rel
        s = jnp.where(valid, s, -jnp.inf)
        mx = jnp.max(s, axis=0, keepdims=True)
        mx = jnp.where(mx == -jnp.inf, 0.0, mx)
        e = jnp.exp2(s - mx)
        den = jnp.maximum(jnp.sum(e, axis=0, keepdims=True), TINY)
        p = e * (1.0 / den)
        psum = psum + p
        outs.append(_dot(vct, p.astype(BF16)) * gtt[3 * hh:3 * hh + 1, :])
    oc_o[0, :, 0:LANES] = jnp.where(row_lo, outs[0], outs[1]).T
    oc_o[0, :, LANES:2 * LANES] = jnp.where(row_lo, outs[2], outs[3]).T

    hi, lo = _split_hl(psum)
    imp = _dot(ovlt_ref[...], hi) + _dot(ovlt_ref[...], lo)
    blk = lax.broadcasted_iota(jnp.int32, (LANES, 1), 0)
    cur = t >> 6
    forced = (blk == 0) | (blk == cur) | (blk == cur - 1)
    allowed = blk <= cur
    assert NSA_HPG < FORCE_BONUS
    score = jnp.where(allowed & ~forced, imp, -jnp.inf)
    blk_f = blk.astype(F32)

    def pick(_, sc):
        mx = jnp.max(sc, axis=0, keepdims=True)
        idx = jnp.min(jnp.where(sc == mx, blk_f, float(LANES)), axis=0, keepdims=True)
        return jnp.where(blk_f == idx, -jnp.inf, sc)

    left = lax.fori_loop(0, SEL_TOPN - 3, pick, score)
    chosen = jnp.where((left == -jnp.inf) & allowed, 1.0, 0.0).T
    selb_o[0, 0] = ((chosen - 1.0) * SEL_OFF).astype(BF16)
    used = jnp.max(chosen, axis=0, keepdims=True)
    any_o[0, 0, 0] = jnp.broadcast_to(used, (8, LANES))


def _nsa_cmp(slopes, q, kc, vct, ovlt, cposc, pos_rows, gates):
    b, s, _ = q.shape
    nch = kc.shape[2]
    tq = pos_rows.shape[2]
    kern = functools.partial(_nsa_cmp_kernel, tq=tq)
    return pl.pallas_call(
        kern,
        grid=(b, NSA_KV_HEADS, s // tq),
        in_specs=[pl.BlockSpec(memory_space=pltpu.SMEM),
                  pl.BlockSpec((1, tq, 4 * LANES), lambda i, g, j: (i, j, g)),
                  pl.BlockSpec((1, 1, nch, LANES), lambda i, g, j: (i, g, 0, 0)),
                  pl.BlockSpec((1, 1, LANES, nch), lambda i, g, j: (i, g, 0, 0)),
                  _full(ovlt.shape), _full(cposc.shape), _full(pos_rows.shape),
                  pl.BlockSpec((1, tq, LANES), lambda i, g, j: (i, j, g))],
        out_specs=[pl.BlockSpec((1, tq, 2 * LANES), lambda i, g, j: (i, j, g)),
                   pl.BlockSpec((1, 1, tq, LANES), lambda i, g, j: (i, g, j, 0)),
                   pl.BlockSpec((1, 1, 1, 8, LANES), lambda i, g, j: (i, g, j, 0, 0))],
        out_shape=[jax.ShapeDtypeStruct((b, s, NSA_W), F32),
                   jax.ShapeDtypeStruct((b, NSA_KV_HEADS, s, LANES), BF16),
                   jax.ShapeDtypeStruct((b, NSA_KV_HEADS, s // tq, 8, LANES), F32)],
        compiler_params=_cparams(3),
        name="nsa_cmp_topk",
    )(slopes, q, kc, vct, ovlt, cposc, pos_rows, gates)


def _gqa_kernel(flag_ref, slope_ref, bnd_ref, q_ref, selb_ref, k_ref, v_ref, pos_ref, pcol_ref, gt_ref, o_ref,
                qa_ref, acc_ref, m_ref, *, tq, tk, branch, bounded):
    bi = pl.program_id(0)
    g = pl.program_id(1)
    qi = pl.program_id(2)
    q4 = q_ref[0]
    lane = lax.broadcasted_iota(jnp.int32, (1, LANES), 1)
    lo_half = lane < HEAD_DIM
    pref = pos_ref[qi * (tq // tk)][:, 0:1]
    for hh in range(NSA_HPG):
        sl = slice(hh * tq, (hh + 1) * tq)
        qa_ref[sl, 0:LANES] = q4[:, hh * LANES:(hh + 1) * LANES]
        if branch == 1:
            qa_ref[sl, LANES:2 * LANES] = selb_ref[0, 0]
        if bounded:
            own = (pcol_ref[...] - pref) * (LOG2E * slope_ref[g * NSA_HPG + hh]) + bnd_ref[0]
            m_ref[sl] = jnp.broadcast_to(own, (tq, LANES))
    acc_ref[...] = jnp.zeros_like(acc_ref)
    if not bounded:
        m_ref[...] = jnp.full_like(m_ref, NEG_MASK)
    rows = qi * tq + lax.broadcasted_iota(jnp.int32, (tq, 1), 0)
    col0 = lax.broadcasted_iota(jnp.int32, (1, tk), 1)
    krow = lax.broadcasted_iota(jnp.int32, (tk, 1), 0)
    per = tq // tk
    nrep = tk // LANES
    flag0 = ((bi * NSA_KV_HEADS + g) * pl.num_programs(2) + qi) * SEL_TILE_STRIDE

    def scores(kt_true, masked, r_lo, r_hi):
        kt = jnp.maximum(kt_true, 0)
        off = pl.multiple_of(kt * tk, tk)
        k = k_ref[0, pl.ds(off, tk), :]
        if branch == 1:
            blk = (kt * tk + krow) >> 6
            onehot = jnp.where(lane == blk, 1.0, 0.0).astype(BF16)
            s_all = _dot_nt(qa_ref[...], jnp.concatenate([k, onehot], axis=1))
            parts = [s_all[hh * tq:(hh + 1) * tq] for hh in range(NSA_HPG)]
        else:
            parts = [_dot_nt(qa_ref[hh * tq + r_lo:hh * tq + r_hi, 0:LANES], k) for hh in range(NSA_HPG)]
        rel = (pos_ref[kt] - pref) * LOG2E
        if masked:
            cols = kt_true * tk + col0
            d = rows[r_lo:r_hi] - cols
            keep = d >= 0
            if branch == 2:
                keep = keep & (d < WIN) & (cols >= 0)
        out = []
        for hh in range(NSA_HPG):
            s = parts[hh] + slope_ref[g * NSA_HPG + hh] * rel
            if masked:
                s = jnp.where(keep, s, NEG_MASK)
            out.append(s)
        return out

    def sweep(fn):
        if branch == 1:
            def body(j, carry):
                @pl.when(flag_ref[flag0 + j] != 0)
                def _():
                    fn(j, False, 0, tq)
                return carry
            lax.fori_loop(0, qi * per, body, 0)
            for dd in range(per):
                fn(qi * per + dd, True, 0, tq)
        else:
            for c in range(-(WIN // tk), per):
                r_lo = max(0, c * tk)
                r_hi = min(tq, -(-(c * tk + tk + WIN - 1) // 8) * 8)
                fn(qi * per + c, True, r_lo, r_hi)

    def row_max(kt_true, masked, r_lo, r_hi):
        for hh, s in enumerate(scores(kt_true, masked, r_lo, r_hi)):
            sl = slice(hh * tq + r_lo, hh * tq + r_hi)
            m_ref[sl] = jnp.maximum(m_ref[sl], _lane_max(s))

    def accumulate(kt_true, masked, r_lo, r_hi):
        off = pl.multiple_of(jnp.maximum(kt_true, 0) * tk, tk)
        v = v_ref[0, pl.ds(off, tk), :]
        for hh, s in enumerate(scores(kt_true, masked, r_lo, r_hi)):
            sl = slice(hh * tq + r_lo, hh * tq + r_hi)
            p = jnp.exp2(s - jnp.concatenate([m_ref[sl]] * nrep, axis=1))
            acc_ref[sl] += _dot(p.astype(BF16), v)

    if not bounded:
        sweep(row_max)
        for hh in range(NSA_HPG):
            sl = slice(hh * tq, (hh + 1) * tq)
            m_ref[sl] = jnp.broadcast_to(jnp.max(m_ref[sl], axis=-1, keepdims=True), (tq, LANES))
    sweep(accumulate)

    gt = gt_ref[0]
    for pr in range(2):
        res = []
        for x in range(2):
            hh = 2 * pr + x
            a = acc_ref[hh * tq:(hh + 1) * tq]
            r = pltpu.roll(a, HEAD_DIM, 1)
            o = a / r if x == 0 else r / a
            res.append(o * gt[:, 3 * hh + branch:3 * hh + branch + 1])
        o_ref[0, :, pr * LANES:(pr + 1) * LANES] = _pair_select(lo_half, res[0], res[1])


def _nsa_gqa(flags, slopes, bound, safe, q, selb, k, v, pos2d, pos_col, gates, branch):
    b, s, _ = q.shape
    tq = min(TQ_NSA, s)
    tk = pos2d.shape[2]

    def call(bounded):
        kern = functools.partial(_gqa_kernel, tq=tq, tk=tk, branch=branch, bounded=bounded)
        smem = pl.BlockSpec(memory_space=pltpu.SMEM)
        name = ("nsa_sel_attn" if branch == 1 else "nsa_win_attn") + ("_bounded" if bounded else "")
        return pl.pallas_call(
            kern,
            grid=(b, NSA_KV_HEADS, s // tq),
            in_specs=[smem, smem, smem,
                      pl.BlockSpec((1, tq, 4 * LANES), lambda i, g, j: (i, j, g)),
                      pl.BlockSpec((1, 1, tq, LANES), lambda i, g, j: (i, g, j, 0)),
                      pl.BlockSpec((1, s, LANES), lambda i, g, j: (i, 0, g)),
                      pl.BlockSpec((1, s, LANES), lambda i, g, j: (i, 0, g)),
                      _full(pos2d.shape),
                      pl.BlockSpec((tq, 1), lambda i, g, j: (j, 0)),
                      pl.BlockSpec((1, tq, LANES), lambda i, g, j: (i, j, g))],
            out_specs=pl.BlockSpec((1, tq, 2 * LANES), lambda i, g, j: (i, j, g)),
            out_shape=jax.ShapeDtypeStruct((b, s, NSA_W), F32),
            scratch_shapes=[pltpu.VMEM((NSA_HPG * tq, 2 * LANES), BF16),
                            pltpu.VMEM((NSA_HPG * tq, LANES), F32),
                            pltpu.VMEM((NSA_HPG * tq, LANES), F32)],
            compiler_params=_cparams(3),
            name=name,
        )(flags, slopes, bound, q, selb, k, v, pos2d, pos_col, gates)

    return lax.cond(safe, lambda: call(True), lambda: call(False))


def _dil_kernel(slope_ref, bnd_ref, q_ref, k_ref, v_ref, pc_ref, pr_ref, o_ref, lse_ref,
                *, t, span, grp, bounded):
    i = pl.program_id(2)
    q4 = q_ref[0]
    lane = lax.broadcasted_iota(jnp.int32, (1, LANES), 1)
    lo_half = lane < HEAD_DIM
    per = t // LANES
    prev = jnp.maximum(i * per - 1, 0)
    o_prev = pl.multiple_of(prev * LANES, LANES)
    o_cur = pl.multiple_of(i * t, t)
    kk = jnp.concatenate([k_ref[0, pl.ds(o_prev, LANES), :], k_ref[0, pl.ds(o_cur, t), :]], axis=0)
    vv = jnp.concatenate([v_ref[0, pl.ds(o_prev, LANES), :], v_ref[0, pl.ds(o_cur, t), :]], axis=0)
    pk = jnp.concatenate([pr_ref[0, prev]] + [pr_ref[0, i * per + c] for c in range(per)], axis=1)
    dist = (pc_ref[0] - pk) * LOG2E
    rows = i * t + lax.broadcasted_iota(jnp.int32, (t, 1), 0)
    cols = i * t - LANES + lax.broadcasted_iota(jnp.int32, (1, t + LANES), 1)
    d = rows - cols
    valid = (cols >= 0) & (d >= 0) & (d <= span)
    zero = jnp.zeros((t, LANES), BF16)
    ones = jnp.ones((t + LANES, LANES), BF16)
    for p in range(2):
        qp = q4[:, p * LANES:(p + 1) * LANES]
        kp = kk[:, p * LANES:(p + 1) * LANES]
        vx = jnp.concatenate([vv[:, p * LANES:(p + 1) * LANES], ones], axis=1)
        res = []
        for x in range(2):
            qx = jnp.where(lo_half, qp, zero) if x == 0 else jnp.where(lo_half, zero, qp)
            s = _dot_nt(qx, kp) - slope_ref[grp * DIL_HEADS + 2 * p + x] * dist
            s = jnp.where(valid, s, -jnp.inf)
            mx = bnd_ref[0] if bounded else jnp.max(s, axis=-1, keepdims=True)
            acc = _dot(jnp.exp2(s - mx).astype(BF16), vx)
            den = acc[:, LANES:2 * LANES]
            res.append((acc[:, 0:LANES] / den, mx * LN2 + jnp.log(den)))
        sl = slice(p * LANES, (p + 1) * LANES)
        o_ref[0, :, sl] = _pair_select(lo_half, res[0][0], res[1][0])
        lse_ref[0, :, sl] = jnp.where(lo_half, res[0][1], res[1][1])


def _dilated(slopes, bound, safe, qv, kv, vv, pos_f, grp):
    window, dil = DIL_CFG[grp]
    b, sub, wd = qv.shape
    w = wd // dil
    t = min(T_DIL, sub)
    span = window // dil
    assert span <= LANES and t % LANES == 0
    pres = pos_f.reshape(sub, dil).T
    pc = pres.reshape(dil, sub, 1)
    pr = pres.reshape(dil, sub // LANES, 1, LANES)
    shp = jax.ShapeDtypeStruct((b, sub, wd), F32)
    smem = pl.BlockSpec(memory_space=pltpu.SMEM)

    def call(bounded):
        kern = functools.partial(_dil_kernel, t=t, span=span, grp=grp, bounded=bounded)
        return pl.pallas_call(
            kern,
            grid=(b, dil, sub // t),
            in_specs=[smem, smem,
                      pl.BlockSpec((1, t, w), lambda i, r, j: (i, j, r)),
                      pl.BlockSpec((1, sub, w), lambda i, r, j: (i, 0, r)),
                      pl.BlockSpec((1, sub, w), lambda i, r, j: (i, 0, r)),
                      pl.BlockSpec((1, t, 1), lambda i, r, j: (r, j, 0)),
                      pl.BlockSpec((1, sub // LANES, 1, LANES), lambda i, r, j: (r, 0, 0, 0))],
            out_specs=[pl.BlockSpec((1, t, w), lambda i, r, j: (i, j, r)),
                       pl.BlockSpec((1, t, w), lambda i, r, j: (i, j, r))],
            out_shape=[shp, shp],
            compiler_params=_cparams(3),
            name="dilated_attn_g%d%s" % (grp, "_bounded" if bounded else ""),
        )(slopes, bound, qv, kv, vv, pc, pr)

    return lax.cond(safe, lambda: call(True), lambda: call(False))


def _out_odd_kernel(x_ref, mod_ref, oc_ref, os_ref, ow_ref, d0_ref, d1_ref, d2_ref,
                    l0_ref, l1_ref, l2_ref, gz_ref, w_ref, o_ref, st):
    ts = x_ref.shape[1]
    gz = gz_ref[0].astype(F32)
    nsa = oc_ref[0] + os_ref[0] + ow_ref[0]
    m1 = (nsa * gz[:, 0:NSA_W]).astype(BF16)

    def token_order(ref, g, slot):
        dil = DIL_CFG[g][1]
        if dil == 1:
            return ref[0]
        for r in range(dil):
            for h in range(2):
                st[slot, h, pl.ds(r, ts // dil, stride=dil), :] = ref[0, :, r * DIL_W + h * 128:r * DIL_W + (h + 1) * 128]
        return jnp.concatenate([st[slot, 0], st[slot, 1]], axis=1)

    d0, d1, d2 = token_order(d0_ref, 0, 0), token_order(d1_ref, 1, 0), token_order(d2_ref, 2, 1)
    l0, l1, l2 = token_order(l0_ref, 0, 2), token_order(l1_ref, 1, 2), token_order(l2_ref, 2, 3)
    mx = jnp.maximum(jnp.maximum(l0, l1), l2)
    e0, e1, e2 = jnp.exp(l0 - mx), jnp.exp(l1 - mx), jnp.exp(l2 - mx)
    dil = (e0 * d0 + e1 * d1 + e2 * d2) / (e0 + e1 + e2)
    m2 = (dil * gz[:, NSA_W:NSA_W + DIL_W]).astype(BF16)
    y = _dot(m1, w_ref[0:NSA_W, :]) + _dot(m2, w_ref[NSA_W:NSA_W + DIL_W, :])
    gate = mod_ref[0][:, 2 * D_MODEL:3 * D_MODEL]
    o_ref[0] = x_ref[0] + gate * y


def _out_odd(x, mod3, oc, os_, ow, dils, lses, gz, w):
    b, s, d = x.shape
    ts = min(TS_PROJ, s)
    row = lambda n: pl.BlockSpec((1, ts, n), lambda i, j: (i, j, 0))
    res = [pl.BlockSpec((1, ts // DIL_CFG[g][1], DIL_CFG[g][1] * DIL_W), lambda i, j: (i, j, 0))
           for g in range(N_DIL)]
    return pl.pallas_call(
        _out_odd_kernel,
        grid=(b, s // ts),
        in_specs=[row(d), pl.BlockSpec((1, 1, 3 * d), lambda i, j: (i, 0, 0)),
                  row(NSA_W), row(NSA_W), row(NSA_W)] + res + res + [row(1024), _full(w.shape)],
        out_specs=row(d),
        out_shape=jax.ShapeDtypeStruct((b, s, d), F32),
        scratch_shapes=[pltpu.VMEM((4, 2, ts, 128), F32)],
        compiler_params=_cparams(2),
        name="outproj_odd",
    )(x, mod3, oc, os_, ow, *dils, *lses, gz, w)


def _pad_vec(v, n=D_MODEL):
    return jnp.pad(v, (0, n - v.shape[0]))


def _group_matrix(sizes, total):
    m = np.zeros((total, total), np.float32)
    off = 0
    for sz, on in sizes:
        if on:
            m[off:off + sz, off:off + sz] = 1.0
        off += sz
    return jnp.asarray(m, BF16)


def _swap_halves(w):
    h = w.shape[-1] // 2
    return jnp.concatenate([w[..., h:], w[..., :h]], axis=-1)


def _column_blocks(segments):
    blocks, cur, room = [], [], LANES
    for src, width in segments:
        while width:
            n = min(width, room)
            cur.append((src, n))
            src = None if src is None else src + n
            width -= n
            room -= n
            if room == 0:
                blocks.append(cur)
                cur, room = [], LANES
    assert not cur
    return blocks


def _pack_cols_kernel(w_ref, o_ref, *, blocks):
    _, rows, n_src = w_ref.shape
    for bi, pieces in enumerate(blocks):
        parts = []
        for src, width in pieces:
            if src is None:
                parts.append(jnp.zeros((rows, width), F32))
                continue
            a0 = src // LANES * LANES
            a1 = min(-(-(src + width) // LANES) * LANES, n_src)
            parts.append(w_ref[0, :, a0:a1][:, src - a0:src - a0 + width])
        blk = parts[0] if len(parts) == 1 else jnp.concatenate(parts, axis=1)
        o_ref[:, bi * LANES:(bi + 1) * LANES] = blk.astype(BF16)


def _pack_cols(w, layer, segments):
    blocks = _column_blocks(segments)
    _, d, n_src = w.shape
    tr = 256
    return pl.pallas_call(
        functools.partial(_pack_cols_kernel, blocks=blocks),
        grid=(d // tr,),
        in_specs=[pl.BlockSpec((1, tr, n_src), lambda i: (layer, i, 0))],
        out_specs=pl.BlockSpec((tr, len(blocks) * LANES), lambda i: (i, 0)),
        out_shape=jax.ShapeDtypeStruct((d, len(blocks) * LANES), BF16),
        compiler_params=_cparams(1),
        name="pack_weight_columns",
    )(w)


EVEN_SEGMENTS = ((0, 1536), (1536, 512), (2464, 512), (2048, 256), (2304, 128),
                 (None, 64), (2432, 32), (None, 32), (None, 64), (2448, 16), (2432, 16), (None, 32))


def _pack_even(w_in_all, layer, norm_g, sb_qn, sb_kn, qa_g, wq_up, kva_g, wkv_up, qn, kn):
    w = _pack_cols(w_in_all, layer, EVEN_SEGMENTS)
    wq3 = wq_up.reshape(MLA_Q_RANK, MLA_HEADS, MLA_NOPE + MLA_ROPE)
    zq = jnp.zeros((MLA_Q_RANK, MLA_HEADS, 32), wq_up.dtype)
    wq = jnp.concatenate([wq3, zq], axis=-1).reshape(MLA_Q_RANK, MLA_HEADS * 128).astype(BF16)
    wqs = jnp.concatenate([jnp.zeros((MLA_Q_RANK, MLA_HEADS, 64), wq_up.dtype),
                           _swap_halves(wq3[..., MLA_NOPE:]), zq], axis=-1)
    wqs = wqs.reshape(MLA_Q_RANK, MLA_HEADS * 128).astype(BF16)
    wkv3 = wkv_up.reshape(MLA_KV_RANK, MLA_HEADS, MLA_NOPE + MLA_V)
    wk = jnp.concatenate([wkv3[..., :MLA_NOPE], jnp.zeros_like(wkv3[..., :MLA_NOPE])], axis=-1)
    wkv = jnp.concatenate([wk.reshape(MLA_KV_RANK, MLA_HEADS * 128),
                           wkv3[..., MLA_NOPE:].reshape(MLA_KV_RANK, MLA_HEADS * MLA_V)], axis=1).astype(BF16)
    z32 = jnp.zeros((32,), F32)
    z64 = jnp.zeros((64,), F32)
    scale = (MLA_NOPE + MLA_ROPE) ** -0.5 * LOG2E
    qg = jnp.tile(jnp.concatenate([qn, z32]), MLA_HEADS) * scale
    qgs = jnp.tile(jnp.concatenate([z64, _swap_halves(qn[MLA_NOPE:]), z32]), MLA_HEADS) * scale
    kg = jnp.tile(jnp.concatenate([kn[:MLA_NOPE], z64]), MLA_HEADS)
    krg = jnp.concatenate([z64, kn[MLA_NOPE:], z32])
    krgs = jnp.concatenate([z64, _swap_halves(kn[MLA_NOPE:]), z32])
    cnt = jnp.tile(jnp.concatenate([jnp.full((64,), 1.0 / 64), jnp.full((32,), 1.0 / 32), jnp.ones((32,))]),
                   MLA_HEADS).astype(F32)
    rows = [norm_g, _pad_vec(jnp.tile(sb_qn, SB_HEADS) * (LOG2E * HEAD_DIM ** -0.5)), _pad_vec(jnp.tile(sb_kn, SB_HEADS)),
            _pad_vec(qa_g), _pad_vec(kva_g), qg, qgs, kg, _pad_vec(krg), _pad_vec(krgs), cnt]
    rows += [jnp.zeros((D_MODEL,), F32)] * (16 - len(rows))
    return w, wq, wqs, wkv, jnp.stack(rows).astype(F32)


def _padded_heads(start, n):
    return tuple(seg for h in range(n) for seg in ((start + h * HEAD_DIM, HEAD_DIM), (None, HEAD_DIM)))


NSA_SEGMENTS = (_padded_heads(0, NSA_HEADS)
                + tuple(seg for c in range(6) for seg in _padded_heads(768 + c * NSA_KV_W, NSA_KV_HEADS))
                + tuple(seg for g in range(NSA_KV_HEADS)
                        for seg in ((1920 + g * NSA_HPG * 3, NSA_HPG * 3), (None, LANES - NSA_HPG * 3)))
                + ((1956, NSA_W), (5028, DIL_W)))
DIL_SEGMENTS = ((2724, 3 * N_DIL * DIL_W),)


def _pack_odd(w_in_all, layer, norm_g, nsa_qn, nsa_kn, dil_qn, dil_kn):
    w_nsa = _pack_cols(w_in_all, layer, NSA_SEGMENTS)
    w_dil = _pack_cols(w_in_all, layer, DIL_SEGMENTS)
    z64 = jnp.zeros((64,), F32)
    n = NSA_HEADS * 128
    rows = [_pad_vec(norm_g, n),
            jnp.tile(jnp.concatenate([nsa_qn * (LOG2E * HEAD_DIM ** -0.5), z64]), NSA_HEADS),
            _pad_vec(jnp.tile(jnp.concatenate([nsa_kn, z64]), NSA_KV_HEADS), n),
            _pad_vec(jnp.tile(dil_qn, DIL_HEADS) * (LOG2E * HEAD_DIM ** -0.5), n),
            _pad_vec(jnp.tile(dil_kn, DIL_HEADS), n)]
    rows += [jnp.zeros((n,), F32)] * (8 - len(rows))
    return w_nsa, w_dil, jnp.stack(rows).astype(F32)


def _pack_compress(w, pe, double):
    w3 = w.reshape(CMP_LEN, HEAD_DIM, HEAD_DIM)
    w3 = jnp.concatenate([w3, jnp.zeros_like(w3)], axis=1)
    w3 = jnp.concatenate([w3, w3 if double else jnp.zeros_like(w3)], axis=2)
    wp = w3.reshape(2, CMP_STRIDE * 128, 128).astype(BF16)
    pe2 = jnp.concatenate([pe, jnp.zeros_like(pe)], axis=1).reshape(2, 1, CMP_STRIDE * 128)
    pe2 = jnp.broadcast_to(pe2, (2, 8, CMP_STRIDE * 128)).astype(F32)
    return wp, pe2


def _normed_len(gain, sizes):
    tot, off = 0.0, 0
    for n in sizes:
        tot = tot + n * jnp.max(jnp.square(gain[off:off + n]))
        off += n
    return jnp.sqrt(tot)


def _mla_logit_bound(qn, kn):
    sizes = (MLA_NOPE, MLA_ROPE)
    scale = (MLA_NOPE + MLA_ROPE) ** -0.5 * LOG2E
    return (_normed_len(qn, sizes) * _normed_len(kn, sizes) * (scale * ROUNDING_MARGIN)).reshape(1).astype(F32)


def _alibi_slopes(n):
    return 2.0 ** (-8.0 * jnp.arange(1, n + 1, dtype=jnp.float32) / n)


def kernel(x, c, positions, ada_w, ada_b, norm_g, ev_w_in, ev_w_out, sb_qn, sb_kn, mla_qa_g, mla_wq_up,
           mla_kva_g, mla_wkv_up, mla_qn, mla_kn, od_w_in, od_w_out, nsa_qn, nsa_kn, nsa_cmp_wk, nsa_cmp_wv,
           nsa_cmp_pe_k, nsa_cmp_pe_v, dil_qn, dil_kn):
    b, s, d = x.shape
    depth = ada_w.shape[0]
    pos_f = positions.astype(F32)

    inv_freq = ROPE_BASE ** (-jnp.arange(0, MLA_ROPE, 2, dtype=F32) / MLA_ROPE)
    ang = pos_f[:, None] * inv_freq[None, :]
    cos, sin = jnp.cos(ang), jnp.sin(ang)
    cos128 = jnp.concatenate([jnp.ones((s, 64), F32), cos, cos, jnp.zeros((s, 32), F32)], axis=1)
    sin128 = jnp.concatenate([jnp.zeros((s, 64), F32), -sin, sin, jnp.zeros((s, 32), F32)], axis=1)
    nsa_slopes = _alibi_slopes(NSA_HEADS)
    dil_slopes = _alibi_slopes(N_DIL * DIL_HEADS)
    nch = s // CMP_STRIDE
    chunk_sum = pos_f.reshape(nch, CMP_STRIDE).sum(axis=1)
    cpos = ((chunk_sum + jnp.roll(chunk_sum, -1)) / CMP_LEN).reshape(nch, 1)
    n_sel = s // SEL_LEN
    cst = np.arange(nch)[:, None] * CMP_STRIDE
    jst = np.arange(LANES)[None, :] * SEL_LEN
    ovl = ((cst <= jst + SEL_LEN - 1) & (cst + CMP_LEN - 1 >= jst) & (np.arange(LANES)[None, :] < n_sel))
    ovlt = jnp.asarray(ovl.astype(np.float32).T, BF16)
    pos_col = pos_f.reshape(s, 1)
    tqc = min(TQ_CMP, s)
    pos_rows = pos_f.reshape(s // tqc, 1, tqc)
    tqn = min(TQ_NSA, s)
    tkn = min(TK_NSA, tqn)
    assert tkn == TK_NSA
    pos2d = pos_f.reshape(s // tkn, 1, tkn)
    pos2d_win = pos_f.reshape(s // TK_WIN, 1, TK_WIN)
    q_first = pos_f[::tqn]
    k_last = pos_f[jnp.minimum((jnp.arange(SEL_TILE_STRIDE) + 1) * tkn - 1, s - 1)]
    group_slope = jnp.min(nsa_slopes.reshape(NSA_KV_HEADS, NSA_HPG), axis=1) * LOG2E
    sel_dead = group_slope[:, None, None] * (q_first[:, None] - k_last[None, :])[None] > F32_ZERO_EXP
    pos_sorted = jnp.all(pos_f[1:] >= pos_f[:-1])

    g64 = _group_matrix([(64, 1)] * 4, 256)
    gm = _group_matrix([(64, 1), (32, 1), (32, 0)] * 2, 256)
    tks = min(TK_SB, min(TQ_SB, s))
    tri = jnp.asarray(np.tril(np.ones((tks, tks), np.float32)), BF16)

    c8 = jnp.pad(c, ((0, 8 - b), (0, 0)))
    mod_all = _modulation(c8, ada_w, ada_b)

    for layer in range(depth):
        j = layer // 2
        mod3 = mod_all[layer, :b].reshape(b, 1, 3 * d)
        if layer % 2 == 0:
            w, wq, wqs, wkv, vec = _pack_even(ev_w_in, j, norm_g[layer], sb_qn[j], sb_kn[j], mla_qa_g[j],
                                              mla_wq_up[j], mla_kva_g[j], mla_wkv_up[j], mla_qn[j], mla_kn[j])
            sbq, sbk, sbv, gz, mq, mk, mv = _k1_even(x, mod3, vec, w, wq, wqs, wkv, g64, gm, cos128, sin128)
            sb_bound = (_normed_len(sb_qn[j], (HEAD_DIM,)) * _normed_len(sb_kn[j], (HEAD_DIM,))
                        * (HEAD_DIM ** -0.5 * LOG2E * ROUNDING_MARGIN)).reshape(1).astype(F32)
            o_sb = _sb_attention(sbq, sbk, sbv, tri, sb_bound, sb_bound[0] <= SAFE_LOGIT_BOUND)
            o_mla = _mla_attention(mq, mk, mv, _mla_logit_bound(mla_qn[j], mla_kn[j]))
            x = _out_even(x, mod3, o_sb, o_mla, gz, ev_w_out[j].astype(BF16))
        else:
            w_nsa, w_dil, vec = _pack_odd(od_w_in, j, norm_g[layer], nsa_qn[j], nsa_kn[j], dil_qn[j], dil_kn[j])
            q, ck, cv, sk, sv, wk, wv, gates, gz = _k1_nsa(x, mod3, vec, w_nsa, g64)
            dqkv = _k1_dil(x, mod3, vec, w_dil, g64)
            wck, pek = _pack_compress(nsa_cmp_wk[j], nsa_cmp_pe_k[j], False)
            wcv, pev = _pack_compress(nsa_cmp_wv[j], nsa_cmp_pe_v[j], True)
            kgain = jnp.concatenate([nsa_kn[j], jnp.zeros((64,), F32)]).reshape(1, 128)
            kc, vc = _compress(ck, cv, wck, wcv, pek, pev, kgain, g64[0:128, 0:128])
            o_c, selb, used = _nsa_cmp(nsa_slopes, q, kc, vc, ovlt, cpos, pos_rows, gates)
            flags = used[:, :, :, 0, :].reshape(b, NSA_KV_HEADS, s // tqn, tqn // min(TQ_CMP, s),
                                                SEL_TILE_STRIDE, LANES // SEL_TILE_STRIDE).max(axis=(3, 5))
            nsa_bound = (_normed_len(nsa_qn[j], (HEAD_DIM,)) * _normed_len(nsa_kn[j], (HEAD_DIM,))
                         * (HEAD_DIM ** -0.5 * LOG2E * ROUNDING_MARGIN)).reshape(1).astype(F32)
            safe = (nsa_bound[0] <= SAFE_LOGIT_BOUND) & pos_sorted
            flags = (flags > 0) & ~(safe & sel_dead[None])
            flags = flags.astype(jnp.int32).reshape(-1)
            o_s = _nsa_gqa(flags, nsa_slopes, nsa_bound, safe, q, selb, sk, sv, pos2d, pos_col, gates, 1)
            o_w = _nsa_gqa(flags, nsa_slopes, nsa_bound, safe, q, selb, wk, wv, pos2d_win, pos_col, gates, 2)
            dil_bound = (_normed_len(dil_qn[j], (HEAD_DIM,)) * _normed_len(dil_kn[j], (HEAD_DIM,))
                         * (HEAD_DIM ** -0.5 * LOG2E * ROUNDING_MARGIN)).reshape(1).astype(F32)
            dil_safe = (dil_bound[0] <= SAFE_LOGIT_BOUND) & pos_sorted
            dils, lses = [], []
            for g in range(N_DIL):
                o, lse = _dilated(dil_slopes, dil_bound, dil_safe, dqkv[3 * g], dqkv[3 * g + 1], dqkv[3 * g + 2],
                                  pos_f, g)
                dils.append(o)
                lses.append(lse)
            x = _out_odd(x, mod3, o_c, o_s, o_w, dils, lses, gz, od_w_out[j].astype(BF16))
    return x
```

```python
import functools

import numpy as np
import jax
import jax.numpy as jnp
from jax import lax
from jax.experimental import pallas as pl
from jax.experimental.pallas import tpu as pltpu

F32 = jnp.float32
BF16 = jnp.bfloat16

D_MODEL = 1024
HEAD_DIM = 64
NORM_EPS = 1e-6
TINY = 1e-30
SB_HEADS = 8
MLA_HEADS = 8
MLA_Q_RANK = 256
MLA_KV_RANK = 128
MLA_NOPE = 64
MLA_ROPE = 32
MLA_V = 64
ROPE_BASE = 10000.0
NSA_HEADS = 12
NSA_KV_HEADS = 3
NSA_HPG = 4
CMP_LEN = 32
CMP_STRIDE = 16
SEL_LEN = 64
SEL_TOPN = 16
WIN = 512
FORCE_BONUS = 1e3
DIL_CFG = ((128, 1), (512, 4), (2048, 16))
N_DIL = 3
DIL_HEADS = 4
SB_W = SB_HEADS * HEAD_DIM
MLA_OUT = MLA_HEADS * MLA_V
NSA_W = NSA_HEADS * HEAD_DIM
NSA_KV_W = NSA_KV_HEADS * HEAD_DIM
DIL_W = DIL_HEADS * HEAD_DIM

LANES = 128
NEG_MASK = -1e30
LOG2E = 1.4426950408889634
LN2 = 0.6931471805599453
F32_ZERO_EXP = 150.0
SAFE_LOGIT_BOUND = 50.0
ROUNDING_MARGIN = 1.02
SEL_OFF = 2.0 ** 30
VMEM_LIMIT = 56 * 1024 * 1024

TS_PROJ = 512
TQ_SB = 1024
TK_SB = 256
TQ_MLA = 1024
TD_MLA = 512
TQ_CMP = 512
TQ_NSA = 512
TK_NSA = 512
TK_WIN = 256
SEL_TILE_STRIDE = LANES * SEL_LEN // TK_NSA
T_DIL = 256


def _dot(a, b):
    return jnp.dot(a, b, preferred_element_type=F32)


def _dot_nt(a, b):
    return lax.dot_general(a, b, (((1,), (1,)), ((), ())), preferred_element_type=F32)


def _split_hl(a):
    hi = a.astype(BF16)
    lo = (a - hi.astype(F32)).astype(BF16)
    return hi, lo


def _dot_hl(a, b):
    hi, lo = _split_hl(a)
    return _dot(hi, b) + _dot(lo, b)


def _sigmoid(z):
    return 1.0 / (1.0 + jnp.exp(-z))


def _cparams(n_axes):
    return pltpu.CompilerParams(dimension_semantics=("arbitrary",) * n_axes,
                                vmem_limit_bytes=VMEM_LIMIT)


def _full(shape):
    n = len(shape)
    return pl.BlockSpec(shape, lambda *a, _n=n: (0,) * _n)


def _mod_kernel(c_ref, w_ref, b_ref, o_ref):
    c = c_ref[...]
    a = c * _sigmoid(c)
    ah, al = _split_hl(a)
    wh, wl = _split_hl(w_ref[0])
    o_ref[0] = _dot(ah, wh) + _dot(ah, wl) + _dot(al, wh) + b_ref[0]


def _modulation(c8, ada_w, ada_b):
    depth, d, n3 = ada_w.shape
    tn = 1024
    return pl.pallas_call(
        _mod_kernel,
        grid=(depth, n3 // tn),
        in_specs=[pl.BlockSpec((8, d), lambda l, j: (0, 0)),
                  pl.BlockSpec((1, d, tn), lambda l, j: (l, 0, j)),
                  pl.BlockSpec((1, 1, tn), lambda l, j: (l, 0, j))],
        out_specs=pl.BlockSpec((1, 8, tn), lambda l, j: (l, 0, j)),
        out_shape=jax.ShapeDtypeStruct((depth, 8, n3), F32),
        compiler_params=_cparams(2),
        name="adaln_mod",
    )(c8, ada_w, ada_b.reshape(depth, 1, n3))


def _modulated(x_ref, mod_ref, ng):
    x = x_ref[0]
    mod = mod_ref[0]
    shift = mod[:, 0:D_MODEL]
    scale = mod[:, D_MODEL:2 * D_MODEL]
    ms = jnp.mean(x * x, axis=-1, keepdims=True)
    h = x * lax.rsqrt(ms + NORM_EPS) * ng
    h = h * (1.0 + scale) + shift
    return h.astype(BF16)


def _group_sumsq(t, g):
    return _dot((t * t).astype(BF16), g)


def _group_norm_chunk(t, g, inv_cnt):
    return t * lax.rsqrt(_group_sumsq(t, g) * inv_cnt + NORM_EPS)


def _row_rms(t, gain):
    ms = jnp.mean(t * t, axis=-1, keepdims=True)
    return t * lax.rsqrt(ms + NORM_EPS) * gain


EV_COLS = 3200


def _k1_even_kernel(x_ref, mod_ref, vec_ref, w_ref, wq_ref, wqs_ref, wkv_ref, g64_ref, gm_ref,
                    cos_ref, sin_ref, sbq_o, sbk_o, sbv_o, gz_o, mq_o, mk_o, mv_o):
    hb = _modulated(x_ref, mod_ref, vec_ref[0:1, :])
    g64 = g64_ref[...]
    gm = gm_ref[...]
    inv64 = 1.0 / HEAD_DIM

    u = _dot(hb, w_ref[:, 0:1536])
    for c in range(2):
        sl = slice(c * 256, (c + 1) * 256)
        qn = _group_norm_chunk(u[:, c * 256:(c + 1) * 256], g64, inv64)
        sbq_o[0, :, sl] = (qn * vec_ref[1:2, sl]).astype(BF16)
        kn = _group_norm_chunk(u[:, 512 + c * 256:512 + (c + 1) * 256], g64, inv64)
        sbk_o[0, :, sl] = (kn * vec_ref[2:3, sl]).astype(BF16)
    sbv_o[0] = u[:, 1024:1536].astype(BF16)

    uz = _dot(hb, w_ref[:, 1536:2560])
    gz_o[0] = (uz * _sigmoid(uz)).astype(BF16)

    ul = _dot(hb, w_ref[:, 2560:3200])
    qlat = _row_rms(ul[:, 0:256], vec_ref[3:4, 0:256]).astype(BF16)
    kvlat = _row_rms(ul[:, 256:384], vec_ref[4:5, 0:128]).astype(BF16)
    krm = ul[:, 384:512]
    krs = ul[:, 512:640]
    cs = cos_ref[...]
    sn = sin_ref[...]
    cs2 = jnp.concatenate([cs, cs], axis=1)
    sn2 = jnp.concatenate([sn, sn], axis=1)

    tq = _dot(qlat, wq_ref[...])
    tqs = _dot(qlat, wqs_ref[...])
    for p in range(4):
        sl = slice(p * 256, (p + 1) * 256)
        tc = tq[:, p * 256:(p + 1) * 256]
        ss = _group_sumsq(tc, gm)
        inv = lax.rsqrt(ss * vec_ref[10:11, sl] + NORM_EPS)
        a = tc * inv * vec_ref[5:6, sl]
        b = tqs[:, p * 256:(p + 1) * 256] * inv * vec_ref[6:7, sl]
        mq_o[0, :, sl] = (a * cs2 + b * sn2).astype(BF16)

    sskr = _group_sumsq(krm, gm[0:128, 0:128])
    invr = lax.rsqrt(sskr * vec_ref[10:11, 0:128] + NORM_EPS)
    kr = (krm * invr * vec_ref[8:9, 0:128]) * cs + (krs * invr * vec_ref[9:10, 0:128]) * sn
    kr2 = jnp.concatenate([kr, kr], axis=1)

    kv = _dot(kvlat, wkv_ref[...])
    for p in range(4):
        sl = slice(p * 256, (p + 1) * 256)
        kn = _group_norm_chunk(kv[:, p * 256:(p + 1) * 256], g64, inv64)
        mk_o[0, :, sl] = (kn * vec_ref[7:8, sl] + kr2).astype(BF16)
    mv_o[0] = kv[:, 1024:1536].astype(BF16)


def _k1_even(x, mod3, vec, w, wq, wqs, wkv, g64, gm, cos128, sin128):
    b, s, d = x.shape
    ts = min(TS_PROJ, s)
    row = lambda n: pl.BlockSpec((1, ts, n), lambda i, j: (i, j, 0))
    outs = [(512, BF16), (512, BF16), (512, BF16), (1024, BF16), (1024, BF16), (1024, BF16), (512, BF16)]
    return pl.pallas_call(
        _k1_even_kernel,
        grid=(b, s // ts),
        in_specs=[row(d),
                  pl.BlockSpec((1, 1, 3 * d), lambda i, j: (i, 0, 0)),
                  _full(vec.shape), _full(w.shape), _full(wq.shape), _full(wqs.shape), _full(wkv.shape),
                  _full(g64.shape), _full(gm.shape),
                  pl.BlockSpec((ts, 128), lambda i, j: (j, 0)),
                  pl.BlockSpec((ts, 128), lambda i, j: (j, 0))],
        out_specs=[row(n) for n, _ in outs],
        out_shape=[jax.ShapeDtypeStruct((b, s, n), dt) for n, dt in outs],
        compiler_params=_cparams(2),
        name="inproj_even",
    )(x, mod3, vec, w, wq, wqs, wkv, g64, gm, cos128, sin128)


def _sb_kernel(bnd_ref, q_ref, k_ref, v_ref, tri_ref, o_ref, acc_ref, car_ref, *, tq, tk, bounded):
    qi = pl.program_id(2)
    q = q_ref[0]
    lane = lax.broadcasted_iota(jnp.int32, (1, LANES), 1)
    lo_half = lane < HEAD_DIM
    zero = jnp.zeros_like(q)
    qs = (jnp.where(lo_half, q, zero), jnp.where(lo_half, zero, q))
    tri = tri_ref[...]
    acc_ref[...] = jnp.zeros_like(acc_ref)
    car_ref[...] = jnp.zeros_like(car_ref)
    col0 = lax.broadcasted_iota(jnp.int32, (1, tk), 1)

    def tile(kt, r0, r1, masked):
        n = r1 - r0
        off = pl.multiple_of(kt * tk, tk)
        k = k_ref[0, pl.ds(off, tk), :]
        v = v_ref[0, pl.ds(off, tk), :]
        if masked:
            rows = qi * tq + r0 + lax.broadcasted_iota(jnp.int32, (n, 1), 0)
            strict = (kt * tk + col0) < rows
        for i in range(2):
            z = _dot_nt(qs[i][r0:r1], k)
            if bounded:
                lom = jnp.log(1.0 + jnp.exp2(z)) * (-LOG2E)
            else:
                lom = -jnp.maximum(z, 0.0) - jnp.log(1.0 + jnp.exp2(-jnp.abs(z))) * LOG2E
            if masked:
                lom = jnp.where(strict, lom, 0.0)
            tt = _dot(lom.astype(BF16), tri)
            car = car_ref[i, r0:r1]
            w = jnp.exp2(tt + z + jnp.concatenate([car] * (tk // LANES), axis=1))
            if masked:
                w = jnp.where(strict, w, 0.0)
            acc_ref[i, r0:r1] += _dot(w.astype(BF16), v)
            car_ref[i, r0:r1] = car + jnp.broadcast_to(tt[:, 0:1], (n, LANES))

    per = tq // tk
    if not bounded:
        for c in reversed(range(per)):
            tile(qi * per + c, c * tk, tq, True)

        def body(j, carry):
            for u in range(per):
                tile(qi * per - 1 - per * j - u, 0, tq, False)
            return carry

        lax.fori_loop(0, qi, body, 0)
    else:
        limit = -(F32_ZERO_EXP + bnd_ref[0])
        half = tq // 2

        def live(r0, r1):
            return jnp.max(jnp.maximum(car_ref[0, r0:r1], car_ref[1, r0:r1])) > limit

        def tile_if_live(kt, r0, r1):
            @pl.when(live(r0, r1))
            def _():
                tile(kt, r0, r1, False)

        for c in reversed(range(per)):
            near = min(tq, (c + 2) * tk)
            tile(qi * per + c, c * tk, near, True)
            if near < tq:
                tile_if_live(qi * per + c, near, tq)

        def single(c):
            j = c[0]
            tile_if_live(qi * per - 1 - j, 0, half)
            tile_if_live(qi * per - 1 - j, half, tq)
            return j + 1, live(0, tq)

        _, alive = lax.while_loop(lambda c: (c[0] < per) & c[1], single, (0, qi > 0))

        def group(c):
            g = c[0]
            for u in range(per):
                tile_if_live(qi * per - 1 - per * (g + 1) - u, 0, half)
                tile_if_live(qi * per - 1 - per * (g + 1) - u, half, tq)
            return g + 1, live(0, tq)

        lax.while_loop(lambda c: (c[0] < qi - 1) & c[1], group, (0, alive))
    o_ref[0] = jnp.where(lo_half, acc_ref[0], acc_ref[1])


def _sb_attention(q, k, v, tri, bound, safe):
    b, s, w = q.shape
    tq = min(TQ_SB, s)
    tk = min(TK_SB, tq)

    def call(bounded):
        kern = functools.partial(_sb_kernel, tq=tq, tk=tk, bounded=bounded)
        return pl.pallas_call(
            kern,
            grid=(b, w // LANES, s // tq),
            in_specs=[pl.BlockSpec(memory_space=pltpu.SMEM),
                      pl.BlockSpec((1, tq, LANES), lambda i, p, j: (i, j, p)),
                      pl.BlockSpec((1, s, LANES), lambda i, p, j: (i, 0, p)),
                      pl.BlockSpec((1, s, LANES), lambda i, p, j: (i, 0, p)),
                      _full(tri.shape)],
            out_specs=pl.BlockSpec((1, tq, LANES), lambda i, p, j: (i, j, p)),
            out_shape=jax.ShapeDtypeStruct((b, s, w), F32),
            scratch_shapes=[pltpu.VMEM((2, tq, LANES), F32), pltpu.VMEM((2, tq, LANES), F32)],
            compiler_params=_cparams(3),
            name="stickbreak_attn_bounded" if bounded else "stickbreak_attn",
        )(bound, q, k, v, tri)

    return lax.cond(safe, lambda: call(True), lambda: call(False))


def _lane_max(s):
    m = s[:, 0:LANES]
    for c in range(1, s.shape[1] // LANES):
        m = jnp.maximum(m, s[:, c * LANES:(c + 1) * LANES])
    return m


def _mla_kernel(bnd_ref, q_ref, k_ref, v_ref, o_ref, acc_ref, m_ref, *, tq, td, bounded):
    qi = pl.program_id(2)
    q = q_ref[0]
    qs = (q[:, 0:LANES], q[:, LANES:2 * LANES])
    lane = lax.broadcasted_iota(jnp.int32, (1, LANES), 1)
    lo_half = lane < HEAD_DIM
    acc_ref[...] = jnp.zeros_like(acc_ref)
    if bounded:
        m_ref[...] = jnp.full(m_ref.shape, bnd_ref[0], F32)
    else:
        m_ref[...] = jnp.full_like(m_ref, NEG_MASK)

    def scores(off, width, r0, masked):
        k = k_ref[0, pl.ds(off, width), :]
        if masked:
            rows = qi * tq + r0 + lax.broadcasted_iota(jnp.int32, (tq - r0, 1), 0)
            keep = (off + lax.broadcasted_iota(jnp.int32, (1, width), 1)) <= rows
        out = []
        for i in range(2):
            s = _dot_nt(qs[i][r0:tq], k[:, i * LANES:(i + 1) * LANES])
            if masked:
                s = jnp.where(keep, s, NEG_MASK)
            out.append(s)
        return out

    def sweep(fn):
        def body(j, carry):
            fn(pl.multiple_of(j * tq, tq), tq, 0, False)
            return carry
        lax.fori_loop(0, qi, body, 0)
        for c in range(tq // td):
            fn(pl.multiple_of(qi * tq + c * td, td), td, c * td, True)

    def row_max(off, width, r0, masked):
        for i, s in enumerate(scores(off, width, r0, masked)):
            m_ref[i, r0:tq] = jnp.maximum(m_ref[i, r0:tq], _lane_max(s))

    def accumulate(off, width, r0, masked):
        vx = jnp.concatenate([v_ref[0, pl.ds(off, width), :], jnp.ones((width, LANES), BF16)], axis=1)
        for i, s in enumerate(scores(off, width, r0, masked)):
            p = jnp.exp2(s - jnp.concatenate([m_ref[i, r0:tq]] * (width // LANES), axis=1))
            acc_ref[i, r0:tq] += _dot(p.astype(BF16), vx)

    if not bounded:
        sweep(row_max)
        for i in range(2):
            m_ref[i] = jnp.broadcast_to(jnp.max(m_ref[i], axis=-1, keepdims=True), (tq, LANES))
    sweep(accumulate)
    o_ref[0] = jnp.where(lo_half, acc_ref[0, :, 0:LANES] / acc_ref[0, :, LANES:2 * LANES],
                         acc_ref[1, :, 0:LANES] / acc_ref[1, :, LANES:2 * LANES])


def _mla_attention(q, k, v, bound):
    b, s, _ = q.shape
    tq = min(TQ_MLA, s)
    td = min(TD_MLA, tq)

    def call(bounded):
        kern = functools.partial(_mla_kernel, tq=tq, td=td, bounded=bounded)
        return pl.pallas_call(
            kern,
            grid=(b, MLA_HEADS // 2, s // tq),
            in_specs=[pl.BlockSpec(memory_space=pltpu.SMEM),
                      pl.BlockSpec((1, tq, 2 * LANES), lambda i, p, j: (i, j, p)),
                      pl.BlockSpec((1, s, 2 * LANES), lambda i, p, j: (i, 0, p)),
                      pl.BlockSpec((1, s, LANES), lambda i, p, j: (i, 0, p))],
            out_specs=pl.BlockSpec((1, tq, LANES), lambda i, p, j: (i, j, p)),
            out_shape=jax.ShapeDtypeStruct((b, s, MLA_OUT), F32),
            scratch_shapes=[pltpu.VMEM((2, tq, 2 * LANES), F32), pltpu.VMEM((2, tq, LANES), F32)],
            compiler_params=_cparams(3),
            name="mla_attn_bounded" if bounded else "mla_attn",
        )(bound, q, k, v)

    return lax.cond(bound[0] <= SAFE_LOGIT_BOUND, lambda: call(True), lambda: call(False))


def _out_even_kernel(x_ref, mod_ref, osb_ref, omla_ref, gz_ref, w_ref, o_ref):
    gz = gz_ref[0].astype(F32)
    m1 = (osb_ref[0] * gz[:, 0:SB_W]).astype(BF16)
    m2 = (omla_ref[0] * gz[:, SB_W:SB_W + MLA_OUT]).astype(BF16)
    y = _dot(m1, w_ref[0:SB_W, :]) + _dot(m2, w_ref[SB_W:SB_W + MLA_OUT, :])
    gate = mod_ref[0][:, 2 * D_MODEL:3 * D_MODEL]
    o_ref[0] = x_ref[0] + gate * y


def _out_even(x, mod3, osb, omla, gz, w):
    b, s, d = x.shape
    ts = min(TS_PROJ, s)
    row = lambda n: pl.BlockSpec((1, ts, n), lambda i, j: (i, j, 0))
    return pl.pallas_call(
        _out_even_kernel,
        grid=(b, s // ts),
        in_specs=[row(d), pl.BlockSpec((1, 1, 3 * d), lambda i, j: (i, 0, 0)),
                  row(SB_W), row(MLA_OUT), row(SB_W + MLA_OUT), _full(w.shape)],
        out_specs=row(d),
        out_shape=jax.ShapeDtypeStruct((b, s, d), F32),
        compiler_params=_cparams(2),
        name="outproj_even",
    )(x, mod3, osb, omla, gz, w)


NSA_COLS = 5248


def _k1_nsa_kernel(x_ref, mod_ref, vec_ref, w_ref, g64_ref,
                   q_o, ck_o, cv_o, sk_o, sv_o, wk_o, wv_o, gt_o, gz_o, chunk_ref):
    ts = x_ref.shape[1]
    hb = _modulated(x_ref, mod_ref, vec_ref[0:1, 0:D_MODEL])
    g64 = g64_ref[...]
    inv64 = 1.0 / HEAD_DIM

    uq = _dot(hb, w_ref[:, 0:1536])
    for c in range(6):
        sl = slice(c * 256, (c + 1) * 256)
        qn = _group_norm_chunk(uq[:, c * 256:(c + 1) * 256], g64, inv64)
        q_o[0, :, sl] = (qn * vec_ref[1:2, sl]).astype(BF16)

    uc = _dot(hb, w_ref[:, 1536:2304])
    for c in range(2 * NSA_KV_HEADS):
        chunk_ref[c] = uc[:, c * 128:(c + 1) * 128]
    for g in range(NSA_KV_HEADS):
        for tok in range(CMP_STRIDE):
            rows = pl.ds(tok, ts // CMP_STRIDE, stride=CMP_STRIDE)
            sl = slice(tok * 128, (tok + 1) * 128)
            ck_o[0, g, :, sl] = chunk_ref[g, rows, :].astype(BF16)
            cv_o[0, g, :, sl] = chunk_ref[NSA_KV_HEADS + g, rows, :].astype(BF16)

    us = _dot(hb, w_ref[:, 2304:3840])
    kgain = vec_ref[2:3, 0:384]
    g128 = g64[0:128, 0:128]
    for g in range(NSA_KV_HEADS):
        sl = slice(g * 128, (g + 1) * 128)
        t = us[:, g * 128:(g + 1) * 128]
        ss = _group_sumsq(t, g128)
        sk_o[0, :, sl] = (t * lax.rsqrt(ss * inv64 + NORM_EPS) * kgain[:, sl]).astype(BF16)
        t = us[:, 768 + g * 128:768 + (g + 1) * 128]
        ss = _group_sumsq(t, g128)
        wk_o[0, :, sl] = (t * lax.rsqrt(ss * inv64 + NORM_EPS) * kgain[:, sl]).astype(BF16)
    ones_hi = (lax.broadcasted_iota(jnp.int32, (1, 384), 1) % 128) >= HEAD_DIM
    sv_o[0] = jnp.where(ones_hi, 1.0, us[:, 384:768]).astype(BF16)
    wv_o[0] = jnp.where(ones_hi, 1.0, us[:, 1152:1536]).astype(BF16)

    ug = _dot(hb, w_ref[:, 3840:4224])
    gt_o[0] = _sigmoid(ug)
    uz = _dot(hb, w_ref[:, 4224:5248])
    gz_o[0] = (uz * _sigmoid(uz)).astype(BF16)


def _k1_nsa(x, mod3, vec, w, g64):
    b, s, d = x.shape
    ts = min(TS_PROJ, s)
    row = lambda n: pl.BlockSpec((1, ts, n), lambda i, j: (i, j, 0))
    cw = CMP_STRIDE * 128
    grp = pl.BlockSpec((1, NSA_KV_HEADS, ts // CMP_STRIDE, cw), lambda i, j: (i, 0, j, 0))
    return pl.pallas_call(
        _k1_nsa_kernel,
        grid=(b, s // ts),
        in_specs=[row(d), pl.BlockSpec((1, 1, 3 * d), lambda i, j: (i, 0, 0)),
                  _full(vec.shape), _full(w.shape), _full(g64.shape)],
        out_specs=[row(1536), grp, grp, row(384), row(384), row(384), row(384), row(384), row(1024)],
        out_shape=[jax.ShapeDtypeStruct((b, s, 1536), BF16),
                   jax.ShapeDtypeStruct((b, NSA_KV_HEADS, s // CMP_STRIDE, cw), BF16),
                   jax.ShapeDtypeStruct((b, NSA_KV_HEADS, s // CMP_STRIDE, cw), BF16),
                   jax.ShapeDtypeStruct((b, s, 384), BF16),
                   jax.ShapeDtypeStruct((b, s, 384), BF16),
                   jax.ShapeDtypeStruct((b, s, 384), BF16),
                   jax.ShapeDtypeStruct((b, s, 384), BF16),
                   jax.ShapeDtypeStruct((b, s, 384), F32),
                   jax.ShapeDtypeStruct((b, s, 1024), BF16)],
        scratch_shapes=[pltpu.VMEM((2 * NSA_KV_HEADS, ts, 128), F32)],
        compiler_params=_cparams(2),
        name="inproj_nsa",
    )(x, mod3, vec, w, g64)


def _k1_dil_kernel(x_ref, mod_ref, vec_ref, w_ref, g64_ref, *refs):
    outs, st = refs[:3 * N_DIL], refs[3 * N_DIL]
    ts = x_ref.shape[1]
    hb = _modulated(x_ref, mod_ref, vec_ref[0:1, 0:D_MODEL])
    g64 = g64_ref[...]
    inv64 = 1.0 / HEAD_DIM
    u = _dot(hb, w_ref[...])
    for g in range(N_DIL):
        dil = DIL_CFG[g][1]
        qn = _group_norm_chunk(u[:, g * 256:(g + 1) * 256], g64, inv64) * vec_ref[3:4, 0:256]
        kn = _group_norm_chunk(u[:, 768 + g * 256:768 + (g + 1) * 256], g64, inv64) * vec_ref[4:5, 0:256]
        vals = (qn, kn, u[:, 1536 + g * 256:1536 + (g + 1) * 256])
        for j, val in enumerate(vals):
            o_ref = outs[3 * g + j]
            if dil == 1:
                o_ref[0] = val.astype(BF16)
                continue
            for h in range(2):
                st[j, h] = val[:, h * 128:(h + 1) * 128]
            for r in range(dil):
                for h in range(2):
                    sl = slice(r * 256 + h * 128, r * 256 + (h + 1) * 128)
                    o_ref[0, :, sl] = st[j, h, pl.ds(r, ts // dil, stride=dil), :].astype(BF16)


def _k1_dil(x, mod3, vec, w, g64):
    b, s, d = x.shape
    ts = min(TS_PROJ, s)
    row = lambda n: pl.BlockSpec((1, ts, n), lambda i, j: (i, j, 0))
    specs, shapes = [], []
    for g in range(N_DIL):
        dil = DIL_CFG[g][1]
        for _ in range(3):
            specs.append(pl.BlockSpec((1, ts // dil, dil * 256), lambda i, j: (i, j, 0)))
            shapes.append(jax.ShapeDtypeStruct((b, s // dil, dil * 256), BF16))
    return pl.pallas_call(
        _k1_dil_kernel,
        grid=(b, s // ts),
        in_specs=[row(d), pl.BlockSpec((1, 1, 3 * d), lambda i, j: (i, 0, 0)),
                  _full(vec.shape), _full(w.shape), _full(g64.shape)],
        out_specs=specs,
        out_shape=shapes,
        scratch_shapes=[pltpu.VMEM((3, 2, ts, 128), F32)],
        compiler_params=_cparams(2),
        name="inproj_dil",
    )(x, mod3, vec, w, g64)


def _compress_kernel(xk_ref, xv_ref, wk_ref, wv_ref, pek_ref, pev_ref, gain_ref, g128_ref, kc_o, vc_o):
    nch = xk_ref.shape[2]

    def comp(x_ref, w_ref, pe_ref):
        x = x_ref[0, 0]
        lo = _dot(x, w_ref[0])
        hi = _dot(x, w_ref[1])
        pec = _dot_hl(pe_ref[0], w_ref[0]) + _dot_hl(pe_ref[1], w_ref[1])
        return lo + pltpu.roll(hi, nch - 1, 0) + pec[0:1, :]

    kc = comp(xk_ref, wk_ref, pek_ref)
    ss = _dot_hl(kc * kc, g128_ref[...])
    kc_o[0, 0] = (kc * lax.rsqrt(ss * (1.0 / HEAD_DIM) + NORM_EPS) * gain_ref[...]).astype(BF16)
    vc_o[0, 0] = comp(xv_ref, wv_ref, pev_ref).T.astype(BF16)


def _compress(ck, cv, wk, wv, pek, pev, gain, g128):
    b, g, nch, _ = ck.shape
    xk, xv = ck, cv
    blk = pl.BlockSpec((1, 1, nch, CMP_STRIDE * 128), lambda i, j: (i, j, 0, 0))
    oblk = pl.BlockSpec((1, 1, nch, 128), lambda i, j: (i, j, 0, 0))
    tblk = pl.BlockSpec((1, 1, 128, nch), lambda i, j: (i, j, 0, 0))
    return pl.pallas_call(
        _compress_kernel,
        grid=(b, g),
        in_specs=[blk, blk, _full(wk.shape), _full(wv.shape), _full(pek.shape), _full(pev.shape),
                  _full(gain.shape), _full(g128.shape)],
        out_specs=[oblk, tblk],
        out_shape=[jax.ShapeDtypeStruct((b, g, nch, 128), BF16), jax.ShapeDtypeStruct((b, g, 128, nch), BF16)],
        compiler_params=_cparams(2),
        name="nsa_compress",
    )(xk, xv, wk, wv, pek, pev, gain, g128)


def _pair_select(lo_half, a, b):
    return jnp.where(lo_half, a, b)


def _nsa_cmp_kernel(slope_ref, q_ref, kc_ref, vct_ref, ovlt_ref, cposc_ref, prow_ref, gt_ref,
                    oc_o, selb_o, any_o, *, tq):
    g = pl.program_id(1)
    qi = pl.program_id(2)
    q4 = q_ref[0]
    kc = kc_ref[0, 0]
    vct = vct_ref[0, 0]
    nch = kc.shape[0]
    t = qi * tq + lax.broadcasted_iota(jnp.int32, (1, tq), 1)
    n_id = lax.broadcasted_iota(jnp.int32, (nch, 1), 0)
    valid = (n_id * CMP_STRIDE + (CMP_LEN - 1)) <= t
    rel = (cposc_ref[...] - prow_ref[qi][:, 0:1]) * LOG2E
    gtt = gt_ref[0].T
    row_lo = lax.broadcasted_iota(jnp.int32, (LANES, 1), 0) < HEAD_DIM
    psum = jnp.zeros((nch, tq), F32)
    outs = []
    for hh in range(NSA_HPG):
        s = _dot_nt(kc, q4[:, hh * LANES:(hh + 1) * LANES]) + slope_ref[g * NSA_HPG + hh] * rel
        s = jnp.where(valid, s, -jnp.inf)
        mx = jnp.max(s, axis=0, keepdims=True)
        mx = jnp.where(mx == -jnp.inf, 0.0, mx)
        e = jnp.exp2(s - mx)
        den = jnp.maximum(jnp.sum(e, axis=0, keepdims=True), TINY)
        p = e * (1.0 / den)
        psum = psum + p
        outs.append(_dot(vct, p.astype(BF16)) * gtt[3 * hh:3 * hh + 1, :])
    oc_o[0, :, 0:LANES] = jnp.where(row_lo, outs[0], outs[1]).T
    oc_o[0, :, LANES:2 * LANES] = jnp.where(row_lo, outs[2], outs[3]).T

    hi, lo = _split_hl(psum)
    imp = _dot(ovlt_ref[...], hi) + _dot(ovlt_ref[...], lo)
    blk = lax.broadcasted_iota(jnp.int32, (LANES, 1), 0)
    cur = t >> 6
    forced = (blk == 0) | (blk == cur) | (blk == cur - 1)
    allowed = blk <= cur
    assert NSA_HPG < FORCE_BONUS
    score = jnp.where(allowed & ~forced, imp, -jnp.inf)
    blk_f = blk.astype(F32)

    def pick(_, sc):
        mx = jnp.max(sc, axis=0, keepdims=True)
        idx = jnp.min(jnp.where(sc == mx, blk_f, float(LANES)), axis=0, keepdims=True)
        return jnp.where(blk_f == idx, -jnp.inf, sc)

    left = lax.fori_loop(0, SEL_TOPN - 3, pick, score)
    chosen = jnp.where((left == -jnp.inf) & allowed, 1.0, 0.0).T
    selb_o[0, 0] = ((chosen - 1.0) * SEL_OFF).astype(BF16)
    used = jnp.max(chosen, axis=0, keepdims=True)
    any_o[0, 0, 0] = jnp.broadcast_to(used, (8, LANES))


def _nsa_cmp(slopes, q, kc, vct, ovlt, cposc, pos_rows, gates):
    b, s, _ = q.shape
    nch = kc.shape[2]
    tq = pos_rows.shape[2]
    kern = functools.partial(_nsa_cmp_kernel, tq=tq)
    return pl.pallas_call(
        kern,
        grid=(b, NSA_KV_HEADS, s // tq),
        in_specs=[pl.BlockSpec(memory_space=pltpu.SMEM),
                  pl.BlockSpec((1, tq, 4 * LANES), lambda i, g, j: (i, j, g)),
                  pl.BlockSpec((1, 1, nch, LANES), lambda i, g, j: (i, g, 0, 0)),
                  pl.BlockSpec((1, 1, LANES, nch), lambda i, g, j: (i, g, 0, 0)),
                  _full(ovlt.shape), _full(cposc.shape), _full(pos_rows.shape),
                  pl.BlockSpec((1, tq, LANES), lambda i, g, j: (i, j, g))],
        out_specs=[pl.BlockSpec((1, tq, 2 * LANES), lambda i, g, j: (i, j, g)),
                   pl.BlockSpec((1, 1, tq, LANES), lambda i, g, j: (i, g, j, 0)),
                   pl.BlockSpec((1, 1, 1, 8, LANES), lambda i, g, j: (i, g, j, 0, 0))],
        out_shape=[jax.ShapeDtypeStruct((b, s, NSA_W), F32),
                   jax.ShapeDtypeStruct((b, NSA_KV_HEADS, s, LANES), BF16),
                   jax.ShapeDtypeStruct((b, NSA_KV_HEADS, s // tq, 8, LANES), F32)],
        compiler_params=_cparams(3),
        name="nsa_cmp_topk",
    )(slopes, q, kc, vct, ovlt, cposc, pos_rows, gates)


def _gqa_kernel(flag_ref, slope_ref, bnd_ref, q_ref, selb_ref, k_ref, v_ref, pos_ref, pcol_ref, gt_ref, o_ref,
                qa_ref, acc_ref, m_ref, *, tq, tk, branch, bounded):
    bi = pl.program_id(0)
    g = pl.program_id(1)
    qi = pl.program_id(2)
    q4 = q_ref[0]
    lane = lax.broadcasted_iota(jnp.int32, (1, LANES), 1)
    lo_half = lane < HEAD_DIM
    pref = pos_ref[qi * (tq // tk)][:, 0:1]
    for hh in range(NSA_HPG):
        sl = slice(hh * tq, (hh + 1) * tq)
        qa_ref[sl, 0:LANES] = q4[:, hh * LANES:(hh + 1) * LANES]
        if branch == 1:
            qa_ref[sl, LANES:2 * LANES] = selb_ref[0, 0]
        if bounded:
            own = (pcol_ref[...] - pref) * (LOG2E * slope_ref[g * NSA_HPG + hh]) + bnd_ref[0]
            m_ref[sl] = jnp.broadcast_to(own, (tq, LANES))
    acc_ref[...] = jnp.zeros_like(acc_ref)
    if not bounded:
        m_ref[...] = jnp.full_like(m_ref, NEG_MASK)
    rows = qi * tq + lax.broadcasted_iota(jnp.int32, (tq, 1), 0)
    col0 = lax.broadcasted_iota(jnp.int32, (1, tk), 1)
    krow = lax.broadcasted_iota(jnp.int32, (tk, 1), 0)
    per = tq // tk
    nrep = tk // LANES
    flag0 = ((bi * NSA_KV_HEADS + g) * pl.num_programs(2) + qi) * SEL_TILE_STRIDE

    def scores(kt_true, masked, r_lo, r_hi):
        kt = jnp.maximum(kt_true, 0)
        off = pl.multiple_of(kt * tk, tk)
        k = k_ref[0, pl.ds(off, tk), :]
        if branch == 1:
            blk = (kt * tk + krow) >> 6
            onehot = jnp.where(lane == blk, 1.0, 0.0).astype(BF16)
            s_all = _dot_nt(qa_ref[...], jnp.concatenate([k, onehot], axis=1))
            parts = [s_all[hh * tq:(hh + 1) * tq] for hh in range(NSA_HPG)]
        else:
            parts = [_dot_nt(qa_ref[hh * tq + r_lo:hh * tq + r_hi, 0:LANES], k) for hh in range(NSA_HPG)]
        rel = (pos_ref[kt] - pref) * LOG2E
        if masked:
            cols = kt_true * tk + col0
            d = rows[r_lo:r_hi] - cols
            keep = d >= 0
            if branch == 2:
                keep = keep & (d < WIN) & (cols >= 0)
        out = []
        for hh in range(NSA_HPG):
            s = parts[hh] + slope_ref[g * NSA_HPG + hh] * rel
            if masked:
                s = jnp.where(keep, s, NEG_MASK)
            out.append(s)
        return out

    def sweep(fn):
        if branch == 1:
            def body(j, carry):
                @pl.when(flag_ref[flag0 + j] != 0)
                def _():
                    fn(j, False, 0, tq)
                return carry
            lax.fori_loop(0, qi * per, body, 0)
            for dd in range(per):
                fn(qi * per + dd, True, 0, tq)
        else:
            for c in range(-(WIN // tk), per):
                r_lo = max(0, c * tk)
                r_hi = min(tq, -(-(c * tk + tk + WIN - 1) // 8) * 8)
                fn(qi * per + c, True, r_lo, r_hi)

    def row_max(kt_true, masked, r_lo, r_hi):
        for hh, s in enumerate(scores(kt_true, masked, r_lo, r_hi)):
            sl = slice(hh * tq + r_lo, hh * tq + r_hi)
            m_ref[sl] = jnp.maximum(m_ref[sl], _lane_max(s))

    def accumulate(kt_true, masked, r_lo, r_hi):
        off = pl.multiple_of(jnp.maximum(kt_true, 0) * tk, tk)
        v = v_ref[0, pl.ds(off, tk), :]
        for hh, s in enumerate(scores(kt_true, masked, r_lo, r_hi)):
            sl = slice(hh * tq + r_lo, hh * tq + r_hi)
            p = jnp.exp2(s - jnp.concatenate([m_ref[sl]] * nrep, axis=1))
            acc_ref[sl] += _dot(p.astype(BF16), v)

    if not bounded:
        sweep(row_max)
        for hh in range(NSA_HPG):
            sl = slice(hh * tq, (hh + 1) * tq)
            m_ref[sl] = jnp.broadcast_to(jnp.max(m_ref[sl], axis=-1, keepdims=True), (tq, LANES))
    sweep(accumulate)

    gt = gt_ref[0]
    for pr in range(2):
        res = []
        for x in range(2):
            hh = 2 * pr + x
            a = acc_ref[hh * tq:(hh + 1) * tq]
            r = pltpu.roll(a, HEAD_DIM, 1)
            o = a / r if x == 0 else r / a
            res.append(o * gt[:, 3 * hh + branch:3 * hh + branch + 1])
        o_ref[0, :, pr * LANES:(pr + 1) * LANES] = _pair_select(lo_half, res[0], res[1])


def _nsa_gqa(flags, slopes, bound, safe, q, selb, k, v, pos2d, pos_col, gates, branch):
    b, s, _ = q.shape
    tq = min(TQ_NSA, s)
    tk = pos2d.shape[2]

    def call(bounded):
        kern = functools.partial(_gqa_kernel, tq=tq, tk=tk, branch=branch, bounded=bounded)
        smem = pl.BlockSpec(memory_space=pltpu.SMEM)
        name = ("nsa_sel_attn" if branch == 1 else "nsa_win_attn") + ("_bounded" if bounded else "")
        return pl.pallas_call(
            kern,
            grid=(b, NSA_KV_HEADS, s // tq),
            in_specs=[smem, smem, smem,
                      pl.BlockSpec((1, tq, 4 * LANES), lambda i, g, j: (i, j, g)),
                      pl.BlockSpec((1, 1, tq, LANES), lambda i, g, j: (i, g, j, 0)),
                      pl.BlockSpec((1, s, LANES), lambda i, g, j: (i, 0, g)),
                      pl.BlockSpec((1, s, LANES), lambda i, g, j: (i, 0, g)),
                      _full(pos2d.shape),
                      pl.BlockSpec((tq, 1), lambda i, g, j: (j, 0)),
                      pl.BlockSpec((1, tq, LANES), lambda i, g, j: (i, j, g))],
            out_specs=pl.BlockSpec((1, tq, 2 * LANES), lambda i, g, j: (i, j, g)),
            out_shape=jax.ShapeDtypeStruct((b, s, NSA_W), F32),
            scratch_shapes=[pltpu.VMEM((NSA_HPG * tq, 2 * LANES), BF16),
                            pltpu.VMEM((NSA_HPG * tq, LANES), F32),
                            pltpu.VMEM((NSA_HPG * tq, LANES), F32)],
            compiler_params=_cparams(3),
            name=name,
        )(flags, slopes, bound, q, selb, k, v, pos2d, pos_col, gates)

    return lax.cond(safe, lambda: call(True), lambda: call(False))


def _dil_kernel(slope_ref, bnd_ref, q_ref, k_ref, v_ref, pc_ref, pr_ref, o_ref, lse_ref,
                *, t, span, grp, bounded):
    i = pl.program_id(2)
    q4 = q_ref[0]
    lane = lax.broadcasted_iota(jnp.int32, (1, LANES), 1)
    lo_half = lane < HEAD_DIM
    per = t // LANES
    prev = jnp.maximum(i * per - 1, 0)
    o_prev = pl.multiple_of(prev * LANES, LANES)
    o_cur = pl.multiple_of(i * t, t)
    kk = jnp.concatenate([k_ref[0, pl.ds(o_prev, LANES), :], k_ref[0, pl.ds(o_cur, t), :]], axis=0)
    vv = jnp.concatenate([v_ref[0, pl.ds(o_prev, LANES), :], v_ref[0, pl.ds(o_cur, t), :]], axis=0)
    pk = jnp.concatenate([pr_ref[0, prev]] + [pr_ref[0, i * per + c] for c in range(per)], axis=1)
    dist = (pc_ref[0] - pk) * LOG2E
    rows = i * t + lax.broadcasted_iota(jnp.int32, (t, 1), 0)
    cols = i * t - LANES + lax.broadcasted_iota(jnp.int32, (1, t + LANES), 1)
    d = rows - cols
    valid = (cols >= 0) & (d >= 0) & (d <= span)
    zero = jnp.zeros((t, LANES), BF16)
    ones = jnp.ones((t + LANES, LANES), BF16)
    for p in range(2):
        qp = q4[:, p * LANES:(p + 1) * LANES]
        kp = kk[:, p * LANES:(p + 1) * LANES]
        vx = jnp.concatenate([vv[:, p * LANES:(p + 1) * LANES], ones], axis=1)
        res = []
        for x in range(2):
            qx = jnp.where(lo_half, qp, zero) if x == 0 else jnp.where(lo_half, zero, qp)
            s = _dot_nt(qx, kp) - slope_ref[grp * DIL_HEADS + 2 * p + x] * dist
            s = jnp.where(valid, s, -jnp.inf)
            mx = bnd_ref[0] if bounded else jnp.max(s, axis=-1, keepdims=True)
            acc = _dot(jnp.exp2(s - mx).astype(BF16), vx)
            den = acc[:, LANES:2 * LANES]
            res.append((acc[:, 0:LANES] / den, mx * LN2 + jnp.log(den)))
        sl = slice(p * LANES, (p + 1) * LANES)
        o_ref[0, :, sl] = _pair_select(lo_half, res[0][0], res[1][0])
        lse_ref[0, :, sl] = jnp.where(lo_half, res[0][1], res[1][1])


def _dilated(slopes, bound, safe, qv, kv, vv, pos_f, grp):
    window, dil = DIL_CFG[grp]
    b, sub, wd = qv.shape
    w = wd // dil
    t = min(T_DIL, sub)
    span = window // dil
    assert span <= LANES and t % LANES == 0
    pres = pos_f.reshape(sub, dil).T
    pc = pres.reshape(dil, sub, 1)
    pr = pres.reshape(dil, sub // LANES, 1, LANES)
    shp = jax.ShapeDtypeStruct((b, sub, wd), F32)
    smem = pl.BlockSpec(memory_space=pltpu.SMEM)

    def call(bounded):
        kern = functools.partial(_dil_kernel, t=t, span=span, grp=grp, bounded=bounded)
        return pl.pallas_call(
            kern,
            grid=(b, dil, sub // t),
            in_specs=[smem, smem,
                      pl.BlockSpec((1, t, w), lambda i, r, j: (i, j, r)),
                      pl.BlockSpec((1, sub, w), lambda i, r, j: (i, 0, r)),
                      pl.BlockSpec((1, sub, w), lambda i, r, j: (i, 0, r)),
                      pl.BlockSpec((1, t, 1), lambda i, r, j: (r, j, 0)),
                      pl.BlockSpec((1, sub // LANES, 1, LANES), lambda i, r, j: (r, 0, 0, 0))],
            out_specs=[pl.BlockSpec((1, t, w), lambda i, r, j: (i, j, r)),
                       pl.BlockSpec((1, t, w), lambda i, r, j: (i, j, r))],
            out_shape=[shp, shp],
            compiler_params=_cparams(3),
            name="dilated_attn_g%d%s" % (grp, "_bounded" if bounded else ""),
        )(slopes, bound, qv, kv, vv, pc, pr)

    return lax.cond(safe, lambda: call(True), lambda: call(False))


def _out_odd_kernel(x_ref, mod_ref, oc_ref, os_ref, ow_ref, d0_ref, d1_ref, d2_ref,
                    l0_ref, l1_ref, l2_ref, gz_ref, w_ref, o_ref, st):
    ts = x_ref.shape[1]
    gz = gz_ref[0].astype(F32)
    nsa = oc_ref[0] + os_ref[0] + ow_ref[0]
    m1 = (nsa * gz[:, 0:NSA_W]).astype(BF16)

    def token_order(ref, g, slot):
        dil = DIL_CFG[g][1]
        if dil == 1:
            return ref[0]
        for r in range(dil):
            for h in range(2):
                st[slot, h, pl.ds(r, ts // dil, stride=dil), :] = ref[0, :, r * DIL_W + h * 128:r * DIL_W + (h + 1) * 128]
        return jnp.concatenate([st[slot, 0], st[slot, 1]], axis=1)

    d0, d1, d2 = token_order(d0_ref, 0, 0), token_order(d1_ref, 1, 0), token_order(d2_ref, 2, 1)
    l0, l1, l2 = token_order(l0_ref, 0, 2), token_order(l1_ref, 1, 2), token_order(l2_ref, 2, 3)
    mx = jnp.maximum(jnp.maximum(l0, l1), l2)
    e0, e1, e2 = jnp.exp(l0 - mx), jnp.exp(l1 - mx), jnp.exp(l2 - mx)
    dil = (e0 * d0 + e1 * d1 + e2 * d2) / (e0 + e1 + e2)
    m2 = (dil * gz[:, NSA_W:NSA_W + DIL_W]).astype(BF16)
    y = _dot(m1, w_ref[0:NSA_W, :]) + _dot(m2, w_ref[NSA_W:NSA_W + DIL_W, :])
    gate = mod_ref[0][:, 2 * D_MODEL:3 * D_MODEL]
    o_ref[0] = x_ref[0] + gate * y


def _out_odd(x, mod3, oc, os_, ow, dils, lses, gz, w):
    b, s, d = x.shape
    ts = min(TS_PROJ, s)
    row = lambda n: pl.BlockSpec((1, ts, n), lambda i, j: (i, j, 0))
    res = [pl.BlockSpec((1, ts // DIL_CFG[g][1], DIL_CFG[g][1] * DIL_W), lambda i, j: (i, j, 0))
           for g in range(N_DIL)]
    return pl.pallas_call(
        _out_odd_kernel,
        grid=(b, s // ts),
        in_specs=[row(d), pl.BlockSpec((1, 1, 3 * d), lambda i, j: (i, 0, 0)),
                  row(NSA_W), row(NSA_W), row(NSA_W)] + res + res + [row(1024), _full(w.shape)],
        out_specs=row(d),
        out_shape=jax.ShapeDtypeStruct((b, s, d), F32),
        scratch_shapes=[pltpu.VMEM((4, 2, ts, 128), F32)],
        compiler_params=_cparams(2),
        name="outproj_odd",
    )(x, mod3, oc, os_, ow, *dils, *lses, gz, w)


def _pad_vec(v, n=D_MODEL):
    return jnp.pad(v, (0, n - v.shape[0]))


def _group_matrix(sizes, total):
    m = np.zeros((total, total), np.float32)
    off = 0
    for sz, on in sizes:
        if on:
            m[off:off + sz, off:off + sz] = 1.0
        off += sz
    return jnp.asarray(m, BF16)


def _swap_halves(w):
    h = w.shape[-1] // 2
    return jnp.concatenate([w[..., h:], w[..., :h]], axis=-1)


def _column_blocks(segments):
    blocks, cur, room = [], [], LANES
    for src, width in segments:
        while width:
            n = min(width, room)
            cur.append((src, n))
            src = None if src is None else src + n
            width -= n
            room -= n
            if room == 0:
                blocks.append(cur)
                cur, room = [], LANES
    assert not cur
    return blocks


def _pack_cols_kernel(w_ref, o_ref, *, blocks):
    _, rows, n_src = w_ref.shape
    for bi, pieces in enumerate(blocks):
        parts = []
        for src, width in pieces:
            if src is None:
                parts.append(jnp.zeros((rows, width), F32))
                continue
            a0 = src // LANES * LANES
            a1 = min(-(-(src + width) // LANES) * LANES, n_src)
            parts.append(w_ref[0, :, a0:a1][:, src - a0:src - a0 + width])
        blk = parts[0] if len(parts) == 1 else jnp.concatenate(parts, axis=1)
        o_ref[:, bi * LANES:(bi + 1) * LANES] = blk.astype(BF16)


def _pack_cols(w, layer, segments):
    blocks = _column_blocks(segments)
    _, d, n_src = w.shape
    tr = 256
    return pl.pallas_call(
        functools.partial(_pack_cols_kernel, blocks=blocks),
        grid=(d // tr,),
        in_specs=[pl.BlockSpec((1, tr, n_src), lambda i: (layer, i, 0))],
        out_specs=pl.BlockSpec((tr, len(blocks) * LANES), lambda i: (i, 0)),
        out_shape=jax.ShapeDtypeStruct((d, len(blocks) * LANES), BF16),
        compiler_params=_cparams(1),
        name="pack_weight_columns",
    )(w)


EVEN_SEGMENTS = ((0, 1536), (1536, 512), (2464, 512), (2048, 256), (2304, 128),
                 (None, 64), (2432, 32), (None, 32), (None, 64), (2448, 16), (2432, 16), (None, 32))


def _pack_even(w_in_all, layer, norm_g, sb_qn, sb_kn, qa_g, wq_up, kva_g, wkv_up, qn, kn):
    w = _pack_cols(w_in_all, layer, EVEN_SEGMENTS)
    wq3 = wq_up.reshape(MLA_Q_RANK, MLA_HEADS, MLA_NOPE + MLA_ROPE)
    zq = jnp.zeros((MLA_Q_RANK, MLA_HEADS, 32), wq_up.dtype)
    wq = jnp.concatenate([wq3, zq], axis=-1).reshape(MLA_Q_RANK, MLA_HEADS * 128).astype(BF16)
    wqs = jnp.concatenate([jnp.zeros((MLA_Q_RANK, MLA_HEADS, 64), wq_up.dtype),
                           _swap_halves(wq3[..., MLA_NOPE:]), zq], axis=-1)
    wqs = wqs.reshape(MLA_Q_RANK, MLA_HEADS * 128).astype(BF16)
    wkv3 = wkv_up.reshape(MLA_KV_RANK, MLA_HEADS, MLA_NOPE + MLA_V)
    wk = jnp.concatenate([wkv3[..., :MLA_NOPE], jnp.zeros_like(wkv3[..., :MLA_NOPE])], axis=-1)
    wkv = jnp.concatenate([wk.reshape(MLA_KV_RANK, MLA_HEADS * 128),
                           wkv3[..., MLA_NOPE:].reshape(MLA_KV_RANK, MLA_HEADS * MLA_V)], axis=1).astype(BF16)
    z32 = jnp.zeros((32,), F32)
    z64 = jnp.zeros((64,), F32)
    scale = (MLA_NOPE + MLA_ROPE) ** -0.5 * LOG2E
    qg = jnp.tile(jnp.concatenate([qn, z32]), MLA_HEADS) * scale
    qgs = jnp.tile(jnp.concatenate([z64, _swap_halves(qn[MLA_NOPE:]), z32]), MLA_HEADS) * scale
    kg = jnp.tile(jnp.concatenate([kn[:MLA_NOPE], z64]), MLA_HEADS)
    krg = jnp.concatenate([z64, kn[MLA_NOPE:], z32])
    krgs = jnp.concatenate([z64, _swap_halves(kn[MLA_NOPE:]), z32])
    cnt = jnp.tile(jnp.concatenate([jnp.full((64,), 1.0 / 64), jnp.full((32,), 1.0 / 32), jnp.ones((32,))]),
                   MLA_HEADS).astype(F32)
    rows = [norm_g, _pad_vec(jnp.tile(sb_qn, SB_HEADS) * (LOG2E * HEAD_DIM ** -0.5)), _pad_vec(jnp.tile(sb_kn, SB_HEADS)),
            _pad_vec(qa_g), _pad_vec(kva_g), qg, qgs, kg, _pad_vec(krg), _pad_vec(krgs), cnt]
    rows += [jnp.zeros((D_MODEL,), F32)] * (16 - len(rows))
    return w, wq, wqs, wkv, jnp.stack(rows).astype(F32)


def _padded_heads(start, n):
    return tuple(seg for h in range(n) for seg in ((start + h * HEAD_DIM, HEAD_DIM), (None, HEAD_DIM)))


NSA_SEGMENTS = (_padded_heads(0, NSA_HEADS)
                + tuple(seg for c in range(6) for seg in _padded_heads(768 + c * NSA_KV_W, NSA_KV_HEADS))
                + tuple(seg for g in range(NSA_KV_HEADS)
                        for seg in ((1920 + g * NSA_HPG * 3, NSA_HPG * 3), (None, LANES - NSA_HPG * 3)))
                + ((1956, NSA_W), (5028, DIL_W)))
DIL_SEGMENTS = ((2724, 3 * N_DIL * DIL_W),)


def _pack_odd(w_in_all, layer, norm_g, nsa_qn, nsa_kn, dil_qn, dil_kn):
    w_nsa = _pack_cols(w_in_all, layer, NSA_SEGMENTS)
    w_dil = _pack_cols(w_in_all, layer, DIL_SEGMENTS)
    z64 = jnp.zeros((64,), F32)
    n = NSA_HEADS * 128
    rows = [_pad_vec(norm_g, n),
            jnp.tile(jnp.concatenate([nsa_qn * (LOG2E * HEAD_DIM ** -0.5), z64]), NSA_HEADS),
            _pad_vec(jnp.tile(jnp.concatenate([nsa_kn, z64]), NSA_KV_HEADS), n),
            _pad_vec(jnp.tile(dil_qn, DIL_HEADS) * (LOG2E * HEAD_DIM ** -0.5), n),
            _pad_vec(jnp.tile(dil_kn, DIL_HEADS), n)]
    rows += [jnp.zeros((n,), F32)] * (8 - len(rows))
    return w_nsa, w_dil, jnp.stack(rows).astype(F32)


def _pack_compress(w, pe, double):
    w3 = w.reshape(CMP_LEN, HEAD_DIM, HEAD_DIM)
    w3 = jnp.concatenate([w3, jnp.zeros_like(w3)], axis=1)
    w3 = jnp.concatenate([w3, w3 if double else jnp.zeros_like(w3)], axis=2)
    wp = w3.reshape(2, CMP_STRIDE * 128, 128).astype(BF16)
    pe2 = jnp.concatenate([pe, jnp.zeros_like(pe)], axis=1).reshape(2, 1, CMP_STRIDE * 128)
    pe2 = jnp.broadcast_to(pe2, (2, 8, CMP_STRIDE * 128)).astype(F32)
    return wp, pe2


def _normed_len(gain, sizes):
    tot, off = 0.0, 0
    for n in sizes:
        tot = tot + n * jnp.max(jnp.square(gain[off:off + n]))
        off += n
    return jnp.sqrt(tot)


def _mla_logit_bound(qn, kn):
    sizes = (MLA_NOPE, MLA_ROPE)
    scale = (MLA_NOPE + MLA_ROPE) ** -0.5 * LOG2E
    return (_normed_len(qn, sizes) * _normed_len(kn, sizes) * (scale * ROUNDING_MARGIN)).reshape(1).astype(F32)


def _alibi_slopes(n):
    return 2.0 ** (-8.0 * jnp.arange(1, n + 1, dtype=jnp.float32) / n)


def kernel(x, c, positions, ada_w, ada_b, norm_g, ev_w_in, ev_w_out, sb_qn, sb_kn, mla_qa_g, mla_wq_up,
           mla_kva_g, mla_wkv_up, mla_qn, mla_kn, od_w_in, od_w_out, nsa_qn, nsa_kn, nsa_cmp_wk, nsa_cmp_wv,
           nsa_cmp_pe_k, nsa_cmp_pe_v, dil_qn, dil_kn):
    b, s, d = x.shape
    depth = ada_w.shape[0]
    pos_f = positions.astype(F32)

    inv_freq = ROPE_BASE ** (-jnp.arange(0, MLA_ROPE, 2, dtype=F32) / MLA_ROPE)
    ang = pos_f[:, None] * inv_freq[None, :]
    cos, sin = jnp.cos(ang), jnp.sin(ang)
    cos128 = jnp.concatenate([jnp.ones((s, 64), F32), cos, cos, jnp.zeros((s, 32), F32)], axis=1)
    sin128 = jnp.concatenate([jnp.zeros((s, 64), F32), -sin, sin, jnp.zeros((s, 32), F32)], axis=1)
    nsa_slopes = _alibi_slopes(NSA_HEADS)
    dil_slopes = _alibi_slopes(N_DIL * DIL_HEADS)
    nch = s // CMP_STRIDE
    chunk_sum = pos_f.reshape(nch, CMP_STRIDE).sum(axis=1)
    cpos = ((chunk_sum + jnp.roll(chunk_sum, -1)) / CMP_LEN).reshape(nch, 1)
    n_sel = s // SEL_LEN
    cst = np.arange(nch)[:, None] * CMP_STRIDE
    jst = np.arange(LANES)[None, :] * SEL_LEN
    ovl = ((cst <= jst + SEL_LEN - 1) & (cst + CMP_LEN - 1 >= jst) & (np.arange(LANES)[None, :] < n_sel))
    ovlt = jnp.asarray(ovl.astype(np.float32).T, BF16)
    pos_col = pos_f.reshape(s, 1)
    tqc = min(TQ_CMP, s)
    pos_rows = pos_f.reshape(s // tqc, 1, tqc)
    tqn = min(TQ_NSA, s)
    tkn = min(TK_NSA, tqn)
    assert tkn == TK_NSA
    pos2d = pos_f.reshape(s // tkn, 1, tkn)
    pos2d_win = pos_f.reshape(s // TK_WIN, 1, TK_WIN)
    q_first = pos_f[::tqn]
    k_last = pos_f[jnp.minimum((jnp.arange(SEL_TILE_STRIDE) + 1) * tkn - 1, s - 1)]
    group_slope = jnp.min(nsa_slopes.reshape(NSA_KV_HEADS, NSA_HPG), axis=1) * LOG2E
    sel_dead = group_slope[:, None, None] * (q_first[:, None] - k_last[None, :])[None] > F32_ZERO_EXP
    pos_sorted = jnp.all(pos_f[1:] >= pos_f[:-1])

    g64 = _group_matrix([(64, 1)] * 4, 256)
    gm = _group_matrix([(64, 1), (32, 1), (32, 0)] * 2, 256)
    tks = min(TK_SB, min(TQ_SB, s))
    tri = jnp.asarray(np.tril(np.ones((tks, tks), np.float32)), BF16)

    c8 = jnp.pad(c, ((0, 8 - b), (0, 0)))
    mod_all = _modulation(c8, ada_w, ada_b)

    for layer in range(depth):
        j = layer // 2
        mod3 = mod_all[layer, :b].reshape(b, 1, 3 * d)
        if layer % 2 == 0:
            w, wq, wqs, wkv, vec = _pack_even(ev_w_in, j, norm_g[layer], sb_qn[j], sb_kn[j], mla_qa_g[j],
                                              mla_wq_up[j], mla_kva_g[j], mla_wkv_up[j], mla_qn[j], mla_kn[j])
            sbq, sbk, sbv, gz, mq, mk, mv = _k1_even(x, mod3, vec, w, wq, wqs, wkv, g64, gm, cos128, sin128)
            sb_bound = (_normed_len(sb_qn[j], (HEAD_DIM,)) * _normed_len(sb_kn[j], (HEAD_DIM,))
                        * (HEAD_DIM ** -0.5 * LOG2E * ROUNDING_MARGIN)).reshape(1).astype(F32)
            o_sb = _sb_attention(sbq, sbk, sbv, tri, sb_bound, sb_bound[0] <= SAFE_LOGIT_BOUND)
            o_mla = _mla_attention(mq, mk, mv, _mla_logit_bound(mla_qn[j], mla_kn[j]))
            x = _out_even(x, mod3, o_sb, o_mla, gz, ev_w_out[j].astype(BF16))
        else:
            w_nsa, w_dil, vec = _pack_odd(od_w_in, j, norm_g[layer], nsa_qn[j], nsa_kn[j], dil_qn[j], dil_kn[j])
            q, ck, cv, sk, sv, wk, wv, gates, gz = _k1_nsa(x, mod3, vec, w_nsa, g64)
            dqkv = _k1_dil(x, mod3, vec, w_dil, g64)
            wck, pek = _pack_compress(nsa_cmp_wk[j], nsa_cmp_pe_k[j], False)
            wcv, pev = _pack_compress(nsa_cmp_wv[j], nsa_cmp_pe_v[j], True)
            kgain = jnp.concatenate([nsa_kn[j], jnp.zeros((64,), F32)]).reshape(1, 128)
            kc, vc = _compress(ck, cv, wck, wcv, pek, pev, kgain, g64[0:128, 0:128])
            o_c, selb, used = _nsa_cmp(nsa_slopes, q, kc, vc, ovlt, cpos, pos_rows, gates)
            flags = used[:, :, :, 0, :].reshape(b, NSA_KV_HEADS, s // tqn, tqn // min(TQ_CMP, s),
                                                SEL_TILE_STRIDE, LANES // SEL_TILE_STRIDE).max(axis=(3, 5))
            nsa_bound = (_normed_len(nsa_qn[j], (HEAD_DIM,)) * _normed_len(nsa_kn[j], (HEAD_DIM,))
                         * (HEAD_DIM ** -0.5 * LOG2E * ROUNDING_MARGIN)).reshape(1).astype(F32)
            safe = (nsa_bound[0] <= SAFE_LOGIT_BOUND) & pos_sorted
            flags = (flags > 0) & ~(safe & sel_dead[None])
            flags = flags.astype(jnp.int32).reshape(-1)
            o_s = _nsa_gqa(flags, nsa_slopes, nsa_bound, safe, q, selb, sk, sv, pos2d, pos_col, gates, 1)
            o_w = _nsa_gqa(flags, nsa_slopes, nsa_bound, safe, q, selb, wk, wv, pos2d_win, pos_col, gates, 2)
            dil_bound = (_normed_len(dil_qn[j], (HEAD_DIM,)) * _normed_len(dil_kn[j], (HEAD_DIM,))
                         * (HEAD_DIM ** -0.5 * LOG2E * ROUNDING_MARGIN)).reshape(1).astype(F32)
            dil_safe = (dil_bound[0] <= SAFE_LOGIT_BOUND) & pos_sorted
            dils, lses = [], []
            for g in range(N_DIL):
                o, lse = _dilated(dil_slopes, dil_bound, dil_safe, dqkv[3 * g], dqkv[3 * g + 1], dqkv[3 * g + 2],
                                  pos_f, g)
                dils.append(o)
                lses.append(lse)
            x = _out_odd(x, mod3, o_c, o_s, o_w, dils, lses, gz, od_w_out[j].astype(BF16))
    return x
```

```python
import functools

import numpy as np
import jax
import jax.numpy as jnp
from jax import lax
from jax.experimental import pallas as pl
from jax.experimental.pallas import tpu as pltpu

F32 = jnp.float32
BF16 = jnp.bfloat16

D_MODEL = 1024
HEAD_DIM = 64
NORM_EPS = 1e-6
TINY = 1e-30
SB_HEADS = 8
MLA_HEADS = 8
MLA_Q_RANK = 256
MLA_KV_RANK = 128
MLA_NOPE = 64
MLA_ROPE = 32
MLA_V = 64
ROPE_BASE = 10000.0
NSA_HEADS = 12
NSA_KV_HEADS = 3
NSA_HPG = 4
CMP_LEN = 32
CMP_STRIDE = 16
SEL_LEN = 64
SEL_TOPN = 16
WIN = 512
FORCE_BONUS = 1e3
DIL_CFG = ((128, 1), (512, 4), (2048, 16))
N_DIL = 3
DIL_HEADS = 4
SB_W = SB_HEADS * HEAD_DIM
MLA_OUT = MLA_HEADS * MLA_V
NSA_W = NSA_HEADS * HEAD_DIM
NSA_KV_W = NSA_KV_HEADS * HEAD_DIM
DIL_W = DIL_HEADS * HEAD_DIM

LANES = 128
NEG_MASK = -1e30
LOG2E = 1.4426950408889634
LN2 = 0.6931471805599453
F32_ZERO_EXP = 150.0
SAFE_LOGIT_BOUND = 50.0
ROUNDING_MARGIN = 1.02
SEL_OFF = 2.0 ** 30
VMEM_LIMIT = 56 * 1024 * 1024

TS_PROJ = 512
TQ_SB = 1024
TK_SB = 256
TQ_MLA = 1024
TD_MLA = 512
TQ_CMP = 512
TQ_NSA = 512
TK_NSA = 512
TK_WIN = 256
SEL_TILE_STRIDE = LANES * SEL_LEN // TK_NSA
T_DIL = 256


def _dot(a, b):
    return jnp.dot(a, b, preferred_element_type=F32)


def _dot_nt(a, b):
    return lax.dot_general(a, b, (((1,), (1,)), ((), ())), preferred_element_type=F32)


def _split_hl(a):
    hi = a.astype(BF16)
    lo = (a - hi.astype(F32)).astype(BF16)
    return hi, lo


def _dot_hl(a, b):
    hi, lo = _split_hl(a)
    return _dot(hi, b) + _dot(lo, b)


def _sigmoid(z):
    return 1.0 / (1.0 + jnp.exp(-z))


def _cparams(n_axes):
    return pltpu.CompilerParams(dimension_semantics=("arbitrary",) * n_axes,
                                vmem_limit_bytes=VMEM_LIMIT)


def _full(shape):
    n = len(shape)
    return pl.BlockSpec(shape, lambda *a, _n=n: (0,) * _n)


def _mod_kernel(c_ref, w_ref, b_ref, o_ref):
    c = c_ref[...]
    a = c * _sigmoid(c)
    ah, al = _split_hl(a)
    wh, wl = _split_hl(w_ref[0])
    o_ref[0] = _dot(ah, wh) + _dot(ah, wl) + _dot(al, wh) + b_ref[0]


def _modulation(c8, ada_w, ada_b):
    depth, d, n3 = ada_w.shape
    tn = 1024
    return pl.pallas_call(
        _mod_kernel,
        grid=(depth, n3 // tn),
        in_specs=[pl.BlockSpec((8, d), lambda l, j: (0, 0)),
                  pl.BlockSpec((1, d, tn), lambda l, j: (l, 0, j)),
                  pl.BlockSpec((1, 1, tn), lambda l, j: (l, 0, j))],
        out_specs=pl.BlockSpec((1, 8, tn), lambda l, j: (l, 0, j)),
        out_shape=jax.ShapeDtypeStruct((depth, 8, n3), F32),
        compiler_params=_cparams(2),
        name="adaln_mod",
    )(c8, ada_w, ada_b.reshape(depth, 1, n3))


def _modulated(x_ref, mod_ref, ng):
    x = x_ref[0]
    mod = mod_ref[0]
    shift = mod[:, 0:D_MODEL]
    scale = mod[:, D_MODEL:2 * D_MODEL]
    ms = jnp.mean(x * x, axis=-1, keepdims=True)
    h = x * lax.rsqrt(ms + NORM_EPS) * ng
    h = h * (1.0 + scale) + shift
    return h.astype(BF16)


def _group_sumsq(t, g):
    return _dot((t * t).astype(BF16), g)


def _group_norm_chunk(t, g, inv_cnt):
    return t * lax.rsqrt(_group_sumsq(t, g) * inv_cnt + NORM_EPS)


def _row_rms(t, gain):
    ms = jnp.mean(t * t, axis=-1, keepdims=True)
    return t * lax.rsqrt(ms + NORM_EPS) * gain


EV_COLS = 3200


def _k1_even_kernel(x_ref, mod_ref, vec_ref, w_ref, wq_ref, wqs_ref, wkv_ref, g64_ref, gm_ref,
                    cos_ref, sin_ref, sbq_o, sbk_o, sbv_o, gz_o, mq_o, mk_o, mv_o):
    hb = _modulated(x_ref, mod_ref, vec_ref[0:1, :])
    g64 = g64_ref[...]
    gm = gm_ref[...]
    inv64 = 1.0 / HEAD_DIM

    u = _dot(hb, w_ref[:, 0:1536])
    for c in range(2):
        sl = slice(c * 256, (c + 1) * 256)
        qn = _group_norm_chunk(u[:, c * 256:(c + 1) * 256], g64, inv64)
        sbq_o[0, :, sl] = (qn * vec_ref[1:2, sl]).astype(BF16)
        kn = _group_norm_chunk(u[:, 512 + c * 256:512 + (c + 1) * 256], g64, inv64)
        sbk_o[0, :, sl] = (kn * vec_ref[2:3, sl]).astype(BF16)
    sbv_o[0] = u[:, 1024:1536].astype(BF16)

    uz = _dot(hb, w_ref[:, 1536:2560])
    gz_o[0] = (uz * _sigmoid(uz)).astype(BF16)

    ul = _dot(hb, w_ref[:, 2560:3200])
    qlat = _row_rms(ul[:, 0:256], vec_ref[3:4, 0:256]).astype(BF16)
    kvlat = _row_rms(ul[:, 256:384], vec_ref[4:5, 0:128]).astype(BF16)
    krm = ul[:, 384:512]
    krs = ul[:, 512:640]
    cs = cos_ref[...]
    sn = sin_ref[...]
    cs2 = jnp.concatenate([cs, cs], axis=1)
    sn2 = jnp.concatenate([sn, sn], axis=1)

    tq = _dot(qlat, wq_ref[...])
    tqs = _dot(qlat, wqs_ref[...])
    for p in range(4):
        sl = slice(p * 256, (p + 1) * 256)
        tc = tq[:, p * 256:(p + 1) * 256]
        ss = _group_sumsq(tc, gm)
        inv = lax.rsqrt(ss * vec_ref[10:11, sl] + NORM_EPS)
        a = tc * inv * vec_ref[5:6, sl]
        b = tqs[:, p * 256:(p + 1) * 256] * inv * vec_ref[6:7, sl]
        mq_o[0, :, sl] = (a * cs2 + b * sn2).astype(BF16)

    sskr = _group_sumsq(krm, gm[0:128, 0:128])
    invr = lax.rsqrt(sskr * vec_ref[10:11, 0:128] + NORM_EPS)
    kr = (krm * invr * vec_ref[8:9, 0:128]) * cs + (krs * invr * vec_ref[9:10, 0:128]) * sn
    kr2 = jnp.concatenate([kr, kr], axis=1)

    kv = _dot(kvlat, wkv_ref[...])
    for p in range(4):
        sl = slice(p * 256, (p + 1) * 256)
        kn = _group_norm_chunk(kv[:, p * 256:(p + 1) * 256], g64, inv64)
        mk_o[0, :, sl] = (kn * vec_ref[7:8, sl] + kr2).astype(BF16)
    mv_o[0] = kv[:, 1024:1536].astype(BF16)


def _k1_even(x, mod3, vec, w, wq, wqs, wkv, g64, gm, cos128, sin128):
    b, s, d = x.shape
    ts = min(TS_PROJ, s)
    row = lambda n: pl.BlockSpec((1, ts, n), lambda i, j: (i, j, 0))
    outs = [(512, BF16), (512, BF16), (512, BF16), (1024, BF16), (1024, BF16), (1024, BF16), (512, BF16)]
    return pl.pallas_call(
        _k1_even_kernel,
        grid=(b, s // ts),
        in_specs=[row(d),
                  pl.BlockSpec((1, 1, 3 * d), lambda i, j: (i, 0, 0)),
                  _full(vec.shape), _full(w.shape), _full(wq.shape), _full(wqs.shape), _full(wkv.shape),
                  _full(g64.shape), _full(gm.shape),
                  pl.BlockSpec((ts, 128), lambda i, j: (j, 0)),
                  pl.BlockSpec((ts, 128), lambda i, j: (j, 0))],
        out_specs=[row(n) for n, _ in outs],
        out_shape=[jax.ShapeDtypeStruct((b, s, n), dt) for n, dt in outs],
        compiler_params=_cparams(2),
        name="inproj_even",
    )(x, mod3, vec, w, wq, wqs, wkv, g64, gm, cos128, sin128)


def _sb_kernel(bnd_ref, q_ref, k_ref, v_ref, tri_ref, o_ref, acc_ref, car_ref, *, tq, tk, bounded):
    qi = pl.program_id(2)
    q = q_ref[0]
    lane = lax.broadcasted_iota(jnp.int32, (1, LANES), 1)
    lo_half = lane < HEAD_DIM
    zero = jnp.zeros_like(q)
    qs = (jnp.where(lo_half, q, zero), jnp.where(lo_half, zero, q))
    tri = tri_ref[...]
    acc_ref[...] = jnp.zeros_like(acc_ref)
    car_ref[...] = jnp.zeros_like(car_ref)
    col0 = lax.broadcasted_iota(jnp.int32, (1, tk), 1)

    def tile(kt, r0, masked):
        n = tq - r0
        off = pl.multiple_of(kt * tk, tk)
        k = k_ref[0, pl.ds(off, tk), :]
        v = v_ref[0, pl.ds(off, tk), :]
        if masked:
            rows = qi * tq + r0 + lax.broadcasted_iota(jnp.int32, (n, 1), 0)
            strict = (kt * tk + col0) < rows
        for i in range(2):
            z = _dot_nt(qs[i][r0:tq], k)
            if bounded:
                lom = jnp.log(1.0 + jnp.exp2(z)) * (-LOG2E)
            else:
                lom = -jnp.maximum(z, 0.0) - jnp.log(1.0 + jnp.exp2(-jnp.abs(z))) * LOG2E
            if masked:
                lom = jnp.where(strict, lom, 0.0)
            tt = _dot(lom.astype(BF16), tri)
            car = car_ref[i, r0:tq]
            w = jnp.exp2(tt + z + jnp.concatenate([car] * (tk // LANES), axis=1))
            if masked:
                w = jnp.where(strict, w, 0.0)
            acc_ref[i, r0:tq] += _dot(w.astype(BF16), v)
            car_ref[i, r0:tq] = car + jnp.broadcast_to(tt[:, 0:1], (n, LANES))

    per = tq // tk
    for c in reversed(range(per)):
        tile(qi * per + c, c * tk, True)

    if bounded:
        limit = -(F32_ZERO_EXP + bnd_ref[0])

        def still_live():
            return jnp.max(jnp.maximum(car_ref[0], car_ref[1])) > limit

        def single(c):
            j, _ = c
            tile(qi * per - 1 - j, 0, False)
            return j + 1, still_live()

        _, live = lax.while_loop(lambda c: (c[0] < per) & c[1], single, (0, qi > 0))

        def group(c):
            g, _ = c
            for u in range(per):
                tile(qi * per - 1 - per * (g + 1) - u, 0, False)
            return g + 1, still_live()

        lax.while_loop(lambda c: (c[0] < qi - 1) & c[1], group, (0, live))
    else:
        def body(j, carry):
            for u in range(per):
                tile(qi * per - 1 - per * j - u, 0, False)
            return carry

        lax.fori_loop(0, qi, body, 0)
    o_ref[0] = jnp.where(lo_half, acc_ref[0], acc_ref[1])


def _sb_attention(q, k, v, tri, bound, safe):
    b, s, w = q.shape
    tq = min(TQ_SB, s)
    tk = min(TK_SB, tq)

    def call(bounded):
        kern = functools.partial(_sb_kernel, tq=tq, tk=tk, bounded=bounded)
        return pl.pallas_call(
            kern,
            grid=(b, w // LANES, s // tq),
            in_specs=[pl.BlockSpec(memory_space=pltpu.SMEM),
                      pl.BlockSpec((1, tq, LANES), lambda i, p, j: (i, j, p)),
                      pl.BlockSpec((1, s, LANES), lambda i, p, j: (i, 0, p)),
                      pl.BlockSpec((1, s, LANES), lambda i, p, j: (i, 0, p)),
                      _full(tri.shape)],
            out_specs=pl.BlockSpec((1, tq, LANES), lambda i, p, j: (i, j, p)),
            out_shape=jax.ShapeDtypeStruct((b, s, w), F32),
            scratch_shapes=[pltpu.VMEM((2, tq, LANES), F32), pltpu.VMEM((2, tq, LANES), F32)],
            compiler_params=_cparams(3),
            name="stickbreak_attn_bounded" if bounded else "stickbreak_attn",
        )(bound, q, k, v, tri)

    return lax.cond(safe, lambda: call(True), lambda: call(False))


def _lane_max(s):
    m = s[:, 0:LANES]
    for c in range(1, s.shape[1] // LANES):
        m = jnp.maximum(m, s[:, c * LANES:(c + 1) * LANES])
    return m


def _mla_kernel(bnd_ref, q_ref, k_ref, v_ref, o_ref, acc_ref, m_ref, *, tq, td, bounded):
    qi = pl.program_id(2)
    q = q_ref[0]
    qs = (q[:, 0:LANES], q[:, LANES:2 * LANES])
    lane = lax.broadcasted_iota(jnp.int32, (1, LANES), 1)
    lo_half = lane < HEAD_DIM
    acc_ref[...] = jnp.zeros_like(acc_ref)
    if bounded:
        m_ref[...] = jnp.full(m_ref.shape, bnd_ref[0], F32)
    else:
        m_ref[...] = jnp.full_like(m_ref, NEG_MASK)

    def scores(off, width, r0, masked):
        k = k_ref[0, pl.ds(off, width), :]
        if masked:
            rows = qi * tq + r0 + lax.broadcasted_iota(jnp.int32, (tq - r0, 1), 0)
            keep = (off + lax.broadcasted_iota(jnp.int32, (1, width), 1)) <= rows
        out = []
        for i in range(2):
            s = _dot_nt(qs[i][r0:tq], k[:, i * LANES:(i + 1) * LANES])
            if masked:
                s = jnp.where(keep, s, NEG_MASK)
            out.append(s)
        return out

    def sweep(fn):
        def body(j, carry):
            fn(pl.multiple_of(j * tq, tq), tq, 0, False)
            return carry
        lax.fori_loop(0, qi, body, 0)
        for c in range(tq // td):
            fn(pl.multiple_of(qi * tq + c * td, td), td, c * td, True)

    def row_max(off, width, r0, masked):
        for i, s in enumerate(scores(off, width, r0, masked)):
            m_ref[i, r0:tq] = jnp.maximum(m_ref[i, r0:tq], _lane_max(s))

    def accumulate(off, width, r0, masked):
        vx = jnp.concatenate([v_ref[0, pl.ds(off, width), :], jnp.ones((width, LANES), BF16)], axis=1)
        for i, s in enumerate(scores(off, width, r0, masked)):
            p = jnp.exp2(s - jnp.concatenate([m_ref[i, r0:tq]] * (width // LANES), axis=1))
            acc_ref[i, r0:tq] += _dot(p.astype(BF16), vx)

    if not bounded:
        sweep(row_max)
        for i in range(2):
            m_ref[i] = jnp.broadcast_to(jnp.max(m_ref[i], axis=-1, keepdims=True), (tq, LANES))
    sweep(accumulate)
    o_ref[0] = jnp.where(lo_half, acc_ref[0, :, 0:LANES] / acc_ref[0, :, LANES:2 * LANES],
                         acc_ref[1, :, 0:LANES] / acc_ref[1, :, LANES:2 * LANES])


def _mla_attention(q, k, v, bound):
    b, s, _ = q.shape
    tq = min(TQ_MLA, s)
    td = min(TD_MLA, tq)

    def call(bounded):
        kern = functools.partial(_mla_kernel, tq=tq, td=td, bounded=bounded)
        return pl.pallas_call(
            kern,
            grid=(b, MLA_HEADS // 2, s // tq),
            in_specs=[pl.BlockSpec(memory_space=pltpu.SMEM),
                      pl.BlockSpec((1, tq, 2 * LANES), lambda i, p, j: (i, j, p)),
                      pl.BlockSpec((1, s, 2 * LANES), lambda i, p, j: (i, 0, p)),
                      pl.BlockSpec((1, s, LANES), lambda i, p, j: (i, 0, p))],
            out_specs=pl.BlockSpec((1, tq, LANES), lambda i, p, j: (i, j, p)),
            out_shape=jax.ShapeDtypeStruct((b, s, MLA_OUT), F32),
            scratch_shapes=[pltpu.VMEM((2, tq, 2 * LANES), F32), pltpu.VMEM((2, tq, LANES), F32)],
            compiler_params=_cparams(3),
            name="mla_attn_bounded" if bounded else "mla_attn",
        )(bound, q, k, v)

    return lax.cond(bound[0] <= SAFE_LOGIT_BOUND, lambda: call(True), lambda: call(False))


def _out_even_kernel(x_ref, mod_ref, osb_ref, omla_ref, gz_ref, w_ref, o_ref):
    gz = gz_ref[0].astype(F32)
    m1 = (osb_ref[0] * gz[:, 0:SB_W]).astype(BF16)
    m2 = (omla_ref[0] * gz[:, SB_W:SB_W + MLA_OUT]).astype(BF16)
    y = _dot(m1, w_ref[0:SB_W, :]) + _dot(m2, w_ref[SB_W:SB_W + MLA_OUT, :])
    gate = mod_ref[0][:, 2 * D_MODEL:3 * D_MODEL]
    o_ref[0] = x_ref[0] + gate * y


def _out_even(x, mod3, osb, omla, gz, w):
    b, s, d = x.shape
    ts = min(TS_PROJ, s)
    row = lambda n: pl.BlockSpec((1, ts, n), lambda i, j: (i, j, 0))
    return pl.pallas_call(
        _out_even_kernel,
        grid=(b, s // ts),
        in_specs=[row(d), pl.BlockSpec((1, 1, 3 * d), lambda i, j: (i, 0, 0)),
                  row(SB_W), row(MLA_OUT), row(SB_W + MLA_OUT), _full(w.shape)],
        out_specs=row(d),
        out_shape=jax.ShapeDtypeStruct((b, s, d), F32),
        compiler_params=_cparams(2),
        name="outproj_even",
    )(x, mod3, osb, omla, gz, w)


NSA_COLS = 5248


def _k1_nsa_kernel(x_ref, mod_ref, vec_ref, w_ref, g64_ref,
                   q_o, ck_o, cv_o, sk_o, sv_o, wk_o, wv_o, gt_o, gz_o, chunk_ref):
    ts = x_ref.shape[1]
    hb = _modulated(x_ref, mod_ref, vec_ref[0:1, 0:D_MODEL])
    g64 = g64_ref[...]
    inv64 = 1.0 / HEAD_DIM

    uq = _dot(hb, w_ref[:, 0:1536])
    for c in range(6):
        sl = slice(c * 256, (c + 1) * 256)
        qn = _group_norm_chunk(uq[:, c * 256:(c + 1) * 256], g64, inv64)
        q_o[0, :, sl] = (qn * vec_ref[1:2, sl]).astype(BF16)

    uc = _dot(hb, w_ref[:, 1536:2304])
    for c in range(2 * NSA_KV_HEADS):
        chunk_ref[c] = uc[:, c * 128:(c + 1) * 128]
    for g in range(NSA_KV_HEADS):
        for tok in range(CMP_STRIDE):
            rows = pl.ds(tok, ts // CMP_STRIDE, stride=CMP_STRIDE)
            sl = slice(tok * 128, (tok + 1) * 128)
            ck_o[0, g, :, sl] = chunk_ref[g, rows, :].astype(BF16)
            cv_o[0, g, :, sl] = chunk_ref[NSA_KV_HEADS + g, rows, :].astype(BF16)

    us = _dot(hb, w_ref[:, 2304:3840])
    kgain = vec_ref[2:3, 0:384]
    g128 = g64[0:128, 0:128]
    for g in range(NSA_KV_HEADS):
        sl = slice(g * 128, (g + 1) * 128)
        t = us[:, g * 128:(g + 1) * 128]
        ss = _group_sumsq(t, g128)
        sk_o[0, :, sl] = (t * lax.rsqrt(ss * inv64 + NORM_EPS) * kgain[:, sl]).astype(BF16)
        t = us[:, 768 + g * 128:768 + (g + 1) * 128]
        ss = _group_sumsq(t, g128)
        wk_o[0, :, sl] = (t * lax.rsqrt(ss * inv64 + NORM_EPS) * kgain[:, sl]).astype(BF16)
    ones_hi = (lax.broadcasted_iota(jnp.int32, (1, 384), 1) % 128) >= HEAD_DIM
    sv_o[0] = jnp.where(ones_hi, 1.0, us[:, 384:768]).astype(BF16)
    wv_o[0] = jnp.where(ones_hi, 1.0, us[:, 1152:1536]).astype(BF16)

    ug = _dot(hb, w_ref[:, 3840:4224])
    gt_o[0] = _sigmoid(ug)
    uz = _dot(hb, w_ref[:, 4224:5248])
    gz_o[0] = (uz * _sigmoid(uz)).astype(BF16)


def _k1_nsa(x, mod3, vec, w, g64):
    b, s, d = x.shape
    ts = min(TS_PROJ, s)
    row = lambda n: pl.BlockSpec((1, ts, n), lambda i, j: (i, j, 0))
    cw = CMP_STRIDE * 128
    grp = pl.BlockSpec((1, NSA_KV_HEADS, ts // CMP_STRIDE, cw), lambda i, j: (i, 0, j, 0))
    return pl.pallas_call(
        _k1_nsa_kernel,
        grid=(b, s // ts),
        in_specs=[row(d), pl.BlockSpec((1, 1, 3 * d), lambda i, j: (i, 0, 0)),
                  _full(vec.shape), _full(w.shape), _full(g64.shape)],
        out_specs=[row(1536), grp, grp, row(384), row(384), row(384), row(384), row(384), row(1024)],
        out_shape=[jax.ShapeDtypeStruct((b, s, 1536), BF16),
                   jax.ShapeDtypeStruct((b, NSA_KV_HEADS, s // CMP_STRIDE, cw), BF16),
                   jax.ShapeDtypeStruct((b, NSA_KV_HEADS, s // CMP_STRIDE, cw), BF16),
                   jax.ShapeDtypeStruct((b, s, 384), BF16),
                   jax.ShapeDtypeStruct((b, s, 384), BF16),
                   jax.ShapeDtypeStruct((b, s, 384), BF16),
                   jax.ShapeDtypeStruct((b, s, 384), BF16),
                   jax.ShapeDtypeStruct((b, s, 384), F32),
                   jax.ShapeDtypeStruct((b, s, 1024), BF16)],
        scratch_shapes=[pltpu.VMEM((2 * NSA_KV_HEADS, ts, 128), F32)],
        compiler_params=_cparams(2),
        name="inproj_nsa",
    )(x, mod3, vec, w, g64)


def _k1_dil_kernel(x_ref, mod_ref, vec_ref, w_ref, g64_ref, *refs):
    outs, st = refs[:3 * N_DIL], refs[3 * N_DIL]
    ts = x_ref.shape[1]
    hb = _modulated(x_ref, mod_ref, vec_ref[0:1, 0:D_MODEL])
    g64 = g64_ref[...]
    inv64 = 1.0 / HEAD_DIM
    u = _dot(hb, w_ref[...])
    for g in range(N_DIL):
        dil = DIL_CFG[g][1]
        qn = _group_norm_chunk(u[:, g * 256:(g + 1) * 256], g64, inv64) * vec_ref[3:4, 0:256]
        kn = _group_norm_chunk(u[:, 768 + g * 256:768 + (g + 1) * 256], g64, inv64) * vec_ref[4:5, 0:256]
        vals = (qn, kn, u[:, 1536 + g * 256:1536 + (g + 1) * 256])
        for j, val in enumerate(vals):
            o_ref = outs[3 * g + j]
            if dil == 1:
                o_ref[0] = val.astype(BF16)
                continue
            for h in range(2):
                st[j, h] = val[:, h * 128:(h + 1) * 128]
            for r in range(dil):
                for h in range(2):
                    sl = slice(r * 256 + h * 128, r * 256 + (h + 1) * 128)
                    o_ref[0, :, sl] = st[j, h, pl.ds(r, ts // dil, stride=dil), :].astype(BF16)


def _k1_dil(x, mod3, vec, w, g64):
    b, s, d = x.shape
    ts = min(TS_PROJ, s)
    row = lambda n: pl.BlockSpec((1, ts, n), lambda i, j: (i, j, 0))
    specs, shapes = [], []
    for g in range(N_DIL):
        dil = DIL_CFG[g][1]
        for _ in range(3):
            specs.append(pl.BlockSpec((1, ts // dil, dil * 256), lambda i, j: (i, j, 0)))
            shapes.append(jax.ShapeDtypeStruct((b, s // dil, dil * 256), BF16))
    return pl.pallas_call(
        _k1_dil_kernel,
        grid=(b, s // ts),
        in_specs=[row(d), pl.BlockSpec((1, 1, 3 * d), lambda i, j: (i, 0, 0)),
                  _full(vec.shape), _full(w.shape), _full(g64.shape)],
        out_specs=specs,
        out_shape=shapes,
        scratch_shapes=[pltpu.VMEM((3, 2, ts, 128), F32)],
        compiler_params=_cparams(2),
        name="inproj_dil",
    )(x, mod3, vec, w, g64)


def _compress_kernel(xk_ref, xv_ref, wk_ref, wv_ref, pek_ref, pev_ref, gain_ref, g128_ref, kc_o, vc_o):
    nch = xk_ref.shape[2]

    def comp(x_ref, w_ref, pe_ref):
        x = x_ref[0, 0]
        lo = _dot(x, w_ref[0])
        hi = _dot(x, w_ref[1])
        pec = _dot_hl(pe_ref[0], w_ref[0]) + _dot_hl(pe_ref[1], w_ref[1])
        return lo + pltpu.roll(hi, nch - 1, 0) + pec[0:1, :]

    kc = comp(xk_ref, wk_ref, pek_ref)
    ss = _dot_hl(kc * kc, g128_ref[...])
    kc_o[0, 0] = (kc * lax.rsqrt(ss * (1.0 / HEAD_DIM) + NORM_EPS) * gain_ref[...]).astype(BF16)
    vc_o[0, 0] = comp(xv_ref, wv_ref, pev_ref).T.astype(BF16)


def _compress(ck, cv, wk, wv, pek, pev, gain, g128):
    b, g, nch, _ = ck.shape
    xk, xv = ck, cv
    blk = pl.BlockSpec((1, 1, nch, CMP_STRIDE * 128), lambda i, j: (i, j, 0, 0))
    oblk = pl.BlockSpec((1, 1, nch, 128), lambda i, j: (i, j, 0, 0))
    tblk = pl.BlockSpec((1, 1, 128, nch), lambda i, j: (i, j, 0, 0))
    return pl.pallas_call(
        _compress_kernel,
        grid=(b, g),
        in_specs=[blk, blk, _full(wk.shape), _full(wv.shape), _full(pek.shape), _full(pev.shape),
                  _full(gain.shape), _full(g128.shape)],
        out_specs=[oblk, tblk],
        out_shape=[jax.ShapeDtypeStruct((b, g, nch, 128), BF16), jax.ShapeDtypeStruct((b, g, 128, nch), BF16)],
        compiler_params=_cparams(2),
        name="nsa_compress",
    )(xk, xv, wk, wv, pek, pev, gain, g128)


def _pair_select(lo_half, a, b):
    return jnp.where(lo_half, a, b)


def _nsa_cmp_kernel(slope_ref, q_ref, kc_ref, vct_ref, ovlt_ref, cposc_ref, prow_ref, gt_ref,
                    oc_o, selb_o, any_o, imp_ref, *, tq):
    g = pl.program_id(1)
    qi = pl.program_id(2)
    q4 = q_ref[0]
    nch = kc_ref.shape[2]
    t = qi * tq + lax.broadcasted_iota(jnp.int32, (1, tq), 1)
    gtt = gt_ref[0].T
    row_lo = lax.broadcasted_iota(jnp.int32, (LANES, 1), 0) < HEAD_DIM
    pref = prow_ref[qi][:, 0:1]

    def attend(nblk):
        kc = kc_ref[0, 0, 0:nblk, :]
        vct = vct_ref[0, 0, :, 0:nblk]
        n_id = lax.broadcasted_iota(jnp.int32, (nblk, 1), 0)
        valid = (n_id * CMP_STRIDE + (CMP_LEN - 1)) <= t
        rel = (cposc_ref[0:nblk, :] - pref) * LOG2E
        psum = jnp.zeros((nblk, tq), F32)
        outs = []
        for hh in range(NSA_HPG):
            s = _dot_nt(kc, q4[:, hh * LANES:(hh + 1) * LANES]) + slope_ref[g * NSA_HPG + hh] * rel
            s = jnp.where(valid, s, -jnp.inf)
            mx = jnp.max(s, axis=0, keepdims=True)
            mx = jnp.where(mx == -jnp.inf, 0.0, mx)
            e = jnp.exp2(s - mx)
            den = jnp.maximum(jnp.sum(e, axis=0, keepdims=True), TINY)
            p = e * (1.0 / den)
            psum = psum + p
            outs.append(_dot(vct, p.astype(BF16)) * gtt[3 * hh:3 * hh + 1, :])
        oc_o[0, :, 0:LANES] = jnp.where(row_lo, outs[0], outs[1]).T
        oc_o[0, :, LANES:2 * LANES] = jnp.where(row_lo, outs[2], outs[3]).T
        hi, lo = _split_hl(psum)
        imp_ref[...] = _dot(ovlt_ref[:, 0:nblk], hi) + _dot(ovlt_ref[:, 0:nblk], lo)

    n_valid = ((qi + 1) * tq - CMP_LEN) // CMP_STRIDE + 1
    n_groups = nch // LANES
    need = jnp.clip((n_valid + LANES - 1) // LANES, 1, n_groups)
    for c in range(1, n_groups + 1):
        @pl.when(need == c)
        def _(c=c):
            attend(c * LANES)

    imp = imp_ref[...]
    blk = lax.broadcasted_iota(jnp.int32, (LANES, 1), 0)
    cur = t >> 6
    forced = (blk == 0) | (blk == cur) | (blk == cur - 1)
    allowed = blk <= cur
    assert NSA_HPG < FORCE_BONUS
    score = jnp.where(allowed & ~forced, imp, -jnp.inf)
    blk_f = blk.astype(F32)

    def pick(_, sc):
        mx = jnp.max(sc, axis=0, keepdims=True)
        idx = jnp.min(jnp.where(sc == mx, blk_f, float(LANES)), axis=0, keepdims=True)
        return jnp.where(blk_f == idx, -jnp.inf, sc)

    left = lax.fori_loop(0, SEL_TOPN - 3, pick, score)
    chosen = jnp.where((left == -jnp.inf) & allowed, 1.0, 0.0).T
    selb_o[0, 0] = ((chosen - 1.0) * SEL_OFF).astype(BF16)
    used = jnp.max(chosen, axis=0, keepdims=True)
    any_o[0, 0, 0] = jnp.broadcast_to(used, (8, LANES))


def _nsa_cmp(slopes, q, kc, vct, ovlt, cposc, pos_rows, gates):
    b, s, _ = q.shape
    nch = kc.shape[2]
    tq = pos_rows.shape[2]
    kern = functools.partial(_nsa_cmp_kernel, tq=tq)
    return pl.pallas_call(
        kern,
        grid=(b, NSA_KV_HEADS, s // tq),
        in_specs=[pl.BlockSpec(memory_space=pltpu.SMEM),
                  pl.BlockSpec((1, tq, 4 * LANES), lambda i, g, j: (i, j, g)),
                  pl.BlockSpec((1, 1, nch, LANES), lambda i, g, j: (i, g, 0, 0)),
                  pl.BlockSpec((1, 1, LANES, nch), lambda i, g, j: (i, g, 0, 0)),
                  _full(ovlt.shape), _full(cposc.shape), _full(pos_rows.shape),
                  pl.BlockSpec((1, tq, LANES), lambda i, g, j: (i, j, g))],
        out_specs=[pl.BlockSpec((1, tq, 2 * LANES), lambda i, g, j: (i, j, g)),
                   pl.BlockSpec((1, 1, tq, LANES), lambda i, g, j: (i, g, j, 0)),
                   pl.BlockSpec((1, 1, 1, 8, LANES), lambda i, g, j: (i, g, j, 0, 0))],
        out_shape=[jax.ShapeDtypeStruct((b, s, NSA_W), F32),
                   jax.ShapeDtypeStruct((b, NSA_KV_HEADS, s, LANES), BF16),
                   jax.ShapeDtypeStruct((b, NSA_KV_HEADS, s // tq, 8, LANES), F32)],
        scratch_shapes=[pltpu.VMEM((LANES, tq), F32)],
        compiler_params=_cparams(3),
        name="nsa_cmp_topk",
    )(slopes, q, kc, vct, ovlt, cposc, pos_rows, gates)


def _gqa_kernel(flag_ref, slope_ref, bnd_ref, q_ref, selb_ref, k_ref, v_ref, pos_ref, pcol_ref, gt_ref, o_ref,
                qa_ref, acc_ref, m_ref, *, tq, tk, branch, bounded):
    bi = pl.program_id(0)
    g = pl.program_id(1)
    qi = pl.program_id(2)
    q4 = q_ref[0]
    lane = lax.broadcasted_iota(jnp.int32, (1, LANES), 1)
    lo_half = lane < HEAD_DIM
    pref = pos_ref[qi * (tq // tk)][:, 0:1]
    for hh in range(NSA_HPG):
        sl = slice(hh * tq, (hh + 1) * tq)
        qa_ref[sl, 0:LANES] = q4[:, hh * LANES:(hh + 1) * LANES]
        if branch == 1:
            qa_ref[sl, LANES:2 * LANES] = selb_ref[0, 0]
        if bounded:
            own = (pcol_ref[...] - pref) * (LOG2E * slope_ref[g * NSA_HPG + hh]) + bnd_ref[0]
            m_ref[sl] = jnp.broadcast_to(own, (tq, LANES))
    acc_ref[...] = jnp.zeros_like(acc_ref)
    if not bounded:
        m_ref[...] = jnp.full_like(m_ref, NEG_MASK)
    rows = qi * tq + lax.broadcasted_iota(jnp.int32, (tq, 1), 0)
    col0 = lax.broadcasted_iota(jnp.int32, (1, tk), 1)
    krow = lax.broadcasted_iota(jnp.int32, (tk, 1), 0)
    per = tq // tk
    nrep = tk // LANES
    flag0 = ((bi * NSA_KV_HEADS + g) * pl.num_programs(2) + qi) * SEL_TILE_STRIDE

    def scores(kt_true, masked, r_lo, r_hi):
        kt = jnp.maximum(kt_true, 0)
        off = pl.multiple_of(kt * tk, tk)
        k = k_ref[0, pl.ds(off, tk), :]
        if branch == 1:
            blk = (kt * tk + krow) >> 6
            onehot = jnp.where(lane == blk, 1.0, 0.0).astype(BF16)
            s_all = _dot_nt(qa_ref[...], jnp.concatenate([k, onehot], axis=1))
            parts = [s_all[hh * tq:(hh + 1) * tq] for hh in range(NSA_HPG)]
        else:
            parts = [_dot_nt(qa_ref[hh * tq + r_lo:hh * tq + r_hi, 0:LANES], k) for hh in range(NSA_HPG)]
        rel = (pos_ref[kt] - pref) * LOG2E
        if masked:
            cols = kt_true * tk + col0
            d = rows[r_lo:r_hi] - cols
            keep = d >= 0
            if branch == 2:
                keep = keep & (d < WIN) & (cols >= 0)
        out = []
        for hh in range(NSA_HPG):
            s = parts[hh] + slope_ref[g * NSA_HPG + hh] * rel
            if masked:
                s = jnp.where(keep, s, NEG_MASK)
            out.append(s)
        return out

    def sweep(fn):
        if branch == 1:
            def body(j, carry):
                @pl.when(flag_ref[flag0 + j] != 0)
                def _():
                    fn(j, False, 0, tq)
                return carry
            lax.fori_loop(0, qi * per, body, 0)
            for dd in range(per):
                fn(qi * per + dd, True, 0, tq)
        else:
            for c in range(-(WIN // tk), per):
                r_lo = max(0, c * tk)
                r_hi = min(tq, -(-(c * tk + tk + WIN - 1) // 8) * 8)
                fn(qi * per + c, True, r_lo, r_hi)

    def row_max(kt_true, masked, r_lo, r_hi):
        for hh, s in enumerate(scores(kt_true, masked, r_lo, r_hi)):
            sl = slice(hh * tq + r_lo, hh * tq + r_hi)
            m_ref[sl] = jnp.maximum(m_ref[sl], _lane_max(s))

    def accumulate(kt_true, masked, r_lo, r_hi):
        off = pl.multiple_of(jnp.maximum(kt_true, 0) * tk, tk)
        v = v_ref[0, pl.ds(off, tk), :]
        for hh, s in enumerate(scores(kt_true, masked, r_lo, r_hi)):
            sl = slice(hh * tq + r_lo, hh * tq + r_hi)
            p = jnp.exp2(s - jnp.concatenate([m_ref[sl]] * nrep, axis=1))
            acc_ref[sl] += _dot(p.astype(BF16), v)

    if not bounded:
        sweep(row_max)
        for hh in range(NSA_HPG):
            sl = slice(hh * tq, (hh + 1) * tq)
            m_ref[sl] = jnp.broadcast_to(jnp.max(m_ref[sl], axis=-1, keepdims=True), (tq, LANES))
    sweep(accumulate)

    gt = gt_ref[0]
    for pr in range(2):
        res = []
        for x in range(2):
            hh = 2 * pr + x
            a = acc_ref[hh * tq:(hh + 1) * tq]
            r = pltpu.roll(a, HEAD_DIM, 1)
            o = a / r if x == 0 else r / a
            res.append(o * gt[:, 3 * hh + branch:3 * hh + branch + 1])
        o_ref[0, :, pr * LANES:(pr + 1) * LANES] = _pair_select(lo_half, res[0], res[1])


def _nsa_gqa(flags, slopes, bound, safe, q, selb, k, v, pos2d, pos_col, gates, branch):
    b, s, _ = q.shape
    tq = min(TQ_NSA, s)
    tk = pos2d.shape[2]

    def call(bounded):
        kern = functools.partial(_gqa_kernel, tq=tq, tk=tk, branch=branch, bounded=bounded)
        smem = pl.BlockSpec(memory_space=pltpu.SMEM)
        name = ("nsa_sel_attn" if branch == 1 else "nsa_win_attn") + ("_bounded" if bounded else "")
        return pl.pallas_call(
            kern,
            grid=(b, NSA_KV_HEADS, s // tq),
            in_specs=[smem, smem, smem,
                      pl.BlockSpec((1, tq, 4 * LANES), lambda i, g, j: (i, j, g)),
                      pl.BlockSpec((1, 1, tq, LANES), lambda i, g, j: (i, g, j, 0)),
                      pl.BlockSpec((1, s, LANES), lambda i, g, j: (i, 0, g)),
                      pl.BlockSpec((1, s, LANES), lambda i, g, j: (i, 0, g)),
                      _full(pos2d.shape),
                      pl.BlockSpec((tq, 1), lambda i, g, j: (j, 0)),
                      pl.BlockSpec((1, tq, LANES), lambda i, g, j: (i, j, g))],
            out_specs=pl.BlockSpec((1, tq, 2 * LANES), lambda i, g, j: (i, j, g)),
            out_shape=jax.ShapeDtypeStruct((b, s, NSA_W), F32),
            scratch_shapes=[pltpu.VMEM((NSA_HPG * tq, 2 * LANES), BF16),
                            pltpu.VMEM((NSA_HPG * tq, LANES), F32),
                            pltpu.VMEM((NSA_HPG * tq, LANES), F32)],
            compiler_params=_cparams(3),
            name=name,
        )(flags, slopes, bound, q, selb, k, v, pos2d, pos_col, gates)

    return lax.cond(safe, lambda: call(True), lambda: call(False))


def _dil_kernel(slope_ref, bnd_ref, q_ref, k_ref, v_ref, pc_ref, pr_ref, o_ref, lse_ref,
                *, t, span, grp, bounded):
    i = pl.program_id(2)
    q4 = q_ref[0]
    lane = lax.broadcasted_iota(jnp.int32, (1, LANES), 1)
    lo_half = lane < HEAD_DIM
    per = t // LANES
    prev = jnp.maximum(i * per - 1, 0)
    o_prev = pl.multiple_of(prev * LANES, LANES)
    o_cur = pl.multiple_of(i * t, t)
    kk = jnp.concatenate([k_ref[0, pl.ds(o_prev, LANES), :], k_ref[0, pl.ds(o_cur, t), :]], axis=0)
    vv = jnp.concatenate([v_ref[0, pl.ds(o_prev, LANES), :], v_ref[0, pl.ds(o_cur, t), :]], axis=0)
    pk = jnp.concatenate([pr_ref[0, prev]] + [pr_ref[0, i * per + c] for c in range(per)], axis=1)
    dist = (pc_ref[0] - pk) * LOG2E
    rows = i * t + lax.broadcasted_iota(jnp.int32, (t, 1), 0)
    cols = i * t - LANES + lax.broadcasted_iota(jnp.int32, (1, t + LANES), 1)
    d = rows - cols
    valid = (cols >= 0) & (d >= 0) & (d <= span)
    zero = jnp.zeros((t, LANES), BF16)
    ones = jnp.ones((t + LANES, LANES), BF16)
    for p in range(2):
        qp = q4[:, p * LANES:(p + 1) * LANES]
        kp = kk[:, p * LANES:(p + 1) * LANES]
        vx = jnp.concatenate([vv[:, p * LANES:(p + 1) * LANES], ones], axis=1)
        res = []
        for x in range(2):
            qx = jnp.where(lo_half, qp, zero) if x == 0 else jnp.where(lo_half, zero, qp)
            s = _dot_nt(qx, kp) - slope_ref[grp * DIL_HEADS + 2 * p + x] * dist
            s = jnp.where(valid, s, -jnp.inf)
            mx = bnd_ref[0] if bounded else jnp.max(s, axis=-1, keepdims=True)
            acc = _dot(jnp.exp2(s - mx).astype(BF16), vx)
            den = acc[:, LANES:2 * LANES]
            res.append((acc[:, 0:LANES] / den, mx * LN2 + jnp.log(den)))
        sl = slice(p * LANES, (p + 1) * LANES)
        o_ref[0, :, sl] = _pair_select(lo_half, res[0][0], res[1][0])
        lse_ref[0, :, sl] = jnp.where(lo_half, res[0][1], res[1][1])


def _dilated(slopes, bound, safe, qv, kv, vv, pos_f, grp):
    window, dil = DIL_CFG[grp]
    b, sub, wd = qv.shape
    w = wd // dil
    t = min(T_DIL, sub)
    span = window // dil
    assert span <= LANES and t % LANES == 0
    pres = pos_f.reshape(sub, dil).T
    pc = pres.reshape(dil, sub, 1)
    pr = pres.reshape(dil, sub // LANES, 1, LANES)
    shp = jax.ShapeDtypeStruct((b, sub, wd), F32)
    smem = pl.BlockSpec(memory_space=pltpu.SMEM)

    def call(bounded):
        kern = functools.partial(_dil_kernel, t=t, span=span, grp=grp, bounded=bounded)
        return pl.pallas_call(
            kern,
            grid=(b, dil, sub // t),
            in_specs=[smem, smem,
                      pl.BlockSpec((1, t, w), lambda i, r, j: (i, j, r)),
                      pl.BlockSpec((1, sub, w), lambda i, r, j: (i, 0, r)),
                      pl.BlockSpec((1, sub, w), lambda i, r, j: (i, 0, r)),
                      pl.BlockSpec((1, t, 1), lambda i, r, j: (r, j, 0)),
                      pl.BlockSpec((1, sub // LANES, 1, LANES), lambda i, r, j: (r, 0, 0, 0))],
            out_specs=[pl.BlockSpec((1, t, w), lambda i, r, j: (i, j, r)),
                       pl.BlockSpec((1, t, w), lambda i, r, j: (i, j, r))],
            out_shape=[shp, shp],
            compiler_params=_cparams(3),
            name="dilated_attn_g%d%s" % (grp, "_bounded" if bounded else ""),
        )(slopes, bound, qv, kv, vv, pc, pr)

    return lax.cond(safe, lambda: call(True), lambda: call(False))


def _out_odd_kernel(x_ref, mod_ref, oc_ref, os_ref, ow_ref, d0_ref, d1_ref, d2_ref,
                    l0_ref, l1_ref, l2_ref, gz_ref, w_ref, o_ref, st):
    ts = x_ref.shape[1]
    gz = gz_ref[0].astype(F32)
    nsa = oc_ref[0] + os_ref[0] + ow_ref[0]
    m1 = (nsa * gz[:, 0:NSA_W]).astype(BF16)

    def token_order(ref, g, slot):
        dil = DIL_CFG[g][1]
        if dil == 1:
            return ref[0]
        for r in range(dil):
            for h in range(2):
                st[slot, h, pl.ds(r, ts // dil, stride=dil), :] = ref[0, :, r * DIL_W + h * 128:r * DIL_W + (h + 1) * 128]
        return jnp.concatenate([st[slot, 0], st[slot, 1]], axis=1)

    d0, d1, d2 = token_order(d0_ref, 0, 0), token_order(d1_ref, 1, 0), token_order(d2_ref, 2, 1)
    l0, l1, l2 = token_order(l0_ref, 0, 2), token_order(l1_ref, 1, 2), token_order(l2_ref, 2, 3)
    mx = jnp.maximum(jnp.maximum(l0, l1), l2)
    e0, e1, e2 = jnp.exp(l0 - mx), jnp.exp(l1 - mx), jnp.exp(l2 - mx)
    dil = (e0 * d0 + e1 * d1 + e2 * d2) / (e0 + e1 + e2)
    m2 = (dil * gz[:, NSA_W:NSA_W + DIL_W]).astype(BF16)
    y = _dot(m1, w_ref[0:NSA_W, :]) + _dot(m2, w_ref[NSA_W:NSA_W + DIL_W, :])
    gate = mod_ref[0][:, 2 * D_MODEL:3 * D_MODEL]
    o_ref[0] = x_ref[0] + gate * y


def _out_odd(x, mod3, oc, os_, ow, dils, lses, gz, w):
    b, s, d = x.shape
    ts = min(TS_PROJ, s)
    row = lambda n: pl.BlockSpec((1, ts, n), lambda i, j: (i, j, 0))
    res = [pl.BlockSpec((1, ts // DIL_CFG[g][1], DIL_CFG[g][1] * DIL_W), lambda i, j: (i, j, 0))
           for g in range(N_DIL)]
    return pl.pallas_call(
        _out_odd_kernel,
        grid=(b, s // ts),
        in_specs=[row(d), pl.BlockSpec((1, 1, 3 * d), lambda i, j: (i, 0, 0)),
                  row(NSA_W), row(NSA_W), row(NSA_W)] + res + res + [row(1024), _full(w.shape)],
        out_specs=row(d),
        out_shape=jax.ShapeDtypeStruct((b, s, d), F32),
        scratch_shapes=[pltpu.VMEM((4, 2, ts, 128), F32)],
        compiler_params=_cparams(2),
        name="outproj_odd",
    )(x, mod3, oc, os_, ow, *dils, *lses, gz, w)


def _pad_vec(v, n=D_MODEL):
    return jnp.pad(v, (0, n - v.shape[0]))


def _group_matrix(sizes, total):
    m = np.zeros((total, total), np.float32)
    off = 0
    for sz, on in sizes:
        if on:
            m[off:off + sz, off:off + sz] = 1.0
        off += sz
    return jnp.asarray(m, BF16)


def _swap_halves(w):
    h = w.shape[-1] // 2
    return jnp.concatenate([w[..., h:], w[..., :h]], axis=-1)


def _column_blocks(segments):
    blocks, cur, room = [], [], LANES
    for src, width in segments:
        while width:
            n = min(width, room)
            cur.append((src, n))
            src = None if src is None else src + n
            width -= n
            room -= n
            if room == 0:
                blocks.append(cur)
                cur, room = [], LANES
    assert not cur
    return blocks


def _pack_cols_kernel(w_ref, o_ref, *, blocks):
    _, rows, n_src = w_ref.shape
    for bi, pieces in enumerate(blocks):
        parts = []
        for src, width in pieces:
            if src is None:
                parts.append(jnp.zeros((rows, width), F32))
                continue
            a0 = src // LANES * LANES
            a1 = min(-(-(src + width) // LANES) * LANES, n_src)
            parts.append(w_ref[0, :, a0:a1][:, src - a0:src - a0 + width])
        blk = parts[0] if len(parts) == 1 else jnp.concatenate(parts, axis=1)
        o_ref[:, bi * LANES:(bi + 1) * LANES] = blk.astype(BF16)


def _pack_cols(w, layer, segments):
    blocks = _column_blocks(segments)
    _, d, n_src = w.shape
    tr = 256
    return pl.pallas_call(
        functools.partial(_pack_cols_kernel, blocks=blocks),
        grid=(d // tr,),
        in_specs=[pl.BlockSpec((1, tr, n_src), lambda i: (layer, i, 0))],
        out_specs=pl.BlockSpec((tr, len(blocks) * LANES), lambda i: (i, 0)),
        out_shape=jax.ShapeDtypeStruct((d, len(blocks) * LANES), BF16),
        compiler_params=_cparams(1),
        name="pack_weight_columns",
    )(w)


EVEN_SEGMENTS = ((0, 1536), (1536, 512), (2464, 512), (2048, 256), (2304, 128),
                 (None, 64), (2432, 32), (None, 32), (None, 64), (2448, 16), (2432, 16), (None, 32))


def _pack_even(w_in_all, layer, norm_g, sb_qn, sb_kn, qa_g, wq_up, kva_g, wkv_up, qn, kn):
    w = _pack_cols(w_in_all, layer, EVEN_SEGMENTS)
    wq3 = wq_up.reshape(MLA_Q_RANK, MLA_HEADS, MLA_NOPE + MLA_ROPE)
    zq = jnp.zeros((MLA_Q_RANK, MLA_HEADS, 32), wq_up.dtype)
    wq = jnp.concatenate([wq3, zq], axis=-1).reshape(MLA_Q_RANK, MLA_HEADS * 128).astype(BF16)
    wqs = jnp.concatenate([jnp.zeros((MLA_Q_RANK, MLA_HEADS, 64), wq_up.dtype),
                           _swap_halves(wq3[..., MLA_NOPE:]), zq], axis=-1)
    wqs = wqs.reshape(MLA_Q_RANK, MLA_HEADS * 128).astype(BF16)
    wkv3 = wkv_up.reshape(MLA_KV_RANK, MLA_HEADS, MLA_NOPE + MLA_V)
    wk = jnp.concatenate([wkv3[..., :MLA_NOPE], jnp.zeros_like(wkv3[..., :MLA_NOPE])], axis=-1)
    wkv = jnp.concatenate([wk.reshape(MLA_KV_RANK, MLA_HEADS * 128),
                           wkv3[..., MLA_NOPE:].reshape(MLA_KV_RANK, MLA_HEADS * MLA_V)], axis=1).astype(BF16)
    z32 = jnp.zeros((32,), F32)
    z64 = jnp.zeros((64,), F32)
    scale = (MLA_NOPE + MLA_ROPE) ** -0.5 * LOG2E
    qg = jnp.tile(jnp.concatenate([qn, z32]), MLA_HEADS) * scale
    qgs = jnp.tile(jnp.concatenate([z64, _swap_halves(qn[MLA_NOPE:]), z32]), MLA_HEADS) * scale
    kg = jnp.tile(jnp.concatenate([kn[:MLA_NOPE], z64]), MLA_HEADS)
    krg = jnp.concatenate([z64, kn[MLA_NOPE:], z32])
    krgs = jnp.concatenate([z64, _swap_halves(kn[MLA_NOPE:]), z32])
    cnt = jnp.tile(jnp.concatenate([jnp.full((64,), 1.0 / 64), jnp.full((32,), 1.0 / 32), jnp.ones((32,))]),
                   MLA_HEADS).astype(F32)
    rows = [norm_g, _pad_vec(jnp.tile(sb_qn, SB_HEADS) * (LOG2E * HEAD_DIM ** -0.5)), _pad_vec(jnp.tile(sb_kn, SB_HEADS)),
            _pad_vec(qa_g), _pad_vec(kva_g), qg, qgs, kg, _pad_vec(krg), _pad_vec(krgs), cnt]
    rows += [jnp.zeros((D_MODEL,), F32)] * (16 - len(rows))
    return w, wq, wqs, wkv, jnp.stack(rows).astype(F32)


def _padded_heads(start, n):
    return tuple(seg for h in range(n) for seg in ((start + h * HEAD_DIM, HEAD_DIM), (None, HEAD_DIM)))


NSA_SEGMENTS = (_padded_heads(0, NSA_HEADS)
                + tuple(seg for c in range(6) for seg in _padded_heads(768 + c * NSA_KV_W, NSA_KV_HEADS))
                + tuple(seg for g in range(NSA_KV_HEADS)
                        for seg in ((1920 + g * NSA_HPG * 3, NSA_HPG * 3), (None, LANES - NSA_HPG * 3)))
                + ((1956, NSA_W), (5028, DIL_W)))
DIL_SEGMENTS = ((2724, 3 * N_DIL * DIL_W),)


def _pack_odd(w_in_all, layer, norm_g, nsa_qn, nsa_kn, dil_qn, dil_kn):
    w_nsa = _pack_cols(w_in_all, layer, NSA_SEGMENTS)
    w_dil = _pack_cols(w_in_all, layer, DIL_SEGMENTS)
    z64 = jnp.zeros((64,), F32)
    n = NSA_HEADS * 128
    rows = [_pad_vec(norm_g, n),
            jnp.tile(jnp.concatenate([nsa_qn * (LOG2E * HEAD_DIM ** -0.5), z64]), NSA_HEADS),
            _pad_vec(jnp.tile(jnp.concatenate([nsa_kn, z64]), NSA_KV_HEADS), n),
            _pad_vec(jnp.tile(dil_qn, DIL_HEADS) * (LOG2E * HEAD_DIM ** -0.5), n),
            _pad_vec(jnp.tile(dil_kn, DIL_HEADS), n)]
    rows += [jnp.zeros((n,), F32)] * (8 - len(rows))
    return w_nsa, w_dil, jnp.stack(rows).astype(F32)


def _pack_compress(w, pe, double):
    w3 = w.reshape(CMP_LEN, HEAD_DIM, HEAD_DIM)
    w3 = jnp.concatenate([w3, jnp.zeros_like(w3)], axis=1)
    w3 = jnp.concatenate([w3, w3 if double else jnp.zeros_like(w3)], axis=2)
    wp = w3.reshape(2, CMP_STRIDE * 128, 128).astype(BF16)
    pe2 = jnp.concatenate([pe, jnp.zeros_like(pe)], axis=1).reshape(2, 1, CMP_STRIDE * 128)
    pe2 = jnp.broadcast_to(pe2, (2, 8, CMP_STRIDE * 128)).astype(F32)
    return wp, pe2


def _normed_len(gain, sizes):
    tot, off = 0.0, 0
    for n in sizes:
        tot = tot + n * jnp.max(jnp.square(gain[off:off + n]))
        off += n
    return jnp.sqrt(tot)


def _mla_logit_bound(qn, kn):
    sizes = (MLA_NOPE, MLA_ROPE)
    scale = (MLA_NOPE + MLA_ROPE) ** -0.5 * LOG2E
    return (_normed_len(qn, sizes) * _normed_len(kn, sizes) * (scale * ROUNDING_MARGIN)).reshape(1).astype(F32)


def _alibi_slopes(n):
    return 2.0 ** (-8.0 * jnp.arange(1, n + 1, dtype=jnp.float32) / n)


def kernel(x, c, positions, ada_w, ada_b, norm_g, ev_w_in, ev_w_out, sb_qn, sb_kn, mla_qa_g, mla_wq_up,
           mla_kva_g, mla_wkv_up, mla_qn, mla_kn, od_w_in, od_w_out, nsa_qn, nsa_kn, nsa_cmp_wk, nsa_cmp_wv,
           nsa_cmp_pe_k, nsa_cmp_pe_v, dil_qn, dil_kn):
    b, s, d = x.shape
    depth = ada_w.shape[0]
    pos_f = positions.astype(F32)

    inv_freq = ROPE_BASE ** (-jnp.arange(0, MLA_ROPE, 2, dtype=F32) / MLA_ROPE)
    ang = pos_f[:, None] * inv_freq[None, :]
    cos, sin = jnp.cos(ang), jnp.sin(ang)
    cos128 = jnp.concatenate([jnp.ones((s, 64), F32), cos, cos, jnp.zeros((s, 32), F32)], axis=1)
    sin128 = jnp.concatenate([jnp.zeros((s, 64), F32), -sin, sin, jnp.zeros((s, 32), F32)], axis=1)
    nsa_slopes = _alibi_slopes(NSA_HEADS)
    dil_slopes = _alibi_slopes(N_DIL * DIL_HEADS)
    nch = s // CMP_STRIDE
    chunk_sum = pos_f.reshape(nch, CMP_STRIDE).sum(axis=1)
    cpos = ((chunk_sum + jnp.roll(chunk_sum, -1)) / CMP_LEN).reshape(nch, 1)
    n_sel = s // SEL_LEN
    cst = np.arange(nch)[:, None] * CMP_STRIDE
    jst = np.arange(LANES)[None, :] * SEL_LEN
    ovl = ((cst <= jst + SEL_LEN - 1) & (cst + CMP_LEN - 1 >= jst) & (np.arange(LANES)[None, :] < n_sel))
    ovlt = jnp.asarray(ovl.astype(np.float32).T, BF16)
    pos_col = pos_f.reshape(s, 1)
    tqc = min(TQ_CMP, s)
    pos_rows = pos_f.reshape(s // tqc, 1, tqc)
    tqn = min(TQ_NSA, s)
    tkn = min(TK_NSA, tqn)
    assert tkn == TK_NSA
    pos2d = pos_f.reshape(s // tkn, 1, tkn)
    pos2d_win = pos_f.reshape(s // TK_WIN, 1, TK_WIN)
    q_first = pos_f[::tqn]
    k_last = pos_f[jnp.minimum((jnp.arange(SEL_TILE_STRIDE) + 1) * tkn - 1, s - 1)]
    group_slope = jnp.min(nsa_slopes.reshape(NSA_KV_HEADS, NSA_HPG), axis=1) * LOG2E
    sel_dead = group_slope[:, None, None] * (q_first[:, None] - k_last[None, :])[None] > F32_ZERO_EXP
    pos_sorted = jnp.all(pos_f[1:] >= pos_f[:-1])

    g64 = _group_matrix([(64, 1)] * 4, 256)
    gm = _group_matrix([(64, 1), (32, 1), (32, 0)] * 2, 256)
    tks = min(TK_SB, min(TQ_SB, s))
    tri = jnp.asarray(np.tril(np.ones((tks, tks), np.float32)), BF16)

    c8 = jnp.pad(c, ((0, 8 - b), (0, 0)))
    mod_all = _modulation(c8, ada_w, ada_b)

    for layer in range(depth):
        j = layer // 2
        mod3 = mod_all[layer, :b].reshape(b, 1, 3 * d)
        if layer % 2 == 0:
            w, wq, wqs, wkv, vec = _pack_even(ev_w_in, j, norm_g[layer], sb_qn[j], sb_kn[j], mla_qa_g[j],
                                              mla_wq_up[j], mla_kva_g[j], mla_wkv_up[j], mla_qn[j], mla_kn[j])
            sbq, sbk, sbv, gz, mq, mk, mv = _k1_even(x, mod3, vec, w, wq, wqs, wkv, g64, gm, cos128, sin128)
            sb_bound = (_normed_len(sb_qn[j], (HEAD_DIM,)) * _normed_len(sb_kn[j], (HEAD_DIM,))
                        * (HEAD_DIM ** -0.5 * LOG2E * ROUNDING_MARGIN)).reshape(1).astype(F32)
            o_sb = _sb_attention(sbq, sbk, sbv, tri, sb_bound, sb_bound[0] <= SAFE_LOGIT_BOUND)
            o_mla = _mla_attention(mq, mk, mv, _mla_logit_bound(mla_qn[j], mla_kn[j]))
            x = _out_even(x, mod3, o_sb, o_mla, gz, ev_w_out[j].astype(BF16))
        else:
            w_nsa, w_dil, vec = _pack_odd(od_w_in, j, norm_g[layer], nsa_qn[j], nsa_kn[j], dil_qn[j], dil_kn[j])
            q, ck, cv, sk, sv, wk, wv, gates, gz = _k1_nsa(x, mod3, vec, w_nsa, g64)
            dqkv = _k1_dil(x, mod3, vec, w_dil, g64)
            wck, pek = _pack_compress(nsa_cmp_wk[j], nsa_cmp_pe_k[j], False)
            wcv, pev = _pack_compress(nsa_cmp_wv[j], nsa_cmp_pe_v[j], True)
            kgain = jnp.concatenate([nsa_kn[j], jnp.zeros((64,), F32)]).reshape(1, 128)
            kc, vc = _compress(ck, cv, wck, wcv, pek, pev, kgain, g64[0:128, 0:128])
            o_c, selb, used = _nsa_cmp(nsa_slopes, q, kc, vc, ovlt, cpos, pos_rows, gates)
            flags = used[:, :, :, 0, :].reshape(b, NSA_KV_HEADS, s // tqn, tqn // min(TQ_CMP, s),
                                                SEL_TILE_STRIDE, LANES // SEL_TILE_STRIDE).max(axis=(3, 5))
            nsa_bound = (_normed_len(nsa_qn[j], (HEAD_DIM,)) * _normed_len(nsa_kn[j], (HEAD_DIM,))
                         * (HEAD_DIM ** -0.5 * LOG2E * ROUNDING_MARGIN)).reshape(1).astype(F32)
            safe = (nsa_bound[0] <= SAFE_LOGIT_BOUND) & pos_sorted
            flags = (flags > 0) & ~(safe & sel_dead[None])
            flags = flags.astype(jnp.int32).reshape(-1)
            o_s = _nsa_gqa(flags, nsa_slopes, nsa_bound, safe, q, selb, sk, sv, pos2d, pos_col, gates, 1)
            o_w = _nsa_gqa(flags, nsa_slopes, nsa_bound, safe, q, selb, wk, wv, pos2d_win, pos_col, gates, 2)
            dil_bound = (_normed_len(dil_qn[j], (HEAD_DIM,)) * _normed_len(dil_kn[j], (HEAD_DIM,))
                         * (HEAD_DIM ** -0.5 * LOG2E * ROUNDING_MARGIN)).reshape(1).astype(F32)
            dil_safe = (dil_bound[0] <= SAFE_LOGIT_BOUND) & pos_sorted
            dils, lses = [], []
            for g in range(N_DIL):
                o, lse = _dilated(dil_slopes, dil_bound, dil_safe, dqkv[3 * g], dqkv[3 * g + 1], dqkv[3 * g + 2],
                                  pos_f, g)
                dils.append(o)
                lses.append(lse)
            x = _out_odd(x, mod3, o_c, o_s, o_w, dils, lses, gz, od_w_out[j].astype(BF16))
    return x
```

```python
import functools

import numpy as np
import jax
import jax.numpy as jnp
from jax import lax
from jax.experimental import pallas as pl
from jax.experimental.pallas import tpu as pltpu

F32 = jnp.float32
BF16 = jnp.bfloat16

D_MODEL = 1024
HEAD_DIM = 64
NORM_EPS = 1e-6
TINY = 1e-30
SB_HEADS = 8
MLA_HEADS = 8
MLA_Q_RANK = 256
MLA_KV_RANK = 128
MLA_NOPE = 64
MLA_ROPE = 32
MLA_V = 64
ROPE_BASE = 10000.0
NSA_HEADS = 12
NSA_KV_HEADS = 3
NSA_HPG = 4
CMP_LEN = 32
CMP_STRIDE = 16
SEL_LEN = 64
SEL_TOPN = 16
WIN = 512
FORCE_BONUS = 1e3
DIL_CFG = ((128, 1), (512, 4), (2048, 16))
N_DIL = 3
DIL_HEADS = 4
SB_W = SB_HEADS * HEAD_DIM
MLA_OUT = MLA_HEADS * MLA_V
NSA_W = NSA_HEADS * HEAD_DIM
NSA_KV_W = NSA_KV_HEADS * HEAD_DIM
DIL_W = DIL_HEADS * HEAD_DIM

LANES = 128
NEG_MASK = -1e30
LOG2E = 1.4426950408889634
LN2 = 0.6931471805599453
F32_ZERO_EXP = 150.0
SAFE_LOGIT_BOUND = 50.0
ROUNDING_MARGIN = 1.02
SEL_OFF = 2.0 ** 30
VMEM_LIMIT = 56 * 1024 * 1024

TS_PROJ = 512
TQ_SB = 1024
TK_SB = 256
TQ_MLA = 1024
TD_MLA = 256
TQ_CMP = 512
SEL_ROWS = 32
TQ_NSA = 512
TK_NSA = 512
TK_WIN = 256
SEL_TILE_STRIDE = LANES * SEL_LEN // TK_NSA
T_DIL = 256


def _dot(a, b):
    return jnp.dot(a, b, preferred_element_type=F32)


def _dot_nt(a, b):
    return lax.dot_general(a, b, (((1,), (1,)), ((), ())), preferred_element_type=F32)


def _split_hl(a):
    hi = a.astype(BF16)
    lo = (a - hi.astype(F32)).astype(BF16)
    return hi, lo


def _dot_hl(a, b):
    hi, lo = _split_hl(a)
    return _dot(hi, b) + _dot(lo, b)


def _sigmoid(z):
    return 1.0 / (1.0 + jnp.exp(-z))


def _cparams(n_axes):
    return pltpu.CompilerParams(dimension_semantics=("arbitrary",) * n_axes,
                                vmem_limit_bytes=VMEM_LIMIT)


def _full(shape):
    n = len(shape)
    return pl.BlockSpec(shape, lambda *a, _n=n: (0,) * _n)


def _mod_kernel(c_ref, w_ref, b_ref, o_ref):
    c = c_ref[...]
    a = c * _sigmoid(c)
    ah, al = _split_hl(a)
    wh, wl = _split_hl(w_ref[0])
    o_ref[0] = _dot(ah, wh) + _dot(ah, wl) + _dot(al, wh) + b_ref[0]


def _modulation(c8, ada_w, ada_b):
    depth, d, n3 = ada_w.shape
    tn = 1024
    return pl.pallas_call(
        _mod_kernel,
        grid=(depth, n3 // tn),
        in_specs=[pl.BlockSpec((8, d), lambda l, j: (0, 0)),
                  pl.BlockSpec((1, d, tn), lambda l, j: (l, 0, j)),
                  pl.BlockSpec((1, 1, tn), lambda l, j: (l, 0, j))],
        out_specs=pl.BlockSpec((1, 8, tn), lambda l, j: (l, 0, j)),
        out_shape=jax.ShapeDtypeStruct((depth, 8, n3), F32),
        compiler_params=_cparams(2),
        name="adaln_mod",
    )(c8, ada_w, ada_b.reshape(depth, 1, n3))


def _modulated(x_ref, mod_ref, ng):
    x = x_ref[0]
    mod = mod_ref[0]
    shift = mod[:, 0:D_MODEL]
    scale = mod[:, D_MODEL:2 * D_MODEL]
    ms = jnp.mean(x * x, axis=-1, keepdims=True)
    h = x * lax.rsqrt(ms + NORM_EPS) * ng
    h = h * (1.0 + scale) + shift
    return h.astype(BF16)


def _group_sumsq(t, g):
    return _dot((t * t).astype(BF16), g)


def _group_norm_chunk(t, g, inv_cnt):
    return t * lax.rsqrt(_group_sumsq(t, g) * inv_cnt + NORM_EPS)


def _row_rms(t, gain):
    ms = jnp.mean(t * t, axis=-1, keepdims=True)
    return t * lax.rsqrt(ms + NORM_EPS) * gain


EV_COLS = 3200


def _k1_even_kernel(x_ref, mod_ref, vec_ref, w_ref, wq_ref, wqs_ref, wkv_ref, g64_ref, gm_ref,
                    cos_ref, sin_ref, sbq_o, sbk_o, sbv_o, gz_o, mq_o, mk_o, mv_o):
    hb = _modulated(x_ref, mod_ref, vec_ref[0:1, :])
    g64 = g64_ref[...]
    gm = gm_ref[...]
    inv64 = 1.0 / HEAD_DIM

    u = _dot(hb, w_ref[:, 0:1536])
    for c in range(2):
        sl = slice(c * 256, (c + 1) * 256)
        qn = _group_norm_chunk(u[:, c * 256:(c + 1) * 256], g64, inv64)
        sbq_o[0, :, sl] = (qn * vec_ref[1:2, sl]).astype(BF16)
        kn = _group_norm_chunk(u[:, 512 + c * 256:512 + (c + 1) * 256], g64, inv64)
        sbk_o[0, :, sl] = (kn * vec_ref[2:3, sl]).astype(BF16)
    sbv_o[0] = u[:, 1024:1536].astype(BF16)

    uz = _dot(hb, w_ref[:, 1536:2560])
    gz_o[0] = (uz * _sigmoid(uz)).astype(BF16)

    ul = _dot(hb, w_ref[:, 2560:3200])
    qlat = _row_rms(ul[:, 0:256], vec_ref[3:4, 0:256]).astype(BF16)
    kvlat = _row_rms(ul[:, 256:384], vec_ref[4:5, 0:128]).astype(BF16)
    krm = ul[:, 384:512]
    krs = ul[:, 512:640]
    cs = cos_ref[...]
    sn = sin_ref[...]
    cs2 = jnp.concatenate([cs, cs], axis=1)
    sn2 = jnp.concatenate([sn, sn], axis=1)

    tq = _dot(qlat, wq_ref[...])
    tqs = _dot(qlat, wqs_ref[...])
    for p in range(4):
        sl = slice(p * 256, (p + 1) * 256)
        tc = tq[:, p * 256:(p + 1) * 256]
        ss = _group_sumsq(tc, gm)
        inv = lax.rsqrt(ss * vec_ref[10:11, sl] + NORM_EPS)
        a = tc * inv * vec_ref[5:6, sl]
        b = tqs[:, p * 256:(p + 1) * 256] * inv * vec_ref[6:7, sl]
        mq_o[0, :, sl] = (a * cs2 + b * sn2).astype(BF16)

    sskr = _group_sumsq(krm, gm[0:128, 0:128])
    invr = lax.rsqrt(sskr * vec_ref[10:11, 0:128] + NORM_EPS)
    kr = (krm * invr * vec_ref[8:9, 0:128]) * cs + (krs * invr * vec_ref[9:10, 0:128]) * sn
    kr2 = jnp.concatenate([kr, kr], axis=1)

    kv = _dot(kvlat, wkv_ref[...])
    for p in range(4):
        sl = slice(p * 256, (p + 1) * 256)
        kn = _group_norm_chunk(kv[:, p * 256:(p + 1) * 256], g64, inv64)
        mk_o[0, :, sl] = (kn * vec_ref[7:8, sl] + kr2).astype(BF16)
    mv_o[0] = kv[:, 1024:1536].astype(BF16)


def _k1_even(x, mod3, vec, w, wq, wqs, wkv, g64, gm, cos128, sin128):
    b, s, d = x.shape
    ts = min(TS_PROJ, s)
    row = lambda n: pl.BlockSpec((1, ts, n), lambda i, j: (i, j, 0))
    outs = [(512, BF16), (512, BF16), (512, BF16), (1024, BF16), (1024, BF16), (1024, BF16), (512, BF16)]
    return pl.pallas_call(
        _k1_even_kernel,
        grid=(b, s // ts),
        in_specs=[row(d),
                  pl.BlockSpec((1, 1, 3 * d), lambda i, j: (i, 0, 0)),
                  _full(vec.shape), _full(w.shape), _full(wq.shape), _full(wqs.shape), _full(wkv.shape),
                  _full(g64.shape), _full(gm.shape),
                  pl.BlockSpec((ts, 128), lambda i, j: (j, 0)),
                  pl.BlockSpec((ts, 128), lambda i, j: (j, 0))],
        out_specs=[row(n) for n, _ in outs],
        out_shape=[jax.ShapeDtypeStruct((b, s, n), dt) for n, dt in outs],
        compiler_params=_cparams(2),
        name="inproj_even",
    )(x, mod3, vec, w, wq, wqs, wkv, g64, gm, cos128, sin128)


def _sb_kernel(bnd_ref, q_ref, k_ref, v_ref, tri_ref, o_ref, acc_ref, car_ref, *, tq, tk, bounded):
    qi = pl.program_id(2)
    q = q_ref[0]
    lane = lax.broadcasted_iota(jnp.int32, (1, LANES), 1)
    lo_half = lane < HEAD_DIM
    zero = jnp.zeros_like(q)
    qs = (jnp.where(lo_half, q, zero), jnp.where(lo_half, zero, q))
    tri = tri_ref[...]
    acc_ref[...] = jnp.zeros_like(acc_ref)
    car_ref[...] = jnp.zeros_like(car_ref)
    col0 = lax.broadcasted_iota(jnp.int32, (1, tk), 1)

    def tile(kt, r0, masked):
        n = tq - r0
        off = pl.multiple_of(kt * tk, tk)
        k = k_ref[0, pl.ds(off, tk), :]
        v = v_ref[0, pl.ds(off, tk), :]
        if masked:
            rows = qi * tq + r0 + lax.broadcasted_iota(jnp.int32, (n, 1), 0)
            strict = (kt * tk + col0) < rows
        for i in range(2):
            z = _dot_nt(qs[i][r0:tq], k)
            if bounded:
                lom = jnp.log(1.0 + jnp.exp2(z)) * (-LOG2E)
            else:
                lom = -jnp.maximum(z, 0.0) - jnp.log(1.0 + jnp.exp2(-jnp.abs(z))) * LOG2E
            if masked:
                lom = jnp.where(strict, lom, 0.0)
            tt = _dot(lom.astype(BF16), tri)
            car = car_ref[i, r0:tq]
            w = jnp.exp2(tt + z + jnp.concatenate([car] * (tk // LANES), axis=1))
            if masked:
                w = jnp.where(strict, w, 0.0)
            acc_ref[i, r0:tq] += _dot(w.astype(BF16), v)
            car_ref[i, r0:tq] = car + jnp.broadcast_to(tt[:, 0:1], (n, LANES))

    per = tq // tk
    for c in reversed(range(per)):
        tile(qi * per + c, c * tk, True)

    if bounded:
        limit = -(F32_ZERO_EXP + bnd_ref[0])

        def still_live():
            return jnp.max(jnp.maximum(car_ref[0], car_ref[1])) > limit

        def single(c):
            j, _ = c
            tile(qi * per - 1 - j, 0, False)
            return j + 1, still_live()

        _, live = lax.while_loop(lambda c: (c[0] < per) & c[1], single, (0, qi > 0))

        def group(c):
            g, _ = c
            for u in range(per):
                tile(qi * per - 1 - per * (g + 1) - u, 0, False)
            return g + 1, still_live()

        lax.while_loop(lambda c: (c[0] < qi - 1) & c[1], group, (0, live))
    else:
        def body(j, carry):
            for u in range(per):
                tile(qi * per - 1 - per * j - u, 0, False)
            return carry

        lax.fori_loop(0, qi, body, 0)
    o_ref[0] = jnp.where(lo_half, acc_ref[0], acc_ref[1])


def _sb_attention(q, k, v, tri, bound, safe):
    b, s, w = q.shape
    tq = min(TQ_SB, s)
    tk = min(TK_SB, tq)

    def call(bounded):
        kern = functools.partial(_sb_kernel, tq=tq, tk=tk, bounded=bounded)
        return pl.pallas_call(
            kern,
            grid=(b, w // LANES, s // tq),
            in_specs=[pl.BlockSpec(memory_space=pltpu.SMEM),
                      pl.BlockSpec((1, tq, LANES), lambda i, p, j: (i, j, p)),
                      pl.BlockSpec((1, s, LANES), lambda i, p, j: (i, 0, p)),
                      pl.BlockSpec((1, s, LANES), lambda i, p, j: (i, 0, p)),
                      _full(tri.shape)],
            out_specs=pl.BlockSpec((1, tq, LANES), lambda i, p, j: (i, j, p)),
            out_shape=jax.ShapeDtypeStruct((b, s, w), F32),
            scratch_shapes=[pltpu.VMEM((2, tq, LANES), F32), pltpu.VMEM((2, tq, LANES), F32)],
            compiler_params=_cparams(3),
            name="stickbreak_attn_bounded" if bounded else "stickbreak_attn",
        )(bound, q, k, v, tri)

    return lax.cond(safe, lambda: call(True), lambda: call(False))


def _lane_max(s):
    m = s[:, 0:LANES]
    for c in range(1, s.shape[1] // LANES):
        m = jnp.maximum(m, s[:, c * LANES:(c + 1) * LANES])
    return m


def _mla_kernel(bnd_ref, q_ref, k_ref, v_ref, o_ref, acc_ref, m_ref, *, tq, td, bounded):
    qi = pl.program_id(2)
    q = q_ref[0]
    qs = (q[:, 0:LANES], q[:, LANES:2 * LANES])
    lane = lax.broadcasted_iota(jnp.int32, (1, LANES), 1)
    lo_half = lane < HEAD_DIM
    acc_ref[...] = jnp.zeros_like(acc_ref)
    if bounded:
        m_ref[...] = jnp.full(m_ref.shape, bnd_ref[0], F32)
    else:
        m_ref[...] = jnp.full_like(m_ref, NEG_MASK)

    def scores(off, width, r0, masked):
        k = k_ref[0, pl.ds(off, width), :]
        if masked:
            rows = qi * tq + r0 + lax.broadcasted_iota(jnp.int32, (tq - r0, 1), 0)
            keep = (off + lax.broadcasted_iota(jnp.int32, (1, width), 1)) <= rows
        out = []
        for i in range(2):
            s = _dot_nt(qs[i][r0:tq], k[:, i * LANES:(i + 1) * LANES])
            if masked:
                s = jnp.where(keep, s, NEG_MASK)
            out.append(s)
        return out

    def sweep(fn):
        def body(j, carry):
            fn(pl.multiple_of(j * tq, tq), tq, 0, False)
            return carry
        lax.fori_loop(0, qi, body, 0)
        for c in range(tq // td):
            fn(pl.multiple_of(qi * tq + c * td, td), td, c * td, True)

    def row_max(off, width, r0, masked):
        for i, s in enumerate(scores(off, width, r0, masked)):
            m_ref[i, r0:tq] = jnp.maximum(m_ref[i, r0:tq], _lane_max(s))

    def accumulate(off, width, r0, masked):
        vx = jnp.concatenate([v_ref[0, pl.ds(off, width), :], jnp.ones((width, LANES), BF16)], axis=1)
        for i, s in enumerate(scores(off, width, r0, masked)):
            p = jnp.exp2(s - jnp.concatenate([m_ref[i, r0:tq]] * (width // LANES), axis=1))
            acc_ref[i, r0:tq] += _dot(p.astype(BF16), vx)

    if not bounded:
        sweep(row_max)
        for i in range(2):
            m_ref[i] = jnp.broadcast_to(jnp.max(m_ref[i], axis=-1, keepdims=True), (tq, LANES))
    sweep(accumulate)
    o_ref[0] = jnp.where(lo_half, acc_ref[0, :, 0:LANES] / acc_ref[0, :, LANES:2 * LANES],
                         acc_ref[1, :, 0:LANES] / acc_ref[1, :, LANES:2 * LANES])


def _mla_attention(q, k, v, bound):
    b, s, _ = q.shape
    tq = min(TQ_MLA, s)
    td = min(TD_MLA, tq)

    def call(bounded):
        kern = functools.partial(_mla_kernel, tq=tq, td=td, bounded=bounded)
        return pl.pallas_call(
            kern,
            grid=(b, MLA_HEADS // 2, s // tq),
            in_specs=[pl.BlockSpec(memory_space=pltpu.SMEM),
                      pl.BlockSpec((1, tq, 2 * LANES), lambda i, p, j: (i, j, p)),
                      pl.BlockSpec((1, s, 2 * LANES), lambda i, p, j: (i, 0, p)),
                      pl.BlockSpec((1, s, LANES), lambda i, p, j: (i, 0, p))],
            out_specs=pl.BlockSpec((1, tq, LANES), lambda i, p, j: (i, j, p)),
            out_shape=jax.ShapeDtypeStruct((b, s, MLA_OUT), F32),
            scratch_shapes=[pltpu.VMEM((2, tq, 2 * LANES), F32), pltpu.VMEM((2, tq, LANES), F32)],
            compiler_params=_cparams(3),
            name="mla_attn_bounded" if bounded else "mla_attn",
        )(bound, q, k, v)

    return lax.cond(bound[0] <= SAFE_LOGIT_BOUND, lambda: call(True), lambda: call(False))


def _out_even_kernel(x_ref, mod_ref, osb_ref, omla_ref, gz_ref, w_ref, o_ref):
    gz = gz_ref[0].astype(F32)
    m1 = (osb_ref[0] * gz[:, 0:SB_W]).astype(BF16)
    m2 = (omla_ref[0] * gz[:, SB_W:SB_W + MLA_OUT]).astype(BF16)
    y = _dot(m1, w_ref[0:SB_W, :]) + _dot(m2, w_ref[SB_W:SB_W + MLA_OUT, :])
    gate = mod_ref[0][:, 2 * D_MODEL:3 * D_MODEL]
    o_ref[0] = x_ref[0] + gate * y


def _out_even(x, mod3, osb, omla, gz, w):
    b, s, d = x.shape
    ts = min(TS_PROJ, s)
    row = lambda n: pl.BlockSpec((1, ts, n), lambda i, j: (i, j, 0))
    return pl.pallas_call(
        _out_even_kernel,
        grid=(b, s // ts),
        in_specs=[row(d), pl.BlockSpec((1, 1, 3 * d), lambda i, j: (i, 0, 0)),
                  row(SB_W), row(MLA_OUT), row(SB_W + MLA_OUT), _full(w.shape)],
        out_specs=row(d),
        out_shape=jax.ShapeDtypeStruct((b, s, d), F32),
        compiler_params=_cparams(2),
        name="outproj_even",
    )(x, mod3, osb, omla, gz, w)


NSA_COLS = 5248


def _k1_nsa_kernel(x_ref, mod_ref, vec_ref, w_ref, g64_ref,
                   q_o, ck_o, cv_o, sk_o, sv_o, wk_o, wv_o, gt_o, gz_o, chunk_ref):
    ts = x_ref.shape[1]
    hb = _modulated(x_ref, mod_ref, vec_ref[0:1, 0:D_MODEL])
    g64 = g64_ref[...]
    inv64 = 1.0 / HEAD_DIM

    uq = _dot(hb, w_ref[:, 0:1536])
    for c in range(6):
        sl = slice(c * 256, (c + 1) * 256)
        qn = _group_norm_chunk(uq[:, c * 256:(c + 1) * 256], g64, inv64)
        q_o[0, :, sl] = (qn * vec_ref[1:2, sl]).astype(BF16)

    uc = _dot(hb, w_ref[:, 1536:2304])
    for c in range(2 * NSA_KV_HEADS):
        chunk_ref[c] = uc[:, c * 128:(c + 1) * 128]
    for g in range(NSA_KV_HEADS):
        for tok in range(CMP_STRIDE):
            rows = pl.ds(tok, ts // CMP_STRIDE, stride=CMP_STRIDE)
            sl = slice(tok * 128, (tok + 1) * 128)
            ck_o[0, g, :, sl] = chunk_ref[g, rows, :].astype(BF16)
            cv_o[0, g, :, sl] = chunk_ref[NSA_KV_HEADS + g, rows, :].astype(BF16)

    us = _dot(hb, w_ref[:, 2304:3840])
    kgain = vec_ref[2:3, 0:384]
    g128 = g64[0:128, 0:128]
    for g in range(NSA_KV_HEADS):
        sl = slice(g * 128, (g + 1) * 128)
        t = us[:, g * 128:(g + 1) * 128]
        ss = _group_sumsq(t, g128)
        sk_o[0, :, sl] = (t * lax.rsqrt(ss * inv64 + NORM_EPS) * kgain[:, sl]).astype(BF16)
        t = us[:, 768 + g * 128:768 + (g + 1) * 128]
        ss = _group_sumsq(t, g128)
        wk_o[0, :, sl] = (t * lax.rsqrt(ss * inv64 + NORM_EPS) * kgain[:, sl]).astype(BF16)
    ones_hi = (lax.broadcasted_iota(jnp.int32, (1, 384), 1) % 128) >= HEAD_DIM
    sv_o[0] = jnp.where(ones_hi, 1.0, us[:, 384:768]).astype(BF16)
    wv_o[0] = jnp.where(ones_hi, 1.0, us[:, 1152:1536]).astype(BF16)

    ug = _dot(hb, w_ref[:, 3840:4224])
    gt_o[0] = _sigmoid(ug)
    uz = _dot(hb, w_ref[:, 4224:5248])
    gz_o[0] = (uz * _sigmoid(uz)).astype(BF16)


def _k1_nsa(x, mod3, vec, w, g64):
    b, s, d = x.shape
    ts = min(TS_PROJ, s)
    row = lambda n: pl.BlockSpec((1, ts, n), lambda i, j: (i, j, 0))
    cw = CMP_STRIDE * 128
    grp = pl.BlockSpec((1, NSA_KV_HEADS, ts // CMP_STRIDE, cw), lambda i, j: (i, 0, j, 0))
    return pl.pallas_call(
        _k1_nsa_kernel,
        grid=(b, s // ts),
        in_specs=[row(d), pl.BlockSpec((1, 1, 3 * d), lambda i, j: (i, 0, 0)),
                  _full(vec.shape), _full(w.shape), _full(g64.shape)],
        out_specs=[row(1536), grp, grp, row(384), row(384), row(384), row(384), row(384), row(1024)],
        out_shape=[jax.ShapeDtypeStruct((b, s, 1536), BF16),
                   jax.ShapeDtypeStruct((b, NSA_KV_HEADS, s // CMP_STRIDE, cw), BF16),
                   jax.ShapeDtypeStruct((b, NSA_KV_HEADS, s // CMP_STRIDE, cw), BF16),
                   jax.ShapeDtypeStruct((b, s, 384), BF16),
                   jax.ShapeDtypeStruct((b, s, 384), BF16),
                   jax.ShapeDtypeStruct((b, s, 384), BF16),
                   jax.ShapeDtypeStruct((b, s, 384), BF16),
                   jax.ShapeDtypeStruct((b, s, 384), F32),
                   jax.ShapeDtypeStruct((b, s, 1024), BF16)],
        scratch_shapes=[pltpu.VMEM((2 * NSA_KV_HEADS, ts, 128), F32)],
        compiler_params=_cparams(2),
        name="inproj_nsa",
    )(x, mod3, vec, w, g64)


def _k1_dil_kernel(x_ref, mod_ref, vec_ref, w_ref, g64_ref, *refs):
    outs, st = refs[:3 * N_DIL], refs[3 * N_DIL]
    ts = x_ref.shape[1]
    hb = _modulated(x_ref, mod_ref, vec_ref[0:1, 0:D_MODEL])
    g64 = g64_ref[...]
    inv64 = 1.0 / HEAD_DIM
    u = _dot(hb, w_ref[...])
    for g in range(N_DIL):
        dil = DIL_CFG[g][1]
        qn = _group_norm_chunk(u[:, g * 256:(g + 1) * 256], g64, inv64) * vec_ref[3:4, 0:256]
        kn = _group_norm_chunk(u[:, 768 + g * 256:768 + (g + 1) * 256], g64, inv64) * vec_ref[4:5, 0:256]
        vals = (qn, kn, u[:, 1536 + g * 256:1536 + (g + 1) * 256])
        for j, val in enumerate(vals):
            o_ref = outs[3 * g + j]
            if dil == 1:
                o_ref[0] = val.astype(BF16)
                continue
            for h in range(2):
                st[j, h] = val[:, h * 128:(h + 1) * 128]
            for r in range(dil):
                for h in range(2):
                    sl = slice(r * 256 + h * 128, r * 256 + (h + 1) * 128)
                    o_ref[0, :, sl] = st[j, h, pl.ds(r, ts // dil, stride=dil), :].astype(BF16)


def _k1_dil(x, mod3, vec, w, g64):
    b, s, d = x.shape
    ts = min(TS_PROJ, s)
    row = lambda n: pl.BlockSpec((1, ts, n), lambda i, j: (i, j, 0))
    specs, shapes = [], []
    for g in range(N_DIL):
        dil = DIL_CFG[g][1]
        for _ in range(3):
            specs.append(pl.BlockSpec((1, ts // dil, dil * 256), lambda i, j: (i, j, 0)))
            shapes.append(jax.ShapeDtypeStruct((b, s // dil, dil * 256), BF16))
    return pl.pallas_call(
        _k1_dil_kernel,
        grid=(b, s // ts),
        in_specs=[row(d), pl.BlockSpec((1, 1, 3 * d), lambda i, j: (i, 0, 0)),
                  _full(vec.shape), _full(w.shape), _full(g64.shape)],
        out_specs=specs,
        out_shape=shapes,
        scratch_shapes=[pltpu.VMEM((3, 2, ts, 128), F32)],
        compiler_params=_cparams(2),
        name="inproj_dil",
    )(x, mod3, vec, w, g64)


def _compress_kernel(xk_ref, xv_ref, wk_ref, wv_ref, pek_ref, pev_ref, gain_ref, g128_ref, kc_o, vc_o):
    nch = xk_ref.shape[2]

    def comp(x_ref, w_ref, pe_ref):
        x = x_ref[0, 0]
        lo = _dot(x, w_ref[0])
        hi = _dot(x, w_ref[1])
        pec = _dot_hl(pe_ref[0], w_ref[0]) + _dot_hl(pe_ref[1], w_ref[1])
        return lo + pltpu.roll(hi, nch - 1, 0) + pec[0:1, :]

    kc = comp(xk_ref, wk_ref, pek_ref)
    ss = _dot_hl(kc * kc, g128_ref[...])
    kc_o[0, 0] = (kc * lax.rsqrt(ss * (1.0 / HEAD_DIM) + NORM_EPS) * gain_ref[...]).astype(BF16)
    vc_o[0, 0] = comp(xv_ref, wv_ref, pev_ref).T.astype(BF16)


def _compress(ck, cv, wk, wv, pek, pev, gain, g128):
    b, g, nch, _ = ck.shape
    xk, xv = ck, cv
    blk = pl.BlockSpec((1, 1, nch, CMP_STRIDE * 128), lambda i, j: (i, j, 0, 0))
    oblk = pl.BlockSpec((1, 1, nch, 128), lambda i, j: (i, j, 0, 0))
    tblk = pl.BlockSpec((1, 1, 128, nch), lambda i, j: (i, j, 0, 0))
    return pl.pallas_call(
        _compress_kernel,
        grid=(b, g),
        in_specs=[blk, blk, _full(wk.shape), _full(wv.shape), _full(pek.shape), _full(pev.shape),
                  _full(gain.shape), _full(g128.shape)],
        out_specs=[oblk, tblk],
        out_shape=[jax.ShapeDtypeStruct((b, g, nch, 128), BF16), jax.ShapeDtypeStruct((b, g, 128, nch), BF16)],
        compiler_params=_cparams(2),
        name="nsa_compress",
    )(xk, xv, wk, wv, pek, pev, gain, g128)


def _pair_select(lo_half, a, b):
    return jnp.where(lo_half, a, b)


def _nsa_cmp_kernel(slope_ref, q_ref, kc_ref, vct_ref, ovlt_ref, cposc_ref, prow_ref, gt_ref,
                    oc_o, selb_o, any_o, imp_ref, *, tq):
    g = pl.program_id(1)
    qi = pl.program_id(2)
    q4 = q_ref[0]
    nch = kc_ref.shape[2]
    t = qi * tq + lax.broadcasted_iota(jnp.int32, (1, tq), 1)
    gtt = gt_ref[0].T
    row_lo = lax.broadcasted_iota(jnp.int32, (LANES, 1), 0) < HEAD_DIM
    pref = prow_ref[qi][:, 0:1]

    def attend(nblk):
        kc = kc_ref[0, 0, 0:nblk, :]
        vct = vct_ref[0, 0, :, 0:nblk]
        n_id = lax.broadcasted_iota(jnp.int32, (nblk, 1), 0)
        valid = (n_id * CMP_STRIDE + (CMP_LEN - 1)) <= t
        rel = (cposc_ref[0:nblk, :] - pref) * LOG2E
        psum = jnp.zeros((nblk, tq), F32)
        outs = []
        for hh in range(NSA_HPG):
            s = _dot_nt(kc, q4[:, hh * LANES:(hh + 1) * LANES]) + slope_ref[g * NSA_HPG + hh] * rel
            s = jnp.where(valid, s, -jnp.inf)
            mx = jnp.max(s, axis=0, keepdims=True)
            mx = jnp.where(mx == -jnp.inf, 0.0, mx)
            e = jnp.exp2(s - mx)
            den = jnp.maximum(jnp.sum(e, axis=0, keepdims=True), TINY)
            p = e * (1.0 / den)
            psum = psum + p
            outs.append(_dot(vct, p.astype(BF16)) * gtt[3 * hh:3 * hh + 1, :])
        oc_o[0, :, 0:LANES] = jnp.where(row_lo, outs[0], outs[1]).T
        oc_o[0, :, LANES:2 * LANES] = jnp.where(row_lo, outs[2], outs[3]).T
        hi, lo = _split_hl(psum)
        imp_ref[...] = _dot(ovlt_ref[:, 0:nblk], hi) + _dot(ovlt_ref[:, 0:nblk], lo)

    n_valid = ((qi + 1) * tq - CMP_LEN) // CMP_STRIDE + 1
    n_groups = nch // LANES
    need = jnp.clip((n_valid + LANES - 1) // LANES, 1, n_groups)
    for c in range(1, n_groups + 1):
        @pl.when(need == c)
        def _(c=c):
            attend(c * LANES)

    assert NSA_HPG < FORCE_BONUS
    cur = t >> 6

    def select(nrow):
        blk = lax.broadcasted_iota(jnp.int32, (nrow, 1), 0)
        forced = (blk == 0) | (blk == cur) | (blk == cur - 1)
        allowed = blk <= cur
        score = jnp.where(allowed & ~forced, imp_ref[0:nrow, :], -jnp.inf)
        blk_f = blk.astype(F32)

        def pick(_, sc):
            mx = jnp.max(sc, axis=0, keepdims=True)
            idx = jnp.min(jnp.where(sc == mx, blk_f, float(LANES)), axis=0, keepdims=True)
            return jnp.where(blk_f == idx, -jnp.inf, sc)

        left = lax.fori_loop(0, SEL_TOPN - 3, pick, score)
        imp_ref[0:nrow, :] = jnp.where((left == -jnp.inf) & allowed, 1.0, 0.0)
        if nrow < LANES:
            imp_ref[nrow:LANES, :] = jnp.zeros((LANES - nrow, tq), F32)

    n_allowed = ((qi + 1) * tq) // SEL_LEN
    n_quarters = LANES // SEL_ROWS
    need_sel = jnp.clip((n_allowed + SEL_ROWS - 1) // SEL_ROWS, 1, n_quarters)
    for c in range(1, n_quarters + 1):
        @pl.when(need_sel == c)
        def _(c=c):
            select(c * SEL_ROWS)

    chosen = imp_ref[...].T
    selb_o[0, 0] = ((chosen - 1.0) * SEL_OFF).astype(BF16)
    used = jnp.max(chosen, axis=0, keepdims=True)
    any_o[0, 0, 0] = jnp.broadcast_to(used, (8, LANES))


def _nsa_cmp(slopes, q, kc, vct, ovlt, cposc, pos_rows, gates):
    b, s, _ = q.shape
    nch = kc.shape[2]
    tq = pos_rows.shape[2]
    kern = functools.partial(_nsa_cmp_kernel, tq=tq)
    return pl.pallas_call(
        kern,
        grid=(b, NSA_KV_HEADS, s // tq),
        in_specs=[pl.BlockSpec(memory_space=pltpu.SMEM),
                  pl.BlockSpec((1, tq, 4 * LANES), lambda i, g, j: (i, j, g)),
                  pl.BlockSpec((1, 1, nch, LANES), lambda i, g, j: (i, g, 0, 0)),
                  pl.BlockSpec((1, 1, LANES, nch), lambda i, g, j: (i, g, 0, 0)),
                  _full(ovlt.shape), _full(cposc.shape), _full(pos_rows.shape),
                  pl.BlockSpec((1, tq, LANES), lambda i, g, j: (i, j, g))],
        out_specs=[pl.BlockSpec((1, tq, 2 * LANES), lambda i, g, j: (i, j, g)),
                   pl.BlockSpec((1, 1, tq, LANES), lambda i, g, j: (i, g, j, 0)),
                   pl.BlockSpec((1, 1, 1, 8, LANES), lambda i, g, j: (i, g, j, 0, 0))],
        out_shape=[jax.ShapeDtypeStruct((b, s, NSA_W), F32),
                   jax.ShapeDtypeStruct((b, NSA_KV_HEADS, s, LANES), BF16),
                   jax.ShapeDtypeStruct((b, NSA_KV_HEADS, s // tq, 8, LANES), F32)],
        scratch_shapes=[pltpu.VMEM((LANES, tq), F32)],
        compiler_params=_cparams(3),
        name="nsa_cmp_topk",
    )(slopes, q, kc, vct, ovlt, cposc, pos_rows, gates)


def _gqa_kernel(flag_ref, slope_ref, bnd_ref, q_ref, selb_ref, k_ref, v_ref, pos_ref, pcol_ref, gt_ref, o_ref,
                qa_ref, acc_ref, m_ref, *, tq, tk, branch, bounded):
    bi = pl.program_id(0)
    g = pl.program_id(1)
    qi = pl.program_id(2)
    q4 = q_ref[0]
    lane = lax.broadcasted_iota(jnp.int32, (1, LANES), 1)
    lo_half = lane < HEAD_DIM
    pref = pos_ref[qi * (tq // tk)][:, 0:1]
    for hh in range(NSA_HPG):
        sl = slice(hh * tq, (hh + 1) * tq)
        qa_ref[sl, 0:LANES] = q4[:, hh * LANES:(hh + 1) * LANES]
        if branch == 1:
            qa_ref[sl, LANES:2 * LANES] = selb_ref[0, 0]
        if bounded:
            own = (pcol_ref[...] - pref) * (LOG2E * slope_ref[g * NSA_HPG + hh]) + bnd_ref[0]
            m_ref[sl] = jnp.broadcast_to(own, (tq, LANES))
    acc_ref[...] = jnp.zeros_like(acc_ref)
    if not bounded:
        m_ref[...] = jnp.full_like(m_ref, NEG_MASK)
    rows = qi * tq + lax.broadcasted_iota(jnp.int32, (tq, 1), 0)
    col0 = lax.broadcasted_iota(jnp.int32, (1, tk), 1)
    krow = lax.broadcasted_iota(jnp.int32, (tk, 1), 0)
    per = tq // tk
    nrep = tk // LANES
    flag0 = ((bi * NSA_KV_HEADS + g) * pl.num_programs(2) + qi) * SEL_TILE_STRIDE

    def scores(kt_true, masked, r_lo, r_hi):
        kt = jnp.maximum(kt_true, 0)
        off = pl.multiple_of(kt * tk, tk)
        k = k_ref[0, pl.ds(off, tk), :]
        if branch == 1:
            blk = (kt * tk + krow) >> 6
            onehot = jnp.where(lane == blk, 1.0, 0.0).astype(BF16)
            s_all = _dot_nt(qa_ref[...], jnp.concatenate([k, onehot], axis=1))
            parts = [s_all[hh * tq:(hh + 1) * tq] for hh in range(NSA_HPG)]
        else:
            parts = [_dot_nt(qa_ref[hh * tq + r_lo:hh * tq + r_hi, 0:LANES], k) for hh in range(NSA_HPG)]
        rel = (pos_ref[kt] - pref) * LOG2E
        if masked:
            cols = kt_true * tk + col0
            d = rows[r_lo:r_hi] - cols
            keep = d >= 0
            if branch == 2:
                keep = keep & (d < WIN) & (cols >= 0)
        out = []
        for hh in range(NSA_HPG):
            s = parts[hh] + slope_ref[g * NSA_HPG + hh] * rel
            if masked:
                s = jnp.where(keep, s, NEG_MASK)
            out.append(s)
        return out

    def sweep(fn):
        if branch == 1:
            def body(j, carry):
                @pl.when(flag_ref[flag0 + j] != 0)
                def _():
                    fn(j, False, 0, tq)
                return carry
            lax.fori_loop(0, qi * per, body, 0)
            for dd in range(per):
                fn(qi * per + dd, True, 0, tq)
        else:
            for c in range(-(WIN // tk), per):
                r_lo = max(0, c * tk)
                r_hi = min(tq, -(-(c * tk + tk + WIN - 1) // 8) * 8)
                fn(qi * per + c, True, r_lo, r_hi)

    def row_max(kt_true, masked, r_lo, r_hi):
        for hh, s in enumerate(scores(kt_true, masked, r_lo, r_hi)):
            sl = slice(hh * tq + r_lo, hh * tq + r_hi)
            m_ref[sl] = jnp.maximum(m_ref[sl], _lane_max(s))

    def accumulate(kt_true, masked, r_lo, r_hi):
        off = pl.multiple_of(jnp.maximum(kt_true, 0) * tk, tk)
        v = v_ref[0, pl.ds(off, tk), :]
        for hh, s in enumerate(scores(kt_true, masked, r_lo, r_hi)):
            sl = slice(hh * tq + r_lo, hh * tq + r_hi)
            p = jnp.exp2(s - jnp.concatenate([m_ref[sl]] * nrep, axis=1))
            acc_ref[sl] += _dot(p.astype(BF16), v)

    if not bounded:
        sweep(row_max)
        for hh in range(NSA_HPG):
            sl = slice(hh * tq, (hh + 1) * tq)
            m_ref[sl] = jnp.broadcast_to(jnp.max(m_ref[sl], axis=-1, keepdims=True), (tq, LANES))
    sweep(accumulate)

    gt = gt_ref[0]
    for pr in range(2):
        res = []
        for x in range(2):
            hh = 2 * pr + x
            a = acc_ref[hh * tq:(hh + 1) * tq]
            r = pltpu.roll(a, HEAD_DIM, 1)
            o = a / r if x == 0 else r / a
            res.append(o * gt[:, 3 * hh + branch:3 * hh + branch + 1])
        o_ref[0, :, pr * LANES:(pr + 1) * LANES] = _pair_select(lo_half, res[0], res[1])


def _nsa_gqa(flags, slopes, bound, safe, q, selb, k, v, pos2d, pos_col, gates, branch):
    b, s, _ = q.shape
    tq = min(TQ_NSA, s)
    tk = pos2d.shape[2]

    def call(bounded):
        kern = functools.partial(_gqa_kernel, tq=tq, tk=tk, branch=branch, bounded=bounded)
        smem = pl.BlockSpec(memory_space=pltpu.SMEM)
        name = ("nsa_sel_attn" if branch == 1 else "nsa_win_attn") + ("_bounded" if bounded else "")
        return pl.pallas_call(
            kern,
            grid=(b, NSA_KV_HEADS, s // tq),
            in_specs=[smem, smem, smem,
                      pl.BlockSpec((1, tq, 4 * LANES), lambda i, g, j: (i, j, g)),
                      pl.BlockSpec((1, 1, tq, LANES), lambda i, g, j: (i, g, j, 0)),
                      pl.BlockSpec((1, s, LANES), lambda i, g, j: (i, 0, g)),
                      pl.BlockSpec((1, s, LANES), lambda i, g, j: (i, 0, g)),
                      _full(pos2d.shape),
                      pl.BlockSpec((tq, 1), lambda i, g, j: (j, 0)),
                      pl.BlockSpec((1, tq, LANES), lambda i, g, j: (i, j, g))],
            out_specs=pl.BlockSpec((1, tq, 2 * LANES), lambda i, g, j: (i, j, g)),
            out_shape=jax.ShapeDtypeStruct((b, s, NSA_W), F32),
            scratch_shapes=[pltpu.VMEM((NSA_HPG * tq, 2 * LANES), BF16),
                            pltpu.VMEM((NSA_HPG * tq, LANES), F32),
                            pltpu.VMEM((NSA_HPG * tq, LANES), F32)],
            compiler_params=_cparams(3),
            name=name,
        )(flags, slopes, bound, q, selb, k, v, pos2d, pos_col, gates)

    return lax.cond(safe, lambda: call(True), lambda: call(False))


def _dil_kernel(slope_ref, bnd_ref, q_ref, k_ref, v_ref, pc_ref, pr_ref, o_ref, lse_ref,
                *, t, span, grp, bounded):
    i = pl.program_id(2)
    q4 = q_ref[0]
    lane = lax.broadcasted_iota(jnp.int32, (1, LANES), 1)
    lo_half = lane < HEAD_DIM
    per = t // LANES
    prev = jnp.maximum(i * per - 1, 0)
    o_prev = pl.multiple_of(prev * LANES, LANES)
    o_cur = pl.multiple_of(i * t, t)
    kk = jnp.concatenate([k_ref[0, pl.ds(o_prev, LANES), :], k_ref[0, pl.ds(o_cur, t), :]], axis=0)
    vv = jnp.concatenate([v_ref[0, pl.ds(o_prev, LANES), :], v_ref[0, pl.ds(o_cur, t), :]], axis=0)
    pk = jnp.concatenate([pr_ref[0, prev]] + [pr_ref[0, i * per + c] for c in range(per)], axis=1)
    dist = (pc_ref[0] - pk) * LOG2E
    rows = i * t + lax.broadcasted_iota(jnp.int32, (t, 1), 0)
    cols = i * t - LANES + lax.broadcasted_iota(jnp.int32, (1, t + LANES), 1)
    d = rows - cols
    valid = (cols >= 0) & (d >= 0) & (d <= span)
    zero = jnp.zeros((t, LANES), BF16)
    ones = jnp.ones((t + LANES, LANES), BF16)
    for p in range(2):
        qp = q4[:, p * LANES:(p + 1) * LANES]
        kp = kk[:, p * LANES:(p + 1) * LANES]
        vx = jnp.concatenate([vv[:, p * LANES:(p + 1) * LANES], ones], axis=1)
        res = []
        for x in range(2):
            qx = jnp.where(lo_half, qp, zero) if x == 0 else jnp.where(lo_half, zero, qp)
            s = _dot_nt(qx, kp) - slope_ref[grp * DIL_HEADS + 2 * p + x] * dist
            s = jnp.where(valid, s, -jnp.inf)
            mx = bnd_ref[0] if bounded else jnp.max(s, axis=-1, keepdims=True)
            acc = _dot(jnp.exp2(s - mx).astype(BF16), vx)
            den = acc[:, LANES:2 * LANES]
            res.append((acc[:, 0:LANES] / den, mx * LN2 + jnp.log(den)))
        sl = slice(p * LANES, (p + 1) * LANES)
        o_ref[0, :, sl] = _pair_select(lo_half, res[0][0], res[1][0])
        lse_ref[0, :, sl] = jnp.where(lo_half, res[0][1], res[1][1])


def _dilated(slopes, bound, safe, qv, kv, vv, pos_f, grp):
    window, dil = DIL_CFG[grp]
    b, sub, wd = qv.shape
    w = wd // dil
    t = min(T_DIL, sub)
    span = window // dil
    assert span <= LANES and t % LANES == 0
    pres = pos_f.reshape(sub, dil).T
    pc = pres.reshape(dil, sub, 1)
    pr = pres.reshape(dil, sub // LANES, 1, LANES)
    shp = jax.ShapeDtypeStruct((b, sub, wd), F32)
    smem = pl.BlockSpec(memory_space=pltpu.SMEM)

    def call(bounded):
        kern = functools.partial(_dil_kernel, t=t, span=span, grp=grp, bounded=bounded)
        return pl.pallas_call(
            kern,
            grid=(b, dil, sub // t),
            in_specs=[smem, smem,
                      pl.BlockSpec((1, t, w), lambda i, r, j: (i, j, r)),
                      pl.BlockSpec((1, sub, w), lambda i, r, j: (i, 0, r)),
                      pl.BlockSpec((1, sub, w), lambda i, r, j: (i, 0, r)),
                      pl.BlockSpec((1, t, 1), lambda i, r, j: (r, j, 0)),
                      pl.BlockSpec((1, sub // LANES, 1, LANES), lambda i, r, j: (r, 0, 0, 0))],
            out_specs=[pl.BlockSpec((1, t, w), lambda i, r, j: (i, j, r)),
                       pl.BlockSpec((1, t, w), lambda i, r, j: (i, j, r))],
            out_shape=[shp, shp],
            compiler_params=_cparams(3),
            name="dilated_attn_g%d%s" % (grp, "_bounded" if bounded else ""),
        )(slopes, bound, qv, kv, vv, pc, pr)

    return lax.cond(safe, lambda: call(True), lambda: call(False))


def _out_odd_kernel(x_ref, mod_ref, oc_ref, os_ref, ow_ref, d0_ref, d1_ref, d2_ref,
                    l0_ref, l1_ref, l2_ref, gz_ref, w_ref, o_ref, st):
    ts = x_ref.shape[1]
    gz = gz_ref[0].astype(F32)
    nsa = oc_ref[0] + os_ref[0] + ow_ref[0]
    m1 = (nsa * gz[:, 0:NSA_W]).astype(BF16)

    def token_order(ref, g, slot):
        dil = DIL_CFG[g][1]
        if dil == 1:
            return ref[0]
        for r in range(dil):
            for h in range(2):
                st[slot, h, pl.ds(r, ts // dil, stride=dil), :] = ref[0, :, r * DIL_W + h * 128:r * DIL_W + (h + 1) * 128]
        return jnp.concatenate([st[slot, 0], st[slot, 1]], axis=1)

    d0, d1, d2 = token_order(d0_ref, 0, 0), token_order(d1_ref, 1, 0), token_order(d2_ref, 2, 1)
    l0, l1, l2 = token_order(l0_ref, 0, 2), token_order(l1_ref, 1, 2), token_order(l2_ref, 2, 3)
    mx = jnp.maximum(jnp.maximum(l0, l1), l2)
    e0, e1, e2 = jnp.exp(l0 - mx), jnp.exp(l1 - mx), jnp.exp(l2 - mx)
    dil = (e0 * d0 + e1 * d1 + e2 * d2) / (e0 + e1 + e2)
    m2 = (dil * gz[:, NSA_W:NSA_W + DIL_W]).astype(BF16)
    y = _dot(m1, w_ref[0:NSA_W, :]) + _dot(m2, w_ref[NSA_W:NSA_W + DIL_W, :])
    gate = mod_ref[0][:, 2 * D_MODEL:3 * D_MODEL]
    o_ref[0] = x_ref[0] + gate * y


def _out_odd(x, mod3, oc, os_, ow, dils, lses, gz, w):
    b, s, d = x.shape
    ts = min(TS_PROJ, s)
    row = lambda n: pl.BlockSpec((1, ts, n), lambda i, j: (i, j, 0))
    res = [pl.BlockSpec((1, ts // DIL_CFG[g][1], DIL_CFG[g][1] * DIL_W), lambda i, j: (i, j, 0))
           for g in range(N_DIL)]
    return pl.pallas_call(
        _out_odd_kernel,
        grid=(b, s // ts),
        in_specs=[row(d), pl.BlockSpec((1, 1, 3 * d), lambda i, j: (i, 0, 0)),
                  row(NSA_W), row(NSA_W), row(NSA_W)] + res + res + [row(1024), _full(w.shape)],
        out_specs=row(d),
        out_shape=jax.ShapeDtypeStruct((b, s, d), F32),
        scratch_shapes=[pltpu.VMEM((4, 2, ts, 128), F32)],
        compiler_params=_cparams(2),
        name="outproj_odd",
    )(x, mod3, oc, os_, ow, *dils, *lses, gz, w)


def _pad_vec(v, n=D_MODEL):
    return jnp.pad(v, (0, n - v.shape[0]))


def _group_matrix(sizes, total):
    m = np.zeros((total, total), np.float32)
    off = 0
    for sz, on in sizes:
        if on:
            m[off:off + sz, off:off + sz] = 1.0
        off += sz
    return jnp.asarray(m, BF16)


def _swap_halves(w):
    h = w.shape[-1] // 2
    return jnp.concatenate([w[..., h:], w[..., :h]], axis=-1)


def _column_blocks(segments):
    blocks, cur, room = [], [], LANES
    for src, width in segments:
        while width:
            n = min(width, room)
            cur.append((src, n))
            src = None if src is None else src + n
            width -= n
            room -= n
            if room == 0:
                blocks.append(cur)
                cur, room = [], LANES
    assert not cur
    return blocks


def _pack_cols_kernel(w_ref, o_ref, *, blocks):
    _, rows, n_src = w_ref.shape
    for bi, pieces in enumerate(blocks):
        parts = []
        for src, width in pieces:
            if src is None:
                parts.append(jnp.zeros((rows, width), F32))
                continue
            a0 = src // LANES * LANES
            a1 = min(-(-(src + width) // LANES) * LANES, n_src)
            parts.append(w_ref[0, :, a0:a1][:, src - a0:src - a0 + width])
        blk = parts[0] if len(parts) == 1 else jnp.concatenate(parts, axis=1)
        o_ref[:, bi * LANES:(bi + 1) * LANES] = blk.astype(BF16)


def _pack_cols(w, layer, segments):
    blocks = _column_blocks(segments)
    _, d, n_src = w.shape
    tr = 256
    return pl.pallas_call(
        functools.partial(_pack_cols_kernel, blocks=blocks),
        grid=(d // tr,),
        in_specs=[pl.BlockSpec((1, tr, n_src), lambda i: (layer, i, 0))],
        out_specs=pl.BlockSpec((tr, len(blocks) * LANES), lambda i: (i, 0)),
        out_shape=jax.ShapeDtypeStruct((d, len(blocks) * LANES), BF16),
        compiler_params=_cparams(1),
        name="pack_weight_columns",
    )(w)


EVEN_SEGMENTS = ((0, 1536), (1536, 512), (2464, 512), (2048, 256), (2304, 128),
                 (None, 64), (2432, 32), (None, 32), (None, 64), (2448, 16), (2432, 16), (None, 32))


def _pack_even(w_in_all, layer, norm_g, sb_qn, sb_kn, qa_g, wq_up, kva_g, wkv_up, qn, kn):
    w = _pack_cols(w_in_all, layer, EVEN_SEGMENTS)
    wq3 = wq_up.reshape(MLA_Q_RANK, MLA_HEADS, MLA_NOPE + MLA_ROPE)
    zq = jnp.zeros((MLA_Q_RANK, MLA_HEADS, 32), wq_up.dtype)
    wq = jnp.concatenate([wq3, zq], axis=-1).reshape(MLA_Q_RANK, MLA_HEADS * 128).astype(BF16)
    wqs = jnp.concatenate([jnp.zeros((MLA_Q_RANK, MLA_HEADS, 64), wq_up.dtype),
                           _swap_halves(wq3[..., MLA_NOPE:]), zq], axis=-1)
    wqs = wqs.reshape(MLA_Q_RANK, MLA_HEADS * 128).astype(BF16)
    wkv3 = wkv_up.reshape(MLA_KV_RANK, MLA_HEADS, MLA_NOPE + MLA_V)
    wk = jnp.concatenate([wkv3[..., :MLA_NOPE], jnp.zeros_like(wkv3[..., :MLA_NOPE])], axis=-1)
    wkv = jnp.concatenate([wk.reshape(MLA_KV_RANK, MLA_HEADS * 128),
                           wkv3[..., MLA_NOPE:].reshape(MLA_KV_RANK, MLA_HEADS * MLA_V)], axis=1).astype(BF16)
    z32 = jnp.zeros((32,), F32)
    z64 = jnp.zeros((64,), F32)
    scale = (MLA_NOPE + MLA_ROPE) ** -0.5 * LOG2E
    qg = jnp.tile(jnp.concatenate([qn, z32]), MLA_HEADS) * scale
    qgs = jnp.tile(jnp.concatenate([z64, _swap_halves(qn[MLA_NOPE:]), z32]), MLA_HEADS) * scale
    kg = jnp.tile(jnp.concatenate([kn[:MLA_NOPE], z64]), MLA_HEADS)
    krg = jnp.concatenate([z64, kn[MLA_NOPE:], z32])
    krgs = jnp.concatenate([z64, _swap_halves(kn[MLA_NOPE:]), z32])
    cnt = jnp.tile(jnp.concatenate([jnp.full((64,), 1.0 / 64), jnp.full((32,), 1.0 / 32), jnp.ones((32,))]),
                   MLA_HEADS).astype(F32)
    rows = [norm_g, _pad_vec(jnp.tile(sb_qn, SB_HEADS) * (LOG2E * HEAD_DIM ** -0.5)), _pad_vec(jnp.tile(sb_kn, SB_HEADS)),
            _pad_vec(qa_g), _pad_vec(kva_g), qg, qgs, kg, _pad_vec(krg), _pad_vec(krgs), cnt]
    rows += [jnp.zeros((D_MODEL,), F32)] * (16 - len(rows))
    return w, wq, wqs, wkv, jnp.stack(rows).astype(F32)


def _padded_heads(start, n):
    return tuple(seg for h in range(n) for seg in ((start + h * HEAD_DIM, HEAD_DIM), (None, HEAD_DIM)))


NSA_SEGMENTS = (_padded_heads(0, NSA_HEADS)
                + tuple(seg for c in range(6) for seg in _padded_heads(768 + c * NSA_KV_W, NSA_KV_HEADS))
                + tuple(seg for g in range(NSA_KV_HEADS)
                        for seg in ((1920 + g * NSA_HPG * 3, NSA_HPG * 3), (None, LANES - NSA_HPG * 3)))
                + ((1956, NSA_W), (5028, DIL_W)))
DIL_SEGMENTS = ((2724, 3 * N_DIL * DIL_W),)


def _pack_odd(w_in_all, layer, norm_g, nsa_qn, nsa_kn, dil_qn, dil_kn):
    w_nsa = _pack_cols(w_in_all, layer, NSA_SEGMENTS)
    w_dil = _pack_cols(w_in_all, layer, DIL_SEGMENTS)
    z64 = jnp.zeros((64,), F32)
    n = NSA_HEADS * 128
    rows = [_pad_vec(norm_g, n),
            jnp.tile(jnp.concatenate([nsa_qn * (LOG2E * HEAD_DIM ** -0.5), z64]), NSA_HEADS),
            _pad_vec(jnp.tile(jnp.concatenate([nsa_kn, z64]), NSA_KV_HEADS), n),
            _pad_vec(jnp.tile(dil_qn, DIL_HEADS) * (LOG2E * HEAD_DIM ** -0.5), n),
            _pad_vec(jnp.tile(dil_kn, DIL_HEADS), n)]
    rows += [jnp.zeros((n,), F32)] * (8 - len(rows))
    return w_nsa, w_dil, jnp.stack(rows).astype(F32)


def _pack_compress(w, pe, double):
    w3 = w.reshape(CMP_LEN, HEAD_DIM, HEAD_DIM)
    w3 = jnp.concatenate([w3, jnp.zeros_like(w3)], axis=1)
    w3 = jnp.concatenate([w3, w3 if double else jnp.zeros_like(w3)], axis=2)
    wp = w3.reshape(2, CMP_STRIDE * 128, 128).astype(BF16)
    pe2 = jnp.concatenate([pe, jnp.zeros_like(pe)], axis=1).reshape(2, 1, CMP_STRIDE * 128)
    pe2 = jnp.broadcast_to(pe2, (2, 8, CMP_STRIDE * 128)).astype(F32)
    return wp, pe2


def _normed_len(gain, sizes):
    tot, off = 0.0, 0
    for n in sizes:
        tot = tot + n * jnp.max(jnp.square(gain[off:off + n]))
        off += n
    return jnp.sqrt(tot)


def _mla_logit_bound(qn, kn):
    sizes = (MLA_NOPE, MLA_ROPE)
    scale = (MLA_NOPE + MLA_ROPE) ** -0.5 * LOG2E
    return (_normed_len(qn, sizes) * _normed_len(kn, sizes) * (scale * ROUNDING_MARGIN)).reshape(1).astype(F32)


def _alibi_slopes(n):
    return 2.0 ** (-8.0 * jnp.arange(1, n + 1, dtype=jnp.float32) / n)


def kernel(x, c, positions, ada_w, ada_b, norm_g, ev_w_in, ev_w_out, sb_qn, sb_kn, mla_qa_g, mla_wq_up,
           mla_kva_g, mla_wkv_up, mla_qn, mla_kn, od_w_in, od_w_out, nsa_qn, nsa_kn, nsa_cmp_wk, nsa_cmp_wv,
           nsa_cmp_pe_k, nsa_cmp_pe_v, dil_qn, dil_kn):
    b, s, d = x.shape
    depth = ada_w.shape[0]
    pos_f = positions.astype(F32)

    inv_freq = ROPE_BASE ** (-jnp.arange(0, MLA_ROPE, 2, dtype=F32) / MLA_ROPE)
    ang = pos_f[:, None] * inv_freq[None, :]
    cos, sin = jnp.cos(ang), jnp.sin(ang)
    cos128 = jnp.concatenate([jnp.ones((s, 64), F32), cos, cos, jnp.zeros((s, 32), F32)], axis=1)
    sin128 = jnp.concatenate([jnp.zeros((s, 64), F32), -sin, sin, jnp.zeros((s, 32), F32)], axis=1)
    nsa_slopes = _alibi_slopes(NSA_HEADS)
    dil_slopes = _alibi_slopes(N_DIL * DIL_HEADS)
    nch = s // CMP_STRIDE
    chunk_sum = pos_f.reshape(nch, CMP_STRIDE).sum(axis=1)
    cpos = ((chunk_sum + jnp.roll(chunk_sum, -1)) / CMP_LEN).reshape(nch, 1)
    n_sel = s // SEL_LEN
    cst = np.arange(nch)[:, None] * CMP_STRIDE
    jst = np.arange(LANES)[None, :] * SEL_LEN
    ovl = ((cst <= jst + SEL_LEN - 1) & (cst + CMP_LEN - 1 >= jst) & (np.arange(LANES)[None, :] < n_sel))
    ovlt = jnp.asarray(ovl.astype(np.float32).T, BF16)
    pos_col = pos_f.reshape(s, 1)
    tqc = min(TQ_CMP, s)
    pos_rows = pos_f.reshape(s // tqc, 1, tqc)
    tqn = min(TQ_NSA, s)
    tkn = min(TK_NSA, tqn)
    assert tkn == TK_NSA
    pos2d = pos_f.reshape(s // tkn, 1, tkn)
    pos2d_win = pos_f.reshape(s // TK_WIN, 1, TK_WIN)
    q_first = pos_f[::tqn]
    k_last = pos_f[jnp.minimum((jnp.arange(SEL_TILE_STRIDE) + 1) * tkn - 1, s - 1)]
    group_slope = jnp.min(nsa_slopes.reshape(NSA_KV_HEADS, NSA_HPG), axis=1) * LOG2E
    sel_dead = group_slope[:, None, None] * (q_first[:, None] - k_last[None, :])[None] > F32_ZERO_EXP
    pos_sorted = jnp.all(pos_f[1:] >= pos_f[:-1])

    g64 = _group_matrix([(64, 1)] * 4, 256)
    gm = _group_matrix([(64, 1), (32, 1), (32, 0)] * 2, 256)
    tks = min(TK_SB, min(TQ_SB, s))
    tri = jnp.asarray(np.tril(np.ones((tks, tks), np.float32)), BF16)

    c8 = jnp.pad(c, ((0, 8 - b), (0, 0)))
    mod_all = _modulation(c8, ada_w, ada_b)

    for layer in range(depth):
        j = layer // 2
        mod3 = mod_all[layer, :b].reshape(b, 1, 3 * d)
        if layer % 2 == 0:
            w, wq, wqs, wkv, vec = _pack_even(ev_w_in, j, norm_g[layer], sb_qn[j], sb_kn[j], mla_qa_g[j],
                                              mla_wq_up[j], mla_kva_g[j], mla_wkv_up[j], mla_qn[j], mla_kn[j])
            sbq, sbk, sbv, gz, mq, mk, mv = _k1_even(x, mod3, vec, w, wq, wqs, wkv, g64, gm, cos128, sin128)
            sb_bound = (_normed_len(sb_qn[j], (HEAD_DIM,)) * _normed_len(sb_kn[j], (HEAD_DIM,))
                        * (HEAD_DIM ** -0.5 * LOG2E * ROUNDING_MARGIN)).reshape(1).astype(F32)
            o_sb = _sb_attention(sbq, sbk, sbv, tri, sb_bound, sb_bound[0] <= SAFE_LOGIT_BOUND)
            o_mla = _mla_attention(mq, mk, mv, _mla_logit_bound(mla_qn[j], mla_kn[j]))
            x = _out_even(x, mod3, o_sb, o_mla, gz, ev_w_out[j].astype(BF16))
        else:
            w_nsa, w_dil, vec = _pack_odd(od_w_in, j, norm_g[layer], nsa_qn[j], nsa_kn[j], dil_qn[j], dil_kn[j])
            q, ck, cv, sk, sv, wk, wv, gates, gz = _k1_nsa(x, mod3, vec, w_nsa, g64)
            dqkv = _k1_dil(x, mod3, vec, w_dil, g64)
            wck, pek = _pack_compress(nsa_cmp_wk[j], nsa_cmp_pe_k[j], False)
            wcv, pev = _pack_compress(nsa_cmp_wv[j], nsa_cmp_pe_v[j], True)
            kgain = jnp.concatenate([nsa_kn[j], jnp.zeros((64,), F32)]).reshape(1, 128)
            kc, vc = _compress(ck, cv, wck, wcv, pek, pev, kgain, g64[0:128, 0:128])
            o_c, selb, used = _nsa_cmp(nsa_slopes, q, kc, vc, ovlt, cpos, pos_rows, gates)
            flags = used[:, :, :, 0, :].reshape(b, NSA_KV_HEADS, s // tqn, tqn // min(TQ_CMP, s),
                                                SEL_TILE_STRIDE, LANES // SEL_TILE_STRIDE).max(axis=(3, 5))
            nsa_bound = (_normed_len(nsa_qn[j], (HEAD_DIM,)) * _normed_len(nsa_kn[j], (HEAD_DIM,))
                         * (HEAD_DIM ** -0.5 * LOG2E * ROUNDING_MARGIN)).reshape(1).astype(F32)
            safe = (nsa_bound[0] <= SAFE_LOGIT_BOUND) & pos_sorted
            flags = (flags > 0) & ~(safe & sel_dead[None])
            flags = flags.astype(jnp.int32).reshape(-1)
            o_s = _nsa_gqa(flags, nsa_slopes, nsa_bound, safe, q, selb, sk, sv, pos2d, pos_col, gates, 1)
            o_w = _nsa_gqa(flags, nsa_slopes, nsa_bound, safe, q, selb, wk, wv, pos2d_win, pos_col, gates, 2)
            dil_bound = (_normed_len(dil_qn[j], (HEAD_DIM,)) * _normed_len(dil_kn[j], (HEAD_DIM,))
                         * (HEAD_DIM ** -0.5 * LOG2E * ROUNDING_MARGIN)).reshape(1).astype(F32)
            dil_safe = (dil_bound[0] <= SAFE_LOGIT_BOUND) & pos_sorted
            dils, lses = [], []
            for g in range(N_DIL):
                o, lse = _dilated(dil_slopes, dil_bound, dil_safe, dqkv[3 * g], dqkv[3 * g + 1], dqkv[3 * g + 2],
                                  pos_f, g)
                dils.append(o)
                lses.append(lse)
            x = _out_odd(x, mod3, o_c, o_s, o_w, dils, lses, gz, od_w_out[j].astype(BF16))
    return x
```

```python
import functools

import numpy as np
import jax
import jax.numpy as jnp
from jax import lax
from jax.experimental import pallas as pl
from jax.experimental.pallas import tpu as pltpu

F32 = jnp.float32
BF16 = jnp.bfloat16

D_MODEL = 1024
HEAD_DIM = 64
NORM_EPS = 1e-6
TINY = 1e-30
SB_HEADS = 8
MLA_HEADS = 8
MLA_Q_RANK = 256
MLA_KV_RANK = 128
MLA_NOPE = 64
MLA_ROPE = 32
MLA_V = 64
ROPE_BASE = 10000.0
NSA_HEADS = 12
NSA_KV_HEADS = 3
NSA_HPG = 4
CMP_LEN = 32
CMP_STRIDE = 16
SEL_LEN = 64
SEL_TOPN = 16
WIN = 512
FORCE_BONUS = 1e3
DIL_CFG = ((128, 1), (512, 4), (2048, 16))
N_DIL = 3
DIL_HEADS = 4
SB_W = SB_HEADS * HEAD_DIM
MLA_OUT = MLA_HEADS * MLA_V
NSA_W = NSA_HEADS * HEAD_DIM
NSA_KV_W = NSA_KV_HEADS * HEAD_DIM
DIL_W = DIL_HEADS * HEAD_DIM

LANES = 128
NEG_MASK = -1e30
LOG2E = 1.4426950408889634
LN2 = 0.6931471805599453
F32_ZERO_EXP = 150.0
SAFE_LOGIT_BOUND = 50.0
ROUNDING_MARGIN = 1.02
SEL_OFF = 2.0 ** 30
VMEM_LIMIT = 56 * 1024 * 1024

TS_PROJ = 512
TQ_SB = 512
TK_SB = 256
TQ_MLA = 1024
TD_MLA = 256
TQ_CMP = 512
SEL_ROWS = 32
TQ_NSA = 512
TK_NSA = 512
TK_WIN = 256
SEL_TILE_STRIDE = LANES * SEL_LEN // TK_NSA
T_DIL = 256


def _dot(a, b):
    return jnp.dot(a, b, preferred_element_type=F32)


def _dot_nt(a, b):
    return lax.dot_general(a, b, (((1,), (1,)), ((), ())), preferred_element_type=F32)


def _split_hl(a):
    hi = a.astype(BF16)
    lo = (a - hi.astype(F32)).astype(BF16)
    return hi, lo


def _dot_hl(a, b):
    hi, lo = _split_hl(a)
    return _dot(hi, b) + _dot(lo, b)


def _sigmoid(z):
    return 1.0 / (1.0 + jnp.exp(-z))


def _cparams(n_axes):
    return pltpu.CompilerParams(dimension_semantics=("arbitrary",) * n_axes,
                                vmem_limit_bytes=VMEM_LIMIT)


def _full(shape):
    n = len(shape)
    return pl.BlockSpec(shape, lambda *a, _n=n: (0,) * _n)


def _mod_kernel(c_ref, w_ref, b_ref, o_ref):
    c = c_ref[...]
    a = c * _sigmoid(c)
    ah, al = _split_hl(a)
    wh, wl = _split_hl(w_ref[0])
    o_ref[0] = _dot(ah, wh) + _dot(ah, wl) + _dot(al, wh) + b_ref[0]


def _modulation(c8, ada_w, ada_b):
    depth, d, n3 = ada_w.shape
    tn = 1024
    return pl.pallas_call(
        _mod_kernel,
        grid=(depth, n3 // tn),
        in_specs=[pl.BlockSpec((8, d), lambda l, j: (0, 0)),
                  pl.BlockSpec((1, d, tn), lambda l, j: (l, 0, j)),
                  pl.BlockSpec((1, 1, tn), lambda l, j: (l, 0, j))],
        out_specs=pl.BlockSpec((1, 8, tn), lambda l, j: (l, 0, j)),
        out_shape=jax.ShapeDtypeStruct((depth, 8, n3), F32),
        compiler_params=_cparams(2),
        name="adaln_mod",
    )(c8, ada_w, ada_b.reshape(depth, 1, n3))


def _modulated(x_ref, mod_ref, ng):
    x = x_ref[0]
    mod = mod_ref[0]
    shift = mod[:, 0:D_MODEL]
    scale = mod[:, D_MODEL:2 * D_MODEL]
    ms = jnp.mean(x * x, axis=-1, keepdims=True)
    h = x * lax.rsqrt(ms + NORM_EPS) * ng
    h = h * (1.0 + scale) + shift
    return h.astype(BF16)


def _group_sumsq(t, g):
    return _dot((t * t).astype(BF16), g)


def _group_norm_chunk(t, g, inv_cnt):
    return t * lax.rsqrt(_group_sumsq(t, g) * inv_cnt + NORM_EPS)


def _row_rms(t, gain):
    ms = jnp.mean(t * t, axis=-1, keepdims=True)
    return t * lax.rsqrt(ms + NORM_EPS) * gain


EV_COLS = 3200


def _k1_even_kernel(x_ref, mod_ref, vec_ref, w_ref, wq_ref, wqs_ref, wkv_ref, g64_ref, gm_ref,
                    cos_ref, sin_ref, sbq_o, sbk_o, sbv_o, gz_o, mq_o, mk_o, mv_o):
    hb = _modulated(x_ref, mod_ref, vec_ref[0:1, :])
    g64 = g64_ref[...]
    gm = gm_ref[...]
    inv64 = 1.0 / HEAD_DIM

    u = _dot(hb, w_ref[:, 0:1536])
    for c in range(2):
        sl = slice(c * 256, (c + 1) * 256)
        qn = _group_norm_chunk(u[:, c * 256:(c + 1) * 256], g64, inv64)
        sbq_o[0, :, sl] = (qn * vec_ref[1:2, sl]).astype(BF16)
        kn = _group_norm_chunk(u[:, 512 + c * 256:512 + (c + 1) * 256], g64, inv64)
        sbk_o[0, :, sl] = (kn * vec_ref[2:3, sl]).astype(BF16)
    sbv_o[0] = u[:, 1024:1536].astype(BF16)

    uz = _dot(hb, w_ref[:, 1536:2560])
    gz_o[0] = (uz * _sigmoid(uz)).astype(BF16)

    ul = _dot(hb, w_ref[:, 2560:3200])
    qlat = _row_rms(ul[:, 0:256], vec_ref[3:4, 0:256]).astype(BF16)
    kvlat = _row_rms(ul[:, 256:384], vec_ref[4:5, 0:128]).astype(BF16)
    krm = ul[:, 384:512]
    krs = ul[:, 512:640]
    cs = cos_ref[...]
    sn = sin_ref[...]
    cs2 = jnp.concatenate([cs, cs], axis=1)
    sn2 = jnp.concatenate([sn, sn], axis=1)

    tq = _dot(qlat, wq_ref[...])
    tqs = _dot(qlat, wqs_ref[...])
    for p in range(4):
        sl = slice(p * 256, (p + 1) * 256)
        tc = tq[:, p * 256:(p + 1) * 256]
        ss = _group_sumsq(tc, gm)
        inv = lax.rsqrt(ss * vec_ref[10:11, sl] + NORM_EPS)
        a = tc * inv * vec_ref[5:6, sl]
        b = tqs[:, p * 256:(p + 1) * 256] * inv * vec_ref[6:7, sl]
        mq_o[0, :, sl] = (a * cs2 + b * sn2).astype(BF16)

    sskr = _group_sumsq(krm, gm[0:128, 0:128])
    invr = lax.rsqrt(sskr * vec_ref[10:11, 0:128] + NORM_EPS)
    kr = (krm * invr * vec_ref[8:9, 0:128]) * cs + (krs * invr * vec_ref[9:10, 0:128]) * sn
    kr2 = jnp.concatenate([kr, kr], axis=1)

    kv = _dot(kvlat, wkv_ref[...])
    for p in range(4):
        sl = slice(p * 256, (p + 1) * 256)
        kn = _group_norm_chunk(kv[:, p * 256:(p + 1) * 256], g64, inv64)
        mk_o[0, :, sl] = (kn * vec_ref[7:8, sl] + kr2).astype(BF16)
    mv_o[0] = kv[:, 1024:1536].astype(BF16)


def _k1_even(x, mod3, vec, w, wq, wqs, wkv, g64, gm, cos128, sin128):
    b, s, d = x.shape
    ts = min(TS_PROJ, s)
    row = lambda n: pl.BlockSpec((1, ts, n), lambda i, j: (i, j, 0))
    outs = [(512, BF16), (512, BF16), (512, BF16), (1024, BF16), (1024, BF16), (1024, BF16), (512, BF16)]
    return pl.pallas_call(
        _k1_even_kernel,
        grid=(b, s // ts),
        in_specs=[row(d),
                  pl.BlockSpec((1, 1, 3 * d), lambda i, j: (i, 0, 0)),
                  _full(vec.shape), _full(w.shape), _full(wq.shape), _full(wqs.shape), _full(wkv.shape),
                  _full(g64.shape), _full(gm.shape),
                  pl.BlockSpec((ts, 128), lambda i, j: (j, 0)),
                  pl.BlockSpec((ts, 128), lambda i, j: (j, 0))],
        out_specs=[row(n) for n, _ in outs],
        out_shape=[jax.ShapeDtypeStruct((b, s, n), dt) for n, dt in outs],
        compiler_params=_cparams(2),
        name="inproj_even",
    )(x, mod3, vec, w, wq, wqs, wkv, g64, gm, cos128, sin128)


def _sb_kernel(bnd_ref, q_ref, k_ref, v_ref, tri_ref, o_ref, acc_ref, car_ref, *, tq, tk, bounded):
    qi = pl.program_id(2)
    q = q_ref[0]
    lane = lax.broadcasted_iota(jnp.int32, (1, LANES), 1)
    lo_half = lane < HEAD_DIM
    zero = jnp.zeros_like(q)
    qs = (jnp.where(lo_half, q, zero), jnp.where(lo_half, zero, q))
    tri = tri_ref[...]
    acc_ref[...] = jnp.zeros_like(acc_ref)
    car_ref[...] = jnp.zeros_like(car_ref)
    col0 = lax.broadcasted_iota(jnp.int32, (1, tk), 1)

    def tile(kt, r0, masked):
        n = tq - r0
        off = pl.multiple_of(kt * tk, tk)
        k = k_ref[0, pl.ds(off, tk), :]
        v = v_ref[0, pl.ds(off, tk), :]
        if masked:
            rows = qi * tq + r0 + lax.broadcasted_iota(jnp.int32, (n, 1), 0)
            strict = (kt * tk + col0) < rows
        for i in range(2):
            z = _dot_nt(qs[i][r0:tq], k)
            if bounded:
                lom = jnp.log(1.0 + jnp.exp2(z)) * (-LOG2E)
            else:
                lom = -jnp.maximum(z, 0.0) - jnp.log(1.0 + jnp.exp2(-jnp.abs(z))) * LOG2E
            if masked:
                lom = jnp.where(strict, lom, 0.0)
            tt = _dot(lom.astype(BF16), tri)
            car = car_ref[i, r0:tq]
            w = jnp.exp2(tt + z + jnp.concatenate([car] * (tk // LANES), axis=1))
            if masked:
                w = jnp.where(strict, w, 0.0)
            acc_ref[i, r0:tq] += _dot(w.astype(BF16), v)
            car_ref[i, r0:tq] = car + jnp.broadcast_to(tt[:, 0:1], (n, LANES))

    per = tq // tk
    for c in reversed(range(per)):
        tile(qi * per + c, c * tk, True)

    if bounded:
        limit = -(F32_ZERO_EXP + bnd_ref[0])

        def still_live():
            return jnp.max(jnp.maximum(car_ref[0], car_ref[1])) > limit

        def single(c):
            j, _ = c
            tile(qi * per - 1 - j, 0, False)
            return j + 1, still_live()

        _, live = lax.while_loop(lambda c: (c[0] < per) & c[1], single, (0, qi > 0))

        def group(c):
            g, _ = c
            for u in range(per):
                tile(qi * per - 1 - per * (g + 1) - u, 0, False)
            return g + 1, still_live()

        lax.while_loop(lambda c: (c[0] < qi - 1) & c[1], group, (0, live))
    else:
        def body(j, carry):
            for u in range(per):
                tile(qi * per - 1 - per * j - u, 0, False)
            return carry

        lax.fori_loop(0, qi, body, 0)
    o_ref[0] = jnp.where(lo_half, acc_ref[0], acc_ref[1])


def _sb_attention(q, k, v, tri, bound, safe):
    b, s, w = q.shape
    tq = min(TQ_SB, s)
    tk = min(TK_SB, tq)

    def call(bounded):
        kern = functools.partial(_sb_kernel, tq=tq, tk=tk, bounded=bounded)
        return pl.pallas_call(
            kern,
            grid=(b, w // LANES, s // tq),
            in_specs=[pl.BlockSpec(memory_space=pltpu.SMEM),
                      pl.BlockSpec((1, tq, LANES), lambda i, p, j: (i, j, p)),
                      pl.BlockSpec((1, s, LANES), lambda i, p, j: (i, 0, p)),
                      pl.BlockSpec((1, s, LANES), lambda i, p, j: (i, 0, p)),
                      _full(tri.shape)],
            out_specs=pl.BlockSpec((1, tq, LANES), lambda i, p, j: (i, j, p)),
            out_shape=jax.ShapeDtypeStruct((b, s, w), F32),
            scratch_shapes=[pltpu.VMEM((2, tq, LANES), F32), pltpu.VMEM((2, tq, LANES), F32)],
            compiler_params=_cparams(3),
            name="stickbreak_attn_bounded" if bounded else "stickbreak_attn",
        )(bound, q, k, v, tri)

    return lax.cond(safe, lambda: call(True), lambda: call(False))


def _lane_max(s):
    m = s[:, 0:LANES]
    for c in range(1, s.shape[1] // LANES):
        m = jnp.maximum(m, s[:, c * LANES:(c + 1) * LANES])
    return m


def _mla_kernel(bnd_ref, q_ref, k_ref, v_ref, o_ref, acc_ref, m_ref, *, tq, td, bounded):
    qi = pl.program_id(2)
    q = q_ref[0]
    qs = (q[:, 0:LANES], q[:, LANES:2 * LANES])
    lane = lax.broadcasted_iota(jnp.int32, (1, LANES), 1)
    lo_half = lane < HEAD_DIM
    acc_ref[...] = jnp.zeros_like(acc_ref)
    if bounded:
        m_ref[...] = jnp.full(m_ref.shape, bnd_ref[0], F32)
    else:
        m_ref[...] = jnp.full_like(m_ref, NEG_MASK)

    def scores(off, width, r0, masked):
        k = k_ref[0, pl.ds(off, width), :]
        if masked:
            rows = qi * tq + r0 + lax.broadcasted_iota(jnp.int32, (tq - r0, 1), 0)
            keep = (off + lax.broadcasted_iota(jnp.int32, (1, width), 1)) <= rows
        out = []
        for i in range(2):
            s = _dot_nt(qs[i][r0:tq], k[:, i * LANES:(i + 1) * LANES])
            if masked:
                s = jnp.where(keep, s, NEG_MASK)
            out.append(s)
        return out

    def sweep(fn):
        def body(j, carry):
            fn(pl.multiple_of(j * tq, tq), tq, 0, False)
            return carry
        lax.fori_loop(0, qi, body, 0)
        for c in range(tq // td):
            fn(pl.multiple_of(qi * tq + c * td, td), td, c * td, True)

    def row_max(off, width, r0, masked):
        for i, s in enumerate(scores(off, width, r0, masked)):
            m_ref[i, r0:tq] = jnp.maximum(m_ref[i, r0:tq], _lane_max(s))

    def accumulate(off, width, r0, masked):
        vx = jnp.concatenate([v_ref[0, pl.ds(off, width), :], jnp.ones((width, LANES), BF16)], axis=1)
        for i, s in enumerate(scores(off, width, r0, masked)):
            p = jnp.exp2(s - jnp.concatenate([m_ref[i, r0:tq]] * (width // LANES), axis=1))
            acc_ref[i, r0:tq] += _dot(p.astype(BF16), vx)

    if not bounded:
        sweep(row_max)
        for i in range(2):
            m_ref[i] = jnp.broadcast_to(jnp.max(m_ref[i], axis=-1, keepdims=True), (tq, LANES))
    sweep(accumulate)
    o_ref[0] = jnp.where(lo_half, acc_ref[0, :, 0:LANES] / acc_ref[0, :, LANES:2 * LANES],
                         acc_ref[1, :, 0:LANES] / acc_ref[1, :, LANES:2 * LANES])


def _mla_attention(q, k, v, bound):
    b, s, _ = q.shape
    tq = min(TQ_MLA, s)
    td = min(TD_MLA, tq)

    def call(bounded):
        kern = functools.partial(_mla_kernel, tq=tq, td=td, bounded=bounded)
        return pl.pallas_call(
            kern,
            grid=(b, MLA_HEADS // 2, s // tq),
            in_specs=[pl.BlockSpec(memory_space=pltpu.SMEM),
                      pl.BlockSpec((1, tq, 2 * LANES), lambda i, p, j: (i, j, p)),
                      pl.BlockSpec((1, s, 2 * LANES), lambda i, p, j: (i, 0, p)),
                      pl.BlockSpec((1, s, LANES), lambda i, p, j: (i, 0, p))],
            out_specs=pl.BlockSpec((1, tq, LANES), lambda i, p, j: (i, j, p)),
            out_shape=jax.ShapeDtypeStruct((b, s, MLA_OUT), F32),
            scratch_shapes=[pltpu.VMEM((2, tq, 2 * LANES), F32), pltpu.VMEM((2, tq, LANES), F32)],
            compiler_params=_cparams(3),
            name="mla_attn_bounded" if bounded else "mla_attn",
        )(bound, q, k, v)

    return lax.cond(bound[0] <= SAFE_LOGIT_BOUND, lambda: call(True), lambda: call(False))


def _out_even_kernel(x_ref, mod_ref, osb_ref, omla_ref, gz_ref, w_ref, o_ref):
    gz = gz_ref[0].astype(F32)
    m1 = (osb_ref[0] * gz[:, 0:SB_W]).astype(BF16)
    m2 = (omla_ref[0] * gz[:, SB_W:SB_W + MLA_OUT]).astype(BF16)
    y = _dot(m1, w_ref[0:SB_W, :]) + _dot(m2, w_ref[SB_W:SB_W + MLA_OUT, :])
    gate = mod_ref[0][:, 2 * D_MODEL:3 * D_MODEL]
    o_ref[0] = x_ref[0] + gate * y


def _out_even(x, mod3, osb, omla, gz, w):
    b, s, d = x.shape
    ts = min(TS_PROJ, s)
    row = lambda n: pl.BlockSpec((1, ts, n), lambda i, j: (i, j, 0))
    return pl.pallas_call(
        _out_even_kernel,
        grid=(b, s // ts),
        in_specs=[row(d), pl.BlockSpec((1, 1, 3 * d), lambda i, j: (i, 0, 0)),
                  row(SB_W), row(MLA_OUT), row(SB_W + MLA_OUT), _full(w.shape)],
        out_specs=row(d),
        out_shape=jax.ShapeDtypeStruct((b, s, d), F32),
        compiler_params=_cparams(2),
        name="outproj_even",
    )(x, mod3, osb, omla, gz, w)


NSA_COLS = 5248


def _k1_nsa_kernel(x_ref, mod_ref, vec_ref, w_ref, g64_ref,
                   q_o, ck_o, cv_o, sk_o, sv_o, wk_o, wv_o, gt_o, gz_o, chunk_ref):
    ts = x_ref.shape[1]
    hb = _modulated(x_ref, mod_ref, vec_ref[0:1, 0:D_MODEL])
    g64 = g64_ref[...]
    inv64 = 1.0 / HEAD_DIM

    uq = _dot(hb, w_ref[:, 0:1536])
    for c in range(6):
        sl = slice(c * 256, (c + 1) * 256)
        qn = _group_norm_chunk(uq[:, c * 256:(c + 1) * 256], g64, inv64)
        q_o[0, :, sl] = (qn * vec_ref[1:2, sl]).astype(BF16)

    uc = _dot(hb, w_ref[:, 1536:2304])
    for c in range(2 * NSA_KV_HEADS):
        chunk_ref[c] = uc[:, c * 128:(c + 1) * 128]
    for g in range(NSA_KV_HEADS):
        for tok in range(CMP_STRIDE):
            rows = pl.ds(tok, ts // CMP_STRIDE, stride=CMP_STRIDE)
            sl = slice(tok * 128, (tok + 1) * 128)
            ck_o[0, g, :, sl] = chunk_ref[g, rows, :].astype(BF16)
            cv_o[0, g, :, sl] = chunk_ref[NSA_KV_HEADS + g, rows, :].astype(BF16)

    us = _dot(hb, w_ref[:, 2304:3840])
    kgain = vec_ref[2:3, 0:384]
    g128 = g64[0:128, 0:128]
    for g in range(NSA_KV_HEADS):
        sl = slice(g * 128, (g + 1) * 128)
        t = us[:, g * 128:(g + 1) * 128]
        ss = _group_sumsq(t, g128)
        sk_o[0, :, sl] = (t * lax.rsqrt(ss * inv64 + NORM_EPS) * kgain[:, sl]).astype(BF16)
        t = us[:, 768 + g * 128:768 + (g + 1) * 128]
        ss = _group_sumsq(t, g128)
        wk_o[0, :, sl] = (t * lax.rsqrt(ss * inv64 + NORM_EPS) * kgain[:, sl]).astype(BF16)
    ones_hi = (lax.broadcasted_iota(jnp.int32, (1, 384), 1) % 128) >= HEAD_DIM
    sv_o[0] = jnp.where(ones_hi, 1.0, us[:, 384:768]).astype(BF16)
    wv_o[0] = jnp.where(ones_hi, 1.0, us[:, 1152:1536]).astype(BF16)

    ug = _dot(hb, w_ref[:, 3840:4224])
    gt_o[0] = _sigmoid(ug)
    uz = _dot(hb, w_ref[:, 4224:5248])
    gz_o[0] = (uz * _sigmoid(uz)).astype(BF16)


def _k1_nsa(x, mod3, vec, w, g64):
    b, s, d = x.shape
    ts = min(TS_PROJ, s)
    row = lambda n: pl.BlockSpec((1, ts, n), lambda i, j: (i, j, 0))
    cw = CMP_STRIDE * 128
    grp = pl.BlockSpec((1, NSA_KV_HEADS, ts // CMP_STRIDE, cw), lambda i, j: (i, 0, j, 0))
    return pl.pallas_call(
        _k1_nsa_kernel,
        grid=(b, s // ts),
        in_specs=[row(d), pl.BlockSpec((1, 1, 3 * d), lambda i, j: (i, 0, 0)),
                  _full(vec.shape), _full(w.shape), _full(g64.shape)],
        out_specs=[row(1536), grp, grp, row(384), row(384), row(384), row(384), row(384), row(1024)],
        out_shape=[jax.ShapeDtypeStruct((b, s, 1536), BF16),
                   jax.ShapeDtypeStruct((b, NSA_KV_HEADS, s // CMP_STRIDE, cw), BF16),
                   jax.ShapeDtypeStruct((b, NSA_KV_HEADS, s // CMP_STRIDE, cw), BF16),
                   jax.ShapeDtypeStruct((b, s, 384), BF16),
                   jax.ShapeDtypeStruct((b, s, 384), BF16),
                   jax.ShapeDtypeStruct((b, s, 384), BF16),
                   jax.ShapeDtypeStruct((b, s, 384), BF16),
                   jax.ShapeDtypeStruct((b, s, 384), F32),
                   jax.ShapeDtypeStruct((b, s, 1024), BF16)],
        scratch_shapes=[pltpu.VMEM((2 * NSA_KV_HEADS, ts, 128), F32)],
        compiler_params=_cparams(2),
        name="inproj_nsa",
    )(x, mod3, vec, w, g64)


def _k1_dil_kernel(x_ref, mod_ref, vec_ref, w_ref, g64_ref, *refs):
    outs, st = refs[:3 * N_DIL], refs[3 * N_DIL]
    ts = x_ref.shape[1]
    hb = _modulated(x_ref, mod_ref, vec_ref[0:1, 0:D_MODEL])
    g64 = g64_ref[...]
    inv64 = 1.0 / HEAD_DIM
    u = _dot(hb, w_ref[...])
    for g in range(N_DIL):
        dil = DIL_CFG[g][1]
        qn = _group_norm_chunk(u[:, g * 256:(g + 1) * 256], g64, inv64) * vec_ref[3:4, 0:256]
        kn = _group_norm_chunk(u[:, 768 + g * 256:768 + (g + 1) * 256], g64, inv64) * vec_ref[4:5, 0:256]
        vals = (qn, kn, u[:, 1536 + g * 256:1536 + (g + 1) * 256])
        for j, val in enumerate(vals):
            o_ref = outs[3 * g + j]
            if dil == 1:
                o_ref[0] = val.astype(BF16)
                continue
            for h in range(2):
                st[j, h] = val[:, h * 128:(h + 1) * 128]
            for r in range(dil):
                for h in range(2):
                    sl = slice(r * 256 + h * 128, r * 256 + (h + 1) * 128)
                    o_ref[0, :, sl] = st[j, h, pl.ds(r, ts // dil, stride=dil), :].astype(BF16)


def _k1_dil(x, mod3, vec, w, g64):
    b, s, d = x.shape
    ts = min(TS_PROJ, s)
    row = lambda n: pl.BlockSpec((1, ts, n), lambda i, j: (i, j, 0))
    specs, shapes = [], []
    for g in range(N_DIL):
        dil = DIL_CFG[g][1]
        for _ in range(3):
            specs.append(pl.BlockSpec((1, ts // dil, dil * 256), lambda i, j: (i, j, 0)))
            shapes.append(jax.ShapeDtypeStruct((b, s // dil, dil * 256), BF16))
    return pl.pallas_call(
        _k1_dil_kernel,
        grid=(b, s // ts),
        in_specs=[row(d), pl.BlockSpec((1, 1, 3 * d), lambda i, j: (i, 0, 0)),
                  _full(vec.shape), _full(w.shape), _full(g64.shape)],
        out_specs=specs,
        out_shape=shapes,
        scratch_shapes=[pltpu.VMEM((3, 2, ts, 128), F32)],
        compiler_params=_cparams(2),
        name="inproj_dil",
    )(x, mod3, vec, w, g64)


def _compress_kernel(xk_ref, xv_ref, wk_ref, wv_ref, pek_ref, pev_ref, gain_ref, g128_ref, kc_o, vc_o):
    nch = xk_ref.shape[2]

    def comp(x_ref, w_ref, pe_ref):
        x = x_ref[0, 0]
        lo = _dot(x, w_ref[0])
        hi = _dot(x, w_ref[1])
        pec = _dot_hl(pe_ref[0], w_ref[0]) + _dot_hl(pe_ref[1], w_ref[1])
        return lo + pltpu.roll(hi, nch - 1, 0) + pec[0:1, :]

    kc = comp(xk_ref, wk_ref, pek_ref)
    ss = _dot_hl(kc * kc, g128_ref[...])
    kc_o[0, 0] = (kc * lax.rsqrt(ss * (1.0 / HEAD_DIM) + NORM_EPS) * gain_ref[...]).astype(BF16)
    vc_o[0, 0] = comp(xv_ref, wv_ref, pev_ref).T.astype(BF16)


def _compress(ck, cv, wk, wv, pek, pev, gain, g128):
    b, g, nch, _ = ck.shape
    xk, xv = ck, cv
    blk = pl.BlockSpec((1, 1, nch, CMP_STRIDE * 128), lambda i, j: (i, j, 0, 0))
    oblk = pl.BlockSpec((1, 1, nch, 128), lambda i, j: (i, j, 0, 0))
    tblk = pl.BlockSpec((1, 1, 128, nch), lambda i, j: (i, j, 0, 0))
    return pl.pallas_call(
        _compress_kernel,
        grid=(b, g),
        in_specs=[blk, blk, _full(wk.shape), _full(wv.shape), _full(pek.shape), _full(pev.shape),
                  _full(gain.shape), _full(g128.shape)],
        out_specs=[oblk, tblk],
        out_shape=[jax.ShapeDtypeStruct((b, g, nch, 128), BF16), jax.ShapeDtypeStruct((b, g, 128, nch), BF16)],
        compiler_params=_cparams(2),
        name="nsa_compress",
    )(xk, xv, wk, wv, pek, pev, gain, g128)


def _pair_select(lo_half, a, b):
    return jnp.where(lo_half, a, b)


def _nsa_cmp_kernel(slope_ref, q_ref, kc_ref, vct_ref, ovlt_ref, cposc_ref, prow_ref, gt_ref,
                    oc_o, selb_o, any_o, imp_ref, *, tq):
    g = pl.program_id(1)
    qi = pl.program_id(2)
    q4 = q_ref[0]
    nch = kc_ref.shape[2]
    t = qi * tq + lax.broadcasted_iota(jnp.int32, (1, tq), 1)
    gtt = gt_ref[0].T
    row_lo = lax.broadcasted_iota(jnp.int32, (LANES, 1), 0) < HEAD_DIM
    pref = prow_ref[qi][:, 0:1]

    def attend(nblk):
        kc = kc_ref[0, 0, 0:nblk, :]
        vct = vct_ref[0, 0, :, 0:nblk]
        n_id = lax.broadcasted_iota(jnp.int32, (nblk, 1), 0)
        valid = (n_id * CMP_STRIDE + (CMP_LEN - 1)) <= t
        rel = (cposc_ref[0:nblk, :] - pref) * LOG2E
        psum = jnp.zeros((nblk, tq), F32)
        outs = []
        for hh in range(NSA_HPG):
            s = _dot_nt(kc, q4[:, hh * LANES:(hh + 1) * LANES]) + slope_ref[g * NSA_HPG + hh] * rel
            s = jnp.where(valid, s, -jnp.inf)
            mx = jnp.max(s, axis=0, keepdims=True)
            mx = jnp.where(mx == -jnp.inf, 0.0, mx)
            e = jnp.exp2(s - mx)
            den = jnp.maximum(jnp.sum(e, axis=0, keepdims=True), TINY)
            p = e * (1.0 / den)
            psum = psum + p
            outs.append(_dot(vct, p.astype(BF16)) * gtt[3 * hh:3 * hh + 1, :])
        oc_o[0, :, 0:LANES] = jnp.where(row_lo, outs[0], outs[1]).T
        oc_o[0, :, LANES:2 * LANES] = jnp.where(row_lo, outs[2], outs[3]).T
        hi, lo = _split_hl(psum)
        imp_ref[...] = _dot(ovlt_ref[:, 0:nblk], hi) + _dot(ovlt_ref[:, 0:nblk], lo)

    n_valid = ((qi + 1) * tq - CMP_LEN) // CMP_STRIDE + 1
    n_groups = nch // LANES
    need = jnp.clip((n_valid + LANES - 1) // LANES, 1, n_groups)
    for c in range(1, n_groups + 1):
        @pl.when(need == c)
        def _(c=c):
            attend(c * LANES)

    assert NSA_HPG < FORCE_BONUS
    cur = t >> 6

    def select(nrow):
        blk = lax.broadcasted_iota(jnp.int32, (nrow, 1), 0)
        forced = (blk == 0) | (blk == cur) | (blk == cur - 1)
        allowed = blk <= cur
        score = jnp.where(allowed & ~forced, imp_ref[0:nrow, :], -jnp.inf)
        blk_f = blk.astype(F32)

        def pick(_, sc):
            mx = jnp.max(sc, axis=0, keepdims=True)
            idx = jnp.min(jnp.where(sc == mx, blk_f, float(LANES)), axis=0, keepdims=True)
            return jnp.where(blk_f == idx, -jnp.inf, sc)

        left = lax.fori_loop(0, SEL_TOPN - 3, pick, score)
        imp_ref[0:nrow, :] = jnp.where((left == -jnp.inf) & allowed, 1.0, 0.0)
        if nrow < LANES:
            imp_ref[nrow:LANES, :] = jnp.zeros((LANES - nrow, tq), F32)

    n_allowed = ((qi + 1) * tq) // SEL_LEN
    n_quarters = LANES // SEL_ROWS
    need_sel = jnp.clip((n_allowed + SEL_ROWS - 1) // SEL_ROWS, 1, n_quarters)
    for c in range(1, n_quarters + 1):
        @pl.when(need_sel == c)
        def _(c=c):
            select(c * SEL_ROWS)

    chosen = imp_ref[...].T
    selb_o[0, 0] = ((chosen - 1.0) * SEL_OFF).astype(BF16)
    used = jnp.max(chosen, axis=0, keepdims=True)
    any_o[0, 0, 0] = jnp.broadcast_to(used, (8, LANES))


def _nsa_cmp(slopes, q, kc, vct, ovlt, cposc, pos_rows, gates):
    b, s, _ = q.shape
    nch = kc.shape[2]
    tq = pos_rows.shape[2]
    kern = functools.partial(_nsa_cmp_kernel, tq=tq)
    return pl.pallas_call(
        kern,
        grid=(b, NSA_KV_HEADS, s // tq),
        in_specs=[pl.BlockSpec(memory_space=pltpu.SMEM),
                  pl.BlockSpec((1, tq, 4 * LANES), lambda i, g, j: (i, j, g)),
                  pl.BlockSpec((1, 1, nch, LANES), lambda i, g, j: (i, g, 0, 0)),
                  pl.BlockSpec((1, 1, LANES, nch), lambda i, g, j: (i, g, 0, 0)),
                  _full(ovlt.shape), _full(cposc.shape), _full(pos_rows.shape),
                  pl.BlockSpec((1, tq, LANES), lambda i, g, j: (i, j, g))],
        out_specs=[pl.BlockSpec((1, tq, 2 * LANES), lambda i, g, j: (i, j, g)),
                   pl.BlockSpec((1, 1, tq, LANES), lambda i, g, j: (i, g, j, 0)),
                   pl.BlockSpec((1, 1, 1, 8, LANES), lambda i, g, j: (i, g, j, 0, 0))],
        out_shape=[jax.ShapeDtypeStruct((b, s, NSA_W), F32),
                   jax.ShapeDtypeStruct((b, NSA_KV_HEADS, s, LANES), BF16),
                   jax.ShapeDtypeStruct((b, NSA_KV_HEADS, s // tq, 8, LANES), F32)],
        scratch_shapes=[pltpu.VMEM((LANES, tq), F32)],
        compiler_params=_cparams(3),
        name="nsa_cmp_topk",
    )(slopes, q, kc, vct, ovlt, cposc, pos_rows, gates)


def _gqa_kernel(flag_ref, slope_ref, bnd_ref, q_ref, selb_ref, k_ref, v_ref, pos_ref, pcol_ref, gt_ref, o_ref,
                qa_ref, acc_ref, m_ref, *, tq, tk, branch, bounded):
    bi = pl.program_id(0)
    g = pl.program_id(1)
    qi = pl.program_id(2)
    q4 = q_ref[0]
    lane = lax.broadcasted_iota(jnp.int32, (1, LANES), 1)
    lo_half = lane < HEAD_DIM
    pref = pos_ref[qi * (tq // tk)][:, 0:1]
    for hh in range(NSA_HPG):
        sl = slice(hh * tq, (hh + 1) * tq)
        qa_ref[sl, 0:LANES] = q4[:, hh * LANES:(hh + 1) * LANES]
        if branch == 1:
            qa_ref[sl, LANES:2 * LANES] = selb_ref[0, 0]
        if bounded:
            own = (pcol_ref[...] - pref) * (LOG2E * slope_ref[g * NSA_HPG + hh]) + bnd_ref[0]
            m_ref[sl] = jnp.broadcast_to(own, (tq, LANES))
    acc_ref[...] = jnp.zeros_like(acc_ref)
    if not bounded:
        m_ref[...] = jnp.full_like(m_ref, NEG_MASK)
    rows = qi * tq + lax.broadcasted_iota(jnp.int32, (tq, 1), 0)
    col0 = lax.broadcasted_iota(jnp.int32, (1, tk), 1)
    krow = lax.broadcasted_iota(jnp.int32, (tk, 1), 0)
    per = tq // tk
    nrep = tk // LANES
    flag0 = ((bi * NSA_KV_HEADS + g) * pl.num_programs(2) + qi) * SEL_TILE_STRIDE

    def scores(kt_true, masked, r_lo, r_hi):
        kt = jnp.maximum(kt_true, 0)
        off = pl.multiple_of(kt * tk, tk)
        k = k_ref[0, pl.ds(off, tk), :]
        if branch == 1:
            blk = (kt * tk + krow) >> 6
            onehot = jnp.where(lane == blk, 1.0, 0.0).astype(BF16)
            s_all = _dot_nt(qa_ref[...], jnp.concatenate([k, onehot], axis=1))
            parts = [s_all[hh * tq:(hh + 1) * tq] for hh in range(NSA_HPG)]
        else:
            parts = [_dot_nt(qa_ref[hh * tq + r_lo:hh * tq + r_hi, 0:LANES], k) for hh in range(NSA_HPG)]
        rel = (pos_ref[kt] - pref) * LOG2E
        if masked:
            cols = kt_true * tk + col0
            d = rows[r_lo:r_hi] - cols
            keep = d >= 0
            if branch == 2:
                keep = keep & (d < WIN) & (cols >= 0)
        out = []
        for hh in range(NSA_HPG):
            s = parts[hh] + slope_ref[g * NSA_HPG + hh] * rel
            if masked:
                s = jnp.where(keep, s, NEG_MASK)
            out.append(s)
        return out

    def sweep(fn):
        if branch == 1:
            def body(j, carry):
                @pl.when(flag_ref[flag0 + j] != 0)
                def _():
                    fn(j, False, 0, tq)
                return carry
            lax.fori_loop(0, qi * per, body, 0)
            for dd in range(per):
                fn(qi * per + dd, True, 0, tq)
        else:
            for c in range(-(WIN // tk), per):
                r_lo = max(0, c * tk)
                r_hi = min(tq, -(-(c * tk + tk + WIN - 1) // 8) * 8)
                fn(qi * per + c, True, r_lo, r_hi)

    def row_max(kt_true, masked, r_lo, r_hi):
        for hh, s in enumerate(scores(kt_true, masked, r_lo, r_hi)):
            sl = slice(hh * tq + r_lo, hh * tq + r_hi)
            m_ref[sl] = jnp.maximum(m_ref[sl], _lane_max(s))

    def accumulate(kt_true, masked, r_lo, r_hi):
        off = pl.multiple_of(jnp.maximum(kt_true, 0) * tk, tk)
        v = v_ref[0, pl.ds(off, tk), :]
        for hh, s in enumerate(scores(kt_true, masked, r_lo, r_hi)):
            sl = slice(hh * tq + r_lo, hh * tq + r_hi)
            p = jnp.exp2(s - jnp.concatenate([m_ref[sl]] * nrep, axis=1))
            acc_ref[sl] += _dot(p.astype(BF16), v)

    if not bounded:
        sweep(row_max)
        for hh in range(NSA_HPG):
            sl = slice(hh * tq, (hh + 1) * tq)
            m_ref[sl] = jnp.broadcast_to(jnp.max(m_ref[sl], axis=-1, keepdims=True), (tq, LANES))
    sweep(accumulate)

    gt = gt_ref[0]
    for pr in range(2):
        res = []
        for x in range(2):
            hh = 2 * pr + x
            a = acc_ref[hh * tq:(hh + 1) * tq]
            r = pltpu.roll(a, HEAD_DIM, 1)
            o = a / r if x == 0 else r / a
            res.append(o * gt[:, 3 * hh + branch:3 * hh + branch + 1])
        o_ref[0, :, pr * LANES:(pr + 1) * LANES] = _pair_select(lo_half, res[0], res[1])


def _nsa_gqa(flags, slopes, bound, safe, q, selb, k, v, pos2d, pos_col, gates, branch):
    b, s, _ = q.shape
    tq = min(TQ_NSA, s)
    tk = pos2d.shape[2]

    def call(bounded):
        kern = functools.partial(_gqa_kernel, tq=tq, tk=tk, branch=branch, bounded=bounded)
        smem = pl.BlockSpec(memory_space=pltpu.SMEM)
        name = ("nsa_sel_attn" if branch == 1 else "nsa_win_attn") + ("_bounded" if bounded else "")
        return pl.pallas_call(
            kern,
            grid=(b, NSA_KV_HEADS, s // tq),
            in_specs=[smem, smem, smem,
                      pl.BlockSpec((1, tq, 4 * LANES), lambda i, g, j: (i, j, g)),
                      pl.BlockSpec((1, 1, tq, LANES), lambda i, g, j: (i, g, j, 0)),
                      pl.BlockSpec((1, s, LANES), lambda i, g, j: (i, 0, g)),
                      pl.BlockSpec((1, s, LANES), lambda i, g, j: (i, 0, g)),
                      _full(pos2d.shape),
                      pl.BlockSpec((tq, 1), lambda i, g, j: (j, 0)),
                      pl.BlockSpec((1, tq, LANES), lambda i, g, j: (i, j, g))],
            out_specs=pl.BlockSpec((1, tq, 2 * LANES), lambda i, g, j: (i, j, g)),
            out_shape=jax.ShapeDtypeStruct((b, s, NSA_W), F32),
            scratch_shapes=[pltpu.VMEM((NSA_HPG * tq, 2 * LANES), BF16),
                            pltpu.VMEM((NSA_HPG * tq, LANES), F32),
                            pltpu.VMEM((NSA_HPG * tq, LANES), F32)],
            compiler_params=_cparams(3),
            name=name,
        )(flags, slopes, bound, q, selb, k, v, pos2d, pos_col, gates)

    return lax.cond(safe, lambda: call(True), lambda: call(False))


def _dil_kernel(slope_ref, bnd_ref, q_ref, k_ref, v_ref, pc_ref, pr_ref, o_ref, lse_ref,
                *, t, span, grp, bounded):
    i = pl.program_id(2)
    q4 = q_ref[0]
    lane = lax.broadcasted_iota(jnp.int32, (1, LANES), 1)
    lo_half = lane < HEAD_DIM
    per = t // LANES
    prev = jnp.maximum(i * per - 1, 0)
    o_prev = pl.multiple_of(prev * LANES, LANES)
    o_cur = pl.multiple_of(i * t, t)
    kk = jnp.concatenate([k_ref[0, pl.ds(o_prev, LANES), :], k_ref[0, pl.ds(o_cur, t), :]], axis=0)
    vv = jnp.concatenate([v_ref[0, pl.ds(o_prev, LANES), :], v_ref[0, pl.ds(o_cur, t), :]], axis=0)
    pk = jnp.concatenate([pr_ref[0, prev]] + [pr_ref[0, i * per + c] for c in range(per)], axis=1)
    dist = (pc_ref[0] - pk) * LOG2E
    rows = i * t + lax.broadcasted_iota(jnp.int32, (t, 1), 0)
    cols = i * t - LANES + lax.broadcasted_iota(jnp.int32, (1, t + LANES), 1)
    d = rows - cols
    valid = (cols >= 0) & (d >= 0) & (d <= span)
    zero = jnp.zeros((t, LANES), BF16)
    ones = jnp.ones((t + LANES, LANES), BF16)
    for p in range(2):
        qp = q4[:, p * LANES:(p + 1) * LANES]
        kp = kk[:, p * LANES:(p + 1) * LANES]
        vx = jnp.concatenate([vv[:, p * LANES:(p + 1) * LANES], ones], axis=1)
        res = []
        for x in range(2):
            qx = jnp.where(lo_half, qp, zero) if x == 0 else jnp.where(lo_half, zero, qp)
            s = _dot_nt(qx, kp) - slope_ref[grp * DIL_HEADS + 2 * p + x] * dist
            s = jnp.where(valid, s, -jnp.inf)
            mx = bnd_ref[0] if bounded else jnp.max(s, axis=-1, keepdims=True)
            acc = _dot(jnp.exp2(s - mx).astype(BF16), vx)
            den = acc[:, LANES:2 * LANES]
            res.append((acc[:, 0:LANES] / den, mx * LN2 + jnp.log(den)))
        sl = slice(p * LANES, (p + 1) * LANES)
        o_ref[0, :, sl] = _pair_select(lo_half, res[0][0], res[1][0])
        lse_ref[0, :, sl] = jnp.where(lo_half, res[0][1], res[1][1])


def _dilated(slopes, bound, safe, qv, kv, vv, pos_f, grp):
    window, dil = DIL_CFG[grp]
    b, sub, wd = qv.shape
    w = wd // dil
    t = min(T_DIL, sub)
    span = window // dil
    assert span <= LANES and t % LANES == 0
    pres = pos_f.reshape(sub, dil).T
    pc = pres.reshape(dil, sub, 1)
    pr = pres.reshape(dil, sub // LANES, 1, LANES)
    shp = jax.ShapeDtypeStruct((b, sub, wd), F32)
    smem = pl.BlockSpec(memory_space=pltpu.SMEM)

    def call(bounded):
        kern = functools.partial(_dil_kernel, t=t, span=span, grp=grp, bounded=bounded)
        return pl.pallas_call(
            kern,
            grid=(b, dil, sub // t),
            in_specs=[smem, smem,
                      pl.BlockSpec((1, t, w), lambda i, r, j: (i, j, r)),
                      pl.BlockSpec((1, sub, w), lambda i, r, j: (i, 0, r)),
                      pl.BlockSpec((1, sub, w), lambda i, r, j: (i, 0, r)),
                      pl.BlockSpec((1, t, 1), lambda i, r, j: (r, j, 0)),
                      pl.BlockSpec((1, sub // LANES, 1, LANES), lambda i, r, j: (r, 0, 0, 0))],
            out_specs=[pl.BlockSpec((1, t, w), lambda i, r, j: (i, j, r)),
                       pl.BlockSpec((1, t, w), lambda i, r, j: (i, j, r))],
            out_shape=[shp, shp],
            compiler_params=_cparams(3),
            name="dilated_attn_g%d%s" % (grp, "_bounded" if bounded else ""),
        )(slopes, bound, qv, kv, vv, pc, pr)

    return lax.cond(safe, lambda: call(True), lambda: call(False))


def _out_odd_kernel(x_ref, mod_ref, oc_ref, os_ref, ow_ref, d0_ref, d1_ref, d2_ref,
                    l0_ref, l1_ref, l2_ref, gz_ref, w_ref, o_ref, st):
    ts = x_ref.shape[1]
    gz = gz_ref[0].astype(F32)
    nsa = oc_ref[0] + os_ref[0] + ow_ref[0]
    m1 = (nsa * gz[:, 0:NSA_W]).astype(BF16)

    def token_order(ref, g, slot):
        dil = DIL_CFG[g][1]
        if dil == 1:
            return ref[0]
        for r in range(dil):
            for h in range(2):
                st[slot, h, pl.ds(r, ts // dil, stride=dil), :] = ref[0, :, r * DIL_W + h * 128:r * DIL_W + (h + 1) * 128]
        return jnp.concatenate([st[slot, 0], st[slot, 1]], axis=1)

    d0, d1, d2 = token_order(d0_ref, 0, 0), token_order(d1_ref, 1, 0), token_order(d2_ref, 2, 1)
    l0, l1, l2 = token_order(l0_ref, 0, 2), token_order(l1_ref, 1, 2), token_order(l2_ref, 2, 3)
    mx = jnp.maximum(jnp.maximum(l0, l1), l2)
    e0, e1, e2 = jnp.exp(l0 - mx), jnp.exp(l1 - mx), jnp.exp(l2 - mx)
    dil = (e0 * d0 + e1 * d1 + e2 * d2) / (e0 + e1 + e2)
    m2 = (dil * gz[:, NSA_W:NSA_W + DIL_W]).astype(BF16)
    y = _dot(m1, w_ref[0:NSA_W, :]) + _dot(m2, w_ref[NSA_W:NSA_W + DIL_W, :])
    gate = mod_ref[0][:, 2 * D_MODEL:3 * D_MODEL]
    o_ref[0] = x_ref[0] + gate * y


def _out_odd(x, mod3, oc, os_, ow, dils, lses, gz, w):
    b, s, d = x.shape
    ts = min(TS_PROJ, s)
    row = lambda n: pl.BlockSpec((1, ts, n), lambda i, j: (i, j, 0))
    res = [pl.BlockSpec((1, ts // DIL_CFG[g][1], DIL_CFG[g][1] * DIL_W), lambda i, j: (i, j, 0))
           for g in range(N_DIL)]
    return pl.pallas_call(
        _out_odd_kernel,
        grid=(b, s // ts),
        in_specs=[row(d), pl.BlockSpec((1, 1, 3 * d), lambda i, j: (i, 0, 0)),
                  row(NSA_W), row(NSA_W), row(NSA_W)] + res + res + [row(1024), _full(w.shape)],
        out_specs=row(d),
        out_shape=jax.ShapeDtypeStruct((b, s, d), F32),
        scratch_shapes=[pltpu.VMEM((4, 2, ts, 128), F32)],
        compiler_params=_cparams(2),
        name="outproj_odd",
    )(x, mod3, oc, os_, ow, *dils, *lses, gz, w)


def _pad_vec(v, n=D_MODEL):
    return jnp.pad(v, (0, n - v.shape[0]))


def _group_matrix(sizes, total):
    m = np.zeros((total, total), np.float32)
    off = 0
    for sz, on in sizes:
        if on:
            m[off:off + sz, off:off + sz] = 1.0
        off += sz
    return jnp.asarray(m, BF16)


def _swap_halves(w):
    h = w.shape[-1] // 2
    return jnp.concatenate([w[..., h:], w[..., :h]], axis=-1)


def _column_blocks(segments):
    blocks, cur, room = [], [], LANES
    for src, width in segments:
        while width:
            n = min(width, room)
            cur.append((src, n))
            src = None if src is None else src + n
            width -= n
            room -= n
            if room == 0:
                blocks.append(cur)
                cur, room = [], LANES
    assert not cur
    return blocks


def _pack_cols_kernel(w_ref, o_ref, *, blocks):
    _, rows, n_src = w_ref.shape
    for bi, pieces in enumerate(blocks):
        parts = []
        for src, width in pieces:
            if src is None:
                parts.append(jnp.zeros((rows, width), F32))
                continue
            a0 = src // LANES * LANES
            a1 = min(-(-(src + width) // LANES) * LANES, n_src)
            parts.append(w_ref[0, :, a0:a1][:, src - a0:src - a0 + width])
        blk = parts[0] if len(parts) == 1 else jnp.concatenate(parts, axis=1)
        o_ref[:, bi * LANES:(bi + 1) * LANES] = blk.astype(BF16)


def _pack_cols(w, layer, segments):
    blocks = _column_blocks(segments)
    _, d, n_src = w.shape
    tr = 256
    return pl.pallas_call(
        functools.partial(_pack_cols_kernel, blocks=blocks),
        grid=(d // tr,),
        in_specs=[pl.BlockSpec((1, tr, n_src), lambda i: (layer, i, 0))],
        out_specs=pl.BlockSpec((tr, len(blocks) * LANES), lambda i: (i, 0)),
        out_shape=jax.ShapeDtypeStruct((d, len(blocks) * LANES), BF16),
        compiler_params=_cparams(1),
        name="pack_weight_columns",
    )(w)


EVEN_SEGMENTS = ((0, 1536), (1536, 512), (2464, 512), (2048, 256), (2304, 128),
                 (None, 64), (2432, 32), (None, 32), (None, 64), (2448, 16), (2432, 16), (None, 32))


def _pack_even(w_in_all, layer, norm_g, sb_qn, sb_kn, qa_g, wq_up, kva_g, wkv_up, qn, kn):
    w = _pack_cols(w_in_all, layer, EVEN_SEGMENTS)
    wq3 = wq_up.reshape(MLA_Q_RANK, MLA_HEADS, MLA_NOPE + MLA_ROPE)
    zq = jnp.zeros((MLA_Q_RANK, MLA_HEADS, 32), wq_up.dtype)
    wq = jnp.concatenate([wq3, zq], axis=-1).reshape(MLA_Q_RANK, MLA_HEADS * 128).astype(BF16)
    wqs = jnp.concatenate([jnp.zeros((MLA_Q_RANK, MLA_HEADS, 64), wq_up.dtype),
                           _swap_halves(wq3[..., MLA_NOPE:]), zq], axis=-1)
    wqs = wqs.reshape(MLA_Q_RANK, MLA_HEADS * 128).astype(BF16)
    wkv3 = wkv_up.reshape(MLA_KV_RANK, MLA_HEADS, MLA_NOPE + MLA_V)
    wk = jnp.concatenate([wkv3[..., :MLA_NOPE], jnp.zeros_like(wkv3[..., :MLA_NOPE])], axis=-1)
    wkv = jnp.concatenate([wk.reshape(MLA_KV_RANK, MLA_HEADS * 128),
                           wkv3[..., MLA_NOPE:].reshape(MLA_KV_RANK, MLA_HEADS * MLA_V)], axis=1).astype(BF16)
    z32 = jnp.zeros((32,), F32)
    z64 = jnp.zeros((64,), F32)
    scale = (MLA_NOPE + MLA_ROPE) ** -0.5 * LOG2E
    qg = jnp.tile(jnp.concatenate([qn, z32]), MLA_HEADS) * scale
    qgs = jnp.tile(jnp.concatenate([z64, _swap_halves(qn[MLA_NOPE:]), z32]), MLA_HEADS) * scale
    kg = jnp.tile(jnp.concatenate([kn[:MLA_NOPE], z64]), MLA_HEADS)
    krg = jnp.concatenate([z64, kn[MLA_NOPE:], z32])
    krgs = jnp.concatenate([z64, _swap_halves(kn[MLA_NOPE:]), z32])
    cnt = jnp.tile(jnp.concatenate([jnp.full((64,), 1.0 / 64), jnp.full((32,), 1.0 / 32), jnp.ones((32,))]),
                   MLA_HEADS).astype(F32)
    rows = [norm_g, _pad_vec(jnp.tile(sb_qn, SB_HEADS) * (LOG2E * HEAD_DIM ** -0.5)), _pad_vec(jnp.tile(sb_kn, SB_HEADS)),
            _pad_vec(qa_g), _pad_vec(kva_g), qg, qgs, kg, _pad_vec(krg), _pad_vec(krgs), cnt]
    rows += [jnp.zeros((D_MODEL,), F32)] * (16 - len(rows))
    return w, wq, wqs, wkv, jnp.stack(rows).astype(F32)


def _padded_heads(start, n):
    return tuple(seg for h in range(n) for seg in ((start + h * HEAD_DIM, HEAD_DIM), (None, HEAD_DIM)))


NSA_SEGMENTS = (_padded_heads(0, NSA_HEADS)
                + tuple(seg for c in range(6) for seg in _padded_heads(768 + c * NSA_KV_W, NSA_KV_HEADS))
                + tuple(seg for g in range(NSA_KV_HEADS)
                        for seg in ((1920 + g * NSA_HPG * 3, NSA_HPG * 3), (None, LANES - NSA_HPG * 3)))
                + ((1956, NSA_W), (5028, DIL_W)))
DIL_SEGMENTS = ((2724, 3 * N_DIL * DIL_W),)


def _pack_odd(w_in_all, layer, norm_g, nsa_qn, nsa_kn, dil_qn, dil_kn):
    w_nsa = _pack_cols(w_in_all, layer, NSA_SEGMENTS)
    w_dil = _pack_cols(w_in_all, layer, DIL_SEGMENTS)
    z64 = jnp.zeros((64,), F32)
    n = NSA_HEADS * 128
    rows = [_pad_vec(norm_g, n),
            jnp.tile(jnp.concatenate([nsa_qn * (LOG2E * HEAD_DIM ** -0.5), z64]), NSA_HEADS),
            _pad_vec(jnp.tile(jnp.concatenate([nsa_kn, z64]), NSA_KV_HEADS), n),
            _pad_vec(jnp.tile(dil_qn, DIL_HEADS) * (LOG2E * HEAD_DIM ** -0.5), n),
            _pad_vec(jnp.tile(dil_kn, DIL_HEADS), n)]
    rows += [jnp.zeros((n,), F32)] * (8 - len(rows))
    return w_nsa, w_dil, jnp.stack(rows).astype(F32)


def _pack_compress(w, pe, double):
    w3 = w.reshape(CMP_LEN, HEAD_DIM, HEAD_DIM)
    w3 = jnp.concatenate([w3, jnp.zeros_like(w3)], axis=1)
    w3 = jnp.concatenate([w3, w3 if double else jnp.zeros_like(w3)], axis=2)
    wp = w3.reshape(2, CMP_STRIDE * 128, 128).astype(BF16)
    pe2 = jnp.concatenate([pe, jnp.zeros_like(pe)], axis=1).reshape(2, 1, CMP_STRIDE * 128)
    pe2 = jnp.broadcast_to(pe2, (2, 8, CMP_STRIDE * 128)).astype(F32)
    return wp, pe2


def _normed_len(gain, sizes):
    tot, off = 0.0, 0
    for n in sizes:
        tot = tot + n * jnp.max(jnp.square(gain[off:off + n]))
        off += n
    return jnp.sqrt(tot)


def _mla_logit_bound(qn, kn):
    sizes = (MLA_NOPE, MLA_ROPE)
    scale = (MLA_NOPE + MLA_ROPE) ** -0.5 * LOG2E
    return (_normed_len(qn, sizes) * _normed_len(kn, sizes) * (scale * ROUNDING_MARGIN)).reshape(1).astype(F32)


def _alibi_slopes(n):
    return 2.0 ** (-8.0 * jnp.arange(1, n + 1, dtype=jnp.float32) / n)


def kernel(x, c, positions, ada_w, ada_b, norm_g, ev_w_in, ev_w_out, sb_qn, sb_kn, mla_qa_g, mla_wq_up,
           mla_kva_g, mla_wkv_up, mla_qn, mla_kn, od_w_in, od_w_out, nsa_qn, nsa_kn, nsa_cmp_wk, nsa_cmp_wv,
           nsa_cmp_pe_k, nsa_cmp_pe_v, dil_qn, dil_kn):
    b, s, d = x.shape
    depth = ada_w.shape[0]
    pos_f = positions.astype(F32)

    inv_freq = ROPE_BASE ** (-jnp.arange(0, MLA_ROPE, 2, dtype=F32) / MLA_ROPE)
    ang = pos_f[:, None] * inv_freq[None, :]
    cos, sin = jnp.cos(ang), jnp.sin(ang)
    cos128 = jnp.concatenate([jnp.ones((s, 64), F32), cos, cos, jnp.zeros((s, 32), F32)], axis=1)
    sin128 = jnp.concatenate([jnp.zeros((s, 64), F32), -sin, sin, jnp.zeros((s, 32), F32)], axis=1)
    nsa_slopes = _alibi_slopes(NSA_HEADS)
    dil_slopes = _alibi_slopes(N_DIL * DIL_HEADS)
    nch = s // CMP_STRIDE
    chunk_sum = pos_f.reshape(nch, CMP_STRIDE).sum(axis=1)
    cpos = ((chunk_sum + jnp.roll(chunk_sum, -1)) / CMP_LEN).reshape(nch, 1)
    n_sel = s // SEL_LEN
    cst = np.arange(nch)[:, None] * CMP_STRIDE
    jst = np.arange(LANES)[None, :] * SEL_LEN
    ovl = ((cst <= jst + SEL_LEN - 1) & (cst + CMP_LEN - 1 >= jst) & (np.arange(LANES)[None, :] < n_sel))
    ovlt = jnp.asarray(ovl.astype(np.float32).T, BF16)
    pos_col = pos_f.reshape(s, 1)
    tqc = min(TQ_CMP, s)
    pos_rows = pos_f.reshape(s // tqc, 1, tqc)
    tqn = min(TQ_NSA, s)
    tkn = min(TK_NSA, tqn)
    assert tkn == TK_NSA
    pos2d = pos_f.reshape(s // tkn, 1, tkn)
    pos2d_win = pos_f.reshape(s // TK_WIN, 1, TK_WIN)
    q_first = pos_f[::tqn]
    k_last = pos_f[jnp.minimum((jnp.arange(SEL_TILE_STRIDE) + 1) * tkn - 1, s - 1)]
    group_slope = jnp.min(nsa_slopes.reshape(NSA_KV_HEADS, NSA_HPG), axis=1) * LOG2E
    sel_dead = group_slope[:, None, None] * (q_first[:, None] - k_last[None, :])[None] > F32_ZERO_EXP
    pos_sorted = jnp.all(pos_f[1:] >= pos_f[:-1])

    g64 = _group_matrix([(64, 1)] * 4, 256)
    gm = _group_matrix([(64, 1), (32, 1), (32, 0)] * 2, 256)
    tks = min(TK_SB, min(TQ_SB, s))
    tri = jnp.asarray(np.tril(np.ones((tks, tks), np.float32)), BF16)

    c8 = jnp.pad(c, ((0, 8 - b), (0, 0)))
    mod_all = _modulation(c8, ada_w, ada_b)

    for layer in range(depth):
        j = layer // 2
        mod3 = mod_all[layer, :b].reshape(b, 1, 3 * d)
        if layer % 2 == 0:
            w, wq, wqs, wkv, vec = _pack_even(ev_w_in, j, norm_g[layer], sb_qn[j], sb_kn[j], mla_qa_g[j],
                                              mla_wq_up[j], mla_kva_g[j], mla_wkv_up[j], mla_qn[j], mla_kn[j])
            sbq, sbk, sbv, gz, mq, mk, mv = _k1_even(x, mod3, vec, w, wq, wqs, wkv, g64, gm, cos128, sin128)
            sb_bound = (_normed_len(sb_qn[j], (HEAD_DIM,)) * _normed_len(sb_kn[j], (HEAD_DIM,))
                        * (HEAD_DIM ** -0.5 * LOG2E * ROUNDING_MARGIN)).reshape(1).astype(F32)
            o_sb = _sb_attention(sbq, sbk, sbv, tri, sb_bound, sb_bound[0] <= SAFE_LOGIT_BOUND)
            o_mla = _mla_attention(mq, mk, mv, _mla_logit_bound(mla_qn[j], mla_kn[j]))
            x = _out_even(x, mod3, o_sb, o_mla, gz, ev_w_out[j].astype(BF16))
        else:
            w_nsa, w_dil, vec = _pack_odd(od_w_in, j, norm_g[layer], nsa_qn[j], nsa_kn[j], dil_qn[j], dil_kn[j])
            q, ck, cv, sk, sv, wk, wv, gates, gz = _k1_nsa(x, mod3, vec, w_nsa, g64)
            dqkv = _k1_dil(x, mod3, vec, w_dil, g64)
            wck, pek = _pack_compress(nsa_cmp_wk[j], nsa_cmp_pe_k[j], False)
            wcv, pev = _pack_compress(nsa_cmp_wv[j], nsa_cmp_pe_v[j], True)
            kgain = jnp.concatenate([nsa_kn[j], jnp.zeros((64,), F32)]).reshape(1, 128)
            kc, vc = _compress(ck, cv, wck, wcv, pek, pev, kgain, g64[0:128, 0:128])
            o_c, selb, used = _nsa_cmp(nsa_slopes, q, kc, vc, ovlt, cpos, pos_rows, gates)
            flags = used[:, :, :, 0, :].reshape(b, NSA_KV_HEADS, s // tqn, tqn // min(TQ_CMP, s),
                                                SEL_TILE_STRIDE, LANES // SEL_TILE_STRIDE).max(axis=(3, 5))
            nsa_bound = (_normed_len(nsa_qn[j], (HEAD_DIM,)) * _normed_len(nsa_kn[j], (HEAD_DIM,))
                         * (HEAD_DIM ** -0.5 * LOG2E * ROUNDING_MARGIN)).reshape(1).astype(F32)
            safe = (nsa_bound[0] <= SAFE_LOGIT_BOUND) & pos_sorted
            flags = (flags > 0) & ~(safe & sel_dead[None])
            flags = flags.astype(jnp.int32).reshape(-1)
            o_s = _nsa_gqa(flags, nsa_slopes, nsa_bound, safe, q, selb, sk, sv, pos2d, pos_col, gates, 1)
            o_w = _nsa_gqa(flags, nsa_slopes, nsa_bound, safe, q, selb, wk, wv, pos2d_win, pos_col, gates, 2)
            dil_bound = (_normed_len(dil_qn[j], (HEAD_DIM,)) * _normed_len(dil_kn[j], (HEAD_DIM,))
                         * (HEAD_DIM ** -0.5 * LOG2E * ROUNDING_MARGIN)).reshape(1).astype(F32)
            dil_safe = (dil_bound[0] <= SAFE_LOGIT_BOUND) & pos_sorted
            dils, lses = [], []
            for g in range(N_DIL):
                o, lse = _dilated(dil_slopes, dil_bound, dil_safe, dqkv[3 * g], dqkv[3 * g + 1], dqkv[3 * g + 2],
                                  pos_f, g)
                dils.append(o)
                lses.append(lse)
            x = _out_odd(x, mod3, o_c, o_s, o_w, dils, lses, gz, od_w_out[j].astype(BF16))
    return x
```

```python
import functools

import numpy as np
import jax
import jax.numpy as jnp
from jax import lax
from jax.experimental import pallas as pl
from jax.experimental.pallas import tpu as pltpu

F32 = jnp.float32
BF16 = jnp.bfloat16

D_MODEL = 1024
HEAD_DIM = 64
NORM_EPS = 1e-6
TINY = 1e-30
SB_HEADS = 8
MLA_HEADS = 8
MLA_Q_RANK = 256
MLA_KV_RANK = 128
MLA_NOPE = 64
MLA_ROPE = 32
MLA_V = 64
ROPE_BASE = 10000.0
NSA_HEADS = 12
NSA_KV_HEADS = 3
NSA_HPG = 4
CMP_LEN = 32
CMP_STRIDE = 16
SEL_LEN = 64
SEL_TOPN = 16
WIN = 512
FORCE_BONUS = 1e3
DIL_CFG = ((128, 1), (512, 4), (2048, 16))
N_DIL = 3
DIL_HEADS = 4
SB_W = SB_HEADS * HEAD_DIM
MLA_OUT = MLA_HEADS * MLA_V
NSA_W = NSA_HEADS * HEAD_DIM
NSA_KV_W = NSA_KV_HEADS * HEAD_DIM
DIL_W = DIL_HEADS * HEAD_DIM

LANES = 128
NEG_MASK = -1e30
LOG2E = 1.4426950408889634
LN2 = 0.6931471805599453
F32_ZERO_EXP = 150.0
SAFE_LOGIT_BOUND = 50.0
ROUNDING_MARGIN = 1.02
SEL_OFF = 2.0 ** 30
VMEM_LIMIT = 56 * 1024 * 1024

TS_PROJ = 512
TQ_SB = 1024
TK_SB = 256
TQ_MLA = 1024
TD_MLA = 256
TQ_CMP = 512
SEL_ROWS = 32
TQ_NSA = 512
TK_NSA = 512
TK_WIN = 256
SEL_TILE_STRIDE = LANES * SEL_LEN // TK_NSA
T_DIL = 256


def _dot(a, b):
    return jnp.dot(a, b, preferred_element_type=F32)


def _dot_nt(a, b):
    return lax.dot_general(a, b, (((1,), (1,)), ((), ())), preferred_element_type=F32)


def _split_hl(a):
    hi = a.astype(BF16)
    lo = (a - hi.astype(F32)).astype(BF16)
    return hi, lo


def _dot_hl(a, b):
    hi, lo = _split_hl(a)
    return _dot(hi, b) + _dot(lo, b)


def _sigmoid(z):
    return 1.0 / (1.0 + jnp.exp(-z))


def _cparams(n_axes):
    return pltpu.CompilerParams(dimension_semantics=("arbitrary",) * n_axes,
                                vmem_limit_bytes=VMEM_LIMIT)


def _full(shape):
    n = len(shape)
    return pl.BlockSpec(shape, lambda *a, _n=n: (0,) * _n)


def _mod_kernel(c_ref, w_ref, b_ref, o_ref):
    c = c_ref[...]
    a = c * _sigmoid(c)
    ah, al = _split_hl(a)
    wh, wl = _split_hl(w_ref[0])
    o_ref[0] = _dot(ah, wh) + _dot(ah, wl) + _dot(al, wh) + b_ref[0]


def _modulation(c8, ada_w, ada_b):
    depth, d, n3 = ada_w.shape
    tn = 1024
    return pl.pallas_call(
        _mod_kernel,
        grid=(depth, n3 // tn),
        in_specs=[pl.BlockSpec((8, d), lambda l, j: (0, 0)),
                  pl.BlockSpec((1, d, tn), lambda l, j: (l, 0, j)),
                  pl.BlockSpec((1, 1, tn), lambda l, j: (l, 0, j))],
        out_specs=pl.BlockSpec((1, 8, tn), lambda l, j: (l, 0, j)),
        out_shape=jax.ShapeDtypeStruct((depth, 8, n3), F32),
        compiler_params=_cparams(2),
        name="adaln_mod",
    )(c8, ada_w, ada_b.reshape(depth, 1, n3))


def _modulated(x_ref, mod_ref, ng):
    x = x_ref[0]
    mod = mod_ref[0]
    shift = mod[:, 0:D_MODEL]
    scale = mod[:, D_MODEL:2 * D_MODEL]
    ms = jnp.mean(x * x, axis=-1, keepdims=True)
    h = x * lax.rsqrt(ms + NORM_EPS) * ng
    h = h * (1.0 + scale) + shift
    return h.astype(BF16)


def _group_sumsq(t, g):
    return _dot((t * t).astype(BF16), g)


def _group_norm_chunk(t, g, inv_cnt):
    return t * lax.rsqrt(_group_sumsq(t, g) * inv_cnt + NORM_EPS)


def _row_rms(t, gain):
    ms = jnp.mean(t * t, axis=-1, keepdims=True)
    return t * lax.rsqrt(ms + NORM_EPS) * gain


EV_COLS = 3200


def _k1_even_kernel(x_ref, mod_ref, vec_ref, w_ref, wq_ref, wqs_ref, wkv_ref, g64_ref, gm_ref,
                    cos_ref, sin_ref, sbq_o, sbk_o, sbv_o, gz_o, mq_o, mk_o, mv_o):
    hb = _modulated(x_ref, mod_ref, vec_ref[0:1, :])
    g64 = g64_ref[...]
    gm = gm_ref[...]
    inv64 = 1.0 / HEAD_DIM

    u = _dot(hb, w_ref[:, 0:1536])
    for c in range(2):
        sl = slice(c * 256, (c + 1) * 256)
        qn = _group_norm_chunk(u[:, c * 256:(c + 1) * 256], g64, inv64)
        sbq_o[0, :, sl] = (qn * vec_ref[1:2, sl]).astype(BF16)
        kn = _group_norm_chunk(u[:, 512 + c * 256:512 + (c + 1) * 256], g64, inv64)
        sbk_o[0, :, sl] = (kn * vec_ref[2:3, sl]).astype(BF16)
    sbv_o[0] = u[:, 1024:1536].astype(BF16)

    uz = _dot(hb, w_ref[:, 1536:2560])
    gz_o[0] = (uz * _sigmoid(uz)).astype(BF16)

    ul = _dot(hb, w_ref[:, 2560:3200])
    qlat = _row_rms(ul[:, 0:256], vec_ref[3:4, 0:256]).astype(BF16)
    kvlat = _row_rms(ul[:, 256:384], vec_ref[4:5, 0:128]).astype(BF16)
    krm = ul[:, 384:512]
    krs = ul[:, 512:640]
    cs = cos_ref[...]
    sn = sin_ref[...]
    cs2 = jnp.concatenate([cs, cs], axis=1)
    sn2 = jnp.concatenate([sn, sn], axis=1)

    tq = _dot(qlat, wq_ref[...])
    tqs = _dot(qlat, wqs_ref[...])
    for p in range(4):
        sl = slice(p * 256, (p + 1) * 256)
        tc = tq[:, p * 256:(p + 1) * 256]
        ss = _group_sumsq(tc, gm)
        inv = lax.rsqrt(ss * vec_ref[10:11, sl] + NORM_EPS)
        a = tc * inv * vec_ref[5:6, sl]
        b = tqs[:, p * 256:(p + 1) * 256] * inv * vec_ref[6:7, sl]
        mq_o[0, :, sl] = (a * cs2 + b * sn2).astype(BF16)

    sskr = _group_sumsq(krm, gm[0:128, 0:128])
    invr = lax.rsqrt(sskr * vec_ref[10:11, 0:128] + NORM_EPS)
    kr = (krm * invr * vec_ref[8:9, 0:128]) * cs + (krs * invr * vec_ref[9:10, 0:128]) * sn
    kr2 = jnp.concatenate([kr, kr], axis=1)

    kv = _dot(kvlat, wkv_ref[...])
    for p in range(4):
        sl = slice(p * 256, (p + 1) * 256)
        kn = _group_norm_chunk(kv[:, p * 256:(p + 1) * 256], g64, inv64)
        mk_o[0, :, sl] = (kn * vec_ref[7:8, sl] + kr2).astype(BF16)
    mv_o[0] = kv[:, 1024:1536].astype(BF16)


def _k1_even(x, mod3, vec, w, wq, wqs, wkv, g64, gm, cos128, sin128):
    b, s, d = x.shape
    ts = min(TS_PROJ, s)
    row = lambda n: pl.BlockSpec((1, ts, n), lambda i, j: (i, j, 0))
    outs = [(512, BF16), (512, BF16), (512, BF16), (1024, BF16), (1024, BF16), (1024, BF16), (512, BF16)]
    return pl.pallas_call(
        _k1_even_kernel,
        grid=(b, s // ts),
        in_specs=[row(d),
                  pl.BlockSpec((1, 1, 3 * d), lambda i, j: (i, 0, 0)),
                  _full(vec.shape), _full(w.shape), _full(wq.shape), _full(wqs.shape), _full(wkv.shape),
                  _full(g64.shape), _full(gm.shape),
                  pl.BlockSpec((ts, 128), lambda i, j: (j, 0)),
                  pl.BlockSpec((ts, 128), lambda i, j: (j, 0))],
        out_specs=[row(n) for n, _ in outs],
        out_shape=[jax.ShapeDtypeStruct((b, s, n), dt) for n, dt in outs],
        compiler_params=_cparams(2),
        name="inproj_even",
    )(x, mod3, vec, w, wq, wqs, wkv, g64, gm, cos128, sin128)


def _sb_kernel(bnd_ref, q_ref, k_ref, v_ref, tri_ref, o_ref, acc_ref, car_ref, *, tq, tk, bounded):
    qi = pl.program_id(2)
    q = q_ref[0]
    lane = lax.broadcasted_iota(jnp.int32, (1, LANES), 1)
    lo_half = lane < HEAD_DIM
    zero = jnp.zeros_like(q)
    qs = (jnp.where(lo_half, q, zero), jnp.where(lo_half, zero, q))
    tri = tri_ref[...]
    acc_ref[...] = jnp.zeros_like(acc_ref)
    car_ref[...] = jnp.zeros_like(car_ref)
    col0 = lax.broadcasted_iota(jnp.int32, (1, tk), 1)

    def tile(kt, r0, masked):
        n = tq - r0
        off = pl.multiple_of(kt * tk, tk)
        k = k_ref[0, pl.ds(off, tk), :]
        v = v_ref[0, pl.ds(off, tk), :]
        if masked:
            rows = qi * tq + r0 + lax.broadcasted_iota(jnp.int32, (n, 1), 0)
            strict = (kt * tk + col0) < rows
        for i in range(2):
            z = _dot_nt(qs[i][r0:tq], k)
            if bounded:
                lom = jnp.log(1.0 + jnp.exp2(z)) * (-LOG2E)
            else:
                lom = -jnp.maximum(z, 0.0) - jnp.log(1.0 + jnp.exp2(-jnp.abs(z))) * LOG2E
            if masked:
                lom = jnp.where(strict, lom, 0.0)
            tt = _dot(lom.astype(BF16), tri)
            car = car_ref[i, r0:tq]
            w = jnp.exp2(tt + z + jnp.concatenate([car] * (tk // LANES), axis=1))
            if masked:
                w = jnp.where(strict, w, 0.0)
            acc_ref[i, r0:tq] += _dot(w.astype(BF16), v)
            car_ref[i, r0:tq] = car + jnp.broadcast_to(tt[:, 0:1], (n, LANES))

    per = tq // tk
    for c in reversed(range(per)):
        tile(qi * per + c, c * tk, True)

    if bounded:
        limit = -(F32_ZERO_EXP + bnd_ref[0])

        def still_live():
            return jnp.max(jnp.maximum(car_ref[0], car_ref[1])) > limit

        def single(c):
            j, _ = c
            tile(qi * per - 1 - j, 0, False)
            return j + 1, still_live()

        _, live = lax.while_loop(lambda c: (c[0] < per) & c[1], single, (0, qi > 0))

        def group(c):
            g, _ = c
            for u in range(per):
                tile(qi * per - 1 - per * (g + 1) - u, 0, False)
            return g + 1, still_live()

        lax.while_loop(lambda c: (c[0] < qi - 1) & c[1], group, (0, live))
    else:
        def body(j, carry):
            for u in range(per):
                tile(qi * per - 1 - per * j - u, 0, False)
            return carry

        lax.fori_loop(0, qi, body, 0)
    o_ref[0] = jnp.where(lo_half, acc_ref[0], acc_ref[1])


def _sb_attention(q, k, v, tri, bound, safe):
    b, s, w = q.shape
    tq = min(TQ_SB, s)
    tk = min(TK_SB, tq)

    def call(bounded):
        kern = functools.partial(_sb_kernel, tq=tq, tk=tk, bounded=bounded)
        return pl.pallas_call(
            kern,
            grid=(b, w // LANES, s // tq),
            in_specs=[pl.BlockSpec(memory_space=pltpu.SMEM),
                      pl.BlockSpec((1, tq, LANES), lambda i, p, j: (i, j, p)),
                      pl.BlockSpec((1, s, LANES), lambda i, p, j: (i, 0, p)),
                      pl.BlockSpec((1, s, LANES), lambda i, p, j: (i, 0, p)),
                      _full(tri.shape)],
            out_specs=pl.BlockSpec((1, tq, LANES), lambda i, p, j: (i, j, p)),
            out_shape=jax.ShapeDtypeStruct((b, s, w), F32),
            scratch_shapes=[pltpu.VMEM((2, tq, LANES), F32), pltpu.VMEM((2, tq, LANES), F32)],
            compiler_params=_cparams(3),
            name="stickbreak_attn_bounded" if bounded else "stickbreak_attn",
        )(bound, q, k, v, tri)

    return lax.cond(safe, lambda: call(True), lambda: call(False))


def _lane_max(s):
    m = s[:, 0:LANES]
    for c in range(1, s.shape[1] // LANES):
        m = jnp.maximum(m, s[:, c * LANES:(c + 1) * LANES])
    return m


def _mla_kernel(bnd_ref, q_ref, k_ref, v_ref, o_ref, acc_ref, m_ref, *, tq, td, bounded):
    qi = pl.program_id(2)
    q = q_ref[0]
    qs = (q[:, 0:LANES], q[:, LANES:2 * LANES])
    lane = lax.broadcasted_iota(jnp.int32, (1, LANES), 1)
    lo_half = lane < HEAD_DIM
    acc_ref[...] = jnp.zeros_like(acc_ref)
    if bounded:
        m_ref[...] = jnp.full(m_ref.shape, bnd_ref[0], F32)
    else:
        m_ref[...] = jnp.full_like(m_ref, NEG_MASK)

    def scores(off, width, r0, masked):
        k = k_ref[0, pl.ds(off, width), :]
        if masked:
            rows = qi * tq + r0 + lax.broadcasted_iota(jnp.int32, (tq - r0, 1), 0)
            keep = (off + lax.broadcasted_iota(jnp.int32, (1, width), 1)) <= rows
        out = []
        for i in range(2):
            s = _dot_nt(qs[i][r0:tq], k[:, i * LANES:(i + 1) * LANES])
            if masked:
                s = jnp.where(keep, s, NEG_MASK)
            out.append(s)
        return out

    def sweep(fn):
        def body(j, carry):
            fn(pl.multiple_of(j * tq, tq), tq, 0, False)
            return carry
        lax.fori_loop(0, qi, body, 0)
        for c in range(tq // td):
            fn(pl.multiple_of(qi * tq + c * td, td), td, c * td, True)

    def row_max(off, width, r0, masked):
        for i, s in enumerate(scores(off, width, r0, masked)):
            m_ref[i, r0:tq] = jnp.maximum(m_ref[i, r0:tq], _lane_max(s))

    def accumulate(off, width, r0, masked):
        vx = jnp.concatenate([v_ref[0, pl.ds(off, width), :], jnp.ones((width, LANES), BF16)], axis=1)
        for i, s in enumerate(scores(off, width, r0, masked)):
            p = jnp.exp2(s - jnp.concatenate([m_ref[i, r0:tq]] * (width // LANES), axis=1))
            acc_ref[i, r0:tq] += _dot(p.astype(BF16), vx)

    if not bounded:
        sweep(row_max)
        for i in range(2):
            m_ref[i] = jnp.broadcast_to(jnp.max(m_ref[i], axis=-1, keepdims=True), (tq, LANES))
    sweep(accumulate)
    o_ref[0] = jnp.where(lo_half, acc_ref[0, :, 0:LANES] / acc_ref[0, :, LANES:2 * LANES],
                         acc_ref[1, :, 0:LANES] / acc_ref[1, :, LANES:2 * LANES])


def _mla_attention(q, k, v, bound):
    b, s, _ = q.shape
    tq = min(TQ_MLA, s)
    td = min(TD_MLA, tq)

    def call(bounded):
        kern = functools.partial(_mla_kernel, tq=tq, td=td, bounded=bounded)
        return pl.pallas_call(
            kern,
            grid=(b, MLA_HEADS // 2, s // tq),
            in_specs=[pl.BlockSpec(memory_space=pltpu.SMEM),
                      pl.BlockSpec((1, tq, 2 * LANES), lambda i, p, j: (i, j, p)),
                      pl.BlockSpec((1, s, 2 * LANES), lambda i, p, j: (i, 0, p)),
                      pl.BlockSpec((1, s, LANES), lambda i, p, j: (i, 0, p))],
            out_specs=pl.BlockSpec((1, tq, LANES), lambda i, p, j: (i, j, p)),
            out_shape=jax.ShapeDtypeStruct((b, s, MLA_OUT), F32),
            scratch_shapes=[pltpu.VMEM((2, tq, 2 * LANES), F32), pltpu.VMEM((2, tq, LANES), F32)],
            compiler_params=_cparams(3),
            name="mla_attn_bounded" if bounded else "mla_attn",
        )(bound, q, k, v)

    return lax.cond(bound[0] <= SAFE_LOGIT_BOUND, lambda: call(True), lambda: call(False))


def _out_even_kernel(x_ref, mod_ref, osb_ref, omla_ref, gz_ref, w_ref, o_ref):
    gz = gz_ref[0].astype(F32)
    m1 = (osb_ref[0] * gz[:, 0:SB_W]).astype(BF16)
    m2 = (omla_ref[0] * gz[:, SB_W:SB_W + MLA_OUT]).astype(BF16)
    y = _dot(m1, w_ref[0:SB_W, :]) + _dot(m2, w_ref[SB_W:SB_W + MLA_OUT, :])
    gate = mod_ref[0][:, 2 * D_MODEL:3 * D_MODEL]
    o_ref[0] = x_ref[0] + gate * y


def _out_even(x, mod3, osb, omla, gz, w):
    b, s, d = x.shape
    ts = min(TS_PROJ, s)
    row = lambda n: pl.BlockSpec((1, ts, n), lambda i, j: (i, j, 0))
    return pl.pallas_call(
        _out_even_kernel,
        grid=(b, s // ts),
        in_specs=[row(d), pl.BlockSpec((1, 1, 3 * d), lambda i, j: (i, 0, 0)),
                  row(SB_W), row(MLA_OUT), row(SB_W + MLA_OUT), _full(w.shape)],
        out_specs=row(d),
        out_shape=jax.ShapeDtypeStruct((b, s, d), F32),
        compiler_params=_cparams(2),
        name="outproj_even",
    )(x, mod3, osb, omla, gz, w)


NSA_COLS = 5248


def _nsa_projection(hb, vec_ref, w_ref, g64_ref,
                    q_o, ck_o, cv_o, sk_o, sv_o, wk_o, wv_o, gt_o, gz_o, chunk_ref):
    ts = hb.shape[0]
    g64 = g64_ref[...]
    inv64 = 1.0 / HEAD_DIM

    uq = _dot(hb, w_ref[:, 0:1536])
    for c in range(6):
        sl = slice(c * 256, (c + 1) * 256)
        qn = _group_norm_chunk(uq[:, c * 256:(c + 1) * 256], g64, inv64)
        q_o[0, :, sl] = (qn * vec_ref[1:2, sl]).astype(BF16)

    uc = _dot(hb, w_ref[:, 1536:2304])
    for c in range(2 * NSA_KV_HEADS):
        chunk_ref[c] = uc[:, c * 128:(c + 1) * 128]
    for g in range(NSA_KV_HEADS):
        for tok in range(CMP_STRIDE):
            rows = pl.ds(tok, ts // CMP_STRIDE, stride=CMP_STRIDE)
            sl = slice(tok * 128, (tok + 1) * 128)
            ck_o[0, g, :, sl] = chunk_ref[g, rows, :].astype(BF16)
            cv_o[0, g, :, sl] = chunk_ref[NSA_KV_HEADS + g, rows, :].astype(BF16)

    us = _dot(hb, w_ref[:, 2304:3840])
    kgain = vec_ref[2:3, 0:384]
    g128 = g64[0:128, 0:128]
    for g in range(NSA_KV_HEADS):
        sl = slice(g * 128, (g + 1) * 128)
        t = us[:, g * 128:(g + 1) * 128]
        ss = _group_sumsq(t, g128)
        sk_o[0, :, sl] = (t * lax.rsqrt(ss * inv64 + NORM_EPS) * kgain[:, sl]).astype(BF16)
        t = us[:, 768 + g * 128:768 + (g + 1) * 128]
        ss = _group_sumsq(t, g128)
        wk_o[0, :, sl] = (t * lax.rsqrt(ss * inv64 + NORM_EPS) * kgain[:, sl]).astype(BF16)
    ones_hi = (lax.broadcasted_iota(jnp.int32, (1, 384), 1) % 128) >= HEAD_DIM
    sv_o[0] = jnp.where(ones_hi, 1.0, us[:, 384:768]).astype(BF16)
    wv_o[0] = jnp.where(ones_hi, 1.0, us[:, 1152:1536]).astype(BF16)

    ug = _dot(hb, w_ref[:, 3840:4224])
    gt_o[0] = _sigmoid(ug)
    uz = _dot(hb, w_ref[:, 4224:5248])
    gz_o[0] = (uz * _sigmoid(uz)).astype(BF16)


def _k1_odd_kernel(x_ref, mod_ref, vec_ref, w_ref, wd_ref, g64_ref, *refs):
    n_nsa = 9
    nsa_outs, dil_outs = refs[:n_nsa], refs[n_nsa:n_nsa + 3 * N_DIL]
    chunk_ref, st = refs[n_nsa + 3 * N_DIL], refs[n_nsa + 3 * N_DIL + 1]
    hb = _modulated(x_ref, mod_ref, vec_ref[0:1, 0:D_MODEL])
    _nsa_projection(hb, vec_ref, w_ref, g64_ref, *nsa_outs, chunk_ref)
    _dil_projection(hb, vec_ref, wd_ref, g64_ref, dil_outs, st)


def _resident(shape):
    n = len(shape)
    return pl.BlockSpec(shape, lambda *a, _n=n: (0,) * _n, pipeline_mode=pl.Buffered(1))


def _k1_odd(x, mod3, vec, w, wd, g64):
    b, s, d = x.shape
    ts = min(TS_PROJ, s)
    row = lambda n: pl.BlockSpec((1, ts, n), lambda i, j: (i, j, 0))
    cw = CMP_STRIDE * 128
    grp = pl.BlockSpec((1, NSA_KV_HEADS, ts // CMP_STRIDE, cw), lambda i, j: (i, 0, j, 0))
    dspecs, dshapes = [], []
    for g in range(N_DIL):
        dil = DIL_CFG[g][1]
        for _ in range(3):
            dspecs.append(pl.BlockSpec((1, ts // dil, dil * 256), lambda i, j: (i, j, 0)))
            dshapes.append(jax.ShapeDtypeStruct((b, s // dil, dil * 256), BF16))
    outs = pl.pallas_call(
        _k1_odd_kernel,
        grid=(b, s // ts),
        in_specs=[row(d), pl.BlockSpec((1, 1, 3 * d), lambda i, j: (i, 0, 0)),
                  _resident(vec.shape), _resident(w.shape), _resident(wd.shape), _resident(g64.shape)],
        out_specs=[row(1536), grp, grp, row(384), row(384), row(384), row(384), row(384), row(1024)] + dspecs,
        out_shape=[jax.ShapeDtypeStruct((b, s, 1536), BF16),
                   jax.ShapeDtypeStruct((b, NSA_KV_HEADS, s // CMP_STRIDE, cw), BF16),
                   jax.ShapeDtypeStruct((b, NSA_KV_HEADS, s // CMP_STRIDE, cw), BF16),
                   jax.ShapeDtypeStruct((b, s, 384), BF16),
                   jax.ShapeDtypeStruct((b, s, 384), BF16),
                   jax.ShapeDtypeStruct((b, s, 384), BF16),
                   jax.ShapeDtypeStruct((b, s, 384), BF16),
                   jax.ShapeDtypeStruct((b, s, 384), F32),
                   jax.ShapeDtypeStruct((b, s, 1024), BF16)] + dshapes,
        scratch_shapes=[pltpu.VMEM((2 * NSA_KV_HEADS, ts, 128), F32), pltpu.VMEM((3, 2, ts, 128), F32)],
        compiler_params=_cparams(2),
        name="inproj_odd",
    )(x, mod3, vec, w, wd, g64)
    return outs[:9], outs[9:]


def _dil_projection(hb, vec_ref, w_ref, g64_ref, outs, st):
    ts = hb.shape[0]
    g64 = g64_ref[...]
    inv64 = 1.0 / HEAD_DIM
    u = _dot(hb, w_ref[...])
    for g in range(N_DIL):
        dil = DIL_CFG[g][1]
        qn = _group_norm_chunk(u[:, g * 256:(g + 1) * 256], g64, inv64) * vec_ref[3:4, 0:256]
        kn = _group_norm_chunk(u[:, 768 + g * 256:768 + (g + 1) * 256], g64, inv64) * vec_ref[4:5, 0:256]
        vals = (qn, kn, u[:, 1536 + g * 256:1536 + (g + 1) * 256])
        for j, val in enumerate(vals):
            o_ref = outs[3 * g + j]
            if dil == 1:
                o_ref[0] = val.astype(BF16)
                continue
            for h in range(2):
                st[j, h] = val[:, h * 128:(h + 1) * 128]
            for r in range(dil):
                for h in range(2):
                    sl = slice(r * 256 + h * 128, r * 256 + (h + 1) * 128)
                    o_ref[0, :, sl] = st[j, h, pl.ds(r, ts // dil, stride=dil), :].astype(BF16)


def _compress_kernel(xk_ref, xv_ref, wk_ref, wv_ref, pek_ref, pev_ref, gain_ref, g128_ref, kc_o, vc_o):
    nch = xk_ref.shape[2]

    def comp(x_ref, w_ref, pe_ref):
        x = x_ref[0, 0]
        lo = _dot(x, w_ref[0])
        hi = _dot(x, w_ref[1])
        pec = _dot_hl(pe_ref[0], w_ref[0]) + _dot_hl(pe_ref[1], w_ref[1])
        return lo + pltpu.roll(hi, nch - 1, 0) + pec[0:1, :]

    kc = comp(xk_ref, wk_ref, pek_ref)
    ss = _dot_hl(kc * kc, g128_ref[...])
    kc_o[0, 0] = (kc * lax.rsqrt(ss * (1.0 / HEAD_DIM) + NORM_EPS) * gain_ref[...]).astype(BF16)
    vc_o[0, 0] = comp(xv_ref, wv_ref, pev_ref).T.astype(BF16)


def _compress(ck, cv, wk, wv, pek, pev, gain, g128):
    b, g, nch, _ = ck.shape
    xk, xv = ck, cv
    blk = pl.BlockSpec((1, 1, nch, CMP_STRIDE * 128), lambda i, j: (i, j, 0, 0))
    oblk = pl.BlockSpec((1, 1, nch, 128), lambda i, j: (i, j, 0, 0))
    tblk = pl.BlockSpec((1, 1, 128, nch), lambda i, j: (i, j, 0, 0))
    return pl.pallas_call(
        _compress_kernel,
        grid=(b, g),
        in_specs=[blk, blk, _full(wk.shape), _full(wv.shape), _full(pek.shape), _full(pev.shape),
                  _full(gain.shape), _full(g128.shape)],
        out_specs=[oblk, tblk],
        out_shape=[jax.ShapeDtypeStruct((b, g, nch, 128), BF16), jax.ShapeDtypeStruct((b, g, 128, nch), BF16)],
        compiler_params=_cparams(2),
        name="nsa_compress",
    )(xk, xv, wk, wv, pek, pev, gain, g128)


def _pair_select(lo_half, a, b):
    return jnp.where(lo_half, a, b)


def _nsa_cmp_kernel(slope_ref, q_ref, kc_ref, vct_ref, ovlt_ref, cposc_ref, prow_ref, gt_ref,
                    oc_o, selb_o, any_o, imp_ref, *, tq):
    g = pl.program_id(1)
    qi = pl.program_id(2)
    q4 = q_ref[0]
    nch = kc_ref.shape[2]
    t = qi * tq + lax.broadcasted_iota(jnp.int32, (1, tq), 1)
    gtt = gt_ref[0].T
    row_lo = lax.broadcasted_iota(jnp.int32, (LANES, 1), 0) < HEAD_DIM
    pref = prow_ref[qi][:, 0:1]

    def attend(nblk):
        kc = kc_ref[0, 0, 0:nblk, :]
        vct = vct_ref[0, 0, :, 0:nblk]
        n_id = lax.broadcasted_iota(jnp.int32, (nblk, 1), 0)
        valid = (n_id * CMP_STRIDE + (CMP_LEN - 1)) <= t
        rel = (cposc_ref[0:nblk, :] - pref) * LOG2E
        psum = jnp.zeros((nblk, tq), F32)
        outs = []
        for hh in range(NSA_HPG):
            s = _dot_nt(kc, q4[:, hh * LANES:(hh + 1) * LANES]) + slope_ref[g * NSA_HPG + hh] * rel
            s = jnp.where(valid, s, -jnp.inf)
            mx = jnp.max(s, axis=0, keepdims=True)
            mx = jnp.where(mx == -jnp.inf, 0.0, mx)
            e = jnp.exp2(s - mx)
            den = jnp.maximum(jnp.sum(e, axis=0, keepdims=True), TINY)
            p = e * (1.0 / den)
            psum = psum + p
            outs.append(_dot(vct, p.astype(BF16)) * gtt[3 * hh:3 * hh + 1, :])
        oc_o[0, :, 0:LANES] = jnp.where(row_lo, outs[0], outs[1]).T
        oc_o[0, :, LANES:2 * LANES] = jnp.where(row_lo, outs[2], outs[3]).T
        hi, lo = _split_hl(psum)
        imp_ref[...] = _dot(ovlt_ref[:, 0:nblk], hi) + _dot(ovlt_ref[:, 0:nblk], lo)

    n_valid = ((qi + 1) * tq - CMP_LEN) // CMP_STRIDE + 1
    n_groups = nch // LANES
    need = jnp.clip((n_valid + LANES - 1) // LANES, 1, n_groups)
    for c in range(1, n_groups + 1):
        @pl.when(need == c)
        def _(c=c):
            attend(c * LANES)

    assert NSA_HPG < FORCE_BONUS
    cur = t >> 6

    def select(nrow):
        blk = lax.broadcasted_iota(jnp.int32, (nrow, 1), 0)
        forced = (blk == 0) | (blk == cur) | (blk == cur - 1)
        allowed = blk <= cur
        score = jnp.where(allowed & ~forced, imp_ref[0:nrow, :], -jnp.inf)
        blk_f = blk.astype(F32)

        def pick(_, sc):
            mx = jnp.max(sc, axis=0, keepdims=True)
            idx = jnp.min(jnp.where(sc == mx, blk_f, float(LANES)), axis=0, keepdims=True)
            return jnp.where(blk_f == idx, -jnp.inf, sc)

        left = lax.fori_loop(0, SEL_TOPN - 3, pick, score)
        imp_ref[0:nrow, :] = jnp.where((left == -jnp.inf) & allowed, 1.0, 0.0)
        if nrow < LANES:
            imp_ref[nrow:LANES, :] = jnp.zeros((LANES - nrow, tq), F32)

    n_allowed = ((qi + 1) * tq) // SEL_LEN
    n_quarters = LANES // SEL_ROWS
    need_sel = jnp.clip((n_allowed + SEL_ROWS - 1) // SEL_ROWS, 1, n_quarters)
    for c in range(1, n_quarters + 1):
        @pl.when(need_sel == c)
        def _(c=c):
            select(c * SEL_ROWS)

    chosen = imp_ref[...].T
    selb_o[0, 0] = ((chosen - 1.0) * SEL_OFF).astype(BF16)
    used = jnp.max(chosen, axis=0, keepdims=True)
    any_o[0, 0, 0] = jnp.broadcast_to(used, (8, LANES))


def _nsa_cmp(slopes, q, kc, vct, ovlt, cposc, pos_rows, gates):
    b, s, _ = q.shape
    nch = kc.shape[2]
    tq = pos_rows.shape[2]
    kern = functools.partial(_nsa_cmp_kernel, tq=tq)
    return pl.pallas_call(
        kern,
        grid=(b, NSA_KV_HEADS, s // tq),
        in_specs=[pl.BlockSpec(memory_space=pltpu.SMEM),
                  pl.BlockSpec((1, tq, 4 * LANES), lambda i, g, j: (i, j, g)),
                  pl.BlockSpec((1, 1, nch, LANES), lambda i, g, j: (i, g, 0, 0)),
                  pl.BlockSpec((1, 1, LANES, nch), lambda i, g, j: (i, g, 0, 0)),
                  _full(ovlt.shape), _full(cposc.shape), _full(pos_rows.shape),
                  pl.BlockSpec((1, tq, LANES), lambda i, g, j: (i, j, g))],
        out_specs=[pl.BlockSpec((1, tq, 2 * LANES), lambda i, g, j: (i, j, g)),
                   pl.BlockSpec((1, 1, tq, LANES), lambda i, g, j: (i, g, j, 0)),
                   pl.BlockSpec((1, 1, 1, 8, LANES), lambda i, g, j: (i, g, j, 0, 0))],
        out_shape=[jax.ShapeDtypeStruct((b, s, NSA_W), F32),
                   jax.ShapeDtypeStruct((b, NSA_KV_HEADS, s, LANES), BF16),
                   jax.ShapeDtypeStruct((b, NSA_KV_HEADS, s // tq, 8, LANES), F32)],
        scratch_shapes=[pltpu.VMEM((LANES, tq), F32)],
        compiler_params=_cparams(3),
        name="nsa_cmp_topk",
    )(slopes, q, kc, vct, ovlt, cposc, pos_rows, gates)


def _gqa_kernel(flag_ref, slope_ref, bnd_ref, q_ref, selb_ref, k_ref, v_ref, pos_ref, pcol_ref, gt_ref, o_ref,
                qa_ref, acc_ref, m_ref, *, tq, tk, branch, bounded):
    bi = pl.program_id(0)
    g = pl.program_id(1)
    qi = pl.program_id(2)
    q4 = q_ref[0]
    lane = lax.broadcasted_iota(jnp.int32, (1, LANES), 1)
    lo_half = lane < HEAD_DIM
    pref = pos_ref[qi * (tq // tk)][:, 0:1]
    for hh in range(NSA_HPG):
        sl = slice(hh * tq, (hh + 1) * tq)
        qa_ref[sl, 0:LANES] = q4[:, hh * LANES:(hh + 1) * LANES]
        if branch == 1:
            qa_ref[sl, LANES:2 * LANES] = selb_ref[0, 0]
        if bounded:
            own = (pcol_ref[...] - pref) * (LOG2E * slope_ref[g * NSA_HPG + hh]) + bnd_ref[0]
            m_ref[sl] = jnp.broadcast_to(own, (tq, LANES))
    acc_ref[...] = jnp.zeros_like(acc_ref)
    if not bounded:
        m_ref[...] = jnp.full_like(m_ref, NEG_MASK)
    rows = qi * tq + lax.broadcasted_iota(jnp.int32, (tq, 1), 0)
    col0 = lax.broadcasted_iota(jnp.int32, (1, tk), 1)
    krow = lax.broadcasted_iota(jnp.int32, (tk, 1), 0)
    per = tq // tk
    nrep = tk // LANES
    flag0 = ((bi * NSA_KV_HEADS + g) * pl.num_programs(2) + qi) * SEL_TILE_STRIDE

    def scores(kt_true, masked, r_lo, r_hi):
        kt = jnp.maximum(kt_true, 0)
        off = pl.multiple_of(kt * tk, tk)
        k = k_ref[0, pl.ds(off, tk), :]
        if branch == 1:
            blk = (kt * tk + krow) >> 6
            onehot = jnp.where(lane == blk, 1.0, 0.0).astype(BF16)
            s_all = _dot_nt(qa_ref[...], jnp.concatenate([k, onehot], axis=1))
            parts = [s_all[hh * tq:(hh + 1) * tq] for hh in range(NSA_HPG)]
        else:
            parts = [_dot_nt(qa_ref[hh * tq + r_lo:hh * tq + r_hi, 0:LANES], k) for hh in range(NSA_HPG)]
        rel = (pos_ref[kt] - pref) * LOG2E
        if masked:
            cols = kt_true * tk + col0
            d = rows[r_lo:r_hi] - cols
            keep = d >= 0
            if branch == 2:
                keep = keep & (d < WIN) & (cols >= 0)
        out = []
        for hh in range(NSA_HPG):
            s = parts[hh] + slope_ref[g * NSA_HPG + hh] * rel
            if masked:
                s = jnp.where(keep, s, NEG_MASK)
            out.append(s)
        return out

    def sweep(fn):
        if branch == 1:
            def body(j, carry):
                @pl.when(flag_ref[flag0 + j] != 0)
                def _():
                    fn(j, False, 0, tq)
                return carry
            lax.fori_loop(0, qi * per, body, 0)
            for dd in range(per):
                fn(qi * per + dd, True, 0, tq)
        else:
            for c in range(-(WIN // tk), per):
                r_lo = max(0, c * tk)
                r_hi = min(tq, -(-(c * tk + tk + WIN - 1) // 8) * 8)
                fn(qi * per + c, True, r_lo, r_hi)

    def row_max(kt_true, masked, r_lo, r_hi):
        for hh, s in enumerate(scores(kt_true, masked, r_lo, r_hi)):
            sl = slice(hh * tq + r_lo, hh * tq + r_hi)
            m_ref[sl] = jnp.maximum(m_ref[sl], _lane_max(s))

    def accumulate(kt_true, masked, r_lo, r_hi):
        off = pl.multiple_of(jnp.maximum(kt_true, 0) * tk, tk)
        v = v_ref[0, pl.ds(off, tk), :]
        for hh, s in enumerate(scores(kt_true, masked, r_lo, r_hi)):
            sl = slice(hh * tq + r_lo, hh * tq + r_hi)
            p = jnp.exp2(s - jnp.concatenate([m_ref[sl]] * nrep, axis=1))
            acc_ref[sl] += _dot(p.astype(BF16), v)

    if not bounded:
        sweep(row_max)
        for hh in range(NSA_HPG):
            sl = slice(hh * tq, (hh + 1) * tq)
            m_ref[sl] = jnp.broadcast_to(jnp.max(m_ref[sl], axis=-1, keepdims=True), (tq, LANES))
    sweep(accumulate)

    gt = gt_ref[0]
    for pr in range(2):
        res = []
        for x in range(2):
            hh = 2 * pr + x
            a = acc_ref[hh * tq:(hh + 1) * tq]
            r = pltpu.roll(a, HEAD_DIM, 1)
            o = a / r if x == 0 else r / a
            res.append(o * gt[:, 3 * hh + branch:3 * hh + branch + 1])
        o_ref[0, :, pr * LANES:(pr + 1) * LANES] = _pair_select(lo_half, res[0], res[1])


def _nsa_gqa(flags, slopes, bound, safe, q, selb, k, v, pos2d, pos_col, gates, branch):
    b, s, _ = q.shape
    tq = min(TQ_NSA, s)
    tk = pos2d.shape[2]

    def call(bounded):
        kern = functools.partial(_gqa_kernel, tq=tq, tk=tk, branch=branch, bounded=bounded)
        smem = pl.BlockSpec(memory_space=pltpu.SMEM)
        name = ("nsa_sel_attn" if branch == 1 else "nsa_win_attn") + ("_bounded" if bounded else "")
        return pl.pallas_call(
            kern,
            grid=(b, NSA_KV_HEADS, s // tq),
            in_specs=[smem, smem, smem,
                      pl.BlockSpec((1, tq, 4 * LANES), lambda i, g, j: (i, j, g)),
                      pl.BlockSpec((1, 1, tq, LANES), lambda i, g, j: (i, g, j, 0)),
                      pl.BlockSpec((1, s, LANES), lambda i, g, j: (i, 0, g)),
                      pl.BlockSpec((1, s, LANES), lambda i, g, j: (i, 0, g)),
                      _full(pos2d.shape),
                      pl.BlockSpec((tq, 1), lambda i, g, j: (j, 0)),
                      pl.BlockSpec((1, tq, LANES), lambda i, g, j: (i, j, g))],
            out_specs=pl.BlockSpec((1, tq, 2 * LANES), lambda i, g, j: (i, j, g)),
            out_shape=jax.ShapeDtypeStruct((b, s, NSA_W), F32),
            scratch_shapes=[pltpu.VMEM((NSA_HPG * tq, 2 * LANES), BF16),
                            pltpu.VMEM((NSA_HPG * tq, LANES), F32),
                            pltpu.VMEM((NSA_HPG * tq, LANES), F32)],
            compiler_params=_cparams(3),
            name=name,
        )(flags, slopes, bound, q, selb, k, v, pos2d, pos_col, gates)

    return lax.cond(safe, lambda: call(True), lambda: call(False))


def _dil_kernel(slope_ref, bnd_ref, q_ref, k_ref, v_ref, pc_ref, pr_ref, o_ref, lse_ref,
                *, t, span, grp, bounded):
    i = pl.program_id(2)
    q4 = q_ref[0]
    lane = lax.broadcasted_iota(jnp.int32, (1, LANES), 1)
    lo_half = lane < HEAD_DIM
    per = t // LANES
    prev = jnp.maximum(i * per - 1, 0)
    o_prev = pl.multiple_of(prev * LANES, LANES)
    o_cur = pl.multiple_of(i * t, t)
    kk = jnp.concatenate([k_ref[0, pl.ds(o_prev, LANES), :], k_ref[0, pl.ds(o_cur, t), :]], axis=0)
    vv = jnp.concatenate([v_ref[0, pl.ds(o_prev, LANES), :], v_ref[0, pl.ds(o_cur, t), :]], axis=0)
    pk = jnp.concatenate([pr_ref[0, prev]] + [pr_ref[0, i * per + c] for c in range(per)], axis=1)
    dist = (pc_ref[0] - pk) * LOG2E
    rows = i * t + lax.broadcasted_iota(jnp.int32, (t, 1), 0)
    cols = i * t - LANES + lax.broadcasted_iota(jnp.int32, (1, t + LANES), 1)
    d = rows - cols
    valid = (cols >= 0) & (d >= 0) & (d <= span)
    zero = jnp.zeros((t, LANES), BF16)
    ones = jnp.ones((t + LANES, LANES), BF16)
    for p in range(2):
        qp = q4[:, p * LANES:(p + 1) * LANES]
        kp = kk[:, p * LANES:(p + 1) * LANES]
        vx = jnp.concatenate([vv[:, p * LANES:(p + 1) * LANES], ones], axis=1)
        res = []
        for x in range(2):
            qx = jnp.where(lo_half, qp, zero) if x == 0 else jnp.where(lo_half, zero, qp)
            s = _dot_nt(qx, kp) - slope_ref[grp * DIL_HEADS + 2 * p + x] * dist
            s = jnp.where(valid, s, -jnp.inf)
            mx = bnd_ref[0] if bounded else jnp.max(s, axis=-1, keepdims=True)
            acc = _dot(jnp.exp2(s - mx).astype(BF16), vx)
            den = acc[:, LANES:2 * LANES]
            res.append((acc[:, 0:LANES] / den, mx * LN2 + jnp.log(den)))
        sl = slice(p * LANES, (p + 1) * LANES)
        o_ref[0, :, sl] = _pair_select(lo_half, res[0][0], res[1][0])
        lse_ref[0, :, sl] = jnp.where(lo_half, res[0][1], res[1][1])


def _dilated(slopes, bound, safe, qv, kv, vv, pos_f, grp):
    window, dil = DIL_CFG[grp]
    b, sub, wd = qv.shape
    w = wd // dil
    t = min(T_DIL, sub)
    span = window // dil
    assert span <= LANES and t % LANES == 0
    pres = pos_f.reshape(sub, dil).T
    pc = pres.reshape(dil, sub, 1)
    pr = pres.reshape(dil, sub // LANES, 1, LANES)
    shp = jax.ShapeDtypeStruct((b, sub, wd), F32)
    smem = pl.BlockSpec(memory_space=pltpu.SMEM)

    def call(bounded):
        kern = functools.partial(_dil_kernel, t=t, span=span, grp=grp, bounded=bounded)
        return pl.pallas_call(
            kern,
            grid=(b, dil, sub // t),
            in_specs=[smem, smem,
                      pl.BlockSpec((1, t, w), lambda i, r, j: (i, j, r)),
                      pl.BlockSpec((1, sub, w), lambda i, r, j: (i, 0, r)),
                      pl.BlockSpec((1, sub, w), lambda i, r, j: (i, 0, r)),
                      pl.BlockSpec((1, t, 1), lambda i, r, j: (r, j, 0)),
                      pl.BlockSpec((1, sub // LANES, 1, LANES), lambda i, r, j: (r, 0, 0, 0))],
            out_specs=[pl.BlockSpec((1, t, w), lambda i, r, j: (i, j, r)),
                       pl.BlockSpec((1, t, w), lambda i, r, j: (i, j, r))],
            out_shape=[shp, shp],
            compiler_params=_cparams(3),
            name="dilated_attn_g%d%s" % (grp, "_bounded" if bounded else ""),
        )(slopes, bound, qv, kv, vv, pc, pr)

    return lax.cond(safe, lambda: call(True), lambda: call(False))


def _out_odd_kernel(x_ref, mod_ref, oc_ref, os_ref, ow_ref, d0_ref, d1_ref, d2_ref,
                    l0_ref, l1_ref, l2_ref, gz_ref, w_ref, o_ref, st):
    ts = x_ref.shape[1]
    gz = gz_ref[0].astype(F32)
    nsa = oc_ref[0] + os_ref[0] + ow_ref[0]
    m1 = (nsa * gz[:, 0:NSA_W]).astype(BF16)

    def token_order(ref, g, slot):
        dil = DIL_CFG[g][1]
        if dil == 1:
            return ref[0]
        for r in range(dil):
            for h in range(2):
                st[slot, h, pl.ds(r, ts // dil, stride=dil), :] = ref[0, :, r * DIL_W + h * 128:r * DIL_W + (h + 1) * 128]
        return jnp.concatenate([st[slot, 0], st[slot, 1]], axis=1)

    d0, d1, d2 = token_order(d0_ref, 0, 0), token_order(d1_ref, 1, 0), token_order(d2_ref, 2, 1)
    l0, l1, l2 = token_order(l0_ref, 0, 2), token_order(l1_ref, 1, 2), token_order(l2_ref, 2, 3)
    mx = jnp.maximum(jnp.maximum(l0, l1), l2)
    e0, e1, e2 = jnp.exp(l0 - mx), jnp.exp(l1 - mx), jnp.exp(l2 - mx)
    dil = (e0 * d0 + e1 * d1 + e2 * d2) / (e0 + e1 + e2)
    m2 = (dil * gz[:, NSA_W:NSA_W + DIL_W]).astype(BF16)
    y = _dot(m1, w_ref[0:NSA_W, :]) + _dot(m2, w_ref[NSA_W:NSA_W + DIL_W, :])
    gate = mod_ref[0][:, 2 * D_MODEL:3 * D_MODEL]
    o_ref[0] = x_ref[0] + gate * y


def _out_odd(x, mod3, oc, os_, ow, dils, lses, gz, w):
    b, s, d = x.shape
    ts = min(TS_PROJ, s)
    row = lambda n: pl.BlockSpec((1, ts, n), lambda i, j: (i, j, 0))
    res = [pl.BlockSpec((1, ts // DIL_CFG[g][1], DIL_CFG[g][1] * DIL_W), lambda i, j: (i, j, 0))
           for g in range(N_DIL)]
    return pl.pallas_call(
        _out_odd_kernel,
        grid=(b, s // ts),
        in_specs=[row(d), pl.BlockSpec((1, 1, 3 * d), lambda i, j: (i, 0, 0)),
                  row(NSA_W), row(NSA_W), row(NSA_W)] + res + res + [row(1024), _full(w.shape)],
        out_specs=row(d),
        out_shape=jax.ShapeDtypeStruct((b, s, d), F32),
        scratch_shapes=[pltpu.VMEM((4, 2, ts, 128), F32)],
        compiler_params=_cparams(2),
        name="outproj_odd",
    )(x, mod3, oc, os_, ow, *dils, *lses, gz, w)


def _pad_vec(v, n=D_MODEL):
    return jnp.pad(v, (0, n - v.shape[0]))


def _group_matrix(sizes, total):
    m = np.zeros((total, total), np.float32)
    off = 0
    for sz, on in sizes:
        if on:
            m[off:off + sz, off:off + sz] = 1.0
        off += sz
    return jnp.asarray(m, BF16)


def _swap_halves(w):
    h = w.shape[-1] // 2
    return jnp.concatenate([w[..., h:], w[..., :h]], axis=-1)


def _column_blocks(segments):
    blocks, cur, room = [], [], LANES
    for src, width in segments:
        while width:
            n = min(width, room)
            cur.append((src, n))
            src = None if src is None else src + n
            width -= n
            room -= n
            if room == 0:
                blocks.append(cur)
                cur, room = [], LANES
    assert not cur
    return blocks


def _pack_cols_kernel(w_ref, o_ref, *, blocks):
    _, rows, n_src = w_ref.shape
    for bi, pieces in enumerate(blocks):
        parts = []
        for src, width in pieces:
            if src is None:
                parts.append(jnp.zeros((rows, width), F32))
                continue
            a0 = src // LANES * LANES
            a1 = min(-(-(src + width) // LANES) * LANES, n_src)
            parts.append(w_ref[0, :, a0:a1][:, src - a0:src - a0 + width])
        blk = parts[0] if len(parts) == 1 else jnp.concatenate(parts, axis=1)
        o_ref[:, bi * LANES:(bi + 1) * LANES] = blk.astype(BF16)


def _pack_cols(w, layer, segments):
    blocks = _column_blocks(segments)
    _, d, n_src = w.shape
    tr = 256
    return pl.pallas_call(
        functools.partial(_pack_cols_kernel, blocks=blocks),
        grid=(d // tr,),
        in_specs=[pl.BlockSpec((1, tr, n_src), lambda i: (layer, i, 0))],
        out_specs=pl.BlockSpec((tr, len(blocks) * LANES), lambda i: (i, 0)),
        out_shape=jax.ShapeDtypeStruct((d, len(blocks) * LANES), BF16),
        compiler_params=_cparams(1),
        name="pack_weight_columns",
    )(w)


EVEN_SEGMENTS = ((0, 1536), (1536, 512), (2464, 512), (2048, 256), (2304, 128),
                 (None, 64), (2432, 32), (None, 32), (None, 64), (2448, 16), (2432, 16), (None, 32))


def _pack_even(w_in_all, layer, norm_g, sb_qn, sb_kn, qa_g, wq_up, kva_g, wkv_up, qn, kn):
    w = _pack_cols(w_in_all, layer, EVEN_SEGMENTS)
    wq3 = wq_up.reshape(MLA_Q_RANK, MLA_HEADS, MLA_NOPE + MLA_ROPE)
    zq = jnp.zeros((MLA_Q_RANK, MLA_HEADS, 32), wq_up.dtype)
    wq = jnp.concatenate([wq3, zq], axis=-1).reshape(MLA_Q_RANK, MLA_HEADS * 128).astype(BF16)
    wqs = jnp.concatenate([jnp.zeros((MLA_Q_RANK, MLA_HEADS, 64), wq_up.dtype),
                           _swap_halves(wq3[..., MLA_NOPE:]), zq], axis=-1)
    wqs = wqs.reshape(MLA_Q_RANK, MLA_HEADS * 128).astype(BF16)
    wkv3 = wkv_up.reshape(MLA_KV_RANK, MLA_HEADS, MLA_NOPE + MLA_V)
    wk = jnp.concatenate([wkv3[..., :MLA_NOPE], jnp.zeros_like(wkv3[..., :MLA_NOPE])], axis=-1)
    wkv = jnp.concatenate([wk.reshape(MLA_KV_RANK, MLA_HEADS * 128),
                           wkv3[..., MLA_NOPE:].reshape(MLA_KV_RANK, MLA_HEADS * MLA_V)], axis=1).astype(BF16)
    z32 = jnp.zeros((32,), F32)
    z64 = jnp.zeros((64,), F32)
    scale = (MLA_NOPE + MLA_ROPE) ** -0.5 * LOG2E
    qg = jnp.tile(jnp.concatenate([qn, z32]), MLA_HEADS) * scale
    qgs = jnp.tile(jnp.concatenate([z64, _swap_halves(qn[MLA_NOPE:]), z32]), MLA_HEADS) * scale
    kg = jnp.tile(jnp.concatenate([kn[:MLA_NOPE], z64]), MLA_HEADS)
    krg = jnp.concatenate([z64, kn[MLA_NOPE:], z32])
    krgs = jnp.concatenate([z64, _swap_halves(kn[MLA_NOPE:]), z32])
    cnt = jnp.tile(jnp.concatenate([jnp.full((64,), 1.0 / 64), jnp.full((32,), 1.0 / 32), jnp.ones((32,))]),
                   MLA_HEADS).astype(F32)
    rows = [norm_g, _pad_vec(jnp.tile(sb_qn, SB_HEADS) * (LOG2E * HEAD_DIM ** -0.5)), _pad_vec(jnp.tile(sb_kn, SB_HEADS)),
            _pad_vec(qa_g), _pad_vec(kva_g), qg, qgs, kg, _pad_vec(krg), _pad_vec(krgs), cnt]
    rows += [jnp.zeros((D_MODEL,), F32)] * (16 - len(rows))
    return w, wq, wqs, wkv, jnp.stack(rows).astype(F32)


def _padded_heads(start, n):
    return tuple(seg for h in range(n) for seg in ((start + h * HEAD_DIM, HEAD_DIM), (None, HEAD_DIM)))


NSA_SEGMENTS = (_padded_heads(0, NSA_HEADS)
                + tuple(seg for c in range(6) for seg in _padded_heads(768 + c * NSA_KV_W, NSA_KV_HEADS))
                + tuple(seg for g in range(NSA_KV_HEADS)
                        for seg in ((1920 + g * NSA_HPG * 3, NSA_HPG * 3), (None, LANES - NSA_HPG * 3)))
                + ((1956, NSA_W), (5028, DIL_W)))
DIL_SEGMENTS = ((2724, 3 * N_DIL * DIL_W),)


def _pack_odd(w_in_all, layer, norm_g, nsa_qn, nsa_kn, dil_qn, dil_kn):
    w_nsa = _pack_cols(w_in_all, layer, NSA_SEGMENTS)
    w_dil = _pack_cols(w_in_all, layer, DIL_SEGMENTS)
    z64 = jnp.zeros((64,), F32)
    n = NSA_HEADS * 128
    rows = [_pad_vec(norm_g, n),
            jnp.tile(jnp.concatenate([nsa_qn * (LOG2E * HEAD_DIM ** -0.5), z64]), NSA_HEADS),
            _pad_vec(jnp.tile(jnp.concatenate([nsa_kn, z64]), NSA_KV_HEADS), n),
            _pad_vec(jnp.tile(dil_qn, DIL_HEADS) * (LOG2E * HEAD_DIM ** -0.5), n),
            _pad_vec(jnp.tile(dil_kn, DIL_HEADS), n)]
    rows += [jnp.zeros((n,), F32)] * (8 - len(rows))
    return w_nsa, w_dil, jnp.stack(rows).astype(F32)


def _pack_compress(w, pe, double):
    w3 = w.reshape(CMP_LEN, HEAD_DIM, HEAD_DIM)
    w3 = jnp.concatenate([w3, jnp.zeros_like(w3)], axis=1)
    w3 = jnp.concatenate([w3, w3 if double else jnp.zeros_like(w3)], axis=2)
    wp = w3.reshape(2, CMP_STRIDE * 128, 128).astype(BF16)
    pe2 = jnp.concatenate([pe, jnp.zeros_like(pe)], axis=1).reshape(2, 1, CMP_STRIDE * 128)
    pe2 = jnp.broadcast_to(pe2, (2, 8, CMP_STRIDE * 128)).astype(F32)
    return wp, pe2


def _normed_len(gain, sizes):
    tot, off = 0.0, 0
    for n in sizes:
        tot = tot + n * jnp.max(jnp.square(gain[off:off + n]))
        off += n
    return jnp.sqrt(tot)


def _mla_logit_bound(qn, kn):
    sizes = (MLA_NOPE, MLA_ROPE)
    scale = (MLA_NOPE + MLA_ROPE) ** -0.5 * LOG2E
    return (_normed_len(qn, sizes) * _normed_len(kn, sizes) * (scale * ROUNDING_MARGIN)).reshape(1).astype(F32)


def _alibi_slopes(n):
    return 2.0 ** (-8.0 * jnp.arange(1, n + 1, dtype=jnp.float32) / n)


def kernel(x, c, positions, ada_w, ada_b, norm_g, ev_w_in, ev_w_out, sb_qn, sb_kn, mla_qa_g, mla_wq_up,
           mla_kva_g, mla_wkv_up, mla_qn, mla_kn, od_w_in, od_w_out, nsa_qn, nsa_kn, nsa_cmp_wk, nsa_cmp_wv,
           nsa_cmp_pe_k, nsa_cmp_pe_v, dil_qn, dil_kn):
    b, s, d = x.shape
    depth = ada_w.shape[0]
    pos_f = positions.astype(F32)

    inv_freq = ROPE_BASE ** (-jnp.arange(0, MLA_ROPE, 2, dtype=F32) / MLA_ROPE)
    ang = pos_f[:, None] * inv_freq[None, :]
    cos, sin = jnp.cos(ang), jnp.sin(ang)
    cos128 = jnp.concatenate([jnp.ones((s, 64), F32), cos, cos, jnp.zeros((s, 32), F32)], axis=1)
    sin128 = jnp.concatenate([jnp.zeros((s, 64), F32), -sin, sin, jnp.zeros((s, 32), F32)], axis=1)
    nsa_slopes = _alibi_slopes(NSA_HEADS)
    dil_slopes = _alibi_slopes(N_DIL * DIL_HEADS)
    nch = s // CMP_STRIDE
    chunk_sum = pos_f.reshape(nch, CMP_STRIDE).sum(axis=1)
    cpos = ((chunk_sum + jnp.roll(chunk_sum, -1)) / CMP_LEN).reshape(nch, 1)
    n_sel = s // SEL_LEN
    cst = np.arange(nch)[:, None] * CMP_STRIDE
    jst = np.arange(LANES)[None, :] * SEL_LEN
    ovl = ((cst <= jst + SEL_LEN - 1) & (cst + CMP_LEN - 1 >= jst) & (np.arange(LANES)[None, :] < n_sel))
    ovlt = jnp.asarray(ovl.astype(np.float32).T, BF16)
    pos_col = pos_f.reshape(s, 1)
    tqc = min(TQ_CMP, s)
    pos_rows = pos_f.reshape(s // tqc, 1, tqc)
    tqn = min(TQ_NSA, s)
    tkn = min(TK_NSA, tqn)
    assert tkn == TK_NSA
    pos2d = pos_f.reshape(s // tkn, 1, tkn)
    pos2d_win = pos_f.reshape(s // TK_WIN, 1, TK_WIN)
    q_first = pos_f[::tqn]
    k_last = pos_f[jnp.minimum((jnp.arange(SEL_TILE_STRIDE) + 1) * tkn - 1, s - 1)]
    group_slope = jnp.min(nsa_slopes.reshape(NSA_KV_HEADS, NSA_HPG), axis=1) * LOG2E
    sel_dead = group_slope[:, None, None] * (q_first[:, None] - k_last[None, :])[None] > F32_ZERO_EXP
    pos_sorted = jnp.all(pos_f[1:] >= pos_f[:-1])

    g64 = _group_matrix([(64, 1)] * 4, 256)
    gm = _group_matrix([(64, 1), (32, 1), (32, 0)] * 2, 256)
    tks = min(TK_SB, min(TQ_SB, s))
    tri = jnp.asarray(np.tril(np.ones((tks, tks), np.float32)), BF16)

    c8 = jnp.pad(c, ((0, 8 - b), (0, 0)))
    mod_all = _modulation(c8, ada_w, ada_b)

    for layer in range(depth):
        j = layer // 2
        mod3 = mod_all[layer, :b].reshape(b, 1, 3 * d)
        if layer % 2 == 0:
            w, wq, wqs, wkv, vec = _pack_even(ev_w_in, j, norm_g[layer], sb_qn[j], sb_kn[j], mla_qa_g[j],
                                              mla_wq_up[j], mla_kva_g[j], mla_wkv_up[j], mla_qn[j], mla_kn[j])
            sbq, sbk, sbv, gz, mq, mk, mv = _k1_even(x, mod3, vec, w, wq, wqs, wkv, g64, gm, cos128, sin128)
            sb_bound = (_normed_len(sb_qn[j], (HEAD_DIM,)) * _normed_len(sb_kn[j], (HEAD_DIM,))
                        * (HEAD_DIM ** -0.5 * LOG2E * ROUNDING_MARGIN)).reshape(1).astype(F32)
            o_sb = _sb_attention(sbq, sbk, sbv, tri, sb_bound, sb_bound[0] <= SAFE_LOGIT_BOUND)
            o_mla = _mla_attention(mq, mk, mv, _mla_logit_bound(mla_qn[j], mla_kn[j]))
            x = _out_even(x, mod3, o_sb, o_mla, gz, ev_w_out[j].astype(BF16))
        else:
            w_nsa, w_dil, vec = _pack_odd(od_w_in, j, norm_g[layer], nsa_qn[j], nsa_kn[j], dil_qn[j], dil_kn[j])
            (q, ck, cv, sk, sv, wk, wv, gates, gz), dqkv = _k1_odd(x, mod3, vec, w_nsa, w_dil, g64)
            wck, pek = _pack_compress(nsa_cmp_wk[j], nsa_cmp_pe_k[j], False)
            wcv, pev = _pack_compress(nsa_cmp_wv[j], nsa_cmp_pe_v[j], True)
            kgain = jnp.concatenate([nsa_kn[j], jnp.zeros((64,), F32)]).reshape(1, 128)
            kc, vc = _compress(ck, cv, wck, wcv, pek, pev, kgain, g64[0:128, 0:128])
            o_c, selb, used = _nsa_cmp(nsa_slopes, q, kc, vc, ovlt, cpos, pos_rows, gates)
            flags = used[:, :, :, 0, :].reshape(b, NSA_KV_HEADS, s // tqn, tqn // min(TQ_CMP, s),
                                                SEL_TILE_STRIDE, LANES // SEL_TILE_STRIDE).max(axis=(3, 5))
            nsa_bound = (_normed_len(nsa_qn[j], (HEAD_DIM,)) * _normed_len(nsa_kn[j], (HEAD_DIM,))
                         * (HEAD_DIM ** -0.5 * LOG2E * ROUNDING_MARGIN)).reshape(1).astype(F32)
            safe = (nsa_bound[0] <= SAFE_LOGIT_BOUND) & pos_sorted
            flags = (flags > 0) & ~(safe & sel_dead[None])
            flags = flags.astype(jnp.int32).reshape(-1)
            o_s = _nsa_gqa(flags, nsa_slopes, nsa_bound, safe, q, selb, sk, sv, pos2d, pos_col, gates, 1)
            o_w = _nsa_gqa(flags, nsa_slopes, nsa_bound, safe, q, selb, wk, wv, pos2d_win, pos_col, gates, 2)
            dil_bound = (_normed_len(dil_qn[j], (HEAD_DIM,)) * _normed_len(dil_kn[j], (HEAD_DIM,))
                         * (HEAD_DIM ** -0.5 * LOG2E * ROUNDING_MARGIN)).reshape(1).astype(F32)
            dil_safe = (dil_bound[0] <= SAFE_LOGIT_BOUND) & pos_sorted
            dils, lses = [], []
            for g in range(N_DIL):
                o, lse = _dilated(dil_slopes, dil_bound, dil_safe, dqkv[3 * g], dqkv[3 * g + 1], dqkv[3 * g + 2],
                                  pos_f, g)
                dils.append(o)
                lses.append(lse)
            x = _out_odd(x, mod3, o_c, o_s, o_w, dils, lses, gz, od_w_out[j].astype(BF16))
    return x
```
